```python
import jax, jax.numpy as jnp
from jax import lax
import numpy as np

D_MODEL = 1024
BATCH = 8
SEQ = 8192
DEPTH = 2

CHUNK = 64
Q_BLOCK = 128
FOX_HEADS = 8
FOX_HEAD_DIM = 64
FOX_WIDTH = FOX_HEADS * FOX_HEAD_DIM
CONV_WIDTH = 512
CONV_K = 3
EVEN_IN = 3 * FOX_WIDTH + FOX_HEADS + 3 * CONV_WIDTH
EVEN_MIX = FOX_WIDTH + CONV_WIDTH
GMLP_BLOCK = 128
GMLP_GROUPS = 8
GMLP_WIDTH = D_MODEL
GMLP_GROUP_DIM = GMLP_WIDTH // GMLP_GROUPS
FFN_HIDDEN = -(-8 * D_MODEL // (3 * 256)) * 256
ALPHA = (2.0 * DEPTH) ** 0.25
BETA = (8.0 * DEPTH) ** -0.25
N_EVEN = (DEPTH + 1) // 2
N_ODD = DEPTH // 2
LN_EPS = 1e-5

kernel_name = "fox_shortconv_gmlp_deepnorm_trunk"


def layer_norm(x, g, b):
    xf = x.astype(jnp.float32)
    mu = jnp.mean(xf, axis=-1, keepdims=True)
    var = jnp.mean(jnp.square(xf - mu), axis=-1, keepdims=True)
    return ((xf - mu) * lax.rsqrt(var + LN_EPS) * g + b).astype(x.dtype)


def forgetting_attention(q, k, v, log_f):
    bsz, s_len, h, dh = q.shape
    nb = s_len // Q_BLOCK
    c = jnp.cumsum(log_f, axis=1).transpose(0, 2, 1)
    kh = k.transpose(0, 2, 1, 3)
    vh = v.transpose(0, 2, 1, 3)
    qb = q.reshape(bsz, nb, Q_BLOCK, h, dh).transpose(1, 0, 3, 2, 4)
    cb = c.reshape(bsz, h, nb, Q_BLOCK).transpose(2, 0, 1, 3)
    pos = jnp.arange(s_len)
    posb = pos.reshape(nb, Q_BLOCK)
    scale = dh ** -0.5

    def block(args):
        q_blk, c_blk, p_blk = args
        s = jnp.einsum('bhqd,bhkd->bhqk', q_blk, kh,
                       preferred_element_type=jnp.float32) * scale
        s = s + c_blk[..., :, None] - c[..., None, :]
        s = jnp.where(p_blk[:, None] >= pos[None, :], s, -jnp.inf)
        p = jax.nn.softmax(s, axis=-1)
        return jnp.einsum('bhqk,bhkd->bhqd', p.astype(vh.dtype), vh)

    o = lax.map(block, (qb, cb, posb))
    return o.transpose(1, 0, 3, 2, 4).reshape(bsz, s_len, h * dh)


def short_conv_mixer(h, b_gate, c_gate, conv_w):
    s_len = h.shape[1]
    z = c_gate * h
    zp = jnp.pad(z, ((0, 0), (CONV_K - 1, 0), (0, 0)))
    y = conv_w[0] * zp[:, 0:s_len]
    for i in range(1, CONV_K):
        y = y + conv_w[i] * zp[:, i:i + s_len]
    return b_gate * y


def fox_conv_mixer(x, w_in, b_f, conv_w, w_out):
    bsz, s_len, _ = x.shape
    proj = x @ w_in
    cuts = np.cumsum([FOX_WIDTH, FOX_WIDTH, FOX_WIDTH, FOX_HEADS, CONV_WIDTH, CONV_WIDTH]).tolist()
    q, k, v, f_logit, b_gate, c_gate, h = jnp.split(proj, cuts, axis=-1)
    log_f = jax.nn.log_sigmoid((f_logit + b_f).astype(jnp.float32))
    heads = (bsz, s_len, FOX_HEADS, FOX_HEAD_DIM)
    attn = forgetting_attention(q.reshape(heads), k.reshape(heads), v.reshape(heads), log_f)
    conv = short_conv_mixer(h, b_gate, c_gate, conv_w)
    return jnp.concatenate([attn.astype(x.dtype), conv], axis=-1) @ w_out


def gmlp_mixer(x, w_in, v_ln_g, v_ln_b, w_s, b_s, w_out):
    bsz, s_len, _ = x.shape
    uv = jax.nn.gelu(x @ w_in, approximate=False)
    u, v = jnp.split(uv, 2, axis=-1)
    v = layer_norm(v, v_ln_g, v_ln_b)
    nc = s_len // GMLP_BLOCK
    vb = v.reshape(bsz, nc, GMLP_BLOCK, GMLP_GROUPS, GMLP_GROUP_DIM)
    chunk_id = jnp.arange(GMLP_BLOCK) // CHUNK
    mask = chunk_id[None, :] <= chunk_id[:, None]
    w = jnp.where(mask[None], w_s, jnp.zeros((), w_s.dtype))
    s = jnp.einsum('gij,bcjgd->bcigd', w, vb) + b_s.T[None, None, :, :, None]
    return (u * s.reshape(bsz, s_len, GMLP_WIDTH)) @ w_out


def swiglu(x, w_in, w_out):
    gate, up = jnp.split(x @ w_in, 2, axis=-1)
    return (jax.nn.silu(gate) * up) @ w_out


def _fwd_setup_inputs(seed: int = 0) -> dict:
    key = jax.random.key(seed)
    ks = jax.random.split(key, 20)
    nrm = jax.random.normal
    f32 = jnp.float32
    return {
        "x": nrm(ks[0], (BATCH, SEQ, D_MODEL), f32),
        "even_w_in": nrm(ks[1], (N_EVEN, D_MODEL, EVEN_IN), f32) * D_MODEL ** -0.5,
        "even_b_f": jax.random.uniform(ks[2], (N_EVEN, FOX_HEADS), f32, 1.0, 5.0),
        "even_conv_w": nrm(ks[3], (N_EVEN, CONV_K, CONV_WIDTH), f32) * CONV_K ** -0.5,
        "even_w_out": nrm(ks[4], (N_EVEN, EVEN_MIX, D_MODEL), f32) * (EVEN_MIX ** -0.5 * BETA),
        "odd_w_in": nrm(ks[5], (N_ODD, D_MODEL, 2 * GMLP_WIDTH), f32) * D_MODEL ** -0.5,
        "odd_v_ln_g": 1.0 + 0.1 * nrm(ks[6], (N_ODD, GMLP_WIDTH), f32),
        "odd_v_ln_b": 0.1 * nrm(ks[7], (N_ODD, GMLP_WIDTH), f32),
        "odd_w_s": nrm(ks[8], (N_ODD, GMLP_GROUPS, GMLP_BLOCK, GMLP_BLOCK), f32) * GMLP_BLOCK ** -0.5,
        "odd_b_s": 1.0 + 0.1 * nrm(ks[9], (N_ODD, GMLP_GROUPS, GMLP_BLOCK), f32),
        "odd_w_out": nrm(ks[10], (N_ODD, GMLP_WIDTH, D_MODEL), f32) * (GMLP_WIDTH ** -0.5 * BETA),
        "mix_ln_g": 1.0 + 0.1 * nrm(ks[11], (DEPTH, D_MODEL), f32),
        "mix_ln_b": 0.1 * nrm(ks[12], (DEPTH, D_MODEL), f32),
        "ffn_w_in": nrm(ks[13], (DEPTH, D_MODEL, 2 * FFN_HIDDEN), f32) * D_MODEL ** -0.5,
        "ffn_w_out": nrm(ks[14], (DEPTH, FFN_HIDDEN, D_MODEL), f32) * (FFN_HIDDEN ** -0.5 * BETA),
        "ffn_ln_g": 1.0 + 0.1 * nrm(ks[15], (DEPTH, D_MODEL), f32),
        "ffn_ln_b": 0.1 * nrm(ks[16], (DEPTH, D_MODEL), f32),
    }


def _fwd_reference(x, even_w_in, even_b_f, even_conv_w, even_w_out, odd_w_in, odd_v_ln_g,
              odd_v_ln_b, odd_w_s, odd_b_s, odd_w_out, mix_ln_g, mix_ln_b, ffn_w_in,
              ffn_w_out, ffn_ln_g, ffn_ln_b):
    for layer in range(DEPTH):
        i = layer // 2
        if layer % 2 == 0:
            m = fox_conv_mixer(x, even_w_in[i], even_b_f[i], even_conv_w[i], even_w_out[i])
        else:
            m = gmlp_mixer(x, odd_w_in[i], odd_v_ln_g[i], odd_v_ln_b[i], odd_w_s[i],
                           odd_b_s[i], odd_w_out[i])
        x = layer_norm(ALPHA * x + m, mix_ln_g[layer], mix_ln_b[layer])
        x = layer_norm(ALPHA * x + swiglu(x, ffn_w_in[layer], ffn_w_out[layer]),
                       ffn_ln_g[layer], ffn_ln_b[layer])
    return x


import jax as _jax
import jax.numpy as _jnp

TWIN_FORMAT = 'train_step'
FWD_PARAMS = ['x', 'even_w_in', 'even_b_f', 'even_conv_w', 'even_w_out', 'odd_w_in', 'odd_v_ln_g', 'odd_v_ln_b', 'odd_w_s', 'odd_b_s', 'odd_w_out', 'mix_ln_g', 'mix_ln_b', 'ffn_w_in', 'ffn_w_out', 'ffn_ln_g', 'ffn_ln_b']
TWIN_WEIGHTS = ['even_w_in', 'even_b_f', 'even_conv_w', 'even_w_out', 'odd_w_in', 'odd_v_ln_g', 'odd_v_ln_b', 'odd_w_s', 'odd_b_s', 'odd_w_out', 'mix_ln_g', 'mix_ln_b', 'ffn_w_in', 'ffn_w_out', 'ffn_ln_g', 'ffn_ln_b']
TWIN_DIFF_INPUT = 'x'
TWIN_INPUTS = ['x', 'even_w_in', 'even_b_f', 'even_conv_w', 'even_w_out', 'odd_w_in', 'odd_v_ln_g', 'odd_v_ln_b', 'odd_w_s', 'odd_b_s', 'odd_w_out', 'mix_ln_g', 'mix_ln_b', 'ffn_w_in', 'ffn_w_out', 'ffn_ln_g', 'ffn_ln_b', 'loss_target', 'm_even_w_in', 'm_even_b_f', 'm_even_conv_w', 'm_even_w_out', 'm_odd_w_in', 'm_odd_v_ln_g', 'm_odd_v_ln_b', 'm_odd_w_s', 'm_odd_b_s', 'm_odd_w_out', 'm_mix_ln_g', 'm_mix_ln_b', 'm_ffn_w_in', 'm_ffn_w_out', 'm_ffn_ln_g', 'm_ffn_ln_b', 'v_even_w_in', 'v_even_b_f', 'v_even_conv_w', 'v_even_w_out', 'v_odd_w_in', 'v_odd_v_ln_g', 'v_odd_v_ln_b', 'v_odd_w_s', 'v_odd_b_s', 'v_odd_w_out', 'v_mix_ln_g', 'v_mix_ln_b', 'v_ffn_w_in', 'v_ffn_w_out', 'v_ffn_ln_g', 'v_ffn_ln_b']
TWIN_OUTPUTS = ['loss', 'grad_x', 'grad_even_w_in', 'grad_even_b_f', 'grad_even_conv_w', 'grad_even_w_out', 'grad_odd_w_in', 'grad_odd_v_ln_g', 'grad_odd_v_ln_b', 'grad_odd_w_s', 'grad_odd_b_s', 'grad_odd_w_out', 'grad_mix_ln_g', 'grad_mix_ln_b', 'grad_ffn_w_in', 'grad_ffn_w_out', 'grad_ffn_ln_g', 'grad_ffn_ln_b', 'delta_even_w_in', 'delta_even_b_f', 'delta_even_conv_w', 'delta_even_w_out', 'delta_odd_w_in', 'delta_odd_v_ln_g', 'delta_odd_v_ln_b', 'delta_odd_w_s', 'delta_odd_b_s', 'delta_odd_w_out', 'delta_mix_ln_g', 'delta_mix_ln_b', 'delta_ffn_w_in', 'delta_ffn_w_out', 'delta_ffn_ln_g', 'delta_ffn_ln_b', 'new_m_even_w_in', 'new_m_even_b_f', 'new_m_even_conv_w', 'new_m_even_w_out', 'new_m_odd_w_in', 'new_m_odd_v_ln_g', 'new_m_odd_v_ln_b', 'new_m_odd_w_s', 'new_m_odd_b_s', 'new_m_odd_w_out', 'new_m_mix_ln_g', 'new_m_mix_ln_b', 'new_m_ffn_w_in', 'new_m_ffn_w_out', 'new_m_ffn_ln_g', 'new_m_ffn_ln_b', 'new_v_even_w_in', 'new_v_even_b_f', 'new_v_even_conv_w', 'new_v_even_w_out', 'new_v_odd_w_in', 'new_v_odd_v_ln_g', 'new_v_odd_v_ln_b', 'new_v_odd_w_s', 'new_v_odd_b_s', 'new_v_odd_w_out', 'new_v_mix_ln_g', 'new_v_mix_ln_b', 'new_v_ffn_w_in', 'new_v_ffn_w_out', 'new_v_ffn_ln_g', 'new_v_ffn_ln_b']
TWIN_LEAF_KINDS = {'loss': 'loss', 'grad_x': 'grad_x', 'grad_even_w_in': 'grad_w', 'grad_even_b_f': 'grad_w', 'grad_even_conv_w': 'grad_w', 'grad_even_w_out': 'grad_w', 'grad_odd_w_in': 'grad_w', 'grad_odd_v_ln_g': 'grad_w', 'grad_odd_v_ln_b': 'grad_w', 'grad_odd_w_s': 'grad_w', 'grad_odd_b_s': 'grad_w', 'grad_odd_w_out': 'grad_w', 'grad_mix_ln_g': 'grad_w', 'grad_mix_ln_b': 'grad_w', 'grad_ffn_w_in': 'grad_w', 'grad_ffn_w_out': 'grad_w', 'grad_ffn_ln_g': 'grad_w', 'grad_ffn_ln_b': 'grad_w', 'delta_even_w_in': 'delta_w', 'delta_even_b_f': 'delta_w', 'delta_even_conv_w': 'delta_w', 'delta_even_w_out': 'delta_w', 'delta_odd_w_in': 'delta_w', 'delta_odd_v_ln_g': 'delta_w', 'delta_odd_v_ln_b': 'delta_w', 'delta_odd_w_s': 'delta_w', 'delta_odd_b_s': 'delta_w', 'delta_odd_w_out': 'delta_w', 'delta_mix_ln_g': 'delta_w', 'delta_mix_ln_b': 'delta_w', 'delta_ffn_w_in': 'delta_w', 'delta_ffn_w_out': 'delta_w', 'delta_ffn_ln_g': 'delta_w', 'delta_ffn_ln_b': 'delta_w', 'new_m_even_w_in': 'new_m', 'new_m_even_b_f': 'new_m', 'new_m_even_conv_w': 'new_m', 'new_m_even_w_out': 'new_m', 'new_m_odd_w_in': 'new_m', 'new_m_odd_v_ln_g': 'new_m', 'new_m_odd_v_ln_b': 'new_m', 'new_m_odd_w_s': 'new_m', 'new_m_odd_b_s': 'new_m', 'new_m_odd_w_out': 'new_m', 'new_m_mix_ln_g': 'new_m', 'new_m_mix_ln_b': 'new_m', 'new_m_ffn_w_in': 'new_m', 'new_m_ffn_w_out': 'new_m', 'new_m_ffn_ln_g': 'new_m', 'new_m_ffn_ln_b': 'new_m', 'new_v_even_w_in': 'new_v', 'new_v_even_b_f': 'new_v', 'new_v_even_conv_w': 'new_v', 'new_v_even_w_out': 'new_v', 'new_v_odd_w_in': 'new_v', 'new_v_odd_v_ln_g': 'new_v', 'new_v_odd_v_ln_b': 'new_v', 'new_v_odd_w_s': 'new_v', 'new_v_odd_b_s': 'new_v', 'new_v_odd_w_out': 'new_v', 'new_v_mix_ln_g': 'new_v', 'new_v_mix_ln_b': 'new_v', 'new_v_ffn_w_in': 'new_v', 'new_v_ffn_w_out': 'new_v', 'new_v_ffn_ln_g': 'new_v', 'new_v_ffn_ln_b': 'new_v'}


def _forward(args):
    return _fwd_reference(*[args[k] for k in FWD_PARAMS])


def _output_shape():
    def fwd():
        inp = _fwd_setup_inputs(0)
        return _fwd_reference(*[inp[k] for k in FWD_PARAMS])
    out = _jax.eval_shape(fwd)
    return out.shape, out.dtype

N_MICROBATCH = 1
ADAM_LR = 0.001
ADAM_B1 = 0.9
ADAM_B2 = 0.999
ADAM_EPS = 1e-08
ADAM_WD = 0.01
ADAM_STEP = 10
PER_EXAMPLE_BATCH_AXIS = {'x': 0, 'loss_target': 0}
SHARED_INPUTS = []
_WEIGHT_DTYPES = {'even_w_in': _jnp.float32, 'even_b_f': _jnp.float32, 'even_conv_w': _jnp.float32, 'even_w_out': _jnp.float32, 'odd_w_in': _jnp.float32, 'odd_v_ln_g': _jnp.float32, 'odd_v_ln_b': _jnp.float32, 'odd_w_s': _jnp.float32, 'odd_b_s': _jnp.float32, 'odd_w_out': _jnp.float32, 'mix_ln_g': _jnp.float32, 'mix_ln_b': _jnp.float32, 'ffn_w_in': _jnp.float32, 'ffn_w_out': _jnp.float32, 'ffn_ln_g': _jnp.float32, 'ffn_ln_b': _jnp.float32}
MOMENT_SCALE = {'even_w_in': 6.623613e-02, 'even_b_f': 4.511725e-01, 'even_conv_w': 9.778967e-02, 'even_w_out': 1.383161e-01, 'odd_w_in': 9.631757e-02, 'odd_v_ln_g': 4.932423e-02, 'odd_v_ln_b': 5.067719e-02, 'odd_w_s': 5.062345e-02, 'odd_b_s': 6.453428e-02, 'odd_w_out': 1.138081e+00, 'mix_ln_g': 1.163952e+01, 'mix_ln_b': 6.273140e+00, 'ffn_w_in': 4.110315e-02, 'ffn_w_out': 1.520890e-01, 'ffn_ln_g': 4.818937e+01, 'ffn_ln_b': 1.032287e+01}


def _to_microbatches(a, axis):
    t = _jnp.moveaxis(a, axis, 0)
    t = t.reshape((N_MICROBATCH, t.shape[0] // N_MICROBATCH) + t.shape[1:])
    return _jnp.moveaxis(t, 1, axis + 1)


def setup_inputs(seed: int = 0) -> dict:
    inp = _fwd_setup_inputs(seed)
    key = _jax.random.fold_in(_jax.random.key(seed), 7919)
    shape, _ = _output_shape()
    out = dict(inp)
    out["loss_target"] = _jax.random.normal(_jax.random.fold_in(key, 0), shape, _jnp.float32)
    for i, name in enumerate(TWIN_WEIGHTS):
        w = inp[name].astype(_jnp.float32)
        if MOMENT_SCALE is None:
            s = _jnp.sqrt(_jnp.mean(_jnp.square(w)) + 1e-30)
        else:
            s = MOMENT_SCALE[name]
        km, kv = _jax.random.split(_jax.random.fold_in(key, i + 1))
        out[name] = w
        out["m_" + name] = s * _jax.random.normal(km, w.shape, _jnp.float32)
        out["v_" + name] = (s * s) * _jax.random.uniform(kv, w.shape, _jnp.float32, 0.5, 1.5)
    if N_MICROBATCH > 1:
        for name, axis in PER_EXAMPLE_BATCH_AXIS.items():
            out[name] = _to_microbatches(out[name], axis)
    return {'x': out['x'], 'even_w_in': out['even_w_in'], 'even_b_f': out['even_b_f'], 'even_conv_w': out['even_conv_w'], 'even_w_out': out['even_w_out'], 'odd_w_in': out['odd_w_in'], 'odd_v_ln_g': out['odd_v_ln_g'], 'odd_v_ln_b': out['odd_v_ln_b'], 'odd_w_s': out['odd_w_s'], 'odd_b_s': out['odd_b_s'], 'odd_w_out': out['odd_w_out'], 'mix_ln_g': out['mix_ln_g'], 'mix_ln_b': out['mix_ln_b'], 'ffn_w_in': out['ffn_w_in'], 'ffn_w_out': out['ffn_w_out'], 'ffn_ln_g': out['ffn_ln_g'], 'ffn_ln_b': out['ffn_ln_b'], 'loss_target': out['loss_target'], 'm_even_w_in': out['m_even_w_in'], 'm_even_b_f': out['m_even_b_f'], 'm_even_conv_w': out['m_even_conv_w'], 'm_even_w_out': out['m_even_w_out'], 'm_odd_w_in': out['m_odd_w_in'], 'm_odd_v_ln_g': out['m_odd_v_ln_g'], 'm_odd_v_ln_b': out['m_odd_v_ln_b'], 'm_odd_w_s': out['m_odd_w_s'], 'm_odd_b_s': out['m_odd_b_s'], 'm_odd_w_out': out['m_odd_w_out'], 'm_mix_ln_g': out['m_mix_ln_g'], 'm_mix_ln_b': out['m_mix_ln_b'], 'm_ffn_w_in': out['m_ffn_w_in'], 'm_ffn_w_out': out['m_ffn_w_out'], 'm_ffn_ln_g': out['m_ffn_ln_g'], 'm_ffn_ln_b': out['m_ffn_ln_b'], 'v_even_w_in': out['v_even_w_in'], 'v_even_b_f': out['v_even_b_f'], 'v_even_conv_w': out['v_even_conv_w'], 'v_even_w_out': out['v_even_w_out'], 'v_odd_w_in': out['v_odd_w_in'], 'v_odd_v_ln_g': out['v_odd_v_ln_g'], 'v_odd_v_ln_b': out['v_odd_v_ln_b'], 'v_odd_w_s': out['v_odd_w_s'], 'v_odd_b_s': out['v_odd_b_s'], 'v_odd_w_out': out['v_odd_w_out'], 'v_mix_ln_g': out['v_mix_ln_g'], 'v_mix_ln_b': out['v_mix_ln_b'], 'v_ffn_w_in': out['v_ffn_w_in'], 'v_ffn_w_out': out['v_ffn_w_out'], 'v_ffn_ln_g': out['v_ffn_ln_g'], 'v_ffn_ln_b': out['v_ffn_ln_b']}


def _loss(weights, diff, rest, loss_target):
    with _jax.named_scope("forward"):
        args = {**rest, TWIN_DIFF_INPUT: diff, **{k: w.astype(_WEIGHT_DTYPES[k]) for k, w in weights.items()}}
        y = _forward(args)
    with _jax.named_scope("loss_head"):
        err = _jnp.square(y.astype(_jnp.float32) - loss_target)
        return 0.5 * _jnp.sum(_jnp.mean(err, axis=-1)) if err.ndim else 0.5 * err


def _adamw(w, g, m, v):
    m = ADAM_B1 * m + (1.0 - ADAM_B1) * g
    v = ADAM_B2 * v + (1.0 - ADAM_B2) * _jnp.square(g)
    m_hat = m / (1.0 - ADAM_B1 ** ADAM_STEP)
    v_hat = v / (1.0 - ADAM_B2 ** ADAM_STEP)
    delta = -ADAM_LR * (m_hat / (_jnp.sqrt(v_hat) + ADAM_EPS) + ADAM_WD * w)
    return delta, m, v


def reference(x, even_w_in, even_b_f, even_conv_w, even_w_out, odd_w_in, odd_v_ln_g, odd_v_ln_b, odd_w_s, odd_b_s, odd_w_out, mix_ln_g, mix_ln_b, ffn_w_in, ffn_w_out, ffn_ln_g, ffn_ln_b, loss_target, m_even_w_in, m_even_b_f, m_even_conv_w, m_even_w_out, m_odd_w_in, m_odd_v_ln_g, m_odd_v_ln_b, m_odd_w_s, m_odd_b_s, m_odd_w_out, m_mix_ln_g, m_mix_ln_b, m_ffn_w_in, m_ffn_w_out, m_ffn_ln_g, m_ffn_ln_b, v_even_w_in, v_even_b_f, v_even_conv_w, v_even_w_out, v_odd_w_in, v_odd_v_ln_g, v_odd_v_ln_b, v_odd_w_s, v_odd_b_s, v_odd_w_out, v_mix_ln_g, v_mix_ln_b, v_ffn_w_in, v_ffn_w_out, v_ffn_ln_g, v_ffn_ln_b):
    given = dict(x=x, even_w_in=even_w_in, even_b_f=even_b_f, even_conv_w=even_conv_w, even_w_out=even_w_out, odd_w_in=odd_w_in, odd_v_ln_g=odd_v_ln_g, odd_v_ln_b=odd_v_ln_b, odd_w_s=odd_w_s, odd_b_s=odd_b_s, odd_w_out=odd_w_out, mix_ln_g=mix_ln_g, mix_ln_b=mix_ln_b, ffn_w_in=ffn_w_in, ffn_w_out=ffn_w_out, ffn_ln_g=ffn_ln_g, ffn_ln_b=ffn_ln_b, loss_target=loss_target, m_even_w_in=m_even_w_in, m_even_b_f=m_even_b_f, m_even_conv_w=m_even_conv_w, m_even_w_out=m_even_w_out, m_odd_w_in=m_odd_w_in, m_odd_v_ln_g=m_odd_v_ln_g, m_odd_v_ln_b=m_odd_v_ln_b, m_odd_w_s=m_odd_w_s, m_odd_b_s=m_odd_b_s, m_odd_w_out=m_odd_w_out, m_mix_ln_g=m_mix_ln_g, m_mix_ln_b=m_mix_ln_b, m_ffn_w_in=m_ffn_w_in, m_ffn_w_out=m_ffn_w_out, m_ffn_ln_g=m_ffn_ln_g, m_ffn_ln_b=m_ffn_ln_b, v_even_w_in=v_even_w_in, v_even_b_f=v_even_b_f, v_even_conv_w=v_even_conv_w, v_even_w_out=v_even_w_out, v_odd_w_in=v_odd_w_in, v_odd_v_ln_g=v_odd_v_ln_g, v_odd_v_ln_b=v_odd_v_ln_b, v_odd_w_s=v_odd_w_s, v_odd_b_s=v_odd_b_s, v_odd_w_out=v_odd_w_out, v_mix_ln_g=v_mix_ln_g, v_mix_ln_b=v_mix_ln_b, v_ffn_w_in=v_ffn_w_in, v_ffn_w_out=v_ffn_w_out, v_ffn_ln_g=v_ffn_ln_g, v_ffn_ln_b=v_ffn_ln_b)
    weights = {n: given[n] for n in TWIN_WEIGHTS}
    shared = {n: given[n] for n in SHARED_INPUTS}
    per_example = {n: given[n] for n in ['x']}
    grad_fn = _jax.value_and_grad(_loss, argnums=(0, 1))

    def one_microbatch(ex, loss_target):
        ex = dict(ex)
        diff = ex.pop(TWIN_DIFF_INPUT)
        return grad_fn(weights, diff, {**shared, **ex}, loss_target)

    if N_MICROBATCH == 1:
        loss, (grad_w, grad_x) = one_microbatch(per_example, given["loss_target"])
    else:
        def body(carry, xs):
            loss_sum, grad_sum = carry
            l_k, (gw_k, gx_k) = one_microbatch(xs[0], xs[1])
            with _jax.named_scope("update"):
                return (loss_sum + l_k, _jax.tree.map(_jnp.add, grad_sum, gw_k)), gx_k

        init = (_jnp.zeros((), _jnp.float32), _jax.tree.map(_jnp.zeros_like, weights))
        (loss, grad_w), grad_x = _jax.lax.scan(body, init, (per_example, given["loss_target"]))
    with _jax.named_scope("update"):
        delta_w, new_m, new_v = {}, {}, {}
        for n in TWIN_WEIGHTS:
            delta_w[n], new_m[n], new_v[n] = _adamw(weights[n], grad_w[n], given["m_" + n], given["v_" + n])
    return (loss, grad_x, *[grad_w[n] for n in TWIN_WEIGHTS], *[delta_w[n] for n in TWIN_WEIGHTS],
            *[new_m[n] for n in TWIN_WEIGHTS], *[new_v[n] for n in TWIN_WEIGHTS])
```

```python
import jax
import jax.numpy as jnp
from jax import lax
from jax.experimental import pallas as pl
from jax.experimental.pallas import tpu as pltpu

F32 = jnp.float32
BF16 = jnp.bfloat16
MESH = pl.DeviceIdType.MESH

DEPTH = 2
ALPHA = (2.0 * DEPTH) ** 0.25
LN_EPS = 1e-5
ADAM_LR = 0.001
ADAM_B1 = 0.9
ADAM_B2 = 0.999
ADAM_EPS = 1e-08
ADAM_WD = 0.01
ADAM_STEP = 10

N_DEV = 8
LANES = 128
SUBLANES = 8
VMEM_LIMIT = 48 * 1024 * 1024
NEG_BIG = -1e30
PACK_COLS = 1024
PACK_ROWS = 512
ROW_TILES = (512, 256, 128)


def _pick(n, cands):
    for c in cands:
        if c <= n and n % c == 0:
            return c
    return n


def _params(*sem):
    return pltpu.CompilerParams(dimension_semantics=sem, vmem_limit_bytes=VMEM_LIMIT)


def _mm(a, b, mode, out_dtype, name):
    if mode == "nn":
        (m, k), (k2, n) = a.shape, b.shape
    elif mode == "nt":
        (m, k), (n, k2) = a.shape, b.shape
    else:
        (k, m), (k2, n) = a.shape, b.shape
    assert k == k2, (a.shape, b.shape, mode)
    tm = _pick(m, (1024, 512, 1408, 256, 128))
    tn = _pick(n, (512, 640, 256, 128))
    tk = _pick(k, (1024, 512, 640, 1408, 256, 128))
    nk = k // tk
    if mode == "nn":
        a_spec = pl.BlockSpec((tm, tk), lambda i, j, kk: (i, kk))
        b_spec = pl.BlockSpec((tk, tn), lambda i, j, kk: (kk, j))
        dims = (((1,), (0,)), ((), ()))
    elif mode == "nt":
        a_spec = pl.BlockSpec((tm, tk), lambda i, j, kk: (i, kk))
        b_spec = pl.BlockSpec((tn, tk), lambda i, j, kk: (j, kk))
        dims = (((1,), (1,)), ((), ()))
    else:
        a_spec = pl.BlockSpec((tk, tm), lambda i, j, kk: (kk, i))
        b_spec = pl.BlockSpec((tk, tn), lambda i, j, kk: (kk, j))
        dims = (((0,), (0,)), ((), ()))

    def body(a_ref, b_ref, o_ref, acc_ref):
        kk = pl.program_id(2)

        @pl.when(kk == 0)
        def _():
            acc_ref[...] = jnp.zeros_like(acc_ref)

        acc_ref[...] += lax.dot_general(a_ref[...].astype(BF16), b_ref[...].astype(BF16), dims,
                                        preferred_element_type=F32)

        @pl.when(kk == nk - 1)
        def _():
            o_ref[...] = acc_ref[...].astype(o_ref.dtype)

    return pl.pallas_call(
        body, name=name, grid=(m // tm, n // tn, nk),
        in_specs=[a_spec, b_spec],
        out_specs=pl.BlockSpec((tm, tn), lambda i, j, kk: (i, j)),
        out_shape=jax.ShapeDtypeStruct((m, n), out_dtype),
        scratch_shapes=[pltpu.VMEM((tm, tn), F32)],
        compiler_params=_params("parallel", "parallel", "arbitrary"),
    )(a, b)


def _ln_fwd(xa, xb, g, b, name):
    t, d = xa.shape
    tb = _pick(t, ROW_TILES)

    def body(xa_ref, xb_ref, g_ref, b_ref, y_ref, yb_ref):
        z = ALPHA * xa_ref[...] + xb_ref[...]
        mu = jnp.mean(z, axis=-1, keepdims=True)
        zc = z - mu
        var = jnp.mean(zc * zc, axis=-1, keepdims=True)
        y = zc * lax.rsqrt(var + LN_EPS) * g_ref[...] + b_ref[...]
        y_ref[...] = y
        yb_ref[...] = y.astype(BF16)

    row = pl.BlockSpec((tb, d), lambda i: (i, 0))
    vec = pl.BlockSpec((1, d), lambda i: (0, 0))
    return pl.pallas_call(
        body, name=name, grid=(t // tb,),
        in_specs=[row, row, vec, vec], out_specs=[row, row],
        out_shape=[jax.ShapeDtypeStruct((t, d), F32), jax.ShapeDtypeStruct((t, d), BF16)],
        compiler_params=_params("parallel"),
    )(xa, xb, g.reshape(1, d), b.reshape(1, d))


def _ln_bwd(xa, xb, g, dya, ca, dyb, name):
    t, d = xa.shape
    tb = _pick(t, ROW_TILES)
    two = dyb is not None

    def body(*refs):
        if two:
            xa_ref, xb_ref, g_ref, dya_ref, dyb_ref, dz_ref, dg_ref, db_ref = refs
            dy = ca * dya_ref[...] + dyb_ref[...]
        else:
            xa_ref, xb_ref, g_ref, dya_ref, dz_ref, dg_ref, db_ref = refs
            dy = ca * dya_ref[...]
        z = ALPHA * xa_ref[...] + xb_ref[...]
        mu = jnp.mean(z, axis=-1, keepdims=True)
        zc = z - mu
        var = jnp.mean(zc * zc, axis=-1, keepdims=True)
        rstd = lax.rsqrt(var + LN_EPS)
        xhat = zc * rstd
        dxh = dy * g_ref[...]
        m1 = jnp.mean(dxh, axis=-1, keepdims=True)
        m2 = jnp.mean(dxh * xhat, axis=-1, keepdims=True)
        dz_ref[...] = rstd * (dxh - m1 - xhat * m2)

        @pl.when(pl.program_id(0) == 0)
        def _():
            dg_ref[...] = jnp.zeros_like(dg_ref)
            db_ref[...] = jnp.zeros_like(db_ref)

        dg_ref[...] += jnp.sum(dy * xhat, axis=0, keepdims=True)
        db_ref[...] += jnp.sum(dy, axis=0, keepdims=True)

    row = pl.BlockSpec((tb, d), lambda i: (i, 0))
    vec = pl.BlockSpec((1, d), lambda i: (0, 0))
    ins = [xa, xb, g.reshape(1, d), dya] + ([dyb] if two else [])
    dz, dg, db = pl.pallas_call(
        body, name=name, grid=(t // tb,),
        in_specs=[row, row, vec, row] + ([row] if two else []),
        out_specs=[row, vec, vec],
        out_shape=[jax.ShapeDtypeStruct((t, d), F32), jax.ShapeDtypeStruct((1, d), F32),
                   jax.ShapeDtypeStruct((1, d), F32)],
        compiler_params=_params("arbitrary"),
    )(*ins)
    return dz, dg[0], db[0]


def _loss(y, target, name):
    t, d = y.shape
    tb = _pick(t, ROW_TILES)

    def body(y_ref, t_ref, dy_ref, l_ref):
        e = y_ref[...] - t_ref[...]
        dy_ref[...] = e * (1.0 / d)

        @pl.when(pl.program_id(0) == 0)
        def _():
            l_ref[...] = jnp.zeros_like(l_ref)

        l_ref[...] += 0.5 * jnp.sum(jnp.mean(e * e, axis=-1, keepdims=True))

    row = pl.BlockSpec((tb, d), lambda i: (i, 0))
    dy, l = pl.pallas_call(
        body, name=name, grid=(t // tb,),
        in_specs=[row, row], out_specs=[row, pl.BlockSpec((1, LANES), lambda i: (0, 0))],
        out_shape=[jax.ShapeDtypeStruct((t, d), F32), jax.ShapeDtypeStruct((1, LANES), F32)],
        compiler_params=_params("arbitrary"),
    )(y, target)
    return l[0, 0], dy


def _axpy(ca, a, b, name):
    t, d = a.shape
    tb = _pick(t, ROW_TILES)

    def body(a_ref, b_ref, o_ref):
        o_ref[...] = ca * a_ref[...] + b_ref[...]

    row = pl.BlockSpec((tb, d), lambda i: (i, 0))
    return pl.pallas_call(
        body, name=name, grid=(t // tb,), in_specs=[row, row], out_specs=row,
        out_shape=jax.ShapeDtypeStruct((t, d), F32), compiler_params=_params("parallel"),
    )(a, b)


def _swiglu_fwd(gu, name):
    t, f2 = gu.shape
    f = f2 // 2
    tb = _pick(t, ROW_TILES[1:] or ROW_TILES)

    def body(gu_ref, h_ref):
        gate = gu_ref[:, :f]
        up = gu_ref[:, f:]
        h_ref[...] = (gate * jax.nn.sigmoid(gate) * up).astype(BF16)

    return pl.pallas_call(
        body, name=name, grid=(t // tb,),
        in_specs=[pl.BlockSpec((tb, f2), lambda i: (i, 0))],
        out_specs=pl.BlockSpec((tb, f), lambda i: (i, 0)),
        out_shape=jax.ShapeDtypeStruct((t, f), BF16), compiler_params=_params("parallel"),
    )(gu)


def _swiglu_bwd(gu, dh, name):
    t, f2 = gu.shape
    f = f2 // 2
    tb = _pick(t, ROW_TILES[1:] or ROW_TILES)

    def body(gu_ref, dh_ref, o_ref):
        gate = gu_ref[:, :f]
        up = gu_ref[:, f:]
        d = dh_ref[...].astype(F32)
        sg = jax.nn.sigmoid(gate)
        silu = gate * sg
        o_ref[:, :f] = (d * up * (sg + silu * (1.0 - sg))).astype(BF16)
        o_ref[:, f:] = (d * silu).astype(BF16)

    return pl.pallas_call(
        body, name=name, grid=(t // tb,),
        in_specs=[pl.BlockSpec((tb, f2), lambda i: (i, 0)), pl.BlockSpec((tb, f), lambda i: (i, 0))],
        out_specs=pl.BlockSpec((tb, f2), lambda i: (i, 0)),
        out_shape=jax.ShapeDtypeStruct((t, f2), BF16), compiler_params=_params("parallel"),
    )(gu, dh)


def _tri_matmul(tri, x):
    x1 = x.astype(BF16)
    r1 = x - x1.astype(F32)
    x2 = r1.astype(BF16)
    x3 = (r1 - x2.astype(F32)).astype(BF16)
    dot = lambda v: jnp.dot(tri, v, preferred_element_type=F32)
    return dot(x1) + dot(x2) + dot(x3)


def _fgate_fwd(proj, bf_pad, fcol, n_heads, name):
    t = proj.shape[0]
    tb = _pick(t, ROW_TILES)

    def body(p_ref, b_ref, c_ref, carry):
        @pl.when(pl.program_id(0) == 0)
        def _():
            carry[...] = jnp.zeros_like(carry)

        z = p_ref[...] + b_ref[...]
        lf = jnp.minimum(z, 0.0) - jnp.log1p(jnp.exp(-jnp.abs(z)))
        lane = lax.broadcasted_iota(jnp.int32, (tb, LANES), 1)
        lf = jnp.where(lane < n_heads, lf, 0.0)
        r = lax.broadcasted_iota(jnp.int32, (tb, tb), 0)
        s = lax.broadcasted_iota(jnp.int32, (tb, tb), 1)
        tri = (s <= r).astype(BF16)
        c = _tri_matmul(tri, lf) + carry[...]
        c_ref[...] = c
        carry[...] = c[tb - 1:tb, :]

    return pl.pallas_call(
        body, name=name, grid=(t // tb,),
        in_specs=[pl.BlockSpec((tb, LANES), lambda i: (i, fcol)), pl.BlockSpec((1, LANES), lambda i: (0, 0))],
        out_specs=pl.BlockSpec((tb, LANES), lambda i: (i, 0)),
        out_shape=jax.ShapeDtypeStruct((t, LANES), F32),
        scratch_shapes=[pltpu.VMEM((1, LANES), F32)],
        compiler_params=_params("arbitrary"),
    )(proj, bf_pad)


def _fgate_bwd(proj, bf_pad, dcq, dck, fcol, n_heads, name):
    t = proj.shape[0]
    tb = _pick(t, ROW_TILES)
    nb = t // tb

    def body(p_ref, b_ref, dcq_ref, dck_ref, dz_ref, db_ref, carry):
        @pl.when(pl.program_id(0) == 0)
        def _():
            carry[...] = jnp.zeros_like(carry)
            db_ref[...] = jnp.zeros_like(db_ref)

        r = lax.broadcasted_iota(jnp.int32, (tb, tb), 0)
        s = lax.broadcasted_iota(jnp.int32, (tb, tb), 1)
        tri = (s >= r).astype(BF16)
        dlf = _tri_matmul(tri, dcq_ref[...] + dck_ref[...]) + carry[...]
        carry[...] = dlf[0:1, :]
        z = p_ref[...] + b_ref[...]
        lane = lax.broadcasted_iota(jnp.int32, (tb, LANES), 1)
        dz = jnp.where(lane < n_heads, dlf * jax.nn.sigmoid(-z), 0.0)
        dz_ref[...] = dz
        db_ref[...] += jnp.sum(dz, axis=0, keepdims=True)

    dz, db = pl.pallas_call(
        body, name=name, grid=(nb,),
        in_specs=[pl.BlockSpec((tb, LANES), lambda i: (nb - 1 - i, fcol)),
                  pl.BlockSpec((1, LANES), lambda i: (0, 0)),
                  pl.BlockSpec((tb, LANES), lambda i: (nb - 1 - i, 0)),
                  pl.BlockSpec((tb, LANES), lambda i: (nb - 1 - i, 0))],
        out_specs=[pl.BlockSpec((tb, LANES), lambda i: (nb - 1 - i, 0)),
                   pl.BlockSpec((1, LANES), lambda i: (0, 0))],
        out_shape=[jax.ShapeDtypeStruct((t, LANES), F32), jax.ShapeDtypeStruct((1, LANES), F32)],
        scratch_shapes=[pltpu.VMEM((1, LANES), F32)],
        compiler_params=_params("arbitrary"),
    )(proj, bf_pad, dcq, dck)
    return dz, db[0]


def _lane_col(x, lane, h):
    return jnp.sum(jnp.where(lane == h, x, 0.0), axis=1, keepdims=True)


def _attn_scores(q_ref, k_ref, ct_ref, cq, h, future):
    s = lax.dot_general(q_ref[h], k_ref[h], (((1,), (1,)), ((), ())), preferred_element_type=F32)
    s = s + (cq - ct_ref[h:h + 1, :])
    return jnp.where(future, NEG_BIG, s)


def _attn_fwd(q, k, v, c, ct, name):
    nh, t, dh = q.shape
    tq = _pick(t, ROW_TILES)
    nq = t // tq

    def body(q_ref, k_ref, v_ref, c_ref, ct_ref, o_ref, lse_ref, m_s, l_s, acc_s, cq_s):
        qi = pl.program_id(0)
        ki = pl.program_id(1)
        lane = lax.broadcasted_iota(jnp.int32, (tq, LANES), 1)

        @pl.when(ki == 0)
        def _():
            m_s[...] = jnp.full(m_s.shape, NEG_BIG, F32)
            l_s[...] = jnp.zeros_like(l_s)
            acc_s[...] = jnp.zeros_like(acc_s)
            cblk = c_ref[...]
            for h in range(nh):
                cq_s[h] = _lane_col(cblk, lane, h)

        @pl.when(ki <= qi)
        def _():
            row = lax.broadcasted_iota(jnp.int32, (tq, tq), 0)
            col = lax.broadcasted_iota(jnp.int32, (tq, tq), 1)
            future = jnp.logical_and(col > row, ki == qi)
            for h in range(nh):
                s = _attn_scores(q_ref, k_ref, ct_ref, cq_s[h], h, future)
                m_prev = m_s[h]
                m_new = jnp.maximum(m_prev, jnp.max(s, axis=1, keepdims=True))
                a = jnp.exp(m_prev - m_new)
                p = jnp.exp(s - m_new)
                l_s[h] = a * l_s[h] + jnp.sum(p, axis=1, keepdims=True)
                acc_s[h] = a * acc_s[h] + jnp.dot(p.astype(BF16), v_ref[h], preferred_element_type=F32)
                m_s[h] = m_new

        @pl.when(ki == qi)
        def _():
            lse = jnp.zeros((tq, LANES), F32)
            for h in range(nh):
                o_ref[h] = (acc_s[h] / l_s[h]).astype(o_ref.dtype)
                lse = lse + jnp.where(lane == h, m_s[h] + jnp.log(l_s[h]), 0.0)
            lse_ref[...] = lse

    qspec = pl.BlockSpec((nh, tq, dh), lambda qi, ki: (0, qi, 0))
    kspec = pl.BlockSpec((nh, tq, dh), lambda qi, ki: (0, jnp.minimum(ki, qi), 0))
    return pl.pallas_call(
        body, name=name, grid=(nq, nq),
        in_specs=[qspec, kspec, kspec, pl.BlockSpec((tq, LANES), lambda qi, ki: (qi, 0)),
                  pl.BlockSpec((nh, tq), lambda qi, ki: (0, jnp.minimum(ki, qi)))],
        out_specs=[qspec, pl.BlockSpec((tq, LANES), lambda qi, ki: (qi, 0))],
        out_shape=[jax.ShapeDtypeStruct((nh, t, dh), F32), jax.ShapeDtypeStruct((t, LANES), F32)],
        scratch_shapes=[pltpu.VMEM((nh, tq, 1), F32), pltpu.VMEM((nh, tq, 1), F32),
                        pltpu.VMEM((nh, tq, dh), F32), pltpu.VMEM((nh, tq, 1), F32)],
        compiler_params=_params("parallel", "arbitrary"),
    )(q, k, v, c, ct)


def _attn_dq(q, k, v, c, ct, lse, o, do, scale, name):
    nh, t, dh = q.shape
    tq = _pick(t, ROW_TILES)
    nq = t // tq

    def body(q_ref, k_ref, v_ref, c_ref, ct_ref, lse_ref, o_ref, do_ref, dq_ref, drow_ref,
             dq_s, cq_s, lse_s, dl_s, rs_s):
        qi = pl.program_id(0)
        ki = pl.program_id(1)
        lane = lax.broadcasted_iota(jnp.int32, (tq, LANES), 1)

        @pl.when(ki == 0)
        def _():
            dq_s[...] = jnp.zeros_like(dq_s)
            rs_s[...] = jnp.zeros_like(rs_s)
            cblk = c_ref[...]
            lblk = lse_ref[...]
            for h in range(nh):
                cq_s[h] = _lane_col(cblk, lane, h)
                lse_s[h] = _lane_col(lblk, lane, h)
                dl_s[h] = jnp.sum(do_ref[h].astype(F32) * o_ref[h].astype(F32), axis=1, keepdims=True)

        @pl.when(ki <= qi)
        def _():
            row = lax.broadcasted_iota(jnp.int32, (tq, tq), 0)
            col = lax.broadcasted_iota(jnp.int32, (tq, tq), 1)
            future = jnp.logical_and(col > row, ki == qi)
            for h in range(nh):
                s = _attn_scores(q_ref, k_ref, ct_ref, cq_s[h], h, future)
                p = jnp.exp(s - lse_s[h])
                dp = lax.dot_general(do_ref[h], v_ref[h], (((1,), (1,)), ((), ())), preferred_element_type=F32)
                ds = p * (dp - dl_s[h])
                dq_s[h] += jnp.dot(ds.astype(BF16), k_ref[h], preferred_element_type=F32)
                rs_s[h] += jnp.sum(ds, axis=1, keepdims=True)

        @pl.when(ki == qi)
        def _():
            dq_ref[...] = dq_s[...] * scale
            drow = jnp.zeros((tq, LANES), F32)
            for h in range(nh):
                drow = drow + jnp.where(lane == h, rs_s[h], 0.0)
            drow_ref[...] = drow

    qspec = pl.BlockSpec((nh, tq, dh), lambda qi, ki: (0, qi, 0))
    kspec = pl.BlockSpec((nh, tq, dh), lambda qi, ki: (0, jnp.minimum(ki, qi), 0))
    lspec = pl.BlockSpec((tq, LANES), lambda qi, ki: (qi, 0))
    return pl.pallas_call(
        body, name=name, grid=(nq, nq),
        in_specs=[qspec, kspec, kspec, lspec, pl.BlockSpec((nh, tq), lambda qi, ki: (0, jnp.minimum(ki, qi))),
                  lspec, qspec, qspec],
        out_specs=[qspec, lspec],
        out_shape=[jax.ShapeDtypeStruct((nh, t, dh), F32), jax.ShapeDtypeStruct((t, LANES), F32)],
        scratch_shapes=[pltpu.VMEM((nh, tq, dh), F32), pltpu.VMEM((nh, tq, 1), F32),
                        pltpu.VMEM((nh, tq, 1), F32), pltpu.VMEM((nh, tq, 1), F32), pltpu.VMEM((nh, tq, 1), F32)],
        compiler_params=_params("parallel", "arbitrary"),
    )(q, k, v, c, ct, lse, o, do)


def _attn_dkv(q, k, v, c, ct, lse, o, do, name):
    nh, t, dh = q.shape
    tq = _pick(t, ROW_TILES)
    nq = t // tq

    def body(q_ref, k_ref, v_ref, c_ref, ct_ref, lse_ref, o_ref, do_ref, dk_ref, dv_ref, dct_ref):
        ki = pl.program_id(0)
        qi = pl.program_id(1)
        lane = lax.broadcasted_iota(jnp.int32, (tq, LANES), 1)

        @pl.when(qi == 0)
        def _():
            dk_ref[...] = jnp.zeros_like(dk_ref)
            dv_ref[...] = jnp.zeros_like(dv_ref)
            dct_ref[...] = jnp.zeros_like(dct_ref)

        @pl.when(qi >= ki)
        def _():
            row = lax.broadcasted_iota(jnp.int32, (tq, tq), 0)
            col = lax.broadcasted_iota(jnp.int32, (tq, tq), 1)
            future = jnp.logical_and(col > row, ki == qi)
            cblk = c_ref[...]
            lblk = lse_ref[...]
            for h in range(nh):
                s = _attn_scores(q_ref, k_ref, ct_ref, _lane_col(cblk, lane, h), h, future)
                p = jnp.exp(s - _lane_col(lblk, lane, h))
                dob = do_ref[h]
                delta = jnp.sum(dob.astype(F32) * o_ref[h].astype(F32), axis=1, keepdims=True)
                dp = lax.dot_general(dob, v_ref[h], (((1,), (1,)), ((), ())), preferred_element_type=F32)
                ds = p * (dp - delta)
                dv_ref[h] += lax.dot_general(p.astype(BF16), dob, (((0,), (0,)), ((), ())),
                                             preferred_element_type=F32)
                dk_ref[h] += lax.dot_general(ds.astype(BF16), q_ref[h], (((0,), (0,)), ((), ())),
                                             preferred_element_type=F32)
                dct_ref[h:h + 1, :] -= jnp.sum(ds, axis=0, keepdims=True)

    qspec = pl.BlockSpec((nh, tq, dh), lambda ki, qi: (0, jnp.maximum(qi, ki), 0))
    kspec = pl.BlockSpec((nh, tq, dh), lambda ki, qi: (0, ki, 0))
    lspec = pl.BlockSpec((tq, LANES), lambda ki, qi: (jnp.maximum(qi, ki), 0))
    ctspec = pl.BlockSpec((nh, tq), lambda ki, qi: (0, ki))
    return pl.pallas_call(
        body, name=name, grid=(nq, nq),
        in_specs=[qspec, kspec, kspec, lspec, ctspec, lspec, qspec, qspec],
        out_specs=[kspec, kspec, ctspec],
        out_shape=[jax.ShapeDtypeStruct((nh, t, dh), F32), jax.ShapeDtypeStruct((nh, t, dh), F32),
                   jax.ShapeDtypeStruct((nh, t), F32)],
        compiler_params=_params("parallel", "arbitrary"),
    )(q, k, v, c, ct, lse, o, do)


def _conv_fwd(proj, cw, w, bcol, name):
    t = proj.shape[0]
    tb = _pick(t, ROW_TILES)
    hb = tb // SUBLANES

    def body(b_ref, c_ref, h_ref, cp_ref, hp_ref, w_ref, y_ref):
        i = pl.program_id(0)
        zp = jnp.where(i > 0, cp_ref[...] * hp_ref[...], 0.0)
        zext = jnp.concatenate([zp, c_ref[...] * h_ref[...]], axis=0)
        z1 = pltpu.roll(zext, 1, 0)[SUBLANES:]
        z2 = pltpu.roll(zext, 2, 0)[SUBLANES:]
        y = w_ref[2:3, :] * zext[SUBLANES:] + w_ref[1:2, :] * z1 + w_ref[0:1, :] * z2
        y_ref[...] = (b_ref[...] * y).astype(BF16)

    cur = lambda j: pl.BlockSpec((tb, w), lambda i: (i, bcol + j))
    prev = lambda j: pl.BlockSpec((SUBLANES, w), lambda i: (jnp.maximum(i * hb - 1, 0), bcol + j))
    return pl.pallas_call(
        body, name=name, grid=(t // tb,),
        in_specs=[cur(0), cur(1), cur(2), prev(1), prev(2), pl.BlockSpec(cw.shape, lambda i: (0, 0))],
        out_specs=pl.BlockSpec((tb, w), lambda i: (i, 0)),
        out_shape=jax.ShapeDtypeStruct((t, w), BF16), compiler_params=_params("parallel"),
    )(proj, proj, proj, proj, proj, cw)


def _conv_bwd(proj, cw, dmix, w, bcol, name):
    t = proj.shape[0]
    tb = _pick(t, ROW_TILES)
    hb = tb // SUBLANES
    nb = t // tb
    n_ext = tb + SUBLANES

    def body(b_ref, c_ref, h_ref, cp_ref, hp_ref, bn_ref, d_ref, dn_ref, w_ref, db_ref, dc_ref, dh_ref, dw_ref):
        i = pl.program_id(0)
        c = c_ref[...]
        hh = h_ref[...]
        zp = jnp.where(i > 0, cp_ref[...] * hp_ref[...], 0.0)
        zext = jnp.concatenate([zp, c * hh], axis=0)
        z0 = zext[SUBLANES:]
        z1 = pltpu.roll(zext, 1, 0)[SUBLANES:]
        z2 = pltpu.roll(zext, 2, 0)[SUBLANES:]
        y = w_ref[2:3, :] * z0 + w_ref[1:2, :] * z1 + w_ref[0:1, :] * z2
        d = d_ref[...]
        db_ref[...] = d * y
        dy = d * b_ref[...]
        dyn = jnp.where(i < nb - 1, dn_ref[...] * bn_ref[...], 0.0)
        dext = jnp.concatenate([dy, dyn], axis=0)
        dy1 = pltpu.roll(dext, n_ext - 1, 0)[:tb]
        dy2 = pltpu.roll(dext, n_ext - 2, 0)[:tb]
        dz = w_ref[2:3, :] * dy + w_ref[1:2, :] * dy1 + w_ref[0:1, :] * dy2
        dc_ref[...] = dz * hh
        dh_ref[...] = dz * c

        @pl.when(i == 0)
        def _():
            dw_ref[...] = jnp.zeros_like(dw_ref)

        dw_ref[0:1, :] += jnp.sum(dy * z2, axis=0, keepdims=True)
        dw_ref[1:2, :] += jnp.sum(dy * z1, axis=0, keepdims=True)
        dw_ref[2:3, :] += jnp.sum(dy * z0, axis=0, keepdims=True)

    cur = lambda j: pl.BlockSpec((tb, w), lambda i: (i, bcol + j))
    prev = lambda j: pl.BlockSpec((SUBLANES, w), lambda i: (jnp.maximum(i * hb - 1, 0), bcol + j))
    nxt = lambda col: pl.BlockSpec((SUBLANES, w), lambda i: (jnp.minimum((i + 1) * hb, nb * hb - 1), col))
    out = pl.BlockSpec((tb, w), lambda i: (i, 0))
    return pl.pallas_call(
        body, name=name, grid=(nb,),
        in_specs=[cur(0), cur(1), cur(2), prev(1), prev(2), nxt(bcol),
                  pl.BlockSpec((tb, w), lambda i: (i, 1)), nxt(1), pl.BlockSpec(cw.shape, lambda i: (0, 0))],
        out_specs=[out, out, out, pl.BlockSpec(cw.shape, lambda i: (0, 0))],
        out_shape=[jax.ShapeDtypeStruct((t, w), F32)] * 3 + [jax.ShapeDtypeStruct(cw.shape, F32)],
        compiler_params=_params("arbitrary"),
    )(proj, proj, proj, proj, proj, proj, dmix, dmix, cw)


SQRT_HALF = 0.7071067811865476
INV_SQRT_2PI = 0.3989422804014327


def _gelu(x):
    return 0.5 * x * (1.0 + lax.erf(x * SQRT_HALF))


def _gelu_grad(x):
    return 0.5 * (1.0 + lax.erf(x * SQRT_HALF)) + x * (INV_SQRT_2PI * jnp.exp(-0.5 * x * x))


def _sgu_fwd(uv, ln_g, ln_b, wm, bs_full, name):
    t, d2 = uv.shape
    d = d2 // 2
    ng, pb, _ = wm.shape
    gd = d // ng
    tb = _pick(t, ROW_TILES[1:] or ROW_TILES)
    assert tb % pb == 0

    def body(uv_ref, g_ref, b_ref, w_ref, bs_ref, o_ref):
        u = _gelu(uv_ref[:, :d])
        v = _gelu(uv_ref[:, d:])
        mu = jnp.mean(v, axis=-1, keepdims=True)
        vc = v - mu
        var = jnp.mean(vc * vc, axis=-1, keepdims=True)
        vn = (vc * lax.rsqrt(var + LN_EPS) * g_ref[...] + b_ref[...]).astype(BF16)
        for r in range(tb // pb):
            rows = slice(r * pb, (r + 1) * pb)
            for gi in range(ng):
                cols = slice(gi * gd, (gi + 1) * gd)
                s = jnp.dot(w_ref[gi], vn[rows, cols], preferred_element_type=F32) + bs_ref[:, cols]
                o_ref[rows, cols] = (u[rows, cols] * s).astype(BF16)

    vec = pl.BlockSpec((1, d), lambda i: (0, 0))
    return pl.pallas_call(
        body, name=name, grid=(t // tb,),
        in_specs=[pl.BlockSpec((tb, d2), lambda i: (i, 0)), vec, vec,
                  pl.BlockSpec(wm.shape, lambda i: (0, 0, 0)), pl.BlockSpec((pb, d), lambda i: (0, 0))],
        out_specs=pl.BlockSpec((tb, d), lambda i: (i, 0)),
        out_shape=jax.ShapeDtypeStruct((t, d), BF16), compiler_params=_params("parallel"),
    )(uv, ln_g.reshape(1, d), ln_b.reshape(1, d), wm, bs_full)


def _sgu_bwd(uv, ln_g, ln_b, wm, bs_full, dgated, name):
    t, d2 = uv.shape
    d = d2 // 2
    ng, pb, _ = wm.shape
    gd = d // ng
    tb = _pick(t, ROW_TILES[1:] or ROW_TILES)
    nb = t // tb

    def body(uv_ref, g_ref, b_ref, w_ref, bs_ref, dg_ref, o_ref, dw_ref, dbs_ref, dlg_ref, dlb_ref,
             du_s, dvn_s, dbs_s):
        i = pl.program_id(0)

        @pl.when(i == 0)
        def _():
            dw_ref[...] = jnp.zeros_like(dw_ref)
            dbs_s[...] = jnp.zeros_like(dbs_s)
            dlg_ref[...] = jnp.zeros_like(dlg_ref)
            dlb_ref[...] = jnp.zeros_like(dlb_ref)

        upre = uv_ref[:, :d]
        vpre = uv_ref[:, d:]
        u = _gelu(upre)
        v = _gelu(vpre)
        mu = jnp.mean(v, axis=-1, keepdims=True)
        vc = v - mu
        var = jnp.mean(vc * vc, axis=-1, keepdims=True)
        rstd = lax.rsqrt(var + LN_EPS)
        xhat = vc * rstd
        vn = (xhat * g_ref[...] + b_ref[...]).astype(BF16)
        dgt = dg_ref[...].astype(F32)
        for r in range(tb // pb):
            rows = slice(r * pb, (r + 1) * pb)
            for gi in range(ng):
                cols = slice(gi * gd, (gi + 1) * gd)
                vblk = vn[rows, cols]
                s = jnp.dot(w_ref[gi], vblk, preferred_element_type=F32) + bs_ref[:, cols]
                dblk = dgt[rows, cols]
                du_s[rows, cols] = dblk * s
                ds = dblk * u[rows, cols]
                dsb = ds.astype(BF16)
                dvn_s[rows, cols] = lax.dot_general(w_ref[gi], dsb, (((0,), (0,)), ((), ())),
                                                    preferred_element_type=F32)
                dw_ref[gi] += lax.dot_general(dsb, vblk, (((1,), (1,)), ((), ())), preferred_element_type=F32)
                dbs_s[:, cols] += ds
        dvn = dvn_s[...]
        dlg_ref[...] += jnp.sum(dvn * xhat, axis=0, keepdims=True)
        dlb_ref[...] += jnp.sum(dvn, axis=0, keepdims=True)
        dxh = dvn * g_ref[...]
        m1 = jnp.mean(dxh, axis=-1, keepdims=True)
        m2 = jnp.mean(dxh * xhat, axis=-1, keepdims=True)
        dv = rstd * (dxh - m1 - xhat * m2)
        o_ref[:, :d] = (du_s[...] * _gelu_grad(upre)).astype(BF16)
        o_ref[:, d:] = (dv * _gelu_grad(vpre)).astype(BF16)

        @pl.when(i == nb - 1)
        def _():
            lane = lax.broadcasted_iota(jnp.int32, (pb, LANES), 1)
            acc = jnp.zeros((pb, LANES), F32)
            for gi in range(ng):
                col = jnp.sum(dbs_s[:, gi * gd:(gi + 1) * gd], axis=1, keepdims=True)
                acc = acc + jnp.where(lane == gi, col, 0.0)
            dbs_ref[...] = acc

    vec = pl.BlockSpec((1, d), lambda i: (0, 0))
    duv, dw, dbs, dlg, dlb = pl.pallas_call(
        body, name=name, grid=(nb,),
        in_specs=[pl.BlockSpec((tb, d2), lambda i: (i, 0)), vec, vec,
                  pl.BlockSpec(wm.shape, lambda i: (0, 0, 0)), pl.BlockSpec((pb, d), lambda i: (0, 0)),
                  pl.BlockSpec((tb, d), lambda i: (i, 0))],
        out_specs=[pl.BlockSpec((tb, d2), lambda i: (i, 0)), pl.BlockSpec(wm.shape, lambda i: (0, 0, 0)),
                   pl.BlockSpec((pb, LANES), lambda i: (0, 0)), vec, vec],
        out_shape=[jax.ShapeDtypeStruct((t, d2), BF16), jax.ShapeDtypeStruct(wm.shape, F32),
                   jax.ShapeDtypeStruct((pb, LANES), F32), jax.ShapeDtypeStruct((1, d), F32),
                   jax.ShapeDtypeStruct((1, d), F32)],
        scratch_shapes=[pltpu.VMEM((tb, d), F32), pltpu.VMEM((tb, d), F32), pltpu.VMEM((pb, d), F32)],
        compiler_params=_params("arbitrary"),
    )(uv, ln_g.reshape(1, d), ln_b.reshape(1, d), wm, bs_full, dgated)
    return duv, dw, dbs, dlg[0], dlb[0]


def _adamw(w, g, m, v, name):
    shape = w.shape
    cols = shape[-1]
    rows = w.size // cols
    tr = _pick(rows, (512, 256, 352, 128, 64, 32, 16, 8))

    def body(w_ref, g_ref, m_ref, v_ref, d_ref, mo_ref, vo_ref):
        gg = g_ref[...]
        mn = ADAM_B1 * m_ref[...] + (1.0 - ADAM_B1) * gg
        vn = ADAM_B2 * v_ref[...] + (1.0 - ADAM_B2) * (gg * gg)
        m_hat = mn / (1.0 - ADAM_B1 ** ADAM_STEP)
        v_hat = vn / (1.0 - ADAM_B2 ** ADAM_STEP)
        d_ref[...] = -ADAM_LR * (m_hat / (jnp.sqrt(v_hat) + ADAM_EPS) + ADAM_WD * w_ref[...])
        mo_ref[...] = mn
        vo_ref[...] = vn

    spec = pl.BlockSpec((tr, cols), lambda i: (i, 0))
    outs = pl.pallas_call(
        body, name=name, grid=(rows // tr,),
        in_specs=[spec] * 4, out_specs=[spec] * 3,
        out_shape=[jax.ShapeDtypeStruct((rows, cols), F32)] * 3,
        compiler_params=_params("parallel"),
    )(*[a.reshape(rows, cols) for a in (w, g, m, v)])
    return [o.reshape(shape) for o in outs]


ANY = pl.BlockSpec(memory_space=pl.ANY)


def _place():
    return lax.axis_index("x"), lax.axis_index("y"), lax.axis_index("c")


def _all_gather(xs, name):
    r, cc = xs.shape

    def body(x_ref, out_ref, send_sems, recv_sems, local_sem):
        x, y, c = _place()
        me, sibling = (x, y, c), (x, y, 1 - c)
        chips = [(1 - x, y), (x, 1 - y), (1 - x, 1 - y)]

        def rows(px, py, pc):
            return out_ref.at[4 * px + 2 * py + pc]

        def copy(k, block, to, src=None):
            return pltpu.make_async_remote_copy(
                src_ref=rows(*block) if src is None else src, dst_ref=rows(*block),
                send_sem=send_sems.at[k], recv_sem=recv_sems.at[k], device_id=to, device_id_type=MESH)

        mine = pltpu.make_async_copy(x_ref, rows(*me), local_sem)
        mine.start()
        first = [copy(0, me, sibling, src=x_ref)]
        first += [copy(1 + j, me, (*chip, c), src=x_ref) for j, chip in enumerate(chips)]
        for cp in first:
            cp.start()
        passed = [copy(4 + j, (*chip, c), sibling) for j, chip in enumerate(chips)]
        for j, chip in enumerate(chips):
            copy(1 + j, (*chip, c), me).wait_recv()
            passed[j].start()
        copy(0, sibling, me).wait_recv()
        for j, chip in enumerate(chips):
            copy(4 + j, (*chip, 1 - c), me).wait_recv()
        for cp in first + passed:
            cp.wait_send()
        mine.wait()

    return pl.pallas_call(
        body, name=name, in_specs=[ANY], out_specs=ANY,
        out_shape=jax.ShapeDtypeStruct((N_DEV, r, cc), xs.dtype),
        scratch_shapes=[pltpu.SemaphoreType.DMA((7,)), pltpu.SemaphoreType.DMA((7,)), pltpu.SemaphoreType.DMA],
    )(xs)


def _rs_sibling_exchange(packed, name):
    _, r, cc = packed.shape

    def body(p_ref, r_ref, send_sems, recv_sems):
        x, y, c = _place()
        cps = []
        for j in range(4):
            cps.append(pltpu.make_async_remote_copy(
                src_ref=p_ref.at[2 * j + (1 - c)], dst_ref=r_ref.at[j],
                send_sem=send_sems.at[j], recv_sem=recv_sems.at[j],
                device_id=(x, y, 1 - c), device_id_type=MESH))
        for cp in cps:
            cp.start()
        for cp in cps:
            cp.wait()

    return pl.pallas_call(
        body, name=name, in_specs=[ANY], out_specs=ANY,
        out_shape=jax.ShapeDtypeStruct((4, r, cc), packed.dtype),
        scratch_shapes=[pltpu.SemaphoreType.DMA((4,)), pltpu.SemaphoreType.DMA((4,))],
    )(packed)


def _rs_chip_sum(packed, from_sibling, c_idx, name):
    _, r, cc = packed.shape
    tr = _pick(r, (512, 256, 128))

    def body(c_ref, a_ref, b_ref, o_ref):
        o_ref[...] = (a_ref[...].astype(F32) + b_ref[...].astype(F32)).astype(o_ref.dtype)

    return pl.pallas_call(
        body, name=name,
        grid_spec=pltpu.PrefetchScalarGridSpec(
            num_scalar_prefetch=1, grid=(4, r // tr),
            in_specs=[pl.BlockSpec((None, tr, cc), lambda j, i, c_ref: (2 * j + c_ref[0], i, 0)),
                      pl.BlockSpec((None, tr, cc), lambda j, i, c_ref: (j, i, 0))],
            out_specs=pl.BlockSpec((None, tr, cc), lambda j, i, c_ref: (j, i, 0))),
        out_shape=jax.ShapeDtypeStruct((4, r, cc), packed.dtype),
        compiler_params=_params("parallel", "parallel"),
    )(c_idx, packed, from_sibling)


def _rs_chip_exchange(partial, name):
    _, r, cc = partial.shape

    def body(p_ref, r_ref, send_sems, recv_sems):
        x, y, c = _place()
        chips = [(1 - x, y), (x, 1 - y), (1 - x, 1 - y)]
        cps = []
        for k, (tx, ty) in enumerate(chips):
            cps.append(pltpu.make_async_remote_copy(
                src_ref=p_ref.at[2 * tx + ty], dst_ref=r_ref.at[k],
                send_sem=send_sems.at[k], recv_sem=recv_sems.at[k],
                device_id=(tx, ty, c), device_id_type=MESH))
        for cp in cps:
            cp.start()
        for cp in cps:
            cp.wait()

    return pl.pallas_call(
        body, name=name, in_specs=[ANY], out_specs=ANY,
        out_shape=jax.ShapeDtypeStruct((3, r, cc), partial.dtype),
        scratch_shapes=[pltpu.SemaphoreType.DMA((3,)), pltpu.SemaphoreType.DMA((3,))],
    )(partial)


def _rs_final_sum(partial, received, chip_idx, name):
    _, r, cc = partial.shape
    tr = _pick(r, (512, 256, 128))

    def body(c_ref, a_ref, r_ref, o_ref):
        acc = a_ref[...].astype(F32)
        for k in range(3):
            acc = acc + r_ref[k].astype(F32)
        o_ref[...] = acc

    return pl.pallas_call(
        body, name=name,
        grid_spec=pltpu.PrefetchScalarGridSpec(
            num_scalar_prefetch=1, grid=(r // tr,),
            in_specs=[pl.BlockSpec((None, tr, cc), lambda i, c_ref: (c_ref[0], i, 0)),
                      pl.BlockSpec((3, tr, cc), lambda i, c_ref: (0, i, 0))],
            out_specs=pl.BlockSpec((tr, cc), lambda i, c_ref: (i, 0))),
        out_shape=jax.ShapeDtypeStruct((r, cc), F32),
        compiler_params=_params("parallel"),
    )(chip_idx, partial, received)


def _all_reduce_small(vals, name):
    r, cc = vals.shape

    def body(v_ref, o_ref, buf, send_sems, recv_sems):
        x, y, c = _place()
        me = 4 * x + 2 * y + c
        buf[0] = v_ref[...]
        cps = []
        for k in range(1, N_DEV):
            kx, ky, kc = (k >> 2) & 1, (k >> 1) & 1, k & 1
            peer = (1 - x if kx else x, 1 - y if ky else y, 1 - c if kc else c)
            cps.append(pltpu.make_async_remote_copy(
                src_ref=buf.at[0], dst_ref=buf.at[k], send_sem=send_sems.at[k - 1],
                recv_sem=recv_sems.at[k - 1], device_id=peer, device_id_type=MESH))
        for cp in cps:
            cp.start()
        for cp in cps:
            cp.wait()
        acc = buf[jnp.bitwise_xor(me, 0)]
        for dev in range(1, N_DEV):
            acc = acc + buf[jnp.bitwise_xor(me, dev)]
        o_ref[...] = acc

    vm = pl.BlockSpec(memory_space=pltpu.VMEM)
    return pl.pallas_call(
        body, name=name, in_specs=[vm], out_specs=vm,
        out_shape=jax.ShapeDtypeStruct((r, cc), F32),
        scratch_shapes=[pltpu.VMEM((N_DEV, r, cc), F32), pltpu.SemaphoreType.DMA((7,)),
                        pltpu.SemaphoreType.DMA((7,))],
        compiler_params=pltpu.CompilerParams(vmem_limit_bytes=VMEM_LIMIT),
    )(vals)


def _flat_pad(parts, mult):
    flat = jnp.concatenate([p.reshape(-1) for p in parts])
    pad = (-flat.shape[0]) % mult
    return jnp.pad(flat, (0, pad))


def _to_shards(full, axis):
    if axis == 0:
        return full.reshape(N_DEV, -1)
    r, n = full.shape
    return full.reshape(r, N_DEV, n // N_DEV).transpose(1, 0, 2).reshape(N_DEV, -1)


def _from_shards(sh, axis, shape):
    if axis == 0:
        return sh.reshape(shape)
    r, n = shape
    return sh.reshape(N_DEV, r, n // N_DEV).transpose(1, 0, 2).reshape(r, n)


def _lanes(flat):
    pad = (-flat.shape[0]) % (SUBLANES * LANES)
    return jnp.pad(flat, (0, pad)).reshape(-1, LANES)


def kernel(x, even_w_in, even_b_f, even_conv_w, even_w_out, odd_w_in, odd_v_ln_g, odd_v_ln_b, odd_w_s, odd_b_s, odd_w_out, mix_ln_g, mix_ln_b, ffn_w_in, ffn_w_out, ffn_ln_g, ffn_ln_b, loss_target, m_even_w_in, m_even_b_f, m_even_conv_w, m_even_w_out, m_odd_w_in, m_odd_v_ln_g, m_odd_v_ln_b, m_odd_w_s, m_odd_b_s, m_odd_w_out, m_mix_ln_g, m_mix_ln_b, m_ffn_w_in, m_ffn_w_out, m_ffn_ln_g, m_ffn_ln_b, v_even_w_in, v_even_b_f, v_even_conv_w, v_even_w_out, v_odd_w_in, v_odd_v_ln_g, v_odd_v_ln_b, v_odd_w_s, v_odd_b_s, v_odd_w_out, v_mix_ln_g, v_mix_ln_b, v_ffn_w_in, v_ffn_w_out, v_ffn_ln_g, v_ffn_ln_b):
    t, d = x.shape[1], x.shape[2]
    nh = even_b_f.shape[-1]
    w = even_conv_w.shape[-1] * N_DEV
    dh = w // nh
    scale = dh ** -0.5
    e_in = even_w_in.shape[-1] * N_DEV
    f2 = ffn_w_in.shape[-1] * N_DEV
    f = f2 // 2
    ng, pb = odd_w_s.shape[1], odd_w_s.shape[2]
    assert e_in == 6 * w + nh and nh <= SUBLANES and (6 * w) % LANES == 0 and d % N_DEV == 0
    mx, my, mc = _place()
    me = 4 * mx + 2 * my + mc

    big = [(even_w_in[0], 1, (d, e_in)), (even_w_out[0], 0, (2 * w, d)), (odd_w_in[0], 1, (d, 2 * d)),
           (odd_w_out[0], 0, (d, d)),
           (ffn_w_in[0], 1, (d, f2)), (ffn_w_in[1], 1, (d, f2)),
           (ffn_w_out[0], 0, (f, d)), (ffn_w_out[1], 0, (f, d))]
    sizes = [s.size for s, _, _ in big]
    mine = _flat_pad([s for s, _, _ in big], PACK_ROWS * PACK_COLS).astype(BF16).reshape(-1, PACK_COLS)
    gathered = _all_gather(mine, "ag_weights").reshape(N_DEV, -1)
    full_w = []
    off = 0
    for (s, axis, shape), n in zip(big, sizes):
        full_w.append(_from_shards(gathered[:, off:off + n], axis, shape))
        off += n
    w_in0, w_out0, w_in1, w_out1, w_fi0, w_fi1, w_fo0, w_fo1 = full_w
    w_all0 = jnp.concatenate([w_in0[:, :3 * w], w_in0[:, 3 * w + nh:], w_in0[:, 3 * w:3 * w + nh],
                              jnp.zeros((d, LANES - nh), BF16)], axis=1)
    e_pad = w_all0.shape[1]

    cs, vs = even_conv_w.shape[-1], odd_v_ln_g.shape[-1]
    small_mine = jnp.concatenate([
        lax.dynamic_update_slice(jnp.zeros((3, w), F32), even_conv_w[0], (0, me * cs)).reshape(-1),
        lax.dynamic_update_slice(jnp.zeros((d,), F32), odd_v_ln_g[0], (me * vs,)),
        lax.dynamic_update_slice(jnp.zeros((d,), F32), odd_v_ln_b[0], (me * vs,))])
    small_all = _all_reduce_small(_lanes(small_mine), "ag_small").reshape(-1)
    conv_w = small_all[:3 * w].reshape(3, w)
    vln_g = small_all[3 * w:3 * w + d]
    vln_b = small_all[3 * w + d:3 * w + 2 * d]

    bf_pad = jnp.pad(even_b_f[0], (0, LANES - nh)).reshape(1, LANES)
    chunk = jnp.arange(pb) // (pb // 2)
    ws_mask = (chunk[None, :] <= chunk[:, None])[None]
    wm = jnp.where(ws_mask, odd_w_s[0], 0.0).astype(BF16)
    bs_full = jnp.repeat(odd_b_s[0].T, d // ng, axis=1)

    x0 = x[0]
    tgt = loss_target[0]
    fcol = 6 * w // LANES
    p0 = _mm(x0, w_all0, "nn", F32, "l0_in_proj")
    cgate = _fgate_fwd(p0, bf_pad, fcol, nh, "l0_fgate")
    ct = cgate[:, :nh].T
    heads = lambda a: a.reshape(t, nh, dh).transpose(1, 0, 2)
    unheads = lambda a: a.transpose(1, 0, 2).reshape(t, w)
    qh = heads((p0[:, :w] * scale).astype(BF16))
    kh = heads(p0[:, w:2 * w].astype(BF16))
    vh = heads(p0[:, 2 * w:3 * w].astype(BF16))
    oh, lse = _attn_fwd(qh, kh, vh, cgate, ct, "l0_attn")
    yconv = _conv_fwd(p0, conv_w, w, 3, "l0_conv")
    mix = jnp.concatenate([unheads(oh).astype(BF16), yconv], axis=1)
    m0 = _mm(mix, w_out0, "nn", F32, "l0_out_proj")
    x1, x1b = _ln_fwd(x0, m0, mix_ln_g[0], mix_ln_b[0], "l0_mix_ln")
    gu0 = _mm(x1b, w_fi0, "nn", F32, "l0_ffn_in")
    h0 = _swiglu_fwd(gu0, "l0_swiglu")
    f0 = _mm(h0, w_fo0, "nn", F32, "l0_ffn_out")
    x2, x2b = _ln_fwd(x1, f0, ffn_ln_g[0], ffn_ln_b[0], "l0_ffn_ln")

    uv = _mm(x2b, w_in1, "nn", F32, "l1_in_proj")
    gated = _sgu_fwd(uv, vln_g, vln_b, wm, bs_full, "l1_sgu")
    m1 = _mm(gated, w_out1, "nn", F32, "l1_out_proj")
    x3, x3b = _ln_fwd(x2, m1, mix_ln_g[1], mix_ln_b[1], "l1_mix_ln")
    gu1 = _mm(x3b, w_fi1, "nn", F32, "l1_ffn_in")
    h1 = _swiglu_fwd(gu1, "l1_swiglu")
    f1 = _mm(h1, w_fo1, "nn", F32, "l1_ffn_out")
    x4, _ = _ln_fwd(x3, f1, ffn_ln_g[1], ffn_ln_b[1], "l1_ffn_ln")
    loss_part, dy4 = _loss(x4, tgt, "loss")

    dz4, g_ffn_g1, g_ffn_b1 = _ln_bwd(x3, f1, ffn_ln_g[1], dy4, 1.0, None, "l1_ffn_ln_bwd")
    g_fo1 = _mm(h1, dz4, "tn", F32, "l1_ffn_out_dw")
    dh1 = _mm(dz4, w_fo1, "nt", BF16, "l1_ffn_out_dx")
    dgu1 = _swiglu_bwd(gu1, dh1, "l1_swiglu_bwd")
    g_fi1 = _mm(x3b, dgu1, "tn", F32, "l1_ffn_in_dw")
    dx3 = _mm(dgu1, w_fi1, "nt", F32, "l1_ffn_in_dx")
    dz3, g_mix_g1, g_mix_b1 = _ln_bwd(x2, m1, mix_ln_g[1], dz4, ALPHA, dx3, "l1_mix_ln_bwd")
    g_out1 = _mm(gated, dz3, "tn", F32, "l1_out_proj_dw")
    dgated = _mm(dz3, w_out1, "nt", BF16, "l1_out_proj_dx")
    duv, g_wm, g_bs_t, g_vln_g, g_vln_b = _sgu_bwd(uv, vln_g, vln_b, wm, bs_full, dgated, "l1_sgu_bwd")
    g_in1 = _mm(x2b, duv, "tn", F32, "l1_in_proj_dw")
    dx2 = _mm(duv, w_in1, "nt", F32, "l1_in_proj_dx")

    dz2, g_ffn_g0, g_ffn_b0 = _ln_bwd(x1, f0, ffn_ln_g[0], dz3, ALPHA, dx2, "l0_ffn_ln_bwd")
    g_fo0 = _mm(h0, dz2, "tn", F32, "l0_ffn_out_dw")
    dh0 = _mm(dz2, w_fo0, "nt", BF16, "l0_ffn_out_dx")
    dgu0 = _swiglu_bwd(gu0, dh0, "l0_swiglu_bwd")
    g_fi0 = _mm(x1b, dgu0, "tn", F32, "l0_ffn_in_dw")
    dx1 = _mm(dgu0, w_fi0, "nt", F32, "l0_ffn_in_dx")
    dz1, g_mix_g0, g_mix_b0 = _ln_bwd(x0, m0, mix_ln_g[0], dz2, ALPHA, dx1, "l0_mix_ln_bwd")
    g_out0 = _mm(mix, dz1, "tn", F32, "l0_out_proj_dw")
    dmix = _mm(dz1, w_out0, "nt", F32, "l0_out_proj_dx")
    d_b, d_c, d_h, g_conv = _conv_bwd(p0, conv_w, dmix, w, 3, "l0_conv_bwd")
    doh = heads(dmix[:, :w].astype(BF16))
    dqh, dcq = _attn_dq(qh, kh, vh, cgate, ct, lse, oh, doh, scale, "l0_attn_dq")
    dkh, dvh, dct = _attn_dkv(qh, kh, vh, cgate, ct, lse, oh, doh, "l0_attn_dkv")
    dck = jnp.pad(dct.T, ((0, 0), (0, LANES - nh)))
    dzf, g_bf = _fgate_bwd(p0, bf_pad, dcq, dck, fcol, nh, "l0_fgate_bwd")
    dp0 = jnp.concatenate([unheads(dqh), unheads(dkh), unheads(dvh), d_b, d_c, d_h, dzf], axis=1).astype(BF16)
    g_all0 = _mm(x0, dp0, "tn", F32, "l0_in_proj_dw")
    dx0 = _mm(dp0, w_all0, "nt", F32, "l0_in_proj_dx")
    grad_x = _axpy(ALPHA, dz1, dx0, "grad_x")
    g_in0 = jnp.concatenate([g_all0[:, :3 * w], g_all0[:, 6 * w:6 * w + nh], g_all0[:, 3 * w:6 * w]], axis=1)
    del e_pad

    big_g = [g_in0, g_out0, g_in1, g_out1, g_fi0, g_fi1, g_fo0, g_fo1]
    packed = jnp.concatenate([_to_shards(g, axis) for g, (_, axis, _) in zip(big_g, big)], axis=1)
    pad = mine.size - packed.shape[1]
    packed = jnp.pad(packed, ((0, 0), (0, pad))).astype(BF16).reshape(N_DEV, -1, PACK_COLS)
    from_sib = _rs_sibling_exchange(packed, "rs_sibling")
    partial = _rs_chip_sum(packed, from_sib, mc.reshape(1).astype(jnp.int32), "rs_chip_sum")
    received = _rs_chip_exchange(partial, "rs_chips")
    g_flat = _rs_final_sum(partial, received, (2 * mx + my).reshape(1).astype(jnp.int32), "rs_final").reshape(-1)
    g_sh = []
    off = 0
    for (s, _, _), n in zip(big, sizes):
        g_sh.append(g_flat[off:off + n].reshape(s.shape))
        off += n
    gs_in0, gs_out0, gs_in1, gs_out1, gs_fi0, gs_fi1, gs_fo0, gs_fo1 = g_sh

    g_ws = jnp.where(ws_mask, g_wm, 0.0)
    g_bs = g_bs_t[:, :ng].T
    small_g = [g_bf[:nh], g_conv, g_vln_g, g_vln_b, g_ws, g_bs,
               jnp.stack([g_mix_g0, g_mix_g1]), jnp.stack([g_mix_b0, g_mix_b1]),
               jnp.stack([g_ffn_g0, g_ffn_g1]), jnp.stack([g_ffn_b0, g_ffn_b1])]
    small_sum = _all_reduce_small(_lanes(jnp.concatenate([a.reshape(-1) for a in small_g])), "ar_small_grads")
    small_sum = small_sum.reshape(-1)
    outs_small = []
    off = 0
    for a in small_g:
        outs_small.append(small_sum[off:off + a.size].reshape(a.shape))
        off += a.size
    gr_bf, gr_conv, gr_vg, gr_vb, gr_ws, gr_bs, gr_mg, gr_mb, gr_fg, gr_fb = outs_small

    loss = lax.psum(loss_part, ("x", "y", "c"))

    grads = {
        "even_w_in": gs_in0[None], "even_b_f": gr_bf[None],
        "even_conv_w": lax.dynamic_slice(gr_conv, (0, me * cs), (3, cs))[None],
        "even_w_out": gs_out0[None], "odd_w_in": gs_in1[None],
        "odd_v_ln_g": lax.dynamic_slice(gr_vg, (me * vs,), (vs,))[None],
        "odd_v_ln_b": lax.dynamic_slice(gr_vb, (me * vs,), (vs,))[None],
        "odd_w_s": gr_ws[None], "odd_b_s": gr_bs[None], "odd_w_out": gs_out1[None],
        "mix_ln_g": gr_mg, "mix_ln_b": gr_mb,
        "ffn_w_in": jnp.stack([gs_fi0, gs_fi1]), "ffn_w_out": jnp.stack([gs_fo0, gs_fo1]),
        "ffn_ln_g": gr_fg, "ffn_ln_b": gr_fb,
    }
    weights = dict(even_w_in=even_w_in, even_b_f=even_b_f, even_conv_w=even_conv_w, even_w_out=even_w_out,
                   odd_w_in=odd_w_in, odd_v_ln_g=odd_v_ln_g, odd_v_ln_b=odd_v_ln_b, odd_w_s=odd_w_s,
                   odd_b_s=odd_b_s, odd_w_out=odd_w_out, mix_ln_g=mix_ln_g, mix_ln_b=mix_ln_b,
                   ffn_w_in=ffn_w_in, ffn_w_out=ffn_w_out, ffn_ln_g=ffn_ln_g, ffn_ln_b=ffn_ln_b)
    moms = dict(even_w_in=(m_even_w_in, v_even_w_in), even_b_f=(m_even_b_f, v_even_b_f),
                even_conv_w=(m_even_conv_w, v_even_conv_w), even_w_out=(m_even_w_out, v_even_w_out),
                odd_w_in=(m_odd_w_in, v_odd_w_in), odd_v_ln_g=(m_odd_v_ln_g, v_odd_v_ln_g),
                odd_v_ln_b=(m_odd_v_ln_b, v_odd_v_ln_b), odd_w_s=(m_odd_w_s, v_odd_w_s),
                odd_b_s=(m_odd_b_s, v_odd_b_s), odd_w_out=(m_odd_w_out, v_odd_w_out),
                mix_ln_g=(m_mix_ln_g, v_mix_ln_g), mix_ln_b=(m_mix_ln_b, v_mix_ln_b),
                ffn_w_in=(m_ffn_w_in, v_ffn_w_in), ffn_w_out=(m_ffn_w_out, v_ffn_w_out),
                ffn_ln_g=(m_ffn_ln_g, v_ffn_ln_g), ffn_ln_b=(m_ffn_ln_b, v_ffn_ln_b))
    names = list(weights)
    deltas, new_m, new_v = [], [], []
    for n in names:
        dlt, mn, vn = _adamw(weights[n], grads[n], moms[n][0], moms[n][1], "adamw_" + n)
        deltas.append(dlt)
        new_m.append(mn)
        new_v.append(vn)
    return (loss, grad_x[None], *[grads[n] for n in names], *deltas, *new_m, *new_v)
```

```python
import jax
import jax.numpy as jnp
from jax import lax
from jax.experimental import pallas as pl
from jax.experimental.pallas import tpu as pltpu

F32 = jnp.float32
BF16 = jnp.bfloat16
MESH = pl.DeviceIdType.MESH

DEPTH = 2
ALPHA = (2.0 * DEPTH) ** 0.25
LN_EPS = 1e-5
ADAM_LR = 0.001
ADAM_B1 = 0.9
ADAM_B2 = 0.999
ADAM_EPS = 1e-08
ADAM_WD = 0.01
ADAM_STEP = 10

N_DEV = 8
LANES = 128
SUBLANES = 8
VMEM_LIMIT = 48 * 1024 * 1024
NEG_BIG = -1e30
ROW_TILES = (512, 256, 128)


def _pick(n, cands):
    for c in cands:
        if c <= n and n % c == 0:
            return c
    return n


def _params(*sem):
    return pltpu.CompilerParams(dimension_semantics=sem, vmem_limit_bytes=VMEM_LIMIT)


NN = (((1,), (0,)), ((), ()))
NT = (((1,), (1,)), ((), ()))
TN = (((0,), (0,)), ((), ()))
M_TILES = (1024, 512, 1408, 256, 128)
N_TILES = (512, 640, 256, 128)
K_TILES = (1024, 512, 640, 1408, 256, 128)


def _mm_core(name, grid, a, b, a_spec, b_spec, o_spec, o_shape, o_dtype, dims, tile):
    nred = grid[2]

    def body(a_ref, b_ref, o_ref, *acc):
        part = lax.dot_general(a_ref[...].astype(BF16), b_ref[...].astype(BF16), dims, preferred_element_type=F32)
        if nred == 1:
            o_ref[...] = part.astype(o_ref.dtype)
            return
        acc_ref, = acc
        kk = pl.program_id(2)

        @pl.when(kk == 0)
        def _():
            acc_ref[...] = jnp.zeros_like(acc_ref)

        acc_ref[...] += part

        @pl.when(kk == nred - 1)
        def _():
            o_ref[...] = acc_ref[...].astype(o_ref.dtype)

    return pl.pallas_call(
        body, name=name, grid=grid, in_specs=[a_spec, b_spec], out_specs=o_spec,
        out_shape=jax.ShapeDtypeStruct(o_shape, o_dtype),
        scratch_shapes=[] if nred == 1 else [pltpu.VMEM(tile, F32)],
        compiler_params=_params("parallel", "parallel", "arbitrary"),
    )(a, b)


def _mm(a, b, mode, out_dtype, name):
    if mode == "nn":
        (m, k), (k2, n) = a.shape, b.shape
    elif mode == "nt":
        (m, k), (n, k2) = a.shape, b.shape
    else:
        (k, m), (k2, n) = a.shape, b.shape
    assert k == k2, (a.shape, b.shape, mode)
    tm, tn, tk = _pick(m, M_TILES), _pick(n, N_TILES), _pick(k, K_TILES)
    if mode == "nn":
        a_spec = pl.BlockSpec((tm, tk), lambda i, j, kk: (i, kk))
        b_spec = pl.BlockSpec((tk, tn), lambda i, j, kk: (kk, j))
        dims = NN
    elif mode == "nt":
        a_spec = pl.BlockSpec((tm, tk), lambda i, j, kk: (i, kk))
        b_spec = pl.BlockSpec((tn, tk), lambda i, j, kk: (j, kk))
        dims = NT
    else:
        a_spec = pl.BlockSpec((tk, tm), lambda i, j, kk: (kk, i))
        b_spec = pl.BlockSpec((tk, tn), lambda i, j, kk: (kk, j))
        dims = TN
    return _mm_core(name, (m // tm, n // tn, k // tk), a, b, a_spec, b_spec,
                    pl.BlockSpec((tm, tn), lambda i, j, kk: (i, j)), (m, n), out_dtype, dims, (tm, tn))


def _act_spec(blocked, rows, ns, row_ax, d_ax):
    if blocked:
        return pl.BlockSpec((None, rows, ns), lambda *g: (g[d_ax], g[row_ax], 0))
    return pl.BlockSpec((rows, ns), lambda *g: (g[row_ax], g[d_ax]))


def _mm_cols_fwd(a, g3, blocked, out_dtype, name):
    (t, k), (nd, k2, ns) = a.shape, g3.shape
    assert k == k2
    tm, tk = _pick(t, M_TILES), _pick(k, K_TILES)
    return _mm_core(name, (t // tm, nd, k // tk), a, g3,
                    pl.BlockSpec((tm, tk), lambda i, d, kk: (i, kk)),
                    pl.BlockSpec((None, tk, ns), lambda i, d, kk: (d, kk, 0)),
                    _act_spec(blocked, tm, ns, 0, 1), (nd, t, ns) if blocked else (t, nd * ns), out_dtype, NN, (tm, ns))


def _mm_cols_dx(dy, g3, blocked, out_dtype, name):
    nd, k, ns = g3.shape
    t = dy.shape[1] if blocked else dy.shape[0]
    tm, tn = _pick(t, M_TILES), _pick(k, (1024,) + N_TILES)
    return _mm_core(name, (t // tm, k // tn, nd), dy, g3,
                    _act_spec(blocked, tm, ns, 0, 2),
                    pl.BlockSpec((None, tn, ns), lambda i, j, d: (d, j, 0)),
                    pl.BlockSpec((tm, tn), lambda i, j, d: (i, j)), (t, k), out_dtype, NT, (tm, tn))


def _mm_cols_dw(a, dy, nd, blocked, out_dtype, name):
    t, k = a.shape
    ns = dy.shape[2] if blocked else dy.shape[1] // nd
    tmk, tk = _pick(k, M_TILES), _pick(t, K_TILES)
    return _mm_core(name, (nd, k // tmk, t // tk), a, dy,
                    pl.BlockSpec((tk, tmk), lambda d, j, kk: (kk, j)),
                    _act_spec(blocked, tk, ns, 2, 0),
                    pl.BlockSpec((None, tmk, ns), lambda d, j, kk: (d, j, 0)), (nd, k, ns), out_dtype, TN, (tmk, ns))


def _mm_blk_fwd(h3, w, out_dtype, name):
    (nb, t, ns), (_, n) = h3.shape, w.shape
    tm, tn = _pick(t, M_TILES), _pick(n, (1024,) + N_TILES)
    return _mm_core(name, (t // tm, n // tn, nb), h3, w,
                    pl.BlockSpec((None, tm, ns), lambda i, j, d: (d, i, 0)),
                    pl.BlockSpec((ns, tn), lambda i, j, d: (d, j)),
                    pl.BlockSpec((tm, tn), lambda i, j, d: (i, j)), (t, n), out_dtype, NN, (tm, tn))


def _mm_blk_dw(h3, dz, out_dtype, name):
    (nb, t, ns), (_, n) = h3.shape, dz.shape
    tn, tk = _pick(n, (1024,) + N_TILES), _pick(t, K_TILES)
    return _mm_core(name, (nb, n // tn, t // tk), h3, dz,
                    pl.BlockSpec((None, tk, ns), lambda d, j, kk: (d, kk, 0)),
                    pl.BlockSpec((tk, tn), lambda d, j, kk: (kk, j)),
                    pl.BlockSpec((ns, tn), lambda d, j, kk: (d, j)), (nb * ns, n), out_dtype, TN, (ns, tn))


def _ln_fwd(xa, xb, g, b, name):
    t, d = xa.shape
    tb = _pick(t, ROW_TILES)

    def body(xa_ref, xb_ref, g_ref, b_ref, y_ref, yb_ref):
        z = ALPHA * xa_ref[...] + xb_ref[...]
        mu = jnp.mean(z, axis=-1, keepdims=True)
        zc = z - mu
        var = jnp.mean(zc * zc, axis=-1, keepdims=True)
        y = zc * lax.rsqrt(var + LN_EPS) * g_ref[...] + b_ref[...]
        y_ref[...] = y
        yb_ref[...] = y.astype(BF16)

    row = pl.BlockSpec((tb, d), lambda i: (i, 0))
    vec = pl.BlockSpec((1, d), lambda i: (0, 0))
    return pl.pallas_call(
        body, name=name, grid=(t // tb,),
        in_specs=[row, row, vec, vec], out_specs=[row, row],
        out_shape=[jax.ShapeDtypeStruct((t, d), F32), jax.ShapeDtypeStruct((t, d), BF16)],
        compiler_params=_params("parallel"),
    )(xa, xb, g.reshape(1, d), b.reshape(1, d))


def _ln_bwd(xa, xb, g, dya, ca, dyb, name):
    t, d = xa.shape
    tb = _pick(t, ROW_TILES)
    two = dyb is not None

    def body(*refs):
        if two:
            xa_ref, xb_ref, g_ref, dya_ref, dyb_ref, dz_ref, dg_ref, db_ref = refs
            dy = ca * dya_ref[...] + dyb_ref[...]
        else:
            xa_ref, xb_ref, g_ref, dya_ref, dz_ref, dg_ref, db_ref = refs
            dy = ca * dya_ref[...]
        z = ALPHA * xa_ref[...] + xb_ref[...]
        mu = jnp.mean(z, axis=-1, keepdims=True)
        zc = z - mu
        var = jnp.mean(zc * zc, axis=-1, keepdims=True)
        rstd = lax.rsqrt(var + LN_EPS)
        xhat = zc * rstd
        dxh = dy * g_ref[...]
        m1 = jnp.mean(dxh, axis=-1, keepdims=True)
        m2 = jnp.mean(dxh * xhat, axis=-1, keepdims=True)
        dz_ref[...] = rstd * (dxh - m1 - xhat * m2)

        @pl.when(pl.program_id(0) == 0)
        def _():
            dg_ref[...] = jnp.zeros_like(dg_ref)
            db_ref[...] = jnp.zeros_like(db_ref)

        dg_ref[...] += jnp.sum(dy * xhat, axis=0, keepdims=True)
        db_ref[...] += jnp.sum(dy, axis=0, keepdims=True)

    row = pl.BlockSpec((tb, d), lambda i: (i, 0))
    vec = pl.BlockSpec((1, d), lambda i: (0, 0))
    ins = [xa, xb, g.reshape(1, d), dya] + ([dyb] if two else [])
    dz, dg, db = pl.pallas_call(
        body, name=name, grid=(t // tb,),
        in_specs=[row, row, vec, row] + ([row] if two else []),
        out_specs=[row, vec, vec],
        out_shape=[jax.ShapeDtypeStruct((t, d), F32), jax.ShapeDtypeStruct((1, d), F32),
                   jax.ShapeDtypeStruct((1, d), F32)],
        compiler_params=_params("arbitrary"),
    )(*ins)
    return dz, dg[0], db[0]


def _loss(y, target, name):
    t, d = y.shape
    tb = _pick(t, ROW_TILES)

    def body(y_ref, t_ref, dy_ref, l_ref):
        e = y_ref[...] - t_ref[...]
        dy_ref[...] = e * (1.0 / d)

        @pl.when(pl.program_id(0) == 0)
        def _():
            l_ref[...] = jnp.zeros_like(l_ref)

        l_ref[...] += 0.5 * jnp.sum(jnp.mean(e * e, axis=-1, keepdims=True))

    row = pl.BlockSpec((tb, d), lambda i: (i, 0))
    dy, l = pl.pallas_call(
        body, name=name, grid=(t // tb,),
        in_specs=[row, row], out_specs=[row, pl.BlockSpec((1, LANES), lambda i: (0, 0))],
        out_shape=[jax.ShapeDtypeStruct((t, d), F32), jax.ShapeDtypeStruct((1, LANES), F32)],
        compiler_params=_params("arbitrary"),
    )(y, target)
    return l[0, 0], dy


def _axpy(ca, a, b, name):
    t, d = a.shape
    tb = _pick(t, ROW_TILES)

    def body(a_ref, b_ref, o_ref):
        o_ref[...] = ca * a_ref[...] + b_ref[...]

    row = pl.BlockSpec((tb, d), lambda i: (i, 0))
    return pl.pallas_call(
        body, name=name, grid=(t // tb,), in_specs=[row, row], out_specs=row,
        out_shape=jax.ShapeDtypeStruct((t, d), F32), compiler_params=_params("parallel"),
    )(a, b)


def _ffn_in_swiglu(xb, g4, name):
    (t, k), (_, nb, _, ns) = xb.shape, g4.shape
    tm = _pick(t, M_TILES)

    def body(x_ref, w_ref, h_ref, gu_ref):
        xv = x_ref[...]
        gate = jnp.dot(xv, w_ref[0], preferred_element_type=F32)
        up = jnp.dot(xv, w_ref[1], preferred_element_type=F32)
        h_ref[...] = (gate * jax.nn.sigmoid(gate) * up).astype(BF16)
        gu_ref[0] = gate.astype(BF16)
        gu_ref[1] = up.astype(BF16)

    return pl.pallas_call(
        body, name=name, grid=(t // tm, nb),
        in_specs=[pl.BlockSpec((tm, k), lambda i, d: (i, 0)),
                  pl.BlockSpec((2, None, k, ns), lambda i, d: (0, d, 0, 0))],
        out_specs=[pl.BlockSpec((None, tm, ns), lambda i, d: (d, i, 0)),
                   pl.BlockSpec((2, None, tm, ns), lambda i, d: (0, d, i, 0))],
        out_shape=[jax.ShapeDtypeStruct((nb, t, ns), BF16), jax.ShapeDtypeStruct((2, nb, t, ns), BF16)],
        compiler_params=_params("parallel", "parallel"),
    )(xb, g4)


def _ffn_out_dx_swiglu(dz, w_out, gu4, name):
    (t, d), (_, nb, _, ns) = dz.shape, gu4.shape
    tm = _pick(t, M_TILES)

    def body(dz_ref, w_ref, gu_ref, o_ref):
        dh = lax.dot_general(dz_ref[...].astype(BF16), w_ref[...], NT, preferred_element_type=F32)
        gate = gu_ref[0].astype(F32)
        up = gu_ref[1].astype(F32)
        sg = jax.nn.sigmoid(gate)
        silu = gate * sg
        o_ref[0] = (dh * up * (sg + silu * (1.0 - sg))).astype(BF16)
        o_ref[1] = (dh * silu).astype(BF16)

    blk = pl.BlockSpec((2, None, tm, ns), lambda i, j: (0, j, i, 0))
    return pl.pallas_call(
        body, name=name, grid=(t // tm, nb),
        in_specs=[pl.BlockSpec((tm, d), lambda i, j: (i, 0)), pl.BlockSpec((ns, d), lambda i, j: (j, 0)), blk],
        out_specs=blk,
        out_shape=jax.ShapeDtypeStruct((2, nb, t, ns), BF16),
        compiler_params=_params("parallel", "parallel"),
    )(dz, w_out, gu4)


def _tri_matmul(tri, x):
    x1 = x.astype(BF16)
    r1 = x - x1.astype(F32)
    x2 = r1.astype(BF16)
    x3 = (r1 - x2.astype(F32)).astype(BF16)
    dot = lambda v: jnp.dot(tri, v, preferred_element_type=F32)
    return dot(x1) + dot(x2) + dot(x3)


def _fgate_fwd(proj, bf_pad, fcol, n_heads, name):
    t = proj.shape[0]
    tb = _pick(t, ROW_TILES)

    def body(p_ref, b_ref, c_ref, carry):
        @pl.when(pl.program_id(0) == 0)
        def _():
            carry[...] = jnp.zeros_like(carry)

        z = p_ref[...] + b_ref[...]
        lf = jnp.minimum(z, 0.0) - jnp.log1p(jnp.exp(-jnp.abs(z)))
        lane = lax.broadcasted_iota(jnp.int32, (tb, LANES), 1)
        lf = jnp.where(lane < n_heads, lf, 0.0)
        r = lax.broadcasted_iota(jnp.int32, (tb, tb), 0)
        s = lax.broadcasted_iota(jnp.int32, (tb, tb), 1)
        tri = (s <= r).astype(BF16)
        c = _tri_matmul(tri, lf) + carry[...]
        c_ref[...] = c
        carry[...] = c[tb - 1:tb, :]

    return pl.pallas_call(
        body, name=name, grid=(t // tb,),
        in_specs=[pl.BlockSpec((tb, LANES), lambda i: (i, fcol)), pl.BlockSpec((1, LANES), lambda i: (0, 0))],
        out_specs=pl.BlockSpec((tb, LANES), lambda i: (i, 0)),
        out_shape=jax.ShapeDtypeStruct((t, LANES), F32),
        scratch_shapes=[pltpu.VMEM((1, LANES), F32)],
        compiler_params=_params("arbitrary"),
    )(proj, bf_pad)


def _fgate_bwd(proj, bf_pad, dcq, dck, fcol, n_heads, name):
    t = proj.shape[0]
    tb = _pick(t, ROW_TILES)
    nb = t // tb

    def body(p_ref, b_ref, dcq_ref, dck_ref, dz_ref, db_ref, carry):
        @pl.when(pl.program_id(0) == 0)
        def _():
            carry[...] = jnp.zeros_like(carry)
            db_ref[...] = jnp.zeros_like(db_ref)

        r = lax.broadcasted_iota(jnp.int32, (tb, tb), 0)
        s = lax.broadcasted_iota(jnp.int32, (tb, tb), 1)
        tri = (s >= r).astype(BF16)
        dlf = _tri_matmul(tri, dcq_ref[...] + dck_ref[...]) + carry[...]
        carry[...] = dlf[0:1, :]
        z = p_ref[...] + b_ref[...]
        lane = lax.broadcasted_iota(jnp.int32, (tb, LANES), 1)
        dz = jnp.where(lane < n_heads, dlf * jax.nn.sigmoid(-z), 0.0)
        dz_ref[...] = dz
        db_ref[...] += jnp.sum(dz, axis=0, keepdims=True)

    dz, db = pl.pallas_call(
        body, name=name, grid=(nb,),
        in_specs=[pl.BlockSpec((tb, LANES), lambda i: (nb - 1 - i, fcol)),
                  pl.BlockSpec((1, LANES), lambda i: (0, 0)),
                  pl.BlockSpec((tb, LANES), lambda i: (nb - 1 - i, 0)),
                  pl.BlockSpec((tb, LANES), lambda i: (nb - 1 - i, 0))],
        out_specs=[pl.BlockSpec((tb, LANES), lambda i: (nb - 1 - i, 0)),
                   pl.BlockSpec((1, LANES), lambda i: (0, 0))],
        out_shape=[jax.ShapeDtypeStruct((t, LANES), F32), jax.ShapeDtypeStruct((1, LANES), F32)],
        scratch_shapes=[pltpu.VMEM((1, LANES), F32)],
        compiler_params=_params("arbitrary"),
    )(proj, bf_pad, dcq, dck)
    return dz, db[0]


def _lane_col(x, lane, h):
    return jnp.sum(jnp.where(lane == h, x, 0.0), axis=1, keepdims=True)


def _attn_scores(q_ref, k_ref, ct_ref, cq, h, future):
    s = lax.dot_general(q_ref[h], k_ref[h], (((1,), (1,)), ((), ())), preferred_element_type=F32)
    s = s + (cq - ct_ref[h:h + 1, :])
    return jnp.where(future, NEG_BIG, s)


def _attn_fwd(q, k, v, c, ct, name):
    nh, t, dh = q.shape
    tq = _pick(t, ROW_TILES)
    nq = t // tq

    def body(q_ref, k_ref, v_ref, c_ref, ct_ref, o_ref, lse_ref, m_s, l_s, acc_s, cq_s):
        qi = pl.program_id(0)
        ki = pl.program_id(1)
        lane = lax.broadcasted_iota(jnp.int32, (tq, LANES), 1)

        @pl.when(ki == 0)
        def _():
            m_s[...] = jnp.full(m_s.shape, NEG_BIG, F32)
            l_s[...] = jnp.zeros_like(l_s)
            acc_s[...] = jnp.zeros_like(acc_s)
            cblk = c_ref[...]
            for h in range(nh):
                cq_s[h] = _lane_col(cblk, lane, h)

        @pl.when(ki <= qi)
        def _():
            row = lax.broadcasted_iota(jnp.int32, (tq, tq), 0)
            col = lax.broadcasted_iota(jnp.int32, (tq, tq), 1)
            future = jnp.logical_and(col > row, ki == qi)
            for h in range(nh):
                s = _attn_scores(q_ref, k_ref, ct_ref, cq_s[h], h, future)
                m_prev = m_s[h]
                m_new = jnp.maximum(m_prev, jnp.max(s, axis=1, keepdims=True))
                a = jnp.exp(m_prev - m_new)
                p = jnp.exp(s - m_new)
                l_s[h] = a * l_s[h] + jnp.sum(p, axis=1, keepdims=True)
                acc_s[h] = a * acc_s[h] + jnp.dot(p.astype(BF16), v_ref[h], preferred_element_type=F32)
                m_s[h] = m_new

        @pl.when(ki == qi)
        def _():
            lse = jnp.zeros((tq, LANES), F32)
            for h in range(nh):
                o_ref[h] = (acc_s[h] / l_s[h]).astype(o_ref.dtype)
                lse = lse + jnp.where(lane == h, m_s[h] + jnp.log(l_s[h]), 0.0)
            lse_ref[...] = lse

    qspec = pl.BlockSpec((nh, tq, dh), lambda qi, ki: (0, qi, 0))
    kspec = pl.BlockSpec((nh, tq, dh), lambda qi, ki: (0, jnp.minimum(ki, qi), 0))
    return pl.pallas_call(
        body, name=name, grid=(nq, nq),
        in_specs=[qspec, kspec, kspec, pl.BlockSpec((tq, LANES), lambda qi, ki: (qi, 0)),
                  pl.BlockSpec((nh, tq), lambda qi, ki: (0, jnp.minimum(ki, qi)))],
        out_specs=[qspec, pl.BlockSpec((tq, LANES), lambda qi, ki: (qi, 0))],
        out_shape=[jax.ShapeDtypeStruct((nh, t, dh), F32), jax.ShapeDtypeStruct((t, LANES), F32)],
        scratch_shapes=[pltpu.VMEM((nh, tq, 1), F32), pltpu.VMEM((nh, tq, 1), F32),
                        pltpu.VMEM((nh, tq, dh), F32), pltpu.VMEM((nh, tq, 1), F32)],
        compiler_params=_params("parallel", "arbitrary"),
    )(q, k, v, c, ct)


def _attn_dq(q, k, v, c, ct, lse, o, do, scale, name):
    nh, t, dh = q.shape
    tq = _pick(t, ROW_TILES)
    nq = t // tq

    def body(q_ref, k_ref, v_ref, c_ref, ct_ref, lse_ref, o_ref, do_ref, dq_ref, drow_ref,
             dq_s, cq_s, lse_s, dl_s, rs_s):
        qi = pl.program_id(0)
        ki = pl.program_id(1)
        lane = lax.broadcasted_iota(jnp.int32, (tq, LANES), 1)

        @pl.when(ki == 0)
        def _():
            dq_s[...] = jnp.zeros_like(dq_s)
            rs_s[...] = jnp.zeros_like(rs_s)
            cblk = c_ref[...]
            lblk = lse_ref[...]
            for h in range(nh):
                cq_s[h] = _lane_col(cblk, lane, h)
                lse_s[h] = _lane_col(lblk, lane, h)
                dl_s[h] = jnp.sum(do_ref[h].astype(F32) * o_ref[h].astype(F32), axis=1, keepdims=True)

        @pl.when(ki <= qi)
        def _():
            row = lax.broadcasted_iota(jnp.int32, (tq, tq), 0)
            col = lax.broadcasted_iota(jnp.int32, (tq, tq), 1)
            future = jnp.logical_and(col > row, ki == qi)
            for h in range(nh):
                s = _attn_scores(q_ref, k_ref, ct_ref, cq_s[h], h, future)
                p = jnp.exp(s - lse_s[h])
                dp = lax.dot_general(do_ref[h], v_ref[h], (((1,), (1,)), ((), ())), preferred_element_type=F32)
                ds = p * (dp - dl_s[h])
                dq_s[h] += jnp.dot(ds.astype(BF16), k_ref[h], preferred_element_type=F32)
                rs_s[h] += jnp.sum(ds, axis=1, keepdims=True)

        @pl.when(ki == qi)
        def _():
            dq_ref[...] = dq_s[...] * scale
            drow = jnp.zeros((tq, LANES), F32)
            for h in range(nh):
                drow = drow + jnp.where(lane == h, rs_s[h], 0.0)
            drow_ref[...] = drow

    qspec = pl.BlockSpec((nh, tq, dh), lambda qi, ki: (0, qi, 0))
    kspec = pl.BlockSpec((nh, tq, dh), lambda qi, ki: (0, jnp.minimum(ki, qi), 0))
    lspec = pl.BlockSpec((tq, LANES), lambda qi, ki: (qi, 0))
    return pl.pallas_call(
        body, name=name, grid=(nq, nq),
        in_specs=[qspec, kspec, kspec, lspec, pl.BlockSpec((nh, tq), lambda qi, ki: (0, jnp.minimum(ki, qi))),
                  lspec, qspec, qspec],
        out_specs=[qspec, lspec],
        out_shape=[jax.ShapeDtypeStruct((nh, t, dh), F32), jax.ShapeDtypeStruct((t, LANES), F32)],
        scratch_shapes=[pltpu.VMEM((nh, tq, dh), F32), pltpu.VMEM((nh, tq, 1), F32),
                        pltpu.VMEM((nh, tq, 1), F32), pltpu.VMEM((nh, tq, 1), F32), pltpu.VMEM((nh, tq, 1), F32)],
        compiler_params=_params("parallel", "arbitrary"),
    )(q, k, v, c, ct, lse, o, do)


def _attn_dkv(q, k, v, c, ct, lse, o, do, name):
    nh, t, dh = q.shape
    tq = _pick(t, ROW_TILES)
    nq = t // tq

    def body(q_ref, k_ref, v_ref, c_ref, ct_ref, lse_ref, o_ref, do_ref, dk_ref, dv_ref, dct_ref):
        ki = pl.program_id(0)
        qi = pl.program_id(1)
        lane = lax.broadcasted_iota(jnp.int32, (tq, LANES), 1)

        @pl.when(qi == 0)
        def _():
            dk_ref[...] = jnp.zeros_like(dk_ref)
            dv_ref[...] = jnp.zeros_like(dv_ref)
            dct_ref[...] = jnp.zeros_like(dct_ref)

        @pl.when(qi >= ki)
        def _():
            row = lax.broadcasted_iota(jnp.int32, (tq, tq), 0)
            col = lax.broadcasted_iota(jnp.int32, (tq, tq), 1)
            future = jnp.logical_and(col > row, ki == qi)
            cblk = c_ref[...]
            lblk = lse_ref[...]
            for h in range(nh):
                s = _attn_scores(q_ref, k_ref, ct_ref, _lane_col(cblk, lane, h), h, future)
                p = jnp.exp(s - _lane_col(lblk, lane, h))
                dob = do_ref[h]
                delta = jnp.sum(dob.astype(F32) * o_ref[h].astype(F32), axis=1, keepdims=True)
                dp = lax.dot_general(dob, v_ref[h], (((1,), (1,)), ((), ())), preferred_element_type=F32)
                ds = p * (dp - delta)
                dv_ref[h] += lax.dot_general(p.astype(BF16), dob, (((0,), (0,)), ((), ())),
                                             preferred_element_type=F32)
                dk_ref[h] += lax.dot_general(ds.astype(BF16), q_ref[h], (((0,), (0,)), ((), ())),
                                             preferred_element_type=F32)
                dct_ref[h:h + 1, :] -= jnp.sum(ds, axis=0, keepdims=True)

    qspec = pl.BlockSpec((nh, tq, dh), lambda ki, qi: (0, jnp.maximum(qi, ki), 0))
    kspec = pl.BlockSpec((nh, tq, dh), lambda ki, qi: (0, ki, 0))
    lspec = pl.BlockSpec((tq, LANES), lambda ki, qi: (jnp.maximum(qi, ki), 0))
    ctspec = pl.BlockSpec((nh, tq), lambda ki, qi: (0, ki))
    return pl.pallas_call(
        body, name=name, grid=(nq, nq),
        in_specs=[qspec, kspec, kspec, lspec, ctspec, lspec, qspec, qspec],
        out_specs=[kspec, kspec, ctspec],
        out_shape=[jax.ShapeDtypeStruct((nh, t, dh), F32), jax.ShapeDtypeStruct((nh, t, dh), F32),
                   jax.ShapeDtypeStruct((nh, t), F32)],
        compiler_params=_params("parallel", "arbitrary"),
    )(q, k, v, c, ct, lse, o, do)


def _conv_fwd(proj, cw, w, bcol, name):
    t = proj.shape[0]
    tb = _pick(t, ROW_TILES)
    hb = tb // SUBLANES

    def body(b_ref, c_ref, h_ref, cp_ref, hp_ref, w_ref, y_ref):
        i = pl.program_id(0)
        zp = jnp.where(i > 0, cp_ref[...] * hp_ref[...], 0.0)
        zext = jnp.concatenate([zp, c_ref[...] * h_ref[...]], axis=0)
        z1 = pltpu.roll(zext, 1, 0)[SUBLANES:]
        z2 = pltpu.roll(zext, 2, 0)[SUBLANES:]
        y = w_ref[2:3, :] * zext[SUBLANES:] + w_ref[1:2, :] * z1 + w_ref[0:1, :] * z2
        y_ref[...] = (b_ref[...] * y).astype(BF16)

    cur = lambda j: pl.BlockSpec((tb, w), lambda i: (i, bcol + j))
    prev = lambda j: pl.BlockSpec((SUBLANES, w), lambda i: (jnp.maximum(i * hb - 1, 0), bcol + j))
    return pl.pallas_call(
        body, name=name, grid=(t // tb,),
        in_specs=[cur(0), cur(1), cur(2), prev(1), prev(2), pl.BlockSpec(cw.shape, lambda i: (0, 0))],
        out_specs=pl.BlockSpec((tb, w), lambda i: (i, 0)),
        out_shape=jax.ShapeDtypeStruct((t, w), BF16), compiler_params=_params("parallel"),
    )(proj, proj, proj, proj, proj, cw)


def _conv_bwd(proj, cw, dmix, w, bcol, name):
    t = proj.shape[0]
    tb = _pick(t, ROW_TILES)
    hb = tb // SUBLANES
    nb = t // tb
    n_ext = tb + SUBLANES

    def body(b_ref, c_ref, h_ref, cp_ref, hp_ref, bn_ref, d_ref, dn_ref, w_ref, db_ref, dc_ref, dh_ref, dw_ref):
        i = pl.program_id(0)
        c = c_ref[...]
        hh = h_ref[...]
        zp = jnp.where(i > 0, cp_ref[...] * hp_ref[...], 0.0)
        zext = jnp.concatenate([zp, c * hh], axis=0)
        z0 = zext[SUBLANES:]
        z1 = pltpu.roll(zext, 1, 0)[SUBLANES:]
        z2 = pltpu.roll(zext, 2, 0)[SUBLANES:]
        y = w_ref[2:3, :] * z0 + w_ref[1:2, :] * z1 + w_ref[0:1, :] * z2
        d = d_ref[...]
        db_ref[...] = d * y
        dy = d * b_ref[...]
        dyn = jnp.where(i < nb - 1, dn_ref[...] * bn_ref[...], 0.0)
        dext = jnp.concatenate([dy, dyn], axis=0)
        dy1 = pltpu.roll(dext, n_ext - 1, 0)[:tb]
        dy2 = pltpu.roll(dext, n_ext - 2, 0)[:tb]
        dz = w_ref[2:3, :] * dy + w_ref[1:2, :] * dy1 + w_ref[0:1, :] * dy2
        dc_ref[...] = dz * hh
        dh_ref[...] = dz * c

        @pl.when(i == 0)
        def _():
            dw_ref[...] = jnp.zeros_like(dw_ref)

        dw_ref[0:1, :] += jnp.sum(dy * z2, axis=0, keepdims=True)
        dw_ref[1:2, :] += jnp.sum(dy * z1, axis=0, keepdims=True)
        dw_ref[2:3, :] += jnp.sum(dy * z0, axis=0, keepdims=True)

    cur = lambda j: pl.BlockSpec((tb, w), lambda i: (i, bcol + j))
    prev = lambda j: pl.BlockSpec((SUBLANES, w), lambda i: (jnp.maximum(i * hb - 1, 0), bcol + j))
    nxt = lambda col: pl.BlockSpec((SUBLANES, w), lambda i: (jnp.minimum((i + 1) * hb, nb * hb - 1), col))
    out = pl.BlockSpec((tb, w), lambda i: (i, 0))
    return pl.pallas_call(
        body, name=name, grid=(nb,),
        in_specs=[cur(0), cur(1), cur(2), prev(1), prev(2), nxt(bcol),
                  pl.BlockSpec((tb, w), lambda i: (i, 1)), nxt(1), pl.BlockSpec(cw.shape, lambda i: (0, 0))],
        out_specs=[out, out, out, pl.BlockSpec(cw.shape, lambda i: (0, 0))],
        out_shape=[jax.ShapeDtypeStruct((t, w), F32)] * 3 + [jax.ShapeDtypeStruct(cw.shape, F32)],
        compiler_params=_params("arbitrary"),
    )(proj, proj, proj, proj, proj, proj, dmix, dmix, cw)


SQRT_HALF = 0.7071067811865476
INV_SQRT_2PI = 0.3989422804014327


def _gelu(x):
    return 0.5 * x * (1.0 + lax.erf(x * SQRT_HALF))


def _gelu_grad(x):
    return 0.5 * (1.0 + lax.erf(x * SQRT_HALF)) + x * (INV_SQRT_2PI * jnp.exp(-0.5 * x * x))


def _sgu_fwd(uv, ln_g, ln_b, wm, bs_full, name):
    t, d2 = uv.shape
    d = d2 // 2
    ng, pb, _ = wm.shape
    gd = d // ng
    tb = _pick(t, ROW_TILES[1:] or ROW_TILES)
    assert tb % pb == 0

    def body(uv_ref, g_ref, b_ref, w_ref, bs_ref, o_ref):
        u = _gelu(uv_ref[:, :d])
        v = _gelu(uv_ref[:, d:])
        mu = jnp.mean(v, axis=-1, keepdims=True)
        vc = v - mu
        var = jnp.mean(vc * vc, axis=-1, keepdims=True)
        vn = (vc * lax.rsqrt(var + LN_EPS) * g_ref[...] + b_ref[...]).astype(BF16)
        for r in range(tb // pb):
            rows = slice(r * pb, (r + 1) * pb)
            for gi in range(ng):
                cols = slice(gi * gd, (gi + 1) * gd)
                s = jnp.dot(w_ref[gi], vn[rows, cols], preferred_element_type=F32) + bs_ref[:, cols]
                o_ref[rows, cols] = (u[rows, cols] * s).astype(BF16)

    vec = pl.BlockSpec((1, d), lambda i: (0, 0))
    return pl.pallas_call(
        body, name=name, grid=(t // tb,),
        in_specs=[pl.BlockSpec((tb, d2), lambda i: (i, 0)), vec, vec,
                  pl.BlockSpec(wm.shape, lambda i: (0, 0, 0)), pl.BlockSpec((pb, d), lambda i: (0, 0))],
        out_specs=pl.BlockSpec((tb, d), lambda i: (i, 0)),
        out_shape=jax.ShapeDtypeStruct((t, d), BF16), compiler_params=_params("parallel"),
    )(uv, ln_g.reshape(1, d), ln_b.reshape(1, d), wm, bs_full)


def _sgu_bwd(uv, ln_g, ln_b, wm, bs_full, dgated, name):
    t, d2 = uv.shape
    d = d2 // 2
    ng, pb, _ = wm.shape
    gd = d // ng
    tb = _pick(t, ROW_TILES[1:] or ROW_TILES)
    nb = t // tb

    def body(uv_ref, g_ref, b_ref, w_ref, bs_ref, dg_ref, o_ref, dw_ref, dbs_ref, dlg_ref, dlb_ref,
             du_s, dvn_s, dbs_s):
        i = pl.program_id(0)

        @pl.when(i == 0)
        def _():
            dw_ref[...] = jnp.zeros_like(dw_ref)
            dbs_s[...] = jnp.zeros_like(dbs_s)
            dlg_ref[...] = jnp.zeros_like(dlg_ref)
            dlb_ref[...] = jnp.zeros_like(dlb_ref)

        upre = uv_ref[:, :d]
        vpre = uv_ref[:, d:]
        u = _gelu(upre)
        v = _gelu(vpre)
        mu = jnp.mean(v, axis=-1, keepdims=True)
        vc = v - mu
        var = jnp.mean(vc * vc, axis=-1, keepdims=True)
        rstd = lax.rsqrt(var + LN_EPS)
        xhat = vc * rstd
        vn = (xhat * g_ref[...] + b_ref[...]).astype(BF16)
        dgt = dg_ref[...].astype(F32)
        for r in range(tb // pb):
            rows = slice(r * pb, (r + 1) * pb)
            for gi in range(ng):
                cols = slice(gi * gd, (gi + 1) * gd)
                vblk = vn[rows, cols]
                s = jnp.dot(w_ref[gi], vblk, preferred_element_type=F32) + bs_ref[:, cols]
                dblk = dgt[rows, cols]
                du_s[rows, cols] = dblk * s
                ds = dblk * u[rows, cols]
                dsb = ds.astype(BF16)
                dvn_s[rows, cols] = lax.dot_general(w_ref[gi], dsb, (((0,), (0,)), ((), ())),
                                                    preferred_element_type=F32)
                dw_ref[gi] += lax.dot_general(dsb, vblk, (((1,), (1,)), ((), ())), preferred_element_type=F32)
                dbs_s[:, cols] += ds
        dvn = dvn_s[...]
        dlg_ref[...] += jnp.sum(dvn * xhat, axis=0, keepdims=True)
        dlb_ref[...] += jnp.sum(dvn, axis=0, keepdims=True)
        dxh = dvn * g_ref[...]
        m1 = jnp.mean(dxh, axis=-1, keepdims=True)
        m2 = jnp.mean(dxh * xhat, axis=-1, keepdims=True)
        dv = rstd * (dxh - m1 - xhat * m2)
        o_ref[:, :d] = (du_s[...] * _gelu_grad(upre)).astype(BF16)
        o_ref[:, d:] = (dv * _gelu_grad(vpre)).astype(BF16)

        @pl.when(i == nb - 1)
        def _():
            lane = lax.broadcasted_iota(jnp.int32, (pb, LANES), 1)
            acc = jnp.zeros((pb, LANES), F32)
            for gi in range(ng):
                col = jnp.sum(dbs_s[:, gi * gd:(gi + 1) * gd], axis=1, keepdims=True)
                acc = acc + jnp.where(lane == gi, col, 0.0)
            dbs_ref[...] = acc

    vec = pl.BlockSpec((1, d), lambda i: (0, 0))
    duv, dw, dbs, dlg, dlb = pl.pallas_call(
        body, name=name, grid=(nb,),
        in_specs=[pl.BlockSpec((tb, d2), lambda i: (i, 0)), vec, vec,
                  pl.BlockSpec(wm.shape, lambda i: (0, 0, 0)), pl.BlockSpec((pb, d), lambda i: (0, 0)),
                  pl.BlockSpec((tb, d), lambda i: (i, 0))],
        out_specs=[pl.BlockSpec((tb, d2), lambda i: (i, 0)), pl.BlockSpec(wm.shape, lambda i: (0, 0, 0)),
                   pl.BlockSpec((pb, LANES), lambda i: (0, 0)), vec, vec],
        out_shape=[jax.ShapeDtypeStruct((t, d2), BF16), jax.ShapeDtypeStruct(wm.shape, F32),
                   jax.ShapeDtypeStruct((pb, LANES), F32), jax.ShapeDtypeStruct((1, d), F32),
                   jax.ShapeDtypeStruct((1, d), F32)],
        scratch_shapes=[pltpu.VMEM((tb, d), F32), pltpu.VMEM((tb, d), F32), pltpu.VMEM((pb, d), F32)],
        compiler_params=_params("arbitrary"),
    )(uv, ln_g.reshape(1, d), ln_b.reshape(1, d), wm, bs_full, dgated)
    return duv, dw, dbs, dlg[0], dlb[0]


def _adamw(w, g, m, v, name):
    shape = w.shape
    cols = shape[-1]
    rows = w.size // cols
    tr = _pick(rows, (512, 256, 352, 128, 64, 32, 16, 8))

    def body(w_ref, g_ref, m_ref, v_ref, d_ref, mo_ref, vo_ref):
        d_ref[...], mo_ref[...], vo_ref[...] = _adam_update(w_ref[...], g_ref[...], m_ref[...], v_ref[...])

    spec = pl.BlockSpec((tr, cols), lambda i: (i, 0))
    outs = pl.pallas_call(
        body, name=name, grid=(rows // tr,),
        in_specs=[spec] * 4, out_specs=[spec] * 3,
        out_shape=[jax.ShapeDtypeStruct((rows, cols), F32)] * 3,
        compiler_params=_params("parallel"),
    )(*[a.reshape(rows, cols) for a in (w, g, m, v)])
    return [o.reshape(shape) for o in outs]


ANY = pl.BlockSpec(memory_space=pl.ANY)


def _place():
    return lax.axis_index("x"), lax.axis_index("y"), lax.axis_index("c")


def _all_gather(shards, name):
    n = len(shards)

    def body(*refs):
        x_refs, out_refs = refs[:n], refs[n:2 * n]
        send_sems, recv_sems, local_sems = refs[2 * n:]
        x, y, c = _place()
        me, sibling = (x, y, c), (x, y, 1 - c)
        chips = [(1 - x, y), (x, 1 - y), (1 - x, 1 - y)]

        def copy(a, k, block, to, own=False):
            px, py, pc = block
            rows = out_refs[a].at[4 * px + 2 * py + pc]
            return pltpu.make_async_remote_copy(
                src_ref=x_refs[a] if own else rows, dst_ref=rows,
                send_sem=send_sems.at[7 * a + k], recv_sem=recv_sems.at[7 * a + k],
                device_id=to, device_id_type=MESH)

        started = []
        for a in range(n):
            mine = pltpu.make_async_copy(x_refs[a], out_refs[a].at[4 * x + 2 * y + c], local_sems.at[a])
            mine.start()
            started.append(mine)
        sends = []
        for a in range(n):
            first = [copy(a, 0, me, sibling, own=True)]
            first += [copy(a, 1 + j, me, (*chip, c), own=True) for j, chip in enumerate(chips)]
            for cp in first:
                cp.start()
            sends += first
        for j, chip in enumerate(chips):
            for a in range(n):
                copy(a, 1 + j, (*chip, c), me).wait_recv()
                fwd = copy(a, 4 + j, (*chip, c), sibling)
                fwd.start()
                sends.append(fwd)
        for a in range(n):
            copy(a, 0, sibling, me).wait_recv()
            for j, chip in enumerate(chips):
                copy(a, 4 + j, (*chip, 1 - c), me).wait_recv()
        for cp in sends:
            cp.wait_send()
        for mine in started:
            mine.wait()

    return pl.pallas_call(
        body, name=name, in_specs=[ANY] * n, out_specs=[ANY] * n,
        out_shape=[jax.ShapeDtypeStruct((N_DEV,) + s.shape, s.dtype) for s in shards],
        scratch_shapes=[pltpu.SemaphoreType.DMA((7 * n,)), pltpu.SemaphoreType.DMA((7 * n,)),
                        pltpu.SemaphoreType.DMA((n,))],
    )(*shards)


def _rs_sibling_exchange(packed, name):
    n = len(packed)

    def body(*refs):
        p_refs, r_refs = refs[:n], refs[n:2 * n]
        send_sems, recv_sems = refs[2 * n:]
        x, y, c = _place()
        cps = []
        for a in range(n):
            for j in range(4):
                cps.append(pltpu.make_async_remote_copy(
                    src_ref=p_refs[a].at[2 * j + (1 - c)], dst_ref=r_refs[a].at[j],
                    send_sem=send_sems.at[4 * a + j], recv_sem=recv_sems.at[4 * a + j],
                    device_id=(x, y, 1 - c), device_id_type=MESH))
        for cp in cps:
            cp.start()
        for cp in cps:
            cp.wait()

    return pl.pallas_call(
        body, name=name, in_specs=[ANY] * n, out_specs=[ANY] * n,
        out_shape=[jax.ShapeDtypeStruct((4,) + p.shape[1:], p.dtype) for p in packed],
        scratch_shapes=[pltpu.SemaphoreType.DMA((4 * n,)), pltpu.SemaphoreType.DMA((4 * n,))],
    )(*packed)


def _rs_chip_sum(packed, from_sibling, c_idx, name):
    _, r, cc = packed.shape
    tr = _pick(r, (512, 256, 352, 128))

    def body(c_ref, a_ref, b_ref, o_ref):
        o_ref[...] = (a_ref[...].astype(F32) + b_ref[...].astype(F32)).astype(o_ref.dtype)

    return pl.pallas_call(
        body, name=name,
        grid_spec=pltpu.PrefetchScalarGridSpec(
            num_scalar_prefetch=1, grid=(4, r // tr),
            in_specs=[pl.BlockSpec((None, tr, cc), lambda j, i, c_ref: (2 * j + c_ref[0], i, 0)),
                      pl.BlockSpec((None, tr, cc), lambda j, i, c_ref: (j, i, 0))],
            out_specs=pl.BlockSpec((None, tr, cc), lambda j, i, c_ref: (j, i, 0))),
        out_shape=jax.ShapeDtypeStruct((4, r, cc), packed.dtype),
        compiler_params=_params("parallel", "parallel"),
    )(c_idx, packed, from_sibling)


def _rs_chip_exchange(partial, name):
    n = len(partial)

    def body(*refs):
        p_refs, r_refs = refs[:n], refs[n:2 * n]
        send_sems, recv_sems = refs[2 * n:]
        x, y, c = _place()
        chips = [(1 - x, y), (x, 1 - y), (1 - x, 1 - y)]
        cps = []
        for a in range(n):
            for k, (tx, ty) in enumerate(chips):
                cps.append(pltpu.make_async_remote_copy(
                    src_ref=p_refs[a].at[2 * tx + ty], dst_ref=r_refs[a].at[k],
                    send_sem=send_sems.at[3 * a + k], recv_sem=recv_sems.at[3 * a + k],
                    device_id=(tx, ty, c), device_id_type=MESH))
        for cp in cps:
            cp.start()
        for cp in cps:
            cp.wait()

    return pl.pallas_call(
        body, name=name, in_specs=[ANY] * n, out_specs=[ANY] * n,
        out_shape=[jax.ShapeDtypeStruct((3,) + p.shape[1:], p.dtype) for p in partial],
        scratch_shapes=[pltpu.SemaphoreType.DMA((3 * n,)), pltpu.SemaphoreType.DMA((3 * n,))],
    )(*partial)


def _adam_update(w, g, m, v):
    mn = ADAM_B1 * m + (1.0 - ADAM_B1) * g
    vn = ADAM_B2 * v + (1.0 - ADAM_B2) * (g * g)
    m_hat = mn / (1.0 - ADAM_B1 ** ADAM_STEP)
    v_hat = vn / (1.0 - ADAM_B2 ** ADAM_STEP)
    return -ADAM_LR * (m_hat / (jnp.sqrt(v_hat) + ADAM_EPS) + ADAM_WD * w), mn, vn


def _rs_final_adamw(partial, received, chip_idx, w, m, v, name):
    _, r, cc = partial.shape
    tr = _pick(r, (512, 256, 352, 128))

    def body(c_ref, a_ref, r_ref, w_ref, m_ref, v_ref, g_ref, d_ref, mo_ref, vo_ref):
        g = a_ref[...].astype(F32)
        for k in range(3):
            g = g + r_ref[k].astype(F32)
        g_ref[...] = g
        d_ref[...], mo_ref[...], vo_ref[...] = _adam_update(w_ref[...], g, m_ref[...], v_ref[...])

    row = pl.BlockSpec((tr, cc), lambda i, c_ref: (i, 0))
    return pl.pallas_call(
        body, name=name,
        grid_spec=pltpu.PrefetchScalarGridSpec(
            num_scalar_prefetch=1, grid=(r // tr,),
            in_specs=[pl.BlockSpec((None, tr, cc), lambda i, c_ref: (c_ref[0], i, 0)),
                      pl.BlockSpec((3, tr, cc), lambda i, c_ref: (0, i, 0)), row, row, row],
            out_specs=[row] * 4),
        out_shape=[jax.ShapeDtypeStruct((r, cc), F32)] * 4,
        compiler_params=_params("parallel"),
    )(chip_idx, partial, received, w.reshape(r, cc), m.reshape(r, cc), v.reshape(r, cc))


def _all_reduce_small(vals, name):
    r, cc = vals.shape

    def body(v_ref, o_ref, buf, send_sems, recv_sems):
        x, y, c = _place()
        me = 4 * x + 2 * y + c
        buf[0] = v_ref[...]
        cps = []
        for k in range(1, N_DEV):
            kx, ky, kc = (k >> 2) & 1, (k >> 1) & 1, k & 1
            peer = (1 - x if kx else x, 1 - y if ky else y, 1 - c if kc else c)
            cps.append(pltpu.make_async_remote_copy(
                src_ref=buf.at[0], dst_ref=buf.at[k], send_sem=send_sems.at[k - 1],
                recv_sem=recv_sems.at[k - 1], device_id=peer, device_id_type=MESH))
        for cp in cps:
            cp.start()
        for cp in cps:
            cp.wait()
        acc = buf[jnp.bitwise_xor(me, 0)]
        for dev in range(1, N_DEV):
            acc = acc + buf[jnp.bitwise_xor(me, dev)]
        o_ref[...] = acc

    vm = pl.BlockSpec(memory_space=pltpu.VMEM)
    return pl.pallas_call(
        body, name=name, in_specs=[vm], out_specs=vm,
        out_shape=jax.ShapeDtypeStruct((r, cc), F32),
        scratch_shapes=[pltpu.VMEM((N_DEV, r, cc), F32), pltpu.SemaphoreType.DMA((7,)),
                        pltpu.SemaphoreType.DMA((7,))],
        compiler_params=pltpu.CompilerParams(vmem_limit_bytes=VMEM_LIMIT),
    )(vals)


def _lanes(flat):
    pad = (-flat.shape[0]) % (SUBLANES * LANES)
    return jnp.pad(flat, (0, pad)).reshape(-1, LANES)


def kernel(x, even_w_in, even_b_f, even_conv_w, even_w_out, odd_w_in, odd_v_ln_g, odd_v_ln_b, odd_w_s, odd_b_s, odd_w_out, mix_ln_g, mix_ln_b, ffn_w_in, ffn_w_out, ffn_ln_g, ffn_ln_b, loss_target, m_even_w_in, m_even_b_f, m_even_conv_w, m_even_w_out, m_odd_w_in, m_odd_v_ln_g, m_odd_v_ln_b, m_odd_w_s, m_odd_b_s, m_odd_w_out, m_mix_ln_g, m_mix_ln_b, m_ffn_w_in, m_ffn_w_out, m_ffn_ln_g, m_ffn_ln_b, v_even_w_in, v_even_b_f, v_even_conv_w, v_even_w_out, v_odd_w_in, v_odd_v_ln_g, v_odd_v_ln_b, v_odd_w_s, v_odd_b_s, v_odd_w_out, v_mix_ln_g, v_mix_ln_b, v_ffn_w_in, v_ffn_w_out, v_ffn_ln_g, v_ffn_ln_b):
    t, d = x.shape[1], x.shape[2]
    nh = even_b_f.shape[-1]
    w = even_conv_w.shape[-1] * N_DEV
    dh = w // nh
    scale = dh ** -0.5
    e_in = even_w_in.shape[-1] * N_DEV
    f2 = ffn_w_in.shape[-1] * N_DEV
    f = f2 // 2
    ng, pb = odd_w_s.shape[1], odd_w_s.shape[2]
    assert e_in == 6 * w + nh and nh <= SUBLANES and (6 * w) % LANES == 0 and d % N_DEV == 0
    mx, my, mc = _place()
    me = 4 * mx + 2 * my + mc

    big = [even_w_in[0], even_w_out[0], odd_w_in[0], odd_w_out[0],
           ffn_w_in[0], ffn_w_in[1], ffn_w_out[0], ffn_w_out[1]]
    g_in0, g_out0, g_in1, g_out1, g_fi0, g_fi1, g_fo0, g_fo1 = _all_gather([s.astype(BF16) for s in big], "ag_weights")
    w_out0, w_out1 = g_out0.reshape(2 * w, d), g_out1.reshape(d, d)
    w_fo0, w_fo1 = g_fo0.reshape(f, d), g_fo1.reshape(f, d)
    nb = N_DEV // 2
    w_fi0, w_fi1 = g_fi0.reshape(2, nb, d, -1), g_fi1.reshape(2, nb, d, -1)
    w_in0 = g_in0.transpose(1, 0, 2).reshape(d, e_in)
    w_all0 = jnp.concatenate([w_in0[:, :3 * w], w_in0[:, 3 * w + nh:], w_in0[:, 3 * w:3 * w + nh],
                              jnp.zeros((d, LANES - nh), BF16)], axis=1)

    cs, vs = even_conv_w.shape[-1], odd_v_ln_g.shape[-1]
    small_mine = jnp.concatenate([
        lax.dynamic_update_slice(jnp.zeros((3, w), F32), even_conv_w[0], (0, me * cs)).reshape(-1),
        lax.dynamic_update_slice(jnp.zeros((d,), F32), odd_v_ln_g[0], (me * vs,)),
        lax.dynamic_update_slice(jnp.zeros((d,), F32), odd_v_ln_b[0], (me * vs,))])
    small_all = _all_reduce_small(_lanes(small_mine), "ag_small").reshape(-1)
    conv_w = small_all[:3 * w].reshape(3, w)
    vln_g = small_all[3 * w:3 * w + d]
    vln_b = small_all[3 * w + d:3 * w + 2 * d]

    bf_pad = jnp.pad(even_b_f[0], (0, LANES - nh)).reshape(1, LANES)
    chunk = jnp.arange(pb) // (pb // 2)
    ws_mask = (chunk[None, :] <= chunk[:, None])[None]
    wm = jnp.where(ws_mask, odd_w_s[0], 0.0).astype(BF16)
    bs_full = jnp.repeat(odd_b_s[0].T, d // ng, axis=1)

    x0 = x[0]
    tgt = loss_target[0]
    fcol = 6 * w // LANES
    p0 = _mm(x0, w_all0, "nn", F32, "l0_in_proj")
    cgate = _fgate_fwd(p0, bf_pad, fcol, nh, "l0_fgate")
    ct = cgate[:, :nh].T
    heads = lambda a: a.reshape(t, nh, dh).transpose(1, 0, 2)
    unheads = lambda a: a.transpose(1, 0, 2).reshape(t, w)
    qh = heads((p0[:, :w] * scale).astype(BF16))
    kh = heads(p0[:, w:2 * w].astype(BF16))
    vh = heads(p0[:, 2 * w:3 * w].astype(BF16))
    oh, lse = _attn_fwd(qh, kh, vh, cgate, ct, "l0_attn")
    yconv = _conv_fwd(p0, conv_w, w, 3, "l0_conv")
    mix = jnp.concatenate([unheads(oh).astype(BF16), yconv], axis=1)
    m0 = _mm(mix, w_out0, "nn", F32, "l0_out_proj")
    x1, x1b = _ln_fwd(x0, m0, mix_ln_g[0], mix_ln_b[0], "l0_mix_ln")
    h0, gu0 = _ffn_in_swiglu(x1b, w_fi0, "l0_ffn_in")
    f0 = _mm_blk_fwd(h0, w_fo0, F32, "l0_ffn_out")
    x2, x2b = _ln_fwd(x1, f0, ffn_ln_g[0], ffn_ln_b[0], "l0_ffn_ln")

    uv = _mm_cols_fwd(x2b, g_in1, False, F32, "l1_in_proj")
    gated = _sgu_fwd(uv, vln_g, vln_b, wm, bs_full, "l1_sgu")
    m1 = _mm(gated, w_out1, "nn", F32, "l1_out_proj")
    x3, x3b = _ln_fwd(x2, m1, mix_ln_g[1], mix_ln_b[1], "l1_mix_ln")
    h1, gu1 = _ffn_in_swiglu(x3b, w_fi1, "l1_ffn_in")
    f1 = _mm_blk_fwd(h1, w_fo1, F32, "l1_ffn_out")
    x4, _ = _ln_fwd(x3, f1, ffn_ln_g[1], ffn_ln_b[1], "l1_ffn_ln")
    loss_part, dy4 = _loss(x4, tgt, "loss")

    dz4, g_ffn_g1, g_ffn_b1 = _ln_bwd(x3, f1, ffn_ln_g[1], dy4, 1.0, None, "l1_ffn_ln_bwd")
    gd_fo1 = _mm_blk_dw(h1, dz4, BF16, "l1_ffn_out_dw").reshape(N_DEV, -1, d)
    dgu1 = _ffn_out_dx_swiglu(dz4, w_fo1, gu1, "l1_ffn_out_dx").reshape(N_DEV, t, -1)
    gd_fi1 = _mm_cols_dw(x3b, dgu1, N_DEV, True, BF16, "l1_ffn_in_dw")
    dx3 = _mm_cols_dx(dgu1, g_fi1, True, F32, "l1_ffn_in_dx")
    dz3, g_mix_g1, g_mix_b1 = _ln_bwd(x2, m1, mix_ln_g[1], dz4, ALPHA, dx3, "l1_mix_ln_bwd")
    gd_out1 = _mm(gated, dz3, "tn", BF16, "l1_out_proj_dw").reshape(N_DEV, -1, d)
    dgated = _mm(dz3, w_out1, "nt", BF16, "l1_out_proj_dx")
    duv, g_wm, g_bs_t, g_vln_g, g_vln_b = _sgu_bwd(uv, vln_g, vln_b, wm, bs_full, dgated, "l1_sgu_bwd")
    gd_in1 = _mm_cols_dw(x2b, duv, N_DEV, False, BF16, "l1_in_proj_dw")
    dx2 = _mm_cols_dx(duv, g_in1, False, F32, "l1_in_proj_dx")

    dz2, g_ffn_g0, g_ffn_b0 = _ln_bwd(x1, f0, ffn_ln_g[0], dz3, ALPHA, dx2, "l0_ffn_ln_bwd")
    gd_fo0 = _mm_blk_dw(h0, dz2, BF16, "l0_ffn_out_dw").reshape(N_DEV, -1, d)
    dgu0 = _ffn_out_dx_swiglu(dz2, w_fo0, gu0, "l0_ffn_out_dx").reshape(N_DEV, t, -1)
    gd_fi0 = _mm_cols_dw(x1b, dgu0, N_DEV, True, BF16, "l0_ffn_in_dw")
    dx1 = _mm_cols_dx(dgu0, g_fi0, True, F32, "l0_ffn_in_dx")
    dz1, g_mix_g0, g_mix_b0 = _ln_bwd(x0, m0, mix_ln_g[0], dz2, ALPHA, dx1, "l0_mix_ln_bwd")
    gd_out0 = _mm(mix, dz1, "tn", BF16, "l0_out_proj_dw").reshape(N_DEV, -1, d)
    dmix = _mm(dz1, w_out0, "nt", F32, "l0_out_proj_dx")
    d_b, d_c, d_h, g_conv = _conv_bwd(p0, conv_w, dmix, w, 3, "l0_conv_bwd")
    doh = heads(dmix[:, :w].astype(BF16))
    dqh, dcq = _attn_dq(qh, kh, vh, cgate, ct, lse, oh, doh, scale, "l0_attn_dq")
    dkh, dvh, dct = _attn_dkv(qh, kh, vh, cgate, ct, lse, oh, doh, "l0_attn_dkv")
    dck = jnp.pad(dct.T, ((0, 0), (0, LANES - nh)))
    dzf, g_bf = _fgate_bwd(p0, bf_pad, dcq, dck, fcol, nh, "l0_fgate_bwd")
    dp0 = jnp.concatenate([unheads(dqh), unheads(dkh), unheads(dvh), d_b, d_c, d_h, dzf], axis=1).astype(BF16)
    g_all0 = _mm(x0, dp0, "tn", F32, "l0_in_proj_dw")
    dx0 = _mm(dp0, w_all0, "nt", F32, "l0_in_proj_dx")
    grad_x = _axpy(ALPHA, dz1, dx0, "grad_x")
    gd_in0 = jnp.concatenate([g_all0[:, :3 * w], g_all0[:, 6 * w:6 * w + nh], g_all0[:, 3 * w:6 * w]], axis=1)
    gd_in0 = gd_in0.reshape(d, N_DEV, -1).transpose(1, 0, 2).astype(BF16)

    big_g = [gd_in0, gd_out0, gd_in1, gd_out1, gd_fi0, gd_fi1, gd_fo0, gd_fo1]
    big_m = [m_even_w_in[0], m_even_w_out[0], m_odd_w_in[0], m_odd_w_out[0],
             m_ffn_w_in[0], m_ffn_w_in[1], m_ffn_w_out[0], m_ffn_w_out[1]]
    big_v = [v_even_w_in[0], v_even_w_out[0], v_odd_w_in[0], v_odd_w_out[0],
             v_ffn_w_in[0], v_ffn_w_in[1], v_ffn_w_out[0], v_ffn_w_out[1]]
    big_names = ["even_w_in", "even_w_out", "odd_w_in", "odd_w_out", "ffn_w_in0", "ffn_w_in1", "ffn_w_out0", "ffn_w_out1"]
    c_idx = mc.reshape(1).astype(jnp.int32)
    chip_idx = (2 * mx + my).reshape(1).astype(jnp.int32)
    from_sib = _rs_sibling_exchange(big_g, "rs_sibling")
    partial = [_rs_chip_sum(g, s, c_idx, "rs_chip_sum_" + n) for g, s, n in zip(big_g, from_sib, big_names)]
    received = _rs_chip_exchange(partial, "rs_chips")
    upd = [_rs_final_adamw(p, r, chip_idx, wt, mt, vt, "rs_final_adamw_" + n)
           for p, r, wt, mt, vt, n in zip(partial, received, big, big_m, big_v, big_names)]
    big_out = {}
    for i, n in enumerate(["even_w_in", "even_w_out", "odd_w_in", "odd_w_out"]):
        big_out[n] = [o[None] for o in upd[i]]
    big_out["ffn_w_in"] = [jnp.stack([a, b]) for a, b in zip(upd[4], upd[5])]
    big_out["ffn_w_out"] = [jnp.stack([a, b]) for a, b in zip(upd[6], upd[7])]

    g_ws = jnp.where(ws_mask, g_wm, 0.0)
    g_bs = g_bs_t[:, :ng].T
    small_g = [g_bf[:nh], g_conv, g_vln_g, g_vln_b, g_ws, g_bs,
               jnp.stack([g_mix_g0, g_mix_g1]), jnp.stack([g_mix_b0, g_mix_b1]),
               jnp.stack([g_ffn_g0, g_ffn_g1]), jnp.stack([g_ffn_b0, g_ffn_b1])]
    small_sum = _all_reduce_small(_lanes(jnp.concatenate([a.reshape(-1) for a in small_g])), "ar_small_grads")
    small_sum = small_sum.reshape(-1)
    outs_small = []
    off = 0
    for a in small_g:
        outs_small.append(small_sum[off:off + a.size].reshape(a.shape))
        off += a.size
    gr_bf, gr_conv, gr_vg, gr_vb, gr_ws, gr_bs, gr_mg, gr_mb, gr_fg, gr_fb = outs_small

    loss = lax.psum(loss_part, ("x", "y", "c"))

    grads = {
        "even_b_f": gr_bf[None],
        "even_conv_w": lax.dynamic_slice(gr_conv, (0, me * cs), (3, cs))[None],
        "odd_v_ln_g": lax.dynamic_slice(gr_vg, (me * vs,), (vs,))[None],
        "odd_v_ln_b": lax.dynamic_slice(gr_vb, (me * vs,), (vs,))[None],
        "odd_w_s": gr_ws[None], "odd_b_s": gr_bs[None],
        "mix_ln_g": gr_mg, "mix_ln_b": gr_mb, "ffn_ln_g": gr_fg, "ffn_ln_b": gr_fb,
    }
    weights = dict(even_w_in=even_w_in, even_b_f=even_b_f, even_conv_w=even_conv_w, even_w_out=even_w_out,
                   odd_w_in=odd_w_in, odd_v_ln_g=odd_v_ln_g, odd_v_ln_b=odd_v_ln_b, odd_w_s=odd_w_s,
                   odd_b_s=odd_b_s, odd_w_out=odd_w_out, mix_ln_g=mix_ln_g, mix_ln_b=mix_ln_b,
                   ffn_w_in=ffn_w_in, ffn_w_out=ffn_w_out, ffn_ln_g=ffn_ln_g, ffn_ln_b=ffn_ln_b)
    moms = dict(even_w_in=(m_even_w_in, v_even_w_in), even_b_f=(m_even_b_f, v_even_b_f),
                even_conv_w=(m_even_conv_w, v_even_conv_w), even_w_out=(m_even_w_out, v_even_w_out),
                odd_w_in=(m_odd_w_in, v_odd_w_in), odd_v_ln_g=(m_odd_v_ln_g, v_odd_v_ln_g),
                odd_v_ln_b=(m_odd_v_ln_b, v_odd_v_ln_b), odd_w_s=(m_odd_w_s, v_odd_w_s),
                odd_b_s=(m_odd_b_s, v_odd_b_s), odd_w_out=(m_odd_w_out, v_odd_w_out),
                mix_ln_g=(m_mix_ln_g, v_mix_ln_g), mix_ln_b=(m_mix_ln_b, v_mix_ln_b),
                ffn_w_in=(m_ffn_w_in, v_ffn_w_in), ffn_w_out=(m_ffn_w_out, v_ffn_w_out),
                ffn_ln_g=(m_ffn_ln_g, v_ffn_ln_g), ffn_ln_b=(m_ffn_ln_b, v_ffn_ln_b))
    names = list(weights)
    gout, deltas, new_m, new_v = [], [], [], []
    for n in names:
        if n in big_out:
            gr, dlt, mn, vn = big_out[n]
        else:
            gr = grads[n]
            dlt, mn, vn = _adamw(weights[n], gr, moms[n][0], moms[n][1], "adamw_" + n)
        gout.append(gr.reshape(weights[n].shape))
        deltas.append(dlt.reshape(weights[n].shape))
        new_m.append(mn.reshape(weights[n].shape))
        new_v.append(vn.reshape(weights[n].shape))
    return (loss, grad_x[None], *gout, *deltas, *new_m, *new_v)
```

```python
import jax
import jax.numpy as jnp
from jax import lax
from jax.experimental import pallas as pl
from jax.experimental.pallas import tpu as pltpu

F32 = jnp.float32
BF16 = jnp.bfloat16
MESH = pl.DeviceIdType.MESH

DEPTH = 2
ALPHA = (2.0 * DEPTH) ** 0.25
LN_EPS = 1e-5
ADAM_LR = 0.001
ADAM_B1 = 0.9
ADAM_B2 = 0.999
ADAM_EPS = 1e-08
ADAM_WD = 0.01
ADAM_STEP = 10

N_DEV = 8
LANES = 128
SUBLANES = 8
VMEM_LIMIT = 48 * 1024 * 1024
NEG_BIG = -1e30
ROW_TILES = (512, 256, 128)


def _pick(n, cands):
    for c in cands:
        if c <= n and n % c == 0:
            return c
    return n


def _params(*sem):
    return pltpu.CompilerParams(dimension_semantics=sem, vmem_limit_bytes=VMEM_LIMIT)


NN = (((1,), (0,)), ((), ()))
NT = (((1,), (1,)), ((), ()))
TN = (((0,), (0,)), ((), ()))
M_TILES = (1024, 512, 1408, 256, 128)
N_TILES = (512, 640, 256, 128)
K_TILES = (1024, 512, 640, 1408, 256, 128)


def _mm_core(name, grid, a, b, a_spec, b_spec, o_spec, o_shape, o_dtype, dims, tile):
    nred = grid[2]

    def body(a_ref, b_ref, o_ref, *acc):
        part = lax.dot_general(a_ref[...].astype(BF16), b_ref[...].astype(BF16), dims, preferred_element_type=F32)
        if nred == 1:
            o_ref[...] = part.astype(o_ref.dtype)
            return
        acc_ref, = acc
        kk = pl.program_id(2)

        @pl.when(kk == 0)
        def _():
            acc_ref[...] = jnp.zeros_like(acc_ref)

        acc_ref[...] += part

        @pl.when(kk == nred - 1)
        def _():
            o_ref[...] = acc_ref[...].astype(o_ref.dtype)

    return pl.pallas_call(
        body, name=name, grid=grid, in_specs=[a_spec, b_spec], out_specs=o_spec,
        out_shape=jax.ShapeDtypeStruct(o_shape, o_dtype),
        scratch_shapes=[] if nred == 1 else [pltpu.VMEM(tile, F32)],
        compiler_params=_params("parallel", "parallel", "arbitrary"),
    )(a, b)


def _mm(a, b, mode, out_dtype, name):
    if mode == "nn":
        (m, k), (k2, n) = a.shape, b.shape
    elif mode == "nt":
        (m, k), (n, k2) = a.shape, b.shape
    else:
        (k, m), (k2, n) = a.shape, b.shape
    assert k == k2, (a.shape, b.shape, mode)
    tm, tn, tk = _pick(m, M_TILES), _pick(n, N_TILES), _pick(k, K_TILES)
    if mode == "nn":
        a_spec = pl.BlockSpec((tm, tk), lambda i, j, kk: (i, kk))
        b_spec = pl.BlockSpec((tk, tn), lambda i, j, kk: (kk, j))
        dims = NN
    elif mode == "nt":
        a_spec = pl.BlockSpec((tm, tk), lambda i, j, kk: (i, kk))
        b_spec = pl.BlockSpec((tn, tk), lambda i, j, kk: (j, kk))
        dims = NT
    else:
        a_spec = pl.BlockSpec((tk, tm), lambda i, j, kk: (kk, i))
        b_spec = pl.BlockSpec((tk, tn), lambda i, j, kk: (kk, j))
        dims = TN
    return _mm_core(name, (m // tm, n // tn, k // tk), a, b, a_spec, b_spec,
                    pl.BlockSpec((tm, tn), lambda i, j, kk: (i, j)), (m, n), out_dtype, dims, (tm, tn))


def _act_spec(blocked, rows, ns, row_ax, d_ax):
    if blocked:
        return pl.BlockSpec((None, rows, ns), lambda *g: (g[d_ax], g[row_ax], 0))
    return pl.BlockSpec((rows, ns), lambda *g: (g[row_ax], g[d_ax]))


def _mm_cols_fwd(a, g3, blocked, out_dtype, name):
    (t, k), (nd, k2, ns) = a.shape, g3.shape
    assert k == k2
    tm, tk = _pick(t, M_TILES), _pick(k, K_TILES)
    return _mm_core(name, (t // tm, nd, k // tk), a, g3,
                    pl.BlockSpec((tm, tk), lambda i, d, kk: (i, kk)),
                    pl.BlockSpec((None, tk, ns), lambda i, d, kk: (d, kk, 0)),
                    _act_spec(blocked, tm, ns, 0, 1), (nd, t, ns) if blocked else (t, nd * ns), out_dtype, NN, (tm, ns))


def _mm_cols_dx(dy, g3, blocked, out_dtype, name):
    nd, k, ns = g3.shape
    t = dy.shape[1] if blocked else dy.shape[0]
    tm, tn = _pick(t, M_TILES), _pick(k, (1024,) + N_TILES)
    return _mm_core(name, (t // tm, k // tn, nd), dy, g3,
                    _act_spec(blocked, tm, ns, 0, 2),
                    pl.BlockSpec((None, tn, ns), lambda i, j, d: (d, j, 0)),
                    pl.BlockSpec((tm, tn), lambda i, j, d: (i, j)), (t, k), out_dtype, NT, (tm, tn))


def _mm_cols_dw(a, dy, nd, blocked, out_dtype, name):
    t, k = a.shape
    ns = dy.shape[2] if blocked else dy.shape[1] // nd
    tmk, tk = _pick(k, M_TILES), _pick(t, K_TILES)
    return _mm_core(name, (nd, k // tmk, t // tk), a, dy,
                    pl.BlockSpec((tk, tmk), lambda d, j, kk: (kk, j)),
                    _act_spec(blocked, tk, ns, 2, 0),
                    pl.BlockSpec((None, tmk, ns), lambda d, j, kk: (d, j, 0)), (nd, k, ns), out_dtype, TN, (tmk, ns))


def _mm_blk_fwd(h3, w, out_dtype, name):
    (nb, t, ns), (_, n) = h3.shape, w.shape
    tm, tn = _pick(t, M_TILES), _pick(n, (1024,) + N_TILES)
    return _mm_core(name, (t // tm, n // tn, nb), h3, w,
                    pl.BlockSpec((None, tm, ns), lambda i, j, d: (d, i, 0)),
                    pl.BlockSpec((ns, tn), lambda i, j, d: (d, j)),
                    pl.BlockSpec((tm, tn), lambda i, j, d: (i, j)), (t, n), out_dtype, NN, (tm, tn))


def _mm_blk_dw(h3, dz, out_dtype, name):
    (nb, t, ns), (_, n) = h3.shape, dz.shape
    tn, tk = _pick(n, (1024,) + N_TILES), _pick(t, K_TILES)
    return _mm_core(name, (nb, n // tn, t // tk), h3, dz,
                    pl.BlockSpec((None, tk, ns), lambda d, j, kk: (d, kk, 0)),
                    pl.BlockSpec((tk, tn), lambda d, j, kk: (kk, j)),
                    pl.BlockSpec((ns, tn), lambda d, j, kk: (d, j)), (nb * ns, n), out_dtype, TN, (ns, tn))


def _ln_fwd(xa, xb, g, b, name):
    t, d = xa.shape
    tb = _pick(t, ROW_TILES)

    def body(xa_ref, xb_ref, g_ref, b_ref, y_ref, yb_ref):
        z = ALPHA * xa_ref[...] + xb_ref[...]
        mu = jnp.mean(z, axis=-1, keepdims=True)
        zc = z - mu
        var = jnp.mean(zc * zc, axis=-1, keepdims=True)
        y = zc * lax.rsqrt(var + LN_EPS) * g_ref[...] + b_ref[...]
        y_ref[...] = y
        yb_ref[...] = y.astype(BF16)

    row = pl.BlockSpec((tb, d), lambda i: (i, 0))
    vec = pl.BlockSpec((1, d), lambda i: (0, 0))
    return pl.pallas_call(
        body, name=name, grid=(t // tb,),
        in_specs=[row, row, vec, vec], out_specs=[row, row],
        out_shape=[jax.ShapeDtypeStruct((t, d), F32), jax.ShapeDtypeStruct((t, d), BF16)],
        compiler_params=_params("parallel"),
    )(xa, xb, g.reshape(1, d), b.reshape(1, d))


def _ln_bwd(xa, xb, g, dya, ca, dyb, name):
    t, d = xa.shape
    tb = _pick(t, ROW_TILES)
    two = dyb is not None

    def body(*refs):
        if two:
            xa_ref, xb_ref, g_ref, dya_ref, dyb_ref, dz_ref, dg_ref, db_ref = refs
            dy = ca * dya_ref[...] + dyb_ref[...]
        else:
            xa_ref, xb_ref, g_ref, dya_ref, dz_ref, dg_ref, db_ref = refs
            dy = ca * dya_ref[...]
        z = ALPHA * xa_ref[...] + xb_ref[...]
        mu = jnp.mean(z, axis=-1, keepdims=True)
        zc = z - mu
        var = jnp.mean(zc * zc, axis=-1, keepdims=True)
        rstd = lax.rsqrt(var + LN_EPS)
        xhat = zc * rstd
        dxh = dy * g_ref[...]
        m1 = jnp.mean(dxh, axis=-1, keepdims=True)
        m2 = jnp.mean(dxh * xhat, axis=-1, keepdims=True)
        dz_ref[...] = rstd * (dxh - m1 - xhat * m2)

        @pl.when(pl.program_id(0) == 0)
        def _():
            dg_ref[...] = jnp.zeros_like(dg_ref)
            db_ref[...] = jnp.zeros_like(db_ref)

        dg_ref[...] += jnp.sum(dy * xhat, axis=0, keepdims=True)
        db_ref[...] += jnp.sum(dy, axis=0, keepdims=True)

    row = pl.BlockSpec((tb, d), lambda i: (i, 0))
    vec = pl.BlockSpec((1, d), lambda i: (0, 0))
    ins = [xa, xb, g.reshape(1, d), dya] + ([dyb] if two else [])
    dz, dg, db = pl.pallas_call(
        body, name=name, grid=(t // tb,),
        in_specs=[row, row, vec, row] + ([row] if two else []),
        out_specs=[row, vec, vec],
        out_shape=[jax.ShapeDtypeStruct((t, d), F32), jax.ShapeDtypeStruct((1, d), F32),
                   jax.ShapeDtypeStruct((1, d), F32)],
        compiler_params=_params("arbitrary"),
    )(*ins)
    return dz, dg[0], db[0]


def _loss(y, target, name):
    t, d = y.shape
    tb = _pick(t, ROW_TILES)

    def body(y_ref, t_ref, dy_ref, l_ref):
        e = y_ref[...] - t_ref[...]
        dy_ref[...] = e * (1.0 / d)

        @pl.when(pl.program_id(0) == 0)
        def _():
            l_ref[...] = jnp.zeros_like(l_ref)

        l_ref[...] += 0.5 * jnp.sum(jnp.mean(e * e, axis=-1, keepdims=True))

    row = pl.BlockSpec((tb, d), lambda i: (i, 0))
    dy, l = pl.pallas_call(
        body, name=name, grid=(t // tb,),
        in_specs=[row, row], out_specs=[row, pl.BlockSpec((1, LANES), lambda i: (0, 0))],
        out_shape=[jax.ShapeDtypeStruct((t, d), F32), jax.ShapeDtypeStruct((1, LANES), F32)],
        compiler_params=_params("arbitrary"),
    )(y, target)
    return l[0, 0], dy


def _axpy(ca, a, b, name):
    t, d = a.shape
    tb = _pick(t, ROW_TILES)

    def body(a_ref, b_ref, o_ref):
        o_ref[...] = ca * a_ref[...] + b_ref[...]

    row = pl.BlockSpec((tb, d), lambda i: (i, 0))
    return pl.pallas_call(
        body, name=name, grid=(t // tb,), in_specs=[row, row], out_specs=row,
        out_shape=jax.ShapeDtypeStruct((t, d), F32), compiler_params=_params("parallel"),
    )(a, b)


def _ffn_in_swiglu(xb, g4, name):
    (t, k), (_, nb, _, ns) = xb.shape, g4.shape
    tm = _pick(t, M_TILES)

    def body(x_ref, w_ref, h_ref, gu_ref):
        xv = x_ref[...]
        gate = jnp.dot(xv, w_ref[0], preferred_element_type=F32)
        up = jnp.dot(xv, w_ref[1], preferred_element_type=F32)
        h_ref[...] = (gate * jax.nn.sigmoid(gate) * up).astype(BF16)
        gu_ref[0] = gate.astype(BF16)
        gu_ref[1] = up.astype(BF16)

    return pl.pallas_call(
        body, name=name, grid=(t // tm, nb),
        in_specs=[pl.BlockSpec((tm, k), lambda i, d: (i, 0)),
                  pl.BlockSpec((2, None, k, ns), lambda i, d: (0, d, 0, 0))],
        out_specs=[pl.BlockSpec((None, tm, ns), lambda i, d: (d, i, 0)),
                   pl.BlockSpec((2, None, tm, ns), lambda i, d: (0, d, i, 0))],
        out_shape=[jax.ShapeDtypeStruct((nb, t, ns), BF16), jax.ShapeDtypeStruct((2, nb, t, ns), BF16)],
        compiler_params=_params("parallel", "parallel"),
    )(xb, g4)


def _ffn_out_dx_swiglu(dz, w_out, gu4, name):
    (t, d), (_, nb, _, ns) = dz.shape, gu4.shape
    tm = _pick(t, M_TILES)

    def body(dz_ref, w_ref, gu_ref, o_ref):
        dh = lax.dot_general(dz_ref[...].astype(BF16), w_ref[...], NT, preferred_element_type=F32)
        gate = gu_ref[0].astype(F32)
        up = gu_ref[1].astype(F32)
        sg = jax.nn.sigmoid(gate)
        silu = gate * sg
        o_ref[0] = (dh * up * (sg + silu * (1.0 - sg))).astype(BF16)
        o_ref[1] = (dh * silu).astype(BF16)

    blk = pl.BlockSpec((2, None, tm, ns), lambda i, j: (0, j, i, 0))
    return pl.pallas_call(
        body, name=name, grid=(t // tm, nb),
        in_specs=[pl.BlockSpec((tm, d), lambda i, j: (i, 0)), pl.BlockSpec((ns, d), lambda i, j: (j, 0)), blk],
        out_specs=blk,
        out_shape=jax.ShapeDtypeStruct((2, nb, t, ns), BF16),
        compiler_params=_params("parallel", "parallel"),
    )(dz, w_out, gu4)


def _tri_matmul(tri, x):
    x1 = x.astype(BF16)
    r1 = x - x1.astype(F32)
    x2 = r1.astype(BF16)
    x3 = (r1 - x2.astype(F32)).astype(BF16)
    dot = lambda v: jnp.dot(tri, v, preferred_element_type=F32)
    return dot(x1) + dot(x2) + dot(x3)


def _fgate_fwd(proj, bf_pad, fcol, n_heads, name):
    t = proj.shape[0]
    tb = _pick(t, ROW_TILES)

    def body(p_ref, b_ref, c_ref, carry):
        @pl.when(pl.program_id(0) == 0)
        def _():
            carry[...] = jnp.zeros_like(carry)

        z = p_ref[...] + b_ref[...]
        lf = jnp.minimum(z, 0.0) - jnp.log1p(jnp.exp(-jnp.abs(z)))
        lane = lax.broadcasted_iota(jnp.int32, (tb, LANES), 1)
        lf = jnp.where(lane < n_heads, lf, 0.0)
        r = lax.broadcasted_iota(jnp.int32, (tb, tb), 0)
        s = lax.broadcasted_iota(jnp.int32, (tb, tb), 1)
        tri = (s <= r).astype(BF16)
        c = _tri_matmul(tri, lf) + carry[...]
        c_ref[...] = c
        carry[...] = c[tb - 1:tb, :]

    return pl.pallas_call(
        body, name=name, grid=(t // tb,),
        in_specs=[pl.BlockSpec((tb, LANES), lambda i: (i, fcol)), pl.BlockSpec((1, LANES), lambda i: (0, 0))],
        out_specs=pl.BlockSpec((tb, LANES), lambda i: (i, 0)),
        out_shape=jax.ShapeDtypeStruct((t, LANES), F32),
        scratch_shapes=[pltpu.VMEM((1, LANES), F32)],
        compiler_params=_params("arbitrary"),
    )(proj, bf_pad)


def _fgate_bwd(proj, bf_pad, dcq, dck, fcol, n_heads, name):
    t = proj.shape[0]
    tb = _pick(t, ROW_TILES)
    nb = t // tb

    def body(p_ref, b_ref, dcq_ref, dck_ref, dz_ref, db_ref, carry):
        @pl.when(pl.program_id(0) == 0)
        def _():
            carry[...] = jnp.zeros_like(carry)
            db_ref[...] = jnp.zeros_like(db_ref)

        r = lax.broadcasted_iota(jnp.int32, (tb, tb), 0)
        s = lax.broadcasted_iota(jnp.int32, (tb, tb), 1)
        tri = (s >= r).astype(BF16)
        dlf = _tri_matmul(tri, dcq_ref[...] + dck_ref[...]) + carry[...]
        carry[...] = dlf[0:1, :]
        z = p_ref[...] + b_ref[...]
        lane = lax.broadcasted_iota(jnp.int32, (tb, LANES), 1)
        dz = jnp.where(lane < n_heads, dlf * jax.nn.sigmoid(-z), 0.0)
        dz_ref[...] = dz
        db_ref[...] += jnp.sum(dz, axis=0, keepdims=True)

    dz, db = pl.pallas_call(
        body, name=name, grid=(nb,),
        in_specs=[pl.BlockSpec((tb, LANES), lambda i: (nb - 1 - i, fcol)),
                  pl.BlockSpec((1, LANES), lambda i: (0, 0)),
                  pl.BlockSpec((tb, LANES), lambda i: (nb - 1 - i, 0)),
                  pl.BlockSpec((tb, LANES), lambda i: (nb - 1 - i, 0))],
        out_specs=[pl.BlockSpec((tb, LANES), lambda i: (nb - 1 - i, 0)),
                   pl.BlockSpec((1, LANES), lambda i: (0, 0))],
        out_shape=[jax.ShapeDtypeStruct((t, LANES), F32), jax.ShapeDtypeStruct((1, LANES), F32)],
        scratch_shapes=[pltpu.VMEM((1, LANES), F32)],
        compiler_params=_params("arbitrary"),
    )(proj, bf_pad, dcq, dck)
    return dz, db[0]


def _split3(x):
    hi = x.astype(BF16)
    r = x - hi.astype(F32)
    mid = r.astype(BF16)
    return hi, mid, (r - mid.astype(F32)).astype(BF16)


def _attn_fwd(qa, ka, va, dh, name):
    nh, t, da = qa.shape
    tq = _pick(t, ROW_TILES)

    def body(q_ref, k_ref, v_ref, o_ref, m_s, acc_s):
        qi = pl.program_id(1)
        q = q_ref[...]
        m_s[...] = jnp.full(m_s.shape, NEG_BIG, F32)
        acc_s[...] = jnp.zeros_like(acc_s)

        def step(j, diagonal):
            off = pl.multiple_of(j * tq, tq)
            s = lax.dot_general(q, k_ref[pl.ds(off, tq), :], NT, preferred_element_type=F32)
            if diagonal:
                row = lax.broadcasted_iota(jnp.int32, (tq, tq), 0)
                col = lax.broadcasted_iota(jnp.int32, (tq, tq), 1)
                s = jnp.where(col > row, NEG_BIG, s)
            m_prev = m_s[...]
            m_new = jnp.maximum(m_prev, jnp.max(s, axis=1, keepdims=True))
            p = jnp.exp(s - m_new)
            acc_s[...] = jnp.exp(m_prev - m_new) * acc_s[...] + jnp.dot(
                p.astype(BF16), v_ref[pl.ds(off, tq), :], preferred_element_type=F32)
            m_s[...] = m_new

        def loop(j, carry):
            step(j, False)
            return carry

        lax.fori_loop(0, qi, loop, 0)
        step(qi, True)
        acc = acc_s[...]
        lane = lax.broadcasted_iota(jnp.int32, (tq, da), 1)
        l = jnp.sum(jnp.where(lane == dh, acc, 0.0), axis=1, keepdims=True)
        o_ref[...] = jnp.where(lane == dh, m_s[...] + jnp.log(l), acc / l)

    full = pl.BlockSpec((None, t, da), lambda h, qi: (h, 0, 0))
    blk = pl.BlockSpec((None, tq, da), lambda h, qi: (h, qi, 0))
    return pl.pallas_call(
        body, name=name, grid=(nh, t // tq),
        in_specs=[blk, full, full], out_specs=blk,
        out_shape=jax.ShapeDtypeStruct((nh, t, da), F32),
        scratch_shapes=[pltpu.VMEM((tq, 1), F32), pltpu.VMEM((tq, da), F32)],
        compiler_params=_params("parallel", "arbitrary"),
    )(qa, ka, va)


def _attn_bwd(qa, ka, va, doa, name):
    nh, t, da = qa.shape
    tq = _pick(t, ROW_TILES)
    nq = t // tq

    def body(q_ref, do_ref, k_ref, v_ref, dq_ref, dk_ref, dv_ref):
        kj = pl.program_id(1)

        @pl.when(kj == 0)
        def _():
            dq_ref[...] = jnp.zeros_like(dq_ref)

        dk_ref[...] = jnp.zeros_like(dk_ref)
        dv_ref[...] = jnp.zeros_like(dv_ref)
        kb = k_ref[...]
        vb = v_ref[...]

        def step(i, diagonal):
            off = pl.multiple_of(i * tq, tq)
            qb = q_ref[pl.ds(off, tq), :]
            dob = do_ref[pl.ds(off, tq), :]
            st = lax.dot_general(kb, qb, NT, preferred_element_type=F32)
            if diagonal:
                row = lax.broadcasted_iota(jnp.int32, (tq, tq), 0)
                col = lax.broadcasted_iota(jnp.int32, (tq, tq), 1)
                st = jnp.where(row > col, NEG_BIG, st)
            pt = jnp.exp(st)
            dst = (pt * lax.dot_general(vb, dob, NT, preferred_element_type=F32)).astype(BF16)
            dv_ref[...] += jnp.dot(pt.astype(BF16), dob, preferred_element_type=F32)
            dk_ref[...] += jnp.dot(dst, qb, preferred_element_type=F32)
            dq_ref[pl.ds(off, tq), :] += lax.dot_general(dst, kb, TN, preferred_element_type=F32)

        def loop(i, carry):
            step(i, False)
            return carry

        step(kj, True)
        lax.fori_loop(kj + 1, nq, loop, 0)

    full = pl.BlockSpec((None, t, da), lambda h, j: (h, 0, 0))
    blk = pl.BlockSpec((None, tq, da), lambda h, j: (h, j, 0))
    return pl.pallas_call(
        body, name=name, grid=(nh, nq),
        in_specs=[full, full, blk, blk], out_specs=[full, blk, blk],
        out_shape=[jax.ShapeDtypeStruct((nh, t, da), F32)] * 3,
        compiler_params=_params("parallel", "arbitrary"),
    )(qa, doa, ka, va)


def _attn_delta(do, o, nh, name):
    t, w = do.shape
    dh = w // nh
    tb = _pick(t, ROW_TILES)

    def body(do_ref, o_ref, d_ref):
        prod = do_ref[...] * o_ref[...]
        r = lax.broadcasted_iota(jnp.int32, (w, LANES), 0)
        c = lax.broadcasted_iota(jnp.int32, (w, LANES), 1)
        seg = jnp.logical_and(r >= c * dh, r < (c + 1) * dh).astype(BF16)
        acc = jnp.zeros((tb, LANES), F32)
        for term in _split3(prod):
            acc = acc + jnp.dot(term, seg, preferred_element_type=F32)
        d_ref[...] = acc

    row = pl.BlockSpec((tb, w), lambda i: (i, 0))
    return pl.pallas_call(
        body, name=name, grid=(t // tb,), in_specs=[row, row],
        out_specs=pl.BlockSpec((tb, LANES), lambda i: (i, 0)),
        out_shape=jax.ShapeDtypeStruct((t, LANES), F32), compiler_params=_params("parallel"),
    )(do, o)


def _conv_fwd(proj, cw, w, bcol, name):
    t = proj.shape[0]
    tb = _pick(t, ROW_TILES)
    hb = tb // SUBLANES

    def body(b_ref, c_ref, h_ref, cp_ref, hp_ref, w_ref, y_ref):
        i = pl.program_id(0)
        zp = jnp.where(i > 0, cp_ref[...] * hp_ref[...], 0.0)
        zext = jnp.concatenate([zp, c_ref[...] * h_ref[...]], axis=0)
        z1 = pltpu.roll(zext, 1, 0)[SUBLANES:]
        z2 = pltpu.roll(zext, 2, 0)[SUBLANES:]
        y = w_ref[2:3, :] * zext[SUBLANES:] + w_ref[1:2, :] * z1 + w_ref[0:1, :] * z2
        y_ref[...] = (b_ref[...] * y).astype(BF16)

    cur = lambda j: pl.BlockSpec((tb, w), lambda i: (i, bcol + j))
    prev = lambda j: pl.BlockSpec((SUBLANES, w), lambda i: (jnp.maximum(i * hb - 1, 0), bcol + j))
    return pl.pallas_call(
        body, name=name, grid=(t // tb,),
        in_specs=[cur(0), cur(1), cur(2), prev(1), prev(2), pl.BlockSpec(cw.shape, lambda i: (0, 0))],
        out_specs=pl.BlockSpec((tb, w), lambda i: (i, 0)),
        out_shape=jax.ShapeDtypeStruct((t, w), BF16), compiler_params=_params("parallel"),
    )(proj, proj, proj, proj, proj, cw)


def _conv_bwd(proj, cw, dmix, w, bcol, name):
    t = proj.shape[0]
    tb = _pick(t, ROW_TILES)
    hb = tb // SUBLANES
    nb = t // tb
    n_ext = tb + SUBLANES

    def body(b_ref, c_ref, h_ref, cp_ref, hp_ref, bn_ref, d_ref, dn_ref, w_ref, db_ref, dc_ref, dh_ref, dw_ref):
        i = pl.program_id(0)
        c = c_ref[...]
        hh = h_ref[...]
        zp = jnp.where(i > 0, cp_ref[...] * hp_ref[...], 0.0)
        zext = jnp.concatenate([zp, c * hh], axis=0)
        z0 = zext[SUBLANES:]
        z1 = pltpu.roll(zext, 1, 0)[SUBLANES:]
        z2 = pltpu.roll(zext, 2, 0)[SUBLANES:]
        y = w_ref[2:3, :] * z0 + w_ref[1:2, :] * z1 + w_ref[0:1, :] * z2
        d = d_ref[...]
        db_ref[...] = d * y
        dy = d * b_ref[...]
        dyn = jnp.where(i < nb - 1, dn_ref[...] * bn_ref[...], 0.0)
        dext = jnp.concatenate([dy, dyn], axis=0)
        dy1 = pltpu.roll(dext, n_ext - 1, 0)[:tb]
        dy2 = pltpu.roll(dext, n_ext - 2, 0)[:tb]
        dz = w_ref[2:3, :] * dy + w_ref[1:2, :] * dy1 + w_ref[0:1, :] * dy2
        dc_ref[...] = dz * hh
        dh_ref[...] = dz * c

        @pl.when(i == 0)
        def _():
            dw_ref[...] = jnp.zeros_like(dw_ref)

        dw_ref[0:1, :] += jnp.sum(dy * z2, axis=0, keepdims=True)
        dw_ref[1:2, :] += jnp.sum(dy * z1, axis=0, keepdims=True)
        dw_ref[2:3, :] += jnp.sum(dy * z0, axis=0, keepdims=True)

    cur = lambda j: pl.BlockSpec((tb, w), lambda i: (i, bcol + j))
    prev = lambda j: pl.BlockSpec((SUBLANES, w), lambda i: (jnp.maximum(i * hb - 1, 0), bcol + j))
    nxt = lambda col: pl.BlockSpec((SUBLANES, w), lambda i: (jnp.minimum((i + 1) * hb, nb * hb - 1), col))
    out = pl.BlockSpec((tb, w), lambda i: (i, 0))
    return pl.pallas_call(
        body, name=name, grid=(nb,),
        in_specs=[cur(0), cur(1), cur(2), prev(1), prev(2), nxt(bcol),
                  pl.BlockSpec((tb, w), lambda i: (i, 1)), nxt(1), pl.BlockSpec(cw.shape, lambda i: (0, 0))],
        out_specs=[out, out, out, pl.BlockSpec(cw.shape, lambda i: (0, 0))],
        out_shape=[jax.ShapeDtypeStruct((t, w), F32)] * 3 + [jax.ShapeDtypeStruct(cw.shape, F32)],
        compiler_params=_params("arbitrary"),
    )(proj, proj, proj, proj, proj, proj, dmix, dmix, cw)


SQRT_HALF = 0.7071067811865476
INV_SQRT_2PI = 0.3989422804014327


def _gelu(x):
    return 0.5 * x * (1.0 + lax.erf(x * SQRT_HALF))


def _gelu_grad(x):
    return 0.5 * (1.0 + lax.erf(x * SQRT_HALF)) + x * (INV_SQRT_2PI * jnp.exp(-0.5 * x * x))


def _sgu_fwd(uv, ln_g, ln_b, wm, bs_full, name):
    t, d2 = uv.shape
    d = d2 // 2
    ng, pb, _ = wm.shape
    gd = d // ng
    tb = _pick(t, ROW_TILES[1:] or ROW_TILES)
    assert tb % pb == 0

    def body(uv_ref, g_ref, b_ref, w_ref, bs_ref, o_ref):
        u = _gelu(uv_ref[:, :d])
        v = _gelu(uv_ref[:, d:])
        mu = jnp.mean(v, axis=-1, keepdims=True)
        vc = v - mu
        var = jnp.mean(vc * vc, axis=-1, keepdims=True)
        vn = (vc * lax.rsqrt(var + LN_EPS) * g_ref[...] + b_ref[...]).astype(BF16)
        for r in range(tb // pb):
            rows = slice(r * pb, (r + 1) * pb)
            for gi in range(ng):
                cols = slice(gi * gd, (gi + 1) * gd)
                s = jnp.dot(w_ref[gi], vn[rows, cols], preferred_element_type=F32) + bs_ref[:, cols]
                o_ref[rows, cols] = (u[rows, cols] * s).astype(BF16)

    vec = pl.BlockSpec((1, d), lambda i: (0, 0))
    return pl.pallas_call(
        body, name=name, grid=(t // tb,),
        in_specs=[pl.BlockSpec((tb, d2), lambda i: (i, 0)), vec, vec,
                  pl.BlockSpec(wm.shape, lambda i: (0, 0, 0)), pl.BlockSpec((pb, d), lambda i: (0, 0))],
        out_specs=pl.BlockSpec((tb, d), lambda i: (i, 0)),
        out_shape=jax.ShapeDtypeStruct((t, d), BF16), compiler_params=_params("parallel"),
    )(uv, ln_g.reshape(1, d), ln_b.reshape(1, d), wm, bs_full)


def _sgu_bwd(uv, ln_g, ln_b, wm, bs_full, dgated, name):
    t, d2 = uv.shape
    d = d2 // 2
    ng, pb, _ = wm.shape
    gd = d // ng
    tb = _pick(t, ROW_TILES[1:] or ROW_TILES)
    nb = t // tb

    def body(uv_ref, g_ref, b_ref, w_ref, bs_ref, dg_ref, o_ref, dw_ref, dbs_ref, dlg_ref, dlb_ref,
             du_s, dvn_s, dbs_s):
        i = pl.program_id(0)

        @pl.when(i == 0)
        def _():
            dw_ref[...] = jnp.zeros_like(dw_ref)
            dbs_s[...] = jnp.zeros_like(dbs_s)
            dlg_ref[...] = jnp.zeros_like(dlg_ref)
            dlb_ref[...] = jnp.zeros_like(dlb_ref)

        upre = uv_ref[:, :d]
        vpre = uv_ref[:, d:]
        u = _gelu(upre)
        v = _gelu(vpre)
        mu = jnp.mean(v, axis=-1, keepdims=True)
        vc = v - mu
        var = jnp.mean(vc * vc, axis=-1, keepdims=True)
        rstd = lax.rsqrt(var + LN_EPS)
        xhat = vc * rstd
        vn = (xhat * g_ref[...] + b_ref[...]).astype(BF16)
        dgt = dg_ref[...].astype(F32)
        for r in range(tb // pb):
            rows = slice(r * pb, (r + 1) * pb)
            for gi in range(ng):
                cols = slice(gi * gd, (gi + 1) * gd)
                vblk = vn[rows, cols]
                s = jnp.dot(w_ref[gi], vblk, preferred_element_type=F32) + bs_ref[:, cols]
                dblk = dgt[rows, cols]
                du_s[rows, cols] = dblk * s
                ds = dblk * u[rows, cols]
                dsb = ds.astype(BF16)
                dvn_s[rows, cols] = lax.dot_general(w_ref[gi], dsb, (((0,), (0,)), ((), ())),
                                                    preferred_element_type=F32)
                dw_ref[gi] += lax.dot_general(dsb, vblk, (((1,), (1,)), ((), ())), preferred_element_type=F32)
                dbs_s[:, cols] += ds
        dvn = dvn_s[...]
        dlg_ref[...] += jnp.sum(dvn * xhat, axis=0, keepdims=True)
        dlb_ref[...] += jnp.sum(dvn, axis=0, keepdims=True)
        dxh = dvn * g_ref[...]
        m1 = jnp.mean(dxh, axis=-1, keepdims=True)
        m2 = jnp.mean(dxh * xhat, axis=-1, keepdims=True)
        dv = rstd * (dxh - m1 - xhat * m2)
        o_ref[:, :d] = (du_s[...] * _gelu_grad(upre)).astype(BF16)
        o_ref[:, d:] = (dv * _gelu_grad(vpre)).astype(BF16)

        @pl.when(i == nb - 1)
        def _():
            lane = lax.broadcasted_iota(jnp.int32, (pb, LANES), 1)
            acc = jnp.zeros((pb, LANES), F32)
            for gi in range(ng):
                col = jnp.sum(dbs_s[:, gi * gd:(gi + 1) * gd], axis=1, keepdims=True)
                acc = acc + jnp.where(lane == gi, col, 0.0)
            dbs_ref[...] = acc

    vec = pl.BlockSpec((1, d), lambda i: (0, 0))
    duv, dw, dbs, dlg, dlb = pl.pallas_call(
        body, name=name, grid=(nb,),
        in_specs=[pl.BlockSpec((tb, d2), lambda i: (i, 0)), vec, vec,
                  pl.BlockSpec(wm.shape, lambda i: (0, 0, 0)), pl.BlockSpec((pb, d), lambda i: (0, 0)),
                  pl.BlockSpec((tb, d), lambda i: (i, 0))],
        out_specs=[pl.BlockSpec((tb, d2), lambda i: (i, 0)), pl.BlockSpec(wm.shape, lambda i: (0, 0, 0)),
                   pl.BlockSpec((pb, LANES), lambda i: (0, 0)), vec, vec],
        out_shape=[jax.ShapeDtypeStruct((t, d2), BF16), jax.ShapeDtypeStruct(wm.shape, F32),
                   jax.ShapeDtypeStruct((pb, LANES), F32), jax.ShapeDtypeStruct((1, d), F32),
                   jax.ShapeDtypeStruct((1, d), F32)],
        scratch_shapes=[pltpu.VMEM((tb, d), F32), pltpu.VMEM((tb, d), F32), pltpu.VMEM((pb, d), F32)],
        compiler_params=_params("arbitrary"),
    )(uv, ln_g.reshape(1, d), ln_b.reshape(1, d), wm, bs_full, dgated)
    return duv, dw, dbs, dlg[0], dlb[0]


def _adamw(w, g, m, v, name):
    shape = w.shape
    cols = shape[-1]
    rows = w.size // cols
    tr = _pick(rows, (512, 256, 352, 128, 64, 32, 16, 8))

    def body(w_ref, g_ref, m_ref, v_ref, d_ref, mo_ref, vo_ref):
        d_ref[...], mo_ref[...], vo_ref[...] = _adam_update(w_ref[...], g_ref[...], m_ref[...], v_ref[...])

    spec = pl.BlockSpec((tr, cols), lambda i: (i, 0))
    outs = pl.pallas_call(
        body, name=name, grid=(rows // tr,),
        in_specs=[spec] * 4, out_specs=[spec] * 3,
        out_shape=[jax.ShapeDtypeStruct((rows, cols), F32)] * 3,
        compiler_params=_params("parallel"),
    )(*[a.reshape(rows, cols) for a in (w, g, m, v)])
    return [o.reshape(shape) for o in outs]


ANY = pl.BlockSpec(memory_space=pl.ANY)


def _place():
    return lax.axis_index("x"), lax.axis_index("y"), lax.axis_index("c")


def _all_gather(shards, name):
    n = len(shards)

    def body(*refs):
        x_refs, out_refs = refs[:n], refs[n:2 * n]
        send_sems, recv_sems, local_sems = refs[2 * n:]
        x, y, c = _place()
        me, sibling = (x, y, c), (x, y, 1 - c)
        chips = [(1 - x, y), (x, 1 - y), (1 - x, 1 - y)]

        def copy(a, k, block, to, own=False):
            px, py, pc = block
            rows = out_refs[a].at[4 * px + 2 * py + pc]
            return pltpu.make_async_remote_copy(
                src_ref=x_refs[a] if own else rows, dst_ref=rows,
                send_sem=send_sems.at[7 * a + k], recv_sem=recv_sems.at[7 * a + k],
                device_id=to, device_id_type=MESH)

        started = []
        for a in range(n):
            mine = pltpu.make_async_copy(x_refs[a], out_refs[a].at[4 * x + 2 * y + c], local_sems.at[a])
            mine.start()
            started.append(mine)
        sends = []
        for a in range(n):
            first = [copy(a, 0, me, sibling, own=True)]
            first += [copy(a, 1 + j, me, (*chip, c), own=True) for j, chip in enumerate(chips)]
            for cp in first:
                cp.start()
            sends += first
        for j, chip in enumerate(chips):
            for a in range(n):
                copy(a, 1 + j, (*chip, c), me).wait_recv()
                fwd = copy(a, 4 + j, (*chip, c), sibling)
                fwd.start()
                sends.append(fwd)
        for a in range(n):
            copy(a, 0, sibling, me).wait_recv()
            for j, chip in enumerate(chips):
                copy(a, 4 + j, (*chip, 1 - c), me).wait_recv()
        for cp in sends:
            cp.wait_send()
        for mine in started:
            mine.wait()

    return pl.pallas_call(
        body, name=name, in_specs=[ANY] * n, out_specs=[ANY] * n,
        out_shape=[jax.ShapeDtypeStruct((N_DEV,) + s.shape, s.dtype) for s in shards],
        scratch_shapes=[pltpu.SemaphoreType.DMA((7 * n,)), pltpu.SemaphoreType.DMA((7 * n,)),
                        pltpu.SemaphoreType.DMA((n,))],
    )(*shards)


def _rs_sibling_exchange(packed, name):
    n = len(packed)

    def body(*refs):
        p_refs, r_refs = refs[:n], refs[n:2 * n]
        send_sems, recv_sems = refs[2 * n:]
        x, y, c = _place()
        cps = []
        for a in range(n):
            for j in range(4):
                cps.append(pltpu.make_async_remote_copy(
                    src_ref=p_refs[a].at[2 * j + (1 - c)], dst_ref=r_refs[a].at[j],
                    send_sem=send_sems.at[4 * a + j], recv_sem=recv_sems.at[4 * a + j],
                    device_id=(x, y, 1 - c), device_id_type=MESH))
        for cp in cps:
            cp.start()
        for cp in cps:
            cp.wait()

    return pl.pallas_call(
        body, name=name, in_specs=[ANY] * n, out_specs=[ANY] * n,
        out_shape=[jax.ShapeDtypeStruct((4,) + p.shape[1:], p.dtype) for p in packed],
        scratch_shapes=[pltpu.SemaphoreType.DMA((4 * n,)), pltpu.SemaphoreType.DMA((4 * n,))],
    )(*packed)


def _rs_chip_sum(packed, from_sibling, c_idx, name):
    _, r, cc = packed.shape
    tr = _pick(r, (512, 256, 352, 128))

    def body(c_ref, a_ref, b_ref, o_ref):
        o_ref[...] = (a_ref[...].astype(F32) + b_ref[...].astype(F32)).astype(o_ref.dtype)

    return pl.pallas_call(
        body, name=name,
        grid_spec=pltpu.PrefetchScalarGridSpec(
            num_scalar_prefetch=1, grid=(4, r // tr),
            in_specs=[pl.BlockSpec((None, tr, cc), lambda j, i, c_ref: (2 * j + c_ref[0], i, 0)),
                      pl.BlockSpec((None, tr, cc), lambda j, i, c_ref: (j, i, 0))],
            out_specs=pl.BlockSpec((None, tr, cc), lambda j, i, c_ref: (j, i, 0))),
        out_shape=jax.ShapeDtypeStruct((4, r, cc), packed.dtype),
        compiler_params=_params("parallel", "parallel"),
    )(c_idx, packed, from_sibling)


def _rs_chip_exchange(partial, name):
    n = len(partial)

    def body(*refs):
        p_refs, r_refs = refs[:n], refs[n:2 * n]
        send_sems, recv_sems = refs[2 * n:]
        x, y, c = _place()
        chips = [(1 - x, y), (x, 1 - y), (1 - x, 1 - y)]
        cps = []
        for a in range(n):
            for k, (tx, ty) in enumerate(chips):
                cps.append(pltpu.make_async_remote_copy(
                    src_ref=p_refs[a].at[2 * tx + ty], dst_ref=r_refs[a].at[k],
                    send_sem=send_sems.at[3 * a + k], recv_sem=recv_sems.at[3 * a + k],
                    device_id=(tx, ty, c), device_id_type=MESH))
        for cp in cps:
            cp.start()
        for cp in cps:
            cp.wait()

    return pl.pallas_call(
        body, name=name, in_specs=[ANY] * n, out_specs=[ANY] * n,
        out_shape=[jax.ShapeDtypeStruct((3,) + p.shape[1:], p.dtype) for p in partial],
        scratch_shapes=[pltpu.SemaphoreType.DMA((3 * n,)), pltpu.SemaphoreType.DMA((3 * n,))],
    )(*partial)


def _adam_update(w, g, m, v):
    mn = ADAM_B1 * m + (1.0 - ADAM_B1) * g
    vn = ADAM_B2 * v + (1.0 - ADAM_B2) * (g * g)
    m_hat = mn / (1.0 - ADAM_B1 ** ADAM_STEP)
    v_hat = vn / (1.0 - ADAM_B2 ** ADAM_STEP)
    return -ADAM_LR * (m_hat / (jnp.sqrt(v_hat) + ADAM_EPS) + ADAM_WD * w), mn, vn


def _rs_final_adamw(partial, received, chip_idx, w, m, v, name):
    _, r, cc = partial.shape
    tr = _pick(r, (512, 256, 352, 128))

    def body(c_ref, a_ref, r_ref, w_ref, m_ref, v_ref, g_ref, d_ref, mo_ref, vo_ref):
        g = a_ref[...].astype(F32)
        for k in range(3):
            g = g + r_ref[k].astype(F32)
        g_ref[...] = g
        d_ref[...], mo_ref[...], vo_ref[...] = _adam_update(w_ref[...], g, m_ref[...], v_ref[...])

    row = pl.BlockSpec((tr, cc), lambda i, c_ref: (i, 0))
    return pl.pallas_call(
        body, name=name,
        grid_spec=pltpu.PrefetchScalarGridSpec(
            num_scalar_prefetch=1, grid=(r // tr,),
            in_specs=[pl.BlockSpec((None, tr, cc), lambda i, c_ref: (c_ref[0], i, 0)),
                      pl.BlockSpec((3, tr, cc), lambda i, c_ref: (0, i, 0)), row, row, row],
            out_specs=[row] * 4),
        out_shape=[jax.ShapeDtypeStruct((r, cc), F32)] * 4,
        compiler_params=_params("parallel"),
    )(chip_idx, partial, received, w.reshape(r, cc), m.reshape(r, cc), v.reshape(r, cc))


def _all_reduce_small(vals, name):
    r, cc = vals.shape

    def body(v_ref, o_ref, buf, send_sems, recv_sems):
        x, y, c = _place()
        me = 4 * x + 2 * y + c
        buf[0] = v_ref[...]
        cps = []
        for k in range(1, N_DEV):
            kx, ky, kc = (k >> 2) & 1, (k >> 1) & 1, k & 1
            peer = (1 - x if kx else x, 1 - y if ky else y, 1 - c if kc else c)
            cps.append(pltpu.make_async_remote_copy(
                src_ref=buf.at[0], dst_ref=buf.at[k], send_sem=send_sems.at[k - 1],
                recv_sem=recv_sems.at[k - 1], device_id=peer, device_id_type=MESH))
        for cp in cps:
            cp.start()
        for cp in cps:
            cp.wait()
        acc = buf[jnp.bitwise_xor(me, 0)]
        for dev in range(1, N_DEV):
            acc = acc + buf[jnp.bitwise_xor(me, dev)]
        o_ref[...] = acc

    vm = pl.BlockSpec(memory_space=pltpu.VMEM)
    return pl.pallas_call(
        body, name=name, in_specs=[vm], out_specs=vm,
        out_shape=jax.ShapeDtypeStruct((r, cc), F32),
        scratch_shapes=[pltpu.VMEM((N_DEV, r, cc), F32), pltpu.SemaphoreType.DMA((7,)),
                        pltpu.SemaphoreType.DMA((7,))],
        compiler_params=pltpu.CompilerParams(vmem_limit_bytes=VMEM_LIMIT),
    )(vals)


def _lanes(flat):
    pad = (-flat.shape[0]) % (SUBLANES * LANES)
    return jnp.pad(flat, (0, pad)).reshape(-1, LANES)


def kernel(x, even_w_in, even_b_f, even_conv_w, even_w_out, odd_w_in, odd_v_ln_g, odd_v_ln_b, odd_w_s, odd_b_s, odd_w_out, mix_ln_g, mix_ln_b, ffn_w_in, ffn_w_out, ffn_ln_g, ffn_ln_b, loss_target, m_even_w_in, m_even_b_f, m_even_conv_w, m_even_w_out, m_odd_w_in, m_odd_v_ln_g, m_odd_v_ln_b, m_odd_w_s, m_odd_b_s, m_odd_w_out, m_mix_ln_g, m_mix_ln_b, m_ffn_w_in, m_ffn_w_out, m_ffn_ln_g, m_ffn_ln_b, v_even_w_in, v_even_b_f, v_even_conv_w, v_even_w_out, v_odd_w_in, v_odd_v_ln_g, v_odd_v_ln_b, v_odd_w_s, v_odd_b_s, v_odd_w_out, v_mix_ln_g, v_mix_ln_b, v_ffn_w_in, v_ffn_w_out, v_ffn_ln_g, v_ffn_ln_b):
    t, d = x.shape[1], x.shape[2]
    nh = even_b_f.shape[-1]
    w = even_conv_w.shape[-1] * N_DEV
    dh = w // nh
    scale = dh ** -0.5
    e_in = even_w_in.shape[-1] * N_DEV
    f2 = ffn_w_in.shape[-1] * N_DEV
    f = f2 // 2
    ng, pb = odd_w_s.shape[1], odd_w_s.shape[2]
    assert e_in == 6 * w + nh and nh <= SUBLANES and (6 * w) % LANES == 0 and d % N_DEV == 0
    mx, my, mc = _place()
    me = 4 * mx + 2 * my + mc

    big = [even_w_in[0], even_w_out[0], odd_w_in[0], odd_w_out[0],
           ffn_w_in[0], ffn_w_in[1], ffn_w_out[0], ffn_w_out[1]]
    g_in0, g_out0, g_in1, g_out1, g_fi0, g_fi1, g_fo0, g_fo1 = _all_gather([s.astype(BF16) for s in big], "ag_weights")
    w_out0, w_out1 = g_out0.reshape(2 * w, d), g_out1.reshape(d, d)
    w_fo0, w_fo1 = g_fo0.reshape(f, d), g_fo1.reshape(f, d)
    nb = N_DEV // 2
    w_fi0, w_fi1 = g_fi0.reshape(2, nb, d, -1), g_fi1.reshape(2, nb, d, -1)
    w_in0 = g_in0.transpose(1, 0, 2).reshape(d, e_in)
    w_all0 = jnp.concatenate([w_in0[:, :3 * w], w_in0[:, 3 * w + nh:], w_in0[:, 3 * w:3 * w + nh],
                              jnp.zeros((d, LANES - nh), BF16)], axis=1)

    cs, vs = even_conv_w.shape[-1], odd_v_ln_g.shape[-1]
    small_mine = jnp.concatenate([
        lax.dynamic_update_slice(jnp.zeros((3, w), F32), even_conv_w[0], (0, me * cs)).reshape(-1),
        lax.dynamic_update_slice(jnp.zeros((d,), F32), odd_v_ln_g[0], (me * vs,)),
        lax.dynamic_update_slice(jnp.zeros((d,), F32), odd_v_ln_b[0], (me * vs,))])
    small_all = _all_reduce_small(_lanes(small_mine), "ag_small").reshape(-1)
    conv_w = small_all[:3 * w].reshape(3, w)
    vln_g = small_all[3 * w:3 * w + d]
    vln_b = small_all[3 * w + d:3 * w + 2 * d]

    bf_pad = jnp.pad(even_b_f[0], (0, LANES - nh)).reshape(1, LANES)
    chunk = jnp.arange(pb) // (pb // 2)
    ws_mask = (chunk[None, :] <= chunk[:, None])[None]
    wm = jnp.where(ws_mask, odd_w_s[0], 0.0).astype(BF16)
    bs_full = jnp.repeat(odd_b_s[0].T, d // ng, axis=1)

    x0 = x[0]
    tgt = loss_target[0]
    fcol = 6 * w // LANES
    p0 = _mm(x0, w_all0, "nn", F32, "l0_in_proj")
    cgate = _fgate_fwd(p0, bf_pad, fcol, nh, "l0_fgate")
    heads = lambda a: a.reshape(t, nh, dh).transpose(1, 0, 2)
    unheads = lambda a: a[:, :, :dh].transpose(1, 0, 2).reshape(t, w)
    def split3(xv):
        rnd = lambda a: lax.reduce_precision(a, 8, 7)
        hi = rnd(xv)
        mid = rnd(xv - hi)
        return [hi.astype(BF16), mid.astype(BF16), rnd(xv - hi - mid).astype(BF16)]

    rows3 = lambda a: [-term[:, :, None] for term in split3(a[:, :nh].T)]
    one = jnp.ones((nh, t, 1), BF16)

    def augment(*parts):
        a = jnp.concatenate(parts, axis=2)
        return jnp.pad(a, ((0, 0), (0, 0), (0, LANES - a.shape[2])))

    assert dh + 7 <= LANES
    qh = heads((p0[:, :w] * scale).astype(BF16))
    qa = augment(qh, one, one, one)
    ka = augment(heads(p0[:, w:2 * w].astype(BF16)), *rows3(cgate), one, one, one, one)
    va = augment(heads(p0[:, 2 * w:3 * w].astype(BF16)), one, one, one)
    oa = _attn_fwd(qa, ka, va, dh, "l0_attn")
    attn = unheads(oa)
    yconv = _conv_fwd(p0, conv_w, w, 3, "l0_conv")
    mix = jnp.concatenate([attn.astype(BF16), yconv], axis=1)
    m0 = _mm(mix, w_out0, "nn", F32, "l0_out_proj")
    x1, x1b = _ln_fwd(x0, m0, mix_ln_g[0], mix_ln_b[0], "l0_mix_ln")
    h0, gu0 = _ffn_in_swiglu(x1b, w_fi0, "l0_ffn_in")
    f0 = _mm_blk_fwd(h0, w_fo0, F32, "l0_ffn_out")
    x2, x2b = _ln_fwd(x1, f0, ffn_ln_g[0], ffn_ln_b[0], "l0_ffn_ln")

    uv = _mm_cols_fwd(x2b, g_in1, False, F32, "l1_in_proj")
    gated = _sgu_fwd(uv, vln_g, vln_b, wm, bs_full, "l1_sgu")
    m1 = _mm(gated, w_out1, "nn", F32, "l1_out_proj")
    x3, x3b = _ln_fwd(x2, m1, mix_ln_g[1], mix_ln_b[1], "l1_mix_ln")
    h1, gu1 = _ffn_in_swiglu(x3b, w_fi1, "l1_ffn_in")
    f1 = _mm_blk_fwd(h1, w_fo1, F32, "l1_ffn_out")
    x4, _ = _ln_fwd(x3, f1, ffn_ln_g[1], ffn_ln_b[1], "l1_ffn_ln")
    loss_part, dy4 = _loss(x4, tgt, "loss")

    dz4, g_ffn_g1, g_ffn_b1 = _ln_bwd(x3, f1, ffn_ln_g[1], dy4, 1.0, None, "l1_ffn_ln_bwd")
    gd_fo1 = _mm_blk_dw(h1, dz4, BF16, "l1_ffn_out_dw").reshape(N_DEV, -1, d)
    dgu1 = _ffn_out_dx_swiglu(dz4, w_fo1, gu1, "l1_ffn_out_dx").reshape(N_DEV, t, -1)
    gd_fi1 = _mm_cols_dw(x3b, dgu1, N_DEV, True, BF16, "l1_ffn_in_dw")
    dx3 = _mm_cols_dx(dgu1, g_fi1, True, F32, "l1_ffn_in_dx")
    dz3, g_mix_g1, g_mix_b1 = _ln_bwd(x2, m1, mix_ln_g[1], dz4, ALPHA, dx3, "l1_mix_ln_bwd")
    gd_out1 = _mm(gated, dz3, "tn", BF16, "l1_out_proj_dw").reshape(N_DEV, -1, d)
    dgated = _mm(dz3, w_out1, "nt", BF16, "l1_out_proj_dx")
    duv, g_wm, g_bs_t, g_vln_g, g_vln_b = _sgu_bwd(uv, vln_g, vln_b, wm, bs_full, dgated, "l1_sgu_bwd")
    gd_in1 = _mm_cols_dw(x2b, duv, N_DEV, False, BF16, "l1_in_proj_dw")
    dx2 = _mm_cols_dx(duv, g_in1, False, F32, "l1_in_proj_dx")

    dz2, g_ffn_g0, g_ffn_b0 = _ln_bwd(x1, f0, ffn_ln_g[0], dz3, ALPHA, dx2, "l0_ffn_ln_bwd")
    gd_fo0 = _mm_blk_dw(h0, dz2, BF16, "l0_ffn_out_dw").reshape(N_DEV, -1, d)
    dgu0 = _ffn_out_dx_swiglu(dz2, w_fo0, gu0, "l0_ffn_out_dx").reshape(N_DEV, t, -1)
    gd_fi0 = _mm_cols_dw(x1b, dgu0, N_DEV, True, BF16, "l0_ffn_in_dw")
    dx1 = _mm_cols_dx(dgu0, g_fi0, True, F32, "l0_ffn_in_dx")
    dz1, g_mix_g0, g_mix_b0 = _ln_bwd(x0, m0, mix_ln_g[0], dz2, ALPHA, dx1, "l0_mix_ln_bwd")
    gd_out0 = _mm(mix, dz1, "tn", BF16, "l0_out_proj_dw").reshape(N_DEV, -1, d)
    dmix = _mm(dz1, w_out0, "nt", F32, "l0_out_proj_dx")
    d_b, d_c, d_h, g_conv = _conv_bwd(p0, conv_w, dmix, w, 3, "l0_conv_bwd")
    d_attn = dmix[:, :w]
    delta = _attn_delta(d_attn, attn, nh, "l0_attn_delta")
    lse3 = [-term for term in split3(oa[:, :, dh:dh + 1])]
    qa2 = augment(qh, one, one, one, jnp.zeros_like(one), *lse3)
    doa = augment(heads(d_attn.astype(BF16)), *rows3(delta))
    dqa, dka, dva = _attn_bwd(qa2, ka, va, doa, "l0_attn_bwd")
    lanes_of = lambda a: jnp.pad(a.T, ((0, 0), (0, LANES - nh)))
    dcq = lanes_of(dqa[:, :, dh + 3])
    dck = lanes_of(-dka[:, :, dh])
    dzf, g_bf = _fgate_bwd(p0, bf_pad, dcq, dck, fcol, nh, "l0_fgate_bwd")
    dp0 = jnp.concatenate([unheads(dqa) * scale, unheads(dka), unheads(dva), d_b, d_c, d_h, dzf], axis=1).astype(BF16)
    g_all0 = _mm(x0, dp0, "tn", F32, "l0_in_proj_dw")
    dx0 = _mm(dp0, w_all0, "nt", F32, "l0_in_proj_dx")
    grad_x = _axpy(ALPHA, dz1, dx0, "grad_x")
    gd_in0 = jnp.concatenate([g_all0[:, :3 * w], g_all0[:, 6 * w:6 * w + nh], g_all0[:, 3 * w:6 * w]], axis=1)
    gd_in0 = gd_in0.reshape(d, N_DEV, -1).transpose(1, 0, 2).astype(BF16)

    big_g = [gd_in0, gd_out0, gd_in1, gd_out1, gd_fi0, gd_fi1, gd_fo0, gd_fo1]
    big_m = [m_even_w_in[0], m_even_w_out[0], m_odd_w_in[0], m_odd_w_out[0],
             m_ffn_w_in[0], m_ffn_w_in[1], m_ffn_w_out[0], m_ffn_w_out[1]]
    big_v = [v_even_w_in[0], v_even_w_out[0], v_odd_w_in[0], v_odd_w_out[0],
             v_ffn_w_in[0], v_ffn_w_in[1], v_ffn_w_out[0], v_ffn_w_out[1]]
    big_names = ["even_w_in", "even_w_out", "odd_w_in", "odd_w_out", "ffn_w_in0", "ffn_w_in1", "ffn_w_out0", "ffn_w_out1"]
    c_idx = mc.reshape(1).astype(jnp.int32)
    chip_idx = (2 * mx + my).reshape(1).astype(jnp.int32)
    from_sib = _rs_sibling_exchange(big_g, "rs_sibling")
    partial = [_rs_chip_sum(g, s, c_idx, "rs_chip_sum_" + n) for g, s, n in zip(big_g, from_sib, big_names)]
    received = _rs_chip_exchange(partial, "rs_chips")
    upd = [_rs_final_adamw(p, r, chip_idx, wt, mt, vt, "rs_final_adamw_" + n)
           for p, r, wt, mt, vt, n in zip(partial, received, big, big_m, big_v, big_names)]
    big_out = {}
    for i, n in enumerate(["even_w_in", "even_w_out", "odd_w_in", "odd_w_out"]):
        big_out[n] = [o[None] for o in upd[i]]
    big_out["ffn_w_in"] = [jnp.stack([a, b]) for a, b in zip(upd[4], upd[5])]
    big_out["ffn_w_out"] = [jnp.stack([a, b]) for a, b in zip(upd[6], upd[7])]

    g_ws = jnp.where(ws_mask, g_wm, 0.0)
    g_bs = g_bs_t[:, :ng].T
    small_g = [g_bf[:nh], g_conv, g_vln_g, g_vln_b, g_ws, g_bs,
               jnp.stack([g_mix_g0, g_mix_g1]), jnp.stack([g_mix_b0, g_mix_b1]),
               jnp.stack([g_ffn_g0, g_ffn_g1]), jnp.stack([g_ffn_b0, g_ffn_b1])]
    small_sum = _all_reduce_small(_lanes(jnp.concatenate([a.reshape(-1) for a in small_g])), "ar_small_grads")
    small_sum = small_sum.reshape(-1)
    outs_small = []
    off = 0
    for a in small_g:
        outs_small.append(small_sum[off:off + a.size].reshape(a.shape))
        off += a.size
    gr_bf, gr_conv, gr_vg, gr_vb, gr_ws, gr_bs, gr_mg, gr_mb, gr_fg, gr_fb = outs_small

    loss = lax.psum(loss_part, ("x", "y", "c"))

    grads = {
        "even_b_f": gr_bf[None],
        "even_conv_w": lax.dynamic_slice(gr_conv, (0, me * cs), (3, cs))[None],
        "odd_v_ln_g": lax.dynamic_slice(gr_vg, (me * vs,), (vs,))[None],
        "odd_v_ln_b": lax.dynamic_slice(gr_vb, (me * vs,), (vs,))[None],
        "odd_w_s": gr_ws[None], "odd_b_s": gr_bs[None],
        "mix_ln_g": gr_mg, "mix_ln_b": gr_mb, "ffn_ln_g": gr_fg, "ffn_ln_b": gr_fb,
    }
    weights = dict(even_w_in=even_w_in, even_b_f=even_b_f, even_conv_w=even_conv_w, even_w_out=even_w_out,
                   odd_w_in=odd_w_in, odd_v_ln_g=odd_v_ln_g, odd_v_ln_b=odd_v_ln_b, odd_w_s=odd_w_s,
                   odd_b_s=odd_b_s, odd_w_out=odd_w_out, mix_ln_g=mix_ln_g, mix_ln_b=mix_ln_b,
                   ffn_w_in=ffn_w_in, ffn_w_out=ffn_w_out, ffn_ln_g=ffn_ln_g, ffn_ln_b=ffn_ln_b)
    moms = dict(even_w_in=(m_even_w_in, v_even_w_in), even_b_f=(m_even_b_f, v_even_b_f),
                even_conv_w=(m_even_conv_w, v_even_conv_w), even_w_out=(m_even_w_out, v_even_w_out),
                odd_w_in=(m_odd_w_in, v_odd_w_in), odd_v_ln_g=(m_odd_v_ln_g, v_odd_v_ln_g),
                odd_v_ln_b=(m_odd_v_ln_b, v_odd_v_ln_b), odd_w_s=(m_odd_w_s, v_odd_w_s),
                odd_b_s=(m_odd_b_s, v_odd_b_s), odd_w_out=(m_odd_w_out, v_odd_w_out),
                mix_ln_g=(m_mix_ln_g, v_mix_ln_g), mix_ln_b=(m_mix_ln_b, v_mix_ln_b),
                ffn_w_in=(m_ffn_w_in, v_ffn_w_in), ffn_w_out=(m_ffn_w_out, v_ffn_w_out),
                ffn_ln_g=(m_ffn_ln_g, v_ffn_ln_g), ffn_ln_b=(m_ffn_ln_b, v_ffn_ln_b))
    names = list(weights)
    gout, deltas, new_m, new_v = [], [], [], []
    for n in names:
        if n in big_out:
            gr, dlt, mn, vn = big_out[n]
        else:
            gr = grads[n]
            dlt, mn, vn = _adamw(weights[n], gr, moms[n][0], moms[n][1], "adamw_" + n)
        gout.append(gr.reshape(weights[n].shape))
        deltas.append(dlt.reshape(weights[n].shape))
        new_m.append(mn.reshape(weights[n].shape))
        new_v.append(vn.reshape(weights[n].shape))
    return (loss, grad_x[None], *gout, *deltas, *new_m, *new_v)
```

```python
import jax
import jax.numpy as jnp
from jax import lax
from jax.experimental import pallas as pl
from jax.experimental.pallas import tpu as pltpu

F32 = jnp.float32
BF16 = jnp.bfloat16
MESH = pl.DeviceIdType.MESH

DEPTH = 2
ALPHA = (2.0 * DEPTH) ** 0.25
LN_EPS = 1e-5
ADAM_LR = 0.001
ADAM_B1 = 0.9
ADAM_B2 = 0.999
ADAM_EPS = 1e-08
ADAM_WD = 0.01
ADAM_STEP = 10

N_DEV = 8
LANES = 128
SUBLANES = 8
VMEM_LIMIT = 48 * 1024 * 1024
NEG_BIG = -1e30
ROW_TILES = (512, 256, 128)


def _pick(n, cands):
    for c in cands:
        if c <= n and n % c == 0:
            return c
    return n


def _params(*sem):
    return pltpu.CompilerParams(dimension_semantics=sem, vmem_limit_bytes=VMEM_LIMIT)


NN = (((1,), (0,)), ((), ()))
NT = (((1,), (1,)), ((), ()))
TN = (((0,), (0,)), ((), ()))
M_TILES = (1024, 512, 1408, 256, 128)
N_TILES = (512, 640, 256, 128)
K_TILES = (1024, 512, 640, 1408, 256, 128)


def _mm_core(name, grid, a, b, a_spec, b_spec, o_spec, o_shape, o_dtype, dims, tile):
    nred = grid[2]

    def body(a_ref, b_ref, o_ref, *acc):
        part = lax.dot_general(a_ref[...].astype(BF16), b_ref[...].astype(BF16), dims, preferred_element_type=F32)
        if nred == 1:
            o_ref[...] = part.astype(o_ref.dtype)
            return
        acc_ref, = acc
        kk = pl.program_id(2)

        @pl.when(kk == 0)
        def _():
            acc_ref[...] = jnp.zeros_like(acc_ref)

        acc_ref[...] += part

        @pl.when(kk == nred - 1)
        def _():
            o_ref[...] = acc_ref[...].astype(o_ref.dtype)

    return pl.pallas_call(
        body, name=name, grid=grid, in_specs=[a_spec, b_spec], out_specs=o_spec,
        out_shape=jax.ShapeDtypeStruct(o_shape, o_dtype),
        scratch_shapes=[] if nred == 1 else [pltpu.VMEM(tile, F32)],
        compiler_params=_params("parallel", "parallel", "arbitrary"),
    )(a, b)


def _mm(a, b, mode, out_dtype, name):
    if mode == "nn":
        (m, k), (k2, n) = a.shape, b.shape
    elif mode == "nt":
        (m, k), (n, k2) = a.shape, b.shape
    else:
        (k, m), (k2, n) = a.shape, b.shape
    assert k == k2, (a.shape, b.shape, mode)
    tm, tn, tk = _pick(m, M_TILES), _pick(n, N_TILES), _pick(k, K_TILES)
    if mode == "nn":
        a_spec = pl.BlockSpec((tm, tk), lambda i, j, kk: (i, kk))
        b_spec = pl.BlockSpec((tk, tn), lambda i, j, kk: (kk, j))
        dims = NN
    elif mode == "nt":
        a_spec = pl.BlockSpec((tm, tk), lambda i, j, kk: (i, kk))
        b_spec = pl.BlockSpec((tn, tk), lambda i, j, kk: (j, kk))
        dims = NT
    else:
        a_spec = pl.BlockSpec((tk, tm), lambda i, j, kk: (kk, i))
        b_spec = pl.BlockSpec((tk, tn), lambda i, j, kk: (kk, j))
        dims = TN
    return _mm_core(name, (m // tm, n // tn, k // tk), a, b, a_spec, b_spec,
                    pl.BlockSpec((tm, tn), lambda i, j, kk: (i, j)), (m, n), out_dtype, dims, (tm, tn))


def _act_spec(blocked, rows, ns, row_ax, d_ax):
    if blocked:
        return pl.BlockSpec((None, rows, ns), lambda *g: (g[d_ax], g[row_ax], 0))
    return pl.BlockSpec((rows, ns), lambda *g: (g[row_ax], g[d_ax]))


def _mm_cols_fwd(a, g3, blocked, out_dtype, name):
    (t, k), (nd, k2, ns) = a.shape, g3.shape
    assert k == k2
    tm, tk = _pick(t, M_TILES), _pick(k, K_TILES)
    return _mm_core(name, (t // tm, nd, k // tk), a, g3,
                    pl.BlockSpec((tm, tk), lambda i, d, kk: (i, kk)),
                    pl.BlockSpec((None, tk, ns), lambda i, d, kk: (d, kk, 0)),
                    _act_spec(blocked, tm, ns, 0, 1), (nd, t, ns) if blocked else (t, nd * ns), out_dtype, NN, (tm, ns))


def _mm_cols_dx(dy, g3, blocked, out_dtype, name):
    nd, k, ns = g3.shape
    t = dy.shape[1] if blocked else dy.shape[0]
    tm, tn = _pick(t, M_TILES), _pick(k, (1024,) + N_TILES)
    return _mm_core(name, (t // tm, k // tn, nd), dy, g3,
                    _act_spec(blocked, tm, ns, 0, 2),
                    pl.BlockSpec((None, tn, ns), lambda i, j, d: (d, j, 0)),
                    pl.BlockSpec((tm, tn), lambda i, j, d: (i, j)), (t, k), out_dtype, NT, (tm, tn))


def _mm_cols_dw(a, dy, nd, blocked, out_dtype, name):
    t, k = a.shape
    ns = dy.shape[2] if blocked else dy.shape[1] // nd
    tmk, tk = _pick(k, M_TILES), _pick(t, K_TILES)
    return _mm_core(name, (nd, k // tmk, t // tk), a, dy,
                    pl.BlockSpec((tk, tmk), lambda d, j, kk: (kk, j)),
                    _act_spec(blocked, tk, ns, 2, 0),
                    pl.BlockSpec((None, tmk, ns), lambda d, j, kk: (d, j, 0)), (nd, k, ns), out_dtype, TN, (tmk, ns))


def _mm_blk_fwd(h3, w, out_dtype, name):
    (nb, t, ns), (_, n) = h3.shape, w.shape
    tm, tn = _pick(t, M_TILES), _pick(n, (1024,) + N_TILES)
    return _mm_core(name, (t // tm, n // tn, nb), h3, w,
                    pl.BlockSpec((None, tm, ns), lambda i, j, d: (d, i, 0)),
                    pl.BlockSpec((ns, tn), lambda i, j, d: (d, j)),
                    pl.BlockSpec((tm, tn), lambda i, j, d: (i, j)), (t, n), out_dtype, NN, (tm, tn))


def _mm_blk_dw(h3, dz, out_dtype, name):
    (nb, t, ns), (_, n) = h3.shape, dz.shape
    tn, tk = _pick(n, (1024,) + N_TILES), _pick(t, K_TILES)
    return _mm_core(name, (nb, n // tn, t // tk), h3, dz,
                    pl.BlockSpec((None, tk, ns), lambda d, j, kk: (d, kk, 0)),
                    pl.BlockSpec((tk, tn), lambda d, j, kk: (kk, j)),
                    pl.BlockSpec((ns, tn), lambda d, j, kk: (d, j)), (nb * ns, n), out_dtype, TN, (ns, tn))


def _ln_fwd(xa, xb, g, b, name):
    t, d = xa.shape
    tb = _pick(t, ROW_TILES)

    def body(xa_ref, xb_ref, g_ref, b_ref, y_ref, yb_ref):
        z = ALPHA * xa_ref[...] + xb_ref[...]
        mu = jnp.mean(z, axis=-1, keepdims=True)
        zc = z - mu
        var = jnp.mean(zc * zc, axis=-1, keepdims=True)
        y = zc * lax.rsqrt(var + LN_EPS) * g_ref[...] + b_ref[...]
        y_ref[...] = y
        yb_ref[...] = y.astype(BF16)

    row = pl.BlockSpec((tb, d), lambda i: (i, 0))
    vec = pl.BlockSpec((1, d), lambda i: (0, 0))
    return pl.pallas_call(
        body, name=name, grid=(t // tb,),
        in_specs=[row, row, vec, vec], out_specs=[row, row],
        out_shape=[jax.ShapeDtypeStruct((t, d), F32), jax.ShapeDtypeStruct((t, d), BF16)],
        compiler_params=_params("parallel"),
    )(xa, xb, g.reshape(1, d), b.reshape(1, d))


def _ln_bwd(xa, xb, g, dya, ca, dyb, name):
    t, d = xa.shape
    tb = _pick(t, ROW_TILES)
    two = dyb is not None

    def body(*refs):
        if two:
            xa_ref, xb_ref, g_ref, dya_ref, dyb_ref, dz_ref, dg_ref, db_ref = refs
            dy = ca * dya_ref[...] + dyb_ref[...]
        else:
            xa_ref, xb_ref, g_ref, dya_ref, dz_ref, dg_ref, db_ref = refs
            dy = ca * dya_ref[...]
        z = ALPHA * xa_ref[...] + xb_ref[...]
        mu = jnp.mean(z, axis=-1, keepdims=True)
        zc = z - mu
        var = jnp.mean(zc * zc, axis=-1, keepdims=True)
        rstd = lax.rsqrt(var + LN_EPS)
        xhat = zc * rstd
        dxh = dy * g_ref[...]
        m1 = jnp.mean(dxh, axis=-1, keepdims=True)
        m2 = jnp.mean(dxh * xhat, axis=-1, keepdims=True)
        dz_ref[...] = rstd * (dxh - m1 - xhat * m2)

        @pl.when(pl.program_id(0) == 0)
        def _():
            dg_ref[...] = jnp.zeros_like(dg_ref)
            db_ref[...] = jnp.zeros_like(db_ref)

        dg_ref[...] += jnp.sum(dy * xhat, axis=0, keepdims=True)
        db_ref[...] += jnp.sum(dy, axis=0, keepdims=True)

    row = pl.BlockSpec((tb, d), lambda i: (i, 0))
    vec = pl.BlockSpec((1, d), lambda i: (0, 0))
    ins = [xa, xb, g.reshape(1, d), dya] + ([dyb] if two else [])
    dz, dg, db = pl.pallas_call(
        body, name=name, grid=(t // tb,),
        in_specs=[row, row, vec, row] + ([row] if two else []),
        out_specs=[row, vec, vec],
        out_shape=[jax.ShapeDtypeStruct((t, d), F32), jax.ShapeDtypeStruct((1, d), F32),
                   jax.ShapeDtypeStruct((1, d), F32)],
        compiler_params=_params("arbitrary"),
    )(*ins)
    return dz, dg[0], db[0]


def _loss(y, target, name):
    t, d = y.shape
    tb = _pick(t, ROW_TILES)

    def body(y_ref, t_ref, dy_ref, l_ref):
        e = y_ref[...] - t_ref[...]
        dy_ref[...] = e * (1.0 / d)

        @pl.when(pl.program_id(0) == 0)
        def _():
            l_ref[...] = jnp.zeros_like(l_ref)

        l_ref[...] += 0.5 * jnp.sum(jnp.mean(e * e, axis=-1, keepdims=True))

    row = pl.BlockSpec((tb, d), lambda i: (i, 0))
    dy, l = pl.pallas_call(
        body, name=name, grid=(t // tb,),
        in_specs=[row, row], out_specs=[row, pl.BlockSpec((1, LANES), lambda i: (0, 0))],
        out_shape=[jax.ShapeDtypeStruct((t, d), F32), jax.ShapeDtypeStruct((1, LANES), F32)],
        compiler_params=_params("arbitrary"),
    )(y, target)
    return l[0, 0], dy


def _axpy(ca, a, b, name):
    t, d = a.shape
    tb = _pick(t, ROW_TILES)

    def body(a_ref, b_ref, o_ref):
        o_ref[...] = ca * a_ref[...] + b_ref[...]

    row = pl.BlockSpec((tb, d), lambda i: (i, 0))
    return pl.pallas_call(
        body, name=name, grid=(t // tb,), in_specs=[row, row], out_specs=row,
        out_shape=jax.ShapeDtypeStruct((t, d), F32), compiler_params=_params("parallel"),
    )(a, b)


def _ffn_in_swiglu(xb, g4, name):
    (t, k), (_, nb, _, ns) = xb.shape, g4.shape
    tm = _pick(t, M_TILES)

    def body(x_ref, w_ref, h_ref, gu_ref):
        xv = x_ref[...]
        gate = jnp.dot(xv, w_ref[0], preferred_element_type=F32)
        up = jnp.dot(xv, w_ref[1], preferred_element_type=F32)
        h_ref[...] = (gate * jax.nn.sigmoid(gate) * up).astype(BF16)
        gu_ref[0] = gate.astype(BF16)
        gu_ref[1] = up.astype(BF16)

    return pl.pallas_call(
        body, name=name, grid=(t // tm, nb),
        in_specs=[pl.BlockSpec((tm, k), lambda i, d: (i, 0)),
                  pl.BlockSpec((2, None, k, ns), lambda i, d: (0, d, 0, 0))],
        out_specs=[pl.BlockSpec((None, tm, ns), lambda i, d: (d, i, 0)),
                   pl.BlockSpec((2, None, tm, ns), lambda i, d: (0, d, i, 0))],
        out_shape=[jax.ShapeDtypeStruct((nb, t, ns), BF16), jax.ShapeDtypeStruct((2, nb, t, ns), BF16)],
        compiler_params=_params("parallel", "parallel"),
    )(xb, g4)


def _ffn_out_dx_swiglu(dz, w_out, gu4, name):
    (t, d), (_, nb, _, ns) = dz.shape, gu4.shape
    tm = _pick(t, M_TILES)

    def body(dz_ref, w_ref, gu_ref, o_ref):
        dh = lax.dot_general(dz_ref[...].astype(BF16), w_ref[...], NT, preferred_element_type=F32)
        gate = gu_ref[0].astype(F32)
        up = gu_ref[1].astype(F32)
        sg = jax.nn.sigmoid(gate)
        silu = gate * sg
        o_ref[0] = (dh * up * (sg + silu * (1.0 - sg))).astype(BF16)
        o_ref[1] = (dh * silu).astype(BF16)

    blk = pl.BlockSpec((2, None, tm, ns), lambda i, j: (0, j, i, 0))
    return pl.pallas_call(
        body, name=name, grid=(t // tm, nb),
        in_specs=[pl.BlockSpec((tm, d), lambda i, j: (i, 0)), pl.BlockSpec((ns, d), lambda i, j: (j, 0)), blk],
        out_specs=blk,
        out_shape=jax.ShapeDtypeStruct((2, nb, t, ns), BF16),
        compiler_params=_params("parallel", "parallel"),
    )(dz, w_out, gu4)


def _tri_matmul(tri, x):
    x1 = x.astype(BF16)
    r1 = x - x1.astype(F32)
    x2 = r1.astype(BF16)
    x3 = (r1 - x2.astype(F32)).astype(BF16)
    dot = lambda v: jnp.dot(tri, v, preferred_element_type=F32)
    return dot(x1) + dot(x2) + dot(x3)


def _fgate_fwd(proj, bf_pad, fcol, n_heads, name):
    t = proj.shape[0]
    tb = _pick(t, ROW_TILES)

    def body(p_ref, b_ref, c_ref, carry):
        @pl.when(pl.program_id(0) == 0)
        def _():
            carry[...] = jnp.zeros_like(carry)

        z = p_ref[...] + b_ref[...]
        lf = jnp.minimum(z, 0.0) - jnp.log1p(jnp.exp(-jnp.abs(z)))
        lane = lax.broadcasted_iota(jnp.int32, (tb, LANES), 1)
        lf = jnp.where(lane < n_heads, lf, 0.0)
        r = lax.broadcasted_iota(jnp.int32, (tb, tb), 0)
        s = lax.broadcasted_iota(jnp.int32, (tb, tb), 1)
        tri = (s <= r).astype(BF16)
        c = _tri_matmul(tri, lf) + carry[...]
        c_ref[...] = c
        carry[...] = c[tb - 1:tb, :]

    return pl.pallas_call(
        body, name=name, grid=(t // tb,),
        in_specs=[pl.BlockSpec((tb, LANES), lambda i: (i, fcol)), pl.BlockSpec((1, LANES), lambda i: (0, 0))],
        out_specs=pl.BlockSpec((tb, LANES), lambda i: (i, 0)),
        out_shape=jax.ShapeDtypeStruct((t, LANES), F32),
        scratch_shapes=[pltpu.VMEM((1, LANES), F32)],
        compiler_params=_params("arbitrary"),
    )(proj, bf_pad)


def _fgate_bwd(proj, bf_pad, dcq, dck, fcol, n_heads, name):
    t = proj.shape[0]
    tb = _pick(t, ROW_TILES)
    nb = t // tb

    def body(p_ref, b_ref, dcq_ref, dck_ref, dz_ref, db_ref, carry):
        @pl.when(pl.program_id(0) == 0)
        def _():
            carry[...] = jnp.zeros_like(carry)
            db_ref[...] = jnp.zeros_like(db_ref)

        r = lax.broadcasted_iota(jnp.int32, (tb, tb), 0)
        s = lax.broadcasted_iota(jnp.int32, (tb, tb), 1)
        tri = (s >= r).astype(BF16)
        dlf = _tri_matmul(tri, dcq_ref[...] + dck_ref[...]) + carry[...]
        carry[...] = dlf[0:1, :]
        z = p_ref[...] + b_ref[...]
        lane = lax.broadcasted_iota(jnp.int32, (tb, LANES), 1)
        dz = jnp.where(lane < n_heads, dlf * jax.nn.sigmoid(-z), 0.0)
        dz_ref[...] = dz
        db_ref[...] += jnp.sum(dz, axis=0, keepdims=True)

    dz, db = pl.pallas_call(
        body, name=name, grid=(nb,),
        in_specs=[pl.BlockSpec((tb, LANES), lambda i: (nb - 1 - i, fcol)),
                  pl.BlockSpec((1, LANES), lambda i: (0, 0)),
                  pl.BlockSpec((tb, LANES), lambda i: (nb - 1 - i, 0)),
                  pl.BlockSpec((tb, LANES), lambda i: (nb - 1 - i, 0))],
        out_specs=[pl.BlockSpec((tb, LANES), lambda i: (nb - 1 - i, 0)),
                   pl.BlockSpec((1, LANES), lambda i: (0, 0))],
        out_shape=[jax.ShapeDtypeStruct((t, LANES), F32), jax.ShapeDtypeStruct((1, LANES), F32)],
        scratch_shapes=[pltpu.VMEM((1, LANES), F32)],
        compiler_params=_params("arbitrary"),
    )(proj, bf_pad, dcq, dck)
    return dz, db[0]


def _split3(x):
    hi = x.astype(BF16)
    r = x - hi.astype(F32)
    mid = r.astype(BF16)
    return hi, mid, (r - mid.astype(F32)).astype(BF16)


def _attn_fwd(qa, ka, va, dh, name):
    nh, t, da = qa.shape
    tq = _pick(t, ROW_TILES)
    hb = 2 if nh % 2 == 0 else 1
    heads = range(hb)

    def body(q_ref, k_ref, v_ref, o_ref, m_s, acc_s):
        qi = pl.program_id(1)
        m_s[...] = jnp.full(m_s.shape, NEG_BIG, F32)
        acc_s[...] = jnp.zeros_like(acc_s)

        def step(j, diagonal):
            off = pl.multiple_of(j * tq, tq)
            s = [lax.dot_general(q_ref[g], k_ref[g, pl.ds(off, tq), :], NT, preferred_element_type=F32)
                 for g in heads]
            if diagonal:
                row = lax.broadcasted_iota(jnp.int32, (tq, tq), 0)
                col = lax.broadcasted_iota(jnp.int32, (tq, tq), 1)
                s = [jnp.where(col > row, NEG_BIG, sg) for sg in s]
            m_prev = [m_s[g] for g in heads]
            m_new = [jnp.maximum(m_prev[g], jnp.max(s[g], axis=1, keepdims=True)) for g in heads]
            p = [jnp.exp(s[g] - m_new[g]).astype(BF16) for g in heads]
            pv = [jnp.dot(p[g], v_ref[g, pl.ds(off, tq), :], preferred_element_type=F32) for g in heads]
            for g in heads:
                acc_s[g] = jnp.exp(m_prev[g] - m_new[g]) * acc_s[g] + pv[g]
                m_s[g] = m_new[g]

        def loop(j, carry):
            step(j, False)
            return carry

        lax.fori_loop(0, qi, loop, 0)
        step(qi, True)
        lane = lax.broadcasted_iota(jnp.int32, (tq, da), 1)
        for g in heads:
            acc = acc_s[g]
            l = jnp.sum(jnp.where(lane == dh, acc, 0.0), axis=1, keepdims=True)
            o_ref[g] = jnp.where(lane == dh, m_s[g] + jnp.log(l), acc / l)

    full = pl.BlockSpec((hb, t, da), lambda h, qi: (h, 0, 0))
    blk = pl.BlockSpec((hb, tq, da), lambda h, qi: (h, qi, 0))
    return pl.pallas_call(
        body, name=name, grid=(nh // hb, t // tq),
        in_specs=[blk, full, full], out_specs=blk,
        out_shape=jax.ShapeDtypeStruct((nh, t, da), F32),
        scratch_shapes=[pltpu.VMEM((hb, tq, 1), F32), pltpu.VMEM((hb, tq, da), F32)],
        compiler_params=_params("parallel", "arbitrary"),
    )(qa, ka, va)


def _attn_bwd(qa, ka, va, doa, name):
    nh, t, da = qa.shape
    tq = _pick(t, ROW_TILES)
    nq = t // tq

    def body(q_ref, do_ref, k_ref, v_ref, dq_ref, dk_ref, dv_ref):
        kj = pl.program_id(1)

        @pl.when(kj == 0)
        def _():
            dq_ref[...] = jnp.zeros_like(dq_ref)

        dk_ref[...] = jnp.zeros_like(dk_ref)
        dv_ref[...] = jnp.zeros_like(dv_ref)
        kb = k_ref[...]
        vb = v_ref[...]

        def step(i, diagonal):
            off = pl.multiple_of(i * tq, tq)
            qb = q_ref[pl.ds(off, tq), :]
            dob = do_ref[pl.ds(off, tq), :]
            st = lax.dot_general(kb, qb, NT, preferred_element_type=F32)
            if diagonal:
                row = lax.broadcasted_iota(jnp.int32, (tq, tq), 0)
                col = lax.broadcasted_iota(jnp.int32, (tq, tq), 1)
                st = jnp.where(row > col, NEG_BIG, st)
            pt = jnp.exp(st)
            dst = (pt * lax.dot_general(vb, dob, NT, preferred_element_type=F32)).astype(BF16)
            dv_ref[...] += jnp.dot(pt.astype(BF16), dob, preferred_element_type=F32)
            dk_ref[...] += jnp.dot(dst, qb, preferred_element_type=F32)
            dq_ref[pl.ds(off, tq), :] += lax.dot_general(dst, kb, TN, preferred_element_type=F32)

        def loop(i, carry):
            step(i, False)
            return carry

        step(kj, True)
        lax.fori_loop(kj + 1, nq, loop, 0)

    full = pl.BlockSpec((None, t, da), lambda h, j: (h, 0, 0))
    blk = pl.BlockSpec((None, tq, da), lambda h, j: (h, j, 0))
    return pl.pallas_call(
        body, name=name, grid=(nh, nq),
        in_specs=[full, full, blk, blk], out_specs=[full, blk, blk],
        out_shape=[jax.ShapeDtypeStruct((nh, t, da), F32)] * 3,
        compiler_params=_params("parallel", "arbitrary"),
    )(qa, doa, ka, va)


def _head_select(w, dh, h, to_heads):
    shape = (w, LANES) if to_heads else (LANES, w)
    r = lax.broadcasted_iota(jnp.int32, shape, 0)
    c = lax.broadcasted_iota(jnp.int32, shape, 1)
    nat, col = (r, c) if to_heads else (c, r)
    return jnp.logical_and(nat == col + h * dh, col < dh).astype(BF16)


def _column(x, lane, j):
    return jnp.sum(jnp.where(lane == j, x, 0.0), axis=1, keepdims=True)


def _bias_columns(lane, first, value):
    out = jnp.zeros(lane.shape, F32)
    for j, term in enumerate(_split3(value)):
        out = out + jnp.where(lane == first + j, -term.astype(F32), 0.0)
    return out


def _attn_pack(proj, cgate, w, nh, scale, name):
    t = proj.shape[0]
    dh = w // nh
    tb = _pick(t, ROW_TILES)

    def body(q_ref, k_ref, v_ref, c_ref, qa_ref, ka_ref, va_ref):
        lane = lax.broadcasted_iota(jnp.int32, (tb, LANES), 1)
        ones_qv = jnp.where(jnp.logical_and(lane >= dh, lane < dh + 3), 1.0, 0.0)
        ones_k = jnp.where(jnp.logical_and(lane >= dh + 3, lane < dh + 7), 1.0, 0.0)
        qb = (q_ref[...] * scale).astype(BF16)
        kb = k_ref[...].astype(BF16)
        vb = v_ref[...].astype(BF16)
        cblk = c_ref[...]
        for h in range(nh):
            sel = _head_select(w, dh, h, True)
            qa_ref[h] = (jnp.dot(qb, sel, preferred_element_type=F32) + ones_qv).astype(BF16)
            va_ref[h] = (jnp.dot(vb, sel, preferred_element_type=F32) + ones_qv).astype(BF16)
            bias = _bias_columns(lane, dh, _column(cblk, lane, h))
            ka_ref[h] = (jnp.dot(kb, sel, preferred_element_type=F32) + bias + ones_k).astype(BF16)

    col = lambda j: pl.BlockSpec((tb, w), lambda i: (i, j))
    out = pl.BlockSpec((nh, tb, LANES), lambda i: (0, i, 0))
    return pl.pallas_call(
        body, name=name, grid=(t // tb,),
        in_specs=[col(0), col(1), col(2), pl.BlockSpec((tb, LANES), lambda i: (i, 0))],
        out_specs=[out, out, out],
        out_shape=[jax.ShapeDtypeStruct((nh, t, LANES), BF16)] * 3,
        compiler_params=_params("parallel"),
    )(proj, proj, proj, cgate)


def _attn_pack_bwd(dmix, oa, qa, w, nh, name):
    t = dmix.shape[0]
    dh = w // nh
    tb = _pick(t, ROW_TILES)

    def body(d_ref, oa_ref, qa_ref, doa_ref, qa2_ref):
        lane = lax.broadcasted_iota(jnp.int32, (tb, LANES), 1)
        db = d_ref[...].astype(BF16)
        for h in range(nh):
            do_h = jnp.dot(db, _head_select(w, dh, h, True), preferred_element_type=F32)
            o_h = oa_ref[h]
            delta = jnp.sum(jnp.where(lane < dh, do_h * o_h, 0.0), axis=1, keepdims=True)
            doa_ref[h] = (do_h + _bias_columns(lane, dh, delta)).astype(BF16)
            qa2_ref[h] = (qa_ref[h].astype(F32) + _bias_columns(lane, dh + 4, _column(o_h, lane, dh))).astype(BF16)

    blk = pl.BlockSpec((nh, tb, LANES), lambda i: (0, i, 0))
    return pl.pallas_call(
        body, name=name, grid=(t // tb,),
        in_specs=[pl.BlockSpec((tb, w), lambda i: (i, 0)), blk, blk], out_specs=[blk, blk],
        out_shape=[jax.ShapeDtypeStruct((nh, t, LANES), BF16)] * 2,
        compiler_params=_params("parallel"),
    )(dmix, oa, qa)


def _attn_unpack(xa, w, nh, mult, sum_col, sum_sign, name):
    t = xa.shape[1]
    dh = w // nh
    tb = _pick(t, ROW_TILES)

    def body(x_ref, o_ref, *rest):
        lane = lax.broadcasted_iota(jnp.int32, (tb, LANES), 1)
        acc = jnp.zeros((tb, w), F32)
        cols = jnp.zeros((tb, LANES), F32)
        for h in range(nh):
            xh = x_ref[h]
            acc = acc + jnp.dot((xh * mult).astype(BF16), _head_select(w, dh, h, False), preferred_element_type=F32)
            if sum_col is not None:
                cols = cols + jnp.where(lane == h, sum_sign * _column(xh, lane, sum_col), 0.0)
        o_ref[...] = acc.astype(BF16)
        if sum_col is not None:
            rest[0][...] = cols

    nat = pl.BlockSpec((tb, w), lambda i: (i, 0))
    lanes = pl.BlockSpec((tb, LANES), lambda i: (i, 0))
    return pl.pallas_call(
        body, name=name, grid=(t // tb,),
        in_specs=[pl.BlockSpec((nh, tb, LANES), lambda i: (0, i, 0))],
        out_specs=[nat, lanes] if sum_col is not None else [nat],
        out_shape=[jax.ShapeDtypeStruct((t, w), BF16)] + ([jax.ShapeDtypeStruct((t, LANES), F32)]
                                                            if sum_col is not None else []),
        compiler_params=_params("parallel"),
    )(xa)


def _conv_fwd(proj, cw, w, bcol, name):
    t = proj.shape[0]
    tb = _pick(t, ROW_TILES)
    hb = tb // SUBLANES

    def body(b_ref, c_ref, h_ref, cp_ref, hp_ref, w_ref, y_ref):
        i = pl.program_id(0)
        zp = jnp.where(i > 0, cp_ref[...] * hp_ref[...], 0.0)
        zext = jnp.concatenate([zp, c_ref[...] * h_ref[...]], axis=0)
        z1 = pltpu.roll(zext, 1, 0)[SUBLANES:]
        z2 = pltpu.roll(zext, 2, 0)[SUBLANES:]
        y = w_ref[2:3, :] * zext[SUBLANES:] + w_ref[1:2, :] * z1 + w_ref[0:1, :] * z2
        y_ref[...] = (b_ref[...] * y).astype(BF16)

    cur = lambda j: pl.BlockSpec((tb, w), lambda i: (i, bcol + j))
    prev = lambda j: pl.BlockSpec((SUBLANES, w), lambda i: (jnp.maximum(i * hb - 1, 0), bcol + j))
    return pl.pallas_call(
        body, name=name, grid=(t // tb,),
        in_specs=[cur(0), cur(1), cur(2), prev(1), prev(2), pl.BlockSpec(cw.shape, lambda i: (0, 0))],
        out_specs=pl.BlockSpec((tb, w), lambda i: (i, 0)),
        out_shape=jax.ShapeDtypeStruct((t, w), BF16), compiler_params=_params("parallel"),
    )(proj, proj, proj, proj, proj, cw)


def _conv_bwd(proj, cw, dmix, w, bcol, name):
    t = proj.shape[0]
    tb = _pick(t, ROW_TILES)
    hb = tb // SUBLANES
    nb = t // tb
    n_ext = tb + SUBLANES

    def body(b_ref, c_ref, h_ref, cp_ref, hp_ref, bn_ref, d_ref, dn_ref, w_ref, db_ref, dc_ref, dh_ref, dw_ref):
        i = pl.program_id(0)
        c = c_ref[...]
        hh = h_ref[...]
        zp = jnp.where(i > 0, cp_ref[...] * hp_ref[...], 0.0)
        zext = jnp.concatenate([zp, c * hh], axis=0)
        z0 = zext[SUBLANES:]
        z1 = pltpu.roll(zext, 1, 0)[SUBLANES:]
        z2 = pltpu.roll(zext, 2, 0)[SUBLANES:]
        y = w_ref[2:3, :] * z0 + w_ref[1:2, :] * z1 + w_ref[0:1, :] * z2
        d = d_ref[...]
        db_ref[...] = d * y
        dy = d * b_ref[...]
        dyn = jnp.where(i < nb - 1, dn_ref[...] * bn_ref[...], 0.0)
        dext = jnp.concatenate([dy, dyn], axis=0)
        dy1 = pltpu.roll(dext, n_ext - 1, 0)[:tb]
        dy2 = pltpu.roll(dext, n_ext - 2, 0)[:tb]
        dz = w_ref[2:3, :] * dy + w_ref[1:2, :] * dy1 + w_ref[0:1, :] * dy2
        dc_ref[...] = dz * hh
        dh_ref[...] = dz * c

        @pl.when(i == 0)
        def _():
            dw_ref[...] = jnp.zeros_like(dw_ref)

        dw_ref[0:1, :] += jnp.sum(dy * z2, axis=0, keepdims=True)
        dw_ref[1:2, :] += jnp.sum(dy * z1, axis=0, keepdims=True)
        dw_ref[2:3, :] += jnp.sum(dy * z0, axis=0, keepdims=True)

    cur = lambda j: pl.BlockSpec((tb, w), lambda i: (i, bcol + j))
    prev = lambda j: pl.BlockSpec((SUBLANES, w), lambda i: (jnp.maximum(i * hb - 1, 0), bcol + j))
    nxt = lambda col: pl.BlockSpec((SUBLANES, w), lambda i: (jnp.minimum((i + 1) * hb, nb * hb - 1), col))
    out = pl.BlockSpec((tb, w), lambda i: (i, 0))
    return pl.pallas_call(
        body, name=name, grid=(nb,),
        in_specs=[cur(0), cur(1), cur(2), prev(1), prev(2), nxt(bcol),
                  pl.BlockSpec((tb, w), lambda i: (i, 1)), nxt(1), pl.BlockSpec(cw.shape, lambda i: (0, 0))],
        out_specs=[out, out, out, pl.BlockSpec(cw.shape, lambda i: (0, 0))],
        out_shape=[jax.ShapeDtypeStruct((t, w), F32)] * 3 + [jax.ShapeDtypeStruct(cw.shape, F32)],
        compiler_params=_params("arbitrary"),
    )(proj, proj, proj, proj, proj, proj, dmix, dmix, cw)


SQRT_HALF = 0.7071067811865476
INV_SQRT_2PI = 0.3989422804014327


def _gelu(x):
    return 0.5 * x * (1.0 + lax.erf(x * SQRT_HALF))


def _gelu_grad(x):
    return 0.5 * (1.0 + lax.erf(x * SQRT_HALF)) + x * (INV_SQRT_2PI * jnp.exp(-0.5 * x * x))


def _sgu_fwd(uv, ln_g, ln_b, wm, bs_full, name):
    t, d2 = uv.shape
    d = d2 // 2
    ng, pb, _ = wm.shape
    gd = d // ng
    tb = _pick(t, ROW_TILES[1:] or ROW_TILES)
    assert tb % pb == 0

    def body(uv_ref, g_ref, b_ref, w_ref, bs_ref, o_ref):
        u = _gelu(uv_ref[:, :d])
        v = _gelu(uv_ref[:, d:])
        mu = jnp.mean(v, axis=-1, keepdims=True)
        vc = v - mu
        var = jnp.mean(vc * vc, axis=-1, keepdims=True)
        vn = (vc * lax.rsqrt(var + LN_EPS) * g_ref[...] + b_ref[...]).astype(BF16)
        for r in range(tb // pb):
            rows = slice(r * pb, (r + 1) * pb)
            for gi in range(ng):
                cols = slice(gi * gd, (gi + 1) * gd)
                s = jnp.dot(w_ref[gi], vn[rows, cols], preferred_element_type=F32) + bs_ref[:, cols]
                o_ref[rows, cols] = (u[rows, cols] * s).astype(BF16)

    vec = pl.BlockSpec((1, d), lambda i: (0, 0))
    return pl.pallas_call(
        body, name=name, grid=(t // tb,),
        in_specs=[pl.BlockSpec((tb, d2), lambda i: (i, 0)), vec, vec,
                  pl.BlockSpec(wm.shape, lambda i: (0, 0, 0)), pl.BlockSpec((pb, d), lambda i: (0, 0))],
        out_specs=pl.BlockSpec((tb, d), lambda i: (i, 0)),
        out_shape=jax.ShapeDtypeStruct((t, d), BF16), compiler_params=_params("parallel"),
    )(uv, ln_g.reshape(1, d), ln_b.reshape(1, d), wm, bs_full)


def _sgu_bwd(uv, ln_g, ln_b, wm, bs_full, dgated, name):
    t, d2 = uv.shape
    d = d2 // 2
    ng, pb, _ = wm.shape
    gd = d // ng
    tb = _pick(t, ROW_TILES[1:] or ROW_TILES)
    nb = t // tb

    def body(uv_ref, g_ref, b_ref, w_ref, bs_ref, dg_ref, o_ref, dw_ref, dbs_ref, dlg_ref, dlb_ref,
             du_s, dvn_s, dbs_s):
        i = pl.program_id(0)

        @pl.when(i == 0)
        def _():
            dw_ref[...] = jnp.zeros_like(dw_ref)
            dbs_s[...] = jnp.zeros_like(dbs_s)
            dlg_ref[...] = jnp.zeros_like(dlg_ref)
            dlb_ref[...] = jnp.zeros_like(dlb_ref)

        upre = uv_ref[:, :d]
        vpre = uv_ref[:, d:]
        u = _gelu(upre)
        v = _gelu(vpre)
        mu = jnp.mean(v, axis=-1, keepdims=True)
        vc = v - mu
        var = jnp.mean(vc * vc, axis=-1, keepdims=True)
        rstd = lax.rsqrt(var + LN_EPS)
        xhat = vc * rstd
        vn = (xhat * g_ref[...] + b_ref[...]).astype(BF16)
        dgt = dg_ref[...].astype(F32)
        for r in range(tb // pb):
            rows = slice(r * pb, (r + 1) * pb)
            for gi in range(ng):
                cols = slice(gi * gd, (gi + 1) * gd)
                vblk = vn[rows, cols]
                s = jnp.dot(w_ref[gi], vblk, preferred_element_type=F32) + bs_ref[:, cols]
                dblk = dgt[rows, cols]
                du_s[rows, cols] = dblk * s
                ds = dblk * u[rows, cols]
                dsb = ds.astype(BF16)
                dvn_s[rows, cols] = lax.dot_general(w_ref[gi], dsb, (((0,), (0,)), ((), ())),
                                                    preferred_element_type=F32)
                dw_ref[gi] += lax.dot_general(dsb, vblk, (((1,), (1,)), ((), ())), preferred_element_type=F32)
                dbs_s[:, cols] += ds
        dvn = dvn_s[...]
        dlg_ref[...] += jnp.sum(dvn * xhat, axis=0, keepdims=True)
        dlb_ref[...] += jnp.sum(dvn, axis=0, keepdims=True)
        dxh = dvn * g_ref[...]
        m1 = jnp.mean(dxh, axis=-1, keepdims=True)
        m2 = jnp.mean(dxh * xhat, axis=-1, keepdims=True)
        dv = rstd * (dxh - m1 - xhat * m2)
        o_ref[:, :d] = (du_s[...] * _gelu_grad(upre)).astype(BF16)
        o_ref[:, d:] = (dv * _gelu_grad(vpre)).astype(BF16)

        @pl.when(i == nb - 1)
        def _():
            lane = lax.broadcasted_iota(jnp.int32, (pb, LANES), 1)
            acc = jnp.zeros((pb, LANES), F32)
            for gi in range(ng):
                col = jnp.sum(dbs_s[:, gi * gd:(gi + 1) * gd], axis=1, keepdims=True)
                acc = acc + jnp.where(lane == gi, col, 0.0)
            dbs_ref[...] = acc

    vec = pl.BlockSpec((1, d), lambda i: (0, 0))
    duv, dw, dbs, dlg, dlb = pl.pallas_call(
        body, name=name, grid=(nb,),
        in_specs=[pl.BlockSpec((tb, d2), lambda i: (i, 0)), vec, vec,
                  pl.BlockSpec(wm.shape, lambda i: (0, 0, 0)), pl.BlockSpec((pb, d), lambda i: (0, 0)),
                  pl.BlockSpec((tb, d), lambda i: (i, 0))],
        out_specs=[pl.BlockSpec((tb, d2), lambda i: (i, 0)), pl.BlockSpec(wm.shape, lambda i: (0, 0, 0)),
                   pl.BlockSpec((pb, LANES), lambda i: (0, 0)), vec, vec],
        out_shape=[jax.ShapeDtypeStruct((t, d2), BF16), jax.ShapeDtypeStruct(wm.shape, F32),
                   jax.ShapeDtypeStruct((pb, LANES), F32), jax.ShapeDtypeStruct((1, d), F32),
                   jax.ShapeDtypeStruct((1, d), F32)],
        scratch_shapes=[pltpu.VMEM((tb, d), F32), pltpu.VMEM((tb, d), F32), pltpu.VMEM((pb, d), F32)],
        compiler_params=_params("arbitrary"),
    )(uv, ln_g.reshape(1, d), ln_b.reshape(1, d), wm, bs_full, dgated)
    return duv, dw, dbs, dlg[0], dlb[0]


def _adamw(w, g, m, v, name):
    shape = w.shape
    cols = shape[-1]
    rows = w.size // cols
    tr = _pick(rows, (512, 256, 352, 128, 64, 32, 16, 8))

    def body(w_ref, g_ref, m_ref, v_ref, d_ref, mo_ref, vo_ref):
        d_ref[...], mo_ref[...], vo_ref[...] = _adam_update(w_ref[...], g_ref[...], m_ref[...], v_ref[...])

    spec = pl.BlockSpec((tr, cols), lambda i: (i, 0))
    outs = pl.pallas_call(
        body, name=name, grid=(rows // tr,),
        in_specs=[spec] * 4, out_specs=[spec] * 3,
        out_shape=[jax.ShapeDtypeStruct((rows, cols), F32)] * 3,
        compiler_params=_params("parallel"),
    )(*[a.reshape(rows, cols) for a in (w, g, m, v)])
    return [o.reshape(shape) for o in outs]


ANY = pl.BlockSpec(memory_space=pl.ANY)


def _place():
    return lax.axis_index("x"), lax.axis_index("y"), lax.axis_index("c")


def _all_gather(shards, name):
    n = len(shards)

    def body(*refs):
        x_refs, out_refs = refs[:n], refs[n:2 * n]
        send_sems, recv_sems, local_sems = refs[2 * n:]
        x, y, c = _place()
        me, sibling = (x, y, c), (x, y, 1 - c)
        chips = [(1 - x, y), (x, 1 - y), (1 - x, 1 - y)]

        def copy(a, k, block, to, own=False):
            px, py, pc = block
            rows = out_refs[a].at[4 * px + 2 * py + pc]
            return pltpu.make_async_remote_copy(
                src_ref=x_refs[a] if own else rows, dst_ref=rows,
                send_sem=send_sems.at[7 * a + k], recv_sem=recv_sems.at[7 * a + k],
                device_id=to, device_id_type=MESH)

        started = []
        for a in range(n):
            mine = pltpu.make_async_copy(x_refs[a], out_refs[a].at[4 * x + 2 * y + c], local_sems.at[a])
            mine.start()
            started.append(mine)
        sends = []
        for a in range(n):
            first = [copy(a, 0, me, sibling, own=True)]
            first += [copy(a, 1 + j, me, (*chip, c), own=True) for j, chip in enumerate(chips)]
            for cp in first:
                cp.start()
            sends += first
        for j, chip in enumerate(chips):
            for a in range(n):
                copy(a, 1 + j, (*chip, c), me).wait_recv()
                fwd = copy(a, 4 + j, (*chip, c), sibling)
                fwd.start()
                sends.append(fwd)
        for a in range(n):
            copy(a, 0, sibling, me).wait_recv()
            for j, chip in enumerate(chips):
                copy(a, 4 + j, (*chip, 1 - c), me).wait_recv()
        for cp in sends:
            cp.wait_send()
        for mine in started:
            mine.wait()

    return pl.pallas_call(
        body, name=name, in_specs=[ANY] * n, out_specs=[ANY] * n,
        out_shape=[jax.ShapeDtypeStruct((N_DEV,) + s.shape, s.dtype) for s in shards],
        scratch_shapes=[pltpu.SemaphoreType.DMA((7 * n,)), pltpu.SemaphoreType.DMA((7 * n,)),
                        pltpu.SemaphoreType.DMA((n,))],
    )(*shards)


def _rs_sibling_exchange(packed, name):
    n = len(packed)

    def body(*refs):
        p_refs, r_refs = refs[:n], refs[n:2 * n]
        send_sems, recv_sems = refs[2 * n:]
        x, y, c = _place()
        cps = []
        for a in range(n):
            for j in range(4):
                cps.append(pltpu.make_async_remote_copy(
                    src_ref=p_refs[a].at[2 * j + (1 - c)], dst_ref=r_refs[a].at[j],
                    send_sem=send_sems.at[4 * a + j], recv_sem=recv_sems.at[4 * a + j],
                    device_id=(x, y, 1 - c), device_id_type=MESH))
        for cp in cps:
            cp.start()
        for cp in cps:
            cp.wait()

    return pl.pallas_call(
        body, name=name, in_specs=[ANY] * n, out_specs=[ANY] * n,
        out_shape=[jax.ShapeDtypeStruct((4,) + p.shape[1:], p.dtype) for p in packed],
        scratch_shapes=[pltpu.SemaphoreType.DMA((4 * n,)), pltpu.SemaphoreType.DMA((4 * n,))],
    )(*packed)


def _rs_chip_sum(packed, from_sibling, c_idx, name):
    _, r, cc = packed.shape
    tr = _pick(r, (512, 256, 352, 128))

    def body(c_ref, a_ref, b_ref, o_ref):
        o_ref[...] = (a_ref[...].astype(F32) + b_ref[...].astype(F32)).astype(o_ref.dtype)

    return pl.pallas_call(
        body, name=name,
        grid_spec=pltpu.PrefetchScalarGridSpec(
            num_scalar_prefetch=1, grid=(4, r // tr),
            in_specs=[pl.BlockSpec((None, tr, cc), lambda j, i, c_ref: (2 * j + c_ref[0], i, 0)),
                      pl.BlockSpec((None, tr, cc), lambda j, i, c_ref: (j, i, 0))],
            out_specs=pl.BlockSpec((None, tr, cc), lambda j, i, c_ref: (j, i, 0))),
        out_shape=jax.ShapeDtypeStruct((4, r, cc), packed.dtype),
        compiler_params=_params("parallel", "parallel"),
    )(c_idx, packed, from_sibling)


def _rs_chip_exchange(partial, name):
    n = len(partial)

    def body(*refs):
        p_refs, r_refs = refs[:n], refs[n:2 * n]
        send_sems, recv_sems = refs[2 * n:]
        x, y, c = _place()
        chips = [(1 - x, y), (x, 1 - y), (1 - x, 1 - y)]
        cps = []
        for a in range(n):
            for k, (tx, ty) in enumerate(chips):
                cps.append(pltpu.make_async_remote_copy(
                    src_ref=p_refs[a].at[2 * tx + ty], dst_ref=r_refs[a].at[k],
                    send_sem=send_sems.at[3 * a + k], recv_sem=recv_sems.at[3 * a + k],
                    device_id=(tx, ty, c), device_id_type=MESH))
        for cp in cps:
            cp.start()
        for cp in cps:
            cp.wait()

    return pl.pallas_call(
        body, name=name, in_specs=[ANY] * n, out_specs=[ANY] * n,
        out_shape=[jax.ShapeDtypeStruct((3,) + p.shape[1:], p.dtype) for p in partial],
        scratch_shapes=[pltpu.SemaphoreType.DMA((3 * n,)), pltpu.SemaphoreType.DMA((3 * n,))],
    )(*partial)


def _adam_update(w, g, m, v):
    mn = ADAM_B1 * m + (1.0 - ADAM_B1) * g
    vn = ADAM_B2 * v + (1.0 - ADAM_B2) * (g * g)
    m_hat = mn / (1.0 - ADAM_B1 ** ADAM_STEP)
    v_hat = vn / (1.0 - ADAM_B2 ** ADAM_STEP)
    return -ADAM_LR * (m_hat / (jnp.sqrt(v_hat) + ADAM_EPS) + ADAM_WD * w), mn, vn


def _rs_final_adamw(partial, received, chip_idx, w, m, v, name):
    _, r, cc = partial.shape
    tr = _pick(r, (512, 256, 352, 128))

    def body(c_ref, a_ref, r_ref, w_ref, m_ref, v_ref, g_ref, d_ref, mo_ref, vo_ref):
        g = a_ref[...].astype(F32)
        for k in range(3):
            g = g + r_ref[k].astype(F32)
        g_ref[...] = g
        d_ref[...], mo_ref[...], vo_ref[...] = _adam_update(w_ref[...], g, m_ref[...], v_ref[...])

    row = pl.BlockSpec((tr, cc), lambda i, c_ref: (i, 0))
    return pl.pallas_call(
        body, name=name,
        grid_spec=pltpu.PrefetchScalarGridSpec(
            num_scalar_prefetch=1, grid=(r // tr,),
            in_specs=[pl.BlockSpec((None, tr, cc), lambda i, c_ref: (c_ref[0], i, 0)),
                      pl.BlockSpec((3, tr, cc), lambda i, c_ref: (0, i, 0)), row, row, row],
            out_specs=[row] * 4),
        out_shape=[jax.ShapeDtypeStruct((r, cc), F32)] * 4,
        compiler_params=_params("parallel"),
    )(chip_idx, partial, received, w.reshape(r, cc), m.reshape(r, cc), v.reshape(r, cc))


def _all_reduce_small(vals, name):
    r, cc = vals.shape

    def body(v_ref, o_ref, buf, send_sems, recv_sems):
        x, y, c = _place()
        me = 4 * x + 2 * y + c
        buf[0] = v_ref[...]
        cps = []
        for k in range(1, N_DEV):
            kx, ky, kc = (k >> 2) & 1, (k >> 1) & 1, k & 1
            peer = (1 - x if kx else x, 1 - y if ky else y, 1 - c if kc else c)
            cps.append(pltpu.make_async_remote_copy(
                src_ref=buf.at[0], dst_ref=buf.at[k], send_sem=send_sems.at[k - 1],
                recv_sem=recv_sems.at[k - 1], device_id=peer, device_id_type=MESH))
        for cp in cps:
            cp.start()
        for cp in cps:
            cp.wait()
        acc = buf[jnp.bitwise_xor(me, 0)]
        for dev in range(1, N_DEV):
            acc = acc + buf[jnp.bitwise_xor(me, dev)]
        o_ref[...] = acc

    vm = pl.BlockSpec(memory_space=pltpu.VMEM)
    return pl.pallas_call(
        body, name=name, in_specs=[vm], out_specs=vm,
        out_shape=jax.ShapeDtypeStruct((r, cc), F32),
        scratch_shapes=[pltpu.VMEM((N_DEV, r, cc), F32), pltpu.SemaphoreType.DMA((7,)),
                        pltpu.SemaphoreType.DMA((7,))],
        compiler_params=pltpu.CompilerParams(vmem_limit_bytes=VMEM_LIMIT),
    )(vals)


def _lanes(flat):
    pad = (-flat.shape[0]) % (SUBLANES * LANES)
    return jnp.pad(flat, (0, pad)).reshape(-1, LANES)


def kernel(x, even_w_in, even_b_f, even_conv_w, even_w_out, odd_w_in, odd_v_ln_g, odd_v_ln_b, odd_w_s, odd_b_s, odd_w_out, mix_ln_g, mix_ln_b, ffn_w_in, ffn_w_out, ffn_ln_g, ffn_ln_b, loss_target, m_even_w_in, m_even_b_f, m_even_conv_w, m_even_w_out, m_odd_w_in, m_odd_v_ln_g, m_odd_v_ln_b, m_odd_w_s, m_odd_b_s, m_odd_w_out, m_mix_ln_g, m_mix_ln_b, m_ffn_w_in, m_ffn_w_out, m_ffn_ln_g, m_ffn_ln_b, v_even_w_in, v_even_b_f, v_even_conv_w, v_even_w_out, v_odd_w_in, v_odd_v_ln_g, v_odd_v_ln_b, v_odd_w_s, v_odd_b_s, v_odd_w_out, v_mix_ln_g, v_mix_ln_b, v_ffn_w_in, v_ffn_w_out, v_ffn_ln_g, v_ffn_ln_b):
    t, d = x.shape[1], x.shape[2]
    nh = even_b_f.shape[-1]
    w = even_conv_w.shape[-1] * N_DEV
    dh = w // nh
    scale = dh ** -0.5
    e_in = even_w_in.shape[-1] * N_DEV
    f2 = ffn_w_in.shape[-1] * N_DEV
    f = f2 // 2
    ng, pb = odd_w_s.shape[1], odd_w_s.shape[2]
    assert e_in == 6 * w + nh and nh <= SUBLANES and (6 * w) % LANES == 0 and d % N_DEV == 0
    mx, my, mc = _place()
    me = 4 * mx + 2 * my + mc

    big = [even_w_in[0], even_w_out[0], odd_w_in[0], odd_w_out[0],
           ffn_w_in[0], ffn_w_in[1], ffn_w_out[0], ffn_w_out[1]]
    g_in0, g_out0, g_in1, g_out1, g_fi0, g_fi1, g_fo0, g_fo1 = _all_gather([s.astype(BF16) for s in big], "ag_weights")
    w_out0, w_out1 = g_out0.reshape(2 * w, d), g_out1.reshape(d, d)
    w_fo0, w_fo1 = g_fo0.reshape(f, d), g_fo1.reshape(f, d)
    nb = N_DEV // 2
    w_fi0, w_fi1 = g_fi0.reshape(2, nb, d, -1), g_fi1.reshape(2, nb, d, -1)
    w_in0 = g_in0.transpose(1, 0, 2).reshape(d, e_in)
    w_all0 = jnp.concatenate([w_in0[:, :3 * w], w_in0[:, 3 * w + nh:], w_in0[:, 3 * w:3 * w + nh],
                              jnp.zeros((d, LANES - nh), BF16)], axis=1)

    cs, vs = even_conv_w.shape[-1], odd_v_ln_g.shape[-1]
    small_mine = jnp.concatenate([
        lax.dynamic_update_slice(jnp.zeros((3, w), F32), even_conv_w[0], (0, me * cs)).reshape(-1),
        lax.dynamic_update_slice(jnp.zeros((d,), F32), odd_v_ln_g[0], (me * vs,)),
        lax.dynamic_update_slice(jnp.zeros((d,), F32), odd_v_ln_b[0], (me * vs,))])
    small_all = _all_reduce_small(_lanes(small_mine), "ag_small").reshape(-1)
    conv_w = small_all[:3 * w].reshape(3, w)
    vln_g = small_all[3 * w:3 * w + d]
    vln_b = small_all[3 * w + d:3 * w + 2 * d]

    bf_pad = jnp.pad(even_b_f[0], (0, LANES - nh)).reshape(1, LANES)
    chunk = jnp.arange(pb) // (pb // 2)
    ws_mask = (chunk[None, :] <= chunk[:, None])[None]
    wm = jnp.where(ws_mask, odd_w_s[0], 0.0).astype(BF16)
    bs_full = jnp.repeat(odd_b_s[0].T, d // ng, axis=1)

    x0 = x[0]
    tgt = loss_target[0]
    fcol = 6 * w // LANES
    p0 = _mm(x0, w_all0, "nn", F32, "l0_in_proj")
    cgate = _fgate_fwd(p0, bf_pad, fcol, nh, "l0_fgate")
    assert dh + 7 <= LANES
    qa, ka, va = _attn_pack(p0, cgate, w, nh, scale, "l0_attn_pack")
    oa = _attn_fwd(qa, ka, va, dh, "l0_attn")
    attn, = _attn_unpack(oa, w, nh, 1.0, None, 1.0, "l0_attn_unpack")
    yconv = _conv_fwd(p0, conv_w, w, 3, "l0_conv")
    mix = jnp.concatenate([attn, yconv], axis=1)
    m0 = _mm(mix, w_out0, "nn", F32, "l0_out_proj")
    x1, x1b = _ln_fwd(x0, m0, mix_ln_g[0], mix_ln_b[0], "l0_mix_ln")
    h0, gu0 = _ffn_in_swiglu(x1b, w_fi0, "l0_ffn_in")
    f0 = _mm_blk_fwd(h0, w_fo0, F32, "l0_ffn_out")
    x2, x2b = _ln_fwd(x1, f0, ffn_ln_g[0], ffn_ln_b[0], "l0_ffn_ln")

    uv = _mm_cols_fwd(x2b, g_in1, False, F32, "l1_in_proj")
    gated = _sgu_fwd(uv, vln_g, vln_b, wm, bs_full, "l1_sgu")
    m1 = _mm(gated, w_out1, "nn", F32, "l1_out_proj")
    x3, x3b = _ln_fwd(x2, m1, mix_ln_g[1], mix_ln_b[1], "l1_mix_ln")
    h1, gu1 = _ffn_in_swiglu(x3b, w_fi1, "l1_ffn_in")
    f1 = _mm_blk_fwd(h1, w_fo1, F32, "l1_ffn_out")
    x4, _ = _ln_fwd(x3, f1, ffn_ln_g[1], ffn_ln_b[1], "l1_ffn_ln")
    loss_part, dy4 = _loss(x4, tgt, "loss")

    dz4, g_ffn_g1, g_ffn_b1 = _ln_bwd(x3, f1, ffn_ln_g[1], dy4, 1.0, None, "l1_ffn_ln_bwd")
    gd_fo1 = _mm_blk_dw(h1, dz4, BF16, "l1_ffn_out_dw").reshape(N_DEV, -1, d)
    dgu1 = _ffn_out_dx_swiglu(dz4, w_fo1, gu1, "l1_ffn_out_dx").reshape(N_DEV, t, -1)
    gd_fi1 = _mm_cols_dw(x3b, dgu1, N_DEV, True, BF16, "l1_ffn_in_dw")
    dx3 = _mm_cols_dx(dgu1, g_fi1, True, F32, "l1_ffn_in_dx")
    dz3, g_mix_g1, g_mix_b1 = _ln_bwd(x2, m1, mix_ln_g[1], dz4, ALPHA, dx3, "l1_mix_ln_bwd")
    gd_out1 = _mm(gated, dz3, "tn", BF16, "l1_out_proj_dw").reshape(N_DEV, -1, d)
    dgated = _mm(dz3, w_out1, "nt", BF16, "l1_out_proj_dx")
    duv, g_wm, g_bs_t, g_vln_g, g_vln_b = _sgu_bwd(uv, vln_g, vln_b, wm, bs_full, dgated, "l1_sgu_bwd")
    gd_in1 = _mm_cols_dw(x2b, duv, N_DEV, False, BF16, "l1_in_proj_dw")
    dx2 = _mm_cols_dx(duv, g_in1, False, F32, "l1_in_proj_dx")

    dz2, g_ffn_g0, g_ffn_b0 = _ln_bwd(x1, f0, ffn_ln_g[0], dz3, ALPHA, dx2, "l0_ffn_ln_bwd")
    gd_fo0 = _mm_blk_dw(h0, dz2, BF16, "l0_ffn_out_dw").reshape(N_DEV, -1, d)
    dgu0 = _ffn_out_dx_swiglu(dz2, w_fo0, gu0, "l0_ffn_out_dx").reshape(N_DEV, t, -1)
    gd_fi0 = _mm_cols_dw(x1b, dgu0, N_DEV, True, BF16, "l0_ffn_in_dw")
    dx1 = _mm_cols_dx(dgu0, g_fi0, True, F32, "l0_ffn_in_dx")
    dz1, g_mix_g0, g_mix_b0 = _ln_bwd(x0, m0, mix_ln_g[0], dz2, ALPHA, dx1, "l0_mix_ln_bwd")
    gd_out0 = _mm(mix, dz1, "tn", BF16, "l0_out_proj_dw").reshape(N_DEV, -1, d)
    dmix = _mm(dz1, w_out0, "nt", F32, "l0_out_proj_dx")
    d_b, d_c, d_h, g_conv = _conv_bwd(p0, conv_w, dmix, w, 3, "l0_conv_bwd")
    doa, qa2 = _attn_pack_bwd(dmix, oa, qa, w, nh, "l0_attn_pack_bwd")
    dqa, dka, dva = _attn_bwd(qa2, ka, va, doa, "l0_attn_bwd")
    dq, dcq = _attn_unpack(dqa, w, nh, scale, dh + 3, 1.0, "l0_attn_unpack_dq")
    dk, dck = _attn_unpack(dka, w, nh, 1.0, dh, -1.0, "l0_attn_unpack_dk")
    dv, = _attn_unpack(dva, w, nh, 1.0, None, 1.0, "l0_attn_unpack_dv")
    dzf, g_bf = _fgate_bwd(p0, bf_pad, dcq, dck, fcol, nh, "l0_fgate_bwd")
    dp0 = jnp.concatenate([dq, dk, dv, d_b.astype(BF16), d_c.astype(BF16), d_h.astype(BF16), dzf.astype(BF16)], axis=1)
    g_all0 = _mm(x0, dp0, "tn", F32, "l0_in_proj_dw")
    dx0 = _mm(dp0, w_all0, "nt", F32, "l0_in_proj_dx")
    grad_x = _axpy(ALPHA, dz1, dx0, "grad_x")
    gd_in0 = jnp.concatenate([g_all0[:, :3 * w], g_all0[:, 6 * w:6 * w + nh], g_all0[:, 3 * w:6 * w]], axis=1)
    gd_in0 = gd_in0.reshape(d, N_DEV, -1).transpose(1, 0, 2).astype(BF16)

    big_g = [gd_in0, gd_out0, gd_in1, gd_out1, gd_fi0, gd_fi1, gd_fo0, gd_fo1]
    big_m = [m_even_w_in[0], m_even_w_out[0], m_odd_w_in[0], m_odd_w_out[0],
             m_ffn_w_in[0], m_ffn_w_in[1], m_ffn_w_out[0], m_ffn_w_out[1]]
    big_v = [v_even_w_in[0], v_even_w_out[0], v_odd_w_in[0], v_odd_w_out[0],
             v_ffn_w_in[0], v_ffn_w_in[1], v_ffn_w_out[0], v_ffn_w_out[1]]
    big_names = ["even_w_in", "even_w_out", "odd_w_in", "odd_w_out", "ffn_w_in0", "ffn_w_in1", "ffn_w_out0", "ffn_w_out1"]
    c_idx = mc.reshape(1).astype(jnp.int32)
    chip_idx = (2 * mx + my).reshape(1).astype(jnp.int32)
    from_sib = _rs_sibling_exchange(big_g, "rs_sibling")
    partial = [_rs_chip_sum(g, s, c_idx, "rs_chip_sum_" + n) for g, s, n in zip(big_g, from_sib, big_names)]
    received = _rs_chip_exchange(partial, "rs_chips")
    upd = [_rs_final_adamw(p, r, chip_idx, wt, mt, vt, "rs_final_adamw_" + n)
           for p, r, wt, mt, vt, n in zip(partial, received, big, big_m, big_v, big_names)]
    big_out = {}
    for i, n in enumerate(["even_w_in", "even_w_out", "odd_w_in", "odd_w_out"]):
        big_out[n] = [o[None] for o in upd[i]]
    big_out["ffn_w_in"] = [jnp.stack([a, b]) for a, b in zip(upd[4], upd[5])]
    big_out["ffn_w_out"] = [jnp.stack([a, b]) for a, b in zip(upd[6], upd[7])]

    g_ws = jnp.where(ws_mask, g_wm, 0.0)
    g_bs = g_bs_t[:, :ng].T
    small_g = [g_bf[:nh], g_conv, g_vln_g, g_vln_b, g_ws, g_bs,
               jnp.stack([g_mix_g0, g_mix_g1]), jnp.stack([g_mix_b0, g_mix_b1]),
               jnp.stack([g_ffn_g0, g_ffn_g1]), jnp.stack([g_ffn_b0, g_ffn_b1])]
    small_sum = _all_reduce_small(_lanes(jnp.concatenate([a.reshape(-1) for a in small_g])), "ar_small_grads")
    small_sum = small_sum.reshape(-1)
    outs_small = []
    off = 0
    for a in small_g:
        outs_small.append(small_sum[off:off + a.size].reshape(a.shape))
        off += a.size
    gr_bf, gr_conv, gr_vg, gr_vb, gr_ws, gr_bs, gr_mg, gr_mb, gr_fg, gr_fb = outs_small

    loss = lax.psum(loss_part, ("x", "y", "c"))

    grads = {
        "even_b_f": gr_bf[None],
        "even_conv_w": lax.dynamic_slice(gr_conv, (0, me * cs), (3, cs))[None],
        "odd_v_ln_g": lax.dynamic_slice(gr_vg, (me * vs,), (vs,))[None],
        "odd_v_ln_b": lax.dynamic_slice(gr_vb, (me * vs,), (vs,))[None],
        "odd_w_s": gr_ws[None], "odd_b_s": gr_bs[None],
        "mix_ln_g": gr_mg, "mix_ln_b": gr_mb, "ffn_ln_g": gr_fg, "ffn_ln_b": gr_fb,
    }
    weights = dict(even_w_in=even_w_in, even_b_f=even_b_f, even_conv_w=even_conv_w, even_w_out=even_w_out,
                   odd_w_in=odd_w_in, odd_v_ln_g=odd_v_ln_g, odd_v_ln_b=odd_v_ln_b, odd_w_s=odd_w_s,
                   odd_b_s=odd_b_s, odd_w_out=odd_w_out, mix_ln_g=mix_ln_g, mix_ln_b=mix_ln_b,
                   ffn_w_in=ffn_w_in, ffn_w_out=ffn_w_out, ffn_ln_g=ffn_ln_g, ffn_ln_b=ffn_ln_b)
    moms = dict(even_w_in=(m_even_w_in, v_even_w_in), even_b_f=(m_even_b_f, v_even_b_f),
                even_conv_w=(m_even_conv_w, v_even_conv_w), even_w_out=(m_even_w_out, v_even_w_out),
                odd_w_in=(m_odd_w_in, v_odd_w_in), odd_v_ln_g=(m_odd_v_ln_g, v_odd_v_ln_g),
                odd_v_ln_b=(m_odd_v_ln_b, v_odd_v_ln_b), odd_w_s=(m_odd_w_s, v_odd_w_s),
                odd_b_s=(m_odd_b_s, v_odd_b_s), odd_w_out=(m_odd_w_out, v_odd_w_out),
                mix_ln_g=(m_mix_ln_g, v_mix_ln_g), mix_ln_b=(m_mix_ln_b, v_mix_ln_b),
                ffn_w_in=(m_ffn_w_in, v_ffn_w_in), ffn_w_out=(m_ffn_w_out, v_ffn_w_out),
                ffn_ln_g=(m_ffn_ln_g, v_ffn_ln_g), ffn_ln_b=(m_ffn_ln_b, v_ffn_ln_b))
    names = list(weights)
    gout, deltas, new_m, new_v = [], [], [], []
    for n in names:
        if n in big_out:
            gr, dlt, mn, vn = big_out[n]
        else:
            gr = grads[n]
            dlt, mn, vn = _adamw(weights[n], gr, moms[n][0], moms[n][1], "adamw_" + n)
        gout.append(gr.reshape(weights[n].shape))
        deltas.append(dlt.reshape(weights[n].shape))
        new_m.append(mn.reshape(weights[n].shape))
        new_v.append(vn.reshape(weights[n].shape))
    return (loss, grad_x[None], *gout, *deltas, *new_m, *new_v)
```

```python
import jax
import jax.numpy as jnp
from jax import lax
from jax.experimental import pallas as pl
from jax.experimental.pallas import tpu as pltpu

F32 = jnp.float32
BF16 = jnp.bfloat16
MESH = pl.DeviceIdType.MESH

DEPTH = 2
ALPHA = (2.0 * DEPTH) ** 0.25
LN_EPS = 1e-5
ADAM_LR = 0.001
ADAM_B1 = 0.9
ADAM_B2 = 0.999
ADAM_EPS = 1e-08
ADAM_WD = 0.01
ADAM_STEP = 10

N_DEV = 8
LANES = 128
SUBLANES = 8
VMEM_LIMIT = 48 * 1024 * 1024
NEG_BIG = -1e30
ROW_TILES = (512, 256, 128)


def _pick(n, cands):
    for c in cands:
        if c <= n and n % c == 0:
            return c
    return n


def _params(*sem):
    return pltpu.CompilerParams(dimension_semantics=sem, vmem_limit_bytes=VMEM_LIMIT)


NN = (((1,), (0,)), ((), ()))
NT = (((1,), (1,)), ((), ()))
TN = (((0,), (0,)), ((), ()))
M_TILES = (1024, 512, 1408, 256, 128)
N_TILES = (512, 640, 256, 128)
K_TILES = (1024, 512, 640, 1408, 256, 128)


def _mm_core(name, grid, a, b, a_spec, b_spec, o_spec, o_shape, o_dtype, dims, tile):
    nred = grid[2]

    def body(a_ref, b_ref, o_ref, *acc):
        part = lax.dot_general(a_ref[...].astype(BF16), b_ref[...].astype(BF16), dims, preferred_element_type=F32)
        if nred == 1:
            o_ref[...] = part.astype(o_ref.dtype)
            return
        acc_ref, = acc
        kk = pl.program_id(2)

        @pl.when(kk == 0)
        def _():
            acc_ref[...] = jnp.zeros_like(acc_ref)

        acc_ref[...] += part

        @pl.when(kk == nred - 1)
        def _():
            o_ref[...] = acc_ref[...].astype(o_ref.dtype)

    return pl.pallas_call(
        body, name=name, grid=grid, in_specs=[a_spec, b_spec], out_specs=o_spec,
        out_shape=jax.ShapeDtypeStruct(o_shape, o_dtype),
        scratch_shapes=[] if nred == 1 else [pltpu.VMEM(tile, F32)],
        compiler_params=_params("parallel", "parallel", "arbitrary"),
    )(a, b)


def _mm(a, b, mode, out_dtype, name):
    if mode == "nn":
        (m, k), (k2, n) = a.shape, b.shape
    elif mode == "nt":
        (m, k), (n, k2) = a.shape, b.shape
    else:
        (k, m), (k2, n) = a.shape, b.shape
    assert k == k2, (a.shape, b.shape, mode)
    tm, tn, tk = _pick(m, M_TILES), _pick(n, N_TILES), _pick(k, K_TILES)
    if mode == "nn":
        a_spec = pl.BlockSpec((tm, tk), lambda i, j, kk: (i, kk))
        b_spec = pl.BlockSpec((tk, tn), lambda i, j, kk: (kk, j))
        dims = NN
    elif mode == "nt":
        a_spec = pl.BlockSpec((tm, tk), lambda i, j, kk: (i, kk))
        b_spec = pl.BlockSpec((tn, tk), lambda i, j, kk: (j, kk))
        dims = NT
    else:
        a_spec = pl.BlockSpec((tk, tm), lambda i, j, kk: (kk, i))
        b_spec = pl.BlockSpec((tk, tn), lambda i, j, kk: (kk, j))
        dims = TN
    return _mm_core(name, (m // tm, n // tn, k // tk), a, b, a_spec, b_spec,
                    pl.BlockSpec((tm, tn), lambda i, j, kk: (i, j)), (m, n), out_dtype, dims, (tm, tn))


def _act_spec(blocked, rows, ns, row_ax, d_ax):
    if blocked:
        return pl.BlockSpec((None, rows, ns), lambda *g: (g[d_ax], g[row_ax], 0))
    return pl.BlockSpec((rows, ns), lambda *g: (g[row_ax], g[d_ax]))


def _mm_cols_fwd(a, g3, blocked, out_dtype, name):
    (t, k), (nd, k2, ns) = a.shape, g3.shape
    assert k == k2
    tm, tk = _pick(t, M_TILES), _pick(k, K_TILES)
    return _mm_core(name, (t // tm, nd, k // tk), a, g3,
                    pl.BlockSpec((tm, tk), lambda i, d, kk: (i, kk)),
                    pl.BlockSpec((None, tk, ns), lambda i, d, kk: (d, kk, 0)),
                    _act_spec(blocked, tm, ns, 0, 1), (nd, t, ns) if blocked else (t, nd * ns), out_dtype, NN, (tm, ns))


def _mm_cols_dx(dy, g3, blocked, out_dtype, name):
    nd, k, ns = g3.shape
    t = dy.shape[1] if blocked else dy.shape[0]
    tm, tn = _pick(t, M_TILES), _pick(k, (1024,) + N_TILES)
    return _mm_core(name, (t // tm, k // tn, nd), dy, g3,
                    _act_spec(blocked, tm, ns, 0, 2),
                    pl.BlockSpec((None, tn, ns), lambda i, j, d: (d, j, 0)),
                    pl.BlockSpec((tm, tn), lambda i, j, d: (i, j)), (t, k), out_dtype, NT, (tm, tn))


def _mm_cols_dw(a, dy, nd, blocked, out_dtype, name):
    t, k = a.shape
    ns = dy.shape[2] if blocked else dy.shape[1] // nd
    tmk, tk = _pick(k, M_TILES), _pick(t, K_TILES)
    return _mm_core(name, (nd, k // tmk, t // tk), a, dy,
                    pl.BlockSpec((tk, tmk), lambda d, j, kk: (kk, j)),
                    _act_spec(blocked, tk, ns, 2, 0),
                    pl.BlockSpec((None, tmk, ns), lambda d, j, kk: (d, j, 0)), (nd, k, ns), out_dtype, TN, (tmk, ns))


def _mm_blk_fwd(h3, w, out_dtype, name):
    (nb, t, ns), (_, n) = h3.shape, w.shape
    tm, tn = _pick(t, M_TILES), _pick(n, (1024,) + N_TILES)
    return _mm_core(name, (t // tm, n // tn, nb), h3, w,
                    pl.BlockSpec((None, tm, ns), lambda i, j, d: (d, i, 0)),
                    pl.BlockSpec((ns, tn), lambda i, j, d: (d, j)),
                    pl.BlockSpec((tm, tn), lambda i, j, d: (i, j)), (t, n), out_dtype, NN, (tm, tn))


def _mm_blk_dw(h3, dz, out_dtype, name):
    (nb, t, ns), (_, n) = h3.shape, dz.shape
    tn, tk = _pick(n, (1024,) + N_TILES), _pick(t, K_TILES)
    return _mm_core(name, (nb, n // tn, t // tk), h3, dz,
                    pl.BlockSpec((None, tk, ns), lambda d, j, kk: (d, kk, 0)),
                    pl.BlockSpec((tk, tn), lambda d, j, kk: (kk, j)),
                    pl.BlockSpec((ns, tn), lambda d, j, kk: (d, j)), (nb * ns, n), out_dtype, TN, (ns, tn))


def _ln_fwd(xa, xb, g, b, name):
    t, d = xa.shape
    tb = _pick(t, ROW_TILES)

    def body(xa_ref, xb_ref, g_ref, b_ref, y_ref, yb_ref):
        z = ALPHA * xa_ref[...] + xb_ref[...]
        mu = jnp.mean(z, axis=-1, keepdims=True)
        zc = z - mu
        var = jnp.mean(zc * zc, axis=-1, keepdims=True)
        y = zc * lax.rsqrt(var + LN_EPS) * g_ref[...] + b_ref[...]
        y_ref[...] = y
        yb_ref[...] = y.astype(BF16)

    row = pl.BlockSpec((tb, d), lambda i: (i, 0))
    vec = pl.BlockSpec((1, d), lambda i: (0, 0))
    return pl.pallas_call(
        body, name=name, grid=(t // tb,),
        in_specs=[row, row, vec, vec], out_specs=[row, row],
        out_shape=[jax.ShapeDtypeStruct((t, d), F32), jax.ShapeDtypeStruct((t, d), BF16)],
        compiler_params=_params("parallel"),
    )(xa, xb, g.reshape(1, d), b.reshape(1, d))


def _ln_bwd(xa, xb, g, dya, ca, dyb, name):
    t, d = xa.shape
    tb = _pick(t, ROW_TILES)
    two = dyb is not None

    def body(*refs):
        if two:
            xa_ref, xb_ref, g_ref, dya_ref, dyb_ref, dz_ref, dg_ref, db_ref = refs
            dy = ca * dya_ref[...] + dyb_ref[...]
        else:
            xa_ref, xb_ref, g_ref, dya_ref, dz_ref, dg_ref, db_ref = refs
            dy = ca * dya_ref[...]
        z = ALPHA * xa_ref[...] + xb_ref[...]
        mu = jnp.mean(z, axis=-1, keepdims=True)
        zc = z - mu
        var = jnp.mean(zc * zc, axis=-1, keepdims=True)
        rstd = lax.rsqrt(var + LN_EPS)
        xhat = zc * rstd
        dxh = dy * g_ref[...]
        m1 = jnp.mean(dxh, axis=-1, keepdims=True)
        m2 = jnp.mean(dxh * xhat, axis=-1, keepdims=True)
        dz_ref[...] = rstd * (dxh - m1 - xhat * m2)

        @pl.when(pl.program_id(0) == 0)
        def _():
            dg_ref[...] = jnp.zeros_like(dg_ref)
            db_ref[...] = jnp.zeros_like(db_ref)

        dg_ref[...] += jnp.sum(dy * xhat, axis=0, keepdims=True)
        db_ref[...] += jnp.sum(dy, axis=0, keepdims=True)

    row = pl.BlockSpec((tb, d), lambda i: (i, 0))
    vec = pl.BlockSpec((1, d), lambda i: (0, 0))
    ins = [xa, xb, g.reshape(1, d), dya] + ([dyb] if two else [])
    dz, dg, db = pl.pallas_call(
        body, name=name, grid=(t // tb,),
        in_specs=[row, row, vec, row] + ([row] if two else []),
        out_specs=[row, vec, vec],
        out_shape=[jax.ShapeDtypeStruct((t, d), F32), jax.ShapeDtypeStruct((1, d), F32),
                   jax.ShapeDtypeStruct((1, d), F32)],
        compiler_params=_params("arbitrary"),
    )(*ins)
    return dz, dg[0], db[0]


def _loss(y, target, name):
    t, d = y.shape
    tb = _pick(t, ROW_TILES)

    def body(y_ref, t_ref, dy_ref, l_ref):
        e = y_ref[...] - t_ref[...]
        dy_ref[...] = e * (1.0 / d)

        @pl.when(pl.program_id(0) == 0)
        def _():
            l_ref[...] = jnp.zeros_like(l_ref)

        l_ref[...] += 0.5 * jnp.sum(jnp.mean(e * e, axis=-1, keepdims=True))

    row = pl.BlockSpec((tb, d), lambda i: (i, 0))
    dy, l = pl.pallas_call(
        body, name=name, grid=(t // tb,),
        in_specs=[row, row], out_specs=[row, pl.BlockSpec((1, LANES), lambda i: (0, 0))],
        out_shape=[jax.ShapeDtypeStruct((t, d), F32), jax.ShapeDtypeStruct((1, LANES), F32)],
        compiler_params=_params("arbitrary"),
    )(y, target)
    return l[0, 0], dy


def _axpy(ca, a, b, name):
    t, d = a.shape
    tb = _pick(t, ROW_TILES)

    def body(a_ref, b_ref, o_ref):
        o_ref[...] = ca * a_ref[...] + b_ref[...]

    row = pl.BlockSpec((tb, d), lambda i: (i, 0))
    return pl.pallas_call(
        body, name=name, grid=(t // tb,), in_specs=[row, row], out_specs=row,
        out_shape=jax.ShapeDtypeStruct((t, d), F32), compiler_params=_params("parallel"),
    )(a, b)


def _ffn_in_swiglu(xb, g4, name):
    (t, k), (_, nb, _, ns) = xb.shape, g4.shape
    tm = _pick(t, M_TILES)

    def body(x_ref, w_ref, h_ref, gu_ref):
        xv = x_ref[...]
        gate = jnp.dot(xv, w_ref[0], preferred_element_type=F32)
        up = jnp.dot(xv, w_ref[1], preferred_element_type=F32)
        h_ref[...] = (gate * jax.nn.sigmoid(gate) * up).astype(BF16)
        gu_ref[0] = gate.astype(BF16)
        gu_ref[1] = up.astype(BF16)

    return pl.pallas_call(
        body, name=name, grid=(t // tm, nb),
        in_specs=[pl.BlockSpec((tm, k), lambda i, d: (i, 0)),
                  pl.BlockSpec((2, None, k, ns), lambda i, d: (0, d, 0, 0))],
        out_specs=[pl.BlockSpec((None, tm, ns), lambda i, d: (d, i, 0)),
                   pl.BlockSpec((2, None, tm, ns), lambda i, d: (0, d, i, 0))],
        out_shape=[jax.ShapeDtypeStruct((nb, t, ns), BF16), jax.ShapeDtypeStruct((2, nb, t, ns), BF16)],
        compiler_params=_params("parallel", "parallel"),
    )(xb, g4)


def _ffn_out_dx_swiglu(dz, w_out, gu4, name):
    (t, d), (_, nb, _, ns) = dz.shape, gu4.shape
    tm = _pick(t, M_TILES)

    def body(dz_ref, w_ref, gu_ref, o_ref):
        dh = lax.dot_general(dz_ref[...].astype(BF16), w_ref[...], NT, preferred_element_type=F32)
        gate = gu_ref[0].astype(F32)
        up = gu_ref[1].astype(F32)
        sg = jax.nn.sigmoid(gate)
        silu = gate * sg
        o_ref[0] = (dh * up * (sg + silu * (1.0 - sg))).astype(BF16)
        o_ref[1] = (dh * silu).astype(BF16)

    blk = pl.BlockSpec((2, None, tm, ns), lambda i, j: (0, j, i, 0))
    return pl.pallas_call(
        body, name=name, grid=(t // tm, nb),
        in_specs=[pl.BlockSpec((tm, d), lambda i, j: (i, 0)), pl.BlockSpec((ns, d), lambda i, j: (j, 0)), blk],
        out_specs=blk,
        out_shape=jax.ShapeDtypeStruct((2, nb, t, ns), BF16),
        compiler_params=_params("parallel", "parallel"),
    )(dz, w_out, gu4)


def _tri_matmul(tri, x):
    x1 = x.astype(BF16)
    r1 = x - x1.astype(F32)
    x2 = r1.astype(BF16)
    x3 = (r1 - x2.astype(F32)).astype(BF16)
    dot = lambda v: jnp.dot(tri, v, preferred_element_type=F32)
    return dot(x1) + dot(x2) + dot(x3)


def _fgate_fwd(proj, bf_pad, fcol, n_heads, name):
    t = proj.shape[0]
    tb = _pick(t, ROW_TILES)

    def body(p_ref, b_ref, c_ref, carry):
        @pl.when(pl.program_id(0) == 0)
        def _():
            carry[...] = jnp.zeros_like(carry)

        z = p_ref[...] + b_ref[...]
        lf = jnp.minimum(z, 0.0) - jnp.log1p(jnp.exp(-jnp.abs(z)))
        lane = lax.broadcasted_iota(jnp.int32, (tb, LANES), 1)
        lf = jnp.where(lane < n_heads, lf, 0.0)
        r = lax.broadcasted_iota(jnp.int32, (tb, tb), 0)
        s = lax.broadcasted_iota(jnp.int32, (tb, tb), 1)
        tri = (s <= r).astype(BF16)
        c = _tri_matmul(tri, lf) + carry[...]
        c_ref[...] = c
        carry[...] = c[tb - 1:tb, :]

    return pl.pallas_call(
        body, name=name, grid=(t // tb,),
        in_specs=[pl.BlockSpec((tb, LANES), lambda i: (i, fcol)), pl.BlockSpec((1, LANES), lambda i: (0, 0))],
        out_specs=pl.BlockSpec((tb, LANES), lambda i: (i, 0)),
        out_shape=jax.ShapeDtypeStruct((t, LANES), F32),
        scratch_shapes=[pltpu.VMEM((1, LANES), F32)],
        compiler_params=_params("arbitrary"),
    )(proj, bf_pad)


def _fgate_bwd(proj, bf_pad, dcq, dck, fcol, n_heads, name):
    t = proj.shape[0]
    tb = _pick(t, ROW_TILES)
    nb = t // tb

    def body(p_ref, b_ref, dcq_ref, dck_ref, dz_ref, db_ref, carry):
        @pl.when(pl.program_id(0) == 0)
        def _():
            carry[...] = jnp.zeros_like(carry)
            db_ref[...] = jnp.zeros_like(db_ref)

        r = lax.broadcasted_iota(jnp.int32, (tb, tb), 0)
        s = lax.broadcasted_iota(jnp.int32, (tb, tb), 1)
        tri = (s >= r).astype(BF16)
        dlf = _tri_matmul(tri, dcq_ref[...] + dck_ref[...]) + carry[...]
        carry[...] = dlf[0:1, :]
        z = p_ref[...] + b_ref[...]
        lane = lax.broadcasted_iota(jnp.int32, (tb, LANES), 1)
        dz = jnp.where(lane < n_heads, dlf * jax.nn.sigmoid(-z), 0.0)
        dz_ref[...] = dz
        db_ref[...] += jnp.sum(dz, axis=0, keepdims=True)

    dz, db = pl.pallas_call(
        body, name=name, grid=(nb,),
        in_specs=[pl.BlockSpec((tb, LANES), lambda i: (nb - 1 - i, fcol)),
                  pl.BlockSpec((1, LANES), lambda i: (0, 0)),
                  pl.BlockSpec((tb, LANES), lambda i: (nb - 1 - i, 0)),
                  pl.BlockSpec((tb, LANES), lambda i: (nb - 1 - i, 0))],
        out_specs=[pl.BlockSpec((tb, LANES), lambda i: (nb - 1 - i, 0)),
                   pl.BlockSpec((1, LANES), lambda i: (0, 0))],
        out_shape=[jax.ShapeDtypeStruct((t, LANES), F32), jax.ShapeDtypeStruct((1, LANES), F32)],
        scratch_shapes=[pltpu.VMEM((1, LANES), F32)],
        compiler_params=_params("arbitrary"),
    )(proj, bf_pad, dcq, dck)
    return dz, db[0]


def _split3(x):
    hi = x.astype(BF16)
    r = x - hi.astype(F32)
    mid = r.astype(BF16)
    return hi, mid, (r - mid.astype(F32)).astype(BF16)


def _attn_fwd(qa, ka, va, dh, name, gather=()):
    nh, t, da = qa.shape
    tq = _pick(t, ROW_TILES)
    hb = 2 if nh % 2 == 0 else 1
    heads = range(hb)
    n = len(gather)
    steps = (nh // hb, t // tq)

    def body(q_ref, k_ref, v_ref, *rest):
        x_refs, o_ref, g_refs = rest[:n], rest[n], rest[n + 1:2 * n + 1]
        m_s, acc_s = rest[2 * n + 1:2 * n + 3]
        qi = pl.program_id(1)
        if n:
            start, forward, finish = _gather_phases(x_refs, g_refs, *rest[2 * n + 3:])
            at = lambda hh, qq: jnp.logical_and(pl.program_id(0) == hh, qi == qq)
            pl.when(at(0, 0))(start)
            pl.when(at(steps[0] // 2, 0))(forward)
        m_s[...] = jnp.full(m_s.shape, NEG_BIG, F32)
        acc_s[...] = jnp.zeros_like(acc_s)

        def step(j, diagonal):
            off = pl.multiple_of(j * tq, tq)
            s = [lax.dot_general(q_ref[g], k_ref[g, pl.ds(off, tq), :], NT, preferred_element_type=F32)
                 for g in heads]
            if diagonal:
                row = lax.broadcasted_iota(jnp.int32, (tq, tq), 0)
                col = lax.broadcasted_iota(jnp.int32, (tq, tq), 1)
                s = [jnp.where(col > row, NEG_BIG, sg) for sg in s]
            m_prev = [m_s[g] for g in heads]
            m_new = [jnp.maximum(m_prev[g], jnp.max(s[g], axis=1, keepdims=True)) for g in heads]
            p = [jnp.exp(s[g] - m_new[g]).astype(BF16) for g in heads]
            pv = [jnp.dot(p[g], v_ref[g, pl.ds(off, tq), :], preferred_element_type=F32) for g in heads]
            for g in heads:
                acc_s[g] = jnp.exp(m_prev[g] - m_new[g]) * acc_s[g] + pv[g]
                m_s[g] = m_new[g]

        def loop(j, carry):
            step(j, False)
            return carry

        lax.fori_loop(0, qi, loop, 0)
        step(qi, True)
        lane = lax.broadcasted_iota(jnp.int32, (tq, da), 1)
        for g in heads:
            acc = acc_s[g]
            l = jnp.sum(jnp.where(lane == dh, acc, 0.0), axis=1, keepdims=True)
            o_ref[g] = jnp.where(lane == dh, m_s[g] + jnp.log(l), acc / l)
        if n:
            pl.when(at(steps[0] - 1, steps[1] - 1))(finish)

    full = pl.BlockSpec((hb, t, da), lambda h, qi: (h, 0, 0))
    blk = pl.BlockSpec((hb, tq, da), lambda h, qi: (h, qi, 0))
    return pl.pallas_call(
        body, name=name, grid=steps,
        in_specs=[blk, full, full] + [ANY] * n, out_specs=[blk] + [ANY] * n,
        out_shape=[jax.ShapeDtypeStruct((nh, t, da), F32)] + _gather_shapes(gather),
        scratch_shapes=[pltpu.VMEM((hb, tq, 1), F32), pltpu.VMEM((hb, tq, da), F32)] + (_gather_sems(n) if n else []),
        compiler_params=_params("arbitrary", "arbitrary"),
    )(qa, ka, va, *gather)


def _attn_bwd(qa, ka, va, doa, name, exchange=()):
    nh, t, da = qa.shape
    tq = _pick(t, ROW_TILES)
    nq = t // tq
    n = len(exchange)

    def body(q_ref, do_ref, k_ref, v_ref, *rest):
        p_refs, (dq_ref, dk_ref, dv_ref), r_refs = rest[:n], rest[n:n + 3], rest[n + 3:2 * n + 3]
        kj = pl.program_id(1)
        if n:
            start, finish = _chip_exchange_phases(p_refs, r_refs, *rest[2 * n + 3:])
            pl.when(jnp.logical_and(pl.program_id(0) == 0, kj == 0))(start)

        @pl.when(kj == 0)
        def _():
            dq_ref[...] = jnp.zeros_like(dq_ref)

        dk_ref[...] = jnp.zeros_like(dk_ref)
        dv_ref[...] = jnp.zeros_like(dv_ref)
        kb = k_ref[...]
        vb = v_ref[...]

        def step(i, diagonal):
            off = pl.multiple_of(i * tq, tq)
            qb = q_ref[pl.ds(off, tq), :]
            dob = do_ref[pl.ds(off, tq), :]
            st = lax.dot_general(kb, qb, NT, preferred_element_type=F32)
            if diagonal:
                row = lax.broadcasted_iota(jnp.int32, (tq, tq), 0)
                col = lax.broadcasted_iota(jnp.int32, (tq, tq), 1)
                st = jnp.where(row > col, NEG_BIG, st)
            pt = jnp.exp(st)
            dst = (pt * lax.dot_general(vb, dob, NT, preferred_element_type=F32)).astype(BF16)
            dv_ref[...] += jnp.dot(pt.astype(BF16), dob, preferred_element_type=F32)
            dk_ref[...] += jnp.dot(dst, qb, preferred_element_type=F32)
            dq_ref[pl.ds(off, tq), :] += lax.dot_general(dst, kb, TN, preferred_element_type=F32)

        def loop(i, carry):
            step(i, False)
            return carry

        step(kj, True)
        lax.fori_loop(kj + 1, nq, loop, 0)
        if n:
            pl.when(jnp.logical_and(pl.program_id(0) == nh - 1, kj == nq - 1))(finish)

    full = pl.BlockSpec((None, t, da), lambda h, j: (h, 0, 0))
    blk = pl.BlockSpec((None, tq, da), lambda h, j: (h, j, 0))
    return pl.pallas_call(
        body, name=name, grid=(nh, nq),
        in_specs=[full, full, blk, blk] + [ANY] * n, out_specs=[full, blk, blk] + [ANY] * n,
        out_shape=[jax.ShapeDtypeStruct((nh, t, da), F32)] * 3 + _chip_exchange_shapes(exchange),
        scratch_shapes=_chip_exchange_sems(n) if n else [],
        compiler_params=_params("arbitrary", "arbitrary"),
    )(qa, doa, ka, va, *exchange)


def _head_select(w, dh, h, to_heads):
    shape = (w, LANES) if to_heads else (LANES, w)
    r = lax.broadcasted_iota(jnp.int32, shape, 0)
    c = lax.broadcasted_iota(jnp.int32, shape, 1)
    nat, col = (r, c) if to_heads else (c, r)
    return jnp.logical_and(nat == col + h * dh, col < dh).astype(BF16)


def _column(x, lane, j):
    return jnp.sum(jnp.where(lane == j, x, 0.0), axis=1, keepdims=True)


def _bias_columns(lane, first, value):
    out = jnp.zeros(lane.shape, F32)
    for j, term in enumerate(_split3(value)):
        out = out + jnp.where(lane == first + j, -term.astype(F32), 0.0)
    return out


def _attn_pack(proj, cgate, w, nh, scale, name):
    t = proj.shape[0]
    dh = w // nh
    tb = _pick(t, ROW_TILES)

    def body(q_ref, k_ref, v_ref, c_ref, qa_ref, ka_ref, va_ref):
        lane = lax.broadcasted_iota(jnp.int32, (tb, LANES), 1)
        ones_qv = jnp.where(jnp.logical_and(lane >= dh, lane < dh + 3), 1.0, 0.0)
        ones_k = jnp.where(jnp.logical_and(lane >= dh + 3, lane < dh + 7), 1.0, 0.0)
        qb = (q_ref[...] * scale).astype(BF16)
        kb = k_ref[...].astype(BF16)
        vb = v_ref[...].astype(BF16)
        cblk = c_ref[...]
        for h in range(nh):
            sel = _head_select(w, dh, h, True)
            qa_ref[h] = (jnp.dot(qb, sel, preferred_element_type=F32) + ones_qv).astype(BF16)
            va_ref[h] = (jnp.dot(vb, sel, preferred_element_type=F32) + ones_qv).astype(BF16)
            bias = _bias_columns(lane, dh, _column(cblk, lane, h))
            ka_ref[h] = (jnp.dot(kb, sel, preferred_element_type=F32) + bias + ones_k).astype(BF16)

    col = lambda j: pl.BlockSpec((tb, w), lambda i: (i, j))
    out = pl.BlockSpec((nh, tb, LANES), lambda i: (0, i, 0))
    return pl.pallas_call(
        body, name=name, grid=(t // tb,),
        in_specs=[col(0), col(1), col(2), pl.BlockSpec((tb, LANES), lambda i: (i, 0))],
        out_specs=[out, out, out],
        out_shape=[jax.ShapeDtypeStruct((nh, t, LANES), BF16)] * 3,
        compiler_params=_params("parallel"),
    )(proj, proj, proj, cgate)


def _attn_pack_bwd(dmix, oa, qa, w, nh, name):
    t = dmix.shape[0]
    dh = w // nh
    tb = _pick(t, ROW_TILES)

    def body(d_ref, oa_ref, qa_ref, doa_ref, qa2_ref):
        lane = lax.broadcasted_iota(jnp.int32, (tb, LANES), 1)
        db = d_ref[...].astype(BF16)
        for h in range(nh):
            do_h = jnp.dot(db, _head_select(w, dh, h, True), preferred_element_type=F32)
            o_h = oa_ref[h]
            delta = jnp.sum(jnp.where(lane < dh, do_h * o_h, 0.0), axis=1, keepdims=True)
            doa_ref[h] = (do_h + _bias_columns(lane, dh, delta)).astype(BF16)
            qa2_ref[h] = (qa_ref[h].astype(F32) + _bias_columns(lane, dh + 4, _column(o_h, lane, dh))).astype(BF16)

    blk = pl.BlockSpec((nh, tb, LANES), lambda i: (0, i, 0))
    return pl.pallas_call(
        body, name=name, grid=(t // tb,),
        in_specs=[pl.BlockSpec((tb, w), lambda i: (i, 0)), blk, blk], out_specs=[blk, blk],
        out_shape=[jax.ShapeDtypeStruct((nh, t, LANES), BF16)] * 2,
        compiler_params=_params("parallel"),
    )(dmix, oa, qa)


def _attn_unpack(xa, w, nh, mult, sum_col, sum_sign, name):
    t = xa.shape[1]
    dh = w // nh
    tb = _pick(t, ROW_TILES)

    def body(x_ref, o_ref, *rest):
        lane = lax.broadcasted_iota(jnp.int32, (tb, LANES), 1)
        acc = jnp.zeros((tb, w), F32)
        cols = jnp.zeros((tb, LANES), F32)
        for h in range(nh):
            xh = x_ref[h]
            acc = acc + jnp.dot((xh * mult).astype(BF16), _head_select(w, dh, h, False), preferred_element_type=F32)
            if sum_col is not None:
                cols = cols + jnp.where(lane == h, sum_sign * _column(xh, lane, sum_col), 0.0)
        o_ref[...] = acc.astype(BF16)
        if sum_col is not None:
            rest[0][...] = cols

    nat = pl.BlockSpec((tb, w), lambda i: (i, 0))
    lanes = pl.BlockSpec((tb, LANES), lambda i: (i, 0))
    return pl.pallas_call(
        body, name=name, grid=(t // tb,),
        in_specs=[pl.BlockSpec((nh, tb, LANES), lambda i: (0, i, 0))],
        out_specs=[nat, lanes] if sum_col is not None else [nat],
        out_shape=[jax.ShapeDtypeStruct((t, w), BF16)] + ([jax.ShapeDtypeStruct((t, LANES), F32)]
                                                            if sum_col is not None else []),
        compiler_params=_params("parallel"),
    )(xa)


def _conv_fwd(proj, cw, w, bcol, name):
    t = proj.shape[0]
    tb = _pick(t, ROW_TILES)
    hb = tb // SUBLANES

    def body(b_ref, c_ref, h_ref, cp_ref, hp_ref, w_ref, y_ref):
        i = pl.program_id(0)
        zp = jnp.where(i > 0, cp_ref[...] * hp_ref[...], 0.0)
        zext = jnp.concatenate([zp, c_ref[...] * h_ref[...]], axis=0)
        z1 = pltpu.roll(zext, 1, 0)[SUBLANES:]
        z2 = pltpu.roll(zext, 2, 0)[SUBLANES:]
        y = w_ref[2:3, :] * zext[SUBLANES:] + w_ref[1:2, :] * z1 + w_ref[0:1, :] * z2
        y_ref[...] = (b_ref[...] * y).astype(BF16)

    cur = lambda j: pl.BlockSpec((tb, w), lambda i: (i, bcol + j))
    prev = lambda j: pl.BlockSpec((SUBLANES, w), lambda i: (jnp.maximum(i * hb - 1, 0), bcol + j))
    return pl.pallas_call(
        body, name=name, grid=(t // tb,),
        in_specs=[cur(0), cur(1), cur(2), prev(1), prev(2), pl.BlockSpec(cw.shape, lambda i: (0, 0))],
        out_specs=pl.BlockSpec((tb, w), lambda i: (i, 0)),
        out_shape=jax.ShapeDtypeStruct((t, w), BF16), compiler_params=_params("parallel"),
    )(proj, proj, proj, proj, proj, cw)


def _conv_bwd(proj, cw, dmix, w, bcol, name):
    t = proj.shape[0]
    tb = _pick(t, ROW_TILES)
    hb = tb // SUBLANES
    nb = t // tb
    n_ext = tb + SUBLANES

    def body(b_ref, c_ref, h_ref, cp_ref, hp_ref, bn_ref, d_ref, dn_ref, w_ref, db_ref, dc_ref, dh_ref, dw_ref):
        i = pl.program_id(0)
        c = c_ref[...]
        hh = h_ref[...]
        zp = jnp.where(i > 0, cp_ref[...] * hp_ref[...], 0.0)
        zext = jnp.concatenate([zp, c * hh], axis=0)
        z0 = zext[SUBLANES:]
        z1 = pltpu.roll(zext, 1, 0)[SUBLANES:]
        z2 = pltpu.roll(zext, 2, 0)[SUBLANES:]
        y = w_ref[2:3, :] * z0 + w_ref[1:2, :] * z1 + w_ref[0:1, :] * z2
        d = d_ref[...]
        db_ref[...] = d * y
        dy = d * b_ref[...]
        dyn = jnp.where(i < nb - 1, dn_ref[...] * bn_ref[...], 0.0)
        dext = jnp.concatenate([dy, dyn], axis=0)
        dy1 = pltpu.roll(dext, n_ext - 1, 0)[:tb]
        dy2 = pltpu.roll(dext, n_ext - 2, 0)[:tb]
        dz = w_ref[2:3, :] * dy + w_ref[1:2, :] * dy1 + w_ref[0:1, :] * dy2
        dc_ref[...] = dz * hh
        dh_ref[...] = dz * c

        @pl.when(i == 0)
        def _():
            dw_ref[...] = jnp.zeros_like(dw_ref)

        dw_ref[0:1, :] += jnp.sum(dy * z2, axis=0, keepdims=True)
        dw_ref[1:2, :] += jnp.sum(dy * z1, axis=0, keepdims=True)
        dw_ref[2:3, :] += jnp.sum(dy * z0, axis=0, keepdims=True)

    cur = lambda j: pl.BlockSpec((tb, w), lambda i: (i, bcol + j))
    prev = lambda j: pl.BlockSpec((SUBLANES, w), lambda i: (jnp.maximum(i * hb - 1, 0), bcol + j))
    nxt = lambda col: pl.BlockSpec((SUBLANES, w), lambda i: (jnp.minimum((i + 1) * hb, nb * hb - 1), col))
    out = pl.BlockSpec((tb, w), lambda i: (i, 0))
    return pl.pallas_call(
        body, name=name, grid=(nb,),
        in_specs=[cur(0), cur(1), cur(2), prev(1), prev(2), nxt(bcol),
                  pl.BlockSpec((tb, w), lambda i: (i, 1)), nxt(1), pl.BlockSpec(cw.shape, lambda i: (0, 0))],
        out_specs=[out, out, out, pl.BlockSpec(cw.shape, lambda i: (0, 0))],
        out_shape=[jax.ShapeDtypeStruct((t, w), F32)] * 3 + [jax.ShapeDtypeStruct(cw.shape, F32)],
        compiler_params=_params("arbitrary"),
    )(proj, proj, proj, proj, proj, proj, dmix, dmix, cw)


SQRT_HALF = 0.7071067811865476
INV_SQRT_2PI = 0.3989422804014327


def _gelu(x):
    return 0.5 * x * (1.0 + lax.erf(x * SQRT_HALF))


def _gelu_grad(x):
    return 0.5 * (1.0 + lax.erf(x * SQRT_HALF)) + x * (INV_SQRT_2PI * jnp.exp(-0.5 * x * x))


def _sgu_fwd(uv, ln_g, ln_b, wm, bs_full, name):
    t, d2 = uv.shape
    d = d2 // 2
    ng, pb, _ = wm.shape
    gd = d // ng
    tb = _pick(t, ROW_TILES[1:] or ROW_TILES)
    assert tb % pb == 0

    def body(uv_ref, g_ref, b_ref, w_ref, bs_ref, o_ref):
        u = _gelu(uv_ref[:, :d])
        v = _gelu(uv_ref[:, d:])
        mu = jnp.mean(v, axis=-1, keepdims=True)
        vc = v - mu
        var = jnp.mean(vc * vc, axis=-1, keepdims=True)
        vn = (vc * lax.rsqrt(var + LN_EPS) * g_ref[...] + b_ref[...]).astype(BF16)
        for r in range(tb // pb):
            rows = slice(r * pb, (r + 1) * pb)
            for gi in range(ng):
                cols = slice(gi * gd, (gi + 1) * gd)
                s = jnp.dot(w_ref[gi], vn[rows, cols], preferred_element_type=F32) + bs_ref[:, cols]
                o_ref[rows, cols] = (u[rows, cols] * s).astype(BF16)

    vec = pl.BlockSpec((1, d), lambda i: (0, 0))
    return pl.pallas_call(
        body, name=name, grid=(t // tb,),
        in_specs=[pl.BlockSpec((tb, d2), lambda i: (i, 0)), vec, vec,
                  pl.BlockSpec(wm.shape, lambda i: (0, 0, 0)), pl.BlockSpec((pb, d), lambda i: (0, 0))],
        out_specs=pl.BlockSpec((tb, d), lambda i: (i, 0)),
        out_shape=jax.ShapeDtypeStruct((t, d), BF16), compiler_params=_params("parallel"),
    )(uv, ln_g.reshape(1, d), ln_b.reshape(1, d), wm, bs_full)


def _sgu_bwd(uv, ln_g, ln_b, wm, bs_full, dgated, name):
    t, d2 = uv.shape
    d = d2 // 2
    ng, pb, _ = wm.shape
    gd = d // ng
    tb = _pick(t, ROW_TILES[1:] or ROW_TILES)
    nb = t // tb

    def body(uv_ref, g_ref, b_ref, w_ref, bs_ref, dg_ref, o_ref, dw_ref, dbs_ref, dlg_ref, dlb_ref,
             du_s, dvn_s, dbs_s):
        i = pl.program_id(0)

        @pl.when(i == 0)
        def _():
            dw_ref[...] = jnp.zeros_like(dw_ref)
            dbs_s[...] = jnp.zeros_like(dbs_s)
            dlg_ref[...] = jnp.zeros_like(dlg_ref)
            dlb_ref[...] = jnp.zeros_like(dlb_ref)

        upre = uv_ref[:, :d]
        vpre = uv_ref[:, d:]
        u = _gelu(upre)
        v = _gelu(vpre)
        mu = jnp.mean(v, axis=-1, keepdims=True)
        vc = v - mu
        var = jnp.mean(vc * vc, axis=-1, keepdims=True)
        rstd = lax.rsqrt(var + LN_EPS)
        xhat = vc * rstd
        vn = (xhat * g_ref[...] + b_ref[...]).astype(BF16)
        dgt = dg_ref[...].astype(F32)
        for r in range(tb // pb):
            rows = slice(r * pb, (r + 1) * pb)
            for gi in range(ng):
                cols = slice(gi * gd, (gi + 1) * gd)
                vblk = vn[rows, cols]
                s = jnp.dot(w_ref[gi], vblk, preferred_element_type=F32) + bs_ref[:, cols]
                dblk = dgt[rows, cols]
                du_s[rows, cols] = dblk * s
                ds = dblk * u[rows, cols]
                dsb = ds.astype(BF16)
                dvn_s[rows, cols] = lax.dot_general(w_ref[gi], dsb, (((0,), (0,)), ((), ())),
                                                    preferred_element_type=F32)
                dw_ref[gi] += lax.dot_general(dsb, vblk, (((1,), (1,)), ((), ())), preferred_element_type=F32)
                dbs_s[:, cols] += ds
        dvn = dvn_s[...]
        dlg_ref[...] += jnp.sum(dvn * xhat, axis=0, keepdims=True)
        dlb_ref[...] += jnp.sum(dvn, axis=0, keepdims=True)
        dxh = dvn * g_ref[...]
        m1 = jnp.mean(dxh, axis=-1, keepdims=True)
        m2 = jnp.mean(dxh * xhat, axis=-1, keepdims=True)
        dv = rstd * (dxh - m1 - xhat * m2)
        o_ref[:, :d] = (du_s[...] * _gelu_grad(upre)).astype(BF16)
        o_ref[:, d:] = (dv * _gelu_grad(vpre)).astype(BF16)

        @pl.when(i == nb - 1)
        def _():
            lane = lax.broadcasted_iota(jnp.int32, (pb, LANES), 1)
            acc = jnp.zeros((pb, LANES), F32)
            for gi in range(ng):
                col = jnp.sum(dbs_s[:, gi * gd:(gi + 1) * gd], axis=1, keepdims=True)
                acc = acc + jnp.where(lane == gi, col, 0.0)
            dbs_ref[...] = acc

    vec = pl.BlockSpec((1, d), lambda i: (0, 0))
    duv, dw, dbs, dlg, dlb = pl.pallas_call(
        body, name=name, grid=(nb,),
        in_specs=[pl.BlockSpec((tb, d2), lambda i: (i, 0)), vec, vec,
                  pl.BlockSpec(wm.shape, lambda i: (0, 0, 0)), pl.BlockSpec((pb, d), lambda i: (0, 0)),
                  pl.BlockSpec((tb, d), lambda i: (i, 0))],
        out_specs=[pl.BlockSpec((tb, d2), lambda i: (i, 0)), pl.BlockSpec(wm.shape, lambda i: (0, 0, 0)),
                   pl.BlockSpec((pb, LANES), lambda i: (0, 0)), vec, vec],
        out_shape=[jax.ShapeDtypeStruct((t, d2), BF16), jax.ShapeDtypeStruct(wm.shape, F32),
                   jax.ShapeDtypeStruct((pb, LANES), F32), jax.ShapeDtypeStruct((1, d), F32),
                   jax.ShapeDtypeStruct((1, d), F32)],
        scratch_shapes=[pltpu.VMEM((tb, d), F32), pltpu.VMEM((tb, d), F32), pltpu.VMEM((pb, d), F32)],
        compiler_params=_params("arbitrary"),
    )(uv, ln_g.reshape(1, d), ln_b.reshape(1, d), wm, bs_full, dgated)
    return duv, dw, dbs, dlg[0], dlb[0]


def _adamw(w, g, m, v, name):
    shape = w.shape
    cols = shape[-1]
    rows = w.size // cols
    tr = _pick(rows, (512, 256, 352, 128, 64, 32, 16, 8))

    def body(w_ref, g_ref, m_ref, v_ref, d_ref, mo_ref, vo_ref):
        d_ref[...], mo_ref[...], vo_ref[...] = _adam_update(w_ref[...], g_ref[...], m_ref[...], v_ref[...])

    spec = pl.BlockSpec((tr, cols), lambda i: (i, 0))
    outs = pl.pallas_call(
        body, name=name, grid=(rows // tr,),
        in_specs=[spec] * 4, out_specs=[spec] * 3,
        out_shape=[jax.ShapeDtypeStruct((rows, cols), F32)] * 3,
        compiler_params=_params("parallel"),
    )(*[a.reshape(rows, cols) for a in (w, g, m, v)])
    return [o.reshape(shape) for o in outs]


ANY = pl.BlockSpec(memory_space=pl.ANY)


def _place():
    return lax.axis_index("x"), lax.axis_index("y"), lax.axis_index("c")


def _all_gather(shards, name):
    n = len(shards)

    def body(*refs):
        start, forward, finish = _gather_phases(refs[:n], refs[n:2 * n], *refs[2 * n:])
        start()
        forward()
        finish()

    return pl.pallas_call(
        body, name=name, in_specs=[ANY] * n, out_specs=[ANY] * n,
        out_shape=_gather_shapes(shards), scratch_shapes=_gather_sems(n),
    )(*shards)


def _gather_shapes(shards):
    return [jax.ShapeDtypeStruct((N_DEV,) + s.shape, s.dtype) for s in shards]


def _gather_sems(n):
    return [pltpu.SemaphoreType.DMA((7 * n,)), pltpu.SemaphoreType.DMA((7 * n,)), pltpu.SemaphoreType.DMA((n,))]


def _gather_phases(x_refs, out_refs, send_sems, recv_sems, local_sems):
    n = len(x_refs)
    x, y, c = _place()
    me, sibling = (x, y, c), (x, y, 1 - c)
    chips = [(1 - x, y), (x, 1 - y), (1 - x, 1 - y)]

    def copy(a, k, block, to, own=False):
        px, py, pc = block
        rows = out_refs[a].at[4 * px + 2 * py + pc]
        return pltpu.make_async_remote_copy(
            src_ref=x_refs[a] if own else rows, dst_ref=rows,
            send_sem=send_sems.at[7 * a + k], recv_sem=recv_sems.at[7 * a + k],
            device_id=to, device_id_type=MESH)

    def local(a):
        return pltpu.make_async_copy(x_refs[a], out_refs[a].at[4 * x + 2 * y + c], local_sems.at[a])

    def first(a):
        return [copy(a, 0, me, sibling, own=True)] + [copy(a, 1 + j, me, (*chip, c), own=True)
                                                      for j, chip in enumerate(chips)]

    def start():
        for a in range(n):
            local(a).start()
            for cp in first(a):
                cp.start()

    def forward():
        for j, chip in enumerate(chips):
            for a in range(n):
                copy(a, 1 + j, (*chip, c), me).wait_recv()
                copy(a, 4 + j, (*chip, c), sibling).start()

    def finish():
        for a in range(n):
            copy(a, 0, sibling, me).wait_recv()
            for j, chip in enumerate(chips):
                copy(a, 4 + j, (*chip, 1 - c), me).wait_recv()
        for a in range(n):
            for cp in first(a) + [copy(a, 4 + j, (*chip, c), sibling) for j, chip in enumerate(chips)]:
                cp.wait_send()
            local(a).wait()

    return start, forward, finish


def _rs_sibling_exchange(packed, name):
    n = len(packed)

    def body(*refs):
        p_refs, r_refs = refs[:n], refs[n:2 * n]
        send_sems, recv_sems = refs[2 * n:]
        x, y, c = _place()
        cps = []
        for a in range(n):
            for j in range(4):
                cps.append(pltpu.make_async_remote_copy(
                    src_ref=p_refs[a].at[2 * j + (1 - c)], dst_ref=r_refs[a].at[j],
                    send_sem=send_sems.at[4 * a + j], recv_sem=recv_sems.at[4 * a + j],
                    device_id=(x, y, 1 - c), device_id_type=MESH))
        for cp in cps:
            cp.start()
        for cp in cps:
            cp.wait()

    return pl.pallas_call(
        body, name=name, in_specs=[ANY] * n, out_specs=[ANY] * n,
        out_shape=[jax.ShapeDtypeStruct((4,) + p.shape[1:], p.dtype) for p in packed],
        scratch_shapes=[pltpu.SemaphoreType.DMA((4 * n,)), pltpu.SemaphoreType.DMA((4 * n,))],
    )(*packed)


def _rs_chip_sum(packed, from_sibling, c_idx, name):
    _, r, cc = packed.shape
    tr = _pick(r, (512, 256, 352, 128))

    def body(c_ref, a_ref, b_ref, o_ref):
        o_ref[...] = (a_ref[...].astype(F32) + b_ref[...].astype(F32)).astype(o_ref.dtype)

    return pl.pallas_call(
        body, name=name,
        grid_spec=pltpu.PrefetchScalarGridSpec(
            num_scalar_prefetch=1, grid=(4, r // tr),
            in_specs=[pl.BlockSpec((None, tr, cc), lambda j, i, c_ref: (2 * j + c_ref[0], i, 0)),
                      pl.BlockSpec((None, tr, cc), lambda j, i, c_ref: (j, i, 0))],
            out_specs=pl.BlockSpec((None, tr, cc), lambda j, i, c_ref: (j, i, 0))),
        out_shape=jax.ShapeDtypeStruct((4, r, cc), packed.dtype),
        compiler_params=_params("parallel", "parallel"),
    )(c_idx, packed, from_sibling)


def _rs_chip_exchange(partial, name):
    n = len(partial)

    def body(*refs):
        start, finish = _chip_exchange_phases(refs[:n], refs[n:2 * n], *refs[2 * n:])
        start()
        finish()

    return pl.pallas_call(
        body, name=name, in_specs=[ANY] * n, out_specs=[ANY] * n,
        out_shape=_chip_exchange_shapes(partial), scratch_shapes=_chip_exchange_sems(n),
    )(*partial)


def _chip_exchange_shapes(partial):
    return [jax.ShapeDtypeStruct((3,) + p.shape[1:], p.dtype) for p in partial]


def _chip_exchange_sems(n):
    return [pltpu.SemaphoreType.DMA((3 * n,)), pltpu.SemaphoreType.DMA((3 * n,))]


def _chip_exchange_phases(p_refs, r_refs, send_sems, recv_sems):
    x, y, c = _place()
    chips = [(1 - x, y), (x, 1 - y), (1 - x, 1 - y)]

    def copies():
        return [pltpu.make_async_remote_copy(
            src_ref=p_refs[a].at[2 * tx + ty], dst_ref=r_refs[a].at[k],
            send_sem=send_sems.at[3 * a + k], recv_sem=recv_sems.at[3 * a + k],
            device_id=(tx, ty, c), device_id_type=MESH)
            for a in range(len(p_refs)) for k, (tx, ty) in enumerate(chips)]

    def start():
        for cp in copies():
            cp.start()

    def finish():
        for cp in copies():
            cp.wait()

    return start, finish


def _adam_update(w, g, m, v):
    mn = ADAM_B1 * m + (1.0 - ADAM_B1) * g
    vn = ADAM_B2 * v + (1.0 - ADAM_B2) * (g * g)
    m_hat = mn / (1.0 - ADAM_B1 ** ADAM_STEP)
    v_hat = vn / (1.0 - ADAM_B2 ** ADAM_STEP)
    return -ADAM_LR * (m_hat / (jnp.sqrt(v_hat) + ADAM_EPS) + ADAM_WD * w), mn, vn


def _rs_final_adamw(partial, received, chip_idx, w, m, v, name):
    _, r, cc = partial.shape
    tr = _pick(r, (512, 256, 352, 128))

    def body(c_ref, a_ref, r_ref, w_ref, m_ref, v_ref, g_ref, d_ref, mo_ref, vo_ref):
        g = a_ref[...].astype(F32)
        for k in range(3):
            g = g + r_ref[k].astype(F32)
        g_ref[...] = g
        d_ref[...], mo_ref[...], vo_ref[...] = _adam_update(w_ref[...], g, m_ref[...], v_ref[...])

    row = pl.BlockSpec((tr, cc), lambda i, c_ref: (i, 0))
    return pl.pallas_call(
        body, name=name,
        grid_spec=pltpu.PrefetchScalarGridSpec(
            num_scalar_prefetch=1, grid=(r // tr,),
            in_specs=[pl.BlockSpec((None, tr, cc), lambda i, c_ref: (c_ref[0], i, 0)),
                      pl.BlockSpec((3, tr, cc), lambda i, c_ref: (0, i, 0)), row, row, row],
            out_specs=[row] * 4),
        out_shape=[jax.ShapeDtypeStruct((r, cc), F32)] * 4,
        compiler_params=_params("parallel"),
    )(chip_idx, partial, received, w.reshape(r, cc), m.reshape(r, cc), v.reshape(r, cc))


def _all_reduce_small(vals, name):
    r, cc = vals.shape

    def body(v_ref, o_ref, buf, send_sems, recv_sems):
        x, y, c = _place()
        me = 4 * x + 2 * y + c
        buf[0] = v_ref[...]
        cps = []
        for k in range(1, N_DEV):
            kx, ky, kc = (k >> 2) & 1, (k >> 1) & 1, k & 1
            peer = (1 - x if kx else x, 1 - y if ky else y, 1 - c if kc else c)
            cps.append(pltpu.make_async_remote_copy(
                src_ref=buf.at[0], dst_ref=buf.at[k], send_sem=send_sems.at[k - 1],
                recv_sem=recv_sems.at[k - 1], device_id=peer, device_id_type=MESH))
        for cp in cps:
            cp.start()
        for cp in cps:
            cp.wait()
        acc = buf[jnp.bitwise_xor(me, 0)]
        for dev in range(1, N_DEV):
            acc = acc + buf[jnp.bitwise_xor(me, dev)]
        o_ref[...] = acc

    vm = pl.BlockSpec(memory_space=pltpu.VMEM)
    return pl.pallas_call(
        body, name=name, in_specs=[vm], out_specs=vm,
        out_shape=jax.ShapeDtypeStruct((r, cc), F32),
        scratch_shapes=[pltpu.VMEM((N_DEV, r, cc), F32), pltpu.SemaphoreType.DMA((7,)),
                        pltpu.SemaphoreType.DMA((7,))],
        compiler_params=pltpu.CompilerParams(vmem_limit_bytes=VMEM_LIMIT),
    )(vals)


def _lanes(flat):
    pad = (-flat.shape[0]) % (SUBLANES * LANES)
    return jnp.pad(flat, (0, pad)).reshape(-1, LANES)


def kernel(x, even_w_in, even_b_f, even_conv_w, even_w_out, odd_w_in, odd_v_ln_g, odd_v_ln_b, odd_w_s, odd_b_s, odd_w_out, mix_ln_g, mix_ln_b, ffn_w_in, ffn_w_out, ffn_ln_g, ffn_ln_b, loss_target, m_even_w_in, m_even_b_f, m_even_conv_w, m_even_w_out, m_odd_w_in, m_odd_v_ln_g, m_odd_v_ln_b, m_odd_w_s, m_odd_b_s, m_odd_w_out, m_mix_ln_g, m_mix_ln_b, m_ffn_w_in, m_ffn_w_out, m_ffn_ln_g, m_ffn_ln_b, v_even_w_in, v_even_b_f, v_even_conv_w, v_even_w_out, v_odd_w_in, v_odd_v_ln_g, v_odd_v_ln_b, v_odd_w_s, v_odd_b_s, v_odd_w_out, v_mix_ln_g, v_mix_ln_b, v_ffn_w_in, v_ffn_w_out, v_ffn_ln_g, v_ffn_ln_b):
    t, d = x.shape[1], x.shape[2]
    nh = even_b_f.shape[-1]
    w = even_conv_w.shape[-1] * N_DEV
    dh = w // nh
    scale = dh ** -0.5
    e_in = even_w_in.shape[-1] * N_DEV
    f2 = ffn_w_in.shape[-1] * N_DEV
    f = f2 // 2
    ng, pb = odd_w_s.shape[1], odd_w_s.shape[2]
    assert e_in == 6 * w + nh and nh <= SUBLANES and (6 * w) % LANES == 0 and d % N_DEV == 0
    mx, my, mc = _place()
    me = 4 * mx + 2 * my + mc

    big = [even_w_in[0], even_w_out[0], odd_w_in[0], odd_w_out[0],
           ffn_w_in[0], ffn_w_in[1], ffn_w_out[0], ffn_w_out[1]]
    g_in0, = _all_gather([big[0].astype(BF16)], "ag_even_w_in")
    w_in0 = g_in0.transpose(1, 0, 2).reshape(d, e_in)
    w_all0 = jnp.concatenate([w_in0[:, :3 * w], w_in0[:, 3 * w + nh:], w_in0[:, 3 * w:3 * w + nh],
                              jnp.zeros((d, LANES - nh), BF16)], axis=1)

    cs, vs = even_conv_w.shape[-1], odd_v_ln_g.shape[-1]
    small_mine = jnp.concatenate([
        lax.dynamic_update_slice(jnp.zeros((3, w), F32), even_conv_w[0], (0, me * cs)).reshape(-1),
        lax.dynamic_update_slice(jnp.zeros((d,), F32), odd_v_ln_g[0], (me * vs,)),
        lax.dynamic_update_slice(jnp.zeros((d,), F32), odd_v_ln_b[0], (me * vs,))])
    small_all = _all_reduce_small(_lanes(small_mine), "ag_small").reshape(-1)
    conv_w = small_all[:3 * w].reshape(3, w)
    vln_g = small_all[3 * w:3 * w + d]
    vln_b = small_all[3 * w + d:3 * w + 2 * d]

    bf_pad = jnp.pad(even_b_f[0], (0, LANES - nh)).reshape(1, LANES)
    chunk = jnp.arange(pb) // (pb // 2)
    ws_mask = (chunk[None, :] <= chunk[:, None])[None]
    wm = jnp.where(ws_mask, odd_w_s[0], 0.0).astype(BF16)
    bs_full = jnp.repeat(odd_b_s[0].T, d // ng, axis=1)

    x0 = x[0]
    tgt = loss_target[0]
    fcol = 6 * w // LANES
    p0 = _mm(x0, w_all0, "nn", F32, "l0_in_proj")
    cgate = _fgate_fwd(p0, bf_pad, fcol, nh, "l0_fgate")
    assert dh + 7 <= LANES
    qa, ka, va = _attn_pack(p0, cgate, w, nh, scale, "l0_attn_pack")
    oa, g_out0, g_in1, g_out1, g_fi0, g_fi1, g_fo0, g_fo1 = _attn_fwd(
        qa, ka, va, dh, "l0_attn", gather=[s.astype(BF16) for s in big[1:]])
    w_out0, w_out1 = g_out0.reshape(2 * w, d), g_out1.reshape(d, d)
    w_fo0, w_fo1 = g_fo0.reshape(f, d), g_fo1.reshape(f, d)
    nb = N_DEV // 2
    w_fi0, w_fi1 = g_fi0.reshape(2, nb, d, -1), g_fi1.reshape(2, nb, d, -1)
    attn, = _attn_unpack(oa, w, nh, 1.0, None, 1.0, "l0_attn_unpack")
    yconv = _conv_fwd(p0, conv_w, w, 3, "l0_conv")
    mix = jnp.concatenate([attn, yconv], axis=1)
    m0 = _mm(mix, w_out0, "nn", F32, "l0_out_proj")
    x1, x1b = _ln_fwd(x0, m0, mix_ln_g[0], mix_ln_b[0], "l0_mix_ln")
    h0, gu0 = _ffn_in_swiglu(x1b, w_fi0, "l0_ffn_in")
    f0 = _mm_blk_fwd(h0, w_fo0, F32, "l0_ffn_out")
    x2, x2b = _ln_fwd(x1, f0, ffn_ln_g[0], ffn_ln_b[0], "l0_ffn_ln")

    uv = _mm_cols_fwd(x2b, g_in1, False, F32, "l1_in_proj")
    gated = _sgu_fwd(uv, vln_g, vln_b, wm, bs_full, "l1_sgu")
    m1 = _mm(gated, w_out1, "nn", F32, "l1_out_proj")
    x3, x3b = _ln_fwd(x2, m1, mix_ln_g[1], mix_ln_b[1], "l1_mix_ln")
    h1, gu1 = _ffn_in_swiglu(x3b, w_fi1, "l1_ffn_in")
    f1 = _mm_blk_fwd(h1, w_fo1, F32, "l1_ffn_out")
    x4, _ = _ln_fwd(x3, f1, ffn_ln_g[1], ffn_ln_b[1], "l1_ffn_ln")
    loss_part, dy4 = _loss(x4, tgt, "loss")

    dz4, g_ffn_g1, g_ffn_b1 = _ln_bwd(x3, f1, ffn_ln_g[1], dy4, 1.0, None, "l1_ffn_ln_bwd")
    gd_fo1 = _mm_blk_dw(h1, dz4, BF16, "l1_ffn_out_dw").reshape(N_DEV, -1, d)
    dgu1 = _ffn_out_dx_swiglu(dz4, w_fo1, gu1, "l1_ffn_out_dx").reshape(N_DEV, t, -1)
    gd_fi1 = _mm_cols_dw(x3b, dgu1, N_DEV, True, BF16, "l1_ffn_in_dw")
    dx3 = _mm_cols_dx(dgu1, g_fi1, True, F32, "l1_ffn_in_dx")
    dz3, g_mix_g1, g_mix_b1 = _ln_bwd(x2, m1, mix_ln_g[1], dz4, ALPHA, dx3, "l1_mix_ln_bwd")
    gd_out1 = _mm(gated, dz3, "tn", BF16, "l1_out_proj_dw").reshape(N_DEV, -1, d)
    dgated = _mm(dz3, w_out1, "nt", BF16, "l1_out_proj_dx")
    duv, g_wm, g_bs_t, g_vln_g, g_vln_b = _sgu_bwd(uv, vln_g, vln_b, wm, bs_full, dgated, "l1_sgu_bwd")
    gd_in1 = _mm_cols_dw(x2b, duv, N_DEV, False, BF16, "l1_in_proj_dw")
    dx2 = _mm_cols_dx(duv, g_in1, False, F32, "l1_in_proj_dx")

    dz2, g_ffn_g0, g_ffn_b0 = _ln_bwd(x1, f0, ffn_ln_g[0], dz3, ALPHA, dx2, "l0_ffn_ln_bwd")
    gd_fo0 = _mm_blk_dw(h0, dz2, BF16, "l0_ffn_out_dw").reshape(N_DEV, -1, d)
    dgu0 = _ffn_out_dx_swiglu(dz2, w_fo0, gu0, "l0_ffn_out_dx").reshape(N_DEV, t, -1)
    gd_fi0 = _mm_cols_dw(x1b, dgu0, N_DEV, True, BF16, "l0_ffn_in_dw")
    dx1 = _mm_cols_dx(dgu0, g_fi0, True, F32, "l0_ffn_in_dx")
    dz1, g_mix_g0, g_mix_b0 = _ln_bwd(x0, m0, mix_ln_g[0], dz2, ALPHA, dx1, "l0_mix_ln_bwd")
    gd_out0 = _mm(mix, dz1, "tn", BF16, "l0_out_proj_dw").reshape(N_DEV, -1, d)
    dmix = _mm(dz1, w_out0, "nt", F32, "l0_out_proj_dx")
    d_b, d_c, d_h, g_conv = _conv_bwd(p0, conv_w, dmix, w, 3, "l0_conv_bwd")
    doa, qa2 = _attn_pack_bwd(dmix, oa, qa, w, nh, "l0_attn_pack_bwd")
    big_names = ["even_w_in", "even_w_out", "odd_w_in", "odd_w_out", "ffn_w_in0", "ffn_w_in1", "ffn_w_out0", "ffn_w_out1"]
    c_idx = mc.reshape(1).astype(jnp.int32)
    chip_idx = (2 * mx + my).reshape(1).astype(jnp.int32)
    early_g = [gd_out0, gd_in1, gd_out1, gd_fi0, gd_fi1, gd_fo0, gd_fo1]
    early_sib = _rs_sibling_exchange(early_g, "rs_sibling_early")
    early_partial = [_rs_chip_sum(g, s, c_idx, "rs_chip_sum_" + n)
                     for g, s, n in zip(early_g, early_sib, big_names[1:])]
    dqa, dka, dva, *early_received = _attn_bwd(qa2, ka, va, doa, "l0_attn_bwd", exchange=early_partial)
    dq, dcq = _attn_unpack(dqa, w, nh, scale, dh + 3, 1.0, "l0_attn_unpack_dq")
    dk, dck = _attn_unpack(dka, w, nh, 1.0, dh, -1.0, "l0_attn_unpack_dk")
    dv, = _attn_unpack(dva, w, nh, 1.0, None, 1.0, "l0_attn_unpack_dv")
    dzf, g_bf = _fgate_bwd(p0, bf_pad, dcq, dck, fcol, nh, "l0_fgate_bwd")
    dp0 = jnp.concatenate([dq, dk, dv, d_b.astype(BF16), d_c.astype(BF16), d_h.astype(BF16), dzf.astype(BF16)], axis=1)
    g_all0 = _mm(x0, dp0, "tn", F32, "l0_in_proj_dw")
    dx0 = _mm(dp0, w_all0, "nt", F32, "l0_in_proj_dx")
    grad_x = _axpy(ALPHA, dz1, dx0, "grad_x")
    gd_in0 = jnp.concatenate([g_all0[:, :3 * w], g_all0[:, 6 * w:6 * w + nh], g_all0[:, 3 * w:6 * w]], axis=1)
    gd_in0 = gd_in0.reshape(d, N_DEV, -1).transpose(1, 0, 2).astype(BF16)

    big_m = [m_even_w_in[0], m_even_w_out[0], m_odd_w_in[0], m_odd_w_out[0],
             m_ffn_w_in[0], m_ffn_w_in[1], m_ffn_w_out[0], m_ffn_w_out[1]]
    big_v = [v_even_w_in[0], v_even_w_out[0], v_odd_w_in[0], v_odd_w_out[0],
             v_ffn_w_in[0], v_ffn_w_in[1], v_ffn_w_out[0], v_ffn_w_out[1]]
    late_sib = _rs_sibling_exchange([gd_in0], "rs_sibling_late")
    late_partial = [_rs_chip_sum(gd_in0, late_sib[0], c_idx, "rs_chip_sum_" + big_names[0])]
    partial = late_partial + early_partial
    received = list(_rs_chip_exchange(late_partial, "rs_chips_late")) + list(early_received)
    upd = [_rs_final_adamw(p, r, chip_idx, wt, mt, vt, "rs_final_adamw_" + n)
           for p, r, wt, mt, vt, n in zip(partial, received, big, big_m, big_v, big_names)]
    big_out = {}
    for i, n in enumerate(["even_w_in", "even_w_out", "odd_w_in", "odd_w_out"]):
        big_out[n] = [o[None] for o in upd[i]]
    big_out["ffn_w_in"] = [jnp.stack([a, b]) for a, b in zip(upd[4], upd[5])]
    big_out["ffn_w_out"] = [jnp.stack([a, b]) for a, b in zip(upd[6], upd[7])]

    g_ws = jnp.where(ws_mask, g_wm, 0.0)
    g_bs = g_bs_t[:, :ng].T
    small_g = [g_bf[:nh], g_conv, g_vln_g, g_vln_b, g_ws, g_bs,
               jnp.stack([g_mix_g0, g_mix_g1]), jnp.stack([g_mix_b0, g_mix_b1]),
               jnp.stack([g_ffn_g0, g_ffn_g1]), jnp.stack([g_ffn_b0, g_ffn_b1])]
    small_sum = _all_reduce_small(_lanes(jnp.concatenate([a.reshape(-1) for a in small_g])), "ar_small_grads")
    small_sum = small_sum.reshape(-1)
    outs_small = []
    off = 0
    for a in small_g:
        outs_small.append(small_sum[off:off + a.size].reshape(a.shape))
        off += a.size
    gr_bf, gr_conv, gr_vg, gr_vb, gr_ws, gr_bs, gr_mg, gr_mb, gr_fg, gr_fb = outs_small

    loss = lax.psum(loss_part, ("x", "y", "c"))

    grads = {
        "even_b_f": gr_bf[None],
        "even_conv_w": lax.dynamic_slice(gr_conv, (0, me * cs), (3, cs))[None],
        "odd_v_ln_g": lax.dynamic_slice(gr_vg, (me * vs,), (vs,))[None],
        "odd_v_ln_b": lax.dynamic_slice(gr_vb, (me * vs,), (vs,))[None],
        "odd_w_s": gr_ws[None], "odd_b_s": gr_bs[None],
        "mix_ln_g": gr_mg, "mix_ln_b": gr_mb, "ffn_ln_g": gr_fg, "ffn_ln_b": gr_fb,
    }
    weights = dict(even_w_in=even_w_in, even_b_f=even_b_f, even_conv_w=even_conv_w, even_w_out=even_w_out,
                   odd_w_in=odd_w_in, odd_v_ln_g=odd_v_ln_g, odd_v_ln_b=odd_v_ln_b, odd_w_s=odd_w_s,
                   odd_b_s=odd_b_s, odd_w_out=odd_w_out, mix_ln_g=mix_ln_g, mix_ln_b=mix_ln_b,
                   ffn_w_in=ffn_w_in, ffn_w_out=ffn_w_out, ffn_ln_g=ffn_ln_g, ffn_ln_b=ffn_ln_b)
    moms = dict(even_w_in=(m_even_w_in, v_even_w_in), even_b_f=(m_even_b_f, v_even_b_f),
                even_conv_w=(m_even_conv_w, v_even_conv_w), even_w_out=(m_even_w_out, v_even_w_out),
                odd_w_in=(m_odd_w_in, v_odd_w_in), odd_v_ln_g=(m_odd_v_ln_g, v_odd_v_ln_g),
                odd_v_ln_b=(m_odd_v_ln_b, v_odd_v_ln_b), odd_w_s=(m_odd_w_s, v_odd_w_s),
                odd_b_s=(m_odd_b_s, v_odd_b_s), odd_w_out=(m_odd_w_out, v_odd_w_out),
                mix_ln_g=(m_mix_ln_g, v_mix_ln_g), mix_ln_b=(m_mix_ln_b, v_mix_ln_b),
                ffn_w_in=(m_ffn_w_in, v_ffn_w_in), ffn_w_out=(m_ffn_w_out, v_ffn_w_out),
                ffn_ln_g=(m_ffn_ln_g, v_ffn_ln_g), ffn_ln_b=(m_ffn_ln_b, v_ffn_ln_b))
    names = list(weights)
    gout, deltas, new_m, new_v = [], [], [], []
    for n in names:
        if n in big_out:
            gr, dlt, mn, vn = big_out[n]
        else:
            gr = grads[n]
            dlt, mn, vn = _adamw(weights[n], gr, moms[n][0], moms[n][1], "adamw_" + n)
        gout.append(gr.reshape(weights[n].shape))
        deltas.append(dlt.reshape(weights[n].shape))
        new_m.append(mn.reshape(weights[n].shape))
        new_v.append(vn.reshape(weights[n].shape))
    return (loss, grad_x[None], *gout, *deltas, *new_m, *new_v)
```

```python
import jax
import jax.numpy as jnp
from jax import lax
from jax.experimental import pallas as pl
from jax.experimental.pallas import tpu as pltpu

F32 = jnp.float32
BF16 = jnp.bfloat16
MESH = pl.DeviceIdType.MESH

DEPTH = 2
ALPHA = (2.0 * DEPTH) ** 0.25
LN_EPS = 1e-5
ADAM_LR = 0.001
ADAM_B1 = 0.9
ADAM_B2 = 0.999
ADAM_EPS = 1e-08
ADAM_WD = 0.01
ADAM_STEP = 10

N_DEV = 8
LANES = 128
SUBLANES = 8
VMEM_LIMIT = 48 * 1024 * 1024
NEG_BIG = -1e30
ROW_TILES = (512, 256, 128)


def _pick(n, cands):
    for c in cands:
        if c <= n and n % c == 0:
            return c
    return n


def _params(*sem):
    return pltpu.CompilerParams(dimension_semantics=sem, vmem_limit_bytes=VMEM_LIMIT)


NN = (((1,), (0,)), ((), ()))
NT = (((1,), (1,)), ((), ()))
TN = (((0,), (0,)), ((), ()))
M_TILES = (1024, 512, 1408, 256, 128)
N_TILES = (512, 640, 256, 128)
K_TILES = (2048, 1024, 512, 640, 1408, 256, 128)
K_WHOLE = 3328


def _mm_core(name, grid, a, b, a_spec, b_spec, o_spec, o_shape, o_dtype, dims, tile, pieces=None):
    nred = grid[2]
    pieces = pieces or [(lambda r: r[...], lambda r: r[...])]

    def body(a_ref, b_ref, o_ref, *acc):
        part = None
        for fa, fb in pieces:
            prod = lax.dot_general(fa(a_ref).astype(BF16), fb(b_ref).astype(BF16), dims, preferred_element_type=F32)
            part = prod if part is None else part + prod
        if nred == 1:
            o_ref[...] = part.astype(o_ref.dtype)
            return
        acc_ref, = acc
        kk = pl.program_id(2)

        @pl.when(kk == 0)
        def _():
            acc_ref[...] = jnp.zeros_like(acc_ref)

        acc_ref[...] += part

        @pl.when(kk == nred - 1)
        def _():
            o_ref[...] = acc_ref[...].astype(o_ref.dtype)

    return pl.pallas_call(
        body, name=name, grid=grid, in_specs=[a_spec, b_spec], out_specs=o_spec,
        out_shape=jax.ShapeDtypeStruct(o_shape, o_dtype),
        scratch_shapes=[] if nred == 1 else [pltpu.VMEM(tile, F32)],
        compiler_params=_params("parallel", "parallel", "arbitrary"),
    )(a, b)


def _mm(a, b, mode, out_dtype, name):
    if mode == "nn":
        (m, k), (k2, n) = a.shape, b.shape
    elif mode == "nt":
        (m, k), (n, k2) = a.shape, b.shape
    else:
        (k, m), (k2, n) = a.shape, b.shape
    assert k == k2, (a.shape, b.shape, mode)
    tm, tn = _pick(m, M_TILES), _pick(n, N_TILES)
    tk = k if k <= K_WHOLE else _pick(k, K_TILES)
    if mode == "nn":
        a_spec = pl.BlockSpec((tm, tk), lambda i, j, kk: (i, kk))
        b_spec = pl.BlockSpec((tk, tn), lambda i, j, kk: (kk, j))
        dims = NN
    elif mode == "nt":
        a_spec = pl.BlockSpec((tm, tk), lambda i, j, kk: (i, kk))
        b_spec = pl.BlockSpec((tn, tk), lambda i, j, kk: (j, kk))
        dims = NT
    else:
        a_spec = pl.BlockSpec((tk, tm), lambda i, j, kk: (kk, i))
        b_spec = pl.BlockSpec((tk, tn), lambda i, j, kk: (kk, j))
        dims = TN
    return _mm_core(name, (m // tm, n // tn, k // tk), a, b, a_spec, b_spec,
                    pl.BlockSpec((tm, tn), lambda i, j, kk: (i, j)), (m, n), out_dtype, dims, (tm, tn))


def _act_spec(blocked, rows, ns, row_ax, d_ax):
    if blocked:
        return pl.BlockSpec((None, rows, ns), lambda *g: (g[d_ax], g[row_ax], 0))
    return pl.BlockSpec((rows, ns), lambda *g: (g[row_ax], g[d_ax]))


def _mm_cols_fwd(a, g3, blocked, out_dtype, name):
    (t, k), (nd, k2, ns) = a.shape, g3.shape
    assert k == k2
    tm, tk = _pick(t, M_TILES), _pick(k, K_TILES)
    return _mm_core(name, (t // tm, nd, k // tk), a, g3,
                    pl.BlockSpec((tm, tk), lambda i, d, kk: (i, kk)),
                    pl.BlockSpec((None, tk, ns), lambda i, d, kk: (d, kk, 0)),
                    _act_spec(blocked, tm, ns, 0, 1), (nd, t, ns) if blocked else (t, nd * ns), out_dtype, NN, (tm, ns))


def _mm_cols_dx(dy, g3, blocked, out_dtype, name):
    nd, k, ns = g3.shape
    t = dy.shape[1] if blocked else dy.shape[0]
    tm, tn = _pick(t, M_TILES), _pick(k, (1024,) + N_TILES)
    o_spec = pl.BlockSpec((tm, tn), lambda i, j, d: (i, j))
    if not blocked:
        whole_b = lambda r: jnp.concatenate([r[s] for s in range(nd)], axis=1)
        return _mm_core(name, (t // tm, k // tn, 1), dy, g3,
                        pl.BlockSpec((tm, nd * ns), lambda i, j, d: (i, 0)),
                        pl.BlockSpec((nd, tn, ns), lambda i, j, d: (0, j, 0)),
                        o_spec, (t, k), out_dtype, NT, (tm, tn), pieces=[(lambda r: r[...], whole_b)])
    grp = 2 if nd % 2 == 0 else 1
    pieces = [(lambda r, s=s: r[s], lambda r, s=s: r[s]) for s in range(grp)]
    return _mm_core(name, (t // tm, k // tn, nd // grp), dy, g3,
                    pl.BlockSpec((grp, tm, ns), lambda i, j, d: (d, i, 0)),
                    pl.BlockSpec((grp, tn, ns), lambda i, j, d: (d, j, 0)),
                    o_spec, (t, k), out_dtype, NT, (tm, tn), pieces=pieces)


def _mm_cols_dw(a, dy, nd, blocked, out_dtype, name):
    t, k = a.shape
    ns = dy.shape[2] if blocked else dy.shape[1] // nd
    tmk, tk = _pick(k, M_TILES), _pick(t, K_TILES)
    return _mm_core(name, (nd, k // tmk, t // tk), a, dy,
                    pl.BlockSpec((tk, tmk), lambda d, j, kk: (kk, j)),
                    _act_spec(blocked, tk, ns, 2, 0),
                    pl.BlockSpec((None, tmk, ns), lambda d, j, kk: (d, j, 0)), (nd, k, ns), out_dtype, TN, (tmk, ns))


def _mm_blk_fwd(h3, w, out_dtype, name):
    (nb, t, ns), (_, n) = h3.shape, w.shape
    tm, tn = _pick(t, M_TILES), _pick(n, (1024,) + N_TILES)
    pieces = [(lambda r, s=s: r[s], lambda r, s=s: r[s * ns:(s + 1) * ns, :]) for s in range(nb)]
    return _mm_core(name, (t // tm, n // tn, 1), h3, w,
                    pl.BlockSpec((nb, tm, ns), lambda i, j, d: (0, i, 0)),
                    pl.BlockSpec((nb * ns, tn), lambda i, j, d: (0, j)),
                    pl.BlockSpec((tm, tn), lambda i, j, d: (i, j)), (t, n), out_dtype, NN, (tm, tn), pieces=pieces)


def _mm_blk_dw(h3, dz, out_dtype, name):
    (nb, t, ns), (_, n) = h3.shape, dz.shape
    tn, tk = _pick(n, (1024,) + N_TILES), _pick(t, K_TILES)
    return _mm_core(name, (nb, n // tn, t // tk), h3, dz,
                    pl.BlockSpec((None, tk, ns), lambda d, j, kk: (d, kk, 0)),
                    pl.BlockSpec((tk, tn), lambda d, j, kk: (kk, j)),
                    pl.BlockSpec((ns, tn), lambda d, j, kk: (d, j)), (nb * ns, n), out_dtype, TN, (ns, tn))


def _ln_fwd(xa, xb, g, b, name):
    t, d = xa.shape
    tb = _pick(t, ROW_TILES)

    def body(xa_ref, xb_ref, g_ref, b_ref, y_ref, yb_ref):
        z = ALPHA * xa_ref[...] + xb_ref[...]
        mu = jnp.mean(z, axis=-1, keepdims=True)
        zc = z - mu
        var = jnp.mean(zc * zc, axis=-1, keepdims=True)
        y = zc * lax.rsqrt(var + LN_EPS) * g_ref[...] + b_ref[...]
        y_ref[...] = y
        yb_ref[...] = y.astype(BF16)

    row = pl.BlockSpec((tb, d), lambda i: (i, 0))
    vec = pl.BlockSpec((1, d), lambda i: (0, 0))
    return pl.pallas_call(
        body, name=name, grid=(t // tb,),
        in_specs=[row, row, vec, vec], out_specs=[row, row],
        out_shape=[jax.ShapeDtypeStruct((t, d), F32), jax.ShapeDtypeStruct((t, d), BF16)],
        compiler_params=_params("parallel"),
    )(xa, xb, g.reshape(1, d), b.reshape(1, d))


def _ln_bwd(xa, xb, g, dya, ca, dyb, name):
    t, d = xa.shape
    tb = _pick(t, ROW_TILES)
    two = dyb is not None

    def body(*refs):
        if two:
            xa_ref, xb_ref, g_ref, dya_ref, dyb_ref, dz_ref, dzb_ref, dg_ref, db_ref = refs
            dy = ca * dya_ref[...] + dyb_ref[...]
        else:
            xa_ref, xb_ref, g_ref, dya_ref, dz_ref, dzb_ref, dg_ref, db_ref = refs
            dy = ca * dya_ref[...]
        z = ALPHA * xa_ref[...] + xb_ref[...]
        mu = jnp.mean(z, axis=-1, keepdims=True)
        zc = z - mu
        var = jnp.mean(zc * zc, axis=-1, keepdims=True)
        rstd = lax.rsqrt(var + LN_EPS)
        xhat = zc * rstd
        dxh = dy * g_ref[...]
        m1 = jnp.mean(dxh, axis=-1, keepdims=True)
        m2 = jnp.mean(dxh * xhat, axis=-1, keepdims=True)
        dz = rstd * (dxh - m1 - xhat * m2)
        dz_ref[...] = dz
        dzb_ref[...] = dz.astype(BF16)

        @pl.when(pl.program_id(0) == 0)
        def _():
            dg_ref[...] = jnp.zeros_like(dg_ref)
            db_ref[...] = jnp.zeros_like(db_ref)

        dg_ref[...] += jnp.sum(dy * xhat, axis=0, keepdims=True)
        db_ref[...] += jnp.sum(dy, axis=0, keepdims=True)

    row = pl.BlockSpec((tb, d), lambda i: (i, 0))
    vec = pl.BlockSpec((1, d), lambda i: (0, 0))
    ins = [xa, xb, g.reshape(1, d), dya] + ([dyb] if two else [])
    dz, dzb, dg, db = pl.pallas_call(
        body, name=name, grid=(t // tb,),
        in_specs=[row, row, vec, row] + ([row] if two else []),
        out_specs=[row, row, vec, vec],
        out_shape=[jax.ShapeDtypeStruct((t, d), F32), jax.ShapeDtypeStruct((t, d), BF16),
                   jax.ShapeDtypeStruct((1, d), F32), jax.ShapeDtypeStruct((1, d), F32)],
        compiler_params=_params("arbitrary"),
    )(*ins)
    return dz, dzb, dg[0], db[0]


def _loss(y, target, name):
    t, d = y.shape
    tb = _pick(t, ROW_TILES)

    def body(y_ref, t_ref, dy_ref, l_ref):
        e = y_ref[...] - t_ref[...]
        dy_ref[...] = e * (1.0 / d)

        @pl.when(pl.program_id(0) == 0)
        def _():
            l_ref[...] = jnp.zeros_like(l_ref)

        l_ref[...] += 0.5 * jnp.sum(jnp.mean(e * e, axis=-1, keepdims=True))

    row = pl.BlockSpec((tb, d), lambda i: (i, 0))
    dy, l = pl.pallas_call(
        body, name=name, grid=(t // tb,),
        in_specs=[row, row], out_specs=[row, pl.BlockSpec((1, LANES), lambda i: (0, 0))],
        out_shape=[jax.ShapeDtypeStruct((t, d), F32), jax.ShapeDtypeStruct((1, LANES), F32)],
        compiler_params=_params("arbitrary"),
    )(y, target)
    return l[0, 0], dy


def _axpy(ca, a, b, name):
    t, d = a.shape
    tb = _pick(t, ROW_TILES)

    def body(a_ref, b_ref, o_ref):
        o_ref[...] = ca * a_ref[...] + b_ref[...]

    row = pl.BlockSpec((tb, d), lambda i: (i, 0))
    return pl.pallas_call(
        body, name=name, grid=(t // tb,), in_specs=[row, row], out_specs=row,
        out_shape=jax.ShapeDtypeStruct((t, d), F32), compiler_params=_params("parallel"),
    )(a, b)


def _ffn_in_swiglu(xb, g4, name):
    (t, k), (_, nb, _, ns) = xb.shape, g4.shape
    tm = _pick(t, M_TILES)

    def body(x_ref, w_ref, h_ref, gu_ref):
        xv = x_ref[...]
        gate = jnp.dot(xv, w_ref[0], preferred_element_type=F32)
        up = jnp.dot(xv, w_ref[1], preferred_element_type=F32)
        h_ref[...] = (gate * jax.nn.sigmoid(gate) * up).astype(BF16)
        gu_ref[0] = gate.astype(BF16)
        gu_ref[1] = up.astype(BF16)

    return pl.pallas_call(
        body, name=name, grid=(t // tm, nb),
        in_specs=[pl.BlockSpec((tm, k), lambda i, d: (i, 0)),
                  pl.BlockSpec((2, None, k, ns), lambda i, d: (0, d, 0, 0))],
        out_specs=[pl.BlockSpec((None, tm, ns), lambda i, d: (d, i, 0)),
                   pl.BlockSpec((2, None, tm, ns), lambda i, d: (0, d, i, 0))],
        out_shape=[jax.ShapeDtypeStruct((nb, t, ns), BF16), jax.ShapeDtypeStruct((2, nb, t, ns), BF16)],
        compiler_params=_params("parallel", "parallel"),
    )(xb, g4)


def _ffn_out_dx_swiglu(dz, w_out, gu4, name):
    (t, d), (_, nb, _, ns) = dz.shape, gu4.shape
    tm = _pick(t, M_TILES)

    def body(dz_ref, w_ref, gu_ref, o_ref):
        dh = lax.dot_general(dz_ref[...].astype(BF16), w_ref[...], NT, preferred_element_type=F32)
        gate = gu_ref[0].astype(F32)
        up = gu_ref[1].astype(F32)
        sg = jax.nn.sigmoid(gate)
        silu = gate * sg
        o_ref[0] = (dh * up * (sg + silu * (1.0 - sg))).astype(BF16)
        o_ref[1] = (dh * silu).astype(BF16)

    blk = pl.BlockSpec((2, None, tm, ns), lambda i, j: (0, j, i, 0))
    return pl.pallas_call(
        body, name=name, grid=(t // tm, nb),
        in_specs=[pl.BlockSpec((tm, d), lambda i, j: (i, 0)), pl.BlockSpec((ns, d), lambda i, j: (j, 0)), blk],
        out_specs=blk,
        out_shape=jax.ShapeDtypeStruct((2, nb, t, ns), BF16),
        compiler_params=_params("parallel", "parallel"),
    )(dz, w_out, gu4)


def _tri_matmul(tri, x):
    x1 = x.astype(BF16)
    r1 = x - x1.astype(F32)
    x2 = r1.astype(BF16)
    x3 = (r1 - x2.astype(F32)).astype(BF16)
    dot = lambda v: jnp.dot(tri, v, preferred_element_type=F32)
    return dot(x1) + dot(x2) + dot(x3)


def _fgate_fwd(proj, bf_pad, fcol, n_heads, name):
    t = proj.shape[0]
    tb = _pick(t, ROW_TILES)

    def body(p_ref, b_ref, c_ref, carry):
        @pl.when(pl.program_id(0) == 0)
        def _():
            carry[...] = jnp.zeros_like(carry)

        z = p_ref[...] + b_ref[...]
        lf = jnp.minimum(z, 0.0) - jnp.log1p(jnp.exp(-jnp.abs(z)))
        lane = lax.broadcasted_iota(jnp.int32, (tb, LANES), 1)
        lf = jnp.where(lane < n_heads, lf, 0.0)
        r = lax.broadcasted_iota(jnp.int32, (tb, tb), 0)
        s = lax.broadcasted_iota(jnp.int32, (tb, tb), 1)
        tri = (s <= r).astype(BF16)
        c = _tri_matmul(tri, lf) + carry[...]
        c_ref[...] = c
        carry[...] = c[tb - 1:tb, :]

    return pl.pallas_call(
        body, name=name, grid=(t // tb,),
        in_specs=[pl.BlockSpec((tb, LANES), lambda i: (i, fcol)), pl.BlockSpec((1, LANES), lambda i: (0, 0))],
        out_specs=pl.BlockSpec((tb, LANES), lambda i: (i, 0)),
        out_shape=jax.ShapeDtypeStruct((t, LANES), F32),
        scratch_shapes=[pltpu.VMEM((1, LANES), F32)],
        compiler_params=_params("arbitrary"),
    )(proj, bf_pad)


def _fgate_bwd(proj, bf_pad, dcq, dck, fcol, n_heads, name):
    t = proj.shape[0]
    tb = _pick(t, ROW_TILES)
    nb = t // tb

    def body(p_ref, b_ref, dcq_ref, dck_ref, dz_ref, db_ref, carry):
        @pl.when(pl.program_id(0) == 0)
        def _():
            carry[...] = jnp.zeros_like(carry)
            db_ref[...] = jnp.zeros_like(db_ref)

        r = lax.broadcasted_iota(jnp.int32, (tb, tb), 0)
        s = lax.broadcasted_iota(jnp.int32, (tb, tb), 1)
        tri = (s >= r).astype(BF16)
        dlf = _tri_matmul(tri, dcq_ref[...] + dck_ref[...]) + carry[...]
        carry[...] = dlf[0:1, :]
        z = p_ref[...] + b_ref[...]
        lane = lax.broadcasted_iota(jnp.int32, (tb, LANES), 1)
        dz = jnp.where(lane < n_heads, dlf * jax.nn.sigmoid(-z), 0.0)
        dz_ref[...] = dz
        db_ref[...] += jnp.sum(dz, axis=0, keepdims=True)

    dz, db = pl.pallas_call(
        body, name=name, grid=(nb,),
        in_specs=[pl.BlockSpec((tb, LANES), lambda i: (nb - 1 - i, fcol)),
                  pl.BlockSpec((1, LANES), lambda i: (0, 0)),
                  pl.BlockSpec((tb, LANES), lambda i: (nb - 1 - i, 0)),
                  pl.BlockSpec((tb, LANES), lambda i: (nb - 1 - i, 0))],
        out_specs=[pl.BlockSpec((tb, LANES), lambda i: (nb - 1 - i, 0)),
                   pl.BlockSpec((1, LANES), lambda i: (0, 0))],
        out_shape=[jax.ShapeDtypeStruct((t, LANES), F32), jax.ShapeDtypeStruct((1, LANES), F32)],
        scratch_shapes=[pltpu.VMEM((1, LANES), F32)],
        compiler_params=_params("arbitrary"),
    )(proj, bf_pad, dcq, dck)
    return dz, db[0]


def _split3(x):
    hi = x.astype(BF16)
    r = x - hi.astype(F32)
    mid = r.astype(BF16)
    return hi, mid, (r - mid.astype(F32)).astype(BF16)


def _attn_fwd(qa, ka, va, dh, name, gather=()):
    nh, t, da = qa.shape
    tq = _pick(t, ROW_TILES)
    hb = 2 if nh % 2 == 0 else 1
    heads = range(hb)
    n = len(gather)
    steps = (nh // hb, t // tq)

    def body(q_ref, k_ref, v_ref, *rest):
        x_refs, o_ref, g_refs = rest[:n], rest[n], rest[n + 1:2 * n + 1]
        m_s, acc_s = rest[2 * n + 1:2 * n + 3]
        qi = pl.program_id(1)
        if n:
            start, forward, finish = _gather_phases(x_refs, g_refs, *rest[2 * n + 3:])
            at = lambda hh, qq: jnp.logical_and(pl.program_id(0) == hh, qi == qq)
            pl.when(at(0, 0))(start)
            pl.when(at(steps[0] // 2, 0))(forward)
        m_s[...] = jnp.full(m_s.shape, NEG_BIG, F32)
        acc_s[...] = jnp.zeros_like(acc_s)

        def step(j, diagonal):
            off = pl.multiple_of(j * tq, tq)
            s = [lax.dot_general(q_ref[g], k_ref[g, pl.ds(off, tq), :], NT, preferred_element_type=F32)
                 for g in heads]
            if diagonal:
                row = lax.broadcasted_iota(jnp.int32, (tq, tq), 0)
                col = lax.broadcasted_iota(jnp.int32, (tq, tq), 1)
                s = [jnp.where(col > row, NEG_BIG, sg) for sg in s]
            m_prev = [m_s[g] for g in heads]
            m_new = [jnp.maximum(m_prev[g], jnp.max(s[g], axis=1, keepdims=True)) for g in heads]
            p = [jnp.exp(s[g] - m_new[g]).astype(BF16) for g in heads]
            pv = [jnp.dot(p[g], v_ref[g, pl.ds(off, tq), :], preferred_element_type=F32) for g in heads]
            for g in heads:
                acc_s[g] = jnp.exp(m_prev[g] - m_new[g]) * acc_s[g] + pv[g]
                m_s[g] = m_new[g]

        def loop(j, carry):
            step(j, False)
            return carry

        lax.fori_loop(0, qi, loop, 0)
        step(qi, True)
        lane = lax.broadcasted_iota(jnp.int32, (tq, da), 1)
        for g in heads:
            acc = acc_s[g]
            l = jnp.sum(jnp.where(lane == dh, acc, 0.0), axis=1, keepdims=True)
            o_ref[g] = jnp.where(lane == dh, m_s[g] + jnp.log(l), acc / l)
        if n:
            pl.when(at(steps[0] - 1, steps[1] - 1))(finish)

    full = pl.BlockSpec((hb, t, da), lambda h, qi: (h, 0, 0))
    blk = pl.BlockSpec((hb, tq, da), lambda h, qi: (h, qi, 0))
    return pl.pallas_call(
        body, name=name, grid=steps,
        in_specs=[blk, full, full] + [ANY] * n, out_specs=[blk] + [ANY] * n,
        out_shape=[jax.ShapeDtypeStruct((nh, t, da), F32)] + _gather_shapes(gather),
        scratch_shapes=[pltpu.VMEM((hb, tq, 1), F32), pltpu.VMEM((hb, tq, da), F32)] + (_gather_sems(n) if n else []),
        compiler_params=_params("arbitrary", "arbitrary"),
    )(qa, ka, va, *gather)


def _attn_bwd(qa, ka, va, doa, name, exchange=()):
    nh, t, da = qa.shape
    tq = _pick(t, ROW_TILES)
    nq = t // tq
    n = len(exchange)

    def body(q_ref, do_ref, k_ref, v_ref, *rest):
        p_refs, (dq_ref, dk_ref, dv_ref), r_refs = rest[:n], rest[n:n + 3], rest[n + 3:2 * n + 3]
        kj = pl.program_id(1)
        if n:
            start, finish = _chip_exchange_phases(p_refs, r_refs, *rest[2 * n + 3:])
            pl.when(jnp.logical_and(pl.program_id(0) == 0, kj == 0))(start)

        @pl.when(kj == 0)
        def _():
            dq_ref[...] = jnp.zeros_like(dq_ref)

        dk_ref[...] = jnp.zeros_like(dk_ref)
        dv_ref[...] = jnp.zeros_like(dv_ref)
        kb = k_ref[...]
        vb = v_ref[...]

        def step(i, diagonal):
            off = pl.multiple_of(i * tq, tq)
            qb = q_ref[pl.ds(off, tq), :]
            dob = do_ref[pl.ds(off, tq), :]
            st = lax.dot_general(kb, qb, NT, preferred_element_type=F32)
            if diagonal:
                row = lax.broadcasted_iota(jnp.int32, (tq, tq), 0)
                col = lax.broadcasted_iota(jnp.int32, (tq, tq), 1)
                st = jnp.where(row > col, NEG_BIG, st)
            pt = jnp.exp(st)
            dst = (pt * lax.dot_general(vb, dob, NT, preferred_element_type=F32)).astype(BF16)
            dv_ref[...] += jnp.dot(pt.astype(BF16), dob, preferred_element_type=F32)
            dk_ref[...] += jnp.dot(dst, qb, preferred_element_type=F32)
            dq_ref[pl.ds(off, tq), :] += lax.dot_general(dst, kb, TN, preferred_element_type=F32)

        def loop(i, carry):
            step(i, False)
            return carry

        step(kj, True)
        lax.fori_loop(kj + 1, nq, loop, 0)
        if n:
            pl.when(jnp.logical_and(pl.program_id(0) == nh - 1, kj == nq - 1))(finish)

    full = pl.BlockSpec((None, t, da), lambda h, j: (h, 0, 0))
    blk = pl.BlockSpec((None, tq, da), lambda h, j: (h, j, 0))
    return pl.pallas_call(
        body, name=name, grid=(nh, nq),
        in_specs=[full, full, blk, blk] + [ANY] * n, out_specs=[full, blk, blk] + [ANY] * n,
        out_shape=[jax.ShapeDtypeStruct((nh, t, da), F32)] * 3 + _chip_exchange_shapes(exchange),
        scratch_shapes=_chip_exchange_sems(n) if n else [],
        compiler_params=_params("arbitrary", "arbitrary"),
    )(qa, doa, ka, va, *exchange)


def _head_select(w, dh, h, to_heads):
    shape = (w, LANES) if to_heads else (LANES, w)
    r = lax.broadcasted_iota(jnp.int32, shape, 0)
    c = lax.broadcasted_iota(jnp.int32, shape, 1)
    nat, col = (r, c) if to_heads else (c, r)
    return jnp.logical_and(nat == col + h * dh, col < dh).astype(BF16)


def _column(x, lane, j):
    return jnp.sum(jnp.where(lane == j, x, 0.0), axis=1, keepdims=True)


def _bias_columns(lane, first, value):
    out = jnp.zeros(lane.shape, F32)
    for j, term in enumerate(_split3(value)):
        out = out + jnp.where(lane == first + j, -term.astype(F32), 0.0)
    return out


def _attn_pack(proj, cgate, w, nh, scale, name):
    t = proj.shape[0]
    dh = w // nh
    tb = _pick(t, ROW_TILES)

    def body(q_ref, k_ref, v_ref, c_ref, qa_ref, ka_ref, va_ref):
        lane = lax.broadcasted_iota(jnp.int32, (tb, LANES), 1)
        ones_qv = jnp.where(jnp.logical_and(lane >= dh, lane < dh + 3), 1.0, 0.0)
        ones_k = jnp.where(jnp.logical_and(lane >= dh + 3, lane < dh + 7), 1.0, 0.0)
        qb = (q_ref[...] * scale).astype(BF16)
        kb = k_ref[...].astype(BF16)
        vb = v_ref[...].astype(BF16)
        cblk = c_ref[...]
        for h in range(nh):
            sel = _head_select(w, dh, h, True)
            qa_ref[h] = (jnp.dot(qb, sel, preferred_element_type=F32) + ones_qv).astype(BF16)
            va_ref[h] = (jnp.dot(vb, sel, preferred_element_type=F32) + ones_qv).astype(BF16)
            bias = _bias_columns(lane, dh, _column(cblk, lane, h))
            ka_ref[h] = (jnp.dot(kb, sel, preferred_element_type=F32) + bias + ones_k).astype(BF16)

    col = lambda j: pl.BlockSpec((tb, w), lambda i: (i, j))
    out = pl.BlockSpec((nh, tb, LANES), lambda i: (0, i, 0))
    return pl.pallas_call(
        body, name=name, grid=(t // tb,),
        in_specs=[col(0), col(1), col(2), pl.BlockSpec((tb, LANES), lambda i: (i, 0))],
        out_specs=[out, out, out],
        out_shape=[jax.ShapeDtypeStruct((nh, t, LANES), BF16)] * 3,
        compiler_params=_params("parallel"),
    )(proj, proj, proj, cgate)


def _attn_pack_bwd(dmix, oa, qa, w, nh, name):
    t = dmix.shape[0]
    dh = w // nh
    tb = _pick(t, ROW_TILES)

    def body(d_ref, oa_ref, qa_ref, doa_ref, qa2_ref):
        lane = lax.broadcasted_iota(jnp.int32, (tb, LANES), 1)
        db = d_ref[...].astype(BF16)
        for h in range(nh):
            do_h = jnp.dot(db, _head_select(w, dh, h, True), preferred_element_type=F32)
            o_h = oa_ref[h]
            delta = jnp.sum(jnp.where(lane < dh, do_h * o_h, 0.0), axis=1, keepdims=True)
            doa_ref[h] = (do_h + _bias_columns(lane, dh, delta)).astype(BF16)
            qa2_ref[h] = (qa_ref[h].astype(F32) + _bias_columns(lane, dh + 4, _column(o_h, lane, dh))).astype(BF16)

    blk = pl.BlockSpec((nh, tb, LANES), lambda i: (0, i, 0))
    return pl.pallas_call(
        body, name=name, grid=(t // tb,),
        in_specs=[pl.BlockSpec((tb, w), lambda i: (i, 0)), blk, blk], out_specs=[blk, blk],
        out_shape=[jax.ShapeDtypeStruct((nh, t, LANES), BF16)] * 2,
        compiler_params=_params("parallel"),
    )(dmix, oa, qa)


def _attn_unpack(xa, w, nh, mult, sum_col, sum_sign, name):
    t = xa.shape[1]
    dh = w // nh
    tb = _pick(t, ROW_TILES)

    def body(x_ref, o_ref, *rest):
        lane = lax.broadcasted_iota(jnp.int32, (tb, LANES), 1)
        acc = jnp.zeros((tb, w), F32)
        cols = jnp.zeros((tb, LANES), F32)
        for h in range(nh):
            xh = x_ref[h]
            acc = acc + jnp.dot((xh * mult).astype(BF16), _head_select(w, dh, h, False), preferred_element_type=F32)
            if sum_col is not None:
                cols = cols + jnp.where(lane == h, sum_sign * _column(xh, lane, sum_col), 0.0)
        o_ref[...] = acc.astype(BF16)
        if sum_col is not None:
            rest[0][...] = cols

    nat = pl.BlockSpec((tb, w), lambda i: (i, 0))
    lanes = pl.BlockSpec((tb, LANES), lambda i: (i, 0))
    return pl.pallas_call(
        body, name=name, grid=(t // tb,),
        in_specs=[pl.BlockSpec((nh, tb, LANES), lambda i: (0, i, 0))],
        out_specs=[nat, lanes] if sum_col is not None else [nat],
        out_shape=[jax.ShapeDtypeStruct((t, w), BF16)] + ([jax.ShapeDtypeStruct((t, LANES), F32)]
                                                            if sum_col is not None else []),
        compiler_params=_params("parallel"),
    )(xa)


def _conv_fwd(proj, cw, w, bcol, name):
    t = proj.shape[0]
    tb = _pick(t, ROW_TILES)
    hb = tb // SUBLANES

    def body(b_ref, c_ref, h_ref, cp_ref, hp_ref, w_ref, y_ref):
        i = pl.program_id(0)
        zp = jnp.where(i > 0, cp_ref[...] * hp_ref[...], 0.0)
        zext = jnp.concatenate([zp, c_ref[...] * h_ref[...]], axis=0)
        z1 = pltpu.roll(zext, 1, 0)[SUBLANES:]
        z2 = pltpu.roll(zext, 2, 0)[SUBLANES:]
        y = w_ref[2:3, :] * zext[SUBLANES:] + w_ref[1:2, :] * z1 + w_ref[0:1, :] * z2
        y_ref[...] = (b_ref[...] * y).astype(BF16)

    cur = lambda j: pl.BlockSpec((tb, w), lambda i: (i, bcol + j))
    prev = lambda j: pl.BlockSpec((SUBLANES, w), lambda i: (jnp.maximum(i * hb - 1, 0), bcol + j))
    return pl.pallas_call(
        body, name=name, grid=(t // tb,),
        in_specs=[cur(0), cur(1), cur(2), prev(1), prev(2), pl.BlockSpec(cw.shape, lambda i: (0, 0))],
        out_specs=pl.BlockSpec((tb, w), lambda i: (i, 0)),
        out_shape=jax.ShapeDtypeStruct((t, w), BF16), compiler_params=_params("parallel"),
    )(proj, proj, proj, proj, proj, cw)


def _conv_bwd(proj, cw, dmix, w, bcol, name):
    t = proj.shape[0]
    tb = _pick(t, ROW_TILES)
    hb = tb // SUBLANES
    nb = t // tb
    n_ext = tb + SUBLANES

    def body(b_ref, c_ref, h_ref, cp_ref, hp_ref, bn_ref, d_ref, dn_ref, w_ref, db_ref, dc_ref, dh_ref, dw_ref):
        i = pl.program_id(0)
        c = c_ref[...]
        hh = h_ref[...]
        zp = jnp.where(i > 0, cp_ref[...] * hp_ref[...], 0.0)
        zext = jnp.concatenate([zp, c * hh], axis=0)
        z0 = zext[SUBLANES:]
        z1 = pltpu.roll(zext, 1, 0)[SUBLANES:]
        z2 = pltpu.roll(zext, 2, 0)[SUBLANES:]
        y = w_ref[2:3, :] * z0 + w_ref[1:2, :] * z1 + w_ref[0:1, :] * z2
        d = d_ref[...]
        db_ref[...] = d * y
        dy = d * b_ref[...]
        dyn = jnp.where(i < nb - 1, dn_ref[...] * bn_ref[...], 0.0)
        dext = jnp.concatenate([dy, dyn], axis=0)
        dy1 = pltpu.roll(dext, n_ext - 1, 0)[:tb]
        dy2 = pltpu.roll(dext, n_ext - 2, 0)[:tb]
        dz = w_ref[2:3, :] * dy + w_ref[1:2, :] * dy1 + w_ref[0:1, :] * dy2
        dc_ref[...] = dz * hh
        dh_ref[...] = dz * c

        @pl.when(i == 0)
        def _():
            dw_ref[...] = jnp.zeros_like(dw_ref)

        dw_ref[0:1, :] += jnp.sum(dy * z2, axis=0, keepdims=True)
        dw_ref[1:2, :] += jnp.sum(dy * z1, axis=0, keepdims=True)
        dw_ref[2:3, :] += jnp.sum(dy * z0, axis=0, keepdims=True)

    cur = lambda j: pl.BlockSpec((tb, w), lambda i: (i, bcol + j))
    prev = lambda j: pl.BlockSpec((SUBLANES, w), lambda i: (jnp.maximum(i * hb - 1, 0), bcol + j))
    nxt = lambda col: pl.BlockSpec((SUBLANES, w), lambda i: (jnp.minimum((i + 1) * hb, nb * hb - 1), col))
    out = pl.BlockSpec((tb, w), lambda i: (i, 0))
    return pl.pallas_call(
        body, name=name, grid=(nb,),
        in_specs=[cur(0), cur(1), cur(2), prev(1), prev(2), nxt(bcol),
                  pl.BlockSpec((tb, w), lambda i: (i, 1)), nxt(1), pl.BlockSpec(cw.shape, lambda i: (0, 0))],
        out_specs=[out, out, out, pl.BlockSpec(cw.shape, lambda i: (0, 0))],
        out_shape=[jax.ShapeDtypeStruct((t, w), F32)] * 3 + [jax.ShapeDtypeStruct(cw.shape, F32)],
        compiler_params=_params("arbitrary"),
    )(proj, proj, proj, proj, proj, proj, dmix, dmix, cw)


SQRT_HALF = 0.7071067811865476
INV_SQRT_2PI = 0.3989422804014327


def _gelu(x):
    return 0.5 * x * (1.0 + lax.erf(x * SQRT_HALF))


def _gelu_grad(x):
    return 0.5 * (1.0 + lax.erf(x * SQRT_HALF)) + x * (INV_SQRT_2PI * jnp.exp(-0.5 * x * x))


def _sgu_fwd(uv, ln_g, ln_b, wm, bs_full, name):
    t, d2 = uv.shape
    d = d2 // 2
    ng, pb, _ = wm.shape
    gd = d // ng
    tb = _pick(t, ROW_TILES[1:] or ROW_TILES)
    assert tb % pb == 0

    def body(uv_ref, g_ref, b_ref, w_ref, bs_ref, o_ref):
        u = _gelu(uv_ref[:, :d])
        v = _gelu(uv_ref[:, d:])
        mu = jnp.mean(v, axis=-1, keepdims=True)
        vc = v - mu
        var = jnp.mean(vc * vc, axis=-1, keepdims=True)
        vn = (vc * lax.rsqrt(var + LN_EPS) * g_ref[...] + b_ref[...]).astype(BF16)
        for r in range(tb // pb):
            rows = slice(r * pb, (r + 1) * pb)
            for gi in range(ng):
                cols = slice(gi * gd, (gi + 1) * gd)
                s = jnp.dot(w_ref[gi], vn[rows, cols], preferred_element_type=F32) + bs_ref[:, cols]
                o_ref[rows, cols] = (u[rows, cols] * s).astype(BF16)

    vec = pl.BlockSpec((1, d), lambda i: (0, 0))
    return pl.pallas_call(
        body, name=name, grid=(t // tb,),
        in_specs=[pl.BlockSpec((tb, d2), lambda i: (i, 0)), vec, vec,
                  pl.BlockSpec(wm.shape, lambda i: (0, 0, 0)), pl.BlockSpec((pb, d), lambda i: (0, 0))],
        out_specs=pl.BlockSpec((tb, d), lambda i: (i, 0)),
        out_shape=jax.ShapeDtypeStruct((t, d), BF16), compiler_params=_params("parallel"),
    )(uv, ln_g.reshape(1, d), ln_b.reshape(1, d), wm, bs_full)


def _sgu_bwd(uv, ln_g, ln_b, wm, bs_full, dgated, name):
    t, d2 = uv.shape
    d = d2 // 2
    ng, pb, _ = wm.shape
    gd = d // ng
    tb = _pick(t, ROW_TILES[1:] or ROW_TILES)
    nb = t // tb

    def body(uv_ref, g_ref, b_ref, w_ref, bs_ref, dg_ref, o_ref, dw_ref, dbs_ref, dlg_ref, dlb_ref,
             du_s, dvn_s, dbs_s):
        i = pl.program_id(0)

        @pl.when(i == 0)
        def _():
            dw_ref[...] = jnp.zeros_like(dw_ref)
            dbs_s[...] = jnp.zeros_like(dbs_s)
            dlg_ref[...] = jnp.zeros_like(dlg_ref)
            dlb_ref[...] = jnp.zeros_like(dlb_ref)

        upre = uv_ref[:, :d]
        vpre = uv_ref[:, d:]
        u = _gelu(upre)
        v = _gelu(vpre)
        mu = jnp.mean(v, axis=-1, keepdims=True)
        vc = v - mu
        var = jnp.mean(vc * vc, axis=-1, keepdims=True)
        rstd = lax.rsqrt(var + LN_EPS)
        xhat = vc * rstd
        vn = (xhat * g_ref[...] + b_ref[...]).astype(BF16)
        dgt = dg_ref[...].astype(F32)
        for r in range(tb // pb):
            rows = slice(r * pb, (r + 1) * pb)
            for gi in range(ng):
                cols = slice(gi * gd, (gi + 1) * gd)
                vblk = vn[rows, cols]
                s = jnp.dot(w_ref[gi], vblk, preferred_element_type=F32) + bs_ref[:, cols]
                dblk = dgt[rows, cols]
                du_s[rows, cols] = dblk * s
                ds = dblk * u[rows, cols]
                dsb = ds.astype(BF16)
                dvn_s[rows, cols] = lax.dot_general(w_ref[gi], dsb, (((0,), (0,)), ((), ())),
                                                    preferred_element_type=F32)
                dw_ref[gi] += lax.dot_general(dsb, vblk, (((1,), (1,)), ((), ())), preferred_element_type=F32)
                dbs_s[:, cols] += ds
        dvn = dvn_s[...]
        dlg_ref[...] += jnp.sum(dvn * xhat, axis=0, keepdims=True)
        dlb_ref[...] += jnp.sum(dvn, axis=0, keepdims=True)
        dxh = dvn * g_ref[...]
        m1 = jnp.mean(dxh, axis=-1, keepdims=True)
        m2 = jnp.mean(dxh * xhat, axis=-1, keepdims=True)
        dv = rstd * (dxh - m1 - xhat * m2)
        o_ref[:, :d] = (du_s[...] * _gelu_grad(upre)).astype(BF16)
        o_ref[:, d:] = (dv * _gelu_grad(vpre)).astype(BF16)

        @pl.when(i == nb - 1)
        def _():
            lane = lax.broadcasted_iota(jnp.int32, (pb, LANES), 1)
            acc = jnp.zeros((pb, LANES), F32)
            for gi in range(ng):
                col = jnp.sum(dbs_s[:, gi * gd:(gi + 1) * gd], axis=1, keepdims=True)
                acc = acc + jnp.where(lane == gi, col, 0.0)
            dbs_ref[...] = acc

    vec = pl.BlockSpec((1, d), lambda i: (0, 0))
    duv, dw, dbs, dlg, dlb = pl.pallas_call(
        body, name=name, grid=(nb,),
        in_specs=[pl.BlockSpec((tb, d2), lambda i: (i, 0)), vec, vec,
                  pl.BlockSpec(wm.shape, lambda i: (0, 0, 0)), pl.BlockSpec((pb, d), lambda i: (0, 0)),
                  pl.BlockSpec((tb, d), lambda i: (i, 0))],
        out_specs=[pl.BlockSpec((tb, d2), lambda i: (i, 0)), pl.BlockSpec(wm.shape, lambda i: (0, 0, 0)),
                   pl.BlockSpec((pb, LANES), lambda i: (0, 0)), vec, vec],
        out_shape=[jax.ShapeDtypeStruct((t, d2), BF16), jax.ShapeDtypeStruct(wm.shape, F32),
                   jax.ShapeDtypeStruct((pb, LANES), F32), jax.ShapeDtypeStruct((1, d), F32),
                   jax.ShapeDtypeStruct((1, d), F32)],
        scratch_shapes=[pltpu.VMEM((tb, d), F32), pltpu.VMEM((tb, d), F32), pltpu.VMEM((pb, d), F32)],
        compiler_params=_params("arbitrary"),
    )(uv, ln_g.reshape(1, d), ln_b.reshape(1, d), wm, bs_full, dgated)
    return duv, dw, dbs, dlg[0], dlb[0]


def _adamw(w, g, m, v, name):
    shape = w.shape
    cols = shape[-1]
    rows = w.size // cols
    tr = _pick(rows, (512, 256, 352, 128, 64, 32, 16, 8))

    def body(w_ref, g_ref, m_ref, v_ref, d_ref, mo_ref, vo_ref):
        d_ref[...], mo_ref[...], vo_ref[...] = _adam_update(w_ref[...], g_ref[...], m_ref[...], v_ref[...])

    spec = pl.BlockSpec((tr, cols), lambda i: (i, 0))
    outs = pl.pallas_call(
        body, name=name, grid=(rows // tr,),
        in_specs=[spec] * 4, out_specs=[spec] * 3,
        out_shape=[jax.ShapeDtypeStruct((rows, cols), F32)] * 3,
        compiler_params=_params("parallel"),
    )(*[a.reshape(rows, cols) for a in (w, g, m, v)])
    return [o.reshape(shape) for o in outs]


ANY = pl.BlockSpec(memory_space=pl.ANY)


def _place():
    return lax.axis_index("x"), lax.axis_index("y"), lax.axis_index("c")


def _all_gather(shards, name):
    n = len(shards)

    def body(*refs):
        start, forward, finish = _gather_phases(refs[:n], refs[n:2 * n], *refs[2 * n:])
        start()
        forward()
        finish()

    return pl.pallas_call(
        body, name=name, in_specs=[ANY] * n, out_specs=[ANY] * n,
        out_shape=_gather_shapes(shards), scratch_shapes=_gather_sems(n),
    )(*shards)


def _gather_shapes(shards):
    return [jax.ShapeDtypeStruct((N_DEV,) + s.shape, s.dtype) for s in shards]


def _gather_sems(n):
    return [pltpu.SemaphoreType.DMA((7 * n,)), pltpu.SemaphoreType.DMA((7 * n,)), pltpu.SemaphoreType.DMA((n,))]


def _gather_phases(x_refs, out_refs, send_sems, recv_sems, local_sems):
    n = len(x_refs)
    x, y, c = _place()
    me, sibling = (x, y, c), (x, y, 1 - c)
    chips = [(1 - x, y), (x, 1 - y), (1 - x, 1 - y)]

    def copy(a, k, block, to, own=False):
        px, py, pc = block
        rows = out_refs[a].at[4 * px + 2 * py + pc]
        return pltpu.make_async_remote_copy(
            src_ref=x_refs[a] if own else rows, dst_ref=rows,
            send_sem=send_sems.at[7 * a + k], recv_sem=recv_sems.at[7 * a + k],
            device_id=to, device_id_type=MESH)

    def local(a):
        return pltpu.make_async_copy(x_refs[a], out_refs[a].at[4 * x + 2 * y + c], local_sems.at[a])

    def first(a):
        return [copy(a, 0, me, sibling, own=True)] + [copy(a, 1 + j, me, (*chip, c), own=True)
                                                      for j, chip in enumerate(chips)]

    def start():
        for a in range(n):
            local(a).start()
            for cp in first(a):
                cp.start()

    def forward():
        for j, chip in enumerate(chips):
            for a in range(n):
                copy(a, 1 + j, (*chip, c), me).wait_recv()
                copy(a, 4 + j, (*chip, c), sibling).start()

    def finish():
        for a in range(n):
            copy(a, 0, sibling, me).wait_recv()
            for j, chip in enumerate(chips):
                copy(a, 4 + j, (*chip, 1 - c), me).wait_recv()
        for a in range(n):
            for cp in first(a) + [copy(a, 4 + j, (*chip, c), sibling) for j, chip in enumerate(chips)]:
                cp.wait_send()
            local(a).wait()

    return start, forward, finish


def _rs_sibling_exchange(packed, name):
    n = len(packed)

    def body(*refs):
        p_refs, r_refs = refs[:n], refs[n:2 * n]
        send_sems, recv_sems = refs[2 * n:]
        x, y, c = _place()
        cps = []
        for a in range(n):
            for j in range(4):
                cps.append(pltpu.make_async_remote_copy(
                    src_ref=p_refs[a].at[2 * j + (1 - c)], dst_ref=r_refs[a].at[j],
                    send_sem=send_sems.at[4 * a + j], recv_sem=recv_sems.at[4 * a + j],
                    device_id=(x, y, 1 - c), device_id_type=MESH))
        for cp in cps:
            cp.start()
        for cp in cps:
            cp.wait()

    return pl.pallas_call(
        body, name=name, in_specs=[ANY] * n, out_specs=[ANY] * n,
        out_shape=[jax.ShapeDtypeStruct((4,) + p.shape[1:], p.dtype) for p in packed],
        scratch_shapes=[pltpu.SemaphoreType.DMA((4 * n,)), pltpu.SemaphoreType.DMA((4 * n,))],
    )(*packed)


def _rs_chip_sum(packed, from_sibling, c_idx, name):
    _, r, cc = packed.shape
    tr = _pick(r, (512, 256, 352, 128))

    def body(c_ref, a_ref, b_ref, o_ref):
        o_ref[...] = (a_ref[...].astype(F32) + b_ref[...].astype(F32)).astype(o_ref.dtype)

    return pl.pallas_call(
        body, name=name,
        grid_spec=pltpu.PrefetchScalarGridSpec(
            num_scalar_prefetch=1, grid=(4, r // tr),
            in_specs=[pl.BlockSpec((None, tr, cc), lambda j, i, c_ref: (2 * j + c_ref[0], i, 0)),
                      pl.BlockSpec((None, tr, cc), lambda j, i, c_ref: (j, i, 0))],
            out_specs=pl.BlockSpec((None, tr, cc), lambda j, i, c_ref: (j, i, 0))),
        out_shape=jax.ShapeDtypeStruct((4, r, cc), packed.dtype),
        compiler_params=_params("parallel", "parallel"),
    )(c_idx, packed, from_sibling)


def _rs_chip_exchange(partial, name):
    n = len(partial)

    def body(*refs):
        start, finish = _chip_exchange_phases(refs[:n], refs[n:2 * n], *refs[2 * n:])
        start()
        finish()

    return pl.pallas_call(
        body, name=name, in_specs=[ANY] * n, out_specs=[ANY] * n,
        out_shape=_chip_exchange_shapes(partial), scratch_shapes=_chip_exchange_sems(n),
    )(*partial)


def _chip_exchange_shapes(partial):
    return [jax.ShapeDtypeStruct((3,) + p.shape[1:], p.dtype) for p in partial]


def _chip_exchange_sems(n):
    return [pltpu.SemaphoreType.DMA((3 * n,)), pltpu.SemaphoreType.DMA((3 * n,))]


def _chip_exchange_phases(p_refs, r_refs, send_sems, recv_sems):
    x, y, c = _place()
    chips = [(1 - x, y), (x, 1 - y), (1 - x, 1 - y)]

    def copies():
        return [pltpu.make_async_remote_copy(
            src_ref=p_refs[a].at[2 * tx + ty], dst_ref=r_refs[a].at[k],
            send_sem=send_sems.at[3 * a + k], recv_sem=recv_sems.at[3 * a + k],
            device_id=(tx, ty, c), device_id_type=MESH)
            for a in range(len(p_refs)) for k, (tx, ty) in enumerate(chips)]

    def start():
        for cp in copies():
            cp.start()

    def finish():
        for cp in copies():
            cp.wait()

    return start, finish


def _adam_update(w, g, m, v):
    mn = ADAM_B1 * m + (1.0 - ADAM_B1) * g
    vn = ADAM_B2 * v + (1.0 - ADAM_B2) * (g * g)
    m_hat = mn / (1.0 - ADAM_B1 ** ADAM_STEP)
    v_hat = vn / (1.0 - ADAM_B2 ** ADAM_STEP)
    return -ADAM_LR * (m_hat / (jnp.sqrt(v_hat) + ADAM_EPS) + ADAM_WD * w), mn, vn


def _rs_final_adamw(partial, received, chip_idx, w, m, v, name):
    _, r, cc = partial.shape
    tr = _pick(r, (512, 256, 352, 128))

    def body(c_ref, a_ref, r_ref, w_ref, m_ref, v_ref, g_ref, d_ref, mo_ref, vo_ref):
        g = a_ref[...].astype(F32)
        for k in range(3):
            g = g + r_ref[k].astype(F32)
        g_ref[...] = g
        d_ref[...], mo_ref[...], vo_ref[...] = _adam_update(w_ref[...], g, m_ref[...], v_ref[...])

    row = pl.BlockSpec((tr, cc), lambda i, c_ref: (i, 0))
    return pl.pallas_call(
        body, name=name,
        grid_spec=pltpu.PrefetchScalarGridSpec(
            num_scalar_prefetch=1, grid=(r // tr,),
            in_specs=[pl.BlockSpec((None, tr, cc), lambda i, c_ref: (c_ref[0], i, 0)),
                      pl.BlockSpec((3, tr, cc), lambda i, c_ref: (0, i, 0)), row, row, row],
            out_specs=[row] * 4),
        out_shape=[jax.ShapeDtypeStruct((r, cc), F32)] * 4,
        compiler_params=_params("parallel"),
    )(chip_idx, partial, received, w.reshape(r, cc), m.reshape(r, cc), v.reshape(r, cc))


def _all_reduce_small(vals, name):
    r, cc = vals.shape

    def body(v_ref, o_ref, buf, send_sems, recv_sems):
        x, y, c = _place()
        me = 4 * x + 2 * y + c
        buf[0] = v_ref[...]
        cps = []
        for k in range(1, N_DEV):
            kx, ky, kc = (k >> 2) & 1, (k >> 1) & 1, k & 1
            peer = (1 - x if kx else x, 1 - y if ky else y, 1 - c if kc else c)
            cps.append(pltpu.make_async_remote_copy(
                src_ref=buf.at[0], dst_ref=buf.at[k], send_sem=send_sems.at[k - 1],
                recv_sem=recv_sems.at[k - 1], device_id=peer, device_id_type=MESH))
        for cp in cps:
            cp.start()
        for cp in cps:
            cp.wait()
        acc = buf[jnp.bitwise_xor(me, 0)]
        for dev in range(1, N_DEV):
            acc = acc + buf[jnp.bitwise_xor(me, dev)]
        o_ref[...] = acc

    vm = pl.BlockSpec(memory_space=pltpu.VMEM)
    return pl.pallas_call(
        body, name=name, in_specs=[vm], out_specs=vm,
        out_shape=jax.ShapeDtypeStruct((r, cc), F32),
        scratch_shapes=[pltpu.VMEM((N_DEV, r, cc), F32), pltpu.SemaphoreType.DMA((7,)),
                        pltpu.SemaphoreType.DMA((7,))],
        compiler_params=pltpu.CompilerParams(vmem_limit_bytes=VMEM_LIMIT),
    )(vals)


def _lanes(flat):
    pad = (-flat.shape[0]) % (SUBLANES * LANES)
    return jnp.pad(flat, (0, pad)).reshape(-1, LANES)


def kernel(x, even_w_in, even_b_f, even_conv_w, even_w_out, odd_w_in, odd_v_ln_g, odd_v_ln_b, odd_w_s, odd_b_s, odd_w_out, mix_ln_g, mix_ln_b, ffn_w_in, ffn_w_out, ffn_ln_g, ffn_ln_b, loss_target, m_even_w_in, m_even_b_f, m_even_conv_w, m_even_w_out, m_odd_w_in, m_odd_v_ln_g, m_odd_v_ln_b, m_odd_w_s, m_odd_b_s, m_odd_w_out, m_mix_ln_g, m_mix_ln_b, m_ffn_w_in, m_ffn_w_out, m_ffn_ln_g, m_ffn_ln_b, v_even_w_in, v_even_b_f, v_even_conv_w, v_even_w_out, v_odd_w_in, v_odd_v_ln_g, v_odd_v_ln_b, v_odd_w_s, v_odd_b_s, v_odd_w_out, v_mix_ln_g, v_mix_ln_b, v_ffn_w_in, v_ffn_w_out, v_ffn_ln_g, v_ffn_ln_b):
    t, d = x.shape[1], x.shape[2]
    nh = even_b_f.shape[-1]
    w = even_conv_w.shape[-1] * N_DEV
    dh = w // nh
    scale = dh ** -0.5
    e_in = even_w_in.shape[-1] * N_DEV
    f2 = ffn_w_in.shape[-1] * N_DEV
    f = f2 // 2
    ng, pb = odd_w_s.shape[1], odd_w_s.shape[2]
    assert e_in == 6 * w + nh and nh <= SUBLANES and (6 * w) % LANES == 0 and d % N_DEV == 0
    mx, my, mc = _place()
    me = 4 * mx + 2 * my + mc

    big = [even_w_in[0], even_w_out[0], odd_w_in[0], odd_w_out[0],
           ffn_w_in[0], ffn_w_in[1], ffn_w_out[0], ffn_w_out[1]]
    g_in0, = _all_gather([big[0].astype(BF16)], "ag_even_w_in")
    w_in0 = g_in0.transpose(1, 0, 2).reshape(d, e_in)
    w_all0 = jnp.concatenate([w_in0[:, :3 * w], w_in0[:, 3 * w + nh:], w_in0[:, 3 * w:3 * w + nh],
                              jnp.zeros((d, LANES - nh), BF16)], axis=1)

    cs, vs = even_conv_w.shape[-1], odd_v_ln_g.shape[-1]
    small_mine = jnp.concatenate([
        lax.dynamic_update_slice(jnp.zeros((3, w), F32), even_conv_w[0], (0, me * cs)).reshape(-1),
        lax.dynamic_update_slice(jnp.zeros((d,), F32), odd_v_ln_g[0], (me * vs,)),
        lax.dynamic_update_slice(jnp.zeros((d,), F32), odd_v_ln_b[0], (me * vs,))])
    small_all = _all_reduce_small(_lanes(small_mine), "ag_small").reshape(-1)
    conv_w = small_all[:3 * w].reshape(3, w)
    vln_g = small_all[3 * w:3 * w + d]
    vln_b = small_all[3 * w + d:3 * w + 2 * d]

    bf_pad = jnp.pad(even_b_f[0], (0, LANES - nh)).reshape(1, LANES)
    chunk = jnp.arange(pb) // (pb // 2)
    ws_mask = (chunk[None, :] <= chunk[:, None])[None]
    wm = jnp.where(ws_mask, odd_w_s[0], 0.0).astype(BF16)
    bs_full = jnp.repeat(odd_b_s[0].T, d // ng, axis=1)

    x0 = x[0]
    tgt = loss_target[0]
    fcol = 6 * w // LANES
    x0b = x0.astype(BF16)
    p0 = _mm(x0b, w_all0, "nn", F32, "l0_in_proj")
    cgate = _fgate_fwd(p0, bf_pad, fcol, nh, "l0_fgate")
    assert dh + 7 <= LANES
    qa, ka, va = _attn_pack(p0, cgate, w, nh, scale, "l0_attn_pack")
    oa, g_out0, g_in1, g_out1, g_fi0, g_fi1, g_fo0, g_fo1 = _attn_fwd(
        qa, ka, va, dh, "l0_attn", gather=[s.astype(BF16) for s in big[1:]])
    w_out0, w_out1 = g_out0.reshape(2 * w, d), g_out1.reshape(d, d)
    w_fo0, w_fo1 = g_fo0.reshape(f, d), g_fo1.reshape(f, d)
    nb = N_DEV // 2
    w_fi0, w_fi1 = g_fi0.reshape(2, nb, d, -1), g_fi1.reshape(2, nb, d, -1)
    attn, = _attn_unpack(oa, w, nh, 1.0, None, 1.0, "l0_attn_unpack")
    yconv = _conv_fwd(p0, conv_w, w, 3, "l0_conv")
    mix = jnp.concatenate([attn, yconv], axis=1)
    m0 = _mm(mix, w_out0, "nn", F32, "l0_out_proj")
    x1, x1b = _ln_fwd(x0, m0, mix_ln_g[0], mix_ln_b[0], "l0_mix_ln")
    h0, gu0 = _ffn_in_swiglu(x1b, w_fi0, "l0_ffn_in")
    f0 = _mm_blk_fwd(h0, w_fo0, F32, "l0_ffn_out")
    x2, x2b = _ln_fwd(x1, f0, ffn_ln_g[0], ffn_ln_b[0], "l0_ffn_ln")

    uv = _mm_cols_fwd(x2b, g_in1, False, F32, "l1_in_proj")
    gated = _sgu_fwd(uv, vln_g, vln_b, wm, bs_full, "l1_sgu")
    m1 = _mm(gated, w_out1, "nn", F32, "l1_out_proj")
    x3, x3b = _ln_fwd(x2, m1, mix_ln_g[1], mix_ln_b[1], "l1_mix_ln")
    h1, gu1 = _ffn_in_swiglu(x3b, w_fi1, "l1_ffn_in")
    f1 = _mm_blk_fwd(h1, w_fo1, F32, "l1_ffn_out")
    x4, _ = _ln_fwd(x3, f1, ffn_ln_g[1], ffn_ln_b[1], "l1_ffn_ln")
    loss_part, dy4 = _loss(x4, tgt, "loss")

    dz4, dz4b, g_ffn_g1, g_ffn_b1 = _ln_bwd(x3, f1, ffn_ln_g[1], dy4, 1.0, None, "l1_ffn_ln_bwd")
    gd_fo1 = _mm_blk_dw(h1, dz4b, BF16, "l1_ffn_out_dw").reshape(N_DEV, -1, d)
    dgu1 = _ffn_out_dx_swiglu(dz4b, w_fo1, gu1, "l1_ffn_out_dx").reshape(N_DEV, t, -1)
    gd_fi1 = _mm_cols_dw(x3b, dgu1, N_DEV, True, BF16, "l1_ffn_in_dw")
    dx3 = _mm_cols_dx(dgu1, g_fi1, True, F32, "l1_ffn_in_dx")
    dz3, dz3b, g_mix_g1, g_mix_b1 = _ln_bwd(x2, m1, mix_ln_g[1], dz4, ALPHA, dx3, "l1_mix_ln_bwd")
    gd_out1 = _mm(gated, dz3b, "tn", BF16, "l1_out_proj_dw").reshape(N_DEV, -1, d)
    dgated = _mm(dz3b, w_out1, "nt", BF16, "l1_out_proj_dx")
    duv, g_wm, g_bs_t, g_vln_g, g_vln_b = _sgu_bwd(uv, vln_g, vln_b, wm, bs_full, dgated, "l1_sgu_bwd")
    gd_in1 = _mm_cols_dw(x2b, duv, N_DEV, False, BF16, "l1_in_proj_dw")
    dx2 = _mm_cols_dx(duv, g_in1, False, F32, "l1_in_proj_dx")

    dz2, dz2b, g_ffn_g0, g_ffn_b0 = _ln_bwd(x1, f0, ffn_ln_g[0], dz3, ALPHA, dx2, "l0_ffn_ln_bwd")
    gd_fo0 = _mm_blk_dw(h0, dz2b, BF16, "l0_ffn_out_dw").reshape(N_DEV, -1, d)
    dgu0 = _ffn_out_dx_swiglu(dz2b, w_fo0, gu0, "l0_ffn_out_dx").reshape(N_DEV, t, -1)
    gd_fi0 = _mm_cols_dw(x1b, dgu0, N_DEV, True, BF16, "l0_ffn_in_dw")
    dx1 = _mm_cols_dx(dgu0, g_fi0, True, F32, "l0_ffn_in_dx")
    dz1, dz1b, g_mix_g0, g_mix_b0 = _ln_bwd(x0, m0, mix_ln_g[0], dz2, ALPHA, dx1, "l0_mix_ln_bwd")
    gd_out0 = _mm(mix, dz1b, "tn", BF16, "l0_out_proj_dw").reshape(N_DEV, -1, d)
    dmix = _mm(dz1b, w_out0, "nt", F32, "l0_out_proj_dx")
    d_b, d_c, d_h, g_conv = _conv_bwd(p0, conv_w, dmix, w, 3, "l0_conv_bwd")
    doa, qa2 = _attn_pack_bwd(dmix, oa, qa, w, nh, "l0_attn_pack_bwd")
    big_names = ["even_w_in", "even_w_out", "odd_w_in", "odd_w_out", "ffn_w_in0", "ffn_w_in1", "ffn_w_out0", "ffn_w_out1"]
    c_idx = mc.reshape(1).astype(jnp.int32)
    chip_idx = (2 * mx + my).reshape(1).astype(jnp.int32)
    early_g = [gd_out0, gd_in1, gd_out1, gd_fi0, gd_fi1, gd_fo0, gd_fo1]
    early_sib = _rs_sibling_exchange(early_g, "rs_sibling_early")
    early_partial = [_rs_chip_sum(g, s, c_idx, "rs_chip_sum_" + n)
                     for g, s, n in zip(early_g, early_sib, big_names[1:])]
    dqa, dka, dva, *early_received = _attn_bwd(qa2, ka, va, doa, "l0_attn_bwd", exchange=early_partial)
    dq, dcq = _attn_unpack(dqa, w, nh, scale, dh + 3, 1.0, "l0_attn_unpack_dq")
    dk, dck = _attn_unpack(dka, w, nh, 1.0, dh, -1.0, "l0_attn_unpack_dk")
    dv, = _attn_unpack(dva, w, nh, 1.0, None, 1.0, "l0_attn_unpack_dv")
    dzf, g_bf = _fgate_bwd(p0, bf_pad, dcq, dck, fcol, nh, "l0_fgate_bwd")
    dp0 = jnp.concatenate([dq, dk, dv, d_b.astype(BF16), d_c.astype(BF16), d_h.astype(BF16), dzf.astype(BF16)], axis=1)
    g_all0 = _mm(x0b, dp0, "tn", F32, "l0_in_proj_dw")
    dx0 = _mm(dp0, w_all0, "nt", F32, "l0_in_proj_dx")
    grad_x = _axpy(ALPHA, dz1, dx0, "grad_x")
    gd_in0 = jnp.concatenate([g_all0[:, :3 * w], g_all0[:, 6 * w:6 * w + nh], g_all0[:, 3 * w:6 * w]], axis=1)
    gd_in0 = gd_in0.reshape(d, N_DEV, -1).transpose(1, 0, 2).astype(BF16)

    big_m = [m_even_w_in[0], m_even_w_out[0], m_odd_w_in[0], m_odd_w_out[0],
             m_ffn_w_in[0], m_ffn_w_in[1], m_ffn_w_out[0], m_ffn_w_out[1]]
    big_v = [v_even_w_in[0], v_even_w_out[0], v_odd_w_in[0], v_odd_w_out[0],
             v_ffn_w_in[0], v_ffn_w_in[1], v_ffn_w_out[0], v_ffn_w_out[1]]
    late_sib = _rs_sibling_exchange([gd_in0], "rs_sibling_late")
    late_partial = [_rs_chip_sum(gd_in0, late_sib[0], c_idx, "rs_chip_sum_" + big_names[0])]
    partial = late_partial + early_partial
    received = list(_rs_chip_exchange(late_partial, "rs_chips_late")) + list(early_received)
    upd = [_rs_final_adamw(p, r, chip_idx, wt, mt, vt, "rs_final_adamw_" + n)
           for p, r, wt, mt, vt, n in zip(partial, received, big, big_m, big_v, big_names)]
    big_out = {}
    for i, n in enumerate(["even_w_in", "even_w_out", "odd_w_in", "odd_w_out"]):
        big_out[n] = [o[None] for o in upd[i]]
    big_out["ffn_w_in"] = [jnp.stack([a, b]) for a, b in zip(upd[4], upd[5])]
    big_out["ffn_w_out"] = [jnp.stack([a, b]) for a, b in zip(upd[6], upd[7])]

    g_ws = jnp.where(ws_mask, g_wm, 0.0)
    g_bs = g_bs_t[:, :ng].T
    small_g = [g_bf[:nh], g_conv, g_vln_g, g_vln_b, g_ws, g_bs,
               jnp.stack([g_mix_g0, g_mix_g1]), jnp.stack([g_mix_b0, g_mix_b1]),
               jnp.stack([g_ffn_g0, g_ffn_g1]), jnp.stack([g_ffn_b0, g_ffn_b1])]
    small_sum = _all_reduce_small(_lanes(jnp.concatenate([a.reshape(-1) for a in small_g])), "ar_small_grads")
    small_sum = small_sum.reshape(-1)
    outs_small = []
    off = 0
    for a in small_g:
        outs_small.append(small_sum[off:off + a.size].reshape(a.shape))
        off += a.size
    gr_bf, gr_conv, gr_vg, gr_vb, gr_ws, gr_bs, gr_mg, gr_mb, gr_fg, gr_fb = outs_small

    loss = lax.psum(loss_part, ("x", "y", "c"))

    grads = {
        "even_b_f": gr_bf[None],
        "even_conv_w": lax.dynamic_slice(gr_conv, (0, me * cs), (3, cs))[None],
        "odd_v_ln_g": lax.dynamic_slice(gr_vg, (me * vs,), (vs,))[None],
        "odd_v_ln_b": lax.dynamic_slice(gr_vb, (me * vs,), (vs,))[None],
        "odd_w_s": gr_ws[None], "odd_b_s": gr_bs[None],
        "mix_ln_g": gr_mg, "mix_ln_b": gr_mb, "ffn_ln_g": gr_fg, "ffn_ln_b": gr_fb,
    }
    weights = dict(even_w_in=even_w_in, even_b_f=even_b_f, even_conv_w=even_conv_w, even_w_out=even_w_out,
                   odd_w_in=odd_w_in, odd_v_ln_g=odd_v_ln_g, odd_v_ln_b=odd_v_ln_b, odd_w_s=odd_w_s,
                   odd_b_s=odd_b_s, odd_w_out=odd_w_out, mix_ln_g=mix_ln_g, mix_ln_b=mix_ln_b,
                   ffn_w_in=ffn_w_in, ffn_w_out=ffn_w_out, ffn_ln_g=ffn_ln_g, ffn_ln_b=ffn_ln_b)
    moms = dict(even_w_in=(m_even_w_in, v_even_w_in), even_b_f=(m_even_b_f, v_even_b_f),
                even_conv_w=(m_even_conv_w, v_even_conv_w), even_w_out=(m_even_w_out, v_even_w_out),
                odd_w_in=(m_odd_w_in, v_odd_w_in), odd_v_ln_g=(m_odd_v_ln_g, v_odd_v_ln_g),
                odd_v_ln_b=(m_odd_v_ln_b, v_odd_v_ln_b), odd_w_s=(m_odd_w_s, v_odd_w_s),
                odd_b_s=(m_odd_b_s, v_odd_b_s), odd_w_out=(m_odd_w_out, v_odd_w_out),
                mix_ln_g=(m_mix_ln_g, v_mix_ln_g), mix_ln_b=(m_mix_ln_b, v_mix_ln_b),
                ffn_w_in=(m_ffn_w_in, v_ffn_w_in), ffn_w_out=(m_ffn_w_out, v_ffn_w_out),
                ffn_ln_g=(m_ffn_ln_g, v_ffn_ln_g), ffn_ln_b=(m_ffn_ln_b, v_ffn_ln_b))
    names = list(weights)
    gout, deltas, new_m, new_v = [], [], [], []
    for n in names:
        if n in big_out:
            gr, dlt, mn, vn = big_out[n]
        else:
            gr = grads[n]
            dlt, mn, vn = _adamw(weights[n], gr, moms[n][0], moms[n][1], "adamw_" + n)
        gout.append(gr.reshape(weights[n].shape))
        deltas.append(dlt.reshape(weights[n].shape))
        new_m.append(mn.reshape(weights[n].shape))
        new_v.append(vn.reshape(weights[n].shape))
    return (loss, grad_x[None], *gout, *deltas, *new_m, *new_v)
```

```python
import jax
import jax.numpy as jnp
from jax import lax
from jax.experimental import pallas as pl
from jax.experimental.pallas import tpu as pltpu

F32 = jnp.float32
BF16 = jnp.bfloat16
MESH = pl.DeviceIdType.MESH

DEPTH = 2
ALPHA = (2.0 * DEPTH) ** 0.25
LN_EPS = 1e-5
ADAM_LR = 0.001
ADAM_B1 = 0.9
ADAM_B2 = 0.999
ADAM_EPS = 1e-08
ADAM_WD = 0.01
ADAM_STEP = 10

N_DEV = 8
LANES = 128
SUBLANES = 8
VMEM_LIMIT = 48 * 1024 * 1024
NEG_BIG = -1e30
ROW_TILES = (512, 256, 128)


def _pick(n, cands):
    for c in cands:
        if c <= n and n % c == 0:
            return c
    return n


def _params(*sem):
    return pltpu.CompilerParams(dimension_semantics=sem, vmem_limit_bytes=VMEM_LIMIT)


NN = (((1,), (0,)), ((), ()))
NT = (((1,), (1,)), ((), ()))
TN = (((0,), (0,)), ((), ()))
M_TILES = (1024, 512, 1408, 256, 128)
N_TILES = (512, 640, 256, 128)
K_TILES = (2048, 1024, 512, 640, 1408, 256, 128)
K_WHOLE = 3328


def _mm_core(name, grid, a, b, a_spec, b_spec, o_spec, o_shape, o_dtype, dims, tile, pieces=None):
    nred = grid[2]
    pieces = pieces or [(lambda r: r[...], lambda r: r[...])]

    def body(a_ref, b_ref, o_ref, *acc):
        part = None
        for fa, fb in pieces:
            prod = lax.dot_general(fa(a_ref).astype(BF16), fb(b_ref).astype(BF16), dims, preferred_element_type=F32)
            part = prod if part is None else part + prod
        if nred == 1:
            o_ref[...] = part.astype(o_ref.dtype)
            return
        acc_ref, = acc
        kk = pl.program_id(2)

        @pl.when(kk == 0)
        def _():
            acc_ref[...] = jnp.zeros_like(acc_ref)

        acc_ref[...] += part

        @pl.when(kk == nred - 1)
        def _():
            o_ref[...] = acc_ref[...].astype(o_ref.dtype)

    return pl.pallas_call(
        body, name=name, grid=grid, in_specs=[a_spec, b_spec], out_specs=o_spec,
        out_shape=jax.ShapeDtypeStruct(o_shape, o_dtype),
        scratch_shapes=[] if nred == 1 else [pltpu.VMEM(tile, F32)],
        compiler_params=_params("parallel", "parallel", "arbitrary"),
    )(a, b)


def _mm(a, b, mode, out_dtype, name):
    if mode == "nn":
        (m, k), (k2, n) = a.shape, b.shape
    elif mode == "nt":
        (m, k), (n, k2) = a.shape, b.shape
    else:
        (k, m), (k2, n) = a.shape, b.shape
    assert k == k2, (a.shape, b.shape, mode)
    tm, tn = _pick(m, M_TILES), _pick(n, N_TILES)
    tk = k if k <= K_WHOLE else _pick(k, K_TILES)
    if mode == "nn":
        a_spec = pl.BlockSpec((tm, tk), lambda i, j, kk: (i, kk))
        b_spec = pl.BlockSpec((tk, tn), lambda i, j, kk: (kk, j))
        dims = NN
    elif mode == "nt":
        a_spec = pl.BlockSpec((tm, tk), lambda i, j, kk: (i, kk))
        b_spec = pl.BlockSpec((tn, tk), lambda i, j, kk: (j, kk))
        dims = NT
    else:
        a_spec = pl.BlockSpec((tk, tm), lambda i, j, kk: (kk, i))
        b_spec = pl.BlockSpec((tk, tn), lambda i, j, kk: (kk, j))
        dims = TN
    return _mm_core(name, (m // tm, n // tn, k // tk), a, b, a_spec, b_spec,
                    pl.BlockSpec((tm, tn), lambda i, j, kk: (i, j)), (m, n), out_dtype, dims, (tm, tn))


def _act_spec(blocked, rows, ns, row_ax, d_ax):
    if blocked:
        return pl.BlockSpec((None, rows, ns), lambda *g: (g[d_ax], g[row_ax], 0))
    return pl.BlockSpec((rows, ns), lambda *g: (g[row_ax], g[d_ax]))


def _mm_cols_fwd(a, g3, blocked, out_dtype, name):
    (t, k), (nd, k2, ns) = a.shape, g3.shape
    assert k == k2
    tm, tk = _pick(t, M_TILES), _pick(k, K_TILES)
    return _mm_core(name, (t // tm, nd, k // tk), a, g3,
                    pl.BlockSpec((tm, tk), lambda i, d, kk: (i, kk)),
                    pl.BlockSpec((None, tk, ns), lambda i, d, kk: (d, kk, 0)),
                    _act_spec(blocked, tm, ns, 0, 1), (nd, t, ns) if blocked else (t, nd * ns), out_dtype, NN, (tm, ns))


def _mm_cols_dx(dy, g3, blocked, out_dtype, name):
    nd, k, ns = g3.shape
    t = dy.shape[1] if blocked else dy.shape[0]
    tm, tn = _pick(t, M_TILES), _pick(k, (1024,) + N_TILES)
    o_spec = pl.BlockSpec((tm, tn), lambda i, j, d: (i, j))
    if not blocked:
        whole_b = lambda r: jnp.concatenate([r[s] for s in range(nd)], axis=1)
        return _mm_core(name, (t // tm, k // tn, 1), dy, g3,
                        pl.BlockSpec((tm, nd * ns), lambda i, j, d: (i, 0)),
                        pl.BlockSpec((nd, tn, ns), lambda i, j, d: (0, j, 0)),
                        o_spec, (t, k), out_dtype, NT, (tm, tn), pieces=[(lambda r: r[...], whole_b)])
    grp = 2 if nd % 2 == 0 else 1
    pieces = [(lambda r, s=s: r[s], lambda r, s=s: r[s]) for s in range(grp)]
    return _mm_core(name, (t // tm, k // tn, nd // grp), dy, g3,
                    pl.BlockSpec((grp, tm, ns), lambda i, j, d: (d, i, 0)),
                    pl.BlockSpec((grp, tn, ns), lambda i, j, d: (d, j, 0)),
                    o_spec, (t, k), out_dtype, NT, (tm, tn), pieces=pieces)


def _mm_cols_dw(a, dy, nd, blocked, out_dtype, name):
    t, k = a.shape
    ns = dy.shape[2] if blocked else dy.shape[1] // nd
    tmk, tk = _pick(k, M_TILES), _pick(t, K_TILES)
    return _mm_core(name, (nd, k // tmk, t // tk), a, dy,
                    pl.BlockSpec((tk, tmk), lambda d, j, kk: (kk, j)),
                    _act_spec(blocked, tk, ns, 2, 0),
                    pl.BlockSpec((None, tmk, ns), lambda d, j, kk: (d, j, 0)), (nd, k, ns), out_dtype, TN, (tmk, ns))


def _mm_blk_fwd(h3, w, out_dtype, name):
    (nb, t, ns), (_, n) = h3.shape, w.shape
    tm, tn = _pick(t, M_TILES), _pick(n, (1024,) + N_TILES)
    pieces = [(lambda r, s=s: r[s], lambda r, s=s: r[s * ns:(s + 1) * ns, :]) for s in range(nb)]
    return _mm_core(name, (t // tm, n // tn, 1), h3, w,
                    pl.BlockSpec((nb, tm, ns), lambda i, j, d: (0, i, 0)),
                    pl.BlockSpec((nb * ns, tn), lambda i, j, d: (0, j)),
                    pl.BlockSpec((tm, tn), lambda i, j, d: (i, j)), (t, n), out_dtype, NN, (tm, tn), pieces=pieces)


def _mm_blk_dw(h3, dz, out_dtype, name):
    (nb, t, ns), (_, n) = h3.shape, dz.shape
    tn, tk = _pick(n, (1024,) + N_TILES), _pick(t, K_TILES)
    return _mm_core(name, (nb, n // tn, t // tk), h3, dz,
                    pl.BlockSpec((None, tk, ns), lambda d, j, kk: (d, kk, 0)),
                    pl.BlockSpec((tk, tn), lambda d, j, kk: (kk, j)),
                    pl.BlockSpec((ns, tn), lambda d, j, kk: (d, j)), (nb * ns, n), out_dtype, TN, (ns, tn))


def _ln_fwd(xa, xb, g, b, name):
    t, d = xa.shape
    tb = _pick(t, ROW_TILES)

    def body(xa_ref, xb_ref, g_ref, b_ref, y_ref, yb_ref):
        z = ALPHA * xa_ref[...] + xb_ref[...]
        mu = jnp.mean(z, axis=-1, keepdims=True)
        zc = z - mu
        var = jnp.mean(zc * zc, axis=-1, keepdims=True)
        y = zc * lax.rsqrt(var + LN_EPS) * g_ref[...] + b_ref[...]
        y_ref[...] = y
        yb_ref[...] = y.astype(BF16)

    row = pl.BlockSpec((tb, d), lambda i: (i, 0))
    vec = pl.BlockSpec((1, d), lambda i: (0, 0))
    return pl.pallas_call(
        body, name=name, grid=(t // tb,),
        in_specs=[row, row, vec, vec], out_specs=[row, row],
        out_shape=[jax.ShapeDtypeStruct((t, d), F32), jax.ShapeDtypeStruct((t, d), BF16)],
        compiler_params=_params("parallel"),
    )(xa, xb, g.reshape(1, d), b.reshape(1, d))


def _ln_bwd(xa, xb, g, dya, ca, dyb, name):
    t, d = xa.shape
    tb = _pick(t, ROW_TILES)
    two = dyb is not None

    def body(*refs):
        if two:
            xa_ref, xb_ref, g_ref, dya_ref, dyb_ref, dz_ref, dzb_ref, dg_ref, db_ref = refs
            dy = ca * dya_ref[...] + dyb_ref[...]
        else:
            xa_ref, xb_ref, g_ref, dya_ref, dz_ref, dzb_ref, dg_ref, db_ref = refs
            dy = ca * dya_ref[...]
        z = ALPHA * xa_ref[...] + xb_ref[...]
        mu = jnp.mean(z, axis=-1, keepdims=True)
        zc = z - mu
        var = jnp.mean(zc * zc, axis=-1, keepdims=True)
        rstd = lax.rsqrt(var + LN_EPS)
        xhat = zc * rstd
        dxh = dy * g_ref[...]
        m1 = jnp.mean(dxh, axis=-1, keepdims=True)
        m2 = jnp.mean(dxh * xhat, axis=-1, keepdims=True)
        dz = rstd * (dxh - m1 - xhat * m2)
        dz_ref[...] = dz
        dzb_ref[...] = dz.astype(BF16)

        @pl.when(pl.program_id(0) == 0)
        def _():
            dg_ref[...] = jnp.zeros_like(dg_ref)
            db_ref[...] = jnp.zeros_like(db_ref)

        dg_ref[...] += jnp.sum(dy * xhat, axis=0, keepdims=True)
        db_ref[...] += jnp.sum(dy, axis=0, keepdims=True)

    row = pl.BlockSpec((tb, d), lambda i: (i, 0))
    vec = pl.BlockSpec((1, d), lambda i: (0, 0))
    ins = [xa, xb, g.reshape(1, d), dya] + ([dyb] if two else [])
    dz, dzb, dg, db = pl.pallas_call(
        body, name=name, grid=(t // tb,),
        in_specs=[row, row, vec, row] + ([row] if two else []),
        out_specs=[row, row, vec, vec],
        out_shape=[jax.ShapeDtypeStruct((t, d), F32), jax.ShapeDtypeStruct((t, d), BF16),
                   jax.ShapeDtypeStruct((1, d), F32), jax.ShapeDtypeStruct((1, d), F32)],
        compiler_params=_params("arbitrary"),
    )(*ins)
    return dz, dzb, dg[0], db[0]


def _loss(y, target, name):
    t, d = y.shape
    tb = _pick(t, ROW_TILES)

    def body(y_ref, t_ref, dy_ref, l_ref):
        e = y_ref[...] - t_ref[...]
        dy_ref[...] = e * (1.0 / d)

        @pl.when(pl.program_id(0) == 0)
        def _():
            l_ref[...] = jnp.zeros_like(l_ref)

        l_ref[...] += 0.5 * jnp.sum(jnp.mean(e * e, axis=-1, keepdims=True))

    row = pl.BlockSpec((tb, d), lambda i: (i, 0))
    dy, l = pl.pallas_call(
        body, name=name, grid=(t // tb,),
        in_specs=[row, row], out_specs=[row, pl.BlockSpec((1, LANES), lambda i: (0, 0))],
        out_shape=[jax.ShapeDtypeStruct((t, d), F32), jax.ShapeDtypeStruct((1, LANES), F32)],
        compiler_params=_params("arbitrary"),
    )(y, target)
    return l[0, 0], dy


def _axpy(ca, a, b, name):
    t, d = a.shape
    tb = _pick(t, ROW_TILES)

    def body(a_ref, b_ref, o_ref):
        o_ref[...] = ca * a_ref[...] + b_ref[...]

    row = pl.BlockSpec((tb, d), lambda i: (i, 0))
    return pl.pallas_call(
        body, name=name, grid=(t // tb,), in_specs=[row, row], out_specs=row,
        out_shape=jax.ShapeDtypeStruct((t, d), F32), compiler_params=_params("parallel"),
    )(a, b)


def _ffn_in_swiglu(xb, g4, name):
    (t, k), (_, nb, _, ns) = xb.shape, g4.shape
    tm = _pick(t, M_TILES)

    def body(x_ref, w_ref, h_ref, gu_ref):
        xv = x_ref[...]
        gate = jnp.dot(xv, w_ref[0], preferred_element_type=F32)
        up = jnp.dot(xv, w_ref[1], preferred_element_type=F32)
        h_ref[...] = (gate * jax.nn.sigmoid(gate) * up).astype(BF16)
        gu_ref[0] = gate.astype(BF16)
        gu_ref[1] = up.astype(BF16)

    return pl.pallas_call(
        body, name=name, grid=(t // tm, nb),
        in_specs=[pl.BlockSpec((tm, k), lambda i, d: (i, 0)),
                  pl.BlockSpec((2, None, k, ns), lambda i, d: (0, d, 0, 0))],
        out_specs=[pl.BlockSpec((None, tm, ns), lambda i, d: (d, i, 0)),
                   pl.BlockSpec((2, None, tm, ns), lambda i, d: (0, d, i, 0))],
        out_shape=[jax.ShapeDtypeStruct((nb, t, ns), BF16), jax.ShapeDtypeStruct((2, nb, t, ns), BF16)],
        compiler_params=_params("parallel", "parallel"),
    )(xb, g4)


def _ffn_out_dx_swiglu(dz, w_out, gu4, name):
    (t, d), (_, nb, _, ns) = dz.shape, gu4.shape
    tm = _pick(t, M_TILES)

    def body(dz_ref, w_ref, gu_ref, o_ref):
        dh = lax.dot_general(dz_ref[...].astype(BF16), w_ref[...], NT, preferred_element_type=F32)
        gate = gu_ref[0].astype(F32)
        up = gu_ref[1].astype(F32)
        sg = jax.nn.sigmoid(gate)
        silu = gate * sg
        o_ref[0] = (dh * up * (sg + silu * (1.0 - sg))).astype(BF16)
        o_ref[1] = (dh * silu).astype(BF16)

    blk = pl.BlockSpec((2, None, tm, ns), lambda i, j: (0, j, i, 0))
    return pl.pallas_call(
        body, name=name, grid=(t // tm, nb),
        in_specs=[pl.BlockSpec((tm, d), lambda i, j: (i, 0)), pl.BlockSpec((ns, d), lambda i, j: (j, 0)), blk],
        out_specs=blk,
        out_shape=jax.ShapeDtypeStruct((2, nb, t, ns), BF16),
        compiler_params=_params("parallel", "parallel"),
    )(dz, w_out, gu4)


def _tri_matmul(tri, x):
    x1 = x.astype(BF16)
    r1 = x - x1.astype(F32)
    x2 = r1.astype(BF16)
    x3 = (r1 - x2.astype(F32)).astype(BF16)
    dot = lambda v: jnp.dot(tri, v, preferred_element_type=F32)
    return dot(x1) + dot(x2) + dot(x3)


def _fgate_fwd(proj, bf_pad, fcol, n_heads, name):
    t = proj.shape[0]
    tb = _pick(t, ROW_TILES)

    def body(p_ref, b_ref, c_ref, carry):
        @pl.when(pl.program_id(0) == 0)
        def _():
            carry[...] = jnp.zeros_like(carry)

        z = p_ref[...] + b_ref[...]
        lf = jnp.minimum(z, 0.0) - jnp.log1p(jnp.exp(-jnp.abs(z)))
        lane = lax.broadcasted_iota(jnp.int32, (tb, LANES), 1)
        lf = jnp.where(lane < n_heads, lf, 0.0)
        r = lax.broadcasted_iota(jnp.int32, (tb, tb), 0)
        s = lax.broadcasted_iota(jnp.int32, (tb, tb), 1)
        tri = (s <= r).astype(BF16)
        c = _tri_matmul(tri, lf) + carry[...]
        c_ref[...] = c
        carry[...] = c[tb - 1:tb, :]

    return pl.pallas_call(
        body, name=name, grid=(t // tb,),
        in_specs=[pl.BlockSpec((tb, LANES), lambda i: (i, fcol)), pl.BlockSpec((1, LANES), lambda i: (0, 0))],
        out_specs=pl.BlockSpec((tb, LANES), lambda i: (i, 0)),
        out_shape=jax.ShapeDtypeStruct((t, LANES), F32),
        scratch_shapes=[pltpu.VMEM((1, LANES), F32)],
        compiler_params=_params("arbitrary"),
    )(proj, bf_pad)


def _fgate_bwd(proj, bf_pad, dcq, dck, fcol, n_heads, name):
    t = proj.shape[0]
    tb = _pick(t, ROW_TILES)
    nb = t // tb

    def body(p_ref, b_ref, dcq_ref, dck_ref, dz_ref, db_ref, carry):
        @pl.when(pl.program_id(0) == 0)
        def _():
            carry[...] = jnp.zeros_like(carry)
            db_ref[...] = jnp.zeros_like(db_ref)

        r = lax.broadcasted_iota(jnp.int32, (tb, tb), 0)
        s = lax.broadcasted_iota(jnp.int32, (tb, tb), 1)
        tri = (s >= r).astype(BF16)
        dlf = _tri_matmul(tri, dcq_ref[...] + dck_ref[...]) + carry[...]
        carry[...] = dlf[0:1, :]
        z = p_ref[...] + b_ref[...]
        lane = lax.broadcasted_iota(jnp.int32, (tb, LANES), 1)
        dz = jnp.where(lane < n_heads, dlf * jax.nn.sigmoid(-z), 0.0)
        dz_ref[...] = dz
        db_ref[...] += jnp.sum(dz, axis=0, keepdims=True)

    dz, db = pl.pallas_call(
        body, name=name, grid=(nb,),
        in_specs=[pl.BlockSpec((tb, LANES), lambda i: (nb - 1 - i, fcol)),
                  pl.BlockSpec((1, LANES), lambda i: (0, 0)),
                  pl.BlockSpec((tb, LANES), lambda i: (nb - 1 - i, 0)),
                  pl.BlockSpec((tb, LANES), lambda i: (nb - 1 - i, 0))],
        out_specs=[pl.BlockSpec((tb, LANES), lambda i: (nb - 1 - i, 0)),
                   pl.BlockSpec((1, LANES), lambda i: (0, 0))],
        out_shape=[jax.ShapeDtypeStruct((t, LANES), F32), jax.ShapeDtypeStruct((1, LANES), F32)],
        scratch_shapes=[pltpu.VMEM((1, LANES), F32)],
        compiler_params=_params("arbitrary"),
    )(proj, bf_pad, dcq, dck)
    return dz, db[0]


def _split3(x):
    hi = x.astype(BF16)
    r = x - hi.astype(F32)
    mid = r.astype(BF16)
    return hi, mid, (r - mid.astype(F32)).astype(BF16)


def _attn_fwd(qa, ka, va, dh, name, gather=()):
    nh, t, da = qa.shape
    tq = _pick(t, ROW_TILES)
    hb = 2 if nh % 2 == 0 else 1
    heads = range(hb)
    n = len(gather)
    steps = (nh // hb, t // tq)

    def body(q_ref, k_ref, v_ref, *rest):
        x_refs, o_ref, g_refs = rest[:n], rest[n], rest[n + 1:2 * n + 1]
        m_s, acc_s, s_a, s_b = rest[2 * n + 1:2 * n + 5]
        qi = pl.program_id(1)
        if n:
            start, forward, finish = _gather_phases(x_refs, g_refs, *rest[2 * n + 5:])
            at = lambda hh, qq: jnp.logical_and(pl.program_id(0) == hh, qi == qq)
            pl.when(at(0, 0))(start)
            pl.when(at(steps[0] // 2, 0))(forward)
        m_s[...] = jnp.full(m_s.shape, NEG_BIG, F32)
        acc_s[...] = jnp.zeros_like(acc_s)

        def scores(s_ref, j):
            off = pl.multiple_of(j * tq, tq)
            for g in heads:
                s_ref[g] = lax.dot_general(q_ref[g], k_ref[g, pl.ds(off, tq), :], NT, preferred_element_type=F32)

        def absorb(s_ref, j, diagonal):
            off = pl.multiple_of(j * tq, tq)
            s = [s_ref[g] for g in heads]
            if diagonal:
                row = lax.broadcasted_iota(jnp.int32, (tq, tq), 0)
                col = lax.broadcasted_iota(jnp.int32, (tq, tq), 1)
                s = [jnp.where(col > row, NEG_BIG, sg) for sg in s]
            m_prev = [m_s[g] for g in heads]
            m_new = [jnp.maximum(m_prev[g], jnp.max(s[g], axis=1, keepdims=True)) for g in heads]
            p = [jnp.exp(s[g] - m_new[g]).astype(BF16) for g in heads]
            pv = [jnp.dot(p[g], v_ref[g, pl.ds(off, tq), :], preferred_element_type=F32) for g in heads]
            for g in heads:
                acc_s[g] = jnp.exp(m_prev[g] - m_new[g]) * acc_s[g] + pv[g]
                m_s[g] = m_new[g]

        def two_blocks(r, carry):
            scores(s_b, 2 * r + 1)
            absorb(s_a, 2 * r, False)
            scores(s_a, 2 * r + 2)
            absorb(s_b, 2 * r + 1, False)
            return carry

        scores(s_a, 0)
        rounds = qi // 2
        lax.fori_loop(0, rounds, two_blocks, 0)

        @pl.when(qi % 2 == 0)
        def _():
            absorb(s_a, qi, True)

        @pl.when(qi % 2 == 1)
        def _():
            scores(s_b, qi)
            absorb(s_a, qi - 1, False)
            absorb(s_b, qi, True)

        lane = lax.broadcasted_iota(jnp.int32, (tq, da), 1)
        for g in heads:
            acc = acc_s[g]
            l = jnp.sum(jnp.where(lane == dh, acc, 0.0), axis=1, keepdims=True)
            o_ref[g] = jnp.where(lane == dh, m_s[g] + jnp.log(l), acc / l)
        if n:
            pl.when(at(steps[0] - 1, steps[1] - 1))(finish)

    full = pl.BlockSpec((hb, t, da), lambda h, qi: (h, 0, 0))
    blk = pl.BlockSpec((hb, tq, da), lambda h, qi: (h, qi, 0))
    return pl.pallas_call(
        body, name=name, grid=steps,
        in_specs=[blk, full, full] + [ANY] * n, out_specs=[blk] + [ANY] * n,
        out_shape=[jax.ShapeDtypeStruct((nh, t, da), F32)] + _gather_shapes(gather),
        scratch_shapes=[pltpu.VMEM((hb, tq, 1), F32), pltpu.VMEM((hb, tq, da), F32),
                        pltpu.VMEM((hb, tq, tq), F32), pltpu.VMEM((hb, tq, tq), F32)] + (_gather_sems(n) if n else []),
        compiler_params=_params("arbitrary", "arbitrary"),
    )(qa, ka, va, *gather)


def _attn_bwd(qa, ka, va, doa, name, exchange=()):
    nh, t, da = qa.shape
    tq = _pick(t, ROW_TILES)
    nq = t // tq
    n = len(exchange)

    def body(q_ref, do_ref, k_ref, v_ref, *rest):
        p_refs, (dq_ref, dk_ref, dv_ref), r_refs = rest[:n], rest[n:n + 3], rest[n + 3:2 * n + 3]
        kj = pl.program_id(1)
        if n:
            start, finish = _chip_exchange_phases(p_refs, r_refs, *rest[2 * n + 3:])
            pl.when(jnp.logical_and(pl.program_id(0) == 0, kj == 0))(start)

        @pl.when(kj == 0)
        def _():
            dq_ref[...] = jnp.zeros_like(dq_ref)

        dk_ref[...] = jnp.zeros_like(dk_ref)
        dv_ref[...] = jnp.zeros_like(dv_ref)
        kb = k_ref[...]
        vb = v_ref[...]

        def step(i, diagonal):
            off = pl.multiple_of(i * tq, tq)
            qb = q_ref[pl.ds(off, tq), :]
            dob = do_ref[pl.ds(off, tq), :]
            st = lax.dot_general(kb, qb, NT, preferred_element_type=F32)
            if diagonal:
                row = lax.broadcasted_iota(jnp.int32, (tq, tq), 0)
                col = lax.broadcasted_iota(jnp.int32, (tq, tq), 1)
                st = jnp.where(row > col, NEG_BIG, st)
            pt = jnp.exp(st)
            dst = (pt * lax.dot_general(vb, dob, NT, preferred_element_type=F32)).astype(BF16)
            dv_ref[...] += jnp.dot(pt.astype(BF16), dob, preferred_element_type=F32)
            dk_ref[...] += jnp.dot(dst, qb, preferred_element_type=F32)
            dq_ref[pl.ds(off, tq), :] += lax.dot_general(dst, kb, TN, preferred_element_type=F32)

        def loop(i, carry):
            step(i, False)
            return carry

        step(kj, True)
        lax.fori_loop(kj + 1, nq, loop, 0)
        if n:
            pl.when(jnp.logical_and(pl.program_id(0) == nh - 1, kj == nq - 1))(finish)

    full = pl.BlockSpec((None, t, da), lambda h, j: (h, 0, 0))
    blk = pl.BlockSpec((None, tq, da), lambda h, j: (h, j, 0))
    return pl.pallas_call(
        body, name=name, grid=(nh, nq),
        in_specs=[full, full, blk, blk] + [ANY] * n, out_specs=[full, blk, blk] + [ANY] * n,
        out_shape=[jax.ShapeDtypeStruct((nh, t, da), F32)] * 3 + _chip_exchange_shapes(exchange),
        scratch_shapes=_chip_exchange_sems(n) if n else [],
        compiler_params=_params("arbitrary", "arbitrary"),
    )(qa, doa, ka, va, *exchange)


def _head_group(dh, h):
    g = h // (LANES // dh)
    return slice(g * LANES, (g + 1) * LANES)


def _head_select(dh, h, to_heads):
    r = lax.broadcasted_iota(jnp.int32, (LANES, LANES), 0)
    c = lax.broadcasted_iota(jnp.int32, (LANES, LANES), 1)
    nat, col = (r, c) if to_heads else (c, r)
    return jnp.logical_and(nat == col + (h % (LANES // dh)) * dh, col < dh).astype(BF16)


def _column(x, lane, j):
    return jnp.sum(jnp.where(lane == j, x, 0.0), axis=1, keepdims=True)


def _bias_columns(lane, first, value):
    out = jnp.zeros(lane.shape, F32)
    for j, term in enumerate(_split3(value)):
        out = out + jnp.where(lane == first + j, -term.astype(F32), 0.0)
    return out


def _attn_pack(proj, cgate, w, nh, scale, name):
    t = proj.shape[0]
    dh = w // nh
    tb = _pick(t, ROW_TILES)

    def body(q_ref, k_ref, v_ref, c_ref, qa_ref, ka_ref, va_ref):
        lane = lax.broadcasted_iota(jnp.int32, (tb, LANES), 1)
        ones_qv = jnp.where(jnp.logical_and(lane >= dh, lane < dh + 3), 1.0, 0.0)
        ones_k = jnp.where(jnp.logical_and(lane >= dh + 3, lane < dh + 7), 1.0, 0.0)
        qb = (q_ref[...] * scale).astype(BF16)
        kb = k_ref[...].astype(BF16)
        vb = v_ref[...].astype(BF16)
        cblk = c_ref[...]
        for h in range(nh):
            sel, grp = _head_select(dh, h, True), _head_group(dh, h)
            qa_ref[h] = (jnp.dot(qb[:, grp], sel, preferred_element_type=F32) + ones_qv).astype(BF16)
            va_ref[h] = (jnp.dot(vb[:, grp], sel, preferred_element_type=F32) + ones_qv).astype(BF16)
            bias = _bias_columns(lane, dh, _column(cblk, lane, h))
            ka_ref[h] = (jnp.dot(kb[:, grp], sel, preferred_element_type=F32) + bias + ones_k).astype(BF16)

    col = lambda j: pl.BlockSpec((tb, w), lambda i: (i, j))
    out = pl.BlockSpec((nh, tb, LANES), lambda i: (0, i, 0))
    return pl.pallas_call(
        body, name=name, grid=(t // tb,),
        in_specs=[col(0), col(1), col(2), pl.BlockSpec((tb, LANES), lambda i: (i, 0))],
        out_specs=[out, out, out],
        out_shape=[jax.ShapeDtypeStruct((nh, t, LANES), BF16)] * 3,
        compiler_params=_params("parallel"),
    )(proj, proj, proj, cgate)


def _attn_pack_bwd(dmix, oa, qa, w, nh, name):
    t = dmix.shape[0]
    dh = w // nh
    tb = _pick(t, ROW_TILES)

    def body(d_ref, oa_ref, qa_ref, doa_ref, qa2_ref):
        lane = lax.broadcasted_iota(jnp.int32, (tb, LANES), 1)
        db = d_ref[...].astype(BF16)
        for h in range(nh):
            do_h = jnp.dot(db[:, _head_group(dh, h)], _head_select(dh, h, True), preferred_element_type=F32)
            o_h = oa_ref[h]
            delta = jnp.sum(jnp.where(lane < dh, do_h * o_h, 0.0), axis=1, keepdims=True)
            doa_ref[h] = (do_h + _bias_columns(lane, dh, delta)).astype(BF16)
            qa2_ref[h] = (qa_ref[h].astype(F32) + _bias_columns(lane, dh + 4, _column(o_h, lane, dh))).astype(BF16)

    blk = pl.BlockSpec((nh, tb, LANES), lambda i: (0, i, 0))
    return pl.pallas_call(
        body, name=name, grid=(t // tb,),
        in_specs=[pl.BlockSpec((tb, w), lambda i: (i, 0)), blk, blk], out_specs=[blk, blk],
        out_shape=[jax.ShapeDtypeStruct((nh, t, LANES), BF16)] * 2,
        compiler_params=_params("parallel"),
    )(dmix, oa, qa)


def _attn_unpack(xa, w, nh, mult, sum_col, sum_sign, name):
    t = xa.shape[1]
    dh = w // nh
    tb = _pick(t, ROW_TILES)

    def body(x_ref, o_ref, *rest):
        lane = lax.broadcasted_iota(jnp.int32, (tb, LANES), 1)
        per = LANES // dh
        cols = jnp.zeros((tb, LANES), F32)
        for h0 in range(0, nh, per):
            acc = jnp.zeros((tb, LANES), F32)
            for h in range(h0, h0 + per):
                xh = x_ref[h]
                acc = acc + jnp.dot((xh * mult).astype(BF16), _head_select(dh, h, False), preferred_element_type=F32)
                if sum_col is not None:
                    cols = cols + jnp.where(lane == h, sum_sign * _column(xh, lane, sum_col), 0.0)
            o_ref[:, _head_group(dh, h0)] = acc.astype(BF16)
        if sum_col is not None:
            rest[0][...] = cols

    nat = pl.BlockSpec((tb, w), lambda i: (i, 0))
    lanes = pl.BlockSpec((tb, LANES), lambda i: (i, 0))
    return pl.pallas_call(
        body, name=name, grid=(t // tb,),
        in_specs=[pl.BlockSpec((nh, tb, LANES), lambda i: (0, i, 0))],
        out_specs=[nat, lanes] if sum_col is not None else [nat],
        out_shape=[jax.ShapeDtypeStruct((t, w), BF16)] + ([jax.ShapeDtypeStruct((t, LANES), F32)]
                                                            if sum_col is not None else []),
        compiler_params=_params("parallel"),
    )(xa)


def _conv_fwd(proj, cw, w, bcol, name):
    t = proj.shape[0]
    tb = _pick(t, ROW_TILES)
    hb = tb // SUBLANES

    def body(b_ref, c_ref, h_ref, cp_ref, hp_ref, w_ref, y_ref):
        i = pl.program_id(0)
        zp = jnp.where(i > 0, cp_ref[...] * hp_ref[...], 0.0)
        zext = jnp.concatenate([zp, c_ref[...] * h_ref[...]], axis=0)
        z1 = pltpu.roll(zext, 1, 0)[SUBLANES:]
        z2 = pltpu.roll(zext, 2, 0)[SUBLANES:]
        y = w_ref[2:3, :] * zext[SUBLANES:] + w_ref[1:2, :] * z1 + w_ref[0:1, :] * z2
        y_ref[...] = (b_ref[...] * y).astype(BF16)

    cur = lambda j: pl.BlockSpec((tb, w), lambda i: (i, bcol + j))
    prev = lambda j: pl.BlockSpec((SUBLANES, w), lambda i: (jnp.maximum(i * hb - 1, 0), bcol + j))
    return pl.pallas_call(
        body, name=name, grid=(t // tb,),
        in_specs=[cur(0), cur(1), cur(2), prev(1), prev(2), pl.BlockSpec(cw.shape, lambda i: (0, 0))],
        out_specs=pl.BlockSpec((tb, w), lambda i: (i, 0)),
        out_shape=jax.ShapeDtypeStruct((t, w), BF16), compiler_params=_params("parallel"),
    )(proj, proj, proj, proj, proj, cw)


def _conv_bwd(proj, cw, dmix, w, bcol, name):
    t = proj.shape[0]
    tb = _pick(t, ROW_TILES)
    hb = tb // SUBLANES
    nb = t // tb
    n_ext = tb + SUBLANES

    def body(b_ref, c_ref, h_ref, cp_ref, hp_ref, bn_ref, d_ref, dn_ref, w_ref, db_ref, dc_ref, dh_ref, dw_ref):
        i = pl.program_id(0)
        c = c_ref[...]
        hh = h_ref[...]
        zp = jnp.where(i > 0, cp_ref[...] * hp_ref[...], 0.0)
        zext = jnp.concatenate([zp, c * hh], axis=0)
        z0 = zext[SUBLANES:]
        z1 = pltpu.roll(zext, 1, 0)[SUBLANES:]
        z2 = pltpu.roll(zext, 2, 0)[SUBLANES:]
        y = w_ref[2:3, :] * z0 + w_ref[1:2, :] * z1 + w_ref[0:1, :] * z2
        d = d_ref[...]
        db_ref[...] = d * y
        dy = d * b_ref[...]
        dyn = jnp.where(i < nb - 1, dn_ref[...] * bn_ref[...], 0.0)
        dext = jnp.concatenate([dy, dyn], axis=0)
        dy1 = pltpu.roll(dext, n_ext - 1, 0)[:tb]
        dy2 = pltpu.roll(dext, n_ext - 2, 0)[:tb]
        dz = w_ref[2:3, :] * dy + w_ref[1:2, :] * dy1 + w_ref[0:1, :] * dy2
        dc_ref[...] = dz * hh
        dh_ref[...] = dz * c

        @pl.when(i == 0)
        def _():
            dw_ref[...] = jnp.zeros_like(dw_ref)

        dw_ref[0:1, :] += jnp.sum(dy * z2, axis=0, keepdims=True)
        dw_ref[1:2, :] += jnp.sum(dy * z1, axis=0, keepdims=True)
        dw_ref[2:3, :] += jnp.sum(dy * z0, axis=0, keepdims=True)

    cur = lambda j: pl.BlockSpec((tb, w), lambda i: (i, bcol + j))
    prev = lambda j: pl.BlockSpec((SUBLANES, w), lambda i: (jnp.maximum(i * hb - 1, 0), bcol + j))
    nxt = lambda col: pl.BlockSpec((SUBLANES, w), lambda i: (jnp.minimum((i + 1) * hb, nb * hb - 1), col))
    out = pl.BlockSpec((tb, w), lambda i: (i, 0))
    return pl.pallas_call(
        body, name=name, grid=(nb,),
        in_specs=[cur(0), cur(1), cur(2), prev(1), prev(2), nxt(bcol),
                  pl.BlockSpec((tb, w), lambda i: (i, 1)), nxt(1), pl.BlockSpec(cw.shape, lambda i: (0, 0))],
        out_specs=[out, out, out, pl.BlockSpec(cw.shape, lambda i: (0, 0))],
        out_shape=[jax.ShapeDtypeStruct((t, w), F32)] * 3 + [jax.ShapeDtypeStruct(cw.shape, F32)],
        compiler_params=_params("arbitrary"),
    )(proj, proj, proj, proj, proj, proj, dmix, dmix, cw)


SQRT_HALF = 0.7071067811865476
INV_SQRT_2PI = 0.3989422804014327


def _gelu(x):
    return 0.5 * x * (1.0 + lax.erf(x * SQRT_HALF))


def _gelu_grad(x):
    return 0.5 * (1.0 + lax.erf(x * SQRT_HALF)) + x * (INV_SQRT_2PI * jnp.exp(-0.5 * x * x))


def _sgu_fwd(uv, ln_g, ln_b, wm, bs_full, name):
    t, d2 = uv.shape
    d = d2 // 2
    ng, pb, _ = wm.shape
    gd = d // ng
    tb = _pick(t, ROW_TILES[1:] or ROW_TILES)
    assert tb % pb == 0

    def body(uv_ref, g_ref, b_ref, w_ref, bs_ref, o_ref):
        u = _gelu(uv_ref[:, :d])
        v = _gelu(uv_ref[:, d:])
        mu = jnp.mean(v, axis=-1, keepdims=True)
        vc = v - mu
        var = jnp.mean(vc * vc, axis=-1, keepdims=True)
        vn = (vc * lax.rsqrt(var + LN_EPS) * g_ref[...] + b_ref[...]).astype(BF16)
        for r in range(tb // pb):
            rows = slice(r * pb, (r + 1) * pb)
            for gi in range(ng):
                cols = slice(gi * gd, (gi + 1) * gd)
                s = jnp.dot(w_ref[gi], vn[rows, cols], preferred_element_type=F32) + bs_ref[:, cols]
                o_ref[rows, cols] = (u[rows, cols] * s).astype(BF16)

    vec = pl.BlockSpec((1, d), lambda i: (0, 0))
    return pl.pallas_call(
        body, name=name, grid=(t // tb,),
        in_specs=[pl.BlockSpec((tb, d2), lambda i: (i, 0)), vec, vec,
                  pl.BlockSpec(wm.shape, lambda i: (0, 0, 0)), pl.BlockSpec((pb, d), lambda i: (0, 0))],
        out_specs=pl.BlockSpec((tb, d), lambda i: (i, 0)),
        out_shape=jax.ShapeDtypeStruct((t, d), BF16), compiler_params=_params("parallel"),
    )(uv, ln_g.reshape(1, d), ln_b.reshape(1, d), wm, bs_full)


def _sgu_bwd(uv, ln_g, ln_b, wm, bs_full, dgated, name):
    t, d2 = uv.shape
    d = d2 // 2
    ng, pb, _ = wm.shape
    gd = d // ng
    tb = _pick(t, ROW_TILES[1:] or ROW_TILES)
    nb = t // tb

    def body(uv_ref, g_ref, b_ref, w_ref, bs_ref, dg_ref, o_ref, dw_ref, dbs_ref, dlg_ref, dlb_ref,
             du_s, dvn_s, dbs_s):
        i = pl.program_id(0)

        @pl.when(i == 0)
        def _():
            dw_ref[...] = jnp.zeros_like(dw_ref)
            dbs_s[...] = jnp.zeros_like(dbs_s)
            dlg_ref[...] = jnp.zeros_like(dlg_ref)
            dlb_ref[...] = jnp.zeros_like(dlb_ref)

        upre = uv_ref[:, :d]
        vpre = uv_ref[:, d:]
        u = _gelu(upre)
        v = _gelu(vpre)
        mu = jnp.mean(v, axis=-1, keepdims=True)
        vc = v - mu
        var = jnp.mean(vc * vc, axis=-1, keepdims=True)
        rstd = lax.rsqrt(var + LN_EPS)
        xhat = vc * rstd
        vn = (xhat * g_ref[...] + b_ref[...]).astype(BF16)
        dgt = dg_ref[...].astype(F32)
        for r in range(tb // pb):
            rows = slice(r * pb, (r + 1) * pb)
            for gi in range(ng):
                cols = slice(gi * gd, (gi + 1) * gd)
                vblk = vn[rows, cols]
                s = jnp.dot(w_ref[gi], vblk, preferred_element_type=F32) + bs_ref[:, cols]
                dblk = dgt[rows, cols]
                du_s[rows, cols] = dblk * s
                ds = dblk * u[rows, cols]
                dsb = ds.astype(BF16)
                dvn_s[rows, cols] = lax.dot_general(w_ref[gi], dsb, (((0,), (0,)), ((), ())),
                                                    preferred_element_type=F32)
                dw_ref[gi] += lax.dot_general(dsb, vblk, (((1,), (1,)), ((), ())), preferred_element_type=F32)
                dbs_s[:, cols] += ds
        dvn = dvn_s[...]
        dlg_ref[...] += jnp.sum(dvn * xhat, axis=0, keepdims=True)
        dlb_ref[...] += jnp.sum(dvn, axis=0, keepdims=True)
        dxh = dvn * g_ref[...]
        m1 = jnp.mean(dxh, axis=-1, keepdims=True)
        m2 = jnp.mean(dxh * xhat, axis=-1, keepdims=True)
        dv = rstd * (dxh - m1 - xhat * m2)
        o_ref[:, :d] = (du_s[...] * _gelu_grad(upre)).astype(BF16)
        o_ref[:, d:] = (dv * _gelu_grad(vpre)).astype(BF16)

        @pl.when(i == nb - 1)
        def _():
            lane = lax.broadcasted_iota(jnp.int32, (pb, LANES), 1)
            acc = jnp.zeros((pb, LANES), F32)
            for gi in range(ng):
                col = jnp.sum(dbs_s[:, gi * gd:(gi + 1) * gd], axis=1, keepdims=True)
                acc = acc + jnp.where(lane == gi, col, 0.0)
            dbs_ref[...] = acc

    vec = pl.BlockSpec((1, d), lambda i: (0, 0))
    duv, dw, dbs, dlg, dlb = pl.pallas_call(
        body, name=name, grid=(nb,),
        in_specs=[pl.BlockSpec((tb, d2), lambda i: (i, 0)), vec, vec,
                  pl.BlockSpec(wm.shape, lambda i: (0, 0, 0)), pl.BlockSpec((pb, d), lambda i: (0, 0)),
                  pl.BlockSpec((tb, d), lambda i: (i, 0))],
        out_specs=[pl.BlockSpec((tb, d2), lambda i: (i, 0)), pl.BlockSpec(wm.shape, lambda i: (0, 0, 0)),
                   pl.BlockSpec((pb, LANES), lambda i: (0, 0)), vec, vec],
        out_shape=[jax.ShapeDtypeStruct((t, d2), BF16), jax.ShapeDtypeStruct(wm.shape, F32),
                   jax.ShapeDtypeStruct((pb, LANES), F32), jax.ShapeDtypeStruct((1, d), F32),
                   jax.ShapeDtypeStruct((1, d), F32)],
        scratch_shapes=[pltpu.VMEM((tb, d), F32), pltpu.VMEM((tb, d), F32), pltpu.VMEM((pb, d), F32)],
        compiler_params=_params("arbitrary"),
    )(uv, ln_g.reshape(1, d), ln_b.reshape(1, d), wm, bs_full, dgated)
    return duv, dw, dbs, dlg[0], dlb[0]


def _adamw(w, g, m, v, name):
    shape = w.shape
    cols = shape[-1]
    rows = w.size // cols
    tr = _pick(rows, (512, 256, 352, 128, 64, 32, 16, 8))

    def body(w_ref, g_ref, m_ref, v_ref, d_ref, mo_ref, vo_ref):
        d_ref[...], mo_ref[...], vo_ref[...] = _adam_update(w_ref[...], g_ref[...], m_ref[...], v_ref[...])

    spec = pl.BlockSpec((tr, cols), lambda i: (i, 0))
    outs = pl.pallas_call(
        body, name=name, grid=(rows // tr,),
        in_specs=[spec] * 4, out_specs=[spec] * 3,
        out_shape=[jax.ShapeDtypeStruct((rows, cols), F32)] * 3,
        compiler_params=_params("parallel"),
    )(*[a.reshape(rows, cols) for a in (w, g, m, v)])
    return [o.reshape(shape) for o in outs]


ANY = pl.BlockSpec(memory_space=pl.ANY)


def _place():
    return lax.axis_index("x"), lax.axis_index("y"), lax.axis_index("c")


def _all_gather(shards, name):
    n = len(shards)

    def body(*refs):
        start, forward, finish = _gather_phases(refs[:n], refs[n:2 * n], *refs[2 * n:])
        start()
        forward()
        finish()

    return pl.pallas_call(
        body, name=name, in_specs=[ANY] * n, out_specs=[ANY] * n,
        out_shape=_gather_shapes(shards), scratch_shapes=_gather_sems(n),
    )(*shards)


def _gather_shapes(shards):
    return [jax.ShapeDtypeStruct((N_DEV,) + s.shape, s.dtype) for s in shards]


def _gather_sems(n):
    return [pltpu.SemaphoreType.DMA((7 * n,)), pltpu.SemaphoreType.DMA((7 * n,)), pltpu.SemaphoreType.DMA((n,))]


def _gather_phases(x_refs, out_refs, send_sems, recv_sems, local_sems):
    n = len(x_refs)
    x, y, c = _place()
    me, sibling = (x, y, c), (x, y, 1 - c)
    chips = [(1 - x, y), (x, 1 - y), (1 - x, 1 - y)]

    def copy(a, k, block, to, own=False):
        px, py, pc = block
        rows = out_refs[a].at[4 * px + 2 * py + pc]
        return pltpu.make_async_remote_copy(
            src_ref=x_refs[a] if own else rows, dst_ref=rows,
            send_sem=send_sems.at[7 * a + k], recv_sem=recv_sems.at[7 * a + k],
            device_id=to, device_id_type=MESH)

    def local(a):
        return pltpu.make_async_copy(x_refs[a], out_refs[a].at[4 * x + 2 * y + c], local_sems.at[a])

    def first(a):
        return [copy(a, 0, me, sibling, own=True)] + [copy(a, 1 + j, me, (*chip, c), own=True)
                                                      for j, chip in enumerate(chips)]

    def start():
        for a in range(n):
            local(a).start()
            for cp in first(a):
                cp.start()

    def forward():
        for j, chip in enumerate(chips):
            for a in range(n):
                copy(a, 1 + j, (*chip, c), me).wait_recv()
                copy(a, 4 + j, (*chip, c), sibling).start()

    def finish():
        for a in range(n):
            copy(a, 0, sibling, me).wait_recv()
            for j, chip in enumerate(chips):
                copy(a, 4 + j, (*chip, 1 - c), me).wait_recv()
        for a in range(n):
            for cp in first(a) + [copy(a, 4 + j, (*chip, c), sibling) for j, chip in enumerate(chips)]:
                cp.wait_send()
            local(a).wait()

    return start, forward, finish


def _rs_sibling_exchange(packed, name):
    n = len(packed)

    def body(*refs):
        p_refs, r_refs = refs[:n], refs[n:2 * n]
        send_sems, recv_sems = refs[2 * n:]
        x, y, c = _place()
        cps = []
        for a in range(n):
            for j in range(4):
                cps.append(pltpu.make_async_remote_copy(
                    src_ref=p_refs[a].at[2 * j + (1 - c)], dst_ref=r_refs[a].at[j],
                    send_sem=send_sems.at[4 * a + j], recv_sem=recv_sems.at[4 * a + j],
                    device_id=(x, y, 1 - c), device_id_type=MESH))
        for cp in cps:
            cp.start()
        for cp in cps:
            cp.wait()

    return pl.pallas_call(
        body, name=name, in_specs=[ANY] * n, out_specs=[ANY] * n,
        out_shape=[jax.ShapeDtypeStruct((4,) + p.shape[1:], p.dtype) for p in packed],
        scratch_shapes=[pltpu.SemaphoreType.DMA((4 * n,)), pltpu.SemaphoreType.DMA((4 * n,))],
    )(*packed)


def _rs_chip_sum(packed, from_sibling, c_idx, name):
    _, r, cc = packed.shape
    tr = _pick(r, (512, 256, 352, 128))

    def body(c_ref, a_ref, b_ref, o_ref):
        o_ref[...] = (a_ref[...].astype(F32) + b_ref[...].astype(F32)).astype(o_ref.dtype)

    return pl.pallas_call(
        body, name=name,
        grid_spec=pltpu.PrefetchScalarGridSpec(
            num_scalar_prefetch=1, grid=(4, r // tr),
            in_specs=[pl.BlockSpec((None, tr, cc), lambda j, i, c_ref: (2 * j + c_ref[0], i, 0)),
                      pl.BlockSpec((None, tr, cc), lambda j, i, c_ref: (j, i, 0))],
            out_specs=pl.BlockSpec((None, tr, cc), lambda j, i, c_ref: (j, i, 0))),
        out_shape=jax.ShapeDtypeStruct((4, r, cc), packed.dtype),
        compiler_params=_params("parallel", "parallel"),
    )(c_idx, packed, from_sibling)


def _rs_chip_exchange(partial, name):
    n = len(partial)

    def body(*refs):
        start, finish = _chip_exchange_phases(refs[:n], refs[n:2 * n], *refs[2 * n:])
        start()
        finish()

    return pl.pallas_call(
        body, name=name, in_specs=[ANY] * n, out_specs=[ANY] * n,
        out_shape=_chip_exchange_shapes(partial), scratch_shapes=_chip_exchange_sems(n),
    )(*partial)


def _chip_exchange_shapes(partial):
    return [jax.ShapeDtypeStruct((3,) + p.shape[1:], p.dtype) for p in partial]


def _chip_exchange_sems(n):
    return [pltpu.SemaphoreType.DMA((3 * n,)), pltpu.SemaphoreType.DMA((3 * n,))]


def _chip_exchange_phases(p_refs, r_refs, send_sems, recv_sems):
    x, y, c = _place()
    chips = [(1 - x, y), (x, 1 - y), (1 - x, 1 - y)]

    def copies():
        return [pltpu.make_async_remote_copy(
            src_ref=p_refs[a].at[2 * tx + ty], dst_ref=r_refs[a].at[k],
            send_sem=send_sems.at[3 * a + k], recv_sem=recv_sems.at[3 * a + k],
            device_id=(tx, ty, c), device_id_type=MESH)
            for a in range(len(p_refs)) for k, (tx, ty) in enumerate(chips)]

    def start():
        for cp in copies():
            cp.start()

    def finish():
        for cp in copies():
            cp.wait()

    return start, finish


def _adam_update(w, g, m, v):
    mn = ADAM_B1 * m + (1.0 - ADAM_B1) * g
    vn = ADAM_B2 * v + (1.0 - ADAM_B2) * (g * g)
    m_hat = mn / (1.0 - ADAM_B1 ** ADAM_STEP)
    v_hat = vn / (1.0 - ADAM_B2 ** ADAM_STEP)
    return -ADAM_LR * (m_hat / (jnp.sqrt(v_hat) + ADAM_EPS) + ADAM_WD * w), mn, vn


def _rs_final_adamw(partial, received, chip_idx, w, m, v, name):
    _, r, cc = partial.shape
    tr = _pick(r, (512, 256, 352, 128))

    def body(c_ref, a_ref, r_ref, w_ref, m_ref, v_ref, g_ref, d_ref, mo_ref, vo_ref):
        g = a_ref[...].astype(F32)
        for k in range(3):
            g = g + r_ref[k].astype(F32)
        g_ref[...] = g
        d_ref[...], mo_ref[...], vo_ref[...] = _adam_update(w_ref[...], g, m_ref[...], v_ref[...])

    row = pl.BlockSpec((tr, cc), lambda i, c_ref: (i, 0))
    return pl.pallas_call(
        body, name=name,
        grid_spec=pltpu.PrefetchScalarGridSpec(
            num_scalar_prefetch=1, grid=(r // tr,),
            in_specs=[pl.BlockSpec((None, tr, cc), lambda i, c_ref: (c_ref[0], i, 0)),
                      pl.BlockSpec((3, tr, cc), lambda i, c_ref: (0, i, 0)), row, row, row],
            out_specs=[row] * 4),
        out_shape=[jax.ShapeDtypeStruct((r, cc), F32)] * 4,
        compiler_params=_params("parallel"),
    )(chip_idx, partial, received, w.reshape(r, cc), m.reshape(r, cc), v.reshape(r, cc))


def _all_reduce_small(vals, name):
    r, cc = vals.shape

    def body(v_ref, o_ref, buf, send_sems, recv_sems):
        x, y, c = _place()
        me = 4 * x + 2 * y + c
        buf[0] = v_ref[...]
        cps = []
        for k in range(1, N_DEV):
            kx, ky, kc = (k >> 2) & 1, (k >> 1) & 1, k & 1
            peer = (1 - x if kx else x, 1 - y if ky else y, 1 - c if kc else c)
            cps.append(pltpu.make_async_remote_copy(
                src_ref=buf.at[0], dst_ref=buf.at[k], send_sem=send_sems.at[k - 1],
                recv_sem=recv_sems.at[k - 1], device_id=peer, device_id_type=MESH))
        for cp in cps:
            cp.start()
        for cp in cps:
            cp.wait()
        acc = buf[jnp.bitwise_xor(me, 0)]
        for dev in range(1, N_DEV):
            acc = acc + buf[jnp.bitwise_xor(me, dev)]
        o_ref[...] = acc

    vm = pl.BlockSpec(memory_space=pltpu.VMEM)
    return pl.pallas_call(
        body, name=name, in_specs=[vm], out_specs=vm,
        out_shape=jax.ShapeDtypeStruct((r, cc), F32),
        scratch_shapes=[pltpu.VMEM((N_DEV, r, cc), F32), pltpu.SemaphoreType.DMA((7,)),
                        pltpu.SemaphoreType.DMA((7,))],
        compiler_params=pltpu.CompilerParams(vmem_limit_bytes=VMEM_LIMIT),
    )(vals)


def _lanes(flat):
    pad = (-flat.shape[0]) % (SUBLANES * LANES)
    return jnp.pad(flat, (0, pad)).reshape(-1, LANES)


def kernel(x, even_w_in, even_b_f, even_conv_w, even_w_out, odd_w_in, odd_v_ln_g, odd_v_ln_b, odd_w_s, odd_b_s, odd_w_out, mix_ln_g, mix_ln_b, ffn_w_in, ffn_w_out, ffn_ln_g, ffn_ln_b, loss_target, m_even_w_in, m_even_b_f, m_even_conv_w, m_even_w_out, m_odd_w_in, m_odd_v_ln_g, m_odd_v_ln_b, m_odd_w_s, m_odd_b_s, m_odd_w_out, m_mix_ln_g, m_mix_ln_b, m_ffn_w_in, m_ffn_w_out, m_ffn_ln_g, m_ffn_ln_b, v_even_w_in, v_even_b_f, v_even_conv_w, v_even_w_out, v_odd_w_in, v_odd_v_ln_g, v_odd_v_ln_b, v_odd_w_s, v_odd_b_s, v_odd_w_out, v_mix_ln_g, v_mix_ln_b, v_ffn_w_in, v_ffn_w_out, v_ffn_ln_g, v_ffn_ln_b):
    t, d = x.shape[1], x.shape[2]
    nh = even_b_f.shape[-1]
    w = even_conv_w.shape[-1] * N_DEV
    dh = w // nh
    scale = dh ** -0.5
    e_in = even_w_in.shape[-1] * N_DEV
    f2 = ffn_w_in.shape[-1] * N_DEV
    f = f2 // 2
    ng, pb = odd_w_s.shape[1], odd_w_s.shape[2]
    assert e_in == 6 * w + nh and nh <= SUBLANES and (6 * w) % LANES == 0 and d % N_DEV == 0
    mx, my, mc = _place()
    me = 4 * mx + 2 * my + mc

    big = [even_w_in[0], even_w_out[0], odd_w_in[0], odd_w_out[0],
           ffn_w_in[0], ffn_w_in[1], ffn_w_out[0], ffn_w_out[1]]
    g_in0, = _all_gather([big[0].astype(BF16)], "ag_even_w_in")
    w_in0 = g_in0.transpose(1, 0, 2).reshape(d, e_in)
    w_all0 = jnp.concatenate([w_in0[:, :3 * w], w_in0[:, 3 * w + nh:], w_in0[:, 3 * w:3 * w + nh],
                              jnp.zeros((d, LANES - nh), BF16)], axis=1)

    cs, vs = even_conv_w.shape[-1], odd_v_ln_g.shape[-1]
    small_mine = jnp.concatenate([
        lax.dynamic_update_slice(jnp.zeros((3, w), F32), even_conv_w[0], (0, me * cs)).reshape(-1),
        lax.dynamic_update_slice(jnp.zeros((d,), F32), odd_v_ln_g[0], (me * vs,)),
        lax.dynamic_update_slice(jnp.zeros((d,), F32), odd_v_ln_b[0], (me * vs,))])
    small_all = _all_reduce_small(_lanes(small_mine), "ag_small").reshape(-1)
    conv_w = small_all[:3 * w].reshape(3, w)
    vln_g = small_all[3 * w:3 * w + d]
    vln_b = small_all[3 * w + d:3 * w + 2 * d]

    bf_pad = jnp.pad(even_b_f[0], (0, LANES - nh)).reshape(1, LANES)
    chunk = jnp.arange(pb) // (pb // 2)
    ws_mask = (chunk[None, :] <= chunk[:, None])[None]
    wm = jnp.where(ws_mask, odd_w_s[0], 0.0).astype(BF16)
    bs_full = jnp.repeat(odd_b_s[0].T, d // ng, axis=1)

    x0 = x[0]
    tgt = loss_target[0]
    fcol = 6 * w // LANES
    x0b = x0.astype(BF16)
    p0 = _mm(x0b, w_all0, "nn", F32, "l0_in_proj")
    cgate = _fgate_fwd(p0, bf_pad, fcol, nh, "l0_fgate")
    assert dh + 7 <= LANES
    qa, ka, va = _attn_pack(p0, cgate, w, nh, scale, "l0_attn_pack")
    oa, g_out0, g_in1, g_out1, g_fi0, g_fi1, g_fo0, g_fo1 = _attn_fwd(
        qa, ka, va, dh, "l0_attn", gather=[s.astype(BF16) for s in big[1:]])
    w_out0, w_out1 = g_out0.reshape(2 * w, d), g_out1.reshape(d, d)
    w_fo0, w_fo1 = g_fo0.reshape(f, d), g_fo1.reshape(f, d)
    nb = N_DEV // 2
    w_fi0, w_fi1 = g_fi0.reshape(2, nb, d, -1), g_fi1.reshape(2, nb, d, -1)
    attn, = _attn_unpack(oa, w, nh, 1.0, None, 1.0, "l0_attn_unpack")
    yconv = _conv_fwd(p0, conv_w, w, 3, "l0_conv")
    mix = jnp.concatenate([attn, yconv], axis=1)
    m0 = _mm(mix, w_out0, "nn", F32, "l0_out_proj")
    x1, x1b = _ln_fwd(x0, m0, mix_ln_g[0], mix_ln_b[0], "l0_mix_ln")
    h0, gu0 = _ffn_in_swiglu(x1b, w_fi0, "l0_ffn_in")
    f0 = _mm_blk_fwd(h0, w_fo0, F32, "l0_ffn_out")
    x2, x2b = _ln_fwd(x1, f0, ffn_ln_g[0], ffn_ln_b[0], "l0_ffn_ln")

    uv = _mm_cols_fwd(x2b, g_in1, False, F32, "l1_in_proj")
    gated = _sgu_fwd(uv, vln_g, vln_b, wm, bs_full, "l1_sgu")
    m1 = _mm(gated, w_out1, "nn", F32, "l1_out_proj")
    x3, x3b = _ln_fwd(x2, m1, mix_ln_g[1], mix_ln_b[1], "l1_mix_ln")
    h1, gu1 = _ffn_in_swiglu(x3b, w_fi1, "l1_ffn_in")
    f1 = _mm_blk_fwd(h1, w_fo1, F32, "l1_ffn_out")
    x4, _ = _ln_fwd(x3, f1, ffn_ln_g[1], ffn_ln_b[1], "l1_ffn_ln")
    loss_part, dy4 = _loss(x4, tgt, "loss")

    dz4, dz4b, g_ffn_g1, g_ffn_b1 = _ln_bwd(x3, f1, ffn_ln_g[1], dy4, 1.0, None, "l1_ffn_ln_bwd")
    gd_fo1 = _mm_blk_dw(h1, dz4b, BF16, "l1_ffn_out_dw").reshape(N_DEV, -1, d)
    dgu1 = _ffn_out_dx_swiglu(dz4b, w_fo1, gu1, "l1_ffn_out_dx").reshape(N_DEV, t, -1)
    gd_fi1 = _mm_cols_dw(x3b, dgu1, N_DEV, True, BF16, "l1_ffn_in_dw")
    dx3 = _mm_cols_dx(dgu1, g_fi1, True, F32, "l1_ffn_in_dx")
    dz3, dz3b, g_mix_g1, g_mix_b1 = _ln_bwd(x2, m1, mix_ln_g[1], dz4, ALPHA, dx3, "l1_mix_ln_bwd")
    gd_out1 = _mm(gated, dz3b, "tn", BF16, "l1_out_proj_dw").reshape(N_DEV, -1, d)
    dgated = _mm(dz3b, w_out1, "nt", BF16, "l1_out_proj_dx")
    duv, g_wm, g_bs_t, g_vln_g, g_vln_b = _sgu_bwd(uv, vln_g, vln_b, wm, bs_full, dgated, "l1_sgu_bwd")
    gd_in1 = _mm_cols_dw(x2b, duv, N_DEV, False, BF16, "l1_in_proj_dw")
    dx2 = _mm_cols_dx(duv, g_in1, False, F32, "l1_in_proj_dx")

    dz2, dz2b, g_ffn_g0, g_ffn_b0 = _ln_bwd(x1, f0, ffn_ln_g[0], dz3, ALPHA, dx2, "l0_ffn_ln_bwd")
    gd_fo0 = _mm_blk_dw(h0, dz2b, BF16, "l0_ffn_out_dw").reshape(N_DEV, -1, d)
    dgu0 = _ffn_out_dx_swiglu(dz2b, w_fo0, gu0, "l0_ffn_out_dx").reshape(N_DEV, t, -1)
    gd_fi0 = _mm_cols_dw(x1b, dgu0, N_DEV, True, BF16, "l0_ffn_in_dw")
    dx1 = _mm_cols_dx(dgu0, g_fi0, True, F32, "l0_ffn_in_dx")
    dz1, dz1b, g_mix_g0, g_mix_b0 = _ln_bwd(x0, m0, mix_ln_g[0], dz2, ALPHA, dx1, "l0_mix_ln_bwd")
    gd_out0 = _mm(mix, dz1b, "tn", BF16, "l0_out_proj_dw").reshape(N_DEV, -1, d)
    dmix = _mm(dz1b, w_out0, "nt", F32, "l0_out_proj_dx")
    d_b, d_c, d_h, g_conv = _conv_bwd(p0, conv_w, dmix, w, 3, "l0_conv_bwd")
    doa, qa2 = _attn_pack_bwd(dmix, oa, qa, w, nh, "l0_attn_pack_bwd")
    big_names = ["even_w_in", "even_w_out", "odd_w_in", "odd_w_out", "ffn_w_in0", "ffn_w_in1", "ffn_w_out0", "ffn_w_out1"]
    c_idx = mc.reshape(1).astype(jnp.int32)
    chip_idx = (2 * mx + my).reshape(1).astype(jnp.int32)
    early_g = [gd_out0, gd_in1, gd_out1, gd_fi0, gd_fi1, gd_fo0, gd_fo1]
    early_sib = _rs_sibling_exchange(early_g, "rs_sibling_early")
    early_partial = [_rs_chip_sum(g, s, c_idx, "rs_chip_sum_" + n)
                     for g, s, n in zip(early_g, early_sib, big_names[1:])]
    dqa, dka, dva, *early_received = _attn_bwd(qa2, ka, va, doa, "l0_attn_bwd", exchange=early_partial)
    dq, dcq = _attn_unpack(dqa, w, nh, scale, dh + 3, 1.0, "l0_attn_unpack_dq")
    dk, dck = _attn_unpack(dka, w, nh, 1.0, dh, -1.0, "l0_attn_unpack_dk")
    dv, = _attn_unpack(dva, w, nh, 1.0, None, 1.0, "l0_attn_unpack_dv")
    dzf, g_bf = _fgate_bwd(p0, bf_pad, dcq, dck, fcol, nh, "l0_fgate_bwd")
    dp0 = jnp.concatenate([dq, dk, dv, d_b.astype(BF16), d_c.astype(BF16), d_h.astype(BF16), dzf.astype(BF16)], axis=1)
    g_all0 = _mm(x0b, dp0, "tn", F32, "l0_in_proj_dw")
    dx0 = _mm(dp0, w_all0, "nt", F32, "l0_in_proj_dx")
    grad_x = _axpy(ALPHA, dz1, dx0, "grad_x")
    gd_in0 = jnp.concatenate([g_all0[:, :3 * w], g_all0[:, 6 * w:6 * w + nh], g_all0[:, 3 * w:6 * w]], axis=1)
    gd_in0 = gd_in0.reshape(d, N_DEV, -1).transpose(1, 0, 2).astype(BF16)

    big_m = [m_even_w_in[0], m_even_w_out[0], m_odd_w_in[0], m_odd_w_out[0],
             m_ffn_w_in[0], m_ffn_w_in[1], m_ffn_w_out[0], m_ffn_w_out[1]]
    big_v = [v_even_w_in[0], v_even_w_out[0], v_odd_w_in[0], v_odd_w_out[0],
             v_ffn_w_in[0], v_ffn_w_in[1], v_ffn_w_out[0], v_ffn_w_out[1]]
    late_sib = _rs_sibling_exchange([gd_in0], "rs_sibling_late")
    late_partial = [_rs_chip_sum(gd_in0, late_sib[0], c_idx, "rs_chip_sum_" + big_names[0])]
    partial = late_partial + early_partial
    received = list(_rs_chip_exchange(late_partial, "rs_chips_late")) + list(early_received)
    upd = [_rs_final_adamw(p, r, chip_idx, wt, mt, vt, "rs_final_adamw_" + n)
           for p, r, wt, mt, vt, n in zip(partial, received, big, big_m, big_v, big_names)]
    big_out = {}
    for i, n in enumerate(["even_w_in", "even_w_out", "odd_w_in", "odd_w_out"]):
        big_out[n] = [o[None] for o in upd[i]]
    big_out["ffn_w_in"] = [jnp.stack([a, b]) for a, b in zip(upd[4], upd[5])]
    big_out["ffn_w_out"] = [jnp.stack([a, b]) for a, b in zip(upd[6], upd[7])]

    g_ws = jnp.where(ws_mask, g_wm, 0.0)
    g_bs = g_bs_t[:, :ng].T
    small_g = [g_bf[:nh], g_conv, g_vln_g, g_vln_b, g_ws, g_bs,
               jnp.stack([g_mix_g0, g_mix_g1]), jnp.stack([g_mix_b0, g_mix_b1]),
               jnp.stack([g_ffn_g0, g_ffn_g1]), jnp.stack([g_ffn_b0, g_ffn_b1])]
    small_sum = _all_reduce_small(_lanes(jnp.concatenate([a.reshape(-1) for a in small_g])), "ar_small_grads")
    small_sum = small_sum.reshape(-1)
    outs_small = []
    off = 0
    for a in small_g:
        outs_small.append(small_sum[off:off + a.size].reshape(a.shape))
        off += a.size
    gr_bf, gr_conv, gr_vg, gr_vb, gr_ws, gr_bs, gr_mg, gr_mb, gr_fg, gr_fb = outs_small

    loss = lax.psum(loss_part, ("x", "y", "c"))

    grads = {
        "even_b_f": gr_bf[None],
        "even_conv_w": lax.dynamic_slice(gr_conv, (0, me * cs), (3, cs))[None],
        "odd_v_ln_g": lax.dynamic_slice(gr_vg, (me * vs,), (vs,))[None],
        "odd_v_ln_b": lax.dynamic_slice(gr_vb, (me * vs,), (vs,))[None],
        "odd_w_s": gr_ws[None], "odd_b_s": gr_bs[None],
        "mix_ln_g": gr_mg, "mix_ln_b": gr_mb, "ffn_ln_g": gr_fg, "ffn_ln_b": gr_fb,
    }
    weights = dict(even_w_in=even_w_in, even_b_f=even_b_f, even_conv_w=even_conv_w, even_w_out=even_w_out,
                   odd_w_in=odd_w_in, odd_v_ln_g=odd_v_ln_g, odd_v_ln_b=odd_v_ln_b, odd_w_s=odd_w_s,
                   odd_b_s=odd_b_s, odd_w_out=odd_w_out, mix_ln_g=mix_ln_g, mix_ln_b=mix_ln_b,
                   ffn_w_in=ffn_w_in, ffn_w_out=ffn_w_out, ffn_ln_g=ffn_ln_g, ffn_ln_b=ffn_ln_b)
    moms = dict(even_w_in=(m_even_w_in, v_even_w_in), even_b_f=(m_even_b_f, v_even_b_f),
                even_conv_w=(m_even_conv_w, v_even_conv_w), even_w_out=(m_even_w_out, v_even_w_out),
                odd_w_in=(m_odd_w_in, v_odd_w_in), odd_v_ln_g=(m_odd_v_ln_g, v_odd_v_ln_g),
                odd_v_ln_b=(m_odd_v_ln_b, v_odd_v_ln_b), odd_w_s=(m_odd_w_s, v_odd_w_s),
                odd_b_s=(m_odd_b_s, v_odd_b_s), odd_w_out=(m_odd_w_out, v_odd_w_out),
                mix_ln_g=(m_mix_ln_g, v_mix_ln_g), mix_ln_b=(m_mix_ln_b, v_mix_ln_b),
                ffn_w_in=(m_ffn_w_in, v_ffn_w_in), ffn_w_out=(m_ffn_w_out, v_ffn_w_out),
                ffn_ln_g=(m_ffn_ln_g, v_ffn_ln_g), ffn_ln_b=(m_ffn_ln_b, v_ffn_ln_b))
    names = list(weights)
    gout, deltas, new_m, new_v = [], [], [], []
    for n in names:
        if n in big_out:
            gr, dlt, mn, vn = big_out[n]
        else:
            gr = grads[n]
            dlt, mn, vn = _adamw(weights[n], gr, moms[n][0], moms[n][1], "adamw_" + n)
        gout.append(gr.reshape(weights[n].shape))
        deltas.append(dlt.reshape(weights[n].shape))
        new_m.append(mn.reshape(weights[n].shape))
        new_v.append(vn.reshape(weights[n].shape))
    return (loss, grad_x[None], *gout, *deltas, *new_m, *new_v)
```

```python
import functools
from typing import Callable, NamedTuple

import jax
import jax.numpy as jnp
from jax import lax
from jax.experimental import pallas as pl
from jax.experimental.pallas import tpu as pltpu

F32 = jnp.float32
BF16 = jnp.bfloat16
MESH = pl.DeviceIdType.MESH

DEPTH = 2
ALPHA = (2.0 * DEPTH) ** 0.25
LN_EPS = 1e-5
ADAM_LR = 0.001
ADAM_B1 = 0.9
ADAM_B2 = 0.999
ADAM_EPS = 1e-08
ADAM_WD = 0.01
ADAM_STEP = 10

N_DEV = 8
LANES = 128
SUBLANES = 8
VMEM_LIMIT = 48 * 1024 * 1024
NEG_BIG = -1e30
ROW_TILES = (512, 256, 128)


def _pick(n, cands):
    for c in cands:
        if c <= n and n % c == 0:
            return c
    return n


def _params(*sem):
    return pltpu.CompilerParams(dimension_semantics=sem, vmem_limit_bytes=VMEM_LIMIT)


NN = (((1,), (0,)), ((), ()))
NT = (((1,), (1,)), ((), ()))
TN = (((0,), (0,)), ((), ()))
M_TILES = (1024, 512, 1408, 256, 128)
N_TILES = (512, 640, 256, 128)
K_TILES = (2048, 1024, 512, 640, 1408, 256, 128)
K_WHOLE = 3328


class _Rider(NamedTuple):
    inputs: list
    in_specs: list
    out_shapes: list
    out_specs: list
    scratch: list
    phases: Callable


def _mm_core(name, grid, a, b, a_spec, b_spec, o_spec, o_shape, o_dtype, dims, tile, pieces=None, rider=None):
    nred = grid[2]
    pieces = pieces or [(lambda r: r[...], lambda r: r[...])]
    ni = len(rider.inputs) if rider else 0
    no = len(rider.out_shapes) if rider else 0
    nacc = 0 if nred == 1 else 1

    def body(a_ref, b_ref, *rest):
        o_ref = rest[ni]
        if rider:
            start, finish = rider.phases(rest[:ni], rest[ni + 1:ni + 1 + no], rest[ni + 1 + no + nacc:])
            ids = [pl.program_id(ax) for ax in range(3)]
            first = functools.reduce(jnp.logical_and, [i == 0 for i in ids])
            last = functools.reduce(jnp.logical_and, [i == g - 1 for i, g in zip(ids, grid)])
            pl.when(first)(start)
        part = None
        for fa, fb in pieces:
            prod = lax.dot_general(fa(a_ref).astype(BF16), fb(b_ref).astype(BF16), dims, preferred_element_type=F32)
            part = prod if part is None else part + prod
        if nred == 1:
            o_ref[...] = part.astype(o_ref.dtype)
        else:
            acc_ref = rest[ni + 1 + no]
            kk = pl.program_id(2)

            @pl.when(kk == 0)
            def _():
                acc_ref[...] = jnp.zeros_like(acc_ref)

            acc_ref[...] += part

            @pl.when(kk == nred - 1)
            def _():
                o_ref[...] = acc_ref[...].astype(o_ref.dtype)
        if rider:
            pl.when(last)(finish)

    out = pl.pallas_call(
        body, name=name, grid=grid,
        in_specs=[a_spec, b_spec] + (rider.in_specs if rider else []),
        out_specs=[o_spec] + (rider.out_specs if rider else []),
        out_shape=[jax.ShapeDtypeStruct(o_shape, o_dtype)] + (rider.out_shapes if rider else []),
        scratch_shapes=([] if nred == 1 else [pltpu.VMEM(tile, F32)]) + (rider.scratch if rider else []),
        compiler_params=_params(*(["arbitrary"] * 3 if rider else ["parallel", "parallel", "arbitrary"])),
    )(a, b, *(rider.inputs if rider else []))
    return out if rider else out[0]


def _mm(a, b, mode, out_dtype, name, rider=None):
    if mode == "nn":
        (m, k), (k2, n) = a.shape, b.shape
    elif mode == "nt":
        (m, k), (n, k2) = a.shape, b.shape
    else:
        (k, m), (k2, n) = a.shape, b.shape
    assert k == k2, (a.shape, b.shape, mode)
    tm, tn = _pick(m, M_TILES), _pick(n, N_TILES)
    tk = k if k <= K_WHOLE else _pick(k, K_TILES)
    if mode == "nn":
        a_spec = pl.BlockSpec((tm, tk), lambda i, j, kk: (i, kk))
        b_spec = pl.BlockSpec((tk, tn), lambda i, j, kk: (kk, j))
        dims = NN
    elif mode == "nt":
        a_spec = pl.BlockSpec((tm, tk), lambda i, j, kk: (i, kk))
        b_spec = pl.BlockSpec((tn, tk), lambda i, j, kk: (j, kk))
        dims = NT
    else:
        a_spec = pl.BlockSpec((tk, tm), lambda i, j, kk: (kk, i))
        b_spec = pl.BlockSpec((tk, tn), lambda i, j, kk: (kk, j))
        dims = TN
    return _mm_core(name, (m // tm, n // tn, k // tk), a, b, a_spec, b_spec,
                    pl.BlockSpec((tm, tn), lambda i, j, kk: (i, j)), (m, n), out_dtype, dims, (tm, tn), rider=rider)


def _act_spec(blocked, rows, ns, row_ax, d_ax):
    if blocked:
        return pl.BlockSpec((None, rows, ns), lambda *g: (g[d_ax], g[row_ax], 0))
    return pl.BlockSpec((rows, ns), lambda *g: (g[row_ax], g[d_ax]))


def _mm_cols_fwd(a, g3, blocked, out_dtype, name):
    (t, k), (nd, k2, ns) = a.shape, g3.shape
    assert k == k2
    tm, tk = _pick(t, M_TILES), _pick(k, K_TILES)
    return _mm_core(name, (t // tm, nd, k // tk), a, g3,
                    pl.BlockSpec((tm, tk), lambda i, d, kk: (i, kk)),
                    pl.BlockSpec((None, tk, ns), lambda i, d, kk: (d, kk, 0)),
                    _act_spec(blocked, tm, ns, 0, 1), (nd, t, ns) if blocked else (t, nd * ns), out_dtype, NN, (tm, ns))


def _mm_cols_dx(dy, g3, blocked, out_dtype, name):
    nd, k, ns = g3.shape
    t = dy.shape[1] if blocked else dy.shape[0]
    tm, tn = _pick(t, M_TILES), _pick(k, (1024,) + N_TILES)
    o_spec = pl.BlockSpec((tm, tn), lambda i, j, d: (i, j))
    if not blocked:
        whole_b = lambda r: jnp.concatenate([r[s] for s in range(nd)], axis=1)
        return _mm_core(name, (t // tm, k // tn, 1), dy, g3,
                        pl.BlockSpec((tm, nd * ns), lambda i, j, d: (i, 0)),
                        pl.BlockSpec((nd, tn, ns), lambda i, j, d: (0, j, 0)),
                        o_spec, (t, k), out_dtype, NT, (tm, tn), pieces=[(lambda r: r[...], whole_b)])
    grp = 2 if nd % 2 == 0 else 1
    pieces = [(lambda r, s=s: r[s], lambda r, s=s: r[s]) for s in range(grp)]
    return _mm_core(name, (t // tm, k // tn, nd // grp), dy, g3,
                    pl.BlockSpec((grp, tm, ns), lambda i, j, d: (d, i, 0)),
                    pl.BlockSpec((grp, tn, ns), lambda i, j, d: (d, j, 0)),
                    o_spec, (t, k), out_dtype, NT, (tm, tn), pieces=pieces)


def _mm_cols_dw(a, dy, nd, blocked, out_dtype, name):
    t, k = a.shape
    ns = dy.shape[2] if blocked else dy.shape[1] // nd
    tmk, tk = _pick(k, M_TILES), _pick(t, K_TILES)
    return _mm_core(name, (nd, k // tmk, t // tk), a, dy,
                    pl.BlockSpec((tk, tmk), lambda d, j, kk: (kk, j)),
                    _act_spec(blocked, tk, ns, 2, 0),
                    pl.BlockSpec((None, tmk, ns), lambda d, j, kk: (d, j, 0)), (nd, k, ns), out_dtype, TN, (tmk, ns))


def _mm_blk_fwd(h3, w, out_dtype, name):
    (nb, t, ns), (_, n) = h3.shape, w.shape
    tm, tn = _pick(t, M_TILES), _pick(n, (1024,) + N_TILES)
    pieces = [(lambda r, s=s: r[s], lambda r, s=s: r[s * ns:(s + 1) * ns, :]) for s in range(nb)]
    return _mm_core(name, (t // tm, n // tn, 1), h3, w,
                    pl.BlockSpec((nb, tm, ns), lambda i, j, d: (0, i, 0)),
                    pl.BlockSpec((nb * ns, tn), lambda i, j, d: (0, j)),
                    pl.BlockSpec((tm, tn), lambda i, j, d: (i, j)), (t, n), out_dtype, NN, (tm, tn), pieces=pieces)


def _mm_blk_dw(h3, dz, out_dtype, name):
    (nb, t, ns), (_, n) = h3.shape, dz.shape
    tn, tk = _pick(n, (1024,) + N_TILES), _pick(t, K_TILES)
    return _mm_core(name, (nb, n // tn, t // tk), h3, dz,
                    pl.BlockSpec((None, tk, ns), lambda d, j, kk: (d, kk, 0)),
                    pl.BlockSpec((tk, tn), lambda d, j, kk: (kk, j)),
                    pl.BlockSpec((ns, tn), lambda d, j, kk: (d, j)), (nb * ns, n), out_dtype, TN, (ns, tn))


def _ln_fwd(xa, xb, g, b, name):
    t, d = xa.shape
    tb = _pick(t, ROW_TILES)

    def body(xa_ref, xb_ref, g_ref, b_ref, y_ref, yb_ref):
        z = ALPHA * xa_ref[...] + xb_ref[...]
        mu = jnp.mean(z, axis=-1, keepdims=True)
        zc = z - mu
        var = jnp.mean(zc * zc, axis=-1, keepdims=True)
        y = zc * lax.rsqrt(var + LN_EPS) * g_ref[...] + b_ref[...]
        y_ref[...] = y
        yb_ref[...] = y.astype(BF16)

    row = pl.BlockSpec((tb, d), lambda i: (i, 0))
    vec = pl.BlockSpec((1, d), lambda i: (0, 0))
    return pl.pallas_call(
        body, name=name, grid=(t // tb,),
        in_specs=[row, row, vec, vec], out_specs=[row, row],
        out_shape=[jax.ShapeDtypeStruct((t, d), F32), jax.ShapeDtypeStruct((t, d), BF16)],
        compiler_params=_params("parallel"),
    )(xa, xb, g.reshape(1, d), b.reshape(1, d))


def _ln_bwd(xa, xb, g, dya, ca, dyb, name):
    t, d = xa.shape
    tb = _pick(t, ROW_TILES)
    two = dyb is not None

    def body(*refs):
        if two:
            xa_ref, xb_ref, g_ref, dya_ref, dyb_ref, dz_ref, dzb_ref, dg_ref, db_ref = refs
            dy = ca * dya_ref[...] + dyb_ref[...]
        else:
            xa_ref, xb_ref, g_ref, dya_ref, dz_ref, dzb_ref, dg_ref, db_ref = refs
            dy = ca * dya_ref[...]
        z = ALPHA * xa_ref[...] + xb_ref[...]
        mu = jnp.mean(z, axis=-1, keepdims=True)
        zc = z - mu
        var = jnp.mean(zc * zc, axis=-1, keepdims=True)
        rstd = lax.rsqrt(var + LN_EPS)
        xhat = zc * rstd
        dxh = dy * g_ref[...]
        m1 = jnp.mean(dxh, axis=-1, keepdims=True)
        m2 = jnp.mean(dxh * xhat, axis=-1, keepdims=True)
        dz = rstd * (dxh - m1 - xhat * m2)
        dz_ref[...] = dz
        dzb_ref[...] = dz.astype(BF16)

        @pl.when(pl.program_id(0) == 0)
        def _():
            dg_ref[...] = jnp.zeros_like(dg_ref)
            db_ref[...] = jnp.zeros_like(db_ref)

        dg_ref[...] += jnp.sum(dy * xhat, axis=0, keepdims=True)
        db_ref[...] += jnp.sum(dy, axis=0, keepdims=True)

    row = pl.BlockSpec((tb, d), lambda i: (i, 0))
    vec = pl.BlockSpec((1, d), lambda i: (0, 0))
    ins = [xa, xb, g.reshape(1, d), dya] + ([dyb] if two else [])
    dz, dzb, dg, db = pl.pallas_call(
        body, name=name, grid=(t // tb,),
        in_specs=[row, row, vec, row] + ([row] if two else []),
        out_specs=[row, row, vec, vec],
        out_shape=[jax.ShapeDtypeStruct((t, d), F32), jax.ShapeDtypeStruct((t, d), BF16),
                   jax.ShapeDtypeStruct((1, d), F32), jax.ShapeDtypeStruct((1, d), F32)],
        compiler_params=_params("arbitrary"),
    )(*ins)
    return dz, dzb, dg[0], db[0]


def _loss(y, target, name):
    t, d = y.shape
    tb = _pick(t, ROW_TILES)

    def body(y_ref, t_ref, dy_ref, l_ref):
        e = y_ref[...] - t_ref[...]
        dy_ref[...] = e * (1.0 / d)

        @pl.when(pl.program_id(0) == 0)
        def _():
            l_ref[...] = jnp.zeros_like(l_ref)

        l_ref[...] += 0.5 * jnp.sum(jnp.mean(e * e, axis=-1, keepdims=True))

    row = pl.BlockSpec((tb, d), lambda i: (i, 0))
    dy, l = pl.pallas_call(
        body, name=name, grid=(t // tb,),
        in_specs=[row, row], out_specs=[row, pl.BlockSpec((1, LANES), lambda i: (0, 0))],
        out_shape=[jax.ShapeDtypeStruct((t, d), F32), jax.ShapeDtypeStruct((1, LANES), F32)],
        compiler_params=_params("arbitrary"),
    )(y, target)
    return l[0, 0], dy


def _axpy(ca, a, b, name):
    t, d = a.shape
    tb = _pick(t, ROW_TILES)

    def body(a_ref, b_ref, o_ref):
        o_ref[...] = ca * a_ref[...] + b_ref[...]

    row = pl.BlockSpec((tb, d), lambda i: (i, 0))
    return pl.pallas_call(
        body, name=name, grid=(t // tb,), in_specs=[row, row], out_specs=row,
        out_shape=jax.ShapeDtypeStruct((t, d), F32), compiler_params=_params("parallel"),
    )(a, b)


def _ffn_in_swiglu(xb, g4, name):
    (t, k), (_, nb, _, ns) = xb.shape, g4.shape
    tm = _pick(t, M_TILES)

    def body(x_ref, w_ref, h_ref, gu_ref):
        xv = x_ref[...]
        gate = jnp.dot(xv, w_ref[0], preferred_element_type=F32)
        up = jnp.dot(xv, w_ref[1], preferred_element_type=F32)
        h_ref[...] = (gate * jax.nn.sigmoid(gate) * up).astype(BF16)
        gu_ref[0] = gate.astype(BF16)
        gu_ref[1] = up.astype(BF16)

    return pl.pallas_call(
        body, name=name, grid=(t // tm, nb),
        in_specs=[pl.BlockSpec((tm, k), lambda i, d: (i, 0)),
                  pl.BlockSpec((2, None, k, ns), lambda i, d: (0, d, 0, 0))],
        out_specs=[pl.BlockSpec((None, tm, ns), lambda i, d: (d, i, 0)),
                   pl.BlockSpec((2, None, tm, ns), lambda i, d: (0, d, i, 0))],
        out_shape=[jax.ShapeDtypeStruct((nb, t, ns), BF16), jax.ShapeDtypeStruct((2, nb, t, ns), BF16)],
        compiler_params=_params("parallel", "parallel"),
    )(xb, g4)


def _ffn_out_dx_swiglu(dz, w_out, gu4, name):
    (t, d), (_, nb, _, ns) = dz.shape, gu4.shape
    tm = _pick(t, M_TILES)

    def body(dz_ref, w_ref, gu_ref, o_ref):
        halves = [slice(0, tm // 2), slice(tm // 2, tm)] if tm % 16 == 0 else [slice(0, tm)]
        dhs = [lax.dot_general(dz_ref[rows, :].astype(BF16), w_ref[...], NT, preferred_element_type=F32)
               for rows in halves]
        for rows, dh in zip(halves, dhs):
            gate = gu_ref[0, rows, :].astype(F32)
            up = gu_ref[1, rows, :].astype(F32)
            sg = jax.nn.sigmoid(gate)
            silu = gate * sg
            o_ref[0, rows, :] = (dh * up * (sg + silu * (1.0 - sg))).astype(BF16)
            o_ref[1, rows, :] = (dh * silu).astype(BF16)

    blk = pl.BlockSpec((2, None, tm, ns), lambda i, j: (0, j, i, 0))
    return pl.pallas_call(
        body, name=name, grid=(t // tm, nb),
        in_specs=[pl.BlockSpec((tm, d), lambda i, j: (i, 0)), pl.BlockSpec((ns, d), lambda i, j: (j, 0)), blk],
        out_specs=blk,
        out_shape=jax.ShapeDtypeStruct((2, nb, t, ns), BF16),
        compiler_params=_params("parallel", "parallel"),
    )(dz, w_out, gu4)


def _tri_matmul(tri, x):
    x1 = x.astype(BF16)
    r1 = x - x1.astype(F32)
    x2 = r1.astype(BF16)
    x3 = (r1 - x2.astype(F32)).astype(BF16)
    dot = lambda v: jnp.dot(tri, v, preferred_element_type=F32)
    return dot(x1) + dot(x2) + dot(x3)


def _fgate_fwd(proj, bf_pad, fcol, n_heads, name):
    t = proj.shape[0]
    tb = _pick(t, ROW_TILES)

    def body(p_ref, b_ref, c_ref, carry):
        @pl.when(pl.program_id(0) == 0)
        def _():
            carry[...] = jnp.zeros_like(carry)

        z = p_ref[...] + b_ref[...]
        lf = jnp.minimum(z, 0.0) - jnp.log1p(jnp.exp(-jnp.abs(z)))
        lane = lax.broadcasted_iota(jnp.int32, (tb, LANES), 1)
        lf = jnp.where(lane < n_heads, lf, 0.0)
        r = lax.broadcasted_iota(jnp.int32, (tb, tb), 0)
        s = lax.broadcasted_iota(jnp.int32, (tb, tb), 1)
        tri = (s <= r).astype(BF16)
        c = _tri_matmul(tri, lf) + carry[...]
        c_ref[...] = c
        carry[...] = c[tb - 1:tb, :]

    return pl.pallas_call(
        body, name=name, grid=(t // tb,),
        in_specs=[pl.BlockSpec((tb, LANES), lambda i: (i, fcol)), pl.BlockSpec((1, LANES), lambda i: (0, 0))],
        out_specs=pl.BlockSpec((tb, LANES), lambda i: (i, 0)),
        out_shape=jax.ShapeDtypeStruct((t, LANES), F32),
        scratch_shapes=[pltpu.VMEM((1, LANES), F32)],
        compiler_params=_params("arbitrary"),
    )(proj, bf_pad)


def _fgate_bwd(proj, bf_pad, dcq, dck, fcol, n_heads, name):
    t = proj.shape[0]
    tb = _pick(t, ROW_TILES)
    nb = t // tb

    def body(p_ref, b_ref, dcq_ref, dck_ref, dz_ref, db_ref, carry):
        @pl.when(pl.program_id(0) == 0)
        def _():
            carry[...] = jnp.zeros_like(carry)
            db_ref[...] = jnp.zeros_like(db_ref)

        r = lax.broadcasted_iota(jnp.int32, (tb, tb), 0)
        s = lax.broadcasted_iota(jnp.int32, (tb, tb), 1)
        tri = (s >= r).astype(BF16)
        dlf = _tri_matmul(tri, dcq_ref[...] + dck_ref[...]) + carry[...]
        carry[...] = dlf[0:1, :]
        z = p_ref[...] + b_ref[...]
        lane = lax.broadcasted_iota(jnp.int32, (tb, LANES), 1)
        dz = jnp.where(lane < n_heads, dlf * jax.nn.sigmoid(-z), 0.0)
        dz_ref[...] = dz
        db_ref[...] += jnp.sum(dz, axis=0, keepdims=True)

    dz, db = pl.pallas_call(
        body, name=name, grid=(nb,),
        in_specs=[pl.BlockSpec((tb, LANES), lambda i: (nb - 1 - i, fcol)),
                  pl.BlockSpec((1, LANES), lambda i: (0, 0)),
                  pl.BlockSpec((tb, LANES), lambda i: (nb - 1 - i, 0)),
                  pl.BlockSpec((tb, LANES), lambda i: (nb - 1 - i, 0))],
        out_specs=[pl.BlockSpec((tb, LANES), lambda i: (nb - 1 - i, 0)),
                   pl.BlockSpec((1, LANES), lambda i: (0, 0))],
        out_shape=[jax.ShapeDtypeStruct((t, LANES), F32), jax.ShapeDtypeStruct((1, LANES), F32)],
        scratch_shapes=[pltpu.VMEM((1, LANES), F32)],
        compiler_params=_params("arbitrary"),
    )(proj, bf_pad, dcq, dck)
    return dz, db[0]


def _split3(x):
    hi = x.astype(BF16)
    r = x - hi.astype(F32)
    mid = r.astype(BF16)
    return hi, mid, (r - mid.astype(F32)).astype(BF16)


def _attn_fwd(qa, ka, va, dh, name, gather=()):
    nh, t, da = qa.shape
    tq = _pick(t, ROW_TILES)
    hb = 2 if nh % 2 == 0 else 1
    heads = range(hb)
    n = len(gather)
    steps = (nh // hb, t // tq)

    def body(q_ref, k_ref, v_ref, *rest):
        x_refs, o_ref, g_refs = rest[:n], rest[n], rest[n + 1:2 * n + 1]
        m_s, acc_s, s_a, s_b = rest[2 * n + 1:2 * n + 5]
        qi = pl.program_id(1)
        if n:
            start, forward, finish = _gather_phases(x_refs, g_refs, *rest[2 * n + 5:])
            at = lambda hh, qq: jnp.logical_and(pl.program_id(0) == hh, qi == qq)
            pl.when(at(0, 0))(start)
            pl.when(at(steps[0] // 2, 0))(forward)
        m_s[...] = jnp.full(m_s.shape, NEG_BIG, F32)
        acc_s[...] = jnp.zeros_like(acc_s)

        def scores(s_ref, j):
            off = pl.multiple_of(j * tq, tq)
            for g in heads:
                s_ref[g] = lax.dot_general(q_ref[g], k_ref[g, pl.ds(off, tq), :], NT, preferred_element_type=F32)

        def absorb(s_ref, j, diagonal):
            off = pl.multiple_of(j * tq, tq)
            s = [s_ref[g] for g in heads]
            if diagonal:
                row = lax.broadcasted_iota(jnp.int32, (tq, tq), 0)
                col = lax.broadcasted_iota(jnp.int32, (tq, tq), 1)
                s = [jnp.where(col > row, NEG_BIG, sg) for sg in s]
            m_prev = [m_s[g] for g in heads]
            m_new = [jnp.maximum(m_prev[g], jnp.max(s[g], axis=1, keepdims=True)) for g in heads]
            p = [jnp.exp(s[g] - m_new[g]).astype(BF16) for g in heads]
            pv = [jnp.dot(p[g], v_ref[g, pl.ds(off, tq), :], preferred_element_type=F32) for g in heads]
            for g in heads:
                acc_s[g] = jnp.exp(m_prev[g] - m_new[g]) * acc_s[g] + pv[g]
                m_s[g] = m_new[g]

        def two_blocks(r, carry):
            scores(s_b, 2 * r + 1)
            absorb(s_a, 2 * r, False)
            scores(s_a, 2 * r + 2)
            absorb(s_b, 2 * r + 1, False)
            return carry

        scores(s_a, 0)
        rounds = qi // 2
        lax.fori_loop(0, rounds, two_blocks, 0)

        @pl.when(qi % 2 == 0)
        def _():
            absorb(s_a, qi, True)

        @pl.when(qi % 2 == 1)
        def _():
            scores(s_b, qi)
            absorb(s_a, qi - 1, False)
            absorb(s_b, qi, True)

        lane = lax.broadcasted_iota(jnp.int32, (tq, da), 1)
        for g in heads:
            acc = acc_s[g]
            l = jnp.sum(jnp.where(lane == dh, acc, 0.0), axis=1, keepdims=True)
            o_ref[g] = jnp.where(lane == dh, m_s[g] + jnp.log(l), acc / l)
        if n:
            pl.when(at(steps[0] - 1, steps[1] - 1))(finish)

    full = pl.BlockSpec((hb, t, da), lambda h, qi: (h, 0, 0))
    blk = pl.BlockSpec((hb, tq, da), lambda h, qi: (h, qi, 0))
    return pl.pallas_call(
        body, name=name, grid=steps,
        in_specs=[blk, full, full] + [ANY] * n, out_specs=[blk] + [ANY] * n,
        out_shape=[jax.ShapeDtypeStruct((nh, t, da), F32)] + _gather_shapes(gather),
        scratch_shapes=[pltpu.VMEM((hb, tq, 1), F32), pltpu.VMEM((hb, tq, da), F32),
                        pltpu.VMEM((hb, tq, tq), F32), pltpu.VMEM((hb, tq, tq), F32)] + (_gather_sems(n) if n else []),
        compiler_params=_params("arbitrary", "arbitrary"),
    )(qa, ka, va, *gather)


def _attn_bwd(qa, ka, va, doa, name, exchange=()):
    nh, t, da = qa.shape
    tq = _pick(t, ROW_TILES)
    nq = t // tq
    n = len(exchange)

    def body(q_ref, do_ref, k_ref, v_ref, *rest):
        p_refs, (dq_ref, dk_ref, dv_ref), r_refs = rest[:n], rest[n:n + 3], rest[n + 3:2 * n + 3]
        kj = pl.program_id(1)
        if n:
            start, finish = _chip_exchange_phases(p_refs, r_refs, *rest[2 * n + 3:])
            pl.when(jnp.logical_and(pl.program_id(0) == 0, kj == 0))(start)

        @pl.when(kj == 0)
        def _():
            dq_ref[...] = jnp.zeros_like(dq_ref)

        dk_ref[...] = jnp.zeros_like(dk_ref)
        dv_ref[...] = jnp.zeros_like(dv_ref)
        kb = k_ref[...]
        vb = v_ref[...]

        def step(i, diagonal):
            off = pl.multiple_of(i * tq, tq)
            qb = q_ref[pl.ds(off, tq), :]
            dob = do_ref[pl.ds(off, tq), :]
            st = lax.dot_general(kb, qb, NT, preferred_element_type=F32)
            if diagonal:
                row = lax.broadcasted_iota(jnp.int32, (tq, tq), 0)
                col = lax.broadcasted_iota(jnp.int32, (tq, tq), 1)
                st = jnp.where(row > col, NEG_BIG, st)
            pt = jnp.exp(st)
            dst = (pt * lax.dot_general(vb, dob, NT, preferred_element_type=F32)).astype(BF16)
            dv_ref[...] += jnp.dot(pt.astype(BF16), dob, preferred_element_type=F32)
            dk_ref[...] += jnp.dot(dst, qb, preferred_element_type=F32)
            dq_ref[pl.ds(off, tq), :] += lax.dot_general(dst, kb, TN, preferred_element_type=F32)

        def loop(i, carry):
            step(i, False)
            return carry

        step(kj, True)
        lax.fori_loop(kj + 1, nq, loop, 0)
        if n:
            pl.when(jnp.logical_and(pl.program_id(0) == nh - 1, kj == nq - 1))(finish)

    full = pl.BlockSpec((None, t, da), lambda h, j: (h, 0, 0))
    blk = pl.BlockSpec((None, tq, da), lambda h, j: (h, j, 0))
    return pl.pallas_call(
        body, name=name, grid=(nh, nq),
        in_specs=[full, full, blk, blk] + [ANY] * n, out_specs=[full, blk, blk] + [ANY] * n,
        out_shape=[jax.ShapeDtypeStruct((nh, t, da), F32)] * 3 + _chip_exchange_shapes(exchange),
        scratch_shapes=_chip_exchange_sems(n) if n else [],
        compiler_params=_params("arbitrary", "arbitrary"),
    )(qa, doa, ka, va, *exchange)


def _head_group(dh, h):
    g = h // (LANES // dh)
    return slice(g * LANES, (g + 1) * LANES)


def _head_select(dh, h, to_heads):
    r = lax.broadcasted_iota(jnp.int32, (LANES, LANES), 0)
    c = lax.broadcasted_iota(jnp.int32, (LANES, LANES), 1)
    nat, col = (r, c) if to_heads else (c, r)
    return jnp.logical_and(nat == col + (h % (LANES // dh)) * dh, col < dh).astype(BF16)


def _column(x, lane, j):
    return jnp.sum(jnp.where(lane == j, x, 0.0), axis=1, keepdims=True)


def _bias_columns(lane, first, value):
    out = jnp.zeros(lane.shape, F32)
    for j, term in enumerate(_split3(value)):
        out = out + jnp.where(lane == first + j, -term.astype(F32), 0.0)
    return out


def _attn_pack(proj, cgate, w, nh, scale, name):
    t = proj.shape[0]
    dh = w // nh
    tb = _pick(t, ROW_TILES)

    def body(q_ref, k_ref, v_ref, c_ref, qa_ref, ka_ref, va_ref):
        lane = lax.broadcasted_iota(jnp.int32, (tb, LANES), 1)
        ones_qv = jnp.where(jnp.logical_and(lane >= dh, lane < dh + 3), 1.0, 0.0)
        ones_k = jnp.where(jnp.logical_and(lane >= dh + 3, lane < dh + 7), 1.0, 0.0)
        qb = (q_ref[...] * scale).astype(BF16)
        kb = k_ref[...].astype(BF16)
        vb = v_ref[...].astype(BF16)
        cblk = c_ref[...]
        for h in range(nh):
            sel, grp = _head_select(dh, h, True), _head_group(dh, h)
            qa_ref[h] = (jnp.dot(qb[:, grp], sel, preferred_element_type=F32) + ones_qv).astype(BF16)
            va_ref[h] = (jnp.dot(vb[:, grp], sel, preferred_element_type=F32) + ones_qv).astype(BF16)
            bias = _bias_columns(lane, dh, _column(cblk, lane, h))
            ka_ref[h] = (jnp.dot(kb[:, grp], sel, preferred_element_type=F32) + bias + ones_k).astype(BF16)

    col = lambda j: pl.BlockSpec((tb, w), lambda i: (i, j))
    out = pl.BlockSpec((nh, tb, LANES), lambda i: (0, i, 0))
    return pl.pallas_call(
        body, name=name, grid=(t // tb,),
        in_specs=[col(0), col(1), col(2), pl.BlockSpec((tb, LANES), lambda i: (i, 0))],
        out_specs=[out, out, out],
        out_shape=[jax.ShapeDtypeStruct((nh, t, LANES), BF16)] * 3,
        compiler_params=_params("parallel"),
    )(proj, proj, proj, cgate)


def _attn_pack_bwd(dmix, oa, qa, w, nh, name):
    t = dmix.shape[0]
    dh = w // nh
    tb = _pick(t, ROW_TILES)

    def body(d_ref, oa_ref, qa_ref, doa_ref, qa2_ref):
        lane = lax.broadcasted_iota(jnp.int32, (tb, LANES), 1)
        db = d_ref[...].astype(BF16)
        for h in range(nh):
            do_h = jnp.dot(db[:, _head_group(dh, h)], _head_select(dh, h, True), preferred_element_type=F32)
            o_h = oa_ref[h]
            delta = jnp.sum(jnp.where(lane < dh, do_h * o_h, 0.0), axis=1, keepdims=True)
            doa_ref[h] = (do_h + _bias_columns(lane, dh, delta)).astype(BF16)
            qa2_ref[h] = (qa_ref[h].astype(F32) + _bias_columns(lane, dh + 4, _column(o_h, lane, dh))).astype(BF16)

    blk = pl.BlockSpec((nh, tb, LANES), lambda i: (0, i, 0))
    return pl.pallas_call(
        body, name=name, grid=(t // tb,),
        in_specs=[pl.BlockSpec((tb, w), lambda i: (i, 0)), blk, blk], out_specs=[blk, blk],
        out_shape=[jax.ShapeDtypeStruct((nh, t, LANES), BF16)] * 2,
        compiler_params=_params("parallel"),
    )(dmix, oa, qa)


def _attn_unpack(xa, w, nh, mult, sum_col, sum_sign, name):
    t = xa.shape[1]
    dh = w // nh
    tb = _pick(t, ROW_TILES)

    def body(x_ref, o_ref, *rest):
        lane = lax.broadcasted_iota(jnp.int32, (tb, LANES), 1)
        per = LANES // dh
        cols = jnp.zeros((tb, LANES), F32)
        for h0 in range(0, nh, per):
            acc = jnp.zeros((tb, LANES), F32)
            for h in range(h0, h0 + per):
                xh = x_ref[h]
                acc = acc + jnp.dot((xh * mult).astype(BF16), _head_select(dh, h, False), preferred_element_type=F32)
                if sum_col is not None:
                    cols = cols + jnp.where(lane == h, sum_sign * _column(xh, lane, sum_col), 0.0)
            o_ref[:, _head_group(dh, h0)] = acc.astype(BF16)
        if sum_col is not None:
            rest[0][...] = cols

    nat = pl.BlockSpec((tb, w), lambda i: (i, 0))
    lanes = pl.BlockSpec((tb, LANES), lambda i: (i, 0))
    return pl.pallas_call(
        body, name=name, grid=(t // tb,),
        in_specs=[pl.BlockSpec((nh, tb, LANES), lambda i: (0, i, 0))],
        out_specs=[nat, lanes] if sum_col is not None else [nat],
        out_shape=[jax.ShapeDtypeStruct((t, w), BF16)] + ([jax.ShapeDtypeStruct((t, LANES), F32)]
                                                            if sum_col is not None else []),
        compiler_params=_params("parallel"),
    )(xa)


def _conv_fwd(proj, cw, w, bcol, name):
    t = proj.shape[0]
    tb = _pick(t, ROW_TILES)
    hb = tb // SUBLANES

    def body(b_ref, c_ref, h_ref, cp_ref, hp_ref, w_ref, y_ref):
        i = pl.program_id(0)
        zp = jnp.where(i > 0, cp_ref[...] * hp_ref[...], 0.0)
        zext = jnp.concatenate([zp, c_ref[...] * h_ref[...]], axis=0)
        z1 = pltpu.roll(zext, 1, 0)[SUBLANES:]
        z2 = pltpu.roll(zext, 2, 0)[SUBLANES:]
        y = w_ref[2:3, :] * zext[SUBLANES:] + w_ref[1:2, :] * z1 + w_ref[0:1, :] * z2
        y_ref[...] = (b_ref[...] * y).astype(BF16)

    cur = lambda j: pl.BlockSpec((tb, w), lambda i: (i, bcol + j))
    prev = lambda j: pl.BlockSpec((SUBLANES, w), lambda i: (jnp.maximum(i * hb - 1, 0), bcol + j))
    return pl.pallas_call(
        body, name=name, grid=(t // tb,),
        in_specs=[cur(0), cur(1), cur(2), prev(1), prev(2), pl.BlockSpec(cw.shape, lambda i: (0, 0))],
        out_specs=pl.BlockSpec((tb, w), lambda i: (i, 0)),
        out_shape=jax.ShapeDtypeStruct((t, w), BF16), compiler_params=_params("parallel"),
    )(proj, proj, proj, proj, proj, cw)


def _conv_bwd(proj, cw, dmix, w, bcol, name):
    t = proj.shape[0]
    tb = _pick(t, ROW_TILES)
    hb = tb // SUBLANES
    nb = t // tb
    n_ext = tb + SUBLANES

    def body(b_ref, c_ref, h_ref, cp_ref, hp_ref, bn_ref, d_ref, dn_ref, w_ref, db_ref, dc_ref, dh_ref, dw_ref):
        i = pl.program_id(0)
        c = c_ref[...]
        hh = h_ref[...]
        zp = jnp.where(i > 0, cp_ref[...] * hp_ref[...], 0.0)
        zext = jnp.concatenate([zp, c * hh], axis=0)
        z0 = zext[SUBLANES:]
        z1 = pltpu.roll(zext, 1, 0)[SUBLANES:]
        z2 = pltpu.roll(zext, 2, 0)[SUBLANES:]
        y = w_ref[2:3, :] * z0 + w_ref[1:2, :] * z1 + w_ref[0:1, :] * z2
        d = d_ref[...]
        db_ref[...] = d * y
        dy = d * b_ref[...]
        dyn = jnp.where(i < nb - 1, dn_ref[...] * bn_ref[...], 0.0)
        dext = jnp.concatenate([dy, dyn], axis=0)
        dy1 = pltpu.roll(dext, n_ext - 1, 0)[:tb]
        dy2 = pltpu.roll(dext, n_ext - 2, 0)[:tb]
        dz = w_ref[2:3, :] * dy + w_ref[1:2, :] * dy1 + w_ref[0:1, :] * dy2
        dc_ref[...] = dz * hh
        dh_ref[...] = dz * c

        @pl.when(i == 0)
        def _():
            dw_ref[...] = jnp.zeros_like(dw_ref)

        dw_ref[0:1, :] += jnp.sum(dy * z2, axis=0, keepdims=True)
        dw_ref[1:2, :] += jnp.sum(dy * z1, axis=0, keepdims=True)
        dw_ref[2:3, :] += jnp.sum(dy * z0, axis=0, keepdims=True)

    cur = lambda j: pl.BlockSpec((tb, w), lambda i: (i, bcol + j))
    prev = lambda j: pl.BlockSpec((SUBLANES, w), lambda i: (jnp.maximum(i * hb - 1, 0), bcol + j))
    nxt = lambda col: pl.BlockSpec((SUBLANES, w), lambda i: (jnp.minimum((i + 1) * hb, nb * hb - 1), col))
    out = pl.BlockSpec((tb, w), lambda i: (i, 0))
    return pl.pallas_call(
        body, name=name, grid=(nb,),
        in_specs=[cur(0), cur(1), cur(2), prev(1), prev(2), nxt(bcol),
                  pl.BlockSpec((tb, w), lambda i: (i, 1)), nxt(1), pl.BlockSpec(cw.shape, lambda i: (0, 0))],
        out_specs=[out, out, out, pl.BlockSpec(cw.shape, lambda i: (0, 0))],
        out_shape=[jax.ShapeDtypeStruct((t, w), F32)] * 3 + [jax.ShapeDtypeStruct(cw.shape, F32)],
        compiler_params=_params("arbitrary"),
    )(proj, proj, proj, proj, proj, proj, dmix, dmix, cw)


SQRT_HALF = 0.7071067811865476
INV_SQRT_2PI = 0.3989422804014327


def _gelu(x):
    return 0.5 * x * (1.0 + lax.erf(x * SQRT_HALF))


def _gelu_grad(x):
    return 0.5 * (1.0 + lax.erf(x * SQRT_HALF)) + x * (INV_SQRT_2PI * jnp.exp(-0.5 * x * x))


def _sgu_fwd(uv, ln_g, ln_b, wm, bs_full, name):
    t, d2 = uv.shape
    d = d2 // 2
    ng, pb, _ = wm.shape
    gd = d // ng
    tb = _pick(t, ROW_TILES[1:] or ROW_TILES)
    assert tb % pb == 0

    def body(uv_ref, g_ref, b_ref, w_ref, bs_ref, o_ref):
        u = _gelu(uv_ref[:, :d])
        v = _gelu(uv_ref[:, d:])
        mu = jnp.mean(v, axis=-1, keepdims=True)
        vc = v - mu
        var = jnp.mean(vc * vc, axis=-1, keepdims=True)
        vn = (vc * lax.rsqrt(var + LN_EPS) * g_ref[...] + b_ref[...]).astype(BF16)
        for r in range(tb // pb):
            rows = slice(r * pb, (r + 1) * pb)
            for gi in range(ng):
                cols = slice(gi * gd, (gi + 1) * gd)
                s = jnp.dot(w_ref[gi], vn[rows, cols], preferred_element_type=F32) + bs_ref[:, cols]
                o_ref[rows, cols] = (u[rows, cols] * s).astype(BF16)

    vec = pl.BlockSpec((1, d), lambda i: (0, 0))
    return pl.pallas_call(
        body, name=name, grid=(t // tb,),
        in_specs=[pl.BlockSpec((tb, d2), lambda i: (i, 0)), vec, vec,
                  pl.BlockSpec(wm.shape, lambda i: (0, 0, 0)), pl.BlockSpec((pb, d), lambda i: (0, 0))],
        out_specs=pl.BlockSpec((tb, d), lambda i: (i, 0)),
        out_shape=jax.ShapeDtypeStruct((t, d), BF16), compiler_params=_params("parallel"),
    )(uv, ln_g.reshape(1, d), ln_b.reshape(1, d), wm, bs_full)


def _sgu_bwd(uv, ln_g, ln_b, wm, bs_full, dgated, name):
    t, d2 = uv.shape
    d = d2 // 2
    ng, pb, _ = wm.shape
    gd = d // ng
    tb = _pick(t, ROW_TILES[1:] or ROW_TILES)
    nb = t // tb

    def body(uv_ref, g_ref, b_ref, w_ref, bs_ref, dg_ref, o_ref, dw_ref, dbs_ref, dlg_ref, dlb_ref,
             du_s, dvn_s, dbs_s):
        i = pl.program_id(0)

        @pl.when(i == 0)
        def _():
            dw_ref[...] = jnp.zeros_like(dw_ref)
            dbs_s[...] = jnp.zeros_like(dbs_s)
            dlg_ref[...] = jnp.zeros_like(dlg_ref)
            dlb_ref[...] = jnp.zeros_like(dlb_ref)

        upre = uv_ref[:, :d]
        vpre = uv_ref[:, d:]
        u = _gelu(upre)
        v = _gelu(vpre)
        mu = jnp.mean(v, axis=-1, keepdims=True)
        vc = v - mu
        var = jnp.mean(vc * vc, axis=-1, keepdims=True)
        rstd = lax.rsqrt(var + LN_EPS)
        xhat = vc * rstd
        vn = (xhat * g_ref[...] + b_ref[...]).astype(BF16)
        dgt = dg_ref[...].astype(F32)
        for r in range(tb // pb):
            rows = slice(r * pb, (r + 1) * pb)
            for gi in range(ng):
                cols = slice(gi * gd, (gi + 1) * gd)
                vblk = vn[rows, cols]
                s = jnp.dot(w_ref[gi], vblk, preferred_element_type=F32) + bs_ref[:, cols]
                dblk = dgt[rows, cols]
                du_s[rows, cols] = dblk * s
                ds = dblk * u[rows, cols]
                dsb = ds.astype(BF16)
                dvn_s[rows, cols] = lax.dot_general(w_ref[gi], dsb, (((0,), (0,)), ((), ())),
                                                    preferred_element_type=F32)
                dw_ref[gi] += lax.dot_general(dsb, vblk, (((1,), (1,)), ((), ())), preferred_element_type=F32)
                dbs_s[:, cols] += ds
        dvn = dvn_s[...]
        dlg_ref[...] += jnp.sum(dvn * xhat, axis=0, keepdims=True)
        dlb_ref[...] += jnp.sum(dvn, axis=0, keepdims=True)
        dxh = dvn * g_ref[...]
        m1 = jnp.mean(dxh, axis=-1, keepdims=True)
        m2 = jnp.mean(dxh * xhat, axis=-1, keepdims=True)
        dv = rstd * (dxh - m1 - xhat * m2)
        o_ref[:, :d] = (du_s[...] * _gelu_grad(upre)).astype(BF16)
        o_ref[:, d:] = (dv * _gelu_grad(vpre)).astype(BF16)

        @pl.when(i == nb - 1)
        def _():
            lane = lax.broadcasted_iota(jnp.int32, (pb, LANES), 1)
            acc = jnp.zeros((pb, LANES), F32)
            for gi in range(ng):
                col = jnp.sum(dbs_s[:, gi * gd:(gi + 1) * gd], axis=1, keepdims=True)
                acc = acc + jnp.where(lane == gi, col, 0.0)
            dbs_ref[...] = acc

    vec = pl.BlockSpec((1, d), lambda i: (0, 0))
    duv, dw, dbs, dlg, dlb = pl.pallas_call(
        body, name=name, grid=(nb,),
        in_specs=[pl.BlockSpec((tb, d2), lambda i: (i, 0)), vec, vec,
                  pl.BlockSpec(wm.shape, lambda i: (0, 0, 0)), pl.BlockSpec((pb, d), lambda i: (0, 0)),
                  pl.BlockSpec((tb, d), lambda i: (i, 0))],
        out_specs=[pl.BlockSpec((tb, d2), lambda i: (i, 0)), pl.BlockSpec(wm.shape, lambda i: (0, 0, 0)),
                   pl.BlockSpec((pb, LANES), lambda i: (0, 0)), vec, vec],
        out_shape=[jax.ShapeDtypeStruct((t, d2), BF16), jax.ShapeDtypeStruct(wm.shape, F32),
                   jax.ShapeDtypeStruct((pb, LANES), F32), jax.ShapeDtypeStruct((1, d), F32),
                   jax.ShapeDtypeStruct((1, d), F32)],
        scratch_shapes=[pltpu.VMEM((tb, d), F32), pltpu.VMEM((tb, d), F32), pltpu.VMEM((pb, d), F32)],
        compiler_params=_params("arbitrary"),
    )(uv, ln_g.reshape(1, d), ln_b.reshape(1, d), wm, bs_full, dgated)
    return duv, dw, dbs, dlg[0], dlb[0]


def _adamw(w, g, m, v, name):
    shape = w.shape
    cols = shape[-1]
    rows = w.size // cols
    tr = _pick(rows, (512, 256, 352, 128, 64, 32, 16, 8))

    def body(w_ref, g_ref, m_ref, v_ref, d_ref, mo_ref, vo_ref):
        d_ref[...], mo_ref[...], vo_ref[...] = _adam_update(w_ref[...], g_ref[...], m_ref[...], v_ref[...])

    spec = pl.BlockSpec((tr, cols), lambda i: (i, 0))
    outs = pl.pallas_call(
        body, name=name, grid=(rows // tr,),
        in_specs=[spec] * 4, out_specs=[spec] * 3,
        out_shape=[jax.ShapeDtypeStruct((rows, cols), F32)] * 3,
        compiler_params=_params("parallel"),
    )(*[a.reshape(rows, cols) for a in (w, g, m, v)])
    return [o.reshape(shape) for o in outs]


ANY = pl.BlockSpec(memory_space=pl.ANY)


def _place():
    return lax.axis_index("x"), lax.axis_index("y"), lax.axis_index("c")


def _all_gather(shards, name):
    n = len(shards)

    def body(*refs):
        start, forward, finish = _gather_phases(refs[:n], refs[n:2 * n], *refs[2 * n:])
        start()
        forward()
        finish()

    return pl.pallas_call(
        body, name=name, in_specs=[ANY] * n, out_specs=[ANY] * n,
        out_shape=_gather_shapes(shards), scratch_shapes=_gather_sems(n),
    )(*shards)


def _gather_shapes(shards):
    return [jax.ShapeDtypeStruct((N_DEV,) + s.shape, s.dtype) for s in shards]


def _gather_sems(n):
    return [pltpu.SemaphoreType.DMA((7 * n,)), pltpu.SemaphoreType.DMA((7 * n,)), pltpu.SemaphoreType.DMA((n,))]


def _gather_phases(x_refs, out_refs, send_sems, recv_sems, local_sems):
    n = len(x_refs)
    x, y, c = _place()
    me, sibling = (x, y, c), (x, y, 1 - c)
    chips = [(1 - x, y), (x, 1 - y), (1 - x, 1 - y)]

    def copy(a, k, block, to, own=False):
        px, py, pc = block
        rows = out_refs[a].at[4 * px + 2 * py + pc]
        return pltpu.make_async_remote_copy(
            src_ref=x_refs[a] if own else rows, dst_ref=rows,
            send_sem=send_sems.at[7 * a + k], recv_sem=recv_sems.at[7 * a + k],
            device_id=to, device_id_type=MESH)

    def local(a):
        return pltpu.make_async_copy(x_refs[a], out_refs[a].at[4 * x + 2 * y + c], local_sems.at[a])

    def first(a):
        return [copy(a, 0, me, sibling, own=True)] + [copy(a, 1 + j, me, (*chip, c), own=True)
                                                      for j, chip in enumerate(chips)]

    def start():
        for a in range(n):
            local(a).start()
            for cp in first(a):
                cp.start()

    def forward():
        for j, chip in enumerate(chips):
            for a in range(n):
                copy(a, 1 + j, (*chip, c), me).wait_recv()
                copy(a, 4 + j, (*chip, c), sibling).start()

    def finish():
        for a in range(n):
            copy(a, 0, sibling, me).wait_recv()
            for j, chip in enumerate(chips):
                copy(a, 4 + j, (*chip, 1 - c), me).wait_recv()
        for a in range(n):
            for cp in first(a) + [copy(a, 4 + j, (*chip, c), sibling) for j, chip in enumerate(chips)]:
                cp.wait_send()
            local(a).wait()

    return start, forward, finish


def _rs_sibling_exchange(packed, name):
    n = len(packed)

    def body(*refs):
        p_refs, r_refs = refs[:n], refs[n:2 * n]
        send_sems, recv_sems = refs[2 * n:]
        x, y, c = _place()
        cps = []
        for a in range(n):
            for j in range(4):
                cps.append(pltpu.make_async_remote_copy(
                    src_ref=p_refs[a].at[2 * j + (1 - c)], dst_ref=r_refs[a].at[j],
                    send_sem=send_sems.at[4 * a + j], recv_sem=recv_sems.at[4 * a + j],
                    device_id=(x, y, 1 - c), device_id_type=MESH))
        for cp in cps:
            cp.start()
        for cp in cps:
            cp.wait()

    return pl.pallas_call(
        body, name=name, in_specs=[ANY] * n, out_specs=[ANY] * n,
        out_shape=[jax.ShapeDtypeStruct((4,) + p.shape[1:], p.dtype) for p in packed],
        scratch_shapes=[pltpu.SemaphoreType.DMA((4 * n,)), pltpu.SemaphoreType.DMA((4 * n,))],
    )(*packed)


def _rs_chip_sum(packed, from_sibling, c_idx, name):
    _, r, cc = packed.shape
    tr = _pick(r, (512, 256, 352, 128))

    def body(c_ref, a_ref, b_ref, o_ref):
        o_ref[...] = (a_ref[...].astype(F32) + b_ref[...].astype(F32)).astype(o_ref.dtype)

    return pl.pallas_call(
        body, name=name,
        grid_spec=pltpu.PrefetchScalarGridSpec(
            num_scalar_prefetch=1, grid=(4, r // tr),
            in_specs=[pl.BlockSpec((None, tr, cc), lambda j, i, c_ref: (2 * j + c_ref[0], i, 0)),
                      pl.BlockSpec((None, tr, cc), lambda j, i, c_ref: (j, i, 0))],
            out_specs=pl.BlockSpec((None, tr, cc), lambda j, i, c_ref: (j, i, 0))),
        out_shape=jax.ShapeDtypeStruct((4, r, cc), packed.dtype),
        compiler_params=_params("parallel", "parallel"),
    )(c_idx, packed, from_sibling)


def _rs_chip_exchange(partial, name):
    n = len(partial)

    def body(*refs):
        start, finish = _chip_exchange_phases(refs[:n], refs[n:2 * n], *refs[2 * n:])
        start()
        finish()

    return pl.pallas_call(
        body, name=name, in_specs=[ANY] * n, out_specs=[ANY] * n,
        out_shape=_chip_exchange_shapes(partial), scratch_shapes=_chip_exchange_sems(n),
    )(*partial)


def _chip_exchange_shapes(partial):
    return [jax.ShapeDtypeStruct((3,) + p.shape[1:], p.dtype) for p in partial]


def _chip_exchange_sems(n):
    return [pltpu.SemaphoreType.DMA((3 * n,)), pltpu.SemaphoreType.DMA((3 * n,))]


def _chip_exchange_phases(p_refs, r_refs, send_sems, recv_sems):
    x, y, c = _place()
    chips = [(1 - x, y), (x, 1 - y), (1 - x, 1 - y)]

    def copies():
        return [pltpu.make_async_remote_copy(
            src_ref=p_refs[a].at[2 * tx + ty], dst_ref=r_refs[a].at[k],
            send_sem=send_sems.at[3 * a + k], recv_sem=recv_sems.at[3 * a + k],
            device_id=(tx, ty, c), device_id_type=MESH)
            for a in range(len(p_refs)) for k, (tx, ty) in enumerate(chips)]

    def start():
        for cp in copies():
            cp.start()

    def finish():
        for cp in copies():
            cp.wait()

    return start, finish


def _adam_update(w, g, m, v):
    mn = ADAM_B1 * m + (1.0 - ADAM_B1) * g
    vn = ADAM_B2 * v + (1.0 - ADAM_B2) * (g * g)
    m_hat = mn / (1.0 - ADAM_B1 ** ADAM_STEP)
    v_hat = vn / (1.0 - ADAM_B2 ** ADAM_STEP)
    return -ADAM_LR * (m_hat / (jnp.sqrt(v_hat) + ADAM_EPS) + ADAM_WD * w), mn, vn


def _rs_final_adamw(partial, received, chip_idx, w, m, v, name):
    _, r, cc = partial.shape
    tr = _pick(r, (512, 256, 352, 128))

    def body(c_ref, a_ref, r_ref, w_ref, m_ref, v_ref, g_ref, d_ref, mo_ref, vo_ref):
        g = a_ref[...].astype(F32)
        for k in range(3):
            g = g + r_ref[k].astype(F32)
        g_ref[...] = g
        d_ref[...], mo_ref[...], vo_ref[...] = _adam_update(w_ref[...], g, m_ref[...], v_ref[...])

    row = pl.BlockSpec((tr, cc), lambda i, c_ref: (i, 0))
    return pl.pallas_call(
        body, name=name,
        grid_spec=pltpu.PrefetchScalarGridSpec(
            num_scalar_prefetch=1, grid=(r // tr,),
            in_specs=[pl.BlockSpec((None, tr, cc), lambda i, c_ref: (c_ref[0], i, 0)),
                      pl.BlockSpec((3, tr, cc), lambda i, c_ref: (0, i, 0)), row, row, row],
            out_specs=[row] * 4),
        out_shape=[jax.ShapeDtypeStruct((r, cc), F32)] * 4,
        compiler_params=_params("parallel"),
    )(chip_idx, partial, received, w.reshape(r, cc), m.reshape(r, cc), v.reshape(r, cc))


def _all_reduce_small(vals, name):
    rider = _all_reduce_rider(vals)

    def body(v_ref, o_ref, *scratch):
        start, finish = rider.phases([v_ref], [o_ref], scratch)
        start()
        finish()

    return pl.pallas_call(
        body, name=name, in_specs=rider.in_specs, out_specs=rider.out_specs[0], out_shape=rider.out_shapes[0],
        scratch_shapes=rider.scratch, compiler_params=pltpu.CompilerParams(vmem_limit_bytes=VMEM_LIMIT),
    )(vals)


def _all_reduce_rider(vals):
    r, cc = vals.shape

    def phases(ins, outs, scratch):
        (v_ref,), (o_ref,), (buf, send_sems, recv_sems) = ins, outs, scratch
        x, y, c = _place()
        me = 4 * x + 2 * y + c

        def copies():
            cps = []
            for k in range(1, N_DEV):
                kx, ky, kc = (k >> 2) & 1, (k >> 1) & 1, k & 1
                peer = (1 - x if kx else x, 1 - y if ky else y, 1 - c if kc else c)
                cps.append(pltpu.make_async_remote_copy(
                    src_ref=buf.at[0], dst_ref=buf.at[k], send_sem=send_sems.at[k - 1],
                    recv_sem=recv_sems.at[k - 1], device_id=peer, device_id_type=MESH))
            return cps

        def start():
            buf[0] = v_ref[...]
            for cp in copies():
                cp.start()

        def finish():
            for cp in copies():
                cp.wait()
            acc = buf[jnp.bitwise_xor(me, 0)]
            for dev in range(1, N_DEV):
                acc = acc + buf[jnp.bitwise_xor(me, dev)]
            o_ref[...] = acc

        return start, finish

    vm = pl.BlockSpec(memory_space=pltpu.VMEM)
    return _Rider([vals], [vm], [jax.ShapeDtypeStruct((r, cc), F32)], [vm],
                  [pltpu.VMEM((N_DEV, r, cc), F32), pltpu.SemaphoreType.DMA((7,)), pltpu.SemaphoreType.DMA((7,))],
                  phases)


def _chip_exchange_rider(partial):
    n = len(partial)
    return _Rider(list(partial), [ANY] * n, _chip_exchange_shapes(partial), [ANY] * n, _chip_exchange_sems(n),
                  lambda ins, outs, scratch: _chip_exchange_phases(ins, outs, *scratch))


def _lanes(flat):
    pad = (-flat.shape[0]) % (SUBLANES * LANES)
    return jnp.pad(flat, (0, pad)).reshape(-1, LANES)


def kernel(x, even_w_in, even_b_f, even_conv_w, even_w_out, odd_w_in, odd_v_ln_g, odd_v_ln_b, odd_w_s, odd_b_s, odd_w_out, mix_ln_g, mix_ln_b, ffn_w_in, ffn_w_out, ffn_ln_g, ffn_ln_b, loss_target, m_even_w_in, m_even_b_f, m_even_conv_w, m_even_w_out, m_odd_w_in, m_odd_v_ln_g, m_odd_v_ln_b, m_odd_w_s, m_odd_b_s, m_odd_w_out, m_mix_ln_g, m_mix_ln_b, m_ffn_w_in, m_ffn_w_out, m_ffn_ln_g, m_ffn_ln_b, v_even_w_in, v_even_b_f, v_even_conv_w, v_even_w_out, v_odd_w_in, v_odd_v_ln_g, v_odd_v_ln_b, v_odd_w_s, v_odd_b_s, v_odd_w_out, v_mix_ln_g, v_mix_ln_b, v_ffn_w_in, v_ffn_w_out, v_ffn_ln_g, v_ffn_ln_b):
    t, d = x.shape[1], x.shape[2]
    nh = even_b_f.shape[-1]
    w = even_conv_w.shape[-1] * N_DEV
    dh = w // nh
    scale = dh ** -0.5
    e_in = even_w_in.shape[-1] * N_DEV
    f2 = ffn_w_in.shape[-1] * N_DEV
    f = f2 // 2
    ng, pb = odd_w_s.shape[1], odd_w_s.shape[2]
    assert e_in == 6 * w + nh and nh <= SUBLANES and (6 * w) % LANES == 0 and d % N_DEV == 0
    mx, my, mc = _place()
    me = 4 * mx + 2 * my + mc

    big = [even_w_in[0], even_w_out[0], odd_w_in[0], odd_w_out[0],
           ffn_w_in[0], ffn_w_in[1], ffn_w_out[0], ffn_w_out[1]]
    g_in0, = _all_gather([big[0].astype(BF16)], "ag_even_w_in")
    w_in0 = g_in0.transpose(1, 0, 2).reshape(d, e_in)
    w_all0 = jnp.concatenate([w_in0[:, :3 * w], w_in0[:, 3 * w + nh:], w_in0[:, 3 * w:3 * w + nh],
                              jnp.zeros((d, LANES - nh), BF16)], axis=1)

    cs, vs = even_conv_w.shape[-1], odd_v_ln_g.shape[-1]
    small_mine = jnp.concatenate([
        lax.dynamic_update_slice(jnp.zeros((3, w), F32), even_conv_w[0], (0, me * cs)).reshape(-1),
        lax.dynamic_update_slice(jnp.zeros((d,), F32), odd_v_ln_g[0], (me * vs,)),
        lax.dynamic_update_slice(jnp.zeros((d,), F32), odd_v_ln_b[0], (me * vs,))])
    small_all = _all_reduce_small(_lanes(small_mine), "ag_small").reshape(-1)
    conv_w = small_all[:3 * w].reshape(3, w)
    vln_g = small_all[3 * w:3 * w + d]
    vln_b = small_all[3 * w + d:3 * w + 2 * d]

    bf_pad = jnp.pad(even_b_f[0], (0, LANES - nh)).reshape(1, LANES)
    chunk = jnp.arange(pb) // (pb // 2)
    ws_mask = (chunk[None, :] <= chunk[:, None])[None]
    wm = jnp.where(ws_mask, odd_w_s[0], 0.0).astype(BF16)
    bs_full = jnp.repeat(odd_b_s[0].T, d // ng, axis=1)

    x0 = x[0]
    tgt = loss_target[0]
    fcol = 6 * w // LANES
    x0b = x0.astype(BF16)
    p0 = _mm(x0b, w_all0, "nn", F32, "l0_in_proj")
    cgate = _fgate_fwd(p0, bf_pad, fcol, nh, "l0_fgate")
    assert dh + 7 <= LANES
    qa, ka, va = _attn_pack(p0, cgate, w, nh, scale, "l0_attn_pack")
    oa, g_out0, g_in1, g_out1, g_fi0, g_fi1, g_fo0, g_fo1 = _attn_fwd(
        qa, ka, va, dh, "l0_attn", gather=[s.astype(BF16) for s in big[1:]])
    w_out0, w_out1 = g_out0.reshape(2 * w, d), g_out1.reshape(d, d)
    w_fo0, w_fo1 = g_fo0.reshape(f, d), g_fo1.reshape(f, d)
    nb = N_DEV // 2
    w_fi0, w_fi1 = g_fi0.reshape(2, nb, d, -1), g_fi1.reshape(2, nb, d, -1)
    attn, = _attn_unpack(oa, w, nh, 1.0, None, 1.0, "l0_attn_unpack")
    yconv = _conv_fwd(p0, conv_w, w, 3, "l0_conv")
    mix = jnp.concatenate([attn, yconv], axis=1)
    m0 = _mm(mix, w_out0, "nn", F32, "l0_out_proj")
    x1, x1b = _ln_fwd(x0, m0, mix_ln_g[0], mix_ln_b[0], "l0_mix_ln")
    h0, gu0 = _ffn_in_swiglu(x1b, w_fi0, "l0_ffn_in")
    f0 = _mm_blk_fwd(h0, w_fo0, F32, "l0_ffn_out")
    x2, x2b = _ln_fwd(x1, f0, ffn_ln_g[0], ffn_ln_b[0], "l0_ffn_ln")

    uv = _mm_cols_fwd(x2b, g_in1, False, F32, "l1_in_proj")
    gated = _sgu_fwd(uv, vln_g, vln_b, wm, bs_full, "l1_sgu")
    m1 = _mm(gated, w_out1, "nn", F32, "l1_out_proj")
    x3, x3b = _ln_fwd(x2, m1, mix_ln_g[1], mix_ln_b[1], "l1_mix_ln")
    h1, gu1 = _ffn_in_swiglu(x3b, w_fi1, "l1_ffn_in")
    f1 = _mm_blk_fwd(h1, w_fo1, F32, "l1_ffn_out")
    x4, _ = _ln_fwd(x3, f1, ffn_ln_g[1], ffn_ln_b[1], "l1_ffn_ln")
    loss_part, dy4 = _loss(x4, tgt, "loss")

    dz4, dz4b, g_ffn_g1, g_ffn_b1 = _ln_bwd(x3, f1, ffn_ln_g[1], dy4, 1.0, None, "l1_ffn_ln_bwd")
    gd_fo1 = _mm_blk_dw(h1, dz4b, BF16, "l1_ffn_out_dw").reshape(N_DEV, -1, d)
    dgu1 = _ffn_out_dx_swiglu(dz4b, w_fo1, gu1, "l1_ffn_out_dx").reshape(N_DEV, t, -1)
    gd_fi1 = _mm_cols_dw(x3b, dgu1, N_DEV, True, BF16, "l1_ffn_in_dw")
    dx3 = _mm_cols_dx(dgu1, g_fi1, True, F32, "l1_ffn_in_dx")
    dz3, dz3b, g_mix_g1, g_mix_b1 = _ln_bwd(x2, m1, mix_ln_g[1], dz4, ALPHA, dx3, "l1_mix_ln_bwd")
    gd_out1 = _mm(gated, dz3b, "tn", BF16, "l1_out_proj_dw").reshape(N_DEV, -1, d)
    dgated = _mm(dz3b, w_out1, "nt", BF16, "l1_out_proj_dx")
    duv, g_wm, g_bs_t, g_vln_g, g_vln_b = _sgu_bwd(uv, vln_g, vln_b, wm, bs_full, dgated, "l1_sgu_bwd")
    gd_in1 = _mm_cols_dw(x2b, duv, N_DEV, False, BF16, "l1_in_proj_dw")
    dx2 = _mm_cols_dx(duv, g_in1, False, F32, "l1_in_proj_dx")

    dz2, dz2b, g_ffn_g0, g_ffn_b0 = _ln_bwd(x1, f0, ffn_ln_g[0], dz3, ALPHA, dx2, "l0_ffn_ln_bwd")
    gd_fo0 = _mm_blk_dw(h0, dz2b, BF16, "l0_ffn_out_dw").reshape(N_DEV, -1, d)
    dgu0 = _ffn_out_dx_swiglu(dz2b, w_fo0, gu0, "l0_ffn_out_dx").reshape(N_DEV, t, -1)
    gd_fi0 = _mm_cols_dw(x1b, dgu0, N_DEV, True, BF16, "l0_ffn_in_dw")
    dx1 = _mm_cols_dx(dgu0, g_fi0, True, F32, "l0_ffn_in_dx")
    dz1, dz1b, g_mix_g0, g_mix_b0 = _ln_bwd(x0, m0, mix_ln_g[0], dz2, ALPHA, dx1, "l0_mix_ln_bwd")
    gd_out0 = _mm(mix, dz1b, "tn", BF16, "l0_out_proj_dw").reshape(N_DEV, -1, d)
    dmix = _mm(dz1b, w_out0, "nt", F32, "l0_out_proj_dx")
    d_b, d_c, d_h, g_conv = _conv_bwd(p0, conv_w, dmix, w, 3, "l0_conv_bwd")
    doa, qa2 = _attn_pack_bwd(dmix, oa, qa, w, nh, "l0_attn_pack_bwd")
    big_names = ["even_w_in", "even_w_out", "odd_w_in", "odd_w_out", "ffn_w_in0", "ffn_w_in1", "ffn_w_out0", "ffn_w_out1"]
    c_idx = mc.reshape(1).astype(jnp.int32)
    chip_idx = (2 * mx + my).reshape(1).astype(jnp.int32)
    early_g = [gd_out0, gd_in1, gd_out1, gd_fi0, gd_fi1, gd_fo0, gd_fo1]
    early_sib = _rs_sibling_exchange(early_g, "rs_sibling_early")
    early_partial = [_rs_chip_sum(g, s, c_idx, "rs_chip_sum_" + n)
                     for g, s, n in zip(early_g, early_sib, big_names[1:])]
    dqa, dka, dva, *early_received = _attn_bwd(qa2, ka, va, doa, "l0_attn_bwd", exchange=early_partial)
    dq, dcq = _attn_unpack(dqa, w, nh, scale, dh + 3, 1.0, "l0_attn_unpack_dq")
    dk, dck = _attn_unpack(dka, w, nh, 1.0, dh, -1.0, "l0_attn_unpack_dk")
    dv, = _attn_unpack(dva, w, nh, 1.0, None, 1.0, "l0_attn_unpack_dv")
    dzf, g_bf = _fgate_bwd(p0, bf_pad, dcq, dck, fcol, nh, "l0_fgate_bwd")
    dp0 = jnp.concatenate([dq, dk, dv, d_b.astype(BF16), d_c.astype(BF16), d_h.astype(BF16), dzf.astype(BF16)], axis=1)
    g_ws = jnp.where(ws_mask, g_wm, 0.0)
    g_bs = g_bs_t[:, :ng].T
    small_g = [g_bf[:nh], g_conv, g_vln_g, g_vln_b, g_ws, g_bs,
               jnp.stack([g_mix_g0, g_mix_g1]), jnp.stack([g_mix_b0, g_mix_b1]),
               jnp.stack([g_ffn_g0, g_ffn_g1]), jnp.stack([g_ffn_b0, g_ffn_b1])]
    small_rider = _all_reduce_rider(_lanes(jnp.concatenate([a.reshape(-1) for a in small_g])))
    g_all0, small_sum = _mm(x0b, dp0, "tn", F32, "l0_in_proj_dw", rider=small_rider)
    gd_in0 = jnp.concatenate([g_all0[:, :3 * w], g_all0[:, 6 * w:6 * w + nh], g_all0[:, 3 * w:6 * w]], axis=1)
    gd_in0 = gd_in0.reshape(d, N_DEV, -1).transpose(1, 0, 2).astype(BF16)

    big_m = [m_even_w_in[0], m_even_w_out[0], m_odd_w_in[0], m_odd_w_out[0],
             m_ffn_w_in[0], m_ffn_w_in[1], m_ffn_w_out[0], m_ffn_w_out[1]]
    big_v = [v_even_w_in[0], v_even_w_out[0], v_odd_w_in[0], v_odd_w_out[0],
             v_ffn_w_in[0], v_ffn_w_in[1], v_ffn_w_out[0], v_ffn_w_out[1]]
    late_sib = _rs_sibling_exchange([gd_in0], "rs_sibling_late")
    late_partial = [_rs_chip_sum(gd_in0, late_sib[0], c_idx, "rs_chip_sum_" + big_names[0])]
    partial = late_partial + early_partial
    dx0, *late_received = _mm(dp0, w_all0, "nt", F32, "l0_in_proj_dx", rider=_chip_exchange_rider(late_partial))
    grad_x = _axpy(ALPHA, dz1, dx0, "grad_x")
    received = list(late_received) + list(early_received)
    upd = [_rs_final_adamw(p, r, chip_idx, wt, mt, vt, "rs_final_adamw_" + n)
           for p, r, wt, mt, vt, n in zip(partial, received, big, big_m, big_v, big_names)]
    big_out = {}
    for i, n in enumerate(["even_w_in", "even_w_out", "odd_w_in", "odd_w_out"]):
        big_out[n] = [o[None] for o in upd[i]]
    big_out["ffn_w_in"] = [jnp.stack([a, b]) for a, b in zip(upd[4], upd[5])]
    big_out["ffn_w_out"] = [jnp.stack([a, b]) for a, b in zip(upd[6], upd[7])]

    small_sum = small_sum.reshape(-1)
    outs_small = []
    off = 0
    for a in small_g:
        outs_small.append(small_sum[off:off + a.size].reshape(a.shape))
        off += a.size
    gr_bf, gr_conv, gr_vg, gr_vb, gr_ws, gr_bs, gr_mg, gr_mb, gr_fg, gr_fb = outs_small

    loss = lax.psum(loss_part, ("x", "y", "c"))

    grads = {
        "even_b_f": gr_bf[None],
        "even_conv_w": lax.dynamic_slice(gr_conv, (0, me * cs), (3, cs))[None],
        "odd_v_ln_g": lax.dynamic_slice(gr_vg, (me * vs,), (vs,))[None],
        "odd_v_ln_b": lax.dynamic_slice(gr_vb, (me * vs,), (vs,))[None],
        "odd_w_s": gr_ws[None], "odd_b_s": gr_bs[None],
        "mix_ln_g": gr_mg, "mix_ln_b": gr_mb, "ffn_ln_g": gr_fg, "ffn_ln_b": gr_fb,
    }
    weights = dict(even_w_in=even_w_in, even_b_f=even_b_f, even_conv_w=even_conv_w, even_w_out=even_w_out,
                   odd_w_in=odd_w_in, odd_v_ln_g=odd_v_ln_g, odd_v_ln_b=odd_v_ln_b, odd_w_s=odd_w_s,
                   odd_b_s=odd_b_s, odd_w_out=odd_w_out, mix_ln_g=mix_ln_g, mix_ln_b=mix_ln_b,
                   ffn_w_in=ffn_w_in, ffn_w_out=ffn_w_out, ffn_ln_g=ffn_ln_g, ffn_ln_b=ffn_ln_b)
    moms = dict(even_w_in=(m_even_w_in, v_even_w_in), even_b_f=(m_even_b_f, v_even_b_f),
                even_conv_w=(m_even_conv_w, v_even_conv_w), even_w_out=(m_even_w_out, v_even_w_out),
                odd_w_in=(m_odd_w_in, v_odd_w_in), odd_v_ln_g=(m_odd_v_ln_g, v_odd_v_ln_g),
                odd_v_ln_b=(m_odd_v_ln_b, v_odd_v_ln_b), odd_w_s=(m_odd_w_s, v_odd_w_s),
                odd_b_s=(m_odd_b_s, v_odd_b_s), odd_w_out=(m_odd_w_out, v_odd_w_out),
                mix_ln_g=(m_mix_ln_g, v_mix_ln_g), mix_ln_b=(m_mix_ln_b, v_mix_ln_b),
                ffn_w_in=(m_ffn_w_in, v_ffn_w_in), ffn_w_out=(m_ffn_w_out, v_ffn_w_out),
                ffn_ln_g=(m_ffn_ln_g, v_ffn_ln_g), ffn_ln_b=(m_ffn_ln_b, v_ffn_ln_b))
    names = list(weights)
    gout, deltas, new_m, new_v = [], [], [], []
    for n in names:
        if n in big_out:
            gr, dlt, mn, vn = big_out[n]
        else:
            gr = grads[n]
            dlt, mn, vn = _adamw(weights[n], gr, moms[n][0], moms[n][1], "adamw_" + n)
        gout.append(gr.reshape(weights[n].shape))
        deltas.append(dlt.reshape(weights[n].shape))
        new_m.append(mn.reshape(weights[n].shape))
        new_v.append(vn.reshape(weights[n].shape))
    return (loss, grad_x[None], *gout, *deltas, *new_m, *new_v)
```

```python
import functools
from typing import Callable, NamedTuple

import jax
import jax.numpy as jnp
from jax import lax
from jax.experimental import pallas as pl
from jax.experimental.pallas import tpu as pltpu

F32 = jnp.float32
BF16 = jnp.bfloat16
MESH = pl.DeviceIdType.MESH

DEPTH = 2
ALPHA = (2.0 * DEPTH) ** 0.25
LN_EPS = 1e-5
ADAM_LR = 0.001
ADAM_B1 = 0.9
ADAM_B2 = 0.999
ADAM_EPS = 1e-08
ADAM_WD = 0.01
ADAM_STEP = 10

N_DEV = 8
LANES = 128
SUBLANES = 8
VMEM_LIMIT = 48 * 1024 * 1024
NEG_BIG = -1e30
ROW_TILES = (512, 256, 128)


def _pick(n, cands):
    for c in cands:
        if c <= n and n % c == 0:
            return c
    return n


def _params(*sem):
    return pltpu.CompilerParams(dimension_semantics=sem, vmem_limit_bytes=VMEM_LIMIT)


NN = (((1,), (0,)), ((), ()))
NT = (((1,), (1,)), ((), ()))
TN = (((0,), (0,)), ((), ()))
M_TILES = (1024, 512, 1408, 256, 128)
N_TILES = (512, 640, 256, 128)
K_TILES = (2048, 1024, 512, 640, 1408, 256, 128)
K_WHOLE = 3328


class _Rider(NamedTuple):
    inputs: list
    in_specs: list
    out_shapes: list
    out_specs: list
    scratch: list
    phases: Callable


def _mm_core(name, grid, a, b, a_spec, b_spec, o_spec, o_shape, o_dtype, dims, tile, pieces=None, rider=None):
    nred = grid[2]
    pieces = pieces or [(lambda r: r[...], lambda r: r[...])]
    ni = len(rider.inputs) if rider else 0
    no = len(rider.out_shapes) if rider else 0
    nacc = 0 if nred == 1 else 1

    def body(a_ref, b_ref, *rest):
        o_ref = rest[ni]
        if rider:
            start, finish = rider.phases(rest[:ni], rest[ni + 1:ni + 1 + no], rest[ni + 1 + no + nacc:])
            ids = [pl.program_id(ax) for ax in range(3)]
            first = functools.reduce(jnp.logical_and, [i == 0 for i in ids])
            last = functools.reduce(jnp.logical_and, [i == g - 1 for i, g in zip(ids, grid)])
            pl.when(first)(start)
        part = None
        for fa, fb in pieces:
            prod = lax.dot_general(fa(a_ref).astype(BF16), fb(b_ref).astype(BF16), dims, preferred_element_type=F32)
            part = prod if part is None else part + prod
        if nred == 1:
            o_ref[...] = part.astype(o_ref.dtype)
        else:
            acc_ref = rest[ni + 1 + no]
            kk = pl.program_id(2)

            @pl.when(kk == 0)
            def _():
                acc_ref[...] = jnp.zeros_like(acc_ref)

            acc_ref[...] += part

            @pl.when(kk == nred - 1)
            def _():
                o_ref[...] = acc_ref[...].astype(o_ref.dtype)
        if rider:
            pl.when(last)(finish)

    out = pl.pallas_call(
        body, name=name, grid=grid,
        in_specs=[a_spec, b_spec] + (rider.in_specs if rider else []),
        out_specs=[o_spec] + (rider.out_specs if rider else []),
        out_shape=[jax.ShapeDtypeStruct(o_shape, o_dtype)] + (rider.out_shapes if rider else []),
        scratch_shapes=([] if nred == 1 else [pltpu.VMEM(tile, F32)]) + (rider.scratch if rider else []),
        compiler_params=_params(*(["arbitrary"] * 3 if rider else ["parallel", "parallel", "arbitrary"])),
    )(a, b, *(rider.inputs if rider else []))
    return out if rider else out[0]


def _mm(a, b, mode, out_dtype, name, rider=None):
    if mode == "nn":
        (m, k), (k2, n) = a.shape, b.shape
    elif mode == "nt":
        (m, k), (n, k2) = a.shape, b.shape
    else:
        (k, m), (k2, n) = a.shape, b.shape
    assert k == k2, (a.shape, b.shape, mode)
    tm, tn = _pick(m, M_TILES), _pick(n, N_TILES)
    tk = k if k <= K_WHOLE else _pick(k, K_TILES)
    if mode == "nn":
        a_spec = pl.BlockSpec((tm, tk), lambda i, j, kk: (i, kk))
        b_spec = pl.BlockSpec((tk, tn), lambda i, j, kk: (kk, j))
        dims = NN
    elif mode == "nt":
        a_spec = pl.BlockSpec((tm, tk), lambda i, j, kk: (i, kk))
        b_spec = pl.BlockSpec((tn, tk), lambda i, j, kk: (j, kk))
        dims = NT
    else:
        a_spec = pl.BlockSpec((tk, tm), lambda i, j, kk: (kk, i))
        b_spec = pl.BlockSpec((tk, tn), lambda i, j, kk: (kk, j))
        dims = TN
    return _mm_core(name, (m // tm, n // tn, k // tk), a, b, a_spec, b_spec,
                    pl.BlockSpec((tm, tn), lambda i, j, kk: (i, j)), (m, n), out_dtype, dims, (tm, tn), rider=rider)


def _act_spec(blocked, rows, ns, row_ax, d_ax):
    if blocked:
        return pl.BlockSpec((None, rows, ns), lambda *g: (g[d_ax], g[row_ax], 0))
    return pl.BlockSpec((rows, ns), lambda *g: (g[row_ax], g[d_ax]))


def _mm_cols_fwd(a, g3, blocked, out_dtype, name):
    (t, k), (nd, k2, ns) = a.shape, g3.shape
    assert k == k2
    tm, tk = _pick(t, M_TILES), _pick(k, K_TILES)
    return _mm_core(name, (t // tm, nd, k // tk), a, g3,
                    pl.BlockSpec((tm, tk), lambda i, d, kk: (i, kk)),
                    pl.BlockSpec((None, tk, ns), lambda i, d, kk: (d, kk, 0)),
                    _act_spec(blocked, tm, ns, 0, 1), (nd, t, ns) if blocked else (t, nd * ns), out_dtype, NN, (tm, ns))


def _mm_cols_dx(dy, g3, blocked, out_dtype, name):
    nd, k, ns = g3.shape
    t = dy.shape[1] if blocked else dy.shape[0]
    tm, tn = _pick(t, M_TILES), _pick(k, (1024,) + N_TILES)
    o_spec = pl.BlockSpec((tm, tn), lambda i, j, d: (i, j))
    if not blocked:
        whole_b = lambda r: jnp.concatenate([r[s] for s in range(nd)], axis=1)
        return _mm_core(name, (t // tm, k // tn, 1), dy, g3,
                        pl.BlockSpec((tm, nd * ns), lambda i, j, d: (i, 0)),
                        pl.BlockSpec((nd, tn, ns), lambda i, j, d: (0, j, 0)),
                        o_spec, (t, k), out_dtype, NT, (tm, tn), pieces=[(lambda r: r[...], whole_b)])
    grp = 2 if nd % 2 == 0 else 1
    pieces = [(lambda r, s=s: r[s], lambda r, s=s: r[s]) for s in range(grp)]
    return _mm_core(name, (t // tm, k // tn, nd // grp), dy, g3,
                    pl.BlockSpec((grp, tm, ns), lambda i, j, d: (d, i, 0)),
                    pl.BlockSpec((grp, tn, ns), lambda i, j, d: (d, j, 0)),
                    o_spec, (t, k), out_dtype, NT, (tm, tn), pieces=pieces)


def _mm_cols_dw(a, dy, nd, blocked, out_dtype, name):
    t, k = a.shape
    ns = dy.shape[2] if blocked else dy.shape[1] // nd
    tmk, tk = _pick(k, M_TILES), _pick(t, K_TILES)
    return _mm_core(name, (nd, k // tmk, t // tk), a, dy,
                    pl.BlockSpec((tk, tmk), lambda d, j, kk: (kk, j)),
                    _act_spec(blocked, tk, ns, 2, 0),
                    pl.BlockSpec((None, tmk, ns), lambda d, j, kk: (d, j, 0)), (nd, k, ns), out_dtype, TN, (tmk, ns))


def _mm_blk_dw(h3, dz, out_dtype, name):
    (nb, t, ns), (_, n) = h3.shape, dz.shape
    tn, tk = _pick(n, (1024,) + N_TILES), _pick(t, K_TILES)
    return _mm_core(name, (nb, n // tn, t // tk), h3, dz,
                    pl.BlockSpec((None, tk, ns), lambda d, j, kk: (d, kk, 0)),
                    pl.BlockSpec((tk, tn), lambda d, j, kk: (kk, j)),
                    pl.BlockSpec((ns, tn), lambda d, j, kk: (d, j)), (nb * ns, n), out_dtype, TN, (ns, tn))


def _mm_ln(a, w, xa, g, b, name, target=None):
    blocked = a.ndim == 3
    t, d = xa.shape
    k = w.shape[0]
    tm = _pick(t, ROW_TILES)
    nb = a.shape[0] if blocked else 1
    ns = k // nb
    halves = [slice(0, tm // 2), slice(tm // 2, tm)] if tm % 32 == 0 else [slice(0, tm)]

    def body(a_ref, w_ref, xa_ref, g_ref, b_ref, *rest):
        def product(rows):
            if not blocked:
                return jnp.dot(a_ref[rows, :], w_ref[...], preferred_element_type=F32)
            acc = None
            for s in range(nb):
                prod = jnp.dot(a_ref[s, rows, :], w_ref[s * ns:(s + 1) * ns, :], preferred_element_type=F32)
                acc = prod if acc is None else acc + prod
            return acc

        if target is not None:
            t_ref, xb_ref, dy_ref, l_ref = rest

            @pl.when(pl.program_id(0) == 0)
            def _():
                l_ref[...] = jnp.zeros_like(l_ref)
        else:
            xb_ref, y_ref, yb_ref = rest
        for rows, xb in zip(halves, [product(rows) for rows in halves]):
            xb_ref[rows, :] = xb
            z = ALPHA * xa_ref[rows, :] + xb
            mu = jnp.mean(z, axis=-1, keepdims=True)
            zc = z - mu
            var = jnp.mean(zc * zc, axis=-1, keepdims=True)
            y = zc * lax.rsqrt(var + LN_EPS) * g_ref[...] + b_ref[...]
            if target is not None:
                e = y - t_ref[rows, :]
                dy_ref[rows, :] = e * (1.0 / d)
                l_ref[...] += 0.5 * jnp.sum(jnp.mean(e * e, axis=-1, keepdims=True))
            else:
                y_ref[rows, :] = y
                yb_ref[rows, :] = y.astype(BF16)

    row = pl.BlockSpec((tm, d), lambda i: (i, 0))
    vec = pl.BlockSpec((1, d), lambda i: (0, 0))
    a_spec = pl.BlockSpec((nb, tm, ns), lambda i: (0, i, 0)) if blocked else pl.BlockSpec((tm, k), lambda i: (i, 0))
    ins = [a, w, xa, g.reshape(1, d), b.reshape(1, d)]
    in_specs = [a_spec, pl.BlockSpec((k, d), lambda i: (0, 0)), row, vec, vec]
    if target is not None:
        xb, dy, l = pl.pallas_call(
            body, name=name, grid=(t // tm,), in_specs=in_specs + [row],
            out_specs=[row, row, pl.BlockSpec((1, LANES), lambda i: (0, 0))],
            out_shape=[jax.ShapeDtypeStruct((t, d), F32)] * 2 + [jax.ShapeDtypeStruct((1, LANES), F32)],
            compiler_params=_params("arbitrary"),
        )(*ins, target)
        return xb, dy, l[0, 0]
    return pl.pallas_call(
        body, name=name, grid=(t // tm,), in_specs=in_specs, out_specs=[row, row, row],
        out_shape=[jax.ShapeDtypeStruct((t, d), F32)] * 2 + [jax.ShapeDtypeStruct((t, d), BF16)],
        compiler_params=_params("parallel"),
    )(*ins)


def _ln_bwd(xa, xb, g, dya, ca, dyb, name):
    t, d = xa.shape
    tb = _pick(t, ROW_TILES)
    two = dyb is not None

    def body(*refs):
        if two:
            xa_ref, xb_ref, g_ref, dya_ref, dyb_ref, dz_ref, dzb_ref, dg_ref, db_ref = refs
            dy = ca * dya_ref[...] + dyb_ref[...]
        else:
            xa_ref, xb_ref, g_ref, dya_ref, dz_ref, dzb_ref, dg_ref, db_ref = refs
            dy = ca * dya_ref[...]
        z = ALPHA * xa_ref[...] + xb_ref[...]
        mu = jnp.mean(z, axis=-1, keepdims=True)
        zc = z - mu
        var = jnp.mean(zc * zc, axis=-1, keepdims=True)
        rstd = lax.rsqrt(var + LN_EPS)
        xhat = zc * rstd
        dxh = dy * g_ref[...]
        m1 = jnp.mean(dxh, axis=-1, keepdims=True)
        m2 = jnp.mean(dxh * xhat, axis=-1, keepdims=True)
        dz = rstd * (dxh - m1 - xhat * m2)
        dz_ref[...] = dz
        dzb_ref[...] = dz.astype(BF16)

        @pl.when(pl.program_id(0) == 0)
        def _():
            dg_ref[...] = jnp.zeros_like(dg_ref)
            db_ref[...] = jnp.zeros_like(db_ref)

        dg_ref[...] += jnp.sum(dy * xhat, axis=0, keepdims=True)
        db_ref[...] += jnp.sum(dy, axis=0, keepdims=True)

    row = pl.BlockSpec((tb, d), lambda i: (i, 0))
    vec = pl.BlockSpec((1, d), lambda i: (0, 0))
    ins = [xa, xb, g.reshape(1, d), dya] + ([dyb] if two else [])
    dz, dzb, dg, db = pl.pallas_call(
        body, name=name, grid=(t // tb,),
        in_specs=[row, row, vec, row] + ([row] if two else []),
        out_specs=[row, row, vec, vec],
        out_shape=[jax.ShapeDtypeStruct((t, d), F32), jax.ShapeDtypeStruct((t, d), BF16),
                   jax.ShapeDtypeStruct((1, d), F32), jax.ShapeDtypeStruct((1, d), F32)],
        compiler_params=_params("arbitrary"),
    )(*ins)
    return dz, dzb, dg[0], db[0]


def _axpy(ca, a, b, name):
    t, d = a.shape
    tb = _pick(t, ROW_TILES)

    def body(a_ref, b_ref, o_ref):
        o_ref[...] = ca * a_ref[...] + b_ref[...]

    row = pl.BlockSpec((tb, d), lambda i: (i, 0))
    return pl.pallas_call(
        body, name=name, grid=(t // tb,), in_specs=[row, row], out_specs=row,
        out_shape=jax.ShapeDtypeStruct((t, d), F32), compiler_params=_params("parallel"),
    )(a, b)


def _ffn_in_swiglu(xb, g4, name):
    (t, k), (_, nb, _, ns) = xb.shape, g4.shape
    tm = _pick(t, M_TILES)

    def body(x_ref, w_ref, h_ref, gu_ref):
        xv = x_ref[...]
        gate = jnp.dot(xv, w_ref[0], preferred_element_type=F32)
        up = jnp.dot(xv, w_ref[1], preferred_element_type=F32)
        h_ref[...] = (gate * jax.nn.sigmoid(gate) * up).astype(BF16)
        gu_ref[0] = gate.astype(BF16)
        gu_ref[1] = up.astype(BF16)

    return pl.pallas_call(
        body, name=name, grid=(t // tm, nb),
        in_specs=[pl.BlockSpec((tm, k), lambda i, d: (i, 0)),
                  pl.BlockSpec((2, None, k, ns), lambda i, d: (0, d, 0, 0))],
        out_specs=[pl.BlockSpec((None, tm, ns), lambda i, d: (d, i, 0)),
                   pl.BlockSpec((2, None, tm, ns), lambda i, d: (0, d, i, 0))],
        out_shape=[jax.ShapeDtypeStruct((nb, t, ns), BF16), jax.ShapeDtypeStruct((2, nb, t, ns), BF16)],
        compiler_params=_params("parallel", "parallel"),
    )(xb, g4)


def _ffn_out_dx_swiglu(dz, w_out, gu4, name):
    (t, d), (_, nb, _, ns) = dz.shape, gu4.shape
    tm = _pick(t, M_TILES)

    def body(dz_ref, w_ref, gu_ref, o_ref):
        halves = [slice(0, tm // 2), slice(tm // 2, tm)] if tm % 16 == 0 else [slice(0, tm)]
        dhs = [lax.dot_general(dz_ref[rows, :].astype(BF16), w_ref[...], NT, preferred_element_type=F32)
               for rows in halves]
        for rows, dh in zip(halves, dhs):
            gate = gu_ref[0, rows, :].astype(F32)
            up = gu_ref[1, rows, :].astype(F32)
            sg = jax.nn.sigmoid(gate)
            silu = gate * sg
            o_ref[0, rows, :] = (dh * up * (sg + silu * (1.0 - sg))).astype(BF16)
            o_ref[1, rows, :] = (dh * silu).astype(BF16)

    blk = pl.BlockSpec((2, None, tm, ns), lambda i, j: (0, j, i, 0))
    return pl.pallas_call(
        body, name=name, grid=(t // tm, nb),
        in_specs=[pl.BlockSpec((tm, d), lambda i, j: (i, 0)), pl.BlockSpec((ns, d), lambda i, j: (j, 0)), blk],
        out_specs=blk,
        out_shape=jax.ShapeDtypeStruct((2, nb, t, ns), BF16),
        compiler_params=_params("parallel", "parallel"),
    )(dz, w_out, gu4)


def _tri_matmul(tri, x):
    x1 = x.astype(BF16)
    r1 = x - x1.astype(F32)
    x2 = r1.astype(BF16)
    x3 = (r1 - x2.astype(F32)).astype(BF16)
    dot = lambda v: jnp.dot(tri, v, preferred_element_type=F32)
    return dot(x1) + dot(x2) + dot(x3)


def _fgate_fwd(proj, bf_pad, fcol, n_heads, name):
    t = proj.shape[0]
    tb = _pick(t, ROW_TILES)

    def body(p_ref, b_ref, c_ref, carry):
        @pl.when(pl.program_id(0) == 0)
        def _():
            carry[...] = jnp.zeros_like(carry)

        z = p_ref[...] + b_ref[...]
        lf = jnp.minimum(z, 0.0) - jnp.log1p(jnp.exp(-jnp.abs(z)))
        lane = lax.broadcasted_iota(jnp.int32, (tb, LANES), 1)
        lf = jnp.where(lane < n_heads, lf, 0.0)
        r = lax.broadcasted_iota(jnp.int32, (tb, tb), 0)
        s = lax.broadcasted_iota(jnp.int32, (tb, tb), 1)
        tri = (s <= r).astype(BF16)
        c = _tri_matmul(tri, lf) + carry[...]
        c_ref[...] = c
        carry[...] = c[tb - 1:tb, :]

    return pl.pallas_call(
        body, name=name, grid=(t // tb,),
        in_specs=[pl.BlockSpec((tb, LANES), lambda i: (i, fcol)), pl.BlockSpec((1, LANES), lambda i: (0, 0))],
        out_specs=pl.BlockSpec((tb, LANES), lambda i: (i, 0)),
        out_shape=jax.ShapeDtypeStruct((t, LANES), F32),
        scratch_shapes=[pltpu.VMEM((1, LANES), F32)],
        compiler_params=_params("arbitrary"),
    )(proj, bf_pad)


def _fgate_bwd(proj, bf_pad, dcq, dck, fcol, n_heads, name):
    t = proj.shape[0]
    tb = _pick(t, ROW_TILES)
    nb = t // tb

    def body(p_ref, b_ref, dcq_ref, dck_ref, dz_ref, db_ref, carry):
        @pl.when(pl.program_id(0) == 0)
        def _():
            carry[...] = jnp.zeros_like(carry)
            db_ref[...] = jnp.zeros_like(db_ref)

        r = lax.broadcasted_iota(jnp.int32, (tb, tb), 0)
        s = lax.broadcasted_iota(jnp.int32, (tb, tb), 1)
        tri = (s >= r).astype(BF16)
        dlf = _tri_matmul(tri, dcq_ref[...] + dck_ref[...]) + carry[...]
        carry[...] = dlf[0:1, :]
        z = p_ref[...] + b_ref[...]
        lane = lax.broadcasted_iota(jnp.int32, (tb, LANES), 1)
        dz = jnp.where(lane < n_heads, dlf * jax.nn.sigmoid(-z), 0.0)
        dz_ref[...] = dz
        db_ref[...] += jnp.sum(dz, axis=0, keepdims=True)

    dz, db = pl.pallas_call(
        body, name=name, grid=(nb,),
        in_specs=[pl.BlockSpec((tb, LANES), lambda i: (nb - 1 - i, fcol)),
                  pl.BlockSpec((1, LANES), lambda i: (0, 0)),
                  pl.BlockSpec((tb, LANES), lambda i: (nb - 1 - i, 0)),
                  pl.BlockSpec((tb, LANES), lambda i: (nb - 1 - i, 0))],
        out_specs=[pl.BlockSpec((tb, LANES), lambda i: (nb - 1 - i, 0)),
                   pl.BlockSpec((1, LANES), lambda i: (0, 0))],
        out_shape=[jax.ShapeDtypeStruct((t, LANES), F32), jax.ShapeDtypeStruct((1, LANES), F32)],
        scratch_shapes=[pltpu.VMEM((1, LANES), F32)],
        compiler_params=_params("arbitrary"),
    )(proj, bf_pad, dcq, dck)
    return dz, db[0]


def _split3(x):
    hi = x.astype(BF16)
    r = x - hi.astype(F32)
    mid = r.astype(BF16)
    return hi, mid, (r - mid.astype(F32)).astype(BF16)


def _attn_fwd(qa, ka, va, dh, name, gather=()):
    nh, t, da = qa.shape
    tq = _pick(t, ROW_TILES)
    hb = 2 if nh % 2 == 0 else 1
    heads = range(hb)
    n = len(gather)
    steps = (nh // hb, t // tq)

    def body(q_ref, k_ref, v_ref, *rest):
        x_refs, o_ref, g_refs = rest[:n], rest[n], rest[n + 1:2 * n + 1]
        m_s, acc_s, s_a, s_b = rest[2 * n + 1:2 * n + 5]
        qi = pl.program_id(1)
        if n:
            start, forward, finish = _gather_phases(x_refs, g_refs, *rest[2 * n + 5:])
            at = lambda hh, qq: jnp.logical_and(pl.program_id(0) == hh, qi == qq)
            pl.when(at(0, 0))(start)
            pl.when(at(steps[0] // 2, 0))(forward)
        m_s[...] = jnp.full(m_s.shape, NEG_BIG, F32)
        acc_s[...] = jnp.zeros_like(acc_s)

        def scores(s_ref, j):
            off = pl.multiple_of(j * tq, tq)
            for g in heads:
                s_ref[g] = lax.dot_general(q_ref[g], k_ref[g, pl.ds(off, tq), :], NT, preferred_element_type=F32)

        def absorb(s_ref, j, diagonal):
            off = pl.multiple_of(j * tq, tq)
            s = [s_ref[g] for g in heads]
            if diagonal:
                row = lax.broadcasted_iota(jnp.int32, (tq, tq), 0)
                col = lax.broadcasted_iota(jnp.int32, (tq, tq), 1)
                s = [jnp.where(col > row, NEG_BIG, sg) for sg in s]
            m_prev = [m_s[g] for g in heads]
            m_new = [jnp.maximum(m_prev[g], jnp.max(s[g], axis=1, keepdims=True)) for g in heads]
            p = [jnp.exp(s[g] - m_new[g]).astype(BF16) for g in heads]
            pv = [jnp.dot(p[g], v_ref[g, pl.ds(off, tq), :], preferred_element_type=F32) for g in heads]
            for g in heads:
                acc_s[g] = jnp.exp(m_prev[g] - m_new[g]) * acc_s[g] + pv[g]
                m_s[g] = m_new[g]

        def two_blocks(r, carry):
            scores(s_b, 2 * r + 1)
            absorb(s_a, 2 * r, False)
            scores(s_a, 2 * r + 2)
            absorb(s_b, 2 * r + 1, False)
            return carry

        scores(s_a, 0)
        rounds = qi // 2
        lax.fori_loop(0, rounds, two_blocks, 0)

        @pl.when(qi % 2 == 0)
        def _():
            absorb(s_a, qi, True)

        @pl.when(qi % 2 == 1)
        def _():
            scores(s_b, qi)
            absorb(s_a, qi - 1, False)
            absorb(s_b, qi, True)

        lane = lax.broadcasted_iota(jnp.int32, (tq, da), 1)
        for g in heads:
            acc = acc_s[g]
            l = jnp.sum(jnp.where(lane == dh, acc, 0.0), axis=1, keepdims=True)
            o_ref[g] = jnp.where(lane == dh, m_s[g] + jnp.log(l), acc / l)
        if n:
            pl.when(at(steps[0] - 1, steps[1] - 1))(finish)

    full = pl.BlockSpec((hb, t, da), lambda h, qi: (h, 0, 0))
    blk = pl.BlockSpec((hb, tq, da), lambda h, qi: (h, qi, 0))
    return pl.pallas_call(
        body, name=name, grid=steps,
        in_specs=[blk, full, full] + [ANY] * n, out_specs=[blk] + [ANY] * n,
        out_shape=[jax.ShapeDtypeStruct((nh, t, da), F32)] + _gather_shapes(gather),
        scratch_shapes=[pltpu.VMEM((hb, tq, 1), F32), pltpu.VMEM((hb, tq, da), F32),
                        pltpu.VMEM((hb, tq, tq), F32), pltpu.VMEM((hb, tq, tq), F32)] + (_gather_sems(n) if n else []),
        compiler_params=_params("arbitrary", "arbitrary"),
    )(qa, ka, va, *gather)


def _attn_bwd(qa, ka, va, doa, name, exchange=()):
    nh, t, da = qa.shape
    tq = _pick(t, ROW_TILES)
    nq = t // tq
    n = len(exchange)

    def body(q_ref, do_ref, k_ref, v_ref, *rest):
        p_refs, (dq_ref, dk_ref, dv_ref), r_refs = rest[:n], rest[n:n + 3], rest[n + 3:2 * n + 3]
        kj = pl.program_id(1)
        if n:
            start, finish = _chip_exchange_phases(p_refs, r_refs, *rest[2 * n + 3:])
            pl.when(jnp.logical_and(pl.program_id(0) == 0, kj == 0))(start)

        @pl.when(kj == 0)
        def _():
            dq_ref[...] = jnp.zeros_like(dq_ref)

        dk_ref[...] = jnp.zeros_like(dk_ref)
        dv_ref[...] = jnp.zeros_like(dv_ref)
        kb = k_ref[...]
        vb = v_ref[...]

        def step(i, diagonal, blocks=1):
            off = pl.multiple_of(i * tq, tq)
            rows = blocks * tq
            qb = q_ref[pl.ds(off, rows), :]
            dob = do_ref[pl.ds(off, rows), :]
            st = lax.dot_general(kb, qb, NT, preferred_element_type=F32)
            if diagonal:
                row = lax.broadcasted_iota(jnp.int32, (tq, tq), 0)
                col = lax.broadcasted_iota(jnp.int32, (tq, tq), 1)
                st = jnp.where(row > col, NEG_BIG, st)
            pt = jnp.exp(st)
            dst = (pt * lax.dot_general(vb, dob, NT, preferred_element_type=F32)).astype(BF16)
            dv_ref[...] += jnp.dot(pt.astype(BF16), dob, preferred_element_type=F32)
            dk_ref[...] += jnp.dot(dst, qb, preferred_element_type=F32)
            dq_ref[pl.ds(off, rows), :] += lax.dot_general(dst, kb, TN, preferred_element_type=F32)

        step(kj, True)
        odd = (nq - 1 - kj) % 2

        @pl.when(odd == 1)
        def _():
            step(kj + 1, False)

        def loop(r, carry):
            step(kj + 1 + odd + 2 * r, False, blocks=2)
            return carry

        lax.fori_loop(0, (nq - 1 - kj) // 2, loop, 0)
        if n:
            pl.when(jnp.logical_and(pl.program_id(0) == nh - 1, kj == nq - 1))(finish)

    full = pl.BlockSpec((None, t, da), lambda h, j: (h, 0, 0))
    blk = pl.BlockSpec((None, tq, da), lambda h, j: (h, j, 0))
    return pl.pallas_call(
        body, name=name, grid=(nh, nq),
        in_specs=[full, full, blk, blk] + [ANY] * n, out_specs=[full, blk, blk] + [ANY] * n,
        out_shape=[jax.ShapeDtypeStruct((nh, t, da), F32)] * 3 + _chip_exchange_shapes(exchange),
        scratch_shapes=_chip_exchange_sems(n) if n else [],
        compiler_params=_params("arbitrary", "arbitrary"),
    )(qa, doa, ka, va, *exchange)


def _head_group(dh, h):
    g = h // (LANES // dh)
    return slice(g * LANES, (g + 1) * LANES)


def _head_select(dh, h, to_heads):
    r = lax.broadcasted_iota(jnp.int32, (LANES, LANES), 0)
    c = lax.broadcasted_iota(jnp.int32, (LANES, LANES), 1)
    nat, col = (r, c) if to_heads else (c, r)
    return jnp.logical_and(nat == col + (h % (LANES // dh)) * dh, col < dh).astype(BF16)


def _column(x, lane, j):
    return jnp.sum(jnp.where(lane == j, x, 0.0), axis=1, keepdims=True)


def _bias_columns(lane, first, value):
    out = jnp.zeros(lane.shape, F32)
    for j, term in enumerate(_split3(value)):
        out = out + jnp.where(lane == first + j, -term.astype(F32), 0.0)
    return out


def _attn_pack(proj, cgate, w, nh, scale, name):
    t = proj.shape[0]
    dh = w // nh
    tb = _pick(t, ROW_TILES)

    def body(q_ref, k_ref, v_ref, c_ref, qa_ref, ka_ref, va_ref):
        lane = lax.broadcasted_iota(jnp.int32, (tb, LANES), 1)
        ones_qv = jnp.where(jnp.logical_and(lane >= dh, lane < dh + 3), 1.0, 0.0)
        ones_k = jnp.where(jnp.logical_and(lane >= dh + 3, lane < dh + 7), 1.0, 0.0)
        qb = (q_ref[...] * scale).astype(BF16)
        kb = k_ref[...].astype(BF16)
        vb = v_ref[...].astype(BF16)
        cblk = c_ref[...]
        for h in range(nh):
            sel, grp = _head_select(dh, h, True), _head_group(dh, h)
            qa_ref[h] = (jnp.dot(qb[:, grp], sel, preferred_element_type=F32) + ones_qv).astype(BF16)
            va_ref[h] = (jnp.dot(vb[:, grp], sel, preferred_element_type=F32) + ones_qv).astype(BF16)
            bias = _bias_columns(lane, dh, _column(cblk, lane, h))
            ka_ref[h] = (jnp.dot(kb[:, grp], sel, preferred_element_type=F32) + bias + ones_k).astype(BF16)

    col = lambda j: pl.BlockSpec((tb, w), lambda i: (i, j))
    out = pl.BlockSpec((nh, tb, LANES), lambda i: (0, i, 0))
    return pl.pallas_call(
        body, name=name, grid=(t // tb,),
        in_specs=[col(0), col(1), col(2), pl.BlockSpec((tb, LANES), lambda i: (i, 0))],
        out_specs=[out, out, out],
        out_shape=[jax.ShapeDtypeStruct((nh, t, LANES), BF16)] * 3,
        compiler_params=_params("parallel"),
    )(proj, proj, proj, cgate)


def _attn_pack_bwd(dmix, oa, qa, w, nh, name):
    t = dmix.shape[0]
    dh = w // nh
    tb = _pick(t, ROW_TILES)

    def body(d_ref, oa_ref, qa_ref, doa_ref, qa2_ref):
        lane = lax.broadcasted_iota(jnp.int32, (tb, LANES), 1)
        db = d_ref[...].astype(BF16)
        for h in range(nh):
            do_h = jnp.dot(db[:, _head_group(dh, h)], _head_select(dh, h, True), preferred_element_type=F32)
            o_h = oa_ref[h]
            delta = jnp.sum(jnp.where(lane < dh, do_h * o_h, 0.0), axis=1, keepdims=True)
            doa_ref[h] = (do_h + _bias_columns(lane, dh, delta)).astype(BF16)
            qa2_ref[h] = (qa_ref[h].astype(F32) + _bias_columns(lane, dh + 4, _column(o_h, lane, dh))).astype(BF16)

    blk = pl.BlockSpec((nh, tb, LANES), lambda i: (0, i, 0))
    return pl.pallas_call(
        body, name=name, grid=(t // tb,),
        in_specs=[pl.BlockSpec((tb, w), lambda i: (i, 0)), blk, blk], out_specs=[blk, blk],
        out_shape=[jax.ShapeDtypeStruct((nh, t, LANES), BF16)] * 2,
        compiler_params=_params("parallel"),
    )(dmix, oa, qa)


def _attn_unpack(xa, w, nh, mult, sum_col, sum_sign, name):
    t = xa.shape[1]
    dh = w // nh
    tb = _pick(t, ROW_TILES)

    def body(x_ref, o_ref, *rest):
        lane = lax.broadcasted_iota(jnp.int32, (tb, LANES), 1)
        per = LANES // dh
        cols = jnp.zeros((tb, LANES), F32)
        for h0 in range(0, nh, per):
            acc = jnp.zeros((tb, LANES), F32)
            for h in range(h0, h0 + per):
                xh = x_ref[h]
                acc = acc + jnp.dot((xh * mult).astype(BF16), _head_select(dh, h, False), preferred_element_type=F32)
                if sum_col is not None:
                    cols = cols + jnp.where(lane == h, sum_sign * _column(xh, lane, sum_col), 0.0)
            o_ref[:, _head_group(dh, h0)] = acc.astype(BF16)
        if sum_col is not None:
            rest[0][...] = cols

    nat = pl.BlockSpec((tb, w), lambda i: (i, 0))
    lanes = pl.BlockSpec((tb, LANES), lambda i: (i, 0))
    return pl.pallas_call(
        body, name=name, grid=(t // tb,),
        in_specs=[pl.BlockSpec((nh, tb, LANES), lambda i: (0, i, 0))],
        out_specs=[nat, lanes] if sum_col is not None else [nat],
        out_shape=[jax.ShapeDtypeStruct((t, w), BF16)] + ([jax.ShapeDtypeStruct((t, LANES), F32)]
                                                            if sum_col is not None else []),
        compiler_params=_params("parallel"),
    )(xa)


def _conv_fwd(proj, cw, w, bcol, name):
    t = proj.shape[0]
    tb = _pick(t, ROW_TILES)
    hb = tb // SUBLANES

    def body(b_ref, c_ref, h_ref, cp_ref, hp_ref, w_ref, y_ref):
        i = pl.program_id(0)
        zp = jnp.where(i > 0, cp_ref[...] * hp_ref[...], 0.0)
        zext = jnp.concatenate([zp, c_ref[...] * h_ref[...]], axis=0)
        z1 = pltpu.roll(zext, 1, 0)[SUBLANES:]
        z2 = pltpu.roll(zext, 2, 0)[SUBLANES:]
        y = w_ref[2:3, :] * zext[SUBLANES:] + w_ref[1:2, :] * z1 + w_ref[0:1, :] * z2
        y_ref[...] = (b_ref[...] * y).astype(BF16)

    cur = lambda j: pl.BlockSpec((tb, w), lambda i: (i, bcol + j))
    prev = lambda j: pl.BlockSpec((SUBLANES, w), lambda i: (jnp.maximum(i * hb - 1, 0), bcol + j))
    return pl.pallas_call(
        body, name=name, grid=(t // tb,),
        in_specs=[cur(0), cur(1), cur(2), prev(1), prev(2), pl.BlockSpec(cw.shape, lambda i: (0, 0))],
        out_specs=pl.BlockSpec((tb, w), lambda i: (i, 0)),
        out_shape=jax.ShapeDtypeStruct((t, w), BF16), compiler_params=_params("parallel"),
    )(proj, proj, proj, proj, proj, cw)


def _conv_bwd(proj, cw, dmix, w, bcol, name):
    t = proj.shape[0]
    tb = _pick(t, ROW_TILES)
    hb = tb // SUBLANES
    nb = t // tb
    n_ext = tb + SUBLANES

    def body(b_ref, c_ref, h_ref, cp_ref, hp_ref, bn_ref, d_ref, dn_ref, w_ref, db_ref, dc_ref, dh_ref, dw_ref):
        i = pl.program_id(0)
        c = c_ref[...]
        hh = h_ref[...]
        zp = jnp.where(i > 0, cp_ref[...] * hp_ref[...], 0.0)
        zext = jnp.concatenate([zp, c * hh], axis=0)
        z0 = zext[SUBLANES:]
        z1 = pltpu.roll(zext, 1, 0)[SUBLANES:]
        z2 = pltpu.roll(zext, 2, 0)[SUBLANES:]
        y = w_ref[2:3, :] * z0 + w_ref[1:2, :] * z1 + w_ref[0:1, :] * z2
        d = d_ref[...]
        db_ref[...] = d * y
        dy = d * b_ref[...]
        dyn = jnp.where(i < nb - 1, dn_ref[...] * bn_ref[...], 0.0)
        dext = jnp.concatenate([dy, dyn], axis=0)
        dy1 = pltpu.roll(dext, n_ext - 1, 0)[:tb]
        dy2 = pltpu.roll(dext, n_ext - 2, 0)[:tb]
        dz = w_ref[2:3, :] * dy + w_ref[1:2, :] * dy1 + w_ref[0:1, :] * dy2
        dc_ref[...] = dz * hh
        dh_ref[...] = dz * c

        @pl.when(i == 0)
        def _():
            dw_ref[...] = jnp.zeros_like(dw_ref)

        dw_ref[0:1, :] += jnp.sum(dy * z2, axis=0, keepdims=True)
        dw_ref[1:2, :] += jnp.sum(dy * z1, axis=0, keepdims=True)
        dw_ref[2:3, :] += jnp.sum(dy * z0, axis=0, keepdims=True)

    cur = lambda j: pl.BlockSpec((tb, w), lambda i: (i, bcol + j))
    prev = lambda j: pl.BlockSpec((SUBLANES, w), lambda i: (jnp.maximum(i * hb - 1, 0), bcol + j))
    nxt = lambda col: pl.BlockSpec((SUBLANES, w), lambda i: (jnp.minimum((i + 1) * hb, nb * hb - 1), col))
    out = pl.BlockSpec((tb, w), lambda i: (i, 0))
    return pl.pallas_call(
        body, name=name, grid=(nb,),
        in_specs=[cur(0), cur(1), cur(2), prev(1), prev(2), nxt(bcol),
                  pl.BlockSpec((tb, w), lambda i: (i, 1)), nxt(1), pl.BlockSpec(cw.shape, lambda i: (0, 0))],
        out_specs=[out, out, out, pl.BlockSpec(cw.shape, lambda i: (0, 0))],
        out_shape=[jax.ShapeDtypeStruct((t, w), F32)] * 3 + [jax.ShapeDtypeStruct(cw.shape, F32)],
        compiler_params=_params("arbitrary"),
    )(proj, proj, proj, proj, proj, proj, dmix, dmix, cw)


SQRT_HALF = 0.7071067811865476
INV_SQRT_2PI = 0.3989422804014327


def _gelu(x):
    return 0.5 * x * (1.0 + lax.erf(x * SQRT_HALF))


def _gelu_grad(x):
    return 0.5 * (1.0 + lax.erf(x * SQRT_HALF)) + x * (INV_SQRT_2PI * jnp.exp(-0.5 * x * x))


def _sgu_fwd(uv, ln_g, ln_b, wm, bs_full, name):
    t, d2 = uv.shape
    d = d2 // 2
    ng, pb, _ = wm.shape
    gd = d // ng
    tb = _pick(t, ROW_TILES[1:] or ROW_TILES)
    assert tb % pb == 0

    def body(uv_ref, g_ref, b_ref, w_ref, bs_ref, o_ref):
        u = _gelu(uv_ref[:, :d])
        v = _gelu(uv_ref[:, d:])
        mu = jnp.mean(v, axis=-1, keepdims=True)
        vc = v - mu
        var = jnp.mean(vc * vc, axis=-1, keepdims=True)
        vn = (vc * lax.rsqrt(var + LN_EPS) * g_ref[...] + b_ref[...]).astype(BF16)
        for r in range(tb // pb):
            rows = slice(r * pb, (r + 1) * pb)
            for gi in range(ng):
                cols = slice(gi * gd, (gi + 1) * gd)
                s = jnp.dot(w_ref[gi], vn[rows, cols], preferred_element_type=F32) + bs_ref[:, cols]
                o_ref[rows, cols] = (u[rows, cols] * s).astype(BF16)

    vec = pl.BlockSpec((1, d), lambda i: (0, 0))
    return pl.pallas_call(
        body, name=name, grid=(t // tb,),
        in_specs=[pl.BlockSpec((tb, d2), lambda i: (i, 0)), vec, vec,
                  pl.BlockSpec(wm.shape, lambda i: (0, 0, 0)), pl.BlockSpec((pb, d), lambda i: (0, 0))],
        out_specs=pl.BlockSpec((tb, d), lambda i: (i, 0)),
        out_shape=jax.ShapeDtypeStruct((t, d), BF16), compiler_params=_params("parallel"),
    )(uv, ln_g.reshape(1, d), ln_b.reshape(1, d), wm, bs_full)


def _sgu_bwd(uv, ln_g, ln_b, wm, bs_full, dgated, name):
    t, d2 = uv.shape
    d = d2 // 2
    ng, pb, _ = wm.shape
    gd = d // ng
    tb = _pick(t, ROW_TILES[1:] or ROW_TILES)
    nb = t // tb

    def body(uv_ref, g_ref, b_ref, w_ref, bs_ref, dg_ref, o_ref, dw_ref, dbs_ref, dlg_ref, dlb_ref,
             du_s, dvn_s, dbs_s):
        i = pl.program_id(0)

        @pl.when(i == 0)
        def _():
            dw_ref[...] = jnp.zeros_like(dw_ref)
            dbs_s[...] = jnp.zeros_like(dbs_s)
            dlg_ref[...] = jnp.zeros_like(dlg_ref)
            dlb_ref[...] = jnp.zeros_like(dlb_ref)

        upre = uv_ref[:, :d]
        vpre = uv_ref[:, d:]
        u = _gelu(upre)
        v = _gelu(vpre)
        mu = jnp.mean(v, axis=-1, keepdims=True)
        vc = v - mu
        var = jnp.mean(vc * vc, axis=-1, keepdims=True)
        rstd = lax.rsqrt(var + LN_EPS)
        xhat = vc * rstd
        vn = (xhat * g_ref[...] + b_ref[...]).astype(BF16)
        dgt = dg_ref[...].astype(F32)
        for r in range(tb // pb):
            rows = slice(r * pb, (r + 1) * pb)
            for gi in range(ng):
                cols = slice(gi * gd, (gi + 1) * gd)
                vblk = vn[rows, cols]
                s = jnp.dot(w_ref[gi], vblk, preferred_element_type=F32) + bs_ref[:, cols]
                dblk = dgt[rows, cols]
                du_s[rows, cols] = dblk * s
                ds = dblk * u[rows, cols]
                dsb = ds.astype(BF16)
                dvn_s[rows, cols] = lax.dot_general(w_ref[gi], dsb, (((0,), (0,)), ((), ())),
                                                    preferred_element_type=F32)
                dw_ref[gi] += lax.dot_general(dsb, vblk, (((1,), (1,)), ((), ())), preferred_element_type=F32)
                dbs_s[:, cols] += ds
        dvn = dvn_s[...]
        dlg_ref[...] += jnp.sum(dvn * xhat, axis=0, keepdims=True)
        dlb_ref[...] += jnp.sum(dvn, axis=0, keepdims=True)
        dxh = dvn * g_ref[...]
        m1 = jnp.mean(dxh, axis=-1, keepdims=True)
        m2 = jnp.mean(dxh * xhat, axis=-1, keepdims=True)
        dv = rstd * (dxh - m1 - xhat * m2)
        o_ref[:, :d] = (du_s[...] * _gelu_grad(upre)).astype(BF16)
        o_ref[:, d:] = (dv * _gelu_grad(vpre)).astype(BF16)

        @pl.when(i == nb - 1)
        def _():
            lane = lax.broadcasted_iota(jnp.int32, (pb, LANES), 1)
            acc = jnp.zeros((pb, LANES), F32)
            for gi in range(ng):
                col = jnp.sum(dbs_s[:, gi * gd:(gi + 1) * gd], axis=1, keepdims=True)
                acc = acc + jnp.where(lane == gi, col, 0.0)
            dbs_ref[...] = acc

    vec = pl.BlockSpec((1, d), lambda i: (0, 0))
    duv, dw, dbs, dlg, dlb = pl.pallas_call(
        body, name=name, grid=(nb,),
        in_specs=[pl.BlockSpec((tb, d2), lambda i: (i, 0)), vec, vec,
                  pl.BlockSpec(wm.shape, lambda i: (0, 0, 0)), pl.BlockSpec((pb, d), lambda i: (0, 0)),
                  pl.BlockSpec((tb, d), lambda i: (i, 0))],
        out_specs=[pl.BlockSpec((tb, d2), lambda i: (i, 0)), pl.BlockSpec(wm.shape, lambda i: (0, 0, 0)),
                   pl.BlockSpec((pb, LANES), lambda i: (0, 0)), vec, vec],
        out_shape=[jax.ShapeDtypeStruct((t, d2), BF16), jax.ShapeDtypeStruct(wm.shape, F32),
                   jax.ShapeDtypeStruct((pb, LANES), F32), jax.ShapeDtypeStruct((1, d), F32),
                   jax.ShapeDtypeStruct((1, d), F32)],
        scratch_shapes=[pltpu.VMEM((tb, d), F32), pltpu.VMEM((tb, d), F32), pltpu.VMEM((pb, d), F32)],
        compiler_params=_params("arbitrary"),
    )(uv, ln_g.reshape(1, d), ln_b.reshape(1, d), wm, bs_full, dgated)
    return duv, dw, dbs, dlg[0], dlb[0]


def _adamw(w, g, m, v, name):
    shape = w.shape
    cols = shape[-1]
    rows = w.size // cols
    tr = _pick(rows, (512, 256, 352, 128, 64, 32, 16, 8))

    def body(w_ref, g_ref, m_ref, v_ref, d_ref, mo_ref, vo_ref):
        d_ref[...], mo_ref[...], vo_ref[...] = _adam_update(w_ref[...], g_ref[...], m_ref[...], v_ref[...])

    spec = pl.BlockSpec((tr, cols), lambda i: (i, 0))
    outs = pl.pallas_call(
        body, name=name, grid=(rows // tr,),
        in_specs=[spec] * 4, out_specs=[spec] * 3,
        out_shape=[jax.ShapeDtypeStruct((rows, cols), F32)] * 3,
        compiler_params=_params("parallel"),
    )(*[a.reshape(rows, cols) for a in (w, g, m, v)])
    return [o.reshape(shape) for o in outs]


ANY = pl.BlockSpec(memory_space=pl.ANY)


def _place():
    return lax.axis_index("x"), lax.axis_index("y"), lax.axis_index("c")


def _all_gather(shards, name):
    n = len(shards)

    def body(*refs):
        start, forward, finish = _gather_phases(refs[:n], refs[n:2 * n], *refs[2 * n:])
        start()
        forward()
        finish()

    return pl.pallas_call(
        body, name=name, in_specs=[ANY] * n, out_specs=[ANY] * n,
        out_shape=_gather_shapes(shards), scratch_shapes=_gather_sems(n),
    )(*shards)


def _gather_shapes(shards):
    return [jax.ShapeDtypeStruct((N_DEV,) + s.shape, s.dtype) for s in shards]


def _gather_sems(n):
    return [pltpu.SemaphoreType.DMA((7 * n,)), pltpu.SemaphoreType.DMA((7 * n,)), pltpu.SemaphoreType.DMA((n,))]


def _gather_phases(x_refs, out_refs, send_sems, recv_sems, local_sems):
    n = len(x_refs)
    x, y, c = _place()
    me, sibling = (x, y, c), (x, y, 1 - c)
    chips = [(1 - x, y), (x, 1 - y), (1 - x, 1 - y)]

    def copy(a, k, block, to, own=False):
        px, py, pc = block
        rows = out_refs[a].at[4 * px + 2 * py + pc]
        return pltpu.make_async_remote_copy(
            src_ref=x_refs[a] if own else rows, dst_ref=rows,
            send_sem=send_sems.at[7 * a + k], recv_sem=recv_sems.at[7 * a + k],
            device_id=to, device_id_type=MESH)

    def local(a):
        return pltpu.make_async_copy(x_refs[a], out_refs[a].at[4 * x + 2 * y + c], local_sems.at[a])

    def first(a):
        return [copy(a, 0, me, sibling, own=True)] + [copy(a, 1 + j, me, (*chip, c), own=True)
                                                      for j, chip in enumerate(chips)]

    def start():
        for a in range(n):
            local(a).start()
            for cp in first(a):
                cp.start()

    def forward():
        for j, chip in enumerate(chips):
            for a in range(n):
                copy(a, 1 + j, (*chip, c), me).wait_recv()
                copy(a, 4 + j, (*chip, c), sibling).start()

    def finish():
        for a in range(n):
            copy(a, 0, sibling, me).wait_recv()
            for j, chip in enumerate(chips):
                copy(a, 4 + j, (*chip, 1 - c), me).wait_recv()
        for a in range(n):
            for cp in first(a) + [copy(a, 4 + j, (*chip, c), sibling) for j, chip in enumerate(chips)]:
                cp.wait_send()
            local(a).wait()

    return start, forward, finish


def _rs_sibling_exchange(packed, name):
    n = len(packed)

    def body(*refs):
        p_refs, r_refs = refs[:n], refs[n:2 * n]
        send_sems, recv_sems = refs[2 * n:]
        x, y, c = _place()
        cps = []
        for a in range(n):
            for j in range(4):
                cps.append(pltpu.make_async_remote_copy(
                    src_ref=p_refs[a].at[2 * j + (1 - c)], dst_ref=r_refs[a].at[j],
                    send_sem=send_sems.at[4 * a + j], recv_sem=recv_sems.at[4 * a + j],
                    device_id=(x, y, 1 - c), device_id_type=MESH))
        for cp in cps:
            cp.start()
        for cp in cps:
            cp.wait()

    return pl.pallas_call(
        body, name=name, in_specs=[ANY] * n, out_specs=[ANY] * n,
        out_shape=[jax.ShapeDtypeStruct((4,) + p.shape[1:], p.dtype) for p in packed],
        scratch_shapes=[pltpu.SemaphoreType.DMA((4 * n,)), pltpu.SemaphoreType.DMA((4 * n,))],
    )(*packed)


def _rs_chip_sum(packed, from_sibling, c_idx, name):
    _, r, cc = packed.shape
    tr = _pick(r, (512, 256, 352, 128))

    def body(c_ref, a_ref, b_ref, o_ref):
        o_ref[...] = (a_ref[...].astype(F32) + b_ref[...].astype(F32)).astype(o_ref.dtype)

    return pl.pallas_call(
        body, name=name,
        grid_spec=pltpu.PrefetchScalarGridSpec(
            num_scalar_prefetch=1, grid=(4, r // tr),
            in_specs=[pl.BlockSpec((None, tr, cc), lambda j, i, c_ref: (2 * j + c_ref[0], i, 0)),
                      pl.BlockSpec((None, tr, cc), lambda j, i, c_ref: (j, i, 0))],
            out_specs=pl.BlockSpec((None, tr, cc), lambda j, i, c_ref: (j, i, 0))),
        out_shape=jax.ShapeDtypeStruct((4, r, cc), packed.dtype),
        compiler_params=_params("parallel", "parallel"),
    )(c_idx, packed, from_sibling)


def _rs_chip_exchange(partial, name):
    n = len(partial)

    def body(*refs):
        start, finish = _chip_exchange_phases(refs[:n], refs[n:2 * n], *refs[2 * n:])
        start()
        finish()

    return pl.pallas_call(
        body, name=name, in_specs=[ANY] * n, out_specs=[ANY] * n,
        out_shape=_chip_exchange_shapes(partial), scratch_shapes=_chip_exchange_sems(n),
    )(*partial)


def _chip_exchange_shapes(partial):
    return [jax.ShapeDtypeStruct((3,) + p.shape[1:], p.dtype) for p in partial]


def _chip_exchange_sems(n):
    return [pltpu.SemaphoreType.DMA((3 * n,)), pltpu.SemaphoreType.DMA((3 * n,))]


def _chip_exchange_phases(p_refs, r_refs, send_sems, recv_sems):
    x, y, c = _place()
    chips = [(1 - x, y), (x, 1 - y), (1 - x, 1 - y)]

    def copies():
        return [pltpu.make_async_remote_copy(
            src_ref=p_refs[a].at[2 * tx + ty], dst_ref=r_refs[a].at[k],
            send_sem=send_sems.at[3 * a + k], recv_sem=recv_sems.at[3 * a + k],
            device_id=(tx, ty, c), device_id_type=MESH)
            for a in range(len(p_refs)) for k, (tx, ty) in enumerate(chips)]

    def start():
        for cp in copies():
            cp.start()

    def finish():
        for cp in copies():
            cp.wait()

    return start, finish


def _adam_update(w, g, m, v):
    mn = ADAM_B1 * m + (1.0 - ADAM_B1) * g
    vn = ADAM_B2 * v + (1.0 - ADAM_B2) * (g * g)
    m_hat = mn / (1.0 - ADAM_B1 ** ADAM_STEP)
    v_hat = vn / (1.0 - ADAM_B2 ** ADAM_STEP)
    return -ADAM_LR * (m_hat / (jnp.sqrt(v_hat) + ADAM_EPS) + ADAM_WD * w), mn, vn


def _rs_final_adamw(partial, received, chip_idx, w, m, v, name):
    _, r, cc = partial.shape
    tr = _pick(r, (512, 256, 352, 128))

    def body(c_ref, a_ref, r_ref, w_ref, m_ref, v_ref, g_ref, d_ref, mo_ref, vo_ref):
        g = a_ref[...].astype(F32)
        for k in range(3):
            g = g + r_ref[k].astype(F32)
        g_ref[...] = g
        d_ref[...], mo_ref[...], vo_ref[...] = _adam_update(w_ref[...], g, m_ref[...], v_ref[...])

    row = pl.BlockSpec((tr, cc), lambda i, c_ref: (i, 0))
    return pl.pallas_call(
        body, name=name,
        grid_spec=pltpu.PrefetchScalarGridSpec(
            num_scalar_prefetch=1, grid=(r // tr,),
            in_specs=[pl.BlockSpec((None, tr, cc), lambda i, c_ref: (c_ref[0], i, 0)),
                      pl.BlockSpec((3, tr, cc), lambda i, c_ref: (0, i, 0)), row, row, row],
            out_specs=[row] * 4),
        out_shape=[jax.ShapeDtypeStruct((r, cc), F32)] * 4,
        compiler_params=_params("parallel"),
    )(chip_idx, partial, received, w.reshape(r, cc), m.reshape(r, cc), v.reshape(r, cc))


def _all_reduce_small(vals, name):
    rider = _all_reduce_rider(vals)

    def body(v_ref, o_ref, *scratch):
        start, finish = rider.phases([v_ref], [o_ref], scratch)
        start()
        finish()

    return pl.pallas_call(
        body, name=name, in_specs=rider.in_specs, out_specs=rider.out_specs[0], out_shape=rider.out_shapes[0],
        scratch_shapes=rider.scratch, compiler_params=pltpu.CompilerParams(vmem_limit_bytes=VMEM_LIMIT),
    )(vals)


def _all_reduce_rider(vals):
    r, cc = vals.shape

    def phases(ins, outs, scratch):
        (v_ref,), (o_ref,), (buf, send_sems, recv_sems) = ins, outs, scratch
        x, y, c = _place()
        me = 4 * x + 2 * y + c

        def copies():
            cps = []
            for k in range(1, N_DEV):
                kx, ky, kc = (k >> 2) & 1, (k >> 1) & 1, k & 1
                peer = (1 - x if kx else x, 1 - y if ky else y, 1 - c if kc else c)
                cps.append(pltpu.make_async_remote_copy(
                    src_ref=buf.at[0], dst_ref=buf.at[k], send_sem=send_sems.at[k - 1],
                    recv_sem=recv_sems.at[k - 1], device_id=peer, device_id_type=MESH))
            return cps

        def start():
            buf[0] = v_ref[...]
            for cp in copies():
                cp.start()

        def finish():
            for cp in copies():
                cp.wait()
            acc = buf[jnp.bitwise_xor(me, 0)]
            for dev in range(1, N_DEV):
                acc = acc + buf[jnp.bitwise_xor(me, dev)]
            o_ref[...] = acc

        return start, finish

    vm = pl.BlockSpec(memory_space=pltpu.VMEM)
    return _Rider([vals], [vm], [jax.ShapeDtypeStruct((r, cc), F32)], [vm],
                  [pltpu.VMEM((N_DEV, r, cc), F32), pltpu.SemaphoreType.DMA((7,)), pltpu.SemaphoreType.DMA((7,))],
                  phases)


def _chip_exchange_rider(partial):
    n = len(partial)
    return _Rider(list(partial), [ANY] * n, _chip_exchange_shapes(partial), [ANY] * n, _chip_exchange_sems(n),
                  lambda ins, outs, scratch: _chip_exchange_phases(ins, outs, *scratch))


def _lanes(flat):
    pad = (-flat.shape[0]) % (SUBLANES * LANES)
    return jnp.pad(flat, (0, pad)).reshape(-1, LANES)


def kernel(x, even_w_in, even_b_f, even_conv_w, even_w_out, odd_w_in, odd_v_ln_g, odd_v_ln_b, odd_w_s, odd_b_s, odd_w_out, mix_ln_g, mix_ln_b, ffn_w_in, ffn_w_out, ffn_ln_g, ffn_ln_b, loss_target, m_even_w_in, m_even_b_f, m_even_conv_w, m_even_w_out, m_odd_w_in, m_odd_v_ln_g, m_odd_v_ln_b, m_odd_w_s, m_odd_b_s, m_odd_w_out, m_mix_ln_g, m_mix_ln_b, m_ffn_w_in, m_ffn_w_out, m_ffn_ln_g, m_ffn_ln_b, v_even_w_in, v_even_b_f, v_even_conv_w, v_even_w_out, v_odd_w_in, v_odd_v_ln_g, v_odd_v_ln_b, v_odd_w_s, v_odd_b_s, v_odd_w_out, v_mix_ln_g, v_mix_ln_b, v_ffn_w_in, v_ffn_w_out, v_ffn_ln_g, v_ffn_ln_b):
    t, d = x.shape[1], x.shape[2]
    nh = even_b_f.shape[-1]
    w = even_conv_w.shape[-1] * N_DEV
    dh = w // nh
    scale = dh ** -0.5
    e_in = even_w_in.shape[-1] * N_DEV
    f2 = ffn_w_in.shape[-1] * N_DEV
    f = f2 // 2
    ng, pb = odd_w_s.shape[1], odd_w_s.shape[2]
    assert e_in == 6 * w + nh and nh <= SUBLANES and (6 * w) % LANES == 0 and d % N_DEV == 0
    mx, my, mc = _place()
    me = 4 * mx + 2 * my + mc

    big = [even_w_in[0], even_w_out[0], odd_w_in[0], odd_w_out[0],
           ffn_w_in[0], ffn_w_in[1], ffn_w_out[0], ffn_w_out[1]]
    g_in0, = _all_gather([big[0].astype(BF16)], "ag_even_w_in")
    w_in0 = g_in0.transpose(1, 0, 2).reshape(d, e_in)
    w_all0 = jnp.concatenate([w_in0[:, :3 * w], w_in0[:, 3 * w + nh:], w_in0[:, 3 * w:3 * w + nh],
                              jnp.zeros((d, LANES - nh), BF16)], axis=1)

    cs, vs = even_conv_w.shape[-1], odd_v_ln_g.shape[-1]
    small_mine = jnp.concatenate([
        lax.dynamic_update_slice(jnp.zeros((3, w), F32), even_conv_w[0], (0, me * cs)).reshape(-1),
        lax.dynamic_update_slice(jnp.zeros((d,), F32), odd_v_ln_g[0], (me * vs,)),
        lax.dynamic_update_slice(jnp.zeros((d,), F32), odd_v_ln_b[0], (me * vs,))])
    small_all = _all_reduce_small(_lanes(small_mine), "ag_small").reshape(-1)
    conv_w = small_all[:3 * w].reshape(3, w)
    vln_g = small_all[3 * w:3 * w + d]
    vln_b = small_all[3 * w + d:3 * w + 2 * d]

    bf_pad = jnp.pad(even_b_f[0], (0, LANES - nh)).reshape(1, LANES)
    chunk = jnp.arange(pb) // (pb // 2)
    ws_mask = (chunk[None, :] <= chunk[:, None])[None]
    wm = jnp.where(ws_mask, odd_w_s[0], 0.0).astype(BF16)
    bs_full = jnp.repeat(odd_b_s[0].T, d // ng, axis=1)

    x0 = x[0]
    tgt = loss_target[0]
    fcol = 6 * w // LANES
    x0b = x0.astype(BF16)
    p0 = _mm(x0b, w_all0, "nn", F32, "l0_in_proj")
    cgate = _fgate_fwd(p0, bf_pad, fcol, nh, "l0_fgate")
    assert dh + 7 <= LANES
    qa, ka, va = _attn_pack(p0, cgate, w, nh, scale, "l0_attn_pack")
    oa, g_out0, g_in1, g_out1, g_fi0, g_fi1, g_fo0, g_fo1 = _attn_fwd(
        qa, ka, va, dh, "l0_attn", gather=[s.astype(BF16) for s in big[1:]])
    w_out0, w_out1 = g_out0.reshape(2 * w, d), g_out1.reshape(d, d)
    w_fo0, w_fo1 = g_fo0.reshape(f, d), g_fo1.reshape(f, d)
    nb = N_DEV // 2
    w_fi0, w_fi1 = g_fi0.reshape(2, nb, d, -1), g_fi1.reshape(2, nb, d, -1)
    attn, = _attn_unpack(oa, w, nh, 1.0, None, 1.0, "l0_attn_unpack")
    yconv = _conv_fwd(p0, conv_w, w, 3, "l0_conv")
    mix = jnp.concatenate([attn, yconv], axis=1)
    m0, x1, x1b = _mm_ln(mix, w_out0, x0, mix_ln_g[0], mix_ln_b[0], "l0_out_proj_ln")
    h0, gu0 = _ffn_in_swiglu(x1b, w_fi0, "l0_ffn_in")
    f0, x2, x2b = _mm_ln(h0, w_fo0, x1, ffn_ln_g[0], ffn_ln_b[0], "l0_ffn_out_ln")

    uv = _mm_cols_fwd(x2b, g_in1, False, F32, "l1_in_proj")
    gated = _sgu_fwd(uv, vln_g, vln_b, wm, bs_full, "l1_sgu")
    m1, x3, x3b = _mm_ln(gated, w_out1, x2, mix_ln_g[1], mix_ln_b[1], "l1_out_proj_ln")
    h1, gu1 = _ffn_in_swiglu(x3b, w_fi1, "l1_ffn_in")
    f1, dy4, loss_part = _mm_ln(h1, w_fo1, x3, ffn_ln_g[1], ffn_ln_b[1], "l1_ffn_out_ln_loss", target=tgt)

    dz4, dz4b, g_ffn_g1, g_ffn_b1 = _ln_bwd(x3, f1, ffn_ln_g[1], dy4, 1.0, None, "l1_ffn_ln_bwd")
    gd_fo1 = _mm_blk_dw(h1, dz4b, BF16, "l1_ffn_out_dw").reshape(N_DEV, -1, d)
    dgu1 = _ffn_out_dx_swiglu(dz4b, w_fo1, gu1, "l1_ffn_out_dx").reshape(N_DEV, t, -1)
    gd_fi1 = _mm_cols_dw(x3b, dgu1, N_DEV, True, BF16, "l1_ffn_in_dw")
    dx3 = _mm_cols_dx(dgu1, g_fi1, True, F32, "l1_ffn_in_dx")
    dz3, dz3b, g_mix_g1, g_mix_b1 = _ln_bwd(x2, m1, mix_ln_g[1], dz4, ALPHA, dx3, "l1_mix_ln_bwd")
    gd_out1 = _mm(gated, dz3b, "tn", BF16, "l1_out_proj_dw").reshape(N_DEV, -1, d)
    dgated = _mm(dz3b, w_out1, "nt", BF16, "l1_out_proj_dx")
    duv, g_wm, g_bs_t, g_vln_g, g_vln_b = _sgu_bwd(uv, vln_g, vln_b, wm, bs_full, dgated, "l1_sgu_bwd")
    gd_in1 = _mm_cols_dw(x2b, duv, N_DEV, False, BF16, "l1_in_proj_dw")
    dx2 = _mm_cols_dx(duv, g_in1, False, F32, "l1_in_proj_dx")

    dz2, dz2b, g_ffn_g0, g_ffn_b0 = _ln_bwd(x1, f0, ffn_ln_g[0], dz3, ALPHA, dx2, "l0_ffn_ln_bwd")
    gd_fo0 = _mm_blk_dw(h0, dz2b, BF16, "l0_ffn_out_dw").reshape(N_DEV, -1, d)
    dgu0 = _ffn_out_dx_swiglu(dz2b, w_fo0, gu0, "l0_ffn_out_dx").reshape(N_DEV, t, -1)
    gd_fi0 = _mm_cols_dw(x1b, dgu0, N_DEV, True, BF16, "l0_ffn_in_dw")
    dx1 = _mm_cols_dx(dgu0, g_fi0, True, F32, "l0_ffn_in_dx")
    dz1, dz1b, g_mix_g0, g_mix_b0 = _ln_bwd(x0, m0, mix_ln_g[0], dz2, ALPHA, dx1, "l0_mix_ln_bwd")
    gd_out0 = _mm(mix, dz1b, "tn", BF16, "l0_out_proj_dw").reshape(N_DEV, -1, d)
    dmix = _mm(dz1b, w_out0, "nt", F32, "l0_out_proj_dx")
    d_b, d_c, d_h, g_conv = _conv_bwd(p0, conv_w, dmix, w, 3, "l0_conv_bwd")
    doa, qa2 = _attn_pack_bwd(dmix, oa, qa, w, nh, "l0_attn_pack_bwd")
    big_names = ["even_w_in", "even_w_out", "odd_w_in", "odd_w_out", "ffn_w_in0", "ffn_w_in1", "ffn_w_out0", "ffn_w_out1"]
    c_idx = mc.reshape(1).astype(jnp.int32)
    chip_idx = (2 * mx + my).reshape(1).astype(jnp.int32)
    early_g = [gd_out0, gd_in1, gd_out1, gd_fi0, gd_fi1, gd_fo0, gd_fo1]
    early_sib = _rs_sibling_exchange(early_g, "rs_sibling_early")
    early_partial = [_rs_chip_sum(g, s, c_idx, "rs_chip_sum_" + n)
                     for g, s, n in zip(early_g, early_sib, big_names[1:])]
    dqa, dka, dva, *early_received = _attn_bwd(qa2, ka, va, doa, "l0_attn_bwd", exchange=early_partial)
    dq, dcq = _attn_unpack(dqa, w, nh, scale, dh + 3, 1.0, "l0_attn_unpack_dq")
    dk, dck = _attn_unpack(dka, w, nh, 1.0, dh, -1.0, "l0_attn_unpack_dk")
    dv, = _attn_unpack(dva, w, nh, 1.0, None, 1.0, "l0_attn_unpack_dv")
    dzf, g_bf = _fgate_bwd(p0, bf_pad, dcq, dck, fcol, nh, "l0_fgate_bwd")
    dp0 = jnp.concatenate([dq, dk, dv, d_b.astype(BF16), d_c.astype(BF16), d_h.astype(BF16), dzf.astype(BF16)], axis=1)
    g_ws = jnp.where(ws_mask, g_wm, 0.0)
    g_bs = g_bs_t[:, :ng].T
    small_g = [g_bf[:nh], g_conv, g_vln_g, g_vln_b, g_ws, g_bs,
               jnp.stack([g_mix_g0, g_mix_g1]), jnp.stack([g_mix_b0, g_mix_b1]),
               jnp.stack([g_ffn_g0, g_ffn_g1]), jnp.stack([g_ffn_b0, g_ffn_b1])]
    small_rider = _all_reduce_rider(_lanes(jnp.concatenate([a.reshape(-1) for a in small_g])))
    g_all0, small_sum = _mm(x0b, dp0, "tn", F32, "l0_in_proj_dw", rider=small_rider)
    gd_in0 = jnp.concatenate([g_all0[:, :3 * w], g_all0[:, 6 * w:6 * w + nh], g_all0[:, 3 * w:6 * w]], axis=1)
    gd_in0 = gd_in0.reshape(d, N_DEV, -1).transpose(1, 0, 2).astype(BF16)

    big_m = [m_even_w_in[0], m_even_w_out[0], m_odd_w_in[0], m_odd_w_out[0],
             m_ffn_w_in[0], m_ffn_w_in[1], m_ffn_w_out[0], m_ffn_w_out[1]]
    big_v = [v_even_w_in[0], v_even_w_out[0], v_odd_w_in[0], v_odd_w_out[0],
             v_ffn_w_in[0], v_ffn_w_in[1], v_ffn_w_out[0], v_ffn_w_out[1]]
    late_sib = _rs_sibling_exchange([gd_in0], "rs_sibling_late")
    late_partial = [_rs_chip_sum(gd_in0, late_sib[0], c_idx, "rs_chip_sum_" + big_names[0])]
    partial = late_partial + early_partial
    dx0, *late_received = _mm(dp0, w_all0, "nt", F32, "l0_in_proj_dx", rider=_chip_exchange_rider(late_partial))
    grad_x = _axpy(ALPHA, dz1, dx0, "grad_x")
    received = list(late_received) + list(early_received)
    upd = [_rs_final_adamw(p, r, chip_idx, wt, mt, vt, "rs_final_adamw_" + n)
           for p, r, wt, mt, vt, n in zip(partial, received, big, big_m, big_v, big_names)]
    big_out = {}
    for i, n in enumerate(["even_w_in", "even_w_out", "odd_w_in", "odd_w_out"]):
        big_out[n] = [o[None] for o in upd[i]]
    big_out["ffn_w_in"] = [jnp.stack([a, b]) for a, b in zip(upd[4], upd[5])]
    big_out["ffn_w_out"] = [jnp.stack([a, b]) for a, b in zip(upd[6], upd[7])]

    small_sum = small_sum.reshape(-1)
    outs_small = []
    off = 0
    for a in small_g:
        outs_small.append(small_sum[off:off + a.size].reshape(a.shape))
        off += a.size
    gr_bf, gr_conv, gr_vg, gr_vb, gr_ws, gr_bs, gr_mg, gr_mb, gr_fg, gr_fb = outs_small

    loss = lax.psum(loss_part, ("x", "y", "c"))

    grads = {
        "even_b_f": gr_bf[None],
        "even_conv_w": lax.dynamic_slice(gr_conv, (0, me * cs), (3, cs))[None],
        "odd_v_ln_g": lax.dynamic_slice(gr_vg, (me * vs,), (vs,))[None],
        "odd_v_ln_b": lax.dynamic_slice(gr_vb, (me * vs,), (vs,))[None],
        "odd_w_s": gr_ws[None], "odd_b_s": gr_bs[None],
        "mix_ln_g": gr_mg, "mix_ln_b": gr_mb, "ffn_ln_g": gr_fg, "ffn_ln_b": gr_fb,
    }
    weights = dict(even_w_in=even_w_in, even_b_f=even_b_f, even_conv_w=even_conv_w, even_w_out=even_w_out,
                   odd_w_in=odd_w_in, odd_v_ln_g=odd_v_ln_g, odd_v_ln_b=odd_v_ln_b, odd_w_s=odd_w_s,
                   odd_b_s=odd_b_s, odd_w_out=odd_w_out, mix_ln_g=mix_ln_g, mix_ln_b=mix_ln_b,
                   ffn_w_in=ffn_w_in, ffn_w_out=ffn_w_out, ffn_ln_g=ffn_ln_g, ffn_ln_b=ffn_ln_b)
    moms = dict(even_w_in=(m_even_w_in, v_even_w_in), even_b_f=(m_even_b_f, v_even_b_f),
                even_conv_w=(m_even_conv_w, v_even_conv_w), even_w_out=(m_even_w_out, v_even_w_out),
                odd_w_in=(m_odd_w_in, v_odd_w_in), odd_v_ln_g=(m_odd_v_ln_g, v_odd_v_ln_g),
                odd_v_ln_b=(m_odd_v_ln_b, v_odd_v_ln_b), odd_w_s=(m_odd_w_s, v_odd_w_s),
                odd_b_s=(m_odd_b_s, v_odd_b_s), odd_w_out=(m_odd_w_out, v_odd_w_out),
                mix_ln_g=(m_mix_ln_g, v_mix_ln_g), mix_ln_b=(m_mix_ln_b, v_mix_ln_b),
                ffn_w_in=(m_ffn_w_in, v_ffn_w_in), ffn_w_out=(m_ffn_w_out, v_ffn_w_out),
                ffn_ln_g=(m_ffn_ln_g, v_ffn_ln_g), ffn_ln_b=(m_ffn_ln_b, v_ffn_ln_b))
    names = list(weights)
    gout, deltas, new_m, new_v = [], [], [], []
    for n in names:
        if n in big_out:
            gr, dlt, mn, vn = big_out[n]
        else:
            gr = grads[n]
            dlt, mn, vn = _adamw(weights[n], gr, moms[n][0], moms[n][1], "adamw_" + n)
        gout.append(gr.reshape(weights[n].shape))
        deltas.append(dlt.reshape(weights[n].shape))
        new_m.append(mn.reshape(weights[n].shape))
        new_v.append(vn.reshape(weights[n].shape))
    return (loss, grad_x[None], *gout, *deltas, *new_m, *new_v)
```

```python
import functools
from typing import Callable, NamedTuple

import jax
import jax.numpy as jnp
from jax import lax
from jax.experimental import pallas as pl
from jax.experimental.pallas import tpu as pltpu

F32 = jnp.float32
BF16 = jnp.bfloat16
MESH = pl.DeviceIdType.MESH

DEPTH = 2
ALPHA = (2.0 * DEPTH) ** 0.25
LN_EPS = 1e-5
ADAM_LR = 0.001
ADAM_B1 = 0.9
ADAM_B2 = 0.999
ADAM_EPS = 1e-08
ADAM_WD = 0.01
ADAM_STEP = 10

N_DEV = 8
LANES = 128
SUBLANES = 8
VMEM_LIMIT = 48 * 1024 * 1024
NEG_BIG = -1e30
ROW_TILES = (512, 256, 128)


def _pick(n, cands):
    for c in cands:
        if c <= n and n % c == 0:
            return c
    return n


def _params(*sem):
    return pltpu.CompilerParams(dimension_semantics=sem, vmem_limit_bytes=VMEM_LIMIT)


NN = (((1,), (0,)), ((), ()))
NT = (((1,), (1,)), ((), ()))
TN = (((0,), (0,)), ((), ()))
M_TILES = (1024, 512, 1408, 256, 128)
N_TILES = (512, 640, 256, 128)
K_TILES = (2048, 1024, 512, 640, 1408, 256, 128)
K_WHOLE = 3328


class _Rider(NamedTuple):
    inputs: list
    in_specs: list
    out_shapes: list
    out_specs: list
    scratch: list
    phases: Callable


def _mm_core(name, grid, a, b, a_spec, b_spec, o_spec, o_shape, o_dtype, dims, tile, pieces=None, rider=None):
    nred = grid[2]
    pieces = pieces or [(lambda r: r[...], lambda r: r[...])]
    ni = len(rider.inputs) if rider else 0
    no = len(rider.out_shapes) if rider else 0
    nacc = 0 if nred == 1 else 1

    def body(a_ref, b_ref, *rest):
        o_ref = rest[ni]
        if rider:
            start, finish = rider.phases(rest[:ni], rest[ni + 1:ni + 1 + no], rest[ni + 1 + no + nacc:])
            ids = [pl.program_id(ax) for ax in range(3)]
            first = functools.reduce(jnp.logical_and, [i == 0 for i in ids])
            last = functools.reduce(jnp.logical_and, [i == g - 1 for i, g in zip(ids, grid)])
            pl.when(first)(start)
        part = None
        for fa, fb in pieces:
            prod = lax.dot_general(fa(a_ref).astype(BF16), fb(b_ref).astype(BF16), dims, preferred_element_type=F32)
            part = prod if part is None else part + prod
        if nred == 1:
            o_ref[...] = part.astype(o_ref.dtype)
        else:
            acc_ref = rest[ni + 1 + no]
            kk = pl.program_id(2)

            @pl.when(kk == 0)
            def _():
                acc_ref[...] = jnp.zeros_like(acc_ref)

            acc_ref[...] += part

            @pl.when(kk == nred - 1)
            def _():
                o_ref[...] = acc_ref[...].astype(o_ref.dtype)
        if rider:
            pl.when(last)(finish)

    out = pl.pallas_call(
        body, name=name, grid=grid,
        in_specs=[a_spec, b_spec] + (rider.in_specs if rider else []),
        out_specs=[o_spec] + (rider.out_specs if rider else []),
        out_shape=[jax.ShapeDtypeStruct(o_shape, o_dtype)] + (rider.out_shapes if rider else []),
        scratch_shapes=([] if nred == 1 else [pltpu.VMEM(tile, F32)]) + (rider.scratch if rider else []),
        compiler_params=_params(*(["arbitrary"] * 3 if rider else ["parallel", "parallel", "arbitrary"])),
    )(a, b, *(rider.inputs if rider else []))
    return out if rider else out[0]


def _mm(a, b, mode, out_dtype, name, rider=None):
    if mode == "nn":
        (m, k), (k2, n) = a.shape, b.shape
    elif mode == "nt":
        (m, k), (n, k2) = a.shape, b.shape
    else:
        (k, m), (k2, n) = a.shape, b.shape
    assert k == k2, (a.shape, b.shape, mode)
    tm, tn = _pick(m, M_TILES), _pick(n, N_TILES)
    tk = k if k <= K_WHOLE else _pick(k, K_TILES)
    if mode == "nn":
        a_spec = pl.BlockSpec((tm, tk), lambda i, j, kk: (i, kk))
        b_spec = pl.BlockSpec((tk, tn), lambda i, j, kk: (kk, j))
        dims = NN
    elif mode == "nt":
        a_spec = pl.BlockSpec((tm, tk), lambda i, j, kk: (i, kk))
        b_spec = pl.BlockSpec((tn, tk), lambda i, j, kk: (j, kk))
        dims = NT
    else:
        a_spec = pl.BlockSpec((tk, tm), lambda i, j, kk: (kk, i))
        b_spec = pl.BlockSpec((tk, tn), lambda i, j, kk: (kk, j))
        dims = TN
    return _mm_core(name, (m // tm, n // tn, k // tk), a, b, a_spec, b_spec,
                    pl.BlockSpec((tm, tn), lambda i, j, kk: (i, j)), (m, n), out_dtype, dims, (tm, tn), rider=rider)


def _act_spec(blocked, rows, ns, row_ax, d_ax):
    if blocked:
        return pl.BlockSpec((None, rows, ns), lambda *g: (g[d_ax], g[row_ax], 0))
    return pl.BlockSpec((rows, ns), lambda *g: (g[row_ax], g[d_ax]))


def _mm_cols_fwd(a, g3, blocked, out_dtype, name):
    (t, k), (nd, k2, ns) = a.shape, g3.shape
    assert k == k2
    tm, tk = _pick(t, M_TILES), _pick(k, K_TILES)
    return _mm_core(name, (t // tm, nd, k // tk), a, g3,
                    pl.BlockSpec((tm, tk), lambda i, d, kk: (i, kk)),
                    pl.BlockSpec((None, tk, ns), lambda i, d, kk: (d, kk, 0)),
                    _act_spec(blocked, tm, ns, 0, 1), (nd, t, ns) if blocked else (t, nd * ns), out_dtype, NN, (tm, ns))


def _mm_cols_dx(dy, g3, blocked, out_dtype, name):
    nd, k, ns = g3.shape
    t = dy.shape[1] if blocked else dy.shape[0]
    tm, tn = _pick(t, M_TILES), _pick(k, (1024,) + N_TILES)
    o_spec = pl.BlockSpec((tm, tn), lambda i, j, d: (i, j))
    if not blocked:
        whole_b = lambda r: jnp.concatenate([r[s] for s in range(nd)], axis=1)
        return _mm_core(name, (t // tm, k // tn, 1), dy, g3,
                        pl.BlockSpec((tm, nd * ns), lambda i, j, d: (i, 0)),
                        pl.BlockSpec((nd, tn, ns), lambda i, j, d: (0, j, 0)),
                        o_spec, (t, k), out_dtype, NT, (tm, tn), pieces=[(lambda r: r[...], whole_b)])
    grp = 2 if nd % 2 == 0 else 1
    pieces = [(lambda r, s=s: r[s], lambda r, s=s: r[s]) for s in range(grp)]
    return _mm_core(name, (t // tm, k // tn, nd // grp), dy, g3,
                    pl.BlockSpec((grp, tm, ns), lambda i, j, d: (d, i, 0)),
                    pl.BlockSpec((grp, tn, ns), lambda i, j, d: (d, j, 0)),
                    o_spec, (t, k), out_dtype, NT, (tm, tn), pieces=pieces)


def _mm_cols_dw(a, dy, nd, blocked, out_dtype, name):
    t, k = a.shape
    ns = dy.shape[2] if blocked else dy.shape[1] // nd
    tmk, tk = _pick(k, M_TILES), _pick(t, K_TILES)
    return _mm_core(name, (nd, k // tmk, t // tk), a, dy,
                    pl.BlockSpec((tk, tmk), lambda d, j, kk: (kk, j)),
                    _act_spec(blocked, tk, ns, 2, 0),
                    pl.BlockSpec((None, tmk, ns), lambda d, j, kk: (d, j, 0)), (nd, k, ns), out_dtype, TN, (tmk, ns))


def _mm_blk_dw(h3, dz, out_dtype, name):
    (nb, t, ns), (_, n) = h3.shape, dz.shape
    tn, tk = _pick(n, (1024,) + N_TILES), _pick(t, K_TILES)
    return _mm_core(name, (nb, n // tn, t // tk), h3, dz,
                    pl.BlockSpec((None, tk, ns), lambda d, j, kk: (d, kk, 0)),
                    pl.BlockSpec((tk, tn), lambda d, j, kk: (kk, j)),
                    pl.BlockSpec((ns, tn), lambda d, j, kk: (d, j)), (nb * ns, n), out_dtype, TN, (ns, tn))


def _mm_ln(a, w, xa, g, b, name, target=None):
    blocked = a.ndim == 3
    t, d = xa.shape
    k = w.shape[0]
    tm = _pick(t, ROW_TILES)
    nb = a.shape[0] if blocked else 1
    ns = k // nb
    halves = [slice(0, tm // 2), slice(tm // 2, tm)] if tm % 32 == 0 else [slice(0, tm)]

    def body(a_ref, w_ref, xa_ref, g_ref, b_ref, *rest):
        def product(rows):
            if not blocked:
                return jnp.dot(a_ref[rows, :], w_ref[...], preferred_element_type=F32)
            acc = None
            for s in range(nb):
                prod = jnp.dot(a_ref[s, rows, :], w_ref[s * ns:(s + 1) * ns, :], preferred_element_type=F32)
                acc = prod if acc is None else acc + prod
            return acc

        if target is not None:
            t_ref, dz_ref, dzb_ref, dg_ref, db_ref, l_ref = rest

            @pl.when(pl.program_id(0) == 0)
            def _():
                l_ref[...] = jnp.zeros_like(l_ref)
                dg_ref[...] = jnp.zeros_like(dg_ref)
                db_ref[...] = jnp.zeros_like(db_ref)
        else:
            xb_ref, y_ref, yb_ref = rest
        for rows, xb in zip(halves, [product(rows) for rows in halves]):
            z = ALPHA * xa_ref[rows, :] + xb
            mu = jnp.mean(z, axis=-1, keepdims=True)
            zc = z - mu
            var = jnp.mean(zc * zc, axis=-1, keepdims=True)
            rstd = lax.rsqrt(var + LN_EPS)
            xhat = zc * rstd
            y = xhat * g_ref[...] + b_ref[...]
            if target is not None:
                e = y - t_ref[rows, :]
                l_ref[...] += 0.5 * jnp.sum(jnp.mean(e * e, axis=-1, keepdims=True))
                dy = e * (1.0 / d)
                dxh = dy * g_ref[...]
                m1 = jnp.mean(dxh, axis=-1, keepdims=True)
                m2 = jnp.mean(dxh * xhat, axis=-1, keepdims=True)
                dz = rstd * (dxh - m1 - xhat * m2)
                dz_ref[rows, :] = dz
                dzb_ref[rows, :] = dz.astype(BF16)
                dg_ref[...] += jnp.sum(dy * xhat, axis=0, keepdims=True)
                db_ref[...] += jnp.sum(dy, axis=0, keepdims=True)
            else:
                xb_ref[rows, :] = xb
                y_ref[rows, :] = y
                yb_ref[rows, :] = y.astype(BF16)

    row = pl.BlockSpec((tm, d), lambda i: (i, 0))
    vec = pl.BlockSpec((1, d), lambda i: (0, 0))
    a_spec = pl.BlockSpec((nb, tm, ns), lambda i: (0, i, 0)) if blocked else pl.BlockSpec((tm, k), lambda i: (i, 0))
    ins = [a, w, xa, g.reshape(1, d), b.reshape(1, d)]
    in_specs = [a_spec, pl.BlockSpec((k, d), lambda i: (0, 0)), row, vec, vec]
    if target is not None:
        dz, dzb, dg, db, l = pl.pallas_call(
            body, name=name, grid=(t // tm,), in_specs=in_specs + [row],
            out_specs=[row, row, vec, vec, pl.BlockSpec((1, LANES), lambda i: (0, 0))],
            out_shape=[jax.ShapeDtypeStruct((t, d), F32), jax.ShapeDtypeStruct((t, d), BF16),
                       jax.ShapeDtypeStruct((1, d), F32), jax.ShapeDtypeStruct((1, d), F32),
                       jax.ShapeDtypeStruct((1, LANES), F32)],
            compiler_params=_params("arbitrary"),
        )(*ins, target)
        return dz, dzb, dg[0], db[0], l[0, 0]
    return pl.pallas_call(
        body, name=name, grid=(t // tm,), in_specs=in_specs, out_specs=[row, row, row],
        out_shape=[jax.ShapeDtypeStruct((t, d), F32)] * 2 + [jax.ShapeDtypeStruct((t, d), BF16)],
        compiler_params=_params("parallel"),
    )(*ins)


def _ln_bwd(xa, xb, g, dya, ca, dyb, name):
    t, d = xa.shape
    tb = _pick(t, ROW_TILES)
    two = dyb is not None

    def body(*refs):
        if two:
            xa_ref, xb_ref, g_ref, dya_ref, dyb_ref, dz_ref, dzb_ref, dg_ref, db_ref = refs
            dy = ca * dya_ref[...] + dyb_ref[...]
        else:
            xa_ref, xb_ref, g_ref, dya_ref, dz_ref, dzb_ref, dg_ref, db_ref = refs
            dy = ca * dya_ref[...]
        z = ALPHA * xa_ref[...] + xb_ref[...]
        mu = jnp.mean(z, axis=-1, keepdims=True)
        zc = z - mu
        var = jnp.mean(zc * zc, axis=-1, keepdims=True)
        rstd = lax.rsqrt(var + LN_EPS)
        xhat = zc * rstd
        dxh = dy * g_ref[...]
        m1 = jnp.mean(dxh, axis=-1, keepdims=True)
        m2 = jnp.mean(dxh * xhat, axis=-1, keepdims=True)
        dz = rstd * (dxh - m1 - xhat * m2)
        dz_ref[...] = dz
        dzb_ref[...] = dz.astype(BF16)

        @pl.when(pl.program_id(0) == 0)
        def _():
            dg_ref[...] = jnp.zeros_like(dg_ref)
            db_ref[...] = jnp.zeros_like(db_ref)

        dg_ref[...] += jnp.sum(dy * xhat, axis=0, keepdims=True)
        db_ref[...] += jnp.sum(dy, axis=0, keepdims=True)

    row = pl.BlockSpec((tb, d), lambda i: (i, 0))
    vec = pl.BlockSpec((1, d), lambda i: (0, 0))
    ins = [xa, xb, g.reshape(1, d), dya] + ([dyb] if two else [])
    dz, dzb, dg, db = pl.pallas_call(
        body, name=name, grid=(t // tb,),
        in_specs=[row, row, vec, row] + ([row] if two else []),
        out_specs=[row, row, vec, vec],
        out_shape=[jax.ShapeDtypeStruct((t, d), F32), jax.ShapeDtypeStruct((t, d), BF16),
                   jax.ShapeDtypeStruct((1, d), F32), jax.ShapeDtypeStruct((1, d), F32)],
        compiler_params=_params("arbitrary"),
    )(*ins)
    return dz, dzb, dg[0], db[0]


def _axpy(ca, a, b, name):
    t, d = a.shape
    tb = _pick(t, ROW_TILES)

    def body(a_ref, b_ref, o_ref):
        o_ref[...] = ca * a_ref[...] + b_ref[...]

    row = pl.BlockSpec((tb, d), lambda i: (i, 0))
    return pl.pallas_call(
        body, name=name, grid=(t // tb,), in_specs=[row, row], out_specs=row,
        out_shape=jax.ShapeDtypeStruct((t, d), F32), compiler_params=_params("parallel"),
    )(a, b)


def _ffn_in_swiglu(xb, g4, name):
    (t, k), (_, nb, _, ns) = xb.shape, g4.shape
    tm = _pick(t, M_TILES)

    def body(x_ref, w_ref, h_ref, gu_ref):
        xv = x_ref[...]
        gate = jnp.dot(xv, w_ref[0], preferred_element_type=F32)
        up = jnp.dot(xv, w_ref[1], preferred_element_type=F32)
        h_ref[...] = (gate * jax.nn.sigmoid(gate) * up).astype(BF16)
        gu_ref[0] = gate.astype(BF16)
        gu_ref[1] = up.astype(BF16)

    return pl.pallas_call(
        body, name=name, grid=(t // tm, nb),
        in_specs=[pl.BlockSpec((tm, k), lambda i, d: (i, 0)),
                  pl.BlockSpec((2, None, k, ns), lambda i, d: (0, d, 0, 0))],
        out_specs=[pl.BlockSpec((None, tm, ns), lambda i, d: (d, i, 0)),
                   pl.BlockSpec((2, None, tm, ns), lambda i, d: (0, d, i, 0))],
        out_shape=[jax.ShapeDtypeStruct((nb, t, ns), BF16), jax.ShapeDtypeStruct((2, nb, t, ns), BF16)],
        compiler_params=_params("parallel", "parallel"),
    )(xb, g4)


def _ffn_out_dx_swiglu(dz, w_out, gu4, name):
    (t, d), (_, nb, _, ns) = dz.shape, gu4.shape
    tm = _pick(t, M_TILES)

    def body(dz_ref, w_ref, gu_ref, o_ref):
        halves = [slice(0, tm // 2), slice(tm // 2, tm)] if tm % 16 == 0 else [slice(0, tm)]
        dhs = [lax.dot_general(dz_ref[rows, :].astype(BF16), w_ref[...], NT, preferred_element_type=F32)
               for rows in halves]
        for rows, dh in zip(halves, dhs):
            gate = gu_ref[0, rows, :].astype(F32)
            up = gu_ref[1, rows, :].astype(F32)
            sg = jax.nn.sigmoid(gate)
            silu = gate * sg
            o_ref[0, rows, :] = (dh * up * (sg + silu * (1.0 - sg))).astype(BF16)
            o_ref[1, rows, :] = (dh * silu).astype(BF16)

    blk = pl.BlockSpec((2, None, tm, ns), lambda i, j: (0, j, i, 0))
    return pl.pallas_call(
        body, name=name, grid=(t // tm, nb),
        in_specs=[pl.BlockSpec((tm, d), lambda i, j: (i, 0)), pl.BlockSpec((ns, d), lambda i, j: (j, 0)), blk],
        out_specs=blk,
        out_shape=jax.ShapeDtypeStruct((2, nb, t, ns), BF16),
        compiler_params=_params("parallel", "parallel"),
    )(dz, w_out, gu4)


def _tri_matmul(tri, x):
    x1 = x.astype(BF16)
    r1 = x - x1.astype(F32)
    x2 = r1.astype(BF16)
    x3 = (r1 - x2.astype(F32)).astype(BF16)
    dot = lambda v: jnp.dot(tri, v, preferred_element_type=F32)
    return dot(x1) + dot(x2) + dot(x3)


def _fgate_fwd(proj, bf_pad, fcol, n_heads, name):
    t = proj.shape[0]
    tb = _pick(t, ROW_TILES)

    def body(p_ref, b_ref, c_ref, carry):
        @pl.when(pl.program_id(0) == 0)
        def _():
            carry[...] = jnp.zeros_like(carry)

        z = p_ref[...] + b_ref[...]
        lf = jnp.minimum(z, 0.0) - jnp.log1p(jnp.exp(-jnp.abs(z)))
        lane = lax.broadcasted_iota(jnp.int32, (tb, LANES), 1)
        lf = jnp.where(lane < n_heads, lf, 0.0)
        r = lax.broadcasted_iota(jnp.int32, (tb, tb), 0)
        s = lax.broadcasted_iota(jnp.int32, (tb, tb), 1)
        tri = (s <= r).astype(BF16)
        c = _tri_matmul(tri, lf) + carry[...]
        c_ref[...] = c
        carry[...] = c[tb - 1:tb, :]

    return pl.pallas_call(
        body, name=name, grid=(t // tb,),
        in_specs=[pl.BlockSpec((tb, LANES), lambda i: (i, fcol)), pl.BlockSpec((1, LANES), lambda i: (0, 0))],
        out_specs=pl.BlockSpec((tb, LANES), lambda i: (i, 0)),
        out_shape=jax.ShapeDtypeStruct((t, LANES), F32),
        scratch_shapes=[pltpu.VMEM((1, LANES), F32)],
        compiler_params=_params("arbitrary"),
    )(proj, bf_pad)


def _fgate_bwd(proj, bf_pad, dcq, dck, fcol, n_heads, name):
    t = proj.shape[0]
    tb = _pick(t, ROW_TILES)
    nb = t // tb

    def body(p_ref, b_ref, dcq_ref, dck_ref, dz_ref, db_ref, carry):
        @pl.when(pl.program_id(0) == 0)
        def _():
            carry[...] = jnp.zeros_like(carry)
            db_ref[...] = jnp.zeros_like(db_ref)

        r = lax.broadcasted_iota(jnp.int32, (tb, tb), 0)
        s = lax.broadcasted_iota(jnp.int32, (tb, tb), 1)
        tri = (s >= r).astype(BF16)
        dlf = _tri_matmul(tri, dcq_ref[...] + dck_ref[...]) + carry[...]
        carry[...] = dlf[0:1, :]
        z = p_ref[...] + b_ref[...]
        lane = lax.broadcasted_iota(jnp.int32, (tb, LANES), 1)
        dz = jnp.where(lane < n_heads, dlf * jax.nn.sigmoid(-z), 0.0)
        dz_ref[...] = dz
        db_ref[...] += jnp.sum(dz, axis=0, keepdims=True)

    dz, db = pl.pallas_call(
        body, name=name, grid=(nb,),
        in_specs=[pl.BlockSpec((tb, LANES), lambda i: (nb - 1 - i, fcol)),
                  pl.BlockSpec((1, LANES), lambda i: (0, 0)),
                  pl.BlockSpec((tb, LANES), lambda i: (nb - 1 - i, 0)),
                  pl.BlockSpec((tb, LANES), lambda i: (nb - 1 - i, 0))],
        out_specs=[pl.BlockSpec((tb, LANES), lambda i: (nb - 1 - i, 0)),
                   pl.BlockSpec((1, LANES), lambda i: (0, 0))],
        out_shape=[jax.ShapeDtypeStruct((t, LANES), F32), jax.ShapeDtypeStruct((1, LANES), F32)],
        scratch_shapes=[pltpu.VMEM((1, LANES), F32)],
        compiler_params=_params("arbitrary"),
    )(proj, bf_pad, dcq, dck)
    return dz, db[0]


def _split3(x):
    hi = x.astype(BF16)
    r = x - hi.astype(F32)
    mid = r.astype(BF16)
    return hi, mid, (r - mid.astype(F32)).astype(BF16)


def _attn_fwd(qa, ka, va, dh, name, gather=()):
    nh, t, da = qa.shape
    tq = _pick(t, ROW_TILES)
    hb = 2 if nh % 2 == 0 else 1
    heads = range(hb)
    n = len(gather)
    steps = (nh // hb, t // tq)

    def body(q_ref, k_ref, v_ref, *rest):
        x_refs, o_ref, g_refs = rest[:n], rest[n], rest[n + 1:2 * n + 1]
        m_s, acc_s, s_a, s_b = rest[2 * n + 1:2 * n + 5]
        qi = pl.program_id(1)
        if n:
            start, forward, finish = _gather_phases(x_refs, g_refs, *rest[2 * n + 5:])
            at = lambda hh, qq: jnp.logical_and(pl.program_id(0) == hh, qi == qq)
            pl.when(at(0, 0))(start)
            pl.when(at(steps[0] // 2, 0))(forward)
        m_s[...] = jnp.full(m_s.shape, NEG_BIG, F32)
        acc_s[...] = jnp.zeros_like(acc_s)

        def scores(s_ref, j):
            off = pl.multiple_of(j * tq, tq)
            for g in heads:
                s_ref[g] = lax.dot_general(q_ref[g], k_ref[g, pl.ds(off, tq), :], NT, preferred_element_type=F32)

        def absorb(s_ref, j, diagonal):
            off = pl.multiple_of(j * tq, tq)
            s = [s_ref[g] for g in heads]
            if diagonal:
                row = lax.broadcasted_iota(jnp.int32, (tq, tq), 0)
                col = lax.broadcasted_iota(jnp.int32, (tq, tq), 1)
                s = [jnp.where(col > row, NEG_BIG, sg) for sg in s]
            m_prev = [m_s[g] for g in heads]
            m_new = [jnp.maximum(m_prev[g], jnp.max(s[g], axis=1, keepdims=True)) for g in heads]
            p = [jnp.exp(s[g] - m_new[g]).astype(BF16) for g in heads]
            pv = [jnp.dot(p[g], v_ref[g, pl.ds(off, tq), :], preferred_element_type=F32) for g in heads]
            for g in heads:
                acc_s[g] = jnp.exp(m_prev[g] - m_new[g]) * acc_s[g] + pv[g]
                m_s[g] = m_new[g]

        def two_blocks(r, carry):
            scores(s_b, 2 * r + 1)
            absorb(s_a, 2 * r, False)
            scores(s_a, 2 * r + 2)
            absorb(s_b, 2 * r + 1, False)
            return carry

        scores(s_a, 0)
        rounds = qi // 2
        lax.fori_loop(0, rounds, two_blocks, 0)

        @pl.when(qi % 2 == 0)
        def _():
            absorb(s_a, qi, True)

        @pl.when(qi % 2 == 1)
        def _():
            scores(s_b, qi)
            absorb(s_a, qi - 1, False)
            absorb(s_b, qi, True)

        lane = lax.broadcasted_iota(jnp.int32, (tq, da), 1)
        for g in heads:
            acc = acc_s[g]
            l = jnp.sum(jnp.where(lane == dh, acc, 0.0), axis=1, keepdims=True)
            o_ref[g] = jnp.where(lane == dh, m_s[g] + jnp.log(l), acc / l)
        if n:
            pl.when(at(steps[0] - 1, steps[1] - 1))(finish)

    full = pl.BlockSpec((hb, t, da), lambda h, qi: (h, 0, 0))
    blk = pl.BlockSpec((hb, tq, da), lambda h, qi: (h, qi, 0))
    return pl.pallas_call(
        body, name=name, grid=steps,
        in_specs=[blk, full, full] + [ANY] * n, out_specs=[blk] + [ANY] * n,
        out_shape=[jax.ShapeDtypeStruct((nh, t, da), F32)] + _gather_shapes(gather),
        scratch_shapes=[pltpu.VMEM((hb, tq, 1), F32), pltpu.VMEM((hb, tq, da), F32),
                        pltpu.VMEM((hb, tq, tq), F32), pltpu.VMEM((hb, tq, tq), F32)] + (_gather_sems(n) if n else []),
        compiler_params=_params("arbitrary", "arbitrary"),
    )(qa, ka, va, *gather)


def _attn_bwd(qa, ka, va, doa, name, exchange=()):
    nh, t, da = qa.shape
    tq = _pick(t, ROW_TILES)
    nq = t // tq
    n = len(exchange)

    def body(q_ref, do_ref, k_ref, v_ref, *rest):
        p_refs, (dq_ref, dk_ref, dv_ref), r_refs = rest[:n], rest[n:n + 3], rest[n + 3:2 * n + 3]
        kj = pl.program_id(1)
        if n:
            start, finish = _chip_exchange_phases(p_refs, r_refs, *rest[2 * n + 3:])
            pl.when(jnp.logical_and(pl.program_id(0) == 0, kj == 0))(start)

        @pl.when(kj == 0)
        def _():
            dq_ref[...] = jnp.zeros_like(dq_ref)

        dk_ref[...] = jnp.zeros_like(dk_ref)
        dv_ref[...] = jnp.zeros_like(dv_ref)
        kb = k_ref[...]
        vb = v_ref[...]

        def step(i, diagonal, blocks=1):
            off = pl.multiple_of(i * tq, tq)
            rows = blocks * tq
            qb = q_ref[pl.ds(off, rows), :]
            dob = do_ref[pl.ds(off, rows), :]
            st = lax.dot_general(kb, qb, NT, preferred_element_type=F32)
            if diagonal:
                row = lax.broadcasted_iota(jnp.int32, (tq, tq), 0)
                col = lax.broadcasted_iota(jnp.int32, (tq, tq), 1)
                st = jnp.where(row > col, NEG_BIG, st)
            pt = jnp.exp(st)
            dst = (pt * lax.dot_general(vb, dob, NT, preferred_element_type=F32)).astype(BF16)
            dv_ref[...] += jnp.dot(pt.astype(BF16), dob, preferred_element_type=F32)
            dk_ref[...] += jnp.dot(dst, qb, preferred_element_type=F32)
            dq_ref[pl.ds(off, rows), :] += lax.dot_general(dst, kb, TN, preferred_element_type=F32)

        step(kj, True)
        odd = (nq - 1 - kj) % 2

        @pl.when(odd == 1)
        def _():
            step(kj + 1, False)

        def loop(r, carry):
            step(kj + 1 + odd + 2 * r, False, blocks=2)
            return carry

        lax.fori_loop(0, (nq - 1 - kj) // 2, loop, 0)
        if n:
            pl.when(jnp.logical_and(pl.program_id(0) == nh - 1, kj == nq - 1))(finish)

    full = pl.BlockSpec((None, t, da), lambda h, j: (h, 0, 0))
    blk = pl.BlockSpec((None, tq, da), lambda h, j: (h, j, 0))
    return pl.pallas_call(
        body, name=name, grid=(nh, nq),
        in_specs=[full, full, blk, blk] + [ANY] * n, out_specs=[full, blk, blk] + [ANY] * n,
        out_shape=[jax.ShapeDtypeStruct((nh, t, da), F32)] * 3 + _chip_exchange_shapes(exchange),
        scratch_shapes=_chip_exchange_sems(n) if n else [],
        compiler_params=_params("arbitrary", "arbitrary"),
    )(qa, doa, ka, va, *exchange)


def _head_group(dh, h):
    g = h // (LANES // dh)
    return slice(g * LANES, (g + 1) * LANES)


def _head_select(dh, h, to_heads):
    r = lax.broadcasted_iota(jnp.int32, (LANES, LANES), 0)
    c = lax.broadcasted_iota(jnp.int32, (LANES, LANES), 1)
    nat, col = (r, c) if to_heads else (c, r)
    return jnp.logical_and(nat == col + (h % (LANES // dh)) * dh, col < dh).astype(BF16)


def _column(x, lane, j):
    return jnp.sum(jnp.where(lane == j, x, 0.0), axis=1, keepdims=True)


def _bias_columns(lane, first, value):
    out = jnp.zeros(lane.shape, F32)
    for j, term in enumerate(_split3(value)):
        out = out + jnp.where(lane == first + j, -term.astype(F32), 0.0)
    return out


def _attn_pack(proj, cgate, w, nh, scale, name):
    t = proj.shape[0]
    dh = w // nh
    tb = _pick(t, ROW_TILES)

    def body(q_ref, k_ref, v_ref, c_ref, qa_ref, ka_ref, va_ref):
        lane = lax.broadcasted_iota(jnp.int32, (tb, LANES), 1)
        ones_qv = jnp.where(jnp.logical_and(lane >= dh, lane < dh + 3), 1.0, 0.0)
        ones_k = jnp.where(jnp.logical_and(lane >= dh + 3, lane < dh + 7), 1.0, 0.0)
        qb = (q_ref[...] * scale).astype(BF16)
        kb = k_ref[...].astype(BF16)
        vb = v_ref[...].astype(BF16)
        cblk = c_ref[...]
        for h in range(nh):
            sel, grp = _head_select(dh, h, True), _head_group(dh, h)
            qa_ref[h] = (jnp.dot(qb[:, grp], sel, preferred_element_type=F32) + ones_qv).astype(BF16)
            va_ref[h] = (jnp.dot(vb[:, grp], sel, preferred_element_type=F32) + ones_qv).astype(BF16)
            bias = _bias_columns(lane, dh, _column(cblk, lane, h))
            ka_ref[h] = (jnp.dot(kb[:, grp], sel, preferred_element_type=F32) + bias + ones_k).astype(BF16)

    col = lambda j: pl.BlockSpec((tb, w), lambda i: (i, j))
    out = pl.BlockSpec((nh, tb, LANES), lambda i: (0, i, 0))
    return pl.pallas_call(
        body, name=name, grid=(t // tb,),
        in_specs=[col(0), col(1), col(2), pl.BlockSpec((tb, LANES), lambda i: (i, 0))],
        out_specs=[out, out, out],
        out_shape=[jax.ShapeDtypeStruct((nh, t, LANES), BF16)] * 3,
        compiler_params=_params("parallel"),
    )(proj, proj, proj, cgate)


def _attn_pack_bwd(dmix, oa, qa, w, nh, name):
    t = dmix.shape[0]
    dh = w // nh
    tb = _pick(t, ROW_TILES)

    def body(d_ref, oa_ref, qa_ref, doa_ref, qa2_ref):
        lane = lax.broadcasted_iota(jnp.int32, (tb, LANES), 1)
        db = d_ref[...].astype(BF16)
        for h in range(nh):
            do_h = jnp.dot(db[:, _head_group(dh, h)], _head_select(dh, h, True), preferred_element_type=F32)
            o_h = oa_ref[h]
            delta = jnp.sum(jnp.where(lane < dh, do_h * o_h, 0.0), axis=1, keepdims=True)
            doa_ref[h] = (do_h + _bias_columns(lane, dh, delta)).astype(BF16)
            qa2_ref[h] = (qa_ref[h].astype(F32) + _bias_columns(lane, dh + 4, _column(o_h, lane, dh))).astype(BF16)

    blk = pl.BlockSpec((nh, tb, LANES), lambda i: (0, i, 0))
    return pl.pallas_call(
        body, name=name, grid=(t // tb,),
        in_specs=[pl.BlockSpec((tb, w), lambda i: (i, 0)), blk, blk], out_specs=[blk, blk],
        out_shape=[jax.ShapeDtypeStruct((nh, t, LANES), BF16)] * 2,
        compiler_params=_params("parallel"),
    )(dmix, oa, qa)


def _attn_unpack(xa, w, nh, mult, sum_col, sum_sign, name):
    t = xa.shape[1]
    dh = w // nh
    tb = _pick(t, ROW_TILES)

    def body(x_ref, o_ref, *rest):
        lane = lax.broadcasted_iota(jnp.int32, (tb, LANES), 1)
        per = LANES // dh
        cols = jnp.zeros((tb, LANES), F32)
        for h0 in range(0, nh, per):
            acc = jnp.zeros((tb, LANES), F32)
            for h in range(h0, h0 + per):
                xh = x_ref[h]
                acc = acc + jnp.dot((xh * mult).astype(BF16), _head_select(dh, h, False), preferred_element_type=F32)
                if sum_col is not None:
                    cols = cols + jnp.where(lane == h, sum_sign * _column(xh, lane, sum_col), 0.0)
            o_ref[:, _head_group(dh, h0)] = acc.astype(BF16)
        if sum_col is not None:
            rest[0][...] = cols

    nat = pl.BlockSpec((tb, w), lambda i: (i, 0))
    lanes = pl.BlockSpec((tb, LANES), lambda i: (i, 0))
    return pl.pallas_call(
        body, name=name, grid=(t // tb,),
        in_specs=[pl.BlockSpec((nh, tb, LANES), lambda i: (0, i, 0))],
        out_specs=[nat, lanes] if sum_col is not None else [nat],
        out_shape=[jax.ShapeDtypeStruct((t, w), BF16)] + ([jax.ShapeDtypeStruct((t, LANES), F32)]
                                                            if sum_col is not None else []),
        compiler_params=_params("parallel"),
    )(xa)


def _conv_fwd(proj, cw, w, bcol, name):
    t = proj.shape[0]
    tb = _pick(t, ROW_TILES)
    hb = tb // SUBLANES

    def body(b_ref, c_ref, h_ref, cp_ref, hp_ref, w_ref, y_ref):
        i = pl.program_id(0)
        zp = jnp.where(i > 0, cp_ref[...] * hp_ref[...], 0.0)
        zext = jnp.concatenate([zp, c_ref[...] * h_ref[...]], axis=0)
        z1 = pltpu.roll(zext, 1, 0)[SUBLANES:]
        z2 = pltpu.roll(zext, 2, 0)[SUBLANES:]
        y = w_ref[2:3, :] * zext[SUBLANES:] + w_ref[1:2, :] * z1 + w_ref[0:1, :] * z2
        y_ref[...] = (b_ref[...] * y).astype(BF16)

    cur = lambda j: pl.BlockSpec((tb, w), lambda i: (i, bcol + j))
    prev = lambda j: pl.BlockSpec((SUBLANES, w), lambda i: (jnp.maximum(i * hb - 1, 0), bcol + j))
    return pl.pallas_call(
        body, name=name, grid=(t // tb,),
        in_specs=[cur(0), cur(1), cur(2), prev(1), prev(2), pl.BlockSpec(cw.shape, lambda i: (0, 0))],
        out_specs=pl.BlockSpec((tb, w), lambda i: (i, 0)),
        out_shape=jax.ShapeDtypeStruct((t, w), BF16), compiler_params=_params("parallel"),
    )(proj, proj, proj, proj, proj, cw)


def _conv_bwd(proj, cw, dmix, w, bcol, name):
    t = proj.shape[0]
    tb = _pick(t, ROW_TILES)
    hb = tb // SUBLANES
    nb = t // tb
    n_ext = tb + SUBLANES

    def body(b_ref, c_ref, h_ref, cp_ref, hp_ref, bn_ref, d_ref, dn_ref, w_ref, db_ref, dc_ref, dh_ref, dw_ref):
        i = pl.program_id(0)
        c = c_ref[...]
        hh = h_ref[...]
        zp = jnp.where(i > 0, cp_ref[...] * hp_ref[...], 0.0)
        zext = jnp.concatenate([zp, c * hh], axis=0)
        z0 = zext[SUBLANES:]
        z1 = pltpu.roll(zext, 1, 0)[SUBLANES:]
        z2 = pltpu.roll(zext, 2, 0)[SUBLANES:]
        y = w_ref[2:3, :] * z0 + w_ref[1:2, :] * z1 + w_ref[0:1, :] * z2
        d = d_ref[...]
        db_ref[...] = d * y
        dy = d * b_ref[...]
        dyn = jnp.where(i < nb - 1, dn_ref[...] * bn_ref[...], 0.0)
        dext = jnp.concatenate([dy, dyn], axis=0)
        dy1 = pltpu.roll(dext, n_ext - 1, 0)[:tb]
        dy2 = pltpu.roll(dext, n_ext - 2, 0)[:tb]
        dz = w_ref[2:3, :] * dy + w_ref[1:2, :] * dy1 + w_ref[0:1, :] * dy2
        dc_ref[...] = dz * hh
        dh_ref[...] = dz * c

        @pl.when(i == 0)
        def _():
            dw_ref[...] = jnp.zeros_like(dw_ref)

        dw_ref[0:1, :] += jnp.sum(dy * z2, axis=0, keepdims=True)
        dw_ref[1:2, :] += jnp.sum(dy * z1, axis=0, keepdims=True)
        dw_ref[2:3, :] += jnp.sum(dy * z0, axis=0, keepdims=True)

    cur = lambda j: pl.BlockSpec((tb, w), lambda i: (i, bcol + j))
    prev = lambda j: pl.BlockSpec((SUBLANES, w), lambda i: (jnp.maximum(i * hb - 1, 0), bcol + j))
    nxt = lambda col: pl.BlockSpec((SUBLANES, w), lambda i: (jnp.minimum((i + 1) * hb, nb * hb - 1), col))
    out = pl.BlockSpec((tb, w), lambda i: (i, 0))
    return pl.pallas_call(
        body, name=name, grid=(nb,),
        in_specs=[cur(0), cur(1), cur(2), prev(1), prev(2), nxt(bcol),
                  pl.BlockSpec((tb, w), lambda i: (i, 1)), nxt(1), pl.BlockSpec(cw.shape, lambda i: (0, 0))],
        out_specs=[out, out, out, pl.BlockSpec(cw.shape, lambda i: (0, 0))],
        out_shape=[jax.ShapeDtypeStruct((t, w), F32)] * 3 + [jax.ShapeDtypeStruct(cw.shape, F32)],
        compiler_params=_params("arbitrary"),
    )(proj, proj, proj, proj, proj, proj, dmix, dmix, cw)


SQRT_HALF = 0.7071067811865476
INV_SQRT_2PI = 0.3989422804014327


def _gelu(x):
    return 0.5 * x * (1.0 + lax.erf(x * SQRT_HALF))


def _gelu_grad(x):
    return 0.5 * (1.0 + lax.erf(x * SQRT_HALF)) + x * (INV_SQRT_2PI * jnp.exp(-0.5 * x * x))


def _sgu_fwd(uv, ln_g, ln_b, wm, bs_full, name):
    t, d2 = uv.shape
    d = d2 // 2
    ng, pb, _ = wm.shape
    gd = d // ng
    tb = _pick(t, ROW_TILES[1:] or ROW_TILES)
    assert tb % pb == 0

    def body(uv_ref, g_ref, b_ref, w_ref, bs_ref, o_ref):
        u = _gelu(uv_ref[:, :d])
        v = _gelu(uv_ref[:, d:])
        mu = jnp.mean(v, axis=-1, keepdims=True)
        vc = v - mu
        var = jnp.mean(vc * vc, axis=-1, keepdims=True)
        vn = (vc * lax.rsqrt(var + LN_EPS) * g_ref[...] + b_ref[...]).astype(BF16)
        for r in range(tb // pb):
            rows = slice(r * pb, (r + 1) * pb)
            for gi in range(ng):
                cols = slice(gi * gd, (gi + 1) * gd)
                s = jnp.dot(w_ref[gi], vn[rows, cols], preferred_element_type=F32) + bs_ref[:, cols]
                o_ref[rows, cols] = (u[rows, cols] * s).astype(BF16)

    vec = pl.BlockSpec((1, d), lambda i: (0, 0))
    return pl.pallas_call(
        body, name=name, grid=(t // tb,),
        in_specs=[pl.BlockSpec((tb, d2), lambda i: (i, 0)), vec, vec,
                  pl.BlockSpec(wm.shape, lambda i: (0, 0, 0)), pl.BlockSpec((pb, d), lambda i: (0, 0))],
        out_specs=pl.BlockSpec((tb, d), lambda i: (i, 0)),
        out_shape=jax.ShapeDtypeStruct((t, d), BF16), compiler_params=_params("parallel"),
    )(uv, ln_g.reshape(1, d), ln_b.reshape(1, d), wm, bs_full)


def _sgu_bwd(uv, ln_g, ln_b, wm, bs_full, dgated, name):
    t, d2 = uv.shape
    d = d2 // 2
    ng, pb, _ = wm.shape
    gd = d // ng
    tb = _pick(t, ROW_TILES[1:] or ROW_TILES)
    nb = t // tb

    def body(uv_ref, g_ref, b_ref, w_ref, bs_ref, dg_ref, o_ref, dw_ref, dbs_ref, dlg_ref, dlb_ref,
             du_s, dvn_s, dbs_s):
        i = pl.program_id(0)

        @pl.when(i == 0)
        def _():
            dw_ref[...] = jnp.zeros_like(dw_ref)
            dbs_s[...] = jnp.zeros_like(dbs_s)
            dlg_ref[...] = jnp.zeros_like(dlg_ref)
            dlb_ref[...] = jnp.zeros_like(dlb_ref)

        upre = uv_ref[:, :d]
        vpre = uv_ref[:, d:]
        u = _gelu(upre)
        v = _gelu(vpre)
        mu = jnp.mean(v, axis=-1, keepdims=True)
        vc = v - mu
        var = jnp.mean(vc * vc, axis=-1, keepdims=True)
        rstd = lax.rsqrt(var + LN_EPS)
        xhat = vc * rstd
        vn = (xhat * g_ref[...] + b_ref[...]).astype(BF16)
        dgt = dg_ref[...].astype(F32)
        for r in range(tb // pb):
            rows = slice(r * pb, (r + 1) * pb)
            for gi in range(ng):
                cols = slice(gi * gd, (gi + 1) * gd)
                vblk = vn[rows, cols]
                s = jnp.dot(w_ref[gi], vblk, preferred_element_type=F32) + bs_ref[:, cols]
                dblk = dgt[rows, cols]
                du_s[rows, cols] = dblk * s
                ds = dblk * u[rows, cols]
                dsb = ds.astype(BF16)
                dvn_s[rows, cols] = lax.dot_general(w_ref[gi], dsb, (((0,), (0,)), ((), ())),
                                                    preferred_element_type=F32)
                dw_ref[gi] += lax.dot_general(dsb, vblk, (((1,), (1,)), ((), ())), preferred_element_type=F32)
                dbs_s[:, cols] += ds
        dvn = dvn_s[...]
        dlg_ref[...] += jnp.sum(dvn * xhat, axis=0, keepdims=True)
        dlb_ref[...] += jnp.sum(dvn, axis=0, keepdims=True)
        dxh = dvn * g_ref[...]
        m1 = jnp.mean(dxh, axis=-1, keepdims=True)
        m2 = jnp.mean(dxh * xhat, axis=-1, keepdims=True)
        dv = rstd * (dxh - m1 - xhat * m2)
        o_ref[:, :d] = (du_s[...] * _gelu_grad(upre)).astype(BF16)
        o_ref[:, d:] = (dv * _gelu_grad(vpre)).astype(BF16)

        @pl.when(i == nb - 1)
        def _():
            lane = lax.broadcasted_iota(jnp.int32, (pb, LANES), 1)
            acc = jnp.zeros((pb, LANES), F32)
            for gi in range(ng):
                col = jnp.sum(dbs_s[:, gi * gd:(gi + 1) * gd], axis=1, keepdims=True)
                acc = acc + jnp.where(lane == gi, col, 0.0)
            dbs_ref[...] = acc

    vec = pl.BlockSpec((1, d), lambda i: (0, 0))
    duv, dw, dbs, dlg, dlb = pl.pallas_call(
        body, name=name, grid=(nb,),
        in_specs=[pl.BlockSpec((tb, d2), lambda i: (i, 0)), vec, vec,
                  pl.BlockSpec(wm.shape, lambda i: (0, 0, 0)), pl.BlockSpec((pb, d), lambda i: (0, 0)),
                  pl.BlockSpec((tb, d), lambda i: (i, 0))],
        out_specs=[pl.BlockSpec((tb, d2), lambda i: (i, 0)), pl.BlockSpec(wm.shape, lambda i: (0, 0, 0)),
                   pl.BlockSpec((pb, LANES), lambda i: (0, 0)), vec, vec],
        out_shape=[jax.ShapeDtypeStruct((t, d2), BF16), jax.ShapeDtypeStruct(wm.shape, F32),
                   jax.ShapeDtypeStruct((pb, LANES), F32), jax.ShapeDtypeStruct((1, d), F32),
                   jax.ShapeDtypeStruct((1, d), F32)],
        scratch_shapes=[pltpu.VMEM((tb, d), F32), pltpu.VMEM((tb, d), F32), pltpu.VMEM((pb, d), F32)],
        compiler_params=_params("arbitrary"),
    )(uv, ln_g.reshape(1, d), ln_b.reshape(1, d), wm, bs_full, dgated)
    return duv, dw, dbs, dlg[0], dlb[0]


def _adamw(w, g, m, v, name):
    shape = w.shape
    cols = shape[-1]
    rows = w.size // cols
    tr = _pick(rows, (512, 256, 352, 128, 64, 32, 16, 8))

    def body(w_ref, g_ref, m_ref, v_ref, d_ref, mo_ref, vo_ref):
        d_ref[...], mo_ref[...], vo_ref[...] = _adam_update(w_ref[...], g_ref[...], m_ref[...], v_ref[...])

    spec = pl.BlockSpec((tr, cols), lambda i: (i, 0))
    outs = pl.pallas_call(
        body, name=name, grid=(rows // tr,),
        in_specs=[spec] * 4, out_specs=[spec] * 3,
        out_shape=[jax.ShapeDtypeStruct((rows, cols), F32)] * 3,
        compiler_params=_params("parallel"),
    )(*[a.reshape(rows, cols) for a in (w, g, m, v)])
    return [o.reshape(shape) for o in outs]


ANY = pl.BlockSpec(memory_space=pl.ANY)


def _place():
    return lax.axis_index("x"), lax.axis_index("y"), lax.axis_index("c")


def _all_gather(shards, name):
    n = len(shards)

    def body(*refs):
        start, forward, finish = _gather_phases(refs[:n], refs[n:2 * n], *refs[2 * n:])
        start()
        forward()
        finish()

    return pl.pallas_call(
        body, name=name, in_specs=[ANY] * n, out_specs=[ANY] * n,
        out_shape=_gather_shapes(shards), scratch_shapes=_gather_sems(n),
    )(*shards)


def _gather_shapes(shards):
    return [jax.ShapeDtypeStruct((N_DEV,) + s.shape, s.dtype) for s in shards]


def _gather_sems(n):
    return [pltpu.SemaphoreType.DMA((7 * n,)), pltpu.SemaphoreType.DMA((7 * n,)), pltpu.SemaphoreType.DMA((n,))]


def _gather_phases(x_refs, out_refs, send_sems, recv_sems, local_sems):
    n = len(x_refs)
    x, y, c = _place()
    me, sibling = (x, y, c), (x, y, 1 - c)
    chips = [(1 - x, y), (x, 1 - y), (1 - x, 1 - y)]

    def copy(a, k, block, to, own=False):
        px, py, pc = block
        rows = out_refs[a].at[4 * px + 2 * py + pc]
        return pltpu.make_async_remote_copy(
            src_ref=x_refs[a] if own else rows, dst_ref=rows,
            send_sem=send_sems.at[7 * a + k], recv_sem=recv_sems.at[7 * a + k],
            device_id=to, device_id_type=MESH)

    def local(a):
        return pltpu.make_async_copy(x_refs[a], out_refs[a].at[4 * x + 2 * y + c], local_sems.at[a])

    def first(a):
        return [copy(a, 0, me, sibling, own=True)] + [copy(a, 1 + j, me, (*chip, c), own=True)
                                                      for j, chip in enumerate(chips)]

    def start():
        for a in range(n):
            local(a).start()
            for cp in first(a):
                cp.start()

    def forward():
        for j, chip in enumerate(chips):
            for a in range(n):
                copy(a, 1 + j, (*chip, c), me).wait_recv()
                copy(a, 4 + j, (*chip, c), sibling).start()

    def finish():
        for a in range(n):
            copy(a, 0, sibling, me).wait_recv()
            for j, chip in enumerate(chips):
                copy(a, 4 + j, (*chip, 1 - c), me).wait_recv()
        for a in range(n):
            for cp in first(a) + [copy(a, 4 + j, (*chip, c), sibling) for j, chip in enumerate(chips)]:
                cp.wait_send()
            local(a).wait()

    return start, forward, finish


def _rs_sibling_exchange(packed, name):
    rider = _sibling_exchange_rider(packed)
    n = len(packed)

    def body(*refs):
        start, finish = rider.phases(refs[:n], refs[n:2 * n], refs[2 * n:])
        start()
        finish()

    return pl.pallas_call(
        body, name=name, in_specs=rider.in_specs, out_specs=rider.out_specs, out_shape=rider.out_shapes,
        scratch_shapes=rider.scratch,
    )(*packed)


def _sibling_exchange_rider(packed):
    n = len(packed)

    def phases(p_refs, r_refs, scratch):
        send_sems, recv_sems = scratch
        x, y, c = _place()

        def copies():
            return [pltpu.make_async_remote_copy(
                src_ref=p_refs[a].at[2 * j + (1 - c)], dst_ref=r_refs[a].at[j],
                send_sem=send_sems.at[4 * a + j], recv_sem=recv_sems.at[4 * a + j],
                device_id=(x, y, 1 - c), device_id_type=MESH) for a in range(n) for j in range(4)]

        def start():
            for cp in copies():
                cp.start()

        def finish():
            for cp in copies():
                cp.wait()

        return start, finish

    return _Rider(list(packed), [ANY] * n, [jax.ShapeDtypeStruct((4,) + p.shape[1:], p.dtype) for p in packed],
                  [ANY] * n, [pltpu.SemaphoreType.DMA((4 * n,)), pltpu.SemaphoreType.DMA((4 * n,))], phases)


def _rs_chip_sum(packed, from_sibling, c_idx, name):
    _, r, cc = packed.shape
    tr = _pick(r, (512, 256, 352, 128))

    def body(c_ref, a_ref, b_ref, o_ref):
        o_ref[...] = (a_ref[...].astype(F32) + b_ref[...].astype(F32)).astype(o_ref.dtype)

    return pl.pallas_call(
        body, name=name,
        grid_spec=pltpu.PrefetchScalarGridSpec(
            num_scalar_prefetch=1, grid=(4, r // tr),
            in_specs=[pl.BlockSpec((None, tr, cc), lambda j, i, c_ref: (2 * j + c_ref[0], i, 0)),
                      pl.BlockSpec((None, tr, cc), lambda j, i, c_ref: (j, i, 0))],
            out_specs=pl.BlockSpec((None, tr, cc), lambda j, i, c_ref: (j, i, 0))),
        out_shape=jax.ShapeDtypeStruct((4, r, cc), packed.dtype),
        compiler_params=_params("parallel", "parallel"),
    )(c_idx, packed, from_sibling)


def _chip_exchange_shapes(partial):
    return [jax.ShapeDtypeStruct((3,) + p.shape[1:], p.dtype) for p in partial]


def _chip_exchange_sems(n):
    return [pltpu.SemaphoreType.DMA((3 * n,)), pltpu.SemaphoreType.DMA((3 * n,))]


def _chip_exchange_phases(p_refs, r_refs, send_sems, recv_sems):
    x, y, c = _place()
    chips = [(1 - x, y), (x, 1 - y), (1 - x, 1 - y)]

    def copies():
        return [pltpu.make_async_remote_copy(
            src_ref=p_refs[a].at[2 * tx + ty], dst_ref=r_refs[a].at[k],
            send_sem=send_sems.at[3 * a + k], recv_sem=recv_sems.at[3 * a + k],
            device_id=(tx, ty, c), device_id_type=MESH)
            for a in range(len(p_refs)) for k, (tx, ty) in enumerate(chips)]

    def start():
        for cp in copies():
            cp.start()

    def finish():
        for cp in copies():
            cp.wait()

    return start, finish


def _adam_update(w, g, m, v):
    mn = ADAM_B1 * m + (1.0 - ADAM_B1) * g
    vn = ADAM_B2 * v + (1.0 - ADAM_B2) * (g * g)
    m_hat = mn / (1.0 - ADAM_B1 ** ADAM_STEP)
    v_hat = vn / (1.0 - ADAM_B2 ** ADAM_STEP)
    return -ADAM_LR * (m_hat / (jnp.sqrt(v_hat) + ADAM_EPS) + ADAM_WD * w), mn, vn


def _rs_final_adamw(partial, received, chip_idx, w, m, v, name):
    _, r, cc = partial.shape
    tr = _pick(r, (512, 256, 352, 128))

    def body(c_ref, a_ref, r_ref, w_ref, m_ref, v_ref, g_ref, d_ref, mo_ref, vo_ref):
        g = a_ref[...].astype(F32)
        for k in range(3):
            g = g + r_ref[k].astype(F32)
        g_ref[...] = g
        d_ref[...], mo_ref[...], vo_ref[...] = _adam_update(w_ref[...], g, m_ref[...], v_ref[...])

    row = pl.BlockSpec((tr, cc), lambda i, c_ref: (i, 0))
    return pl.pallas_call(
        body, name=name,
        grid_spec=pltpu.PrefetchScalarGridSpec(
            num_scalar_prefetch=1, grid=(r // tr,),
            in_specs=[pl.BlockSpec((None, tr, cc), lambda i, c_ref: (c_ref[0], i, 0)),
                      pl.BlockSpec((3, tr, cc), lambda i, c_ref: (0, i, 0)), row, row, row],
            out_specs=[row] * 4),
        out_shape=[jax.ShapeDtypeStruct((r, cc), F32)] * 4,
        compiler_params=_params("parallel"),
    )(chip_idx, partial, received, w.reshape(r, cc), m.reshape(r, cc), v.reshape(r, cc))


def _all_reduce_small(vals, name):
    rider = _all_reduce_rider(vals)

    def body(v_ref, o_ref, *scratch):
        start, finish = rider.phases([v_ref], [o_ref], scratch)
        start()
        finish()

    return pl.pallas_call(
        body, name=name, in_specs=rider.in_specs, out_specs=rider.out_specs[0], out_shape=rider.out_shapes[0],
        scratch_shapes=rider.scratch, compiler_params=pltpu.CompilerParams(vmem_limit_bytes=VMEM_LIMIT),
    )(vals)


def _all_reduce_rider(vals):
    r, cc = vals.shape

    def phases(ins, outs, scratch):
        (v_ref,), (o_ref,), (buf, send_sems, recv_sems) = ins, outs, scratch
        x, y, c = _place()
        me = 4 * x + 2 * y + c

        def copies():
            cps = []
            for k in range(1, N_DEV):
                kx, ky, kc = (k >> 2) & 1, (k >> 1) & 1, k & 1
                peer = (1 - x if kx else x, 1 - y if ky else y, 1 - c if kc else c)
                cps.append(pltpu.make_async_remote_copy(
                    src_ref=buf.at[0], dst_ref=buf.at[k], send_sem=send_sems.at[k - 1],
                    recv_sem=recv_sems.at[k - 1], device_id=peer, device_id_type=MESH))
            return cps

        def start():
            buf[0] = v_ref[...]
            for cp in copies():
                cp.start()

        def finish():
            for cp in copies():
                cp.wait()
            acc = buf[jnp.bitwise_xor(me, 0)]
            for dev in range(1, N_DEV):
                acc = acc + buf[jnp.bitwise_xor(me, dev)]
            o_ref[...] = acc

        return start, finish

    vm = pl.BlockSpec(memory_space=pltpu.VMEM)
    return _Rider([vals], [vm], [jax.ShapeDtypeStruct((r, cc), F32)], [vm],
                  [pltpu.VMEM((N_DEV, r, cc), F32), pltpu.SemaphoreType.DMA((7,)), pltpu.SemaphoreType.DMA((7,))],
                  phases)


def _chip_exchange_rider(partial):
    n = len(partial)
    return _Rider(list(partial), [ANY] * n, _chip_exchange_shapes(partial), [ANY] * n, _chip_exchange_sems(n),
                  lambda ins, outs, scratch: _chip_exchange_phases(ins, outs, *scratch))


def _lanes(flat):
    pad = (-flat.shape[0]) % (SUBLANES * LANES)
    return jnp.pad(flat, (0, pad)).reshape(-1, LANES)


def kernel(x, even_w_in, even_b_f, even_conv_w, even_w_out, odd_w_in, odd_v_ln_g, odd_v_ln_b, odd_w_s, odd_b_s, odd_w_out, mix_ln_g, mix_ln_b, ffn_w_in, ffn_w_out, ffn_ln_g, ffn_ln_b, loss_target, m_even_w_in, m_even_b_f, m_even_conv_w, m_even_w_out, m_odd_w_in, m_odd_v_ln_g, m_odd_v_ln_b, m_odd_w_s, m_odd_b_s, m_odd_w_out, m_mix_ln_g, m_mix_ln_b, m_ffn_w_in, m_ffn_w_out, m_ffn_ln_g, m_ffn_ln_b, v_even_w_in, v_even_b_f, v_even_conv_w, v_even_w_out, v_odd_w_in, v_odd_v_ln_g, v_odd_v_ln_b, v_odd_w_s, v_odd_b_s, v_odd_w_out, v_mix_ln_g, v_mix_ln_b, v_ffn_w_in, v_ffn_w_out, v_ffn_ln_g, v_ffn_ln_b):
    t, d = x.shape[1], x.shape[2]
    nh = even_b_f.shape[-1]
    w = even_conv_w.shape[-1] * N_DEV
    dh = w // nh
    scale = dh ** -0.5
    e_in = even_w_in.shape[-1] * N_DEV
    f2 = ffn_w_in.shape[-1] * N_DEV
    f = f2 // 2
    ng, pb = odd_w_s.shape[1], odd_w_s.shape[2]
    assert e_in == 6 * w + nh and nh <= SUBLANES and (6 * w) % LANES == 0 and d % N_DEV == 0
    mx, my, mc = _place()
    me = 4 * mx + 2 * my + mc

    big = [even_w_in[0], even_w_out[0], odd_w_in[0], odd_w_out[0],
           ffn_w_in[0], ffn_w_in[1], ffn_w_out[0], ffn_w_out[1]]
    g_in0, = _all_gather([big[0].astype(BF16)], "ag_even_w_in")
    w_in0 = g_in0.transpose(1, 0, 2).reshape(d, e_in)
    w_all0 = jnp.concatenate([w_in0[:, :3 * w], w_in0[:, 3 * w + nh:], w_in0[:, 3 * w:3 * w + nh],
                              jnp.zeros((d, LANES - nh), BF16)], axis=1)

    cs, vs = even_conv_w.shape[-1], odd_v_ln_g.shape[-1]
    small_mine = jnp.concatenate([
        lax.dynamic_update_slice(jnp.zeros((3, w), F32), even_conv_w[0], (0, me * cs)).reshape(-1),
        lax.dynamic_update_slice(jnp.zeros((d,), F32), odd_v_ln_g[0], (me * vs,)),
        lax.dynamic_update_slice(jnp.zeros((d,), F32), odd_v_ln_b[0], (me * vs,))])
    small_all = _all_reduce_small(_lanes(small_mine), "ag_small").reshape(-1)
    conv_w = small_all[:3 * w].reshape(3, w)
    vln_g = small_all[3 * w:3 * w + d]
    vln_b = small_all[3 * w + d:3 * w + 2 * d]

    bf_pad = jnp.pad(even_b_f[0], (0, LANES - nh)).reshape(1, LANES)
    chunk = jnp.arange(pb) // (pb // 2)
    ws_mask = (chunk[None, :] <= chunk[:, None])[None]
    wm = jnp.where(ws_mask, odd_w_s[0], 0.0).astype(BF16)
    bs_full = jnp.repeat(odd_b_s[0].T, d // ng, axis=1)

    x0 = x[0]
    tgt = loss_target[0]
    fcol = 6 * w // LANES
    x0b = x0.astype(BF16)
    p0 = _mm(x0b, w_all0, "nn", F32, "l0_in_proj")
    cgate = _fgate_fwd(p0, bf_pad, fcol, nh, "l0_fgate")
    assert dh + 7 <= LANES
    qa, ka, va = _attn_pack(p0, cgate, w, nh, scale, "l0_attn_pack")
    oa, g_out0, g_in1, g_out1, g_fi0, g_fi1, g_fo0, g_fo1 = _attn_fwd(
        qa, ka, va, dh, "l0_attn", gather=[s.astype(BF16) for s in big[1:]])
    w_out0, w_out1 = g_out0.reshape(2 * w, d), g_out1.reshape(d, d)
    w_fo0, w_fo1 = g_fo0.reshape(f, d), g_fo1.reshape(f, d)
    nb = N_DEV // 2
    w_fi0, w_fi1 = g_fi0.reshape(2, nb, d, -1), g_fi1.reshape(2, nb, d, -1)
    attn, = _attn_unpack(oa, w, nh, 1.0, None, 1.0, "l0_attn_unpack")
    yconv = _conv_fwd(p0, conv_w, w, 3, "l0_conv")
    mix = jnp.concatenate([attn, yconv], axis=1)
    m0, x1, x1b = _mm_ln(mix, w_out0, x0, mix_ln_g[0], mix_ln_b[0], "l0_out_proj_ln")
    h0, gu0 = _ffn_in_swiglu(x1b, w_fi0, "l0_ffn_in")
    f0, x2, x2b = _mm_ln(h0, w_fo0, x1, ffn_ln_g[0], ffn_ln_b[0], "l0_ffn_out_ln")

    uv = _mm_cols_fwd(x2b, g_in1, False, F32, "l1_in_proj")
    gated = _sgu_fwd(uv, vln_g, vln_b, wm, bs_full, "l1_sgu")
    m1, x3, x3b = _mm_ln(gated, w_out1, x2, mix_ln_g[1], mix_ln_b[1], "l1_out_proj_ln")
    h1, gu1 = _ffn_in_swiglu(x3b, w_fi1, "l1_ffn_in")
    dz4, dz4b, g_ffn_g1, g_ffn_b1, loss_part = _mm_ln(h1, w_fo1, x3, ffn_ln_g[1], ffn_ln_b[1],
                                                      "l1_ffn_out_ln_loss", target=tgt)
    gd_fo1 = _mm_blk_dw(h1, dz4b, BF16, "l1_ffn_out_dw").reshape(N_DEV, -1, d)
    dgu1 = _ffn_out_dx_swiglu(dz4b, w_fo1, gu1, "l1_ffn_out_dx").reshape(N_DEV, t, -1)
    gd_fi1 = _mm_cols_dw(x3b, dgu1, N_DEV, True, BF16, "l1_ffn_in_dw")
    dx3 = _mm_cols_dx(dgu1, g_fi1, True, F32, "l1_ffn_in_dx")
    dz3, dz3b, g_mix_g1, g_mix_b1 = _ln_bwd(x2, m1, mix_ln_g[1], dz4, ALPHA, dx3, "l1_mix_ln_bwd")
    gd_out1 = _mm(gated, dz3b, "tn", BF16, "l1_out_proj_dw").reshape(N_DEV, -1, d)
    dgated = _mm(dz3b, w_out1, "nt", BF16, "l1_out_proj_dx")
    duv, g_wm, g_bs_t, g_vln_g, g_vln_b = _sgu_bwd(uv, vln_g, vln_b, wm, bs_full, dgated, "l1_sgu_bwd")
    gd_in1 = _mm_cols_dw(x2b, duv, N_DEV, False, BF16, "l1_in_proj_dw")
    dx2 = _mm_cols_dx(duv, g_in1, False, F32, "l1_in_proj_dx")

    dz2, dz2b, g_ffn_g0, g_ffn_b0 = _ln_bwd(x1, f0, ffn_ln_g[0], dz3, ALPHA, dx2, "l0_ffn_ln_bwd")
    gd_fo0 = _mm_blk_dw(h0, dz2b, BF16, "l0_ffn_out_dw").reshape(N_DEV, -1, d)
    dgu0 = _ffn_out_dx_swiglu(dz2b, w_fo0, gu0, "l0_ffn_out_dx").reshape(N_DEV, t, -1)
    gd_fi0 = _mm_cols_dw(x1b, dgu0, N_DEV, True, BF16, "l0_ffn_in_dw")
    dx1 = _mm_cols_dx(dgu0, g_fi0, True, F32, "l0_ffn_in_dx")
    dz1, dz1b, g_mix_g0, g_mix_b0 = _ln_bwd(x0, m0, mix_ln_g[0], dz2, ALPHA, dx1, "l0_mix_ln_bwd")
    gd_out0 = _mm(mix, dz1b, "tn", BF16, "l0_out_proj_dw").reshape(N_DEV, -1, d)
    early_g = [gd_out0, gd_in1, gd_out1, gd_fi0, gd_fi1, gd_fo0, gd_fo1]
    dmix, *early_sib = _mm(dz1b, w_out0, "nt", F32, "l0_out_proj_dx", rider=_sibling_exchange_rider(early_g))
    d_b, d_c, d_h, g_conv = _conv_bwd(p0, conv_w, dmix, w, 3, "l0_conv_bwd")
    doa, qa2 = _attn_pack_bwd(dmix, oa, qa, w, nh, "l0_attn_pack_bwd")
    big_names = ["even_w_in", "even_w_out", "odd_w_in", "odd_w_out", "ffn_w_in0", "ffn_w_in1", "ffn_w_out0", "ffn_w_out1"]
    c_idx = mc.reshape(1).astype(jnp.int32)
    chip_idx = (2 * mx + my).reshape(1).astype(jnp.int32)
    early_partial = [_rs_chip_sum(g, s, c_idx, "rs_chip_sum_" + n)
                     for g, s, n in zip(early_g, early_sib, big_names[1:])]
    dqa, dka, dva, *early_received = _attn_bwd(qa2, ka, va, doa, "l0_attn_bwd", exchange=early_partial)
    dq, dcq = _attn_unpack(dqa, w, nh, scale, dh + 3, 1.0, "l0_attn_unpack_dq")
    dk, dck = _attn_unpack(dka, w, nh, 1.0, dh, -1.0, "l0_attn_unpack_dk")
    dv, = _attn_unpack(dva, w, nh, 1.0, None, 1.0, "l0_attn_unpack_dv")
    dzf, g_bf = _fgate_bwd(p0, bf_pad, dcq, dck, fcol, nh, "l0_fgate_bwd")
    dp0 = jnp.concatenate([dq, dk, dv, d_b.astype(BF16), d_c.astype(BF16), d_h.astype(BF16), dzf.astype(BF16)], axis=1)
    g_ws = jnp.where(ws_mask, g_wm, 0.0)
    g_bs = g_bs_t[:, :ng].T
    small_g = [g_bf[:nh], g_conv, g_vln_g, g_vln_b, g_ws, g_bs,
               jnp.stack([g_mix_g0, g_mix_g1]), jnp.stack([g_mix_b0, g_mix_b1]),
               jnp.stack([g_ffn_g0, g_ffn_g1]), jnp.stack([g_ffn_b0, g_ffn_b1])]
    small_rider = _all_reduce_rider(_lanes(jnp.concatenate([a.reshape(-1) for a in small_g])))
    g_all0, small_sum = _mm(x0b, dp0, "tn", F32, "l0_in_proj_dw", rider=small_rider)
    gd_in0 = jnp.concatenate([g_all0[:, :3 * w], g_all0[:, 6 * w:6 * w + nh], g_all0[:, 3 * w:6 * w]], axis=1)
    gd_in0 = gd_in0.reshape(d, N_DEV, -1).transpose(1, 0, 2).astype(BF16)

    big_m = [m_even_w_in[0], m_even_w_out[0], m_odd_w_in[0], m_odd_w_out[0],
             m_ffn_w_in[0], m_ffn_w_in[1], m_ffn_w_out[0], m_ffn_w_out[1]]
    big_v = [v_even_w_in[0], v_even_w_out[0], v_odd_w_in[0], v_odd_w_out[0],
             v_ffn_w_in[0], v_ffn_w_in[1], v_ffn_w_out[0], v_ffn_w_out[1]]
    late_sib = _rs_sibling_exchange([gd_in0], "rs_sibling_late")
    late_partial = [_rs_chip_sum(gd_in0, late_sib[0], c_idx, "rs_chip_sum_" + big_names[0])]
    partial = late_partial + early_partial
    dx0, *late_received = _mm(dp0, w_all0, "nt", F32, "l0_in_proj_dx", rider=_chip_exchange_rider(late_partial))
    grad_x = _axpy(ALPHA, dz1, dx0, "grad_x")
    received = list(late_received) + list(early_received)
    upd = [_rs_final_adamw(p, r, chip_idx, wt, mt, vt, "rs_final_adamw_" + n)
           for p, r, wt, mt, vt, n in zip(partial, received, big, big_m, big_v, big_names)]
    big_out = {}
    for i, n in enumerate(["even_w_in", "even_w_out", "odd_w_in", "odd_w_out"]):
        big_out[n] = [o[None] for o in upd[i]]
    big_out["ffn_w_in"] = [jnp.stack([a, b]) for a, b in zip(upd[4], upd[5])]
    big_out["ffn_w_out"] = [jnp.stack([a, b]) for a, b in zip(upd[6], upd[7])]

    small_sum = small_sum.reshape(-1)
    outs_small = []
    off = 0
    for a in small_g:
        outs_small.append(small_sum[off:off + a.size].reshape(a.shape))
        off += a.size
    gr_bf, gr_conv, gr_vg, gr_vb, gr_ws, gr_bs, gr_mg, gr_mb, gr_fg, gr_fb = outs_small

    loss = lax.psum(loss_part, ("x", "y", "c"))

    grads = {
        "even_b_f": gr_bf[None],
        "even_conv_w": lax.dynamic_slice(gr_conv, (0, me * cs), (3, cs))[None],
        "odd_v_ln_g": lax.dynamic_slice(gr_vg, (me * vs,), (vs,))[None],
        "odd_v_ln_b": lax.dynamic_slice(gr_vb, (me * vs,), (vs,))[None],
        "odd_w_s": gr_ws[None], "odd_b_s": gr_bs[None],
        "mix_ln_g": gr_mg, "mix_ln_b": gr_mb, "ffn_ln_g": gr_fg, "ffn_ln_b": gr_fb,
    }
    weights = dict(even_w_in=even_w_in, even_b_f=even_b_f, even_conv_w=even_conv_w, even_w_out=even_w_out,
                   odd_w_in=odd_w_in, odd_v_ln_g=odd_v_ln_g, odd_v_ln_b=odd_v_ln_b, odd_w_s=odd_w_s,
                   odd_b_s=odd_b_s, odd_w_out=odd_w_out, mix_ln_g=mix_ln_g, mix_ln_b=mix_ln_b,
                   ffn_w_in=ffn_w_in, ffn_w_out=ffn_w_out, ffn_ln_g=ffn_ln_g, ffn_ln_b=ffn_ln_b)
    moms = dict(even_w_in=(m_even_w_in, v_even_w_in), even_b_f=(m_even_b_f, v_even_b_f),
                even_conv_w=(m_even_conv_w, v_even_conv_w), even_w_out=(m_even_w_out, v_even_w_out),
                odd_w_in=(m_odd_w_in, v_odd_w_in), odd_v_ln_g=(m_odd_v_ln_g, v_odd_v_ln_g),
                odd_v_ln_b=(m_odd_v_ln_b, v_odd_v_ln_b), odd_w_s=(m_odd_w_s, v_odd_w_s),
                odd_b_s=(m_odd_b_s, v_odd_b_s), odd_w_out=(m_odd_w_out, v_odd_w_out),
                mix_ln_g=(m_mix_ln_g, v_mix_ln_g), mix_ln_b=(m_mix_ln_b, v_mix_ln_b),
                ffn_w_in=(m_ffn_w_in, v_ffn_w_in), ffn_w_out=(m_ffn_w_out, v_ffn_w_out),
                ffn_ln_g=(m_ffn_ln_g, v_ffn_ln_g), ffn_ln_b=(m_ffn_ln_b, v_ffn_ln_b))
    names = list(weights)
    gout, deltas, new_m, new_v = [], [], [], []
    for n in names:
        if n in big_out:
            gr, dlt, mn, vn = big_out[n]
        else:
            gr = grads[n]
            dlt, mn, vn = _adamw(weights[n], gr, moms[n][0], moms[n][1], "adamw_" + n)
        gout.append(gr.reshape(weights[n].shape))
        deltas.append(dlt.reshape(weights[n].shape))
        new_m.append(mn.reshape(weights[n].shape))
        new_v.append(vn.reshape(weights[n].shape))
    return (loss, grad_x[None], *gout, *deltas, *new_m, *new_v)
```

```python
import functools
from typing import Callable, NamedTuple

import jax
import jax.numpy as jnp
from jax import lax
from jax.experimental import pallas as pl
from jax.experimental.pallas import tpu as pltpu

F32 = jnp.float32
BF16 = jnp.bfloat16
MESH = pl.DeviceIdType.MESH

DEPTH = 2
ALPHA = (2.0 * DEPTH) ** 0.25
LN_EPS = 1e-5
ADAM_LR = 0.001
ADAM_B1 = 0.9
ADAM_B2 = 0.999
ADAM_EPS = 1e-08
ADAM_WD = 0.01
ADAM_STEP = 10

N_DEV = 8
LANES = 128
SUBLANES = 8
VMEM_LIMIT = 48 * 1024 * 1024
NEG_BIG = -1e30
ROW_TILES = (512, 256, 128)


def _pick(n, cands):
    for c in cands:
        if c <= n and n % c == 0:
            return c
    return n


def _params(*sem):
    return pltpu.CompilerParams(dimension_semantics=sem, vmem_limit_bytes=VMEM_LIMIT)


NN = (((1,), (0,)), ((), ()))
NT = (((1,), (1,)), ((), ()))
TN = (((0,), (0,)), ((), ()))
M_TILES = (1024, 512, 1408, 256, 128)
N_TILES = (512, 640, 256, 128)
K_TILES = (2048, 1024, 512, 640, 1408, 256, 128)
K_WHOLE = 3328


class _Rider(NamedTuple):
    inputs: list
    in_specs: list
    out_shapes: list
    out_specs: list
    scratch: list
    phases: Callable


def _mm_core(name, grid, a, b, a_spec, b_spec, o_spec, o_shape, o_dtype, dims, tile, pieces=None, rider=None):
    nred = grid[2]
    pieces = pieces or [(lambda r: r[...], lambda r: r[...])]
    ni = len(rider.inputs) if rider else 0
    no = len(rider.out_shapes) if rider else 0
    nacc = 0 if nred == 1 else 1

    def body(a_ref, b_ref, *rest):
        o_ref = rest[ni]
        if rider:
            start, finish = rider.phases(rest[:ni], rest[ni + 1:ni + 1 + no], rest[ni + 1 + no + nacc:])
            ids = [pl.program_id(ax) for ax in range(3)]
            first = functools.reduce(jnp.logical_and, [i == 0 for i in ids])
            last = functools.reduce(jnp.logical_and, [i == g - 1 for i, g in zip(ids, grid)])
            pl.when(first)(start)
        part = None
        for fa, fb in pieces:
            prod = lax.dot_general(fa(a_ref).astype(BF16), fb(b_ref).astype(BF16), dims, preferred_element_type=F32)
            part = prod if part is None else part + prod
        if nred == 1:
            o_ref[...] = part.astype(o_ref.dtype)
        else:
            acc_ref = rest[ni + 1 + no]
            kk = pl.program_id(2)

            @pl.when(kk == 0)
            def _():
                acc_ref[...] = jnp.zeros_like(acc_ref)

            acc_ref[...] += part

            @pl.when(kk == nred - 1)
            def _():
                o_ref[...] = acc_ref[...].astype(o_ref.dtype)
        if rider:
            pl.when(last)(finish)

    out = pl.pallas_call(
        body, name=name, grid=grid,
        in_specs=[a_spec, b_spec] + (rider.in_specs if rider else []),
        out_specs=[o_spec] + (rider.out_specs if rider else []),
        out_shape=[jax.ShapeDtypeStruct(o_shape, o_dtype)] + (rider.out_shapes if rider else []),
        scratch_shapes=([] if nred == 1 else [pltpu.VMEM(tile, F32)]) + (rider.scratch if rider else []),
        compiler_params=_params(*(["arbitrary"] * 3 if rider else ["parallel", "parallel", "arbitrary"])),
    )(a, b, *(rider.inputs if rider else []))
    return out if rider else out[0]


def _mm(a, b, mode, out_dtype, name, rider=None):
    if mode == "nn":
        (m, k), (k2, n) = a.shape, b.shape
    elif mode == "nt":
        (m, k), (n, k2) = a.shape, b.shape
    else:
        (k, m), (k2, n) = a.shape, b.shape
    assert k == k2, (a.shape, b.shape, mode)
    tm, tn = _pick(m, M_TILES), _pick(n, N_TILES)
    tk = k if k <= K_WHOLE else _pick(k, K_TILES)
    if mode == "nn":
        a_spec = pl.BlockSpec((tm, tk), lambda i, j, kk: (i, kk))
        b_spec = pl.BlockSpec((tk, tn), lambda i, j, kk: (kk, j))
        dims = NN
    elif mode == "nt":
        a_spec = pl.BlockSpec((tm, tk), lambda i, j, kk: (i, kk))
        b_spec = pl.BlockSpec((tn, tk), lambda i, j, kk: (j, kk))
        dims = NT
    else:
        a_spec = pl.BlockSpec((tk, tm), lambda i, j, kk: (kk, i))
        b_spec = pl.BlockSpec((tk, tn), lambda i, j, kk: (kk, j))
        dims = TN
    return _mm_core(name, (m // tm, n // tn, k // tk), a, b, a_spec, b_spec,
                    pl.BlockSpec((tm, tn), lambda i, j, kk: (i, j)), (m, n), out_dtype, dims, (tm, tn), rider=rider)


def _act_spec(blocked, rows, ns, row_ax, d_ax):
    if blocked:
        return pl.BlockSpec((None, rows, ns), lambda *g: (g[d_ax], g[row_ax], 0))
    return pl.BlockSpec((rows, ns), lambda *g: (g[row_ax], g[d_ax]))


def _mm_cols_fwd(a, g3, blocked, out_dtype, name):
    (t, k), (nd, k2, ns) = a.shape, g3.shape
    assert k == k2
    tm, tk = _pick(t, M_TILES), _pick(k, K_TILES)
    return _mm_core(name, (t // tm, nd, k // tk), a, g3,
                    pl.BlockSpec((tm, tk), lambda i, d, kk: (i, kk)),
                    pl.BlockSpec((None, tk, ns), lambda i, d, kk: (d, kk, 0)),
                    _act_spec(blocked, tm, ns, 0, 1), (nd, t, ns) if blocked else (t, nd * ns), out_dtype, NN, (tm, ns))


def _mm_cols_dx_ln_bwd(dy, g3, blocked, xa, xb, gam, dya, name):
    nd, k, ns = g3.shape
    t, d = xa.shape
    assert k == d
    tm = _pick(t, ROW_TILES)
    grp = nd if not blocked else (2 if nd % 2 == 0 else 1)
    nred = nd // grp

    def body(a_ref, b_ref, xa_ref, xb_ref, g_ref, dya_ref, dz_ref, dzb_ref, dg_ref, db_ref, *acc):
        i, kk = pl.program_id(0), pl.program_id(1)

        @pl.when(jnp.logical_and(i == 0, kk == 0))
        def _():
            dg_ref[...] = jnp.zeros_like(dg_ref)
            db_ref[...] = jnp.zeros_like(db_ref)

        if blocked:
            part = None
            for s in range(grp):
                prod = lax.dot_general(a_ref[s], b_ref[s], NT, preferred_element_type=F32)
                part = prod if part is None else part + prod
        else:
            whole_b = jnp.concatenate([b_ref[s] for s in range(nd)], axis=1)
            part = lax.dot_general(a_ref[...], whole_b, NT, preferred_element_type=F32)

        def ln_bwd(dyb):
            dy_t = ALPHA * dya_ref[...] + dyb
            z = ALPHA * xa_ref[...] + xb_ref[...]
            mu = jnp.mean(z, axis=-1, keepdims=True)
            zc = z - mu
            var = jnp.mean(zc * zc, axis=-1, keepdims=True)
            rstd = lax.rsqrt(var + LN_EPS)
            xhat = zc * rstd
            dxh = dy_t * g_ref[...]
            m1 = jnp.mean(dxh, axis=-1, keepdims=True)
            m2 = jnp.mean(dxh * xhat, axis=-1, keepdims=True)
            dz = rstd * (dxh - m1 - xhat * m2)
            dz_ref[...] = dz
            dzb_ref[...] = dz.astype(BF16)
            dg_ref[...] += jnp.sum(dy_t * xhat, axis=0, keepdims=True)
            db_ref[...] += jnp.sum(dy_t, axis=0, keepdims=True)

        if nred == 1:
            ln_bwd(part)
        else:
            acc_ref, = acc

            @pl.when(kk == 0)
            def _():
                acc_ref[...] = part

            @pl.when(kk > 0)
            def _():
                acc_ref[...] += part

            @pl.when(kk == nred - 1)
            def _():
                ln_bwd(acc_ref[...])

    row = pl.BlockSpec((tm, d), lambda i, kk: (i, 0))
    vec = pl.BlockSpec((1, d), lambda i, kk: (0, 0))
    if blocked:
        a_spec = pl.BlockSpec((grp, tm, ns), lambda i, kk: (kk, i, 0))
        b_spec = pl.BlockSpec((grp, k, ns), lambda i, kk: (kk, 0, 0))
    else:
        a_spec = pl.BlockSpec((tm, nd * ns), lambda i, kk: (i, 0))
        b_spec = pl.BlockSpec((nd, k, ns), lambda i, kk: (0, 0, 0))
    dz, dzb, dg, db = pl.pallas_call(
        body, name=name, grid=(t // tm, nred),
        in_specs=[a_spec, b_spec, row, row, vec, row], out_specs=[row, row, vec, vec],
        out_shape=[jax.ShapeDtypeStruct((t, d), F32), jax.ShapeDtypeStruct((t, d), BF16),
                   jax.ShapeDtypeStruct((1, d), F32), jax.ShapeDtypeStruct((1, d), F32)],
        scratch_shapes=[] if nred == 1 else [pltpu.VMEM((tm, d), F32)],
        compiler_params=_params("arbitrary", "arbitrary"),
    )(dy, g3, xa, xb, gam.reshape(1, d), dya)
    return dz, dzb, dg[0], db[0]


def _mm_cols_dw(a, dy, nd, blocked, out_dtype, name):
    t, k = a.shape
    ns = dy.shape[2] if blocked else dy.shape[1] // nd
    tmk, tk = _pick(k, M_TILES), _pick(t, K_TILES)
    return _mm_core(name, (nd, k // tmk, t // tk), a, dy,
                    pl.BlockSpec((tk, tmk), lambda d, j, kk: (kk, j)),
                    _act_spec(blocked, tk, ns, 2, 0),
                    pl.BlockSpec((None, tmk, ns), lambda d, j, kk: (d, j, 0)), (nd, k, ns), out_dtype, TN, (tmk, ns))


def _mm_blk_dw(h3, dz, out_dtype, name):
    (nb, t, ns), (_, n) = h3.shape, dz.shape
    tn, tk = _pick(n, (1024,) + N_TILES), _pick(t, K_TILES)
    return _mm_core(name, (nb, n // tn, t // tk), h3, dz,
                    pl.BlockSpec((None, tk, ns), lambda d, j, kk: (d, kk, 0)),
                    pl.BlockSpec((tk, tn), lambda d, j, kk: (kk, j)),
                    pl.BlockSpec((ns, tn), lambda d, j, kk: (d, j)), (nb * ns, n), out_dtype, TN, (ns, tn))


def _mm_ln(a, w, xa, g, b, name, target=None):
    blocked = a.ndim == 3
    t, d = xa.shape
    k = w.shape[0]
    tm = _pick(t, ROW_TILES)
    nb = a.shape[0] if blocked else 1
    ns = k // nb
    halves = [slice(0, tm // 2), slice(tm // 2, tm)] if tm % 32 == 0 else [slice(0, tm)]

    def body(a_ref, w_ref, xa_ref, g_ref, b_ref, *rest):
        def product(rows):
            if not blocked:
                return jnp.dot(a_ref[rows, :], w_ref[...], preferred_element_type=F32)
            acc = None
            for s in range(nb):
                prod = jnp.dot(a_ref[s, rows, :], w_ref[s * ns:(s + 1) * ns, :], preferred_element_type=F32)
                acc = prod if acc is None else acc + prod
            return acc

        if target is not None:
            t_ref, dz_ref, dzb_ref, dg_ref, db_ref, l_ref = rest

            @pl.when(pl.program_id(0) == 0)
            def _():
                l_ref[...] = jnp.zeros_like(l_ref)
                dg_ref[...] = jnp.zeros_like(dg_ref)
                db_ref[...] = jnp.zeros_like(db_ref)
        else:
            xb_ref, y_ref, yb_ref = rest
        for rows, xb in zip(halves, [product(rows) for rows in halves]):
            z = ALPHA * xa_ref[rows, :] + xb
            mu = jnp.mean(z, axis=-1, keepdims=True)
            zc = z - mu
            var = jnp.mean(zc * zc, axis=-1, keepdims=True)
            rstd = lax.rsqrt(var + LN_EPS)
            xhat = zc * rstd
            y = xhat * g_ref[...] + b_ref[...]
            if target is not None:
                e = y - t_ref[rows, :]
                l_ref[...] += 0.5 * jnp.sum(jnp.mean(e * e, axis=-1, keepdims=True))
                dy = e * (1.0 / d)
                dxh = dy * g_ref[...]
                m1 = jnp.mean(dxh, axis=-1, keepdims=True)
                m2 = jnp.mean(dxh * xhat, axis=-1, keepdims=True)
                dz = rstd * (dxh - m1 - xhat * m2)
                dz_ref[rows, :] = dz
                dzb_ref[rows, :] = dz.astype(BF16)
                dg_ref[...] += jnp.sum(dy * xhat, axis=0, keepdims=True)
                db_ref[...] += jnp.sum(dy, axis=0, keepdims=True)
            else:
                xb_ref[rows, :] = xb
                y_ref[rows, :] = y
                yb_ref[rows, :] = y.astype(BF16)

    row = pl.BlockSpec((tm, d), lambda i: (i, 0))
    vec = pl.BlockSpec((1, d), lambda i: (0, 0))
    a_spec = pl.BlockSpec((nb, tm, ns), lambda i: (0, i, 0)) if blocked else pl.BlockSpec((tm, k), lambda i: (i, 0))
    ins = [a, w, xa, g.reshape(1, d), b.reshape(1, d)]
    in_specs = [a_spec, pl.BlockSpec((k, d), lambda i: (0, 0)), row, vec, vec]
    if target is not None:
        dz, dzb, dg, db, l = pl.pallas_call(
            body, name=name, grid=(t // tm,), in_specs=in_specs + [row],
            out_specs=[row, row, vec, vec, pl.BlockSpec((1, LANES), lambda i: (0, 0))],
            out_shape=[jax.ShapeDtypeStruct((t, d), F32), jax.ShapeDtypeStruct((t, d), BF16),
                       jax.ShapeDtypeStruct((1, d), F32), jax.ShapeDtypeStruct((1, d), F32),
                       jax.ShapeDtypeStruct((1, LANES), F32)],
            compiler_params=_params("arbitrary"),
        )(*ins, target)
        return dz, dzb, dg[0], db[0], l[0, 0]
    return pl.pallas_call(
        body, name=name, grid=(t // tm,), in_specs=in_specs, out_specs=[row, row, row],
        out_shape=[jax.ShapeDtypeStruct((t, d), F32)] * 2 + [jax.ShapeDtypeStruct((t, d), BF16)],
        compiler_params=_params("parallel"),
    )(*ins)


def _axpy(ca, a, b, name):
    t, d = a.shape
    tb = _pick(t, ROW_TILES)

    def body(a_ref, b_ref, o_ref):
        o_ref[...] = ca * a_ref[...] + b_ref[...]

    row = pl.BlockSpec((tb, d), lambda i: (i, 0))
    return pl.pallas_call(
        body, name=name, grid=(t // tb,), in_specs=[row, row], out_specs=row,
        out_shape=jax.ShapeDtypeStruct((t, d), F32), compiler_params=_params("parallel"),
    )(a, b)


def _ffn_in_swiglu(xb, g4, name):
    (t, k), (_, nb, _, ns) = xb.shape, g4.shape
    tm = _pick(t, M_TILES)

    def body(x_ref, w_ref, h_ref, gu_ref):
        xv = x_ref[...]
        gate = jnp.dot(xv, w_ref[0], preferred_element_type=F32)
        up = jnp.dot(xv, w_ref[1], preferred_element_type=F32)
        h_ref[...] = (gate * jax.nn.sigmoid(gate) * up).astype(BF16)
        gu_ref[0] = gate.astype(BF16)
        gu_ref[1] = up.astype(BF16)

    return pl.pallas_call(
        body, name=name, grid=(t // tm, nb),
        in_specs=[pl.BlockSpec((tm, k), lambda i, d: (i, 0)),
                  pl.BlockSpec((2, None, k, ns), lambda i, d: (0, d, 0, 0))],
        out_specs=[pl.BlockSpec((None, tm, ns), lambda i, d: (d, i, 0)),
                   pl.BlockSpec((2, None, tm, ns), lambda i, d: (0, d, i, 0))],
        out_shape=[jax.ShapeDtypeStruct((nb, t, ns), BF16), jax.ShapeDtypeStruct((2, nb, t, ns), BF16)],
        compiler_params=_params("parallel", "parallel"),
    )(xb, g4)


def _ffn_out_dx_swiglu(dz, w_out, gu4, name):
    (t, d), (_, nb, _, ns) = dz.shape, gu4.shape
    tm = _pick(t, M_TILES)

    def body(dz_ref, w_ref, gu_ref, o_ref):
        halves = [slice(0, tm // 2), slice(tm // 2, tm)] if tm % 16 == 0 else [slice(0, tm)]
        dhs = [lax.dot_general(dz_ref[rows, :].astype(BF16), w_ref[...], NT, preferred_element_type=F32)
               for rows in halves]
        for rows, dh in zip(halves, dhs):
            gate = gu_ref[0, rows, :].astype(F32)
            up = gu_ref[1, rows, :].astype(F32)
            sg = jax.nn.sigmoid(gate)
            silu = gate * sg
            o_ref[0, rows, :] = (dh * up * (sg + silu * (1.0 - sg))).astype(BF16)
            o_ref[1, rows, :] = (dh * silu).astype(BF16)

    blk = pl.BlockSpec((2, None, tm, ns), lambda i, j: (0, j, i, 0))
    return pl.pallas_call(
        body, name=name, grid=(t // tm, nb),
        in_specs=[pl.BlockSpec((tm, d), lambda i, j: (i, 0)), pl.BlockSpec((ns, d), lambda i, j: (j, 0)), blk],
        out_specs=blk,
        out_shape=jax.ShapeDtypeStruct((2, nb, t, ns), BF16),
        compiler_params=_params("parallel", "parallel"),
    )(dz, w_out, gu4)


def _tri_matmul(tri, x):
    x1 = x.astype(BF16)
    r1 = x - x1.astype(F32)
    x2 = r1.astype(BF16)
    x3 = (r1 - x2.astype(F32)).astype(BF16)
    dot = lambda v: jnp.dot(tri, v, preferred_element_type=F32)
    return dot(x1) + dot(x2) + dot(x3)


def _fgate_fwd(proj, bf_pad, fcol, n_heads, name):
    t = proj.shape[0]
    tb = _pick(t, ROW_TILES)

    def body(p_ref, b_ref, c_ref, carry):
        @pl.when(pl.program_id(0) == 0)
        def _():
            carry[...] = jnp.zeros_like(carry)

        z = p_ref[...] + b_ref[...]
        lf = jnp.minimum(z, 0.0) - jnp.log1p(jnp.exp(-jnp.abs(z)))
        lane = lax.broadcasted_iota(jnp.int32, (tb, LANES), 1)
        lf = jnp.where(lane < n_heads, lf, 0.0)
        r = lax.broadcasted_iota(jnp.int32, (tb, tb), 0)
        s = lax.broadcasted_iota(jnp.int32, (tb, tb), 1)
        tri = (s <= r).astype(BF16)
        c = _tri_matmul(tri, lf) + carry[...]
        c_ref[...] = c
        carry[...] = c[tb - 1:tb, :]

    return pl.pallas_call(
        body, name=name, grid=(t // tb,),
        in_specs=[pl.BlockSpec((tb, LANES), lambda i: (i, fcol)), pl.BlockSpec((1, LANES), lambda i: (0, 0))],
        out_specs=pl.BlockSpec((tb, LANES), lambda i: (i, 0)),
        out_shape=jax.ShapeDtypeStruct((t, LANES), F32),
        scratch_shapes=[pltpu.VMEM((1, LANES), F32)],
        compiler_params=_params("arbitrary"),
    )(proj, bf_pad)


def _fgate_bwd(proj, bf_pad, dcq, dck, fcol, n_heads, name):
    t = proj.shape[0]
    tb = _pick(t, ROW_TILES)
    nb = t // tb

    def body(p_ref, b_ref, dcq_ref, dck_ref, dz_ref, db_ref, carry):
        @pl.when(pl.program_id(0) == 0)
        def _():
            carry[...] = jnp.zeros_like(carry)
            db_ref[...] = jnp.zeros_like(db_ref)

        r = lax.broadcasted_iota(jnp.int32, (tb, tb), 0)
        s = lax.broadcasted_iota(jnp.int32, (tb, tb), 1)
        tri = (s >= r).astype(BF16)
        dlf = _tri_matmul(tri, dcq_ref[...] + dck_ref[...]) + carry[...]
        carry[...] = dlf[0:1, :]
        z = p_ref[...] + b_ref[...]
        lane = lax.broadcasted_iota(jnp.int32, (tb, LANES), 1)
        dz = jnp.where(lane < n_heads, dlf * jax.nn.sigmoid(-z), 0.0)
        dz_ref[...] = dz
        db_ref[...] += jnp.sum(dz, axis=0, keepdims=True)

    dz, db = pl.pallas_call(
        body, name=name, grid=(nb,),
        in_specs=[pl.BlockSpec((tb, LANES), lambda i: (nb - 1 - i, fcol)),
                  pl.BlockSpec((1, LANES), lambda i: (0, 0)),
                  pl.BlockSpec((tb, LANES), lambda i: (nb - 1 - i, 0)),
                  pl.BlockSpec((tb, LANES), lambda i: (nb - 1 - i, 0))],
        out_specs=[pl.BlockSpec((tb, LANES), lambda i: (nb - 1 - i, 0)),
                   pl.BlockSpec((1, LANES), lambda i: (0, 0))],
        out_shape=[jax.ShapeDtypeStruct((t, LANES), F32), jax.ShapeDtypeStruct((1, LANES), F32)],
        scratch_shapes=[pltpu.VMEM((1, LANES), F32)],
        compiler_params=_params("arbitrary"),
    )(proj, bf_pad, dcq, dck)
    return dz, db[0]


def _split3(x):
    hi = x.astype(BF16)
    r = x - hi.astype(F32)
    mid = r.astype(BF16)
    return hi, mid, (r - mid.astype(F32)).astype(BF16)


def _attn_fwd(qa, ka, va, dh, name, gather=()):
    nh, t, da = qa.shape
    tq = _pick(t, ROW_TILES)
    hb = 2 if nh % 2 == 0 else 1
    heads = range(hb)
    n = len(gather)
    steps = (nh // hb, t // tq)

    def body(q_ref, k_ref, v_ref, *rest):
        x_refs, o_ref, g_refs = rest[:n], rest[n], rest[n + 1:2 * n + 1]
        m_s, acc_s, s_a, s_b = rest[2 * n + 1:2 * n + 5]
        qi = pl.program_id(1)
        if n:
            start, forward, finish = _gather_phases(x_refs, g_refs, *rest[2 * n + 5:])
            at = lambda hh, qq: jnp.logical_and(pl.program_id(0) == hh, qi == qq)
            pl.when(at(0, 0))(start)
            pl.when(at(steps[0] // 2, 0))(forward)
        m_s[...] = jnp.full(m_s.shape, NEG_BIG, F32)
        acc_s[...] = jnp.zeros_like(acc_s)

        def scores(s_ref, j):
            off = pl.multiple_of(j * tq, tq)
            for g in heads:
                s_ref[g] = lax.dot_general(q_ref[g], k_ref[g, pl.ds(off, tq), :], NT, preferred_element_type=F32)

        def absorb(s_ref, j, diagonal):
            off = pl.multiple_of(j * tq, tq)
            s = [s_ref[g] for g in heads]
            if diagonal:
                row = lax.broadcasted_iota(jnp.int32, (tq, tq), 0)
                col = lax.broadcasted_iota(jnp.int32, (tq, tq), 1)
                s = [jnp.where(col > row, NEG_BIG, sg) for sg in s]
            m_prev = [m_s[g] for g in heads]
            m_new = [jnp.maximum(m_prev[g], jnp.max(s[g], axis=1, keepdims=True)) for g in heads]
            p = [jnp.exp(s[g] - m_new[g]).astype(BF16) for g in heads]
            pv = [jnp.dot(p[g], v_ref[g, pl.ds(off, tq), :], preferred_element_type=F32) for g in heads]
            for g in heads:
                acc_s[g] = jnp.exp(m_prev[g] - m_new[g]) * acc_s[g] + pv[g]
                m_s[g] = m_new[g]

        def two_blocks(r, carry):
            scores(s_b, 2 * r + 1)
            absorb(s_a, 2 * r, False)
            scores(s_a, 2 * r + 2)
            absorb(s_b, 2 * r + 1, False)
            return carry

        scores(s_a, 0)
        rounds = qi // 2
        lax.fori_loop(0, rounds, two_blocks, 0)

        @pl.when(qi % 2 == 0)
        def _():
            absorb(s_a, qi, True)

        @pl.when(qi % 2 == 1)
        def _():
            scores(s_b, qi)
            absorb(s_a, qi - 1, False)
            absorb(s_b, qi, True)

        lane = lax.broadcasted_iota(jnp.int32, (tq, da), 1)
        for g in heads:
            acc = acc_s[g]
            l = jnp.sum(jnp.where(lane == dh, acc, 0.0), axis=1, keepdims=True)
            o_ref[g] = jnp.where(lane == dh, m_s[g] + jnp.log(l), acc / l)
        if n:
            pl.when(at(steps[0] - 1, steps[1] - 1))(finish)

    full = pl.BlockSpec((hb, t, da), lambda h, qi: (h, 0, 0))
    blk = pl.BlockSpec((hb, tq, da), lambda h, qi: (h, qi, 0))
    return pl.pallas_call(
        body, name=name, grid=steps,
        in_specs=[blk, full, full] + [ANY] * n, out_specs=[blk] + [ANY] * n,
        out_shape=[jax.ShapeDtypeStruct((nh, t, da), F32)] + _gather_shapes(gather),
        scratch_shapes=[pltpu.VMEM((hb, tq, 1), F32), pltpu.VMEM((hb, tq, da), F32),
                        pltpu.VMEM((hb, tq, tq), F32), pltpu.VMEM((hb, tq, tq), F32)] + (_gather_sems(n) if n else []),
        compiler_params=_params("arbitrary", "arbitrary"),
    )(qa, ka, va, *gather)


def _attn_bwd(qa, ka, va, doa, name, exchange=()):
    nh, t, da = qa.shape
    tq = _pick(t, ROW_TILES)
    nq = t // tq
    n = len(exchange)

    def body(q_ref, do_ref, k_ref, v_ref, *rest):
        p_refs, (dq_ref, dk_ref, dv_ref), r_refs = rest[:n], rest[n:n + 3], rest[n + 3:2 * n + 3]
        kj = pl.program_id(1)
        if n:
            start, finish = _chip_exchange_phases(p_refs, r_refs, *rest[2 * n + 3:])
            pl.when(jnp.logical_and(pl.program_id(0) == 0, kj == 0))(start)

        @pl.when(kj == 0)
        def _():
            dq_ref[...] = jnp.zeros_like(dq_ref)

        dk_ref[...] = jnp.zeros_like(dk_ref)
        dv_ref[...] = jnp.zeros_like(dv_ref)
        kb = k_ref[...]
        vb = v_ref[...]

        def step(i, diagonal, blocks=1):
            off = pl.multiple_of(i * tq, tq)
            rows = blocks * tq
            qb = q_ref[pl.ds(off, rows), :]
            dob = do_ref[pl.ds(off, rows), :]
            st = lax.dot_general(kb, qb, NT, preferred_element_type=F32)
            if diagonal:
                row = lax.broadcasted_iota(jnp.int32, (tq, tq), 0)
                col = lax.broadcasted_iota(jnp.int32, (tq, tq), 1)
                st = jnp.where(row > col, NEG_BIG, st)
            pt = jnp.exp(st)
            dst = (pt * lax.dot_general(vb, dob, NT, preferred_element_type=F32)).astype(BF16)
            dv_ref[...] += jnp.dot(pt.astype(BF16), dob, preferred_element_type=F32)
            dk_ref[...] += jnp.dot(dst, qb, preferred_element_type=F32)
            dq_ref[pl.ds(off, rows), :] += lax.dot_general(dst, kb, TN, preferred_element_type=F32)

        step(kj, True)
        odd = (nq - 1 - kj) % 2

        @pl.when(odd == 1)
        def _():
            step(kj + 1, False)

        def loop(r, carry):
            step(kj + 1 + odd + 2 * r, False, blocks=2)
            return carry

        lax.fori_loop(0, (nq - 1 - kj) // 2, loop, 0)
        if n:
            pl.when(jnp.logical_and(pl.program_id(0) == nh - 1, kj == nq - 1))(finish)

    full = pl.BlockSpec((None, t, da), lambda h, j: (h, 0, 0))
    blk = pl.BlockSpec((None, tq, da), lambda h, j: (h, j, 0))
    return pl.pallas_call(
        body, name=name, grid=(nh, nq),
        in_specs=[full, full, blk, blk] + [ANY] * n, out_specs=[full, blk, blk] + [ANY] * n,
        out_shape=[jax.ShapeDtypeStruct((nh, t, da), F32)] * 3 + _chip_exchange_shapes(exchange),
        scratch_shapes=_chip_exchange_sems(n) if n else [],
        compiler_params=_params("arbitrary", "arbitrary"),
    )(qa, doa, ka, va, *exchange)


def _head_group(dh, h):
    g = h // (LANES // dh)
    return slice(g * LANES, (g + 1) * LANES)


def _head_select(dh, h, to_heads):
    r = lax.broadcasted_iota(jnp.int32, (LANES, LANES), 0)
    c = lax.broadcasted_iota(jnp.int32, (LANES, LANES), 1)
    nat, col = (r, c) if to_heads else (c, r)
    return jnp.logical_and(nat == col + (h % (LANES // dh)) * dh, col < dh).astype(BF16)


def _column(x, lane, j):
    return jnp.sum(jnp.where(lane == j, x, 0.0), axis=1, keepdims=True)


def _bias_columns(lane, first, value):
    out = jnp.zeros(lane.shape, F32)
    for j, term in enumerate(_split3(value)):
        out = out + jnp.where(lane == first + j, -term.astype(F32), 0.0)
    return out


def _attn_pack(proj, cgate, w, nh, scale, name):
    t = proj.shape[0]
    dh = w // nh
    tb = _pick(t, ROW_TILES)

    def body(q_ref, k_ref, v_ref, c_ref, qa_ref, ka_ref, va_ref):
        lane = lax.broadcasted_iota(jnp.int32, (tb, LANES), 1)
        ones_qv = jnp.where(jnp.logical_and(lane >= dh, lane < dh + 3), 1.0, 0.0)
        ones_k = jnp.where(jnp.logical_and(lane >= dh + 3, lane < dh + 7), 1.0, 0.0)
        qb = (q_ref[...] * scale).astype(BF16)
        kb = k_ref[...].astype(BF16)
        vb = v_ref[...].astype(BF16)
        cblk = c_ref[...]
        for h in range(nh):
            sel, grp = _head_select(dh, h, True), _head_group(dh, h)
            qa_ref[h] = (jnp.dot(qb[:, grp], sel, preferred_element_type=F32) + ones_qv).astype(BF16)
            va_ref[h] = (jnp.dot(vb[:, grp], sel, preferred_element_type=F32) + ones_qv).astype(BF16)
            bias = _bias_columns(lane, dh, _column(cblk, lane, h))
            ka_ref[h] = (jnp.dot(kb[:, grp], sel, preferred_element_type=F32) + bias + ones_k).astype(BF16)

    col = lambda j: pl.BlockSpec((tb, w), lambda i: (i, j))
    out = pl.BlockSpec((nh, tb, LANES), lambda i: (0, i, 0))
    return pl.pallas_call(
        body, name=name, grid=(t // tb,),
        in_specs=[col(0), col(1), col(2), pl.BlockSpec((tb, LANES), lambda i: (i, 0))],
        out_specs=[out, out, out],
        out_shape=[jax.ShapeDtypeStruct((nh, t, LANES), BF16)] * 3,
        compiler_params=_params("parallel"),
    )(proj, proj, proj, cgate)


def _attn_pack_bwd(dmix, oa, qa, w, nh, name):
    t = dmix.shape[0]
    dh = w // nh
    tb = _pick(t, ROW_TILES)

    def body(d_ref, oa_ref, qa_ref, doa_ref, qa2_ref):
        lane = lax.broadcasted_iota(jnp.int32, (tb, LANES), 1)
        db = d_ref[...].astype(BF16)
        for h in range(nh):
            do_h = jnp.dot(db[:, _head_group(dh, h)], _head_select(dh, h, True), preferred_element_type=F32)
            o_h = oa_ref[h]
            delta = jnp.sum(jnp.where(lane < dh, do_h * o_h, 0.0), axis=1, keepdims=True)
            doa_ref[h] = (do_h + _bias_columns(lane, dh, delta)).astype(BF16)
            qa2_ref[h] = (qa_ref[h].astype(F32) + _bias_columns(lane, dh + 4, _column(o_h, lane, dh))).astype(BF16)

    blk = pl.BlockSpec((nh, tb, LANES), lambda i: (0, i, 0))
    return pl.pallas_call(
        body, name=name, grid=(t // tb,),
        in_specs=[pl.BlockSpec((tb, w), lambda i: (i, 0)), blk, blk], out_specs=[blk, blk],
        out_shape=[jax.ShapeDtypeStruct((nh, t, LANES), BF16)] * 2,
        compiler_params=_params("parallel"),
    )(dmix, oa, qa)


def _attn_unpack(xa, w, nh, mult, sum_col, sum_sign, name):
    t = xa.shape[1]
    dh = w // nh
    tb = _pick(t, ROW_TILES)

    def body(x_ref, o_ref, *rest):
        lane = lax.broadcasted_iota(jnp.int32, (tb, LANES), 1)
        per = LANES // dh
        cols = jnp.zeros((tb, LANES), F32)
        for h0 in range(0, nh, per):
            acc = jnp.zeros((tb, LANES), F32)
            for h in range(h0, h0 + per):
                xh = x_ref[h]
                acc = acc + jnp.dot((xh * mult).astype(BF16), _head_select(dh, h, False), preferred_element_type=F32)
                if sum_col is not None:
                    cols = cols + jnp.where(lane == h, sum_sign * _column(xh, lane, sum_col), 0.0)
            o_ref[:, _head_group(dh, h0)] = acc.astype(BF16)
        if sum_col is not None:
            rest[0][...] = cols

    nat = pl.BlockSpec((tb, w), lambda i: (i, 0))
    lanes = pl.BlockSpec((tb, LANES), lambda i: (i, 0))
    return pl.pallas_call(
        body, name=name, grid=(t // tb,),
        in_specs=[pl.BlockSpec((nh, tb, LANES), lambda i: (0, i, 0))],
        out_specs=[nat, lanes] if sum_col is not None else [nat],
        out_shape=[jax.ShapeDtypeStruct((t, w), BF16)] + ([jax.ShapeDtypeStruct((t, LANES), F32)]
                                                            if sum_col is not None else []),
        compiler_params=_params("parallel"),
    )(xa)


def _conv_fwd(proj, cw, w, bcol, name):
    t = proj.shape[0]
    tb = _pick(t, ROW_TILES)
    hb = tb // SUBLANES

    def body(b_ref, c_ref, h_ref, cp_ref, hp_ref, w_ref, y_ref):
        i = pl.program_id(0)
        zp = jnp.where(i > 0, cp_ref[...] * hp_ref[...], 0.0)
        zext = jnp.concatenate([zp, c_ref[...] * h_ref[...]], axis=0)
        z1 = pltpu.roll(zext, 1, 0)[SUBLANES:]
        z2 = pltpu.roll(zext, 2, 0)[SUBLANES:]
        y = w_ref[2:3, :] * zext[SUBLANES:] + w_ref[1:2, :] * z1 + w_ref[0:1, :] * z2
        y_ref[...] = (b_ref[...] * y).astype(BF16)

    cur = lambda j: pl.BlockSpec((tb, w), lambda i: (i, bcol + j))
    prev = lambda j: pl.BlockSpec((SUBLANES, w), lambda i: (jnp.maximum(i * hb - 1, 0), bcol + j))
    return pl.pallas_call(
        body, name=name, grid=(t // tb,),
        in_specs=[cur(0), cur(1), cur(2), prev(1), prev(2), pl.BlockSpec(cw.shape, lambda i: (0, 0))],
        out_specs=pl.BlockSpec((tb, w), lambda i: (i, 0)),
        out_shape=jax.ShapeDtypeStruct((t, w), BF16), compiler_params=_params("parallel"),
    )(proj, proj, proj, proj, proj, cw)


def _conv_bwd(proj, cw, dmix, w, bcol, name):
    t = proj.shape[0]
    tb = _pick(t, ROW_TILES)
    hb = tb // SUBLANES
    nb = t // tb
    n_ext = tb + SUBLANES

    def body(b_ref, c_ref, h_ref, cp_ref, hp_ref, bn_ref, d_ref, dn_ref, w_ref, db_ref, dc_ref, dh_ref, dw_ref):
        i = pl.program_id(0)
        c = c_ref[...]
        hh = h_ref[...]
        zp = jnp.where(i > 0, cp_ref[...] * hp_ref[...], 0.0)
        zext = jnp.concatenate([zp, c * hh], axis=0)
        z0 = zext[SUBLANES:]
        z1 = pltpu.roll(zext, 1, 0)[SUBLANES:]
        z2 = pltpu.roll(zext, 2, 0)[SUBLANES:]
        y = w_ref[2:3, :] * z0 + w_ref[1:2, :] * z1 + w_ref[0:1, :] * z2
        d = d_ref[...]
        db_ref[...] = d * y
        dy = d * b_ref[...]
        dyn = jnp.where(i < nb - 1, dn_ref[...] * bn_ref[...], 0.0)
        dext = jnp.concatenate([dy, dyn], axis=0)
        dy1 = pltpu.roll(dext, n_ext - 1, 0)[:tb]
        dy2 = pltpu.roll(dext, n_ext - 2, 0)[:tb]
        dz = w_ref[2:3, :] * dy + w_ref[1:2, :] * dy1 + w_ref[0:1, :] * dy2
        dc_ref[...] = dz * hh
        dh_ref[...] = dz * c

        @pl.when(i == 0)
        def _():
            dw_ref[...] = jnp.zeros_like(dw_ref)

        dw_ref[0:1, :] += jnp.sum(dy * z2, axis=0, keepdims=True)
        dw_ref[1:2, :] += jnp.sum(dy * z1, axis=0, keepdims=True)
        dw_ref[2:3, :] += jnp.sum(dy * z0, axis=0, keepdims=True)

    cur = lambda j: pl.BlockSpec((tb, w), lambda i: (i, bcol + j))
    prev = lambda j: pl.BlockSpec((SUBLANES, w), lambda i: (jnp.maximum(i * hb - 1, 0), bcol + j))
    nxt = lambda col: pl.BlockSpec((SUBLANES, w), lambda i: (jnp.minimum((i + 1) * hb, nb * hb - 1), col))
    out = pl.BlockSpec((tb, w), lambda i: (i, 0))
    return pl.pallas_call(
        body, name=name, grid=(nb,),
        in_specs=[cur(0), cur(1), cur(2), prev(1), prev(2), nxt(bcol),
                  pl.BlockSpec((tb, w), lambda i: (i, 1)), nxt(1), pl.BlockSpec(cw.shape, lambda i: (0, 0))],
        out_specs=[out, out, out, pl.BlockSpec(cw.shape, lambda i: (0, 0))],
        out_shape=[jax.ShapeDtypeStruct((t, w), F32)] * 3 + [jax.ShapeDtypeStruct(cw.shape, F32)],
        compiler_params=_params("arbitrary"),
    )(proj, proj, proj, proj, proj, proj, dmix, dmix, cw)


SQRT_HALF = 0.7071067811865476
INV_SQRT_2PI = 0.3989422804014327


def _gelu(x):
    return 0.5 * x * (1.0 + lax.erf(x * SQRT_HALF))


def _gelu_grad(x):
    return 0.5 * (1.0 + lax.erf(x * SQRT_HALF)) + x * (INV_SQRT_2PI * jnp.exp(-0.5 * x * x))


def _sgu_fwd(uv, ln_g, ln_b, wm, bs_full, name):
    t, d2 = uv.shape
    d = d2 // 2
    ng, pb, _ = wm.shape
    gd = d // ng
    tb = _pick(t, ROW_TILES[1:] or ROW_TILES)
    assert tb % pb == 0

    def body(uv_ref, g_ref, b_ref, w_ref, bs_ref, o_ref):
        u = _gelu(uv_ref[:, :d])
        v = _gelu(uv_ref[:, d:])
        mu = jnp.mean(v, axis=-1, keepdims=True)
        vc = v - mu
        var = jnp.mean(vc * vc, axis=-1, keepdims=True)
        vn = (vc * lax.rsqrt(var + LN_EPS) * g_ref[...] + b_ref[...]).astype(BF16)
        for r in range(tb // pb):
            rows = slice(r * pb, (r + 1) * pb)
            for gi in range(ng):
                cols = slice(gi * gd, (gi + 1) * gd)
                s = jnp.dot(w_ref[gi], vn[rows, cols], preferred_element_type=F32) + bs_ref[:, cols]
                o_ref[rows, cols] = (u[rows, cols] * s).astype(BF16)

    vec = pl.BlockSpec((1, d), lambda i: (0, 0))
    return pl.pallas_call(
        body, name=name, grid=(t // tb,),
        in_specs=[pl.BlockSpec((tb, d2), lambda i: (i, 0)), vec, vec,
                  pl.BlockSpec(wm.shape, lambda i: (0, 0, 0)), pl.BlockSpec((pb, d), lambda i: (0, 0))],
        out_specs=pl.BlockSpec((tb, d), lambda i: (i, 0)),
        out_shape=jax.ShapeDtypeStruct((t, d), BF16), compiler_params=_params("parallel"),
    )(uv, ln_g.reshape(1, d), ln_b.reshape(1, d), wm, bs_full)


def _sgu_bwd(uv, ln_g, ln_b, wm, bs_full, dgated, name):
    t, d2 = uv.shape
    d = d2 // 2
    ng, pb, _ = wm.shape
    gd = d // ng
    tb = _pick(t, ROW_TILES[1:] or ROW_TILES)
    nb = t // tb

    def body(uv_ref, g_ref, b_ref, w_ref, bs_ref, dg_ref, o_ref, dw_ref, dbs_ref, dlg_ref, dlb_ref,
             du_s, dvn_s, dbs_s):
        i = pl.program_id(0)

        @pl.when(i == 0)
        def _():
            dw_ref[...] = jnp.zeros_like(dw_ref)
            dbs_s[...] = jnp.zeros_like(dbs_s)
            dlg_ref[...] = jnp.zeros_like(dlg_ref)
            dlb_ref[...] = jnp.zeros_like(dlb_ref)

        upre = uv_ref[:, :d]
        vpre = uv_ref[:, d:]
        u = _gelu(upre)
        v = _gelu(vpre)
        mu = jnp.mean(v, axis=-1, keepdims=True)
        vc = v - mu
        var = jnp.mean(vc * vc, axis=-1, keepdims=True)
        rstd = lax.rsqrt(var + LN_EPS)
        xhat = vc * rstd
        vn = (xhat * g_ref[...] + b_ref[...]).astype(BF16)
        dgt = dg_ref[...].astype(F32)
        for r in range(tb // pb):
            rows = slice(r * pb, (r + 1) * pb)
            for gi in range(ng):
                cols = slice(gi * gd, (gi + 1) * gd)
                vblk = vn[rows, cols]
                s = jnp.dot(w_ref[gi], vblk, preferred_element_type=F32) + bs_ref[:, cols]
                dblk = dgt[rows, cols]
                du_s[rows, cols] = dblk * s
                ds = dblk * u[rows, cols]
                dsb = ds.astype(BF16)
                dvn_s[rows, cols] = lax.dot_general(w_ref[gi], dsb, (((0,), (0,)), ((), ())),
                                                    preferred_element_type=F32)
                dw_ref[gi] += lax.dot_general(dsb, vblk, (((1,), (1,)), ((), ())), preferred_element_type=F32)
                dbs_s[:, cols] += ds
        dvn = dvn_s[...]
        dlg_ref[...] += jnp.sum(dvn * xhat, axis=0, keepdims=True)
        dlb_ref[...] += jnp.sum(dvn, axis=0, keepdims=True)
        dxh = dvn * g_ref[...]
        m1 = jnp.mean(dxh, axis=-1, keepdims=True)
        m2 = jnp.mean(dxh * xhat, axis=-1, keepdims=True)
        dv = rstd * (dxh - m1 - xhat * m2)
        o_ref[:, :d] = (du_s[...] * _gelu_grad(upre)).astype(BF16)
        o_ref[:, d:] = (dv * _gelu_grad(vpre)).astype(BF16)

        @pl.when(i == nb - 1)
        def _():
            lane = lax.broadcasted_iota(jnp.int32, (pb, LANES), 1)
            acc = jnp.zeros((pb, LANES), F32)
            for gi in range(ng):
                col = jnp.sum(dbs_s[:, gi * gd:(gi + 1) * gd], axis=1, keepdims=True)
                acc = acc + jnp.where(lane == gi, col, 0.0)
            dbs_ref[...] = acc

    vec = pl.BlockSpec((1, d), lambda i: (0, 0))
    duv, dw, dbs, dlg, dlb = pl.pallas_call(
        body, name=name, grid=(nb,),
        in_specs=[pl.BlockSpec((tb, d2), lambda i: (i, 0)), vec, vec,
                  pl.BlockSpec(wm.shape, lambda i: (0, 0, 0)), pl.BlockSpec((pb, d), lambda i: (0, 0)),
                  pl.BlockSpec((tb, d), lambda i: (i, 0))],
        out_specs=[pl.BlockSpec((tb, d2), lambda i: (i, 0)), pl.BlockSpec(wm.shape, lambda i: (0, 0, 0)),
                   pl.BlockSpec((pb, LANES), lambda i: (0, 0)), vec, vec],
        out_shape=[jax.ShapeDtypeStruct((t, d2), BF16), jax.ShapeDtypeStruct(wm.shape, F32),
                   jax.ShapeDtypeStruct((pb, LANES), F32), jax.ShapeDtypeStruct((1, d), F32),
                   jax.ShapeDtypeStruct((1, d), F32)],
        scratch_shapes=[pltpu.VMEM((tb, d), F32), pltpu.VMEM((tb, d), F32), pltpu.VMEM((pb, d), F32)],
        compiler_params=_params("arbitrary"),
    )(uv, ln_g.reshape(1, d), ln_b.reshape(1, d), wm, bs_full, dgated)
    return duv, dw, dbs, dlg[0], dlb[0]


def _adamw(w, g, m, v, name):
    shape = w.shape
    cols = shape[-1]
    rows = w.size // cols
    tr = _pick(rows, (512, 256, 352, 128, 64, 32, 16, 8))

    def body(w_ref, g_ref, m_ref, v_ref, d_ref, mo_ref, vo_ref):
        d_ref[...], mo_ref[...], vo_ref[...] = _adam_update(w_ref[...], g_ref[...], m_ref[...], v_ref[...])

    spec = pl.BlockSpec((tr, cols), lambda i: (i, 0))
    outs = pl.pallas_call(
        body, name=name, grid=(rows // tr,),
        in_specs=[spec] * 4, out_specs=[spec] * 3,
        out_shape=[jax.ShapeDtypeStruct((rows, cols), F32)] * 3,
        compiler_params=_params("parallel"),
    )(*[a.reshape(rows, cols) for a in (w, g, m, v)])
    return [o.reshape(shape) for o in outs]


ANY = pl.BlockSpec(memory_space=pl.ANY)


def _place():
    return lax.axis_index("x"), lax.axis_index("y"), lax.axis_index("c")


def _all_gather(shards, name):
    n = len(shards)

    def body(*refs):
        start, forward, finish = _gather_phases(refs[:n], refs[n:2 * n], *refs[2 * n:])
        start()
        forward()
        finish()

    return pl.pallas_call(
        body, name=name, in_specs=[ANY] * n, out_specs=[ANY] * n,
        out_shape=_gather_shapes(shards), scratch_shapes=_gather_sems(n),
    )(*shards)


def _gather_shapes(shards):
    return [jax.ShapeDtypeStruct((N_DEV,) + s.shape, s.dtype) for s in shards]


def _gather_sems(n):
    return [pltpu.SemaphoreType.DMA((7 * n,)), pltpu.SemaphoreType.DMA((7 * n,)), pltpu.SemaphoreType.DMA((n,))]


def _gather_phases(x_refs, out_refs, send_sems, recv_sems, local_sems):
    n = len(x_refs)
    x, y, c = _place()
    me, sibling = (x, y, c), (x, y, 1 - c)
    chips = [(1 - x, y), (x, 1 - y), (1 - x, 1 - y)]

    def copy(a, k, block, to, own=False):
        px, py, pc = block
        rows = out_refs[a].at[4 * px + 2 * py + pc]
        return pltpu.make_async_remote_copy(
            src_ref=x_refs[a] if own else rows, dst_ref=rows,
            send_sem=send_sems.at[7 * a + k], recv_sem=recv_sems.at[7 * a + k],
            device_id=to, device_id_type=MESH)

    def local(a):
        return pltpu.make_async_copy(x_refs[a], out_refs[a].at[4 * x + 2 * y + c], local_sems.at[a])

    def first(a):
        return [copy(a, 0, me, sibling, own=True)] + [copy(a, 1 + j, me, (*chip, c), own=True)
                                                      for j, chip in enumerate(chips)]

    def start():
        for a in range(n):
            local(a).start()
            for cp in first(a):
                cp.start()

    def forward():
        for j, chip in enumerate(chips):
            for a in range(n):
                copy(a, 1 + j, (*chip, c), me).wait_recv()
                copy(a, 4 + j, (*chip, c), sibling).start()

    def finish():
        for a in range(n):
            copy(a, 0, sibling, me).wait_recv()
            for j, chip in enumerate(chips):
                copy(a, 4 + j, (*chip, 1 - c), me).wait_recv()
        for a in range(n):
            for cp in first(a) + [copy(a, 4 + j, (*chip, c), sibling) for j, chip in enumerate(chips)]:
                cp.wait_send()
            local(a).wait()

    return start, forward, finish


def _rs_sibling_exchange(packed, name):
    rider = _sibling_exchange_rider(packed)
    n = len(packed)

    def body(*refs):
        start, finish = rider.phases(refs[:n], refs[n:2 * n], refs[2 * n:])
        start()
        finish()

    return pl.pallas_call(
        body, name=name, in_specs=rider.in_specs, out_specs=rider.out_specs, out_shape=rider.out_shapes,
        scratch_shapes=rider.scratch,
    )(*packed)


def _sibling_exchange_rider(packed):
    n = len(packed)

    def phases(p_refs, r_refs, scratch):
        send_sems, recv_sems = scratch
        x, y, c = _place()

        def copies():
            return [pltpu.make_async_remote_copy(
                src_ref=p_refs[a].at[2 * j + (1 - c)], dst_ref=r_refs[a].at[j],
                send_sem=send_sems.at[4 * a + j], recv_sem=recv_sems.at[4 * a + j],
                device_id=(x, y, 1 - c), device_id_type=MESH) for a in range(n) for j in range(4)]

        def start():
            for cp in copies():
                cp.start()

        def finish():
            for cp in copies():
                cp.wait()

        return start, finish

    return _Rider(list(packed), [ANY] * n, [jax.ShapeDtypeStruct((4,) + p.shape[1:], p.dtype) for p in packed],
                  [ANY] * n, [pltpu.SemaphoreType.DMA((4 * n,)), pltpu.SemaphoreType.DMA((4 * n,))], phases)


def _rs_chip_sum(packed, from_sibling, c_idx, name):
    _, r, cc = packed.shape
    tr = _pick(r, (512, 256, 352, 128))

    def body(c_ref, a_ref, b_ref, o_ref):
        o_ref[...] = (a_ref[...].astype(F32) + b_ref[...].astype(F32)).astype(o_ref.dtype)

    return pl.pallas_call(
        body, name=name,
        grid_spec=pltpu.PrefetchScalarGridSpec(
            num_scalar_prefetch=1, grid=(4, r // tr),
            in_specs=[pl.BlockSpec((None, tr, cc), lambda j, i, c_ref: (2 * j + c_ref[0], i, 0)),
                      pl.BlockSpec((None, tr, cc), lambda j, i, c_ref: (j, i, 0))],
            out_specs=pl.BlockSpec((None, tr, cc), lambda j, i, c_ref: (j, i, 0))),
        out_shape=jax.ShapeDtypeStruct((4, r, cc), packed.dtype),
        compiler_params=_params("parallel", "parallel"),
    )(c_idx, packed, from_sibling)


def _chip_exchange_shapes(partial):
    return [jax.ShapeDtypeStruct((3,) + p.shape[1:], p.dtype) for p in partial]


def _chip_exchange_sems(n):
    return [pltpu.SemaphoreType.DMA((3 * n,)), pltpu.SemaphoreType.DMA((3 * n,))]


def _chip_exchange_phases(p_refs, r_refs, send_sems, recv_sems):
    x, y, c = _place()
    chips = [(1 - x, y), (x, 1 - y), (1 - x, 1 - y)]

    def copies():
        return [pltpu.make_async_remote_copy(
            src_ref=p_refs[a].at[2 * tx + ty], dst_ref=r_refs[a].at[k],
            send_sem=send_sems.at[3 * a + k], recv_sem=recv_sems.at[3 * a + k],
            device_id=(tx, ty, c), device_id_type=MESH)
            for a in range(len(p_refs)) for k, (tx, ty) in enumerate(chips)]

    def start():
        for cp in copies():
            cp.start()

    def finish():
        for cp in copies():
            cp.wait()

    return start, finish


def _adam_update(w, g, m, v):
    mn = ADAM_B1 * m + (1.0 - ADAM_B1) * g
    vn = ADAM_B2 * v + (1.0 - ADAM_B2) * (g * g)
    m_hat = mn / (1.0 - ADAM_B1 ** ADAM_STEP)
    v_hat = vn / (1.0 - ADAM_B2 ** ADAM_STEP)
    return -ADAM_LR * (m_hat / (jnp.sqrt(v_hat) + ADAM_EPS) + ADAM_WD * w), mn, vn


def _rs_final_adamw(partial, received, chip_idx, w, m, v, name):
    _, r, cc = partial.shape
    tr = _pick(r, (512, 256, 352, 128))

    def body(c_ref, a_ref, r_ref, w_ref, m_ref, v_ref, g_ref, d_ref, mo_ref, vo_ref):
        g = a_ref[...].astype(F32)
        for k in range(3):
            g = g + r_ref[k].astype(F32)
        g_ref[...] = g
        d_ref[...], mo_ref[...], vo_ref[...] = _adam_update(w_ref[...], g, m_ref[...], v_ref[...])

    row = pl.BlockSpec((tr, cc), lambda i, c_ref: (i, 0))
    return pl.pallas_call(
        body, name=name,
        grid_spec=pltpu.PrefetchScalarGridSpec(
            num_scalar_prefetch=1, grid=(r // tr,),
            in_specs=[pl.BlockSpec((None, tr, cc), lambda i, c_ref: (c_ref[0], i, 0)),
                      pl.BlockSpec((3, tr, cc), lambda i, c_ref: (0, i, 0)), row, row, row],
            out_specs=[row] * 4),
        out_shape=[jax.ShapeDtypeStruct((r, cc), F32)] * 4,
        compiler_params=_params("parallel"),
    )(chip_idx, partial, received, w.reshape(r, cc), m.reshape(r, cc), v.reshape(r, cc))


def _all_reduce_small(vals, name):
    rider = _all_reduce_rider(vals)

    def body(v_ref, o_ref, *scratch):
        start, finish = rider.phases([v_ref], [o_ref], scratch)
        start()
        finish()

    return pl.pallas_call(
        body, name=name, in_specs=rider.in_specs, out_specs=rider.out_specs[0], out_shape=rider.out_shapes[0],
        scratch_shapes=rider.scratch, compiler_params=pltpu.CompilerParams(vmem_limit_bytes=VMEM_LIMIT),
    )(vals)


def _all_reduce_rider(vals):
    r, cc = vals.shape

    def phases(ins, outs, scratch):
        (v_ref,), (o_ref,), (buf, send_sems, recv_sems) = ins, outs, scratch
        x, y, c = _place()
        me = 4 * x + 2 * y + c

        def copies():
            cps = []
            for k in range(1, N_DEV):
                kx, ky, kc = (k >> 2) & 1, (k >> 1) & 1, k & 1
                peer = (1 - x if kx else x, 1 - y if ky else y, 1 - c if kc else c)
                cps.append(pltpu.make_async_remote_copy(
                    src_ref=buf.at[0], dst_ref=buf.at[k], send_sem=send_sems.at[k - 1],
                    recv_sem=recv_sems.at[k - 1], device_id=peer, device_id_type=MESH))
            return cps

        def start():
            buf[0] = v_ref[...]
            for cp in copies():
                cp.start()

        def finish():
            for cp in copies():
                cp.wait()
            acc = buf[jnp.bitwise_xor(me, 0)]
            for dev in range(1, N_DEV):
                acc = acc + buf[jnp.bitwise_xor(me, dev)]
            o_ref[...] = acc

        return start, finish

    vm = pl.BlockSpec(memory_space=pltpu.VMEM)
    return _Rider([vals], [vm], [jax.ShapeDtypeStruct((r, cc), F32)], [vm],
                  [pltpu.VMEM((N_DEV, r, cc), F32), pltpu.SemaphoreType.DMA((7,)), pltpu.SemaphoreType.DMA((7,))],
                  phases)


def _chip_exchange_rider(partial):
    n = len(partial)
    return _Rider(list(partial), [ANY] * n, _chip_exchange_shapes(partial), [ANY] * n, _chip_exchange_sems(n),
                  lambda ins, outs, scratch: _chip_exchange_phases(ins, outs, *scratch))


def _lanes(flat):
    pad = (-flat.shape[0]) % (SUBLANES * LANES)
    return jnp.pad(flat, (0, pad)).reshape(-1, LANES)


def kernel(x, even_w_in, even_b_f, even_conv_w, even_w_out, odd_w_in, odd_v_ln_g, odd_v_ln_b, odd_w_s, odd_b_s, odd_w_out, mix_ln_g, mix_ln_b, ffn_w_in, ffn_w_out, ffn_ln_g, ffn_ln_b, loss_target, m_even_w_in, m_even_b_f, m_even_conv_w, m_even_w_out, m_odd_w_in, m_odd_v_ln_g, m_odd_v_ln_b, m_odd_w_s, m_odd_b_s, m_odd_w_out, m_mix_ln_g, m_mix_ln_b, m_ffn_w_in, m_ffn_w_out, m_ffn_ln_g, m_ffn_ln_b, v_even_w_in, v_even_b_f, v_even_conv_w, v_even_w_out, v_odd_w_in, v_odd_v_ln_g, v_odd_v_ln_b, v_odd_w_s, v_odd_b_s, v_odd_w_out, v_mix_ln_g, v_mix_ln_b, v_ffn_w_in, v_ffn_w_out, v_ffn_ln_g, v_ffn_ln_b):
    t, d = x.shape[1], x.shape[2]
    nh = even_b_f.shape[-1]
    w = even_conv_w.shape[-1] * N_DEV
    dh = w // nh
    scale = dh ** -0.5
    e_in = even_w_in.shape[-1] * N_DEV
    f2 = ffn_w_in.shape[-1] * N_DEV
    f = f2 // 2
    ng, pb = odd_w_s.shape[1], odd_w_s.shape[2]
    assert e_in == 6 * w + nh and nh <= SUBLANES and (6 * w) % LANES == 0 and d % N_DEV == 0
    mx, my, mc = _place()
    me = 4 * mx + 2 * my + mc

    big = [even_w_in[0], even_w_out[0], odd_w_in[0], odd_w_out[0],
           ffn_w_in[0], ffn_w_in[1], ffn_w_out[0], ffn_w_out[1]]
    g_in0, = _all_gather([big[0].astype(BF16)], "ag_even_w_in")
    w_in0 = g_in0.transpose(1, 0, 2).reshape(d, e_in)
    w_all0 = jnp.concatenate([w_in0[:, :3 * w], w_in0[:, 3 * w + nh:], w_in0[:, 3 * w:3 * w + nh],
                              jnp.zeros((d, LANES - nh), BF16)], axis=1)

    cs, vs = even_conv_w.shape[-1], odd_v_ln_g.shape[-1]
    small_mine = jnp.concatenate([
        lax.dynamic_update_slice(jnp.zeros((3, w), F32), even_conv_w[0], (0, me * cs)).reshape(-1),
        lax.dynamic_update_slice(jnp.zeros((d,), F32), odd_v_ln_g[0], (me * vs,)),
        lax.dynamic_update_slice(jnp.zeros((d,), F32), odd_v_ln_b[0], (me * vs,))])
    small_all = _all_reduce_small(_lanes(small_mine), "ag_small").reshape(-1)
    conv_w = small_all[:3 * w].reshape(3, w)
    vln_g = small_all[3 * w:3 * w + d]
    vln_b = small_all[3 * w + d:3 * w + 2 * d]

    bf_pad = jnp.pad(even_b_f[0], (0, LANES - nh)).reshape(1, LANES)
    chunk = jnp.arange(pb) // (pb // 2)
    ws_mask = (chunk[None, :] <= chunk[:, None])[None]
    wm = jnp.where(ws_mask, odd_w_s[0], 0.0).astype(BF16)
    bs_full = jnp.repeat(odd_b_s[0].T, d // ng, axis=1)

    x0 = x[0]
    tgt = loss_target[0]
    fcol = 6 * w // LANES
    x0b = x0.astype(BF16)
    p0 = _mm(x0b, w_all0, "nn", F32, "l0_in_proj")
    cgate = _fgate_fwd(p0, bf_pad, fcol, nh, "l0_fgate")
    assert dh + 7 <= LANES
    qa, ka, va = _attn_pack(p0, cgate, w, nh, scale, "l0_attn_pack")
    oa, g_out0, g_in1, g_out1, g_fi0, g_fi1, g_fo0, g_fo1 = _attn_fwd(
        qa, ka, va, dh, "l0_attn", gather=[s.astype(BF16) for s in big[1:]])
    w_out0, w_out1 = g_out0.reshape(2 * w, d), g_out1.reshape(d, d)
    w_fo0, w_fo1 = g_fo0.reshape(f, d), g_fo1.reshape(f, d)
    nb = N_DEV // 2
    w_fi0, w_fi1 = g_fi0.reshape(2, nb, d, -1), g_fi1.reshape(2, nb, d, -1)
    attn, = _attn_unpack(oa, w, nh, 1.0, None, 1.0, "l0_attn_unpack")
    yconv = _conv_fwd(p0, conv_w, w, 3, "l0_conv")
    mix = jnp.concatenate([attn, yconv], axis=1)
    m0, x1, x1b = _mm_ln(mix, w_out0, x0, mix_ln_g[0], mix_ln_b[0], "l0_out_proj_ln")
    h0, gu0 = _ffn_in_swiglu(x1b, w_fi0, "l0_ffn_in")
    f0, x2, x2b = _mm_ln(h0, w_fo0, x1, ffn_ln_g[0], ffn_ln_b[0], "l0_ffn_out_ln")

    uv = _mm_cols_fwd(x2b, g_in1, False, F32, "l1_in_proj")
    gated = _sgu_fwd(uv, vln_g, vln_b, wm, bs_full, "l1_sgu")
    m1, x3, x3b = _mm_ln(gated, w_out1, x2, mix_ln_g[1], mix_ln_b[1], "l1_out_proj_ln")
    h1, gu1 = _ffn_in_swiglu(x3b, w_fi1, "l1_ffn_in")
    dz4, dz4b, g_ffn_g1, g_ffn_b1, loss_part = _mm_ln(h1, w_fo1, x3, ffn_ln_g[1], ffn_ln_b[1],
                                                      "l1_ffn_out_ln_loss", target=tgt)
    gd_fo1 = _mm_blk_dw(h1, dz4b, BF16, "l1_ffn_out_dw").reshape(N_DEV, -1, d)
    dgu1 = _ffn_out_dx_swiglu(dz4b, w_fo1, gu1, "l1_ffn_out_dx").reshape(N_DEV, t, -1)
    gd_fi1 = _mm_cols_dw(x3b, dgu1, N_DEV, True, BF16, "l1_ffn_in_dw")
    dz3, dz3b, g_mix_g1, g_mix_b1 = _mm_cols_dx_ln_bwd(dgu1, g_fi1, True, x2, m1, mix_ln_g[1], dz4,
                                                       "l1_ffn_in_dx_ln_bwd")
    gd_out1 = _mm(gated, dz3b, "tn", BF16, "l1_out_proj_dw").reshape(N_DEV, -1, d)
    dgated = _mm(dz3b, w_out1, "nt", BF16, "l1_out_proj_dx")
    duv, g_wm, g_bs_t, g_vln_g, g_vln_b = _sgu_bwd(uv, vln_g, vln_b, wm, bs_full, dgated, "l1_sgu_bwd")
    gd_in1 = _mm_cols_dw(x2b, duv, N_DEV, False, BF16, "l1_in_proj_dw")

    dz2, dz2b, g_ffn_g0, g_ffn_b0 = _mm_cols_dx_ln_bwd(duv, g_in1, False, x1, f0, ffn_ln_g[0], dz3,
                                                       "l1_in_proj_dx_ln_bwd")
    gd_fo0 = _mm_blk_dw(h0, dz2b, BF16, "l0_ffn_out_dw").reshape(N_DEV, -1, d)
    dgu0 = _ffn_out_dx_swiglu(dz2b, w_fo0, gu0, "l0_ffn_out_dx").reshape(N_DEV, t, -1)
    gd_fi0 = _mm_cols_dw(x1b, dgu0, N_DEV, True, BF16, "l0_ffn_in_dw")
    dz1, dz1b, g_mix_g0, g_mix_b0 = _mm_cols_dx_ln_bwd(dgu0, g_fi0, True, x0, m0, mix_ln_g[0], dz2,
                                                       "l0_ffn_in_dx_ln_bwd")
    gd_out0 = _mm(mix, dz1b, "tn", BF16, "l0_out_proj_dw").reshape(N_DEV, -1, d)
    early_g = [gd_out0, gd_in1, gd_out1, gd_fi0, gd_fi1, gd_fo0, gd_fo1]
    dmix, *early_sib = _mm(dz1b, w_out0, "nt", F32, "l0_out_proj_dx", rider=_sibling_exchange_rider(early_g))
    d_b, d_c, d_h, g_conv = _conv_bwd(p0, conv_w, dmix, w, 3, "l0_conv_bwd")
    doa, qa2 = _attn_pack_bwd(dmix, oa, qa, w, nh, "l0_attn_pack_bwd")
    big_names = ["even_w_in", "even_w_out", "odd_w_in", "odd_w_out", "ffn_w_in0", "ffn_w_in1", "ffn_w_out0", "ffn_w_out1"]
    c_idx = mc.reshape(1).astype(jnp.int32)
    chip_idx = (2 * mx + my).reshape(1).astype(jnp.int32)
    early_partial = [_rs_chip_sum(g, s, c_idx, "rs_chip_sum_" + n)
                     for g, s, n in zip(early_g, early_sib, big_names[1:])]
    dqa, dka, dva, *early_received = _attn_bwd(qa2, ka, va, doa, "l0_attn_bwd", exchange=early_partial)
    dq, dcq = _attn_unpack(dqa, w, nh, scale, dh + 3, 1.0, "l0_attn_unpack_dq")
    dk, dck = _attn_unpack(dka, w, nh, 1.0, dh, -1.0, "l0_attn_unpack_dk")
    dv, = _attn_unpack(dva, w, nh, 1.0, None, 1.0, "l0_attn_unpack_dv")
    dzf, g_bf = _fgate_bwd(p0, bf_pad, dcq, dck, fcol, nh, "l0_fgate_bwd")
    dp0 = jnp.concatenate([dq, dk, dv, d_b.astype(BF16), d_c.astype(BF16), d_h.astype(BF16), dzf.astype(BF16)], axis=1)
    g_ws = jnp.where(ws_mask, g_wm, 0.0)
    g_bs = g_bs_t[:, :ng].T
    small_g = [g_bf[:nh], g_conv, g_vln_g, g_vln_b, g_ws, g_bs,
               jnp.stack([g_mix_g0, g_mix_g1]), jnp.stack([g_mix_b0, g_mix_b1]),
               jnp.stack([g_ffn_g0, g_ffn_g1]), jnp.stack([g_ffn_b0, g_ffn_b1])]
    small_rider = _all_reduce_rider(_lanes(jnp.concatenate([a.reshape(-1) for a in small_g])))
    g_all0, small_sum = _mm(x0b, dp0, "tn", F32, "l0_in_proj_dw", rider=small_rider)
    gd_in0 = jnp.concatenate([g_all0[:, :3 * w], g_all0[:, 6 * w:6 * w + nh], g_all0[:, 3 * w:6 * w]], axis=1)
    gd_in0 = gd_in0.reshape(d, N_DEV, -1).transpose(1, 0, 2).astype(BF16)

    big_m = [m_even_w_in[0], m_even_w_out[0], m_odd_w_in[0], m_odd_w_out[0],
             m_ffn_w_in[0], m_ffn_w_in[1], m_ffn_w_out[0], m_ffn_w_out[1]]
    big_v = [v_even_w_in[0], v_even_w_out[0], v_odd_w_in[0], v_odd_w_out[0],
             v_ffn_w_in[0], v_ffn_w_in[1], v_ffn_w_out[0], v_ffn_w_out[1]]
    late_sib = _rs_sibling_exchange([gd_in0], "rs_sibling_late")
    late_partial = [_rs_chip_sum(gd_in0, late_sib[0], c_idx, "rs_chip_sum_" + big_names[0])]
    partial = late_partial + early_partial
    dx0, *late_received = _mm(dp0, w_all0, "nt", F32, "l0_in_proj_dx", rider=_chip_exchange_rider(late_partial))
    grad_x = _axpy(ALPHA, dz1, dx0, "grad_x")
    received = list(late_received) + list(early_received)
    upd = [_rs_final_adamw(p, r, chip_idx, wt, mt, vt, "rs_final_adamw_" + n)
           for p, r, wt, mt, vt, n in zip(partial, received, big, big_m, big_v, big_names)]
    big_out = {}
    for i, n in enumerate(["even_w_in", "even_w_out", "odd_w_in", "odd_w_out"]):
        big_out[n] = [o[None] for o in upd[i]]
    big_out["ffn_w_in"] = [jnp.stack([a, b]) for a, b in zip(upd[4], upd[5])]
    big_out["ffn_w_out"] = [jnp.stack([a, b]) for a, b in zip(upd[6], upd[7])]

    small_sum = small_sum.reshape(-1)
    outs_small = []
    off = 0
    for a in small_g:
        outs_small.append(small_sum[off:off + a.size].reshape(a.shape))
        off += a.size
    gr_bf, gr_conv, gr_vg, gr_vb, gr_ws, gr_bs, gr_mg, gr_mb, gr_fg, gr_fb = outs_small

    loss = lax.psum(loss_part, ("x", "y", "c"))

    grads = {
        "even_b_f": gr_bf[None],
        "even_conv_w": lax.dynamic_slice(gr_conv, (0, me * cs), (3, cs))[None],
        "odd_v_ln_g": lax.dynamic_slice(gr_vg, (me * vs,), (vs,))[None],
        "odd_v_ln_b": lax.dynamic_slice(gr_vb, (me * vs,), (vs,))[None],
        "odd_w_s": gr_ws[None], "odd_b_s": gr_bs[None],
        "mix_ln_g": gr_mg, "mix_ln_b": gr_mb, "ffn_ln_g": gr_fg, "ffn_ln_b": gr_fb,
    }
    weights = dict(even_w_in=even_w_in, even_b_f=even_b_f, even_conv_w=even_conv_w, even_w_out=even_w_out,
                   odd_w_in=odd_w_in, odd_v_ln_g=odd_v_ln_g, odd_v_ln_b=odd_v_ln_b, odd_w_s=odd_w_s,
                   odd_b_s=odd_b_s, odd_w_out=odd_w_out, mix_ln_g=mix_ln_g, mix_ln_b=mix_ln_b,
                   ffn_w_in=ffn_w_in, ffn_w_out=ffn_w_out, ffn_ln_g=ffn_ln_g, ffn_ln_b=ffn_ln_b)
    moms = dict(even_w_in=(m_even_w_in, v_even_w_in), even_b_f=(m_even_b_f, v_even_b_f),
                even_conv_w=(m_even_conv_w, v_even_conv_w), even_w_out=(m_even_w_out, v_even_w_out),
                odd_w_in=(m_odd_w_in, v_odd_w_in), odd_v_ln_g=(m_odd_v_ln_g, v_odd_v_ln_g),
                odd_v_ln_b=(m_odd_v_ln_b, v_odd_v_ln_b), odd_w_s=(m_odd_w_s, v_odd_w_s),
                odd_b_s=(m_odd_b_s, v_odd_b_s), odd_w_out=(m_odd_w_out, v_odd_w_out),
                mix_ln_g=(m_mix_ln_g, v_mix_ln_g), mix_ln_b=(m_mix_ln_b, v_mix_ln_b),
                ffn_w_in=(m_ffn_w_in, v_ffn_w_in), ffn_w_out=(m_ffn_w_out, v_ffn_w_out),
                ffn_ln_g=(m_ffn_ln_g, v_ffn_ln_g), ffn_ln_b=(m_ffn_ln_b, v_ffn_ln_b))
    names = list(weights)
    gout, deltas, new_m, new_v = [], [], [], []
    for n in names:
        if n in big_out:
            gr, dlt, mn, vn = big_out[n]
        else:
            gr = grads[n]
            dlt, mn, vn = _adamw(weights[n], gr, moms[n][0], moms[n][1], "adamw_" + n)
        gout.append(gr.reshape(weights[n].shape))
        deltas.append(dlt.reshape(weights[n].shape))
        new_m.append(mn.reshape(weights[n].shape))
        new_v.append(vn.reshape(weights[n].shape))
    return (loss, grad_x[None], *gout, *deltas, *new_m, *new_v)
```

```python
import functools
from typing import Callable, NamedTuple

import jax
import jax.numpy as jnp
from jax import lax
from jax.experimental import pallas as pl
from jax.experimental.pallas import tpu as pltpu

F32 = jnp.float32
BF16 = jnp.bfloat16
MESH = pl.DeviceIdType.MESH

DEPTH = 2
ALPHA = (2.0 * DEPTH) ** 0.25
LN_EPS = 1e-5
ADAM_LR = 0.001
ADAM_B1 = 0.9
ADAM_B2 = 0.999
ADAM_EPS = 1e-08
ADAM_WD = 0.01
ADAM_STEP = 10

N_DEV = 8
LANES = 128
SUBLANES = 8
VMEM_LIMIT = 48 * 1024 * 1024
NEG_BIG = -1e30
ROW_TILES = (512, 256, 128)


def _pick(n, cands):
    for c in cands:
        if c <= n and n % c == 0:
            return c
    return n


def _params(*sem):
    return pltpu.CompilerParams(dimension_semantics=sem, vmem_limit_bytes=VMEM_LIMIT)


NN = (((1,), (0,)), ((), ()))
NT = (((1,), (1,)), ((), ()))
TN = (((0,), (0,)), ((), ()))
M_TILES = (1024, 512, 1408, 256, 128)
N_TILES = (512, 640, 256, 128)
K_TILES = (2048, 1024, 512, 640, 1408, 256, 128)
K_WHOLE = 3328


class _Rider(NamedTuple):
    inputs: list
    in_specs: list
    out_shapes: list
    out_specs: list
    scratch: list
    phases: Callable


def _mm_core(name, grid, a, b, a_spec, b_spec, o_spec, o_shape, o_dtype, dims, tile, pieces=None, rider=None,
             addend=None):
    nred = grid[2]
    pieces = pieces or [(lambda r: r[...], lambda r: r[...])]
    ni = len(rider.inputs) if rider else 0
    no = len(rider.out_shapes) if rider else 0
    nacc = 0 if nred == 1 else 1
    add_arrays = [addend[0]] if addend else []

    def body(a_ref, b_ref, *rest):
        if addend:
            add_ref, rest = rest[0], rest[1:]
        finished = (lambda v: addend[1] * add_ref[...] + v) if addend else (lambda v: v)
        o_ref = rest[ni]
        if rider:
            start, finish = rider.phases(rest[:ni], rest[ni + 1:ni + 1 + no], rest[ni + 1 + no + nacc:])
            ids = [pl.program_id(ax) for ax in range(3)]
            first = functools.reduce(jnp.logical_and, [i == 0 for i in ids])
            last = functools.reduce(jnp.logical_and, [i == g - 1 for i, g in zip(ids, grid)])
            pl.when(first)(start)
        part = None
        for fa, fb in pieces:
            prod = lax.dot_general(fa(a_ref).astype(BF16), fb(b_ref).astype(BF16), dims, preferred_element_type=F32)
            part = prod if part is None else part + prod
        if nred == 1:
            o_ref[...] = finished(part).astype(o_ref.dtype)
        else:
            acc_ref = rest[ni + 1 + no]
            kk = pl.program_id(2)

            @pl.when(kk == 0)
            def _():
                acc_ref[...] = jnp.zeros_like(acc_ref)

            acc_ref[...] += part

            @pl.when(kk == nred - 1)
            def _():
                o_ref[...] = finished(acc_ref[...]).astype(o_ref.dtype)
        if rider:
            pl.when(last)(finish)

    out = pl.pallas_call(
        body, name=name, grid=grid,
        in_specs=[a_spec, b_spec] + ([o_spec] if addend else []) + (rider.in_specs if rider else []),
        out_specs=[o_spec] + (rider.out_specs if rider else []),
        out_shape=[jax.ShapeDtypeStruct(o_shape, o_dtype)] + (rider.out_shapes if rider else []),
        scratch_shapes=([] if nred == 1 else [pltpu.VMEM(tile, F32)]) + (rider.scratch if rider else []),
        compiler_params=_params(*(["arbitrary"] * 3 if rider else ["parallel", "parallel", "arbitrary"])),
    )(a, b, *add_arrays, *(rider.inputs if rider else []))
    return out if rider else out[0]


def _mm(a, b, mode, out_dtype, name, rider=None, addend=None):
    if mode == "nn":
        (m, k), (k2, n) = a.shape, b.shape
    elif mode == "nt":
        (m, k), (n, k2) = a.shape, b.shape
    else:
        (k, m), (k2, n) = a.shape, b.shape
    assert k == k2, (a.shape, b.shape, mode)
    tm, tn = _pick(m, M_TILES), _pick(n, N_TILES)
    tk = k if k <= K_WHOLE else _pick(k, K_TILES)
    if mode == "nn":
        a_spec = pl.BlockSpec((tm, tk), lambda i, j, kk: (i, kk))
        b_spec = pl.BlockSpec((tk, tn), lambda i, j, kk: (kk, j))
        dims = NN
    elif mode == "nt":
        a_spec = pl.BlockSpec((tm, tk), lambda i, j, kk: (i, kk))
        b_spec = pl.BlockSpec((tn, tk), lambda i, j, kk: (j, kk))
        dims = NT
    else:
        a_spec = pl.BlockSpec((tk, tm), lambda i, j, kk: (kk, i))
        b_spec = pl.BlockSpec((tk, tn), lambda i, j, kk: (kk, j))
        dims = TN
    return _mm_core(name, (m // tm, n // tn, k // tk), a, b, a_spec, b_spec,
                    pl.BlockSpec((tm, tn), lambda i, j, kk: (i, j)), (m, n), out_dtype, dims, (tm, tn), rider=rider,
                    addend=addend)


def _act_spec(blocked, rows, ns, row_ax, d_ax):
    if blocked:
        return pl.BlockSpec((None, rows, ns), lambda *g: (g[d_ax], g[row_ax], 0))
    return pl.BlockSpec((rows, ns), lambda *g: (g[row_ax], g[d_ax]))


def _mm_cols_fwd(a, g3, blocked, out_dtype, name):
    (t, k), (nd, k2, ns) = a.shape, g3.shape
    assert k == k2
    tm, tk = _pick(t, M_TILES), _pick(k, K_TILES)
    return _mm_core(name, (t // tm, nd, k // tk), a, g3,
                    pl.BlockSpec((tm, tk), lambda i, d, kk: (i, kk)),
                    pl.BlockSpec((None, tk, ns), lambda i, d, kk: (d, kk, 0)),
                    _act_spec(blocked, tm, ns, 0, 1), (nd, t, ns) if blocked else (t, nd * ns), out_dtype, NN, (tm, ns))


def _mm_cols_dx_ln_bwd(dy, g3, blocked, xa, xb, gam, dya, name):
    nd, k, ns = g3.shape
    t, d = xa.shape
    assert k == d
    tm = _pick(t, ROW_TILES)
    grp = nd if not blocked else (2 if nd % 2 == 0 else 1)
    nred = nd // grp

    def body(a_ref, b_ref, xa_ref, xb_ref, g_ref, dya_ref, dz_ref, dzb_ref, dg_ref, db_ref, *acc):
        i, kk = pl.program_id(0), pl.program_id(1)

        @pl.when(jnp.logical_and(i == 0, kk == 0))
        def _():
            dg_ref[...] = jnp.zeros_like(dg_ref)
            db_ref[...] = jnp.zeros_like(db_ref)

        if blocked:
            part = None
            for s in range(grp):
                prod = lax.dot_general(a_ref[s], b_ref[s], NT, preferred_element_type=F32)
                part = prod if part is None else part + prod
        else:
            whole_b = jnp.concatenate([b_ref[s] for s in range(nd)], axis=1)
            part = lax.dot_general(a_ref[...], whole_b, NT, preferred_element_type=F32)

        def ln_bwd(dyb):
            dy_t = ALPHA * dya_ref[...] + dyb
            z = ALPHA * xa_ref[...] + xb_ref[...]
            mu = jnp.mean(z, axis=-1, keepdims=True)
            zc = z - mu
            var = jnp.mean(zc * zc, axis=-1, keepdims=True)
            rstd = lax.rsqrt(var + LN_EPS)
            xhat = zc * rstd
            dxh = dy_t * g_ref[...]
            m1 = jnp.mean(dxh, axis=-1, keepdims=True)
            m2 = jnp.mean(dxh * xhat, axis=-1, keepdims=True)
            dz = rstd * (dxh - m1 - xhat * m2)
            dz_ref[...] = dz
            dzb_ref[...] = dz.astype(BF16)
            dg_ref[...] += jnp.sum(dy_t * xhat, axis=0, keepdims=True)
            db_ref[...] += jnp.sum(dy_t, axis=0, keepdims=True)

        if nred == 1:
            ln_bwd(part)
        else:
            acc_ref, = acc

            @pl.when(kk == 0)
            def _():
                acc_ref[...] = part

            @pl.when(kk > 0)
            def _():
                acc_ref[...] += part

            @pl.when(kk == nred - 1)
            def _():
                ln_bwd(acc_ref[...])

    row = pl.BlockSpec((tm, d), lambda i, kk: (i, 0))
    vec = pl.BlockSpec((1, d), lambda i, kk: (0, 0))
    if blocked:
        a_spec = pl.BlockSpec((grp, tm, ns), lambda i, kk: (kk, i, 0))
        b_spec = pl.BlockSpec((grp, k, ns), lambda i, kk: (kk, 0, 0))
    else:
        a_spec = pl.BlockSpec((tm, nd * ns), lambda i, kk: (i, 0))
        b_spec = pl.BlockSpec((nd, k, ns), lambda i, kk: (0, 0, 0))
    dz, dzb, dg, db = pl.pallas_call(
        body, name=name, grid=(t // tm, nred),
        in_specs=[a_spec, b_spec, row, row, vec, row], out_specs=[row, row, vec, vec],
        out_shape=[jax.ShapeDtypeStruct((t, d), F32), jax.ShapeDtypeStruct((t, d), BF16),
                   jax.ShapeDtypeStruct((1, d), F32), jax.ShapeDtypeStruct((1, d), F32)],
        scratch_shapes=[] if nred == 1 else [pltpu.VMEM((tm, d), F32)],
        compiler_params=_params("arbitrary", "arbitrary"),
    )(dy, g3, xa, xb, gam.reshape(1, d), dya)
    return dz, dzb, dg[0], db[0]


def _mm_cols_dw(a, dy, nd, blocked, out_dtype, name):
    t, k = a.shape
    ns = dy.shape[2] if blocked else dy.shape[1] // nd
    tmk, tk = _pick(k, M_TILES), _pick(t, (4096,) + K_TILES)
    return _mm_core(name, (nd, k // tmk, t // tk), a, dy,
                    pl.BlockSpec((tk, tmk), lambda d, j, kk: (kk, j)),
                    _act_spec(blocked, tk, ns, 2, 0),
                    pl.BlockSpec((None, tmk, ns), lambda d, j, kk: (d, j, 0)), (nd, k, ns), out_dtype, TN, (tmk, ns))


def _mm_blk_dw(h3, dz, out_dtype, name):
    (nb, t, ns), (_, n) = h3.shape, dz.shape
    tn, tk = _pick(n, (1024,) + N_TILES), _pick(t, K_TILES)
    return _mm_core(name, (nb, n // tn, t // tk), h3, dz,
                    pl.BlockSpec((None, tk, ns), lambda d, j, kk: (d, kk, 0)),
                    pl.BlockSpec((tk, tn), lambda d, j, kk: (kk, j)),
                    pl.BlockSpec((ns, tn), lambda d, j, kk: (d, j)), (nb * ns, n), out_dtype, TN, (ns, tn))


def _mm_ln(a, w, xa, g, b, name, target=None):
    blocked = a.ndim == 3
    t, d = xa.shape
    k = w.shape[0]
    tm = _pick(t, ROW_TILES)
    nb = a.shape[0] if blocked else 1
    ns = k // nb
    halves = [slice(0, tm // 2), slice(tm // 2, tm)] if tm % 32 == 0 else [slice(0, tm)]

    def body(a_ref, w_ref, xa_ref, g_ref, b_ref, *rest):
        def product(rows):
            if not blocked:
                return jnp.dot(a_ref[rows, :], w_ref[...], preferred_element_type=F32)
            acc = None
            for s in range(nb):
                prod = jnp.dot(a_ref[s, rows, :], w_ref[s * ns:(s + 1) * ns, :], preferred_element_type=F32)
                acc = prod if acc is None else acc + prod
            return acc

        if target is not None:
            t_ref, dz_ref, dzb_ref, dg_ref, db_ref, l_ref = rest

            @pl.when(pl.program_id(0) == 0)
            def _():
                l_ref[...] = jnp.zeros_like(l_ref)
                dg_ref[...] = jnp.zeros_like(dg_ref)
                db_ref[...] = jnp.zeros_like(db_ref)
        else:
            xb_ref, y_ref, yb_ref = rest
        for rows, xb in zip(halves, [product(rows) for rows in halves]):
            z = ALPHA * xa_ref[rows, :] + xb
            mu = jnp.mean(z, axis=-1, keepdims=True)
            zc = z - mu
            var = jnp.mean(zc * zc, axis=-1, keepdims=True)
            rstd = lax.rsqrt(var + LN_EPS)
            xhat = zc * rstd
            y = xhat * g_ref[...] + b_ref[...]
            if target is not None:
                e = y - t_ref[rows, :]
                l_ref[...] += 0.5 * jnp.sum(jnp.mean(e * e, axis=-1, keepdims=True))
                dy = e * (1.0 / d)
                dxh = dy * g_ref[...]
                m1 = jnp.mean(dxh, axis=-1, keepdims=True)
                m2 = jnp.mean(dxh * xhat, axis=-1, keepdims=True)
                dz = rstd * (dxh - m1 - xhat * m2)
                dz_ref[rows, :] = dz
                dzb_ref[rows, :] = dz.astype(BF16)
                dg_ref[...] += jnp.sum(dy * xhat, axis=0, keepdims=True)
                db_ref[...] += jnp.sum(dy, axis=0, keepdims=True)
            else:
                xb_ref[rows, :] = xb
                y_ref[rows, :] = y
                yb_ref[rows, :] = y.astype(BF16)

    row = pl.BlockSpec((tm, d), lambda i: (i, 0))
    vec = pl.BlockSpec((1, d), lambda i: (0, 0))
    a_spec = pl.BlockSpec((nb, tm, ns), lambda i: (0, i, 0)) if blocked else pl.BlockSpec((tm, k), lambda i: (i, 0))
    ins = [a, w, xa, g.reshape(1, d), b.reshape(1, d)]
    in_specs = [a_spec, pl.BlockSpec((k, d), lambda i: (0, 0)), row, vec, vec]
    if target is not None:
        dz, dzb, dg, db, l = pl.pallas_call(
            body, name=name, grid=(t // tm,), in_specs=in_specs + [row],
            out_specs=[row, row, vec, vec, pl.BlockSpec((1, LANES), lambda i: (0, 0))],
            out_shape=[jax.ShapeDtypeStruct((t, d), F32), jax.ShapeDtypeStruct((t, d), BF16),
                       jax.ShapeDtypeStruct((1, d), F32), jax.ShapeDtypeStruct((1, d), F32),
                       jax.ShapeDtypeStruct((1, LANES), F32)],
            compiler_params=_params("arbitrary"),
        )(*ins, target)
        return dz, dzb, dg[0], db[0], l[0, 0]
    return pl.pallas_call(
        body, name=name, grid=(t // tm,), in_specs=in_specs, out_specs=[row, row, row],
        out_shape=[jax.ShapeDtypeStruct((t, d), F32)] * 2 + [jax.ShapeDtypeStruct((t, d), BF16)],
        compiler_params=_params("parallel"),
    )(*ins)


def _ffn_in_swiglu(xb, g4, name):
    (t, k), (_, nb, _, ns) = xb.shape, g4.shape
    tm = _pick(t, M_TILES)

    def body(x_ref, w_ref, h_ref, gu_ref):
        xv = x_ref[...]
        gate = jnp.dot(xv, w_ref[0], preferred_element_type=F32)
        up = jnp.dot(xv, w_ref[1], preferred_element_type=F32)
        h_ref[...] = (gate * jax.nn.sigmoid(gate) * up).astype(BF16)
        gu_ref[0] = gate.astype(BF16)
        gu_ref[1] = up.astype(BF16)

    return pl.pallas_call(
        body, name=name, grid=(t // tm, nb),
        in_specs=[pl.BlockSpec((tm, k), lambda i, d: (i, 0)),
                  pl.BlockSpec((2, None, k, ns), lambda i, d: (0, d, 0, 0))],
        out_specs=[pl.BlockSpec((None, tm, ns), lambda i, d: (d, i, 0)),
                   pl.BlockSpec((2, None, tm, ns), lambda i, d: (0, d, i, 0))],
        out_shape=[jax.ShapeDtypeStruct((nb, t, ns), BF16), jax.ShapeDtypeStruct((2, nb, t, ns), BF16)],
        compiler_params=_params("parallel", "parallel"),
    )(xb, g4)


def _ffn_out_dx_swiglu(dz, w_out, gu4, name):
    (t, d), (_, nb, _, ns) = dz.shape, gu4.shape
    tm = _pick(t, M_TILES)

    def body(dz_ref, w_ref, gu_ref, o_ref):
        halves = [slice(0, tm // 2), slice(tm // 2, tm)] if tm % 16 == 0 else [slice(0, tm)]
        dhs = [lax.dot_general(dz_ref[rows, :].astype(BF16), w_ref[...], NT, preferred_element_type=F32)
               for rows in halves]
        for rows, dh in zip(halves, dhs):
            gate = gu_ref[0, rows, :].astype(F32)
            up = gu_ref[1, rows, :].astype(F32)
            sg = jax.nn.sigmoid(gate)
            silu = gate * sg
            o_ref[0, rows, :] = (dh * up * (sg + silu * (1.0 - sg))).astype(BF16)
            o_ref[1, rows, :] = (dh * silu).astype(BF16)

    blk = pl.BlockSpec((2, None, tm, ns), lambda i, j: (0, j, i, 0))
    return pl.pallas_call(
        body, name=name, grid=(t // tm, nb),
        in_specs=[pl.BlockSpec((tm, d), lambda i, j: (i, 0)), pl.BlockSpec((ns, d), lambda i, j: (j, 0)), blk],
        out_specs=blk,
        out_shape=jax.ShapeDtypeStruct((2, nb, t, ns), BF16),
        compiler_params=_params("parallel", "parallel"),
    )(dz, w_out, gu4)


def _tri_matmul(tri, x):
    x1 = x.astype(BF16)
    r1 = x - x1.astype(F32)
    x2 = r1.astype(BF16)
    x3 = (r1 - x2.astype(F32)).astype(BF16)
    dot = lambda v: jnp.dot(tri, v, preferred_element_type=F32)
    return dot(x1) + dot(x2) + dot(x3)


def _fgate_fwd(proj, bf_pad, fcol, n_heads, name):
    t = proj.shape[0]
    tb = _pick(t, ROW_TILES)

    def body(p_ref, b_ref, c_ref, carry):
        @pl.when(pl.program_id(0) == 0)
        def _():
            carry[...] = jnp.zeros_like(carry)

        z = p_ref[...] + b_ref[...]
        lf = jnp.minimum(z, 0.0) - jnp.log1p(jnp.exp(-jnp.abs(z)))
        lane = lax.broadcasted_iota(jnp.int32, (tb, LANES), 1)
        lf = jnp.where(lane < n_heads, lf, 0.0)
        r = lax.broadcasted_iota(jnp.int32, (tb, tb), 0)
        s = lax.broadcasted_iota(jnp.int32, (tb, tb), 1)
        tri = (s <= r).astype(BF16)
        c = _tri_matmul(tri, lf) + carry[...]
        c_ref[...] = c
        carry[...] = c[tb - 1:tb, :]

    return pl.pallas_call(
        body, name=name, grid=(t // tb,),
        in_specs=[pl.BlockSpec((tb, LANES), lambda i: (i, fcol)), pl.BlockSpec((1, LANES), lambda i: (0, 0))],
        out_specs=pl.BlockSpec((tb, LANES), lambda i: (i, 0)),
        out_shape=jax.ShapeDtypeStruct((t, LANES), F32),
        scratch_shapes=[pltpu.VMEM((1, LANES), F32)],
        compiler_params=_params("arbitrary"),
    )(proj, bf_pad)


def _fgate_bwd(proj, bf_pad, dcq, dck, fcol, n_heads, name):
    t = proj.shape[0]
    tb = _pick(t, ROW_TILES)
    nb = t // tb

    def body(p_ref, b_ref, dcq_ref, dck_ref, dz_ref, db_ref, carry):
        @pl.when(pl.program_id(0) == 0)
        def _():
            carry[...] = jnp.zeros_like(carry)
            db_ref[...] = jnp.zeros_like(db_ref)

        r = lax.broadcasted_iota(jnp.int32, (tb, tb), 0)
        s = lax.broadcasted_iota(jnp.int32, (tb, tb), 1)
        tri = (s >= r).astype(BF16)
        dlf = _tri_matmul(tri, dcq_ref[...] + dck_ref[...]) + carry[...]
        carry[...] = dlf[0:1, :]
        z = p_ref[...] + b_ref[...]
        lane = lax.broadcasted_iota(jnp.int32, (tb, LANES), 1)
        dz = jnp.where(lane < n_heads, dlf * jax.nn.sigmoid(-z), 0.0)
        dz_ref[...] = dz
        db_ref[...] += jnp.sum(dz, axis=0, keepdims=True)

    dz, db = pl.pallas_call(
        body, name=name, grid=(nb,),
        in_specs=[pl.BlockSpec((tb, LANES), lambda i: (nb - 1 - i, fcol)),
                  pl.BlockSpec((1, LANES), lambda i: (0, 0)),
                  pl.BlockSpec((tb, LANES), lambda i: (nb - 1 - i, 0)),
                  pl.BlockSpec((tb, LANES), lambda i: (nb - 1 - i, 0))],
        out_specs=[pl.BlockSpec((tb, LANES), lambda i: (nb - 1 - i, 0)),
                   pl.BlockSpec((1, LANES), lambda i: (0, 0))],
        out_shape=[jax.ShapeDtypeStruct((t, LANES), F32), jax.ShapeDtypeStruct((1, LANES), F32)],
        scratch_shapes=[pltpu.VMEM((1, LANES), F32)],
        compiler_params=_params("arbitrary"),
    )(proj, bf_pad, dcq, dck)
    return dz, db[0]


def _split3(x):
    hi = x.astype(BF16)
    r = x - hi.astype(F32)
    mid = r.astype(BF16)
    return hi, mid, (r - mid.astype(F32)).astype(BF16)


def _attn_fwd(qa, ka, va, dh, name, gather=()):
    nh, t, da = qa.shape
    tq = _pick(t, ROW_TILES)
    hb = 2 if nh % 2 == 0 else 1
    heads = range(hb)
    n = len(gather)
    steps = (nh // hb, t // tq)

    def body(q_ref, k_ref, v_ref, *rest):
        x_refs, o_ref, g_refs = rest[:n], rest[n], rest[n + 1:2 * n + 1]
        m_s, acc_s, s_a, s_b = rest[2 * n + 1:2 * n + 5]
        qi = pl.program_id(1)
        if n:
            start, forward, finish = _gather_phases(x_refs, g_refs, *rest[2 * n + 5:])
            at = lambda hh, qq: jnp.logical_and(pl.program_id(0) == hh, qi == qq)
            pl.when(at(0, 0))(start)
            pl.when(at(steps[0] // 2, 0))(forward)
        m_s[...] = jnp.full(m_s.shape, NEG_BIG, F32)
        acc_s[...] = jnp.zeros_like(acc_s)

        def scores(s_ref, j):
            off = pl.multiple_of(j * tq, tq)
            for g in heads:
                s_ref[g] = lax.dot_general(q_ref[g], k_ref[g, pl.ds(off, tq), :], NT, preferred_element_type=F32)

        def absorb(s_ref, j, diagonal):
            off = pl.multiple_of(j * tq, tq)
            s = [s_ref[g] for g in heads]
            if diagonal:
                row = lax.broadcasted_iota(jnp.int32, (tq, tq), 0)
                col = lax.broadcasted_iota(jnp.int32, (tq, tq), 1)
                s = [jnp.where(col > row, NEG_BIG, sg) for sg in s]
            m_prev = [m_s[g] for g in heads]
            m_new = [jnp.maximum(m_prev[g], jnp.max(s[g], axis=1, keepdims=True)) for g in heads]
            p = [jnp.exp(s[g] - m_new[g]).astype(BF16) for g in heads]
            pv = [jnp.dot(p[g], v_ref[g, pl.ds(off, tq), :], preferred_element_type=F32) for g in heads]
            for g in heads:
                acc_s[g] = jnp.exp(m_prev[g] - m_new[g]) * acc_s[g] + pv[g]
                m_s[g] = m_new[g]

        def two_blocks(r, carry):
            scores(s_b, 2 * r + 1)
            absorb(s_a, 2 * r, False)
            scores(s_a, 2 * r + 2)
            absorb(s_b, 2 * r + 1, False)
            return carry

        scores(s_a, 0)
        rounds = qi // 2
        lax.fori_loop(0, rounds, two_blocks, 0)

        @pl.when(qi % 2 == 0)
        def _():
            absorb(s_a, qi, True)

        @pl.when(qi % 2 == 1)
        def _():
            scores(s_b, qi)
            absorb(s_a, qi - 1, False)
            absorb(s_b, qi, True)

        lane = lax.broadcasted_iota(jnp.int32, (tq, da), 1)
        for g in heads:
            acc = acc_s[g]
            l = jnp.sum(jnp.where(lane == dh, acc, 0.0), axis=1, keepdims=True)
            o_ref[g] = jnp.where(lane == dh, m_s[g] + jnp.log(l), acc / l)
        if n:
            pl.when(at(steps[0] - 1, steps[1] - 1))(finish)

    full = pl.BlockSpec((hb, t, da), lambda h, qi: (h, 0, 0))
    blk = pl.BlockSpec((hb, tq, da), lambda h, qi: (h, qi, 0))
    return pl.pallas_call(
        body, name=name, grid=steps,
        in_specs=[blk, full, full] + [ANY] * n, out_specs=[blk] + [ANY] * n,
        out_shape=[jax.ShapeDtypeStruct((nh, t, da), F32)] + _gather_shapes(gather),
        scratch_shapes=[pltpu.VMEM((hb, tq, 1), F32), pltpu.VMEM((hb, tq, da), F32),
                        pltpu.VMEM((hb, tq, tq), F32), pltpu.VMEM((hb, tq, tq), F32)] + (_gather_sems(n) if n else []),
        compiler_params=_params("arbitrary", "arbitrary"),
    )(qa, ka, va, *gather)


def _attn_bwd(qa, ka, va, doa, name, exchange=()):
    nh, t, da = qa.shape
    tq = _pick(t, ROW_TILES)
    nq = t // tq
    n = len(exchange)

    def body(q_ref, do_ref, k_ref, v_ref, *rest):
        p_refs, (dq_ref, dk_ref, dv_ref), r_refs = rest[:n], rest[n:n + 3], rest[n + 3:2 * n + 3]
        kj = pl.program_id(1)
        if n:
            start, finish = _chip_exchange_phases(p_refs, r_refs, *rest[2 * n + 3:])
            pl.when(jnp.logical_and(pl.program_id(0) == 0, kj == 0))(start)

        @pl.when(kj == 0)
        def _():
            dq_ref[...] = jnp.zeros_like(dq_ref)

        dk_ref[...] = jnp.zeros_like(dk_ref)
        dv_ref[...] = jnp.zeros_like(dv_ref)
        kb = k_ref[...]
        vb = v_ref[...]

        def step(i, diagonal, blocks=1):
            off = pl.multiple_of(i * tq, tq)
            rows = blocks * tq
            qb = q_ref[pl.ds(off, rows), :]
            dob = do_ref[pl.ds(off, rows), :]
            st = lax.dot_general(kb, qb, NT, preferred_element_type=F32)
            if diagonal:
                row = lax.broadcasted_iota(jnp.int32, (tq, tq), 0)
                col = lax.broadcasted_iota(jnp.int32, (tq, tq), 1)
                st = jnp.where(row > col, NEG_BIG, st)
            pt = jnp.exp(st)
            dst = (pt * lax.dot_general(vb, dob, NT, preferred_element_type=F32)).astype(BF16)
            dv_ref[...] += jnp.dot(pt.astype(BF16), dob, preferred_element_type=F32)
            dk_ref[...] += jnp.dot(dst, qb, preferred_element_type=F32)
            dq_ref[pl.ds(off, rows), :] += lax.dot_general(dst, kb, TN, preferred_element_type=F32)

        step(kj, True)
        odd = (nq - 1 - kj) % 2

        @pl.when(odd == 1)
        def _():
            step(kj + 1, False)

        def loop(r, carry):
            step(kj + 1 + odd + 2 * r, False, blocks=2)
            return carry

        lax.fori_loop(0, (nq - 1 - kj) // 2, loop, 0)
        if n:
            pl.when(jnp.logical_and(pl.program_id(0) == nh - 1, kj == nq - 1))(finish)

    full = pl.BlockSpec((None, t, da), lambda h, j: (h, 0, 0))
    blk = pl.BlockSpec((None, tq, da), lambda h, j: (h, j, 0))
    return pl.pallas_call(
        body, name=name, grid=(nh, nq),
        in_specs=[full, full, blk, blk] + [ANY] * n, out_specs=[full, blk, blk] + [ANY] * n,
        out_shape=[jax.ShapeDtypeStruct((nh, t, da), F32)] * 3 + _chip_exchange_shapes(exchange),
        scratch_shapes=_chip_exchange_sems(n) if n else [],
        compiler_params=_params("arbitrary", "arbitrary"),
    )(qa, doa, ka, va, *exchange)


def _head_group(dh, h):
    g = h // (LANES // dh)
    return slice(g * LANES, (g + 1) * LANES)


def _head_select(dh, h, to_heads):
    r = lax.broadcasted_iota(jnp.int32, (LANES, LANES), 0)
    c = lax.broadcasted_iota(jnp.int32, (LANES, LANES), 1)
    nat, col = (r, c) if to_heads else (c, r)
    return jnp.logical_and(nat == col + (h % (LANES // dh)) * dh, col < dh).astype(BF16)


def _column(x, lane, j):
    return jnp.sum(jnp.where(lane == j, x, 0.0), axis=1, keepdims=True)


def _bias_columns(lane, first, value):
    out = jnp.zeros(lane.shape, F32)
    for j, term in enumerate(_split3(value)):
        out = out + jnp.where(lane == first + j, -term.astype(F32), 0.0)
    return out


def _attn_pack(proj, cgate, w, nh, scale, name):
    t = proj.shape[0]
    dh = w // nh
    tb = _pick(t, ROW_TILES)

    def body(q_ref, k_ref, v_ref, c_ref, qa_ref, ka_ref, va_ref):
        lane = lax.broadcasted_iota(jnp.int32, (tb, LANES), 1)
        ones_qv = jnp.where(jnp.logical_and(lane >= dh, lane < dh + 3), 1.0, 0.0)
        ones_k = jnp.where(jnp.logical_and(lane >= dh + 3, lane < dh + 7), 1.0, 0.0)
        qb = (q_ref[...] * scale).astype(BF16)
        kb = k_ref[...].astype(BF16)
        vb = v_ref[...].astype(BF16)
        cblk = c_ref[...]
        for h in range(nh):
            sel, grp = _head_select(dh, h, True), _head_group(dh, h)
            qa_ref[h] = (jnp.dot(qb[:, grp], sel, preferred_element_type=F32) + ones_qv).astype(BF16)
            va_ref[h] = (jnp.dot(vb[:, grp], sel, preferred_element_type=F32) + ones_qv).astype(BF16)
            bias = _bias_columns(lane, dh, _column(cblk, lane, h))
            ka_ref[h] = (jnp.dot(kb[:, grp], sel, preferred_element_type=F32) + bias + ones_k).astype(BF16)

    col = lambda j: pl.BlockSpec((tb, w), lambda i: (i, j))
    out = pl.BlockSpec((nh, tb, LANES), lambda i: (0, i, 0))
    return pl.pallas_call(
        body, name=name, grid=(t // tb,),
        in_specs=[col(0), col(1), col(2), pl.BlockSpec((tb, LANES), lambda i: (i, 0))],
        out_specs=[out, out, out],
        out_shape=[jax.ShapeDtypeStruct((nh, t, LANES), BF16)] * 3,
        compiler_params=_params("parallel"),
    )(proj, proj, proj, cgate)


def _attn_pack_bwd(dmix, oa, qa, w, nh, name):
    t = dmix.shape[0]
    dh = w // nh
    tb = _pick(t, ROW_TILES)

    def body(d_ref, oa_ref, qa_ref, doa_ref, qa2_ref):
        lane = lax.broadcasted_iota(jnp.int32, (tb, LANES), 1)
        db = d_ref[...].astype(BF16)
        for h in range(nh):
            do_h = jnp.dot(db[:, _head_group(dh, h)], _head_select(dh, h, True), preferred_element_type=F32)
            o_h = oa_ref[h]
            delta = jnp.sum(jnp.where(lane < dh, do_h * o_h, 0.0), axis=1, keepdims=True)
            doa_ref[h] = (do_h + _bias_columns(lane, dh, delta)).astype(BF16)
            qa2_ref[h] = (qa_ref[h].astype(F32) + _bias_columns(lane, dh + 4, _column(o_h, lane, dh))).astype(BF16)

    blk = pl.BlockSpec((nh, tb, LANES), lambda i: (0, i, 0))
    return pl.pallas_call(
        body, name=name, grid=(t // tb,),
        in_specs=[pl.BlockSpec((tb, w), lambda i: (i, 0)), blk, blk], out_specs=[blk, blk],
        out_shape=[jax.ShapeDtypeStruct((nh, t, LANES), BF16)] * 2,
        compiler_params=_params("parallel"),
    )(dmix, oa, qa)


def _attn_unpack(xa, w, nh, mult, sum_col, sum_sign, name):
    t = xa.shape[1]
    dh = w // nh
    tb = _pick(t, ROW_TILES)

    def body(x_ref, o_ref, *rest):
        lane = lax.broadcasted_iota(jnp.int32, (tb, LANES), 1)
        per = LANES // dh
        cols = jnp.zeros((tb, LANES), F32)
        for h0 in range(0, nh, per):
            acc = jnp.zeros((tb, LANES), F32)
            for h in range(h0, h0 + per):
                xh = x_ref[h]
                acc = acc + jnp.dot((xh * mult).astype(BF16), _head_select(dh, h, False), preferred_element_type=F32)
                if sum_col is not None:
                    cols = cols + jnp.where(lane == h, sum_sign * _column(xh, lane, sum_col), 0.0)
            o_ref[:, _head_group(dh, h0)] = acc.astype(BF16)
        if sum_col is not None:
            rest[0][...] = cols

    nat = pl.BlockSpec((tb, w), lambda i: (i, 0))
    lanes = pl.BlockSpec((tb, LANES), lambda i: (i, 0))
    return pl.pallas_call(
        body, name=name, grid=(t // tb,),
        in_specs=[pl.BlockSpec((nh, tb, LANES), lambda i: (0, i, 0))],
        out_specs=[nat, lanes] if sum_col is not None else [nat],
        out_shape=[jax.ShapeDtypeStruct((t, w), BF16)] + ([jax.ShapeDtypeStruct((t, LANES), F32)]
                                                            if sum_col is not None else []),
        compiler_params=_params("parallel"),
    )(xa)


def _conv_fwd(proj, cw, w, bcol, name):
    t = proj.shape[0]
    tb = _pick(t, ROW_TILES)
    hb = tb // SUBLANES

    def body(b_ref, c_ref, h_ref, cp_ref, hp_ref, w_ref, y_ref):
        i = pl.program_id(0)
        zp = jnp.where(i > 0, cp_ref[...] * hp_ref[...], 0.0)
        zext = jnp.concatenate([zp, c_ref[...] * h_ref[...]], axis=0)
        z1 = pltpu.roll(zext, 1, 0)[SUBLANES:]
        z2 = pltpu.roll(zext, 2, 0)[SUBLANES:]
        y = w_ref[2:3, :] * zext[SUBLANES:] + w_ref[1:2, :] * z1 + w_ref[0:1, :] * z2
        y_ref[...] = (b_ref[...] * y).astype(BF16)

    cur = lambda j: pl.BlockSpec((tb, w), lambda i: (i, bcol + j))
    prev = lambda j: pl.BlockSpec((SUBLANES, w), lambda i: (jnp.maximum(i * hb - 1, 0), bcol + j))
    return pl.pallas_call(
        body, name=name, grid=(t // tb,),
        in_specs=[cur(0), cur(1), cur(2), prev(1), prev(2), pl.BlockSpec(cw.shape, lambda i: (0, 0))],
        out_specs=pl.BlockSpec((tb, w), lambda i: (i, 0)),
        out_shape=jax.ShapeDtypeStruct((t, w), BF16), compiler_params=_params("parallel"),
    )(proj, proj, proj, proj, proj, cw)


def _conv_bwd(proj, cw, dmix, w, bcol, name):
    t = proj.shape[0]
    tb = _pick(t, ROW_TILES)
    hb = tb // SUBLANES
    nb = t // tb
    n_ext = tb + SUBLANES

    def body(b_ref, c_ref, h_ref, cp_ref, hp_ref, bn_ref, d_ref, dn_ref, w_ref, db_ref, dc_ref, dh_ref, dw_ref):
        i = pl.program_id(0)
        c = c_ref[...]
        hh = h_ref[...]
        zp = jnp.where(i > 0, cp_ref[...] * hp_ref[...], 0.0)
        zext = jnp.concatenate([zp, c * hh], axis=0)
        z0 = zext[SUBLANES:]
        z1 = pltpu.roll(zext, 1, 0)[SUBLANES:]
        z2 = pltpu.roll(zext, 2, 0)[SUBLANES:]
        y = w_ref[2:3, :] * z0 + w_ref[1:2, :] * z1 + w_ref[0:1, :] * z2
        d = d_ref[...]
        db_ref[...] = d * y
        dy = d * b_ref[...]
        dyn = jnp.where(i < nb - 1, dn_ref[...] * bn_ref[...], 0.0)
        dext = jnp.concatenate([dy, dyn], axis=0)
        dy1 = pltpu.roll(dext, n_ext - 1, 0)[:tb]
        dy2 = pltpu.roll(dext, n_ext - 2, 0)[:tb]
        dz = w_ref[2:3, :] * dy + w_ref[1:2, :] * dy1 + w_ref[0:1, :] * dy2
        dc_ref[...] = dz * hh
        dh_ref[...] = dz * c

        @pl.when(i == 0)
        def _():
            dw_ref[...] = jnp.zeros_like(dw_ref)

        dw_ref[0:1, :] += jnp.sum(dy * z2, axis=0, keepdims=True)
        dw_ref[1:2, :] += jnp.sum(dy * z1, axis=0, keepdims=True)
        dw_ref[2:3, :] += jnp.sum(dy * z0, axis=0, keepdims=True)

    cur = lambda j: pl.BlockSpec((tb, w), lambda i: (i, bcol + j))
    prev = lambda j: pl.BlockSpec((SUBLANES, w), lambda i: (jnp.maximum(i * hb - 1, 0), bcol + j))
    nxt = lambda col: pl.BlockSpec((SUBLANES, w), lambda i: (jnp.minimum((i + 1) * hb, nb * hb - 1), col))
    out = pl.BlockSpec((tb, w), lambda i: (i, 0))
    return pl.pallas_call(
        body, name=name, grid=(nb,),
        in_specs=[cur(0), cur(1), cur(2), prev(1), prev(2), nxt(bcol),
                  pl.BlockSpec((tb, w), lambda i: (i, 1)), nxt(1), pl.BlockSpec(cw.shape, lambda i: (0, 0))],
        out_specs=[out, out, out, pl.BlockSpec(cw.shape, lambda i: (0, 0))],
        out_shape=[jax.ShapeDtypeStruct((t, w), F32)] * 3 + [jax.ShapeDtypeStruct(cw.shape, F32)],
        compiler_params=_params("arbitrary"),
    )(proj, proj, proj, proj, proj, proj, dmix, dmix, cw)


SQRT_HALF = 0.7071067811865476
INV_SQRT_2PI = 0.3989422804014327


def _gelu(x):
    return 0.5 * x * (1.0 + lax.erf(x * SQRT_HALF))


def _gelu_grad(x):
    return 0.5 * (1.0 + lax.erf(x * SQRT_HALF)) + x * (INV_SQRT_2PI * jnp.exp(-0.5 * x * x))


def _sgu_fwd(uv, ln_g, ln_b, wm, bs_full, name):
    t, d2 = uv.shape
    d = d2 // 2
    ng, pb, _ = wm.shape
    gd = d // ng
    tb = _pick(t, ROW_TILES[1:] or ROW_TILES)
    assert tb % pb == 0

    def body(uv_ref, g_ref, b_ref, w_ref, bs_ref, o_ref):
        u = _gelu(uv_ref[:, :d])
        v = _gelu(uv_ref[:, d:])
        mu = jnp.mean(v, axis=-1, keepdims=True)
        vc = v - mu
        var = jnp.mean(vc * vc, axis=-1, keepdims=True)
        vn = (vc * lax.rsqrt(var + LN_EPS) * g_ref[...] + b_ref[...]).astype(BF16)
        for r in range(tb // pb):
            rows = slice(r * pb, (r + 1) * pb)
            for gi in range(ng):
                cols = slice(gi * gd, (gi + 1) * gd)
                s = jnp.dot(w_ref[gi], vn[rows, cols], preferred_element_type=F32) + bs_ref[:, cols]
                o_ref[rows, cols] = (u[rows, cols] * s).astype(BF16)

    vec = pl.BlockSpec((1, d), lambda i: (0, 0))
    return pl.pallas_call(
        body, name=name, grid=(t // tb,),
        in_specs=[pl.BlockSpec((tb, d2), lambda i: (i, 0)), vec, vec,
                  pl.BlockSpec(wm.shape, lambda i: (0, 0, 0)), pl.BlockSpec((pb, d), lambda i: (0, 0))],
        out_specs=pl.BlockSpec((tb, d), lambda i: (i, 0)),
        out_shape=jax.ShapeDtypeStruct((t, d), BF16), compiler_params=_params("parallel"),
    )(uv, ln_g.reshape(1, d), ln_b.reshape(1, d), wm, bs_full)


def _sgu_bwd(uv, ln_g, ln_b, wm, bs_full, dgated, name):
    t, d2 = uv.shape
    d = d2 // 2
    ng, pb, _ = wm.shape
    gd = d // ng
    tb = _pick(t, ROW_TILES[1:] or ROW_TILES)
    nb = t // tb

    def body(uv_ref, g_ref, b_ref, w_ref, bs_ref, dg_ref, o_ref, dw_ref, dbs_ref, dlg_ref, dlb_ref,
             du_s, dvn_s, dbs_s):
        i = pl.program_id(0)

        @pl.when(i == 0)
        def _():
            dw_ref[...] = jnp.zeros_like(dw_ref)
            dbs_s[...] = jnp.zeros_like(dbs_s)
            dlg_ref[...] = jnp.zeros_like(dlg_ref)
            dlb_ref[...] = jnp.zeros_like(dlb_ref)

        upre = uv_ref[:, :d]
        vpre = uv_ref[:, d:]
        u = _gelu(upre)
        v = _gelu(vpre)
        mu = jnp.mean(v, axis=-1, keepdims=True)
        vc = v - mu
        var = jnp.mean(vc * vc, axis=-1, keepdims=True)
        rstd = lax.rsqrt(var + LN_EPS)
        xhat = vc * rstd
        vn = (xhat * g_ref[...] + b_ref[...]).astype(BF16)
        dgt = dg_ref[...].astype(F32)
        for r in range(tb // pb):
            rows = slice(r * pb, (r + 1) * pb)
            for gi in range(ng):
                cols = slice(gi * gd, (gi + 1) * gd)
                vblk = vn[rows, cols]
                s = jnp.dot(w_ref[gi], vblk, preferred_element_type=F32) + bs_ref[:, cols]
                dblk = dgt[rows, cols]
                du_s[rows, cols] = dblk * s
                ds = dblk * u[rows, cols]
                dsb = ds.astype(BF16)
                dvn_s[rows, cols] = lax.dot_general(w_ref[gi], dsb, (((0,), (0,)), ((), ())),
                                                    preferred_element_type=F32)
                dw_ref[gi] += lax.dot_general(dsb, vblk, (((1,), (1,)), ((), ())), preferred_element_type=F32)
                dbs_s[:, cols] += ds
        dvn = dvn_s[...]
        dlg_ref[...] += jnp.sum(dvn * xhat, axis=0, keepdims=True)
        dlb_ref[...] += jnp.sum(dvn, axis=0, keepdims=True)
        dxh = dvn * g_ref[...]
        m1 = jnp.mean(dxh, axis=-1, keepdims=True)
        m2 = jnp.mean(dxh * xhat, axis=-1, keepdims=True)
        dv = rstd * (dxh - m1 - xhat * m2)
        o_ref[:, :d] = (du_s[...] * _gelu_grad(upre)).astype(BF16)
        o_ref[:, d:] = (dv * _gelu_grad(vpre)).astype(BF16)

        @pl.when(i == nb - 1)
        def _():
            lane = lax.broadcasted_iota(jnp.int32, (pb, LANES), 1)
            acc = jnp.zeros((pb, LANES), F32)
            for gi in range(ng):
                col = jnp.sum(dbs_s[:, gi * gd:(gi + 1) * gd], axis=1, keepdims=True)
                acc = acc + jnp.where(lane == gi, col, 0.0)
            dbs_ref[...] = acc

    vec = pl.BlockSpec((1, d), lambda i: (0, 0))
    duv, dw, dbs, dlg, dlb = pl.pallas_call(
        body, name=name, grid=(nb,),
        in_specs=[pl.BlockSpec((tb, d2), lambda i: (i, 0)), vec, vec,
                  pl.BlockSpec(wm.shape, lambda i: (0, 0, 0)), pl.BlockSpec((pb, d), lambda i: (0, 0)),
                  pl.BlockSpec((tb, d), lambda i: (i, 0))],
        out_specs=[pl.BlockSpec((tb, d2), lambda i: (i, 0)), pl.BlockSpec(wm.shape, lambda i: (0, 0, 0)),
                   pl.BlockSpec((pb, LANES), lambda i: (0, 0)), vec, vec],
        out_shape=[jax.ShapeDtypeStruct((t, d2), BF16), jax.ShapeDtypeStruct(wm.shape, F32),
                   jax.ShapeDtypeStruct((pb, LANES), F32), jax.ShapeDtypeStruct((1, d), F32),
                   jax.ShapeDtypeStruct((1, d), F32)],
        scratch_shapes=[pltpu.VMEM((tb, d), F32), pltpu.VMEM((tb, d), F32), pltpu.VMEM((pb, d), F32)],
        compiler_params=_params("arbitrary"),
    )(uv, ln_g.reshape(1, d), ln_b.reshape(1, d), wm, bs_full, dgated)
    return duv, dw, dbs, dlg[0], dlb[0]


def _adamw(w, g, m, v, name):
    shape = w.shape
    cols = shape[-1]
    rows = w.size // cols
    tr = _pick(rows, (512, 256, 352, 128, 64, 32, 16, 8))

    def body(w_ref, g_ref, m_ref, v_ref, d_ref, mo_ref, vo_ref):
        d_ref[...], mo_ref[...], vo_ref[...] = _adam_update(w_ref[...], g_ref[...], m_ref[...], v_ref[...])

    spec = pl.BlockSpec((tr, cols), lambda i: (i, 0))
    outs = pl.pallas_call(
        body, name=name, grid=(rows // tr,),
        in_specs=[spec] * 4, out_specs=[spec] * 3,
        out_shape=[jax.ShapeDtypeStruct((rows, cols), F32)] * 3,
        compiler_params=_params("parallel"),
    )(*[a.reshape(rows, cols) for a in (w, g, m, v)])
    return [o.reshape(shape) for o in outs]


ANY = pl.BlockSpec(memory_space=pl.ANY)


def _place():
    return lax.axis_index("x"), lax.axis_index("y"), lax.axis_index("c")


def _all_gather(shards, name):
    n = len(shards)

    def body(*refs):
        start, forward, finish = _gather_phases(refs[:n], refs[n:2 * n], *refs[2 * n:])
        start()
        forward()
        finish()

    return pl.pallas_call(
        body, name=name, in_specs=[ANY] * n, out_specs=[ANY] * n,
        out_shape=_gather_shapes(shards), scratch_shapes=_gather_sems(n),
    )(*shards)


def _gather_shapes(shards):
    return [jax.ShapeDtypeStruct((N_DEV,) + s.shape, s.dtype) for s in shards]


def _gather_sems(n):
    return [pltpu.SemaphoreType.DMA((7 * n,)), pltpu.SemaphoreType.DMA((7 * n,)), pltpu.SemaphoreType.DMA((n,))]


def _gather_phases(x_refs, out_refs, send_sems, recv_sems, local_sems):
    n = len(x_refs)
    x, y, c = _place()
    me, sibling = (x, y, c), (x, y, 1 - c)
    chips = [(1 - x, y), (x, 1 - y), (1 - x, 1 - y)]

    def copy(a, k, block, to, own=False):
        px, py, pc = block
        rows = out_refs[a].at[4 * px + 2 * py + pc]
        return pltpu.make_async_remote_copy(
            src_ref=x_refs[a] if own else rows, dst_ref=rows,
            send_sem=send_sems.at[7 * a + k], recv_sem=recv_sems.at[7 * a + k],
            device_id=to, device_id_type=MESH)

    def local(a):
        return pltpu.make_async_copy(x_refs[a], out_refs[a].at[4 * x + 2 * y + c], local_sems.at[a])

    def first(a):
        return [copy(a, 0, me, sibling, own=True)] + [copy(a, 1 + j, me, (*chip, c), own=True)
                                                      for j, chip in enumerate(chips)]

    def start():
        for a in range(n):
            local(a).start()
            for cp in first(a):
                cp.start()

    def forward():
        for j, chip in enumerate(chips):
            for a in range(n):
                copy(a, 1 + j, (*chip, c), me).wait_recv()
                copy(a, 4 + j, (*chip, c), sibling).start()

    def finish():
        for a in range(n):
            copy(a, 0, sibling, me).wait_recv()
            for j, chip in enumerate(chips):
                copy(a, 4 + j, (*chip, 1 - c), me).wait_recv()
        for a in range(n):
            for cp in first(a) + [copy(a, 4 + j, (*chip, c), sibling) for j, chip in enumerate(chips)]:
                cp.wait_send()
            local(a).wait()

    return start, forward, finish


def _rs_sibling_exchange(packed, name):
    rider = _sibling_exchange_rider(packed)
    n = len(packed)

    def body(*refs):
        start, finish = rider.phases(refs[:n], refs[n:2 * n], refs[2 * n:])
        start()
        finish()

    return pl.pallas_call(
        body, name=name, in_specs=rider.in_specs, out_specs=rider.out_specs, out_shape=rider.out_shapes,
        scratch_shapes=rider.scratch,
    )(*packed)


def _sibling_exchange_rider(packed):
    n = len(packed)

    def phases(p_refs, r_refs, scratch):
        send_sems, recv_sems = scratch
        x, y, c = _place()

        def copies():
            return [pltpu.make_async_remote_copy(
                src_ref=p_refs[a].at[2 * j + (1 - c)], dst_ref=r_refs[a].at[j],
                send_sem=send_sems.at[4 * a + j], recv_sem=recv_sems.at[4 * a + j],
                device_id=(x, y, 1 - c), device_id_type=MESH) for a in range(n) for j in range(4)]

        def start():
            for cp in copies():
                cp.start()

        def finish():
            for cp in copies():
                cp.wait()

        return start, finish

    return _Rider(list(packed), [ANY] * n, [jax.ShapeDtypeStruct((4,) + p.shape[1:], p.dtype) for p in packed],
                  [ANY] * n, [pltpu.SemaphoreType.DMA((4 * n,)), pltpu.SemaphoreType.DMA((4 * n,))], phases)


def _rs_chip_sum(packed, from_sibling, c_idx, name):
    _, r, cc = packed.shape
    tr = _pick(r, (512, 256, 352, 128))

    def body(c_ref, a_ref, b_ref, o_ref):
        o_ref[...] = (a_ref[...].astype(F32) + b_ref[...].astype(F32)).astype(o_ref.dtype)

    return pl.pallas_call(
        body, name=name,
        grid_spec=pltpu.PrefetchScalarGridSpec(
            num_scalar_prefetch=1, grid=(4, r // tr),
            in_specs=[pl.BlockSpec((None, tr, cc), lambda j, i, c_ref: (2 * j + c_ref[0], i, 0)),
                      pl.BlockSpec((None, tr, cc), lambda j, i, c_ref: (j, i, 0))],
            out_specs=pl.BlockSpec((None, tr, cc), lambda j, i, c_ref: (j, i, 0))),
        out_shape=jax.ShapeDtypeStruct((4, r, cc), packed.dtype),
        compiler_params=_params("parallel", "parallel"),
    )(c_idx, packed, from_sibling)


def _chip_exchange_shapes(partial):
    return [jax.ShapeDtypeStruct((3,) + p.shape[1:], p.dtype) for p in partial]


def _chip_exchange_sems(n):
    return [pltpu.SemaphoreType.DMA((3 * n,)), pltpu.SemaphoreType.DMA((3 * n,))]


def _chip_exchange_phases(p_refs, r_refs, send_sems, recv_sems):
    x, y, c = _place()
    chips = [(1 - x, y), (x, 1 - y), (1 - x, 1 - y)]

    def copies():
        return [pltpu.make_async_remote_copy(
            src_ref=p_refs[a].at[2 * tx + ty], dst_ref=r_refs[a].at[k],
            send_sem=send_sems.at[3 * a + k], recv_sem=recv_sems.at[3 * a + k],
            device_id=(tx, ty, c), device_id_type=MESH)
            for a in range(len(p_refs)) for k, (tx, ty) in enumerate(chips)]

    def start():
        for cp in copies():
            cp.start()

    def finish():
        for cp in copies():
            cp.wait()

    return start, finish


def _adam_update(w, g, m, v):
    mn = ADAM_B1 * m + (1.0 - ADAM_B1) * g
    vn = ADAM_B2 * v + (1.0 - ADAM_B2) * (g * g)
    m_hat = mn / (1.0 - ADAM_B1 ** ADAM_STEP)
    v_hat = vn / (1.0 - ADAM_B2 ** ADAM_STEP)
    return -ADAM_LR * (m_hat / (jnp.sqrt(v_hat) + ADAM_EPS) + ADAM_WD * w), mn, vn


def _rs_final_adamw(partial, received, chip_idx, w, m, v, name):
    _, r, cc = partial.shape
    tr = _pick(r, (512, 256, 352, 128))

    def body(c_ref, a_ref, r_ref, w_ref, m_ref, v_ref, g_ref, d_ref, mo_ref, vo_ref):
        g = a_ref[...].astype(F32)
        for k in range(3):
            g = g + r_ref[k].astype(F32)
        g_ref[...] = g
        d_ref[...], mo_ref[...], vo_ref[...] = _adam_update(w_ref[...], g, m_ref[...], v_ref[...])

    row = pl.BlockSpec((tr, cc), lambda i, c_ref: (i, 0))
    return pl.pallas_call(
        body, name=name,
        grid_spec=pltpu.PrefetchScalarGridSpec(
            num_scalar_prefetch=1, grid=(r // tr,),
            in_specs=[pl.BlockSpec((None, tr, cc), lambda i, c_ref: (c_ref[0], i, 0)),
                      pl.BlockSpec((3, tr, cc), lambda i, c_ref: (0, i, 0)), row, row, row],
            out_specs=[row] * 4),
        out_shape=[jax.ShapeDtypeStruct((r, cc), F32)] * 4,
        compiler_params=_params("parallel"),
    )(chip_idx, partial, received, w.reshape(r, cc), m.reshape(r, cc), v.reshape(r, cc))


def _all_reduce_small(vals, name):
    rider = _all_reduce_rider(vals)

    def body(v_ref, o_ref, *scratch):
        start, finish = rider.phases([v_ref], [o_ref], scratch)
        start()
        finish()

    return pl.pallas_call(
        body, name=name, in_specs=rider.in_specs, out_specs=rider.out_specs[0], out_shape=rider.out_shapes[0],
        scratch_shapes=rider.scratch, compiler_params=pltpu.CompilerParams(vmem_limit_bytes=VMEM_LIMIT),
    )(vals)


def _all_reduce_rider(vals):
    r, cc = vals.shape

    def phases(ins, outs, scratch):
        (v_ref,), (o_ref,), (buf, send_sems, recv_sems) = ins, outs, scratch
        x, y, c = _place()
        me = 4 * x + 2 * y + c

        def copies():
            cps = []
            for k in range(1, N_DEV):
                kx, ky, kc = (k >> 2) & 1, (k >> 1) & 1, k & 1
                peer = (1 - x if kx else x, 1 - y if ky else y, 1 - c if kc else c)
                cps.append(pltpu.make_async_remote_copy(
                    src_ref=buf.at[0], dst_ref=buf.at[k], send_sem=send_sems.at[k - 1],
                    recv_sem=recv_sems.at[k - 1], device_id=peer, device_id_type=MESH))
            return cps

        def start():
            buf[0] = v_ref[...]
            for cp in copies():
                cp.start()

        def finish():
            for cp in copies():
                cp.wait()
            acc = buf[jnp.bitwise_xor(me, 0)]
            for dev in range(1, N_DEV):
                acc = acc + buf[jnp.bitwise_xor(me, dev)]
            o_ref[...] = acc

        return start, finish

    vm = pl.BlockSpec(memory_space=pltpu.VMEM)
    return _Rider([vals], [vm], [jax.ShapeDtypeStruct((r, cc), F32)], [vm],
                  [pltpu.VMEM((N_DEV, r, cc), F32), pltpu.SemaphoreType.DMA((7,)), pltpu.SemaphoreType.DMA((7,))],
                  phases)


def _chip_exchange_rider(partial):
    n = len(partial)
    return _Rider(list(partial), [ANY] * n, _chip_exchange_shapes(partial), [ANY] * n, _chip_exchange_sems(n),
                  lambda ins, outs, scratch: _chip_exchange_phases(ins, outs, *scratch))


def _lanes(flat):
    pad = (-flat.shape[0]) % (SUBLANES * LANES)
    return jnp.pad(flat, (0, pad)).reshape(-1, LANES)


def kernel(x, even_w_in, even_b_f, even_conv_w, even_w_out, odd_w_in, odd_v_ln_g, odd_v_ln_b, odd_w_s, odd_b_s, odd_w_out, mix_ln_g, mix_ln_b, ffn_w_in, ffn_w_out, ffn_ln_g, ffn_ln_b, loss_target, m_even_w_in, m_even_b_f, m_even_conv_w, m_even_w_out, m_odd_w_in, m_odd_v_ln_g, m_odd_v_ln_b, m_odd_w_s, m_odd_b_s, m_odd_w_out, m_mix_ln_g, m_mix_ln_b, m_ffn_w_in, m_ffn_w_out, m_ffn_ln_g, m_ffn_ln_b, v_even_w_in, v_even_b_f, v_even_conv_w, v_even_w_out, v_odd_w_in, v_odd_v_ln_g, v_odd_v_ln_b, v_odd_w_s, v_odd_b_s, v_odd_w_out, v_mix_ln_g, v_mix_ln_b, v_ffn_w_in, v_ffn_w_out, v_ffn_ln_g, v_ffn_ln_b):
    t, d = x.shape[1], x.shape[2]
    nh = even_b_f.shape[-1]
    w = even_conv_w.shape[-1] * N_DEV
    dh = w // nh
    scale = dh ** -0.5
    e_in = even_w_in.shape[-1] * N_DEV
    f2 = ffn_w_in.shape[-1] * N_DEV
    f = f2 // 2
    ng, pb = odd_w_s.shape[1], odd_w_s.shape[2]
    assert e_in == 6 * w + nh and nh <= SUBLANES and (6 * w) % LANES == 0 and d % N_DEV == 0
    mx, my, mc = _place()
    me = 4 * mx + 2 * my + mc

    big = [even_w_in[0], even_w_out[0], odd_w_in[0], odd_w_out[0],
           ffn_w_in[0], ffn_w_in[1], ffn_w_out[0], ffn_w_out[1]]
    g_in0, = _all_gather([big[0].astype(BF16)], "ag_even_w_in")
    w_in0 = g_in0.transpose(1, 0, 2).reshape(d, e_in)
    w_all0 = jnp.concatenate([w_in0[:, :3 * w], w_in0[:, 3 * w + nh:], w_in0[:, 3 * w:3 * w + nh],
                              jnp.zeros((d, LANES - nh), BF16)], axis=1)

    cs, vs = even_conv_w.shape[-1], odd_v_ln_g.shape[-1]
    small_mine = jnp.concatenate([
        lax.dynamic_update_slice(jnp.zeros((3, w), F32), even_conv_w[0], (0, me * cs)).reshape(-1),
        lax.dynamic_update_slice(jnp.zeros((d,), F32), odd_v_ln_g[0], (me * vs,)),
        lax.dynamic_update_slice(jnp.zeros((d,), F32), odd_v_ln_b[0], (me * vs,))])
    small_all = _all_reduce_small(_lanes(small_mine), "ag_small").reshape(-1)
    conv_w = small_all[:3 * w].reshape(3, w)
    vln_g = small_all[3 * w:3 * w + d]
    vln_b = small_all[3 * w + d:3 * w + 2 * d]

    bf_pad = jnp.pad(even_b_f[0], (0, LANES - nh)).reshape(1, LANES)
    chunk = jnp.arange(pb) // (pb // 2)
    ws_mask = (chunk[None, :] <= chunk[:, None])[None]
    wm = jnp.where(ws_mask, odd_w_s[0], 0.0).astype(BF16)
    bs_full = jnp.repeat(odd_b_s[0].T, d // ng, axis=1)

    x0 = x[0]
    tgt = loss_target[0]
    fcol = 6 * w // LANES
    x0b = x0.astype(BF16)
    p0 = _mm(x0b, w_all0, "nn", F32, "l0_in_proj")
    cgate = _fgate_fwd(p0, bf_pad, fcol, nh, "l0_fgate")
    assert dh + 7 <= LANES
    qa, ka, va = _attn_pack(p0, cgate, w, nh, scale, "l0_attn_pack")
    oa, g_out0, g_in1, g_out1, g_fi0, g_fi1, g_fo0, g_fo1 = _attn_fwd(
        qa, ka, va, dh, "l0_attn", gather=[s.astype(BF16) for s in big[1:]])
    w_out0, w_out1 = g_out0.reshape(2 * w, d), g_out1.reshape(d, d)
    w_fo0, w_fo1 = g_fo0.reshape(f, d), g_fo1.reshape(f, d)
    nb = N_DEV // 2
    w_fi0, w_fi1 = g_fi0.reshape(2, nb, d, -1), g_fi1.reshape(2, nb, d, -1)
    attn, = _attn_unpack(oa, w, nh, 1.0, None, 1.0, "l0_attn_unpack")
    yconv = _conv_fwd(p0, conv_w, w, 3, "l0_conv")
    mix = jnp.concatenate([attn, yconv], axis=1)
    m0, x1, x1b = _mm_ln(mix, w_out0, x0, mix_ln_g[0], mix_ln_b[0], "l0_out_proj_ln")
    h0, gu0 = _ffn_in_swiglu(x1b, w_fi0, "l0_ffn_in")
    f0, x2, x2b = _mm_ln(h0, w_fo0, x1, ffn_ln_g[0], ffn_ln_b[0], "l0_ffn_out_ln")

    uv = _mm_cols_fwd(x2b, g_in1, False, F32, "l1_in_proj")
    gated = _sgu_fwd(uv, vln_g, vln_b, wm, bs_full, "l1_sgu")
    m1, x3, x3b = _mm_ln(gated, w_out1, x2, mix_ln_g[1], mix_ln_b[1], "l1_out_proj_ln")
    h1, gu1 = _ffn_in_swiglu(x3b, w_fi1, "l1_ffn_in")
    dz4, dz4b, g_ffn_g1, g_ffn_b1, loss_part = _mm_ln(h1, w_fo1, x3, ffn_ln_g[1], ffn_ln_b[1],
                                                      "l1_ffn_out_ln_loss", target=tgt)
    gd_fo1 = _mm_blk_dw(h1, dz4b, BF16, "l1_ffn_out_dw").reshape(N_DEV, -1, d)
    dgu1 = _ffn_out_dx_swiglu(dz4b, w_fo1, gu1, "l1_ffn_out_dx").reshape(N_DEV, t, -1)
    gd_fi1 = _mm_cols_dw(x3b, dgu1, N_DEV, True, BF16, "l1_ffn_in_dw")
    dz3, dz3b, g_mix_g1, g_mix_b1 = _mm_cols_dx_ln_bwd(dgu1, g_fi1, True, x2, m1, mix_ln_g[1], dz4,
                                                       "l1_ffn_in_dx_ln_bwd")
    gd_out1 = _mm(gated, dz3b, "tn", BF16, "l1_out_proj_dw").reshape(N_DEV, -1, d)
    dgated = _mm(dz3b, w_out1, "nt", BF16, "l1_out_proj_dx")
    duv, g_wm, g_bs_t, g_vln_g, g_vln_b = _sgu_bwd(uv, vln_g, vln_b, wm, bs_full, dgated, "l1_sgu_bwd")
    gd_in1 = _mm_cols_dw(x2b, duv, N_DEV, False, BF16, "l1_in_proj_dw")

    dz2, dz2b, g_ffn_g0, g_ffn_b0 = _mm_cols_dx_ln_bwd(duv, g_in1, False, x1, f0, ffn_ln_g[0], dz3,
                                                       "l1_in_proj_dx_ln_bwd")
    gd_fo0 = _mm_blk_dw(h0, dz2b, BF16, "l0_ffn_out_dw").reshape(N_DEV, -1, d)
    dgu0 = _ffn_out_dx_swiglu(dz2b, w_fo0, gu0, "l0_ffn_out_dx").reshape(N_DEV, t, -1)
    gd_fi0 = _mm_cols_dw(x1b, dgu0, N_DEV, True, BF16, "l0_ffn_in_dw")
    dz1, dz1b, g_mix_g0, g_mix_b0 = _mm_cols_dx_ln_bwd(dgu0, g_fi0, True, x0, m0, mix_ln_g[0], dz2,
                                                       "l0_ffn_in_dx_ln_bwd")
    gd_out0 = _mm(mix, dz1b, "tn", BF16, "l0_out_proj_dw").reshape(N_DEV, -1, d)
    early_g = [gd_out0, gd_in1, gd_out1, gd_fi0, gd_fi1, gd_fo0, gd_fo1]
    dmix, *early_sib = _mm(dz1b, w_out0, "nt", F32, "l0_out_proj_dx", rider=_sibling_exchange_rider(early_g))
    d_b, d_c, d_h, g_conv = _conv_bwd(p0, conv_w, dmix, w, 3, "l0_conv_bwd")
    doa, qa2 = _attn_pack_bwd(dmix, oa, qa, w, nh, "l0_attn_pack_bwd")
    big_names = ["even_w_in", "even_w_out", "odd_w_in", "odd_w_out", "ffn_w_in0", "ffn_w_in1", "ffn_w_out0", "ffn_w_out1"]
    c_idx = mc.reshape(1).astype(jnp.int32)
    chip_idx = (2 * mx + my).reshape(1).astype(jnp.int32)
    early_partial = [_rs_chip_sum(g, s, c_idx, "rs_chip_sum_" + n)
                     for g, s, n in zip(early_g, early_sib, big_names[1:])]
    dqa, dka, dva, *early_received = _attn_bwd(qa2, ka, va, doa, "l0_attn_bwd", exchange=early_partial)
    dq, dcq = _attn_unpack(dqa, w, nh, scale, dh + 3, 1.0, "l0_attn_unpack_dq")
    dk, dck = _attn_unpack(dka, w, nh, 1.0, dh, -1.0, "l0_attn_unpack_dk")
    dv, = _attn_unpack(dva, w, nh, 1.0, None, 1.0, "l0_attn_unpack_dv")
    dzf, g_bf = _fgate_bwd(p0, bf_pad, dcq, dck, fcol, nh, "l0_fgate_bwd")
    dp0 = jnp.concatenate([dq, dk, dv, d_b.astype(BF16), d_c.astype(BF16), d_h.astype(BF16), dzf.astype(BF16)], axis=1)
    g_ws = jnp.where(ws_mask, g_wm, 0.0)
    g_bs = g_bs_t[:, :ng].T
    small_g = [g_bf[:nh], g_conv, g_vln_g, g_vln_b, g_ws, g_bs,
               jnp.stack([g_mix_g0, g_mix_g1]), jnp.stack([g_mix_b0, g_mix_b1]),
               jnp.stack([g_ffn_g0, g_ffn_g1]), jnp.stack([g_ffn_b0, g_ffn_b1])]
    small_rider = _all_reduce_rider(_lanes(jnp.concatenate([a.reshape(-1) for a in small_g])))
    g_all0, small_sum = _mm(x0b, dp0, "tn", F32, "l0_in_proj_dw", rider=small_rider)
    gd_in0 = jnp.concatenate([g_all0[:, :3 * w], g_all0[:, 6 * w:6 * w + nh], g_all0[:, 3 * w:6 * w]], axis=1)
    gd_in0 = gd_in0.reshape(d, N_DEV, -1).transpose(1, 0, 2).astype(BF16)

    big_m = [m_even_w_in[0], m_even_w_out[0], m_odd_w_in[0], m_odd_w_out[0],
             m_ffn_w_in[0], m_ffn_w_in[1], m_ffn_w_out[0], m_ffn_w_out[1]]
    big_v = [v_even_w_in[0], v_even_w_out[0], v_odd_w_in[0], v_odd_w_out[0],
             v_ffn_w_in[0], v_ffn_w_in[1], v_ffn_w_out[0], v_ffn_w_out[1]]
    late_sib = _rs_sibling_exchange([gd_in0], "rs_sibling_late")
    late_partial = [_rs_chip_sum(gd_in0, late_sib[0], c_idx, "rs_chip_sum_" + big_names[0])]
    partial = late_partial + early_partial
    grad_x, *late_received = _mm(dp0, w_all0, "nt", F32, "l0_in_proj_dx", rider=_chip_exchange_rider(late_partial),
                                 addend=(dz1, ALPHA))
    received = list(late_received) + list(early_received)
    upd = [_rs_final_adamw(p, r, chip_idx, wt, mt, vt, "rs_final_adamw_" + n)
           for p, r, wt, mt, vt, n in zip(partial, received, big, big_m, big_v, big_names)]
    big_out = {}
    for i, n in enumerate(["even_w_in", "even_w_out", "odd_w_in", "odd_w_out"]):
        big_out[n] = [o[None] for o in upd[i]]
    big_out["ffn_w_in"] = [jnp.stack([a, b]) for a, b in zip(upd[4], upd[5])]
    big_out["ffn_w_out"] = [jnp.stack([a, b]) for a, b in zip(upd[6], upd[7])]

    small_sum = small_sum.reshape(-1)
    outs_small = []
    off = 0
    for a in small_g:
        outs_small.append(small_sum[off:off + a.size].reshape(a.shape))
        off += a.size
    gr_bf, gr_conv, gr_vg, gr_vb, gr_ws, gr_bs, gr_mg, gr_mb, gr_fg, gr_fb = outs_small

    loss = lax.psum(loss_part, ("x", "y", "c"))

    grads = {
        "even_b_f": gr_bf[None],
        "even_conv_w": lax.dynamic_slice(gr_conv, (0, me * cs), (3, cs))[None],
        "odd_v_ln_g": lax.dynamic_slice(gr_vg, (me * vs,), (vs,))[None],
        "odd_v_ln_b": lax.dynamic_slice(gr_vb, (me * vs,), (vs,))[None],
        "odd_w_s": gr_ws[None], "odd_b_s": gr_bs[None],
        "mix_ln_g": gr_mg, "mix_ln_b": gr_mb, "ffn_ln_g": gr_fg, "ffn_ln_b": gr_fb,
    }
    weights = dict(even_w_in=even_w_in, even_b_f=even_b_f, even_conv_w=even_conv_w, even_w_out=even_w_out,
                   odd_w_in=odd_w_in, odd_v_ln_g=odd_v_ln_g, odd_v_ln_b=odd_v_ln_b, odd_w_s=odd_w_s,
                   odd_b_s=odd_b_s, odd_w_out=odd_w_out, mix_ln_g=mix_ln_g, mix_ln_b=mix_ln_b,
                   ffn_w_in=ffn_w_in, ffn_w_out=ffn_w_out, ffn_ln_g=ffn_ln_g, ffn_ln_b=ffn_ln_b)
    moms = dict(even_w_in=(m_even_w_in, v_even_w_in), even_b_f=(m_even_b_f, v_even_b_f),
                even_conv_w=(m_even_conv_w, v_even_conv_w), even_w_out=(m_even_w_out, v_even_w_out),
                odd_w_in=(m_odd_w_in, v_odd_w_in), odd_v_ln_g=(m_odd_v_ln_g, v_odd_v_ln_g),
                odd_v_ln_b=(m_odd_v_ln_b, v_odd_v_ln_b), odd_w_s=(m_odd_w_s, v_odd_w_s),
                odd_b_s=(m_odd_b_s, v_odd_b_s), odd_w_out=(m_odd_w_out, v_odd_w_out),
                mix_ln_g=(m_mix_ln_g, v_mix_ln_g), mix_ln_b=(m_mix_ln_b, v_mix_ln_b),
                ffn_w_in=(m_ffn_w_in, v_ffn_w_in), ffn_w_out=(m_ffn_w_out, v_ffn_w_out),
                ffn_ln_g=(m_ffn_ln_g, v_ffn_ln_g), ffn_ln_b=(m_ffn_ln_b, v_ffn_ln_b))
    names = list(weights)
    gout, deltas, new_m, new_v = [], [], [], []
    for n in names:
        if n in big_out:
            gr, dlt, mn, vn = big_out[n]
        else:
            gr = grads[n]
            dlt, mn, vn = _adamw(weights[n], gr, moms[n][0], moms[n][1], "adamw_" + n)
        gout.append(gr.reshape(weights[n].shape))
        deltas.append(dlt.reshape(weights[n].shape))
        new_m.append(mn.reshape(weights[n].shape))
        new_v.append(vn.reshape(weights[n].shape))
    return (loss, grad_x[None], *gout, *deltas, *new_m, *new_v)
```

```python
import functools
from typing import Callable, NamedTuple

import jax
import jax.numpy as jnp
from jax import lax
from jax.experimental import pallas as pl
from jax.experimental.pallas import tpu as pltpu

F32 = jnp.float32
BF16 = jnp.bfloat16
MESH = pl.DeviceIdType.MESH

DEPTH = 2
ALPHA = (2.0 * DEPTH) ** 0.25
LN_EPS = 1e-5
ADAM_LR = 0.001
ADAM_B1 = 0.9
ADAM_B2 = 0.999
ADAM_EPS = 1e-08
ADAM_WD = 0.01
ADAM_STEP = 10

N_DEV = 8
LANES = 128
SUBLANES = 8
VMEM_LIMIT = 48 * 1024 * 1024
NEG_BIG = -1e30
ROW_TILES = (512, 256, 128)


def _pick(n, cands):
    for c in cands:
        if c <= n and n % c == 0:
            return c
    return n


def _params(*sem):
    return pltpu.CompilerParams(dimension_semantics=sem, vmem_limit_bytes=VMEM_LIMIT)


NN = (((1,), (0,)), ((), ()))
NT = (((1,), (1,)), ((), ()))
TN = (((0,), (0,)), ((), ()))
M_TILES = (1024, 512, 1408, 256, 128)
N_TILES = (512, 640, 256, 128)
K_TILES = (4096, 2048, 1024, 512, 640, 1408, 256, 128)
K_WHOLE = 3328


class _Rider(NamedTuple):
    inputs: list
    in_specs: list
    out_shapes: list
    out_specs: list
    scratch: list
    phases: Callable


def _mm_core(name, grid, a, b, a_spec, b_spec, o_spec, o_shape, o_dtype, dims, tile, pieces=None, rider=None,
             addend=None):
    nred = grid[2]
    pieces = pieces or [(lambda r: r[...], lambda r: r[...])]
    ni = len(rider.inputs) if rider else 0
    no = len(rider.out_shapes) if rider else 0
    nacc = 0 if nred == 1 else 1
    add_arrays = [addend[0]] if addend else []

    def body(a_ref, b_ref, *rest):
        if addend:
            add_ref, rest = rest[0], rest[1:]
        finished = (lambda v: addend[1] * add_ref[...] + v) if addend else (lambda v: v)
        o_ref = rest[ni]
        if rider:
            start, finish = rider.phases(rest[:ni], rest[ni + 1:ni + 1 + no], rest[ni + 1 + no + nacc:])
            ids = [pl.program_id(ax) for ax in range(3)]
            first = functools.reduce(jnp.logical_and, [i == 0 for i in ids])
            last = functools.reduce(jnp.logical_and, [i == g - 1 for i, g in zip(ids, grid)])
            pl.when(first)(start)
        part = None
        for fa, fb in pieces:
            prod = lax.dot_general(fa(a_ref).astype(BF16), fb(b_ref).astype(BF16), dims, preferred_element_type=F32)
            part = prod if part is None else part + prod
        if nred == 1:
            o_ref[...] = finished(part).astype(o_ref.dtype)
        else:
            acc_ref = rest[ni + 1 + no]
            kk = pl.program_id(2)

            @pl.when(kk == 0)
            def _():
                acc_ref[...] = jnp.zeros_like(acc_ref)

            acc_ref[...] += part

            @pl.when(kk == nred - 1)
            def _():
                o_ref[...] = finished(acc_ref[...]).astype(o_ref.dtype)
        if rider:
            pl.when(last)(finish)

    out = pl.pallas_call(
        body, name=name, grid=grid,
        in_specs=[a_spec, b_spec] + ([o_spec] if addend else []) + (rider.in_specs if rider else []),
        out_specs=[o_spec] + (rider.out_specs if rider else []),
        out_shape=[jax.ShapeDtypeStruct(o_shape, o_dtype)] + (rider.out_shapes if rider else []),
        scratch_shapes=([] if nred == 1 else [pltpu.VMEM(tile, F32)]) + (rider.scratch if rider else []),
        compiler_params=_params(*(["arbitrary"] * 3 if rider else ["parallel", "parallel", "arbitrary"])),
    )(a, b, *add_arrays, *(rider.inputs if rider else []))
    return out if rider else out[0]


def _mm(a, b, mode, out_dtype, name, rider=None, addend=None):
    if mode == "nn":
        (m, k), (k2, n) = a.shape, b.shape
    elif mode == "nt":
        (m, k), (n, k2) = a.shape, b.shape
    else:
        (k, m), (k2, n) = a.shape, b.shape
    assert k == k2, (a.shape, b.shape, mode)
    tm, tn = _pick(m, M_TILES), _pick(n, N_TILES)
    tk = k if k <= K_WHOLE else _pick(k, K_TILES)
    if mode == "nn":
        a_spec = pl.BlockSpec((tm, tk), lambda i, j, kk: (i, kk))
        b_spec = pl.BlockSpec((tk, tn), lambda i, j, kk: (kk, j))
        dims = NN
    elif mode == "nt":
        a_spec = pl.BlockSpec((tm, tk), lambda i, j, kk: (i, kk))
        b_spec = pl.BlockSpec((tn, tk), lambda i, j, kk: (j, kk))
        dims = NT
    else:
        a_spec = pl.BlockSpec((tk, tm), lambda i, j, kk: (kk, i))
        b_spec = pl.BlockSpec((tk, tn), lambda i, j, kk: (kk, j))
        dims = TN
    return _mm_core(name, (m // tm, n // tn, k // tk), a, b, a_spec, b_spec,
                    pl.BlockSpec((tm, tn), lambda i, j, kk: (i, j)), (m, n), out_dtype, dims, (tm, tn), rider=rider,
                    addend=addend)


def _act_spec(blocked, rows, ns, row_ax, d_ax):
    if blocked:
        return pl.BlockSpec((None, rows, ns), lambda *g: (g[d_ax], g[row_ax], 0))
    return pl.BlockSpec((rows, ns), lambda *g: (g[row_ax], g[d_ax]))


def _mm_cols_fwd(a, g3, blocked, out_dtype, name):
    (t, k), (nd, k2, ns) = a.shape, g3.shape
    assert k == k2
    tm, tk = _pick(t, M_TILES), _pick(k, K_TILES)
    return _mm_core(name, (t // tm, nd, k // tk), a, g3,
                    pl.BlockSpec((tm, tk), lambda i, d, kk: (i, kk)),
                    pl.BlockSpec((None, tk, ns), lambda i, d, kk: (d, kk, 0)),
                    _act_spec(blocked, tm, ns, 0, 1), (nd, t, ns) if blocked else (t, nd * ns), out_dtype, NN, (tm, ns))


def _mm_cols_dx_ln_bwd(dy, g3, blocked, xa, xb, gam, dya, name):
    nd, k, ns = g3.shape
    t, d = xa.shape
    assert k == d
    tm = _pick(t, ROW_TILES)
    grp = nd if not blocked else (2 if nd % 2 == 0 else 1)
    nred = nd // grp

    def body(a_ref, b_ref, xa_ref, xb_ref, g_ref, dya_ref, dz_ref, dzb_ref, dg_ref, db_ref, *acc):
        i, kk = pl.program_id(0), pl.program_id(1)

        @pl.when(jnp.logical_and(i == 0, kk == 0))
        def _():
            dg_ref[...] = jnp.zeros_like(dg_ref)
            db_ref[...] = jnp.zeros_like(db_ref)

        if blocked:
            part = None
            for s in range(grp):
                prod = lax.dot_general(a_ref[s], b_ref[s], NT, preferred_element_type=F32)
                part = prod if part is None else part + prod
        else:
            whole_b = jnp.concatenate([b_ref[s] for s in range(nd)], axis=1)
            part = lax.dot_general(a_ref[...], whole_b, NT, preferred_element_type=F32)

        def ln_bwd(dyb):
            dy_t = ALPHA * dya_ref[...] + dyb
            z = ALPHA * xa_ref[...] + xb_ref[...]
            mu = jnp.mean(z, axis=-1, keepdims=True)
            zc = z - mu
            var = jnp.mean(zc * zc, axis=-1, keepdims=True)
            rstd = lax.rsqrt(var + LN_EPS)
            xhat = zc * rstd
            dxh = dy_t * g_ref[...]
            m1 = jnp.mean(dxh, axis=-1, keepdims=True)
            m2 = jnp.mean(dxh * xhat, axis=-1, keepdims=True)
            dz = rstd * (dxh - m1 - xhat * m2)
            dz_ref[...] = dz
            dzb_ref[...] = dz.astype(BF16)
            dg_ref[...] += jnp.sum(dy_t * xhat, axis=0, keepdims=True)
            db_ref[...] += jnp.sum(dy_t, axis=0, keepdims=True)

        if nred == 1:
            ln_bwd(part)
        else:
            acc_ref, = acc

            @pl.when(kk == 0)
            def _():
                acc_ref[...] = part

            @pl.when(kk > 0)
            def _():
                acc_ref[...] += part

            @pl.when(kk == nred - 1)
            def _():
                ln_bwd(acc_ref[...])

    row = pl.BlockSpec((tm, d), lambda i, kk: (i, 0))
    vec = pl.BlockSpec((1, d), lambda i, kk: (0, 0))
    if blocked:
        a_spec = pl.BlockSpec((grp, tm, ns), lambda i, kk: (kk, i, 0))
        b_spec = pl.BlockSpec((grp, k, ns), lambda i, kk: (kk, 0, 0))
    else:
        a_spec = pl.BlockSpec((tm, nd * ns), lambda i, kk: (i, 0))
        b_spec = pl.BlockSpec((nd, k, ns), lambda i, kk: (0, 0, 0))
    dz, dzb, dg, db = pl.pallas_call(
        body, name=name, grid=(t // tm, nred),
        in_specs=[a_spec, b_spec, row, row, vec, row], out_specs=[row, row, vec, vec],
        out_shape=[jax.ShapeDtypeStruct((t, d), F32), jax.ShapeDtypeStruct((t, d), BF16),
                   jax.ShapeDtypeStruct((1, d), F32), jax.ShapeDtypeStruct((1, d), F32)],
        scratch_shapes=[] if nred == 1 else [pltpu.VMEM((tm, d), F32)],
        compiler_params=_params("arbitrary", "arbitrary"),
    )(dy, g3, xa, xb, gam.reshape(1, d), dya)
    return dz, dzb, dg[0], db[0]


def _mm_cols_dw(a, dy, nd, blocked, out_dtype, name):
    t, k = a.shape
    ns = dy.shape[2] if blocked else dy.shape[1] // nd
    tmk, tk = _pick(k, M_TILES), _pick(t, K_TILES)
    return _mm_core(name, (nd, k // tmk, t // tk), a, dy,
                    pl.BlockSpec((tk, tmk), lambda d, j, kk: (kk, j)),
                    _act_spec(blocked, tk, ns, 2, 0),
                    pl.BlockSpec((None, tmk, ns), lambda d, j, kk: (d, j, 0)), (nd, k, ns), out_dtype, TN, (tmk, ns))


def _mm_blk_dw(h3, dz, out_dtype, name):
    (nb, t, ns), (_, n) = h3.shape, dz.shape
    tn, tk = _pick(n, (1024,) + N_TILES), _pick(t, K_TILES)
    return _mm_core(name, (nb, n // tn, t // tk), h3, dz,
                    pl.BlockSpec((None, tk, ns), lambda d, j, kk: (d, kk, 0)),
                    pl.BlockSpec((tk, tn), lambda d, j, kk: (kk, j)),
                    pl.BlockSpec((ns, tn), lambda d, j, kk: (d, j)), (nb * ns, n), out_dtype, TN, (ns, tn))


def _mm_ln(a, w, xa, g, b, name, target=None):
    blocked = a.ndim == 3
    t, d = xa.shape
    k = w.shape[0]
    tm = _pick(t, ROW_TILES)
    nb = a.shape[0] if blocked else 1
    ns = k // nb
    halves = [slice(0, tm // 2), slice(tm // 2, tm)] if tm % 32 == 0 else [slice(0, tm)]

    def body(a_ref, w_ref, xa_ref, g_ref, b_ref, *rest):
        def product(rows):
            if not blocked:
                return jnp.dot(a_ref[rows, :], w_ref[...], preferred_element_type=F32)
            acc = None
            for s in range(nb):
                prod = jnp.dot(a_ref[s, rows, :], w_ref[s * ns:(s + 1) * ns, :], preferred_element_type=F32)
                acc = prod if acc is None else acc + prod
            return acc

        if target is not None:
            t_ref, dz_ref, dzb_ref, dg_ref, db_ref, l_ref = rest

            @pl.when(pl.program_id(0) == 0)
            def _():
                l_ref[...] = jnp.zeros_like(l_ref)
                dg_ref[...] = jnp.zeros_like(dg_ref)
                db_ref[...] = jnp.zeros_like(db_ref)
        else:
            xb_ref, y_ref, yb_ref = rest
        for rows, xb in zip(halves, [product(rows) for rows in halves]):
            z = ALPHA * xa_ref[rows, :] + xb
            mu = jnp.mean(z, axis=-1, keepdims=True)
            zc = z - mu
            var = jnp.mean(zc * zc, axis=-1, keepdims=True)
            rstd = lax.rsqrt(var + LN_EPS)
            xhat = zc * rstd
            y = xhat * g_ref[...] + b_ref[...]
            if target is not None:
                e = y - t_ref[rows, :]
                l_ref[...] += 0.5 * jnp.sum(jnp.mean(e * e, axis=-1, keepdims=True))
                dy = e * (1.0 / d)
                dxh = dy * g_ref[...]
                m1 = jnp.mean(dxh, axis=-1, keepdims=True)
                m2 = jnp.mean(dxh * xhat, axis=-1, keepdims=True)
                dz = rstd * (dxh - m1 - xhat * m2)
                dz_ref[rows, :] = dz
                dzb_ref[rows, :] = dz.astype(BF16)
                dg_ref[...] += jnp.sum(dy * xhat, axis=0, keepdims=True)
                db_ref[...] += jnp.sum(dy, axis=0, keepdims=True)
            else:
                xb_ref[rows, :] = xb
                y_ref[rows, :] = y
                yb_ref[rows, :] = y.astype(BF16)

    row = pl.BlockSpec((tm, d), lambda i: (i, 0))
    vec = pl.BlockSpec((1, d), lambda i: (0, 0))
    a_spec = pl.BlockSpec((nb, tm, ns), lambda i: (0, i, 0)) if blocked else pl.BlockSpec((tm, k), lambda i: (i, 0))
    ins = [a, w, xa, g.reshape(1, d), b.reshape(1, d)]
    in_specs = [a_spec, pl.BlockSpec((k, d), lambda i: (0, 0)), row, vec, vec]
    if target is not None:
        dz, dzb, dg, db, l = pl.pallas_call(
            body, name=name, grid=(t // tm,), in_specs=in_specs + [row],
            out_specs=[row, row, vec, vec, pl.BlockSpec((1, LANES), lambda i: (0, 0))],
            out_shape=[jax.ShapeDtypeStruct((t, d), F32), jax.ShapeDtypeStruct((t, d), BF16),
                       jax.ShapeDtypeStruct((1, d), F32), jax.ShapeDtypeStruct((1, d), F32),
                       jax.ShapeDtypeStruct((1, LANES), F32)],
            compiler_params=_params("arbitrary"),
        )(*ins, target)
        return dz, dzb, dg[0], db[0], l[0, 0]
    return pl.pallas_call(
        body, name=name, grid=(t // tm,), in_specs=in_specs, out_specs=[row, row, row],
        out_shape=[jax.ShapeDtypeStruct((t, d), F32)] * 2 + [jax.ShapeDtypeStruct((t, d), BF16)],
        compiler_params=_params("parallel"),
    )(*ins)


def _ffn_in_swiglu(xb, g4, name):
    (t, k), (_, nb, _, ns) = xb.shape, g4.shape
    tm = _pick(t, M_TILES)

    def body(x_ref, w_ref, h_ref, gu_ref):
        xv = x_ref[...]
        gate = jnp.dot(xv, w_ref[0], preferred_element_type=F32)
        up = jnp.dot(xv, w_ref[1], preferred_element_type=F32)
        h_ref[...] = (gate * jax.nn.sigmoid(gate) * up).astype(BF16)
        gu_ref[0] = gate.astype(BF16)
        gu_ref[1] = up.astype(BF16)

    return pl.pallas_call(
        body, name=name, grid=(t // tm, nb),
        in_specs=[pl.BlockSpec((tm, k), lambda i, d: (i, 0)),
                  pl.BlockSpec((2, None, k, ns), lambda i, d: (0, d, 0, 0))],
        out_specs=[pl.BlockSpec((None, tm, ns), lambda i, d: (d, i, 0)),
                   pl.BlockSpec((2, None, tm, ns), lambda i, d: (0, d, i, 0))],
        out_shape=[jax.ShapeDtypeStruct((nb, t, ns), BF16), jax.ShapeDtypeStruct((2, nb, t, ns), BF16)],
        compiler_params=_params("parallel", "parallel"),
    )(xb, g4)


def _ffn_out_dx_swiglu(dz, w_out, gu4, name):
    (t, d), (_, nb, _, ns) = dz.shape, gu4.shape
    tm = _pick(t, M_TILES)

    def body(dz_ref, w_ref, gu_ref, o_ref):
        halves = [slice(0, tm // 2), slice(tm // 2, tm)] if tm % 16 == 0 else [slice(0, tm)]
        dhs = [lax.dot_general(dz_ref[rows, :].astype(BF16), w_ref[...], NT, preferred_element_type=F32)
               for rows in halves]
        for rows, dh in zip(halves, dhs):
            gate = gu_ref[0, rows, :].astype(F32)
            up = gu_ref[1, rows, :].astype(F32)
            sg = jax.nn.sigmoid(gate)
            silu = gate * sg
            o_ref[0, rows, :] = (dh * up * (sg + silu * (1.0 - sg))).astype(BF16)
            o_ref[1, rows, :] = (dh * silu).astype(BF16)

    blk = pl.BlockSpec((2, None, tm, ns), lambda i, j: (0, j, i, 0))
    return pl.pallas_call(
        body, name=name, grid=(t // tm, nb),
        in_specs=[pl.BlockSpec((tm, d), lambda i, j: (i, 0)), pl.BlockSpec((ns, d), lambda i, j: (j, 0)), blk],
        out_specs=blk,
        out_shape=jax.ShapeDtypeStruct((2, nb, t, ns), BF16),
        compiler_params=_params("parallel", "parallel"),
    )(dz, w_out, gu4)


def _tri_matmul(tri, x):
    x1 = x.astype(BF16)
    r1 = x - x1.astype(F32)
    x2 = r1.astype(BF16)
    x3 = (r1 - x2.astype(F32)).astype(BF16)
    dot = lambda v: jnp.dot(tri, v, preferred_element_type=F32)
    return dot(x1) + dot(x2) + dot(x3)


def _fgate_fwd(proj, bf_pad, fcol, n_heads, name):
    t = proj.shape[0]
    tb = _pick(t, ROW_TILES)

    def body(p_ref, b_ref, c_ref, carry):
        @pl.when(pl.program_id(0) == 0)
        def _():
            carry[...] = jnp.zeros_like(carry)

        z = p_ref[...] + b_ref[...]
        lf = jnp.minimum(z, 0.0) - jnp.log1p(jnp.exp(-jnp.abs(z)))
        lane = lax.broadcasted_iota(jnp.int32, (tb, LANES), 1)
        lf = jnp.where(lane < n_heads, lf, 0.0)
        r = lax.broadcasted_iota(jnp.int32, (tb, tb), 0)
        s = lax.broadcasted_iota(jnp.int32, (tb, tb), 1)
        tri = (s <= r).astype(BF16)
        c = _tri_matmul(tri, lf) + carry[...]
        c_ref[...] = c
        carry[...] = c[tb - 1:tb, :]

    return pl.pallas_call(
        body, name=name, grid=(t // tb,),
        in_specs=[pl.BlockSpec((tb, LANES), lambda i: (i, fcol)), pl.BlockSpec((1, LANES), lambda i: (0, 0))],
        out_specs=pl.BlockSpec((tb, LANES), lambda i: (i, 0)),
        out_shape=jax.ShapeDtypeStruct((t, LANES), F32),
        scratch_shapes=[pltpu.VMEM((1, LANES), F32)],
        compiler_params=_params("arbitrary"),
    )(proj, bf_pad)


def _fgate_bwd(proj, bf_pad, dcq, dck, fcol, n_heads, name):
    t = proj.shape[0]
    tb = _pick(t, ROW_TILES)
    nb = t // tb

    def body(p_ref, b_ref, dcq_ref, dck_ref, dz_ref, db_ref, carry):
        @pl.when(pl.program_id(0) == 0)
        def _():
            carry[...] = jnp.zeros_like(carry)
            db_ref[...] = jnp.zeros_like(db_ref)

        r = lax.broadcasted_iota(jnp.int32, (tb, tb), 0)
        s = lax.broadcasted_iota(jnp.int32, (tb, tb), 1)
        tri = (s >= r).astype(BF16)
        dlf = _tri_matmul(tri, dcq_ref[...] + dck_ref[...]) + carry[...]
        carry[...] = dlf[0:1, :]
        z = p_ref[...] + b_ref[...]
        lane = lax.broadcasted_iota(jnp.int32, (tb, LANES), 1)
        dz = jnp.where(lane < n_heads, dlf * jax.nn.sigmoid(-z), 0.0)
        dz_ref[...] = dz
        db_ref[...] += jnp.sum(dz, axis=0, keepdims=True)

    dz, db = pl.pallas_call(
        body, name=name, grid=(nb,),
        in_specs=[pl.BlockSpec((tb, LANES), lambda i: (nb - 1 - i, fcol)),
                  pl.BlockSpec((1, LANES), lambda i: (0, 0)),
                  pl.BlockSpec((tb, LANES), lambda i: (nb - 1 - i, 0)),
                  pl.BlockSpec((tb, LANES), lambda i: (nb - 1 - i, 0))],
        out_specs=[pl.BlockSpec((tb, LANES), lambda i: (nb - 1 - i, 0)),
                   pl.BlockSpec((1, LANES), lambda i: (0, 0))],
        out_shape=[jax.ShapeDtypeStruct((t, LANES), F32), jax.ShapeDtypeStruct((1, LANES), F32)],
        scratch_shapes=[pltpu.VMEM((1, LANES), F32)],
        compiler_params=_params("arbitrary"),
    )(proj, bf_pad, dcq, dck)
    return dz, db[0]


def _split3(x):
    hi = x.astype(BF16)
    r = x - hi.astype(F32)
    mid = r.astype(BF16)
    return hi, mid, (r - mid.astype(F32)).astype(BF16)


def _attn_fwd(qa, ka, va, dh, name, gather=()):
    nh, t, da = qa.shape
    tq = _pick(t, ROW_TILES)
    hb = 2 if nh % 2 == 0 else 1
    heads = range(hb)
    n = len(gather)
    steps = (nh // hb, t // tq)

    def body(q_ref, k_ref, v_ref, *rest):
        x_refs, o_ref, g_refs = rest[:n], rest[n], rest[n + 1:2 * n + 1]
        m_s, acc_s, s_a, s_b = rest[2 * n + 1:2 * n + 5]
        qi = pl.program_id(1)
        if n:
            start, forward, finish = _gather_phases(x_refs, g_refs, *rest[2 * n + 5:])
            at = lambda hh, qq: jnp.logical_and(pl.program_id(0) == hh, qi == qq)
            pl.when(at(0, 0))(start)
            pl.when(at(steps[0] // 2, 0))(forward)
        m_s[...] = jnp.full(m_s.shape, NEG_BIG, F32)
        acc_s[...] = jnp.zeros_like(acc_s)

        def scores(s_ref, j):
            off = pl.multiple_of(j * tq, tq)
            for g in heads:
                s_ref[g] = lax.dot_general(q_ref[g], k_ref[g, pl.ds(off, tq), :], NT, preferred_element_type=F32)

        def absorb(s_ref, j, diagonal):
            off = pl.multiple_of(j * tq, tq)
            s = [s_ref[g] for g in heads]
            if diagonal:
                row = lax.broadcasted_iota(jnp.int32, (tq, tq), 0)
                col = lax.broadcasted_iota(jnp.int32, (tq, tq), 1)
                s = [jnp.where(col > row, NEG_BIG, sg) for sg in s]
            m_prev = [m_s[g] for g in heads]
            m_new = [jnp.maximum(m_prev[g], jnp.max(s[g], axis=1, keepdims=True)) for g in heads]
            p = [jnp.exp(s[g] - m_new[g]).astype(BF16) for g in heads]
            pv = [jnp.dot(p[g], v_ref[g, pl.ds(off, tq), :], preferred_element_type=F32) for g in heads]
            for g in heads:
                acc_s[g] = jnp.exp(m_prev[g] - m_new[g]) * acc_s[g] + pv[g]
                m_s[g] = m_new[g]

        def two_blocks(r, carry):
            scores(s_b, 2 * r + 1)
            absorb(s_a, 2 * r, False)
            scores(s_a, 2 * r + 2)
            absorb(s_b, 2 * r + 1, False)
            return carry

        scores(s_a, 0)
        rounds = qi // 2
        lax.fori_loop(0, rounds, two_blocks, 0)

        @pl.when(qi % 2 == 0)
        def _():
            absorb(s_a, qi, True)

        @pl.when(qi % 2 == 1)
        def _():
            scores(s_b, qi)
            absorb(s_a, qi - 1, False)
            absorb(s_b, qi, True)

        lane = lax.broadcasted_iota(jnp.int32, (tq, da), 1)
        for g in heads:
            acc = acc_s[g]
            l = jnp.sum(jnp.where(lane == dh, acc, 0.0), axis=1, keepdims=True)
            o_ref[g] = jnp.where(lane == dh, m_s[g] + jnp.log(l), acc / l)
        if n:
            pl.when(at(steps[0] - 1, steps[1] - 1))(finish)

    full = pl.BlockSpec((hb, t, da), lambda h, qi: (h, 0, 0))
    blk = pl.BlockSpec((hb, tq, da), lambda h, qi: (h, qi, 0))
    return pl.pallas_call(
        body, name=name, grid=steps,
        in_specs=[blk, full, full] + [ANY] * n, out_specs=[blk] + [ANY] * n,
        out_shape=[jax.ShapeDtypeStruct((nh, t, da), F32)] + _gather_shapes(gather),
        scratch_shapes=[pltpu.VMEM((hb, tq, 1), F32), pltpu.VMEM((hb, tq, da), F32),
                        pltpu.VMEM((hb, tq, tq), F32), pltpu.VMEM((hb, tq, tq), F32)] + (_gather_sems(n) if n else []),
        compiler_params=_params("arbitrary", "arbitrary"),
    )(qa, ka, va, *gather)


def _attn_bwd(qa, ka, va, doa, name, exchange=()):
    nh, t, da = qa.shape
    tq = _pick(t, ROW_TILES)
    nq = t // tq
    n = len(exchange)

    def body(q_ref, do_ref, k_ref, v_ref, *rest):
        p_refs, (dq_ref, dk_ref, dv_ref), r_refs = rest[:n], rest[n:n + 3], rest[n + 3:2 * n + 3]
        kj = pl.program_id(1)
        if n:
            start, finish = _chip_exchange_phases(p_refs, r_refs, *rest[2 * n + 3:])
            pl.when(jnp.logical_and(pl.program_id(0) == 0, kj == 0))(start)

        @pl.when(kj == 0)
        def _():
            dq_ref[...] = jnp.zeros_like(dq_ref)

        dk_ref[...] = jnp.zeros_like(dk_ref)
        dv_ref[...] = jnp.zeros_like(dv_ref)
        kb = k_ref[...]
        vb = v_ref[...]

        def step(i, diagonal, blocks=1):
            off = pl.multiple_of(i * tq, tq)
            rows = blocks * tq
            qb = q_ref[pl.ds(off, rows), :]
            dob = do_ref[pl.ds(off, rows), :]
            st = lax.dot_general(kb, qb, NT, preferred_element_type=F32)
            if diagonal:
                row = lax.broadcasted_iota(jnp.int32, (tq, tq), 0)
                col = lax.broadcasted_iota(jnp.int32, (tq, tq), 1)
                st = jnp.where(row > col, NEG_BIG, st)
            pt = jnp.exp(st)
            dst = (pt * lax.dot_general(vb, dob, NT, preferred_element_type=F32)).astype(BF16)
            dv_ref[...] += jnp.dot(pt.astype(BF16), dob, preferred_element_type=F32)
            dk_ref[...] += jnp.dot(dst, qb, preferred_element_type=F32)
            dq_ref[pl.ds(off, rows), :] += lax.dot_general(dst, kb, TN, preferred_element_type=F32)

        step(kj, True)
        odd = (nq - 1 - kj) % 2

        @pl.when(odd == 1)
        def _():
            step(kj + 1, False)

        def loop(r, carry):
            step(kj + 1 + odd + 2 * r, False, blocks=2)
            return carry

        lax.fori_loop(0, (nq - 1 - kj) // 2, loop, 0)
        if n:
            pl.when(jnp.logical_and(pl.program_id(0) == nh - 1, kj == nq - 1))(finish)

    full = pl.BlockSpec((None, t, da), lambda h, j: (h, 0, 0))
    blk = pl.BlockSpec((None, tq, da), lambda h, j: (h, j, 0))
    return pl.pallas_call(
        body, name=name, grid=(nh, nq),
        in_specs=[full, full, blk, blk] + [ANY] * n, out_specs=[full, blk, blk] + [ANY] * n,
        out_shape=[jax.ShapeDtypeStruct((nh, t, da), F32)] * 3 + _chip_exchange_shapes(exchange),
        scratch_shapes=_chip_exchange_sems(n) if n else [],
        compiler_params=_params("arbitrary", "arbitrary"),
    )(qa, doa, ka, va, *exchange)


def _head_group(dh, h):
    g = h // (LANES // dh)
    return slice(g * LANES, (g + 1) * LANES)


def _head_select(dh, h, to_heads):
    r = lax.broadcasted_iota(jnp.int32, (LANES, LANES), 0)
    c = lax.broadcasted_iota(jnp.int32, (LANES, LANES), 1)
    nat, col = (r, c) if to_heads else (c, r)
    return jnp.logical_and(nat == col + (h % (LANES // dh)) * dh, col < dh).astype(BF16)


def _column(x, lane, j):
    return jnp.sum(jnp.where(lane == j, x, 0.0), axis=1, keepdims=True)


def _bias_columns(lane, first, value):
    out = jnp.zeros(lane.shape, F32)
    for j, term in enumerate(_split3(value)):
        out = out + jnp.where(lane == first + j, -term.astype(F32), 0.0)
    return out


def _attn_pack(proj, cgate, w, nh, scale, name):
    t = proj.shape[0]
    dh = w // nh
    tb = _pick(t, ROW_TILES)

    def body(q_ref, k_ref, v_ref, c_ref, qa_ref, ka_ref, va_ref):
        lane = lax.broadcasted_iota(jnp.int32, (tb, LANES), 1)
        ones_qv = jnp.where(jnp.logical_and(lane >= dh, lane < dh + 3), 1.0, 0.0)
        ones_k = jnp.where(jnp.logical_and(lane >= dh + 3, lane < dh + 7), 1.0, 0.0)
        qb = (q_ref[...] * scale).astype(BF16)
        kb = k_ref[...].astype(BF16)
        vb = v_ref[...].astype(BF16)
        cblk = c_ref[...]
        for h in range(nh):
            sel, grp = _head_select(dh, h, True), _head_group(dh, h)
            qa_ref[h] = (jnp.dot(qb[:, grp], sel, preferred_element_type=F32) + ones_qv).astype(BF16)
            va_ref[h] = (jnp.dot(vb[:, grp], sel, preferred_element_type=F32) + ones_qv).astype(BF16)
            bias = _bias_columns(lane, dh, _column(cblk, lane, h))
            ka_ref[h] = (jnp.dot(kb[:, grp], sel, preferred_element_type=F32) + bias + ones_k).astype(BF16)

    col = lambda j: pl.BlockSpec((tb, w), lambda i: (i, j))
    out = pl.BlockSpec((nh, tb, LANES), lambda i: (0, i, 0))
    return pl.pallas_call(
        body, name=name, grid=(t // tb,),
        in_specs=[col(0), col(1), col(2), pl.BlockSpec((tb, LANES), lambda i: (i, 0))],
        out_specs=[out, out, out],
        out_shape=[jax.ShapeDtypeStruct((nh, t, LANES), BF16)] * 3,
        compiler_params=_params("parallel"),
    )(proj, proj, proj, cgate)


def _attn_pack_bwd(dmix, oa, qa, w, nh, name):
    t = dmix.shape[0]
    dh = w // nh
    tb = _pick(t, ROW_TILES)

    def body(d_ref, oa_ref, qa_ref, doa_ref, qa2_ref):
        lane = lax.broadcasted_iota(jnp.int32, (tb, LANES), 1)
        db = d_ref[...].astype(BF16)
        for h in range(nh):
            do_h = jnp.dot(db[:, _head_group(dh, h)], _head_select(dh, h, True), preferred_element_type=F32)
            o_h = oa_ref[h]
            delta = jnp.sum(jnp.where(lane < dh, do_h * o_h, 0.0), axis=1, keepdims=True)
            doa_ref[h] = (do_h + _bias_columns(lane, dh, delta)).astype(BF16)
            qa2_ref[h] = (qa_ref[h].astype(F32) + _bias_columns(lane, dh + 4, _column(o_h, lane, dh))).astype(BF16)

    blk = pl.BlockSpec((nh, tb, LANES), lambda i: (0, i, 0))
    return pl.pallas_call(
        body, name=name, grid=(t // tb,),
        in_specs=[pl.BlockSpec((tb, w), lambda i: (i, 0)), blk, blk], out_specs=[blk, blk],
        out_shape=[jax.ShapeDtypeStruct((nh, t, LANES), BF16)] * 2,
        compiler_params=_params("parallel"),
    )(dmix, oa, qa)


def _attn_unpack(xa, w, nh, mult, sum_col, sum_sign, name):
    t = xa.shape[1]
    dh = w // nh
    tb = _pick(t, ROW_TILES)

    def body(x_ref, o_ref, *rest):
        lane = lax.broadcasted_iota(jnp.int32, (tb, LANES), 1)
        per = LANES // dh
        cols = jnp.zeros((tb, LANES), F32)
        for h0 in range(0, nh, per):
            acc = jnp.zeros((tb, LANES), F32)
            for h in range(h0, h0 + per):
                xh = x_ref[h]
                acc = acc + jnp.dot((xh * mult).astype(BF16), _head_select(dh, h, False), preferred_element_type=F32)
                if sum_col is not None:
                    cols = cols + jnp.where(lane == h, sum_sign * _column(xh, lane, sum_col), 0.0)
            o_ref[:, _head_group(dh, h0)] = acc.astype(BF16)
        if sum_col is not None:
            rest[0][...] = cols

    nat = pl.BlockSpec((tb, w), lambda i: (i, 0))
    lanes = pl.BlockSpec((tb, LANES), lambda i: (i, 0))
    return pl.pallas_call(
        body, name=name, grid=(t // tb,),
        in_specs=[pl.BlockSpec((nh, tb, LANES), lambda i: (0, i, 0))],
        out_specs=[nat, lanes] if sum_col is not None else [nat],
        out_shape=[jax.ShapeDtypeStruct((t, w), BF16)] + ([jax.ShapeDtypeStruct((t, LANES), F32)]
                                                            if sum_col is not None else []),
        compiler_params=_params("parallel"),
    )(xa)


def _conv_fwd(proj, cw, w, bcol, name):
    t = proj.shape[0]
    tb = _pick(t, ROW_TILES)
    hb = tb // SUBLANES

    def body(b_ref, c_ref, h_ref, cp_ref, hp_ref, w_ref, y_ref):
        i = pl.program_id(0)
        zp = jnp.where(i > 0, cp_ref[...] * hp_ref[...], 0.0)
        zext = jnp.concatenate([zp, c_ref[...] * h_ref[...]], axis=0)
        z1 = pltpu.roll(zext, 1, 0)[SUBLANES:]
        z2 = pltpu.roll(zext, 2, 0)[SUBLANES:]
        y = w_ref[2:3, :] * zext[SUBLANES:] + w_ref[1:2, :] * z1 + w_ref[0:1, :] * z2
        y_ref[...] = (b_ref[...] * y).astype(BF16)

    cur = lambda j: pl.BlockSpec((tb, w), lambda i: (i, bcol + j))
    prev = lambda j: pl.BlockSpec((SUBLANES, w), lambda i: (jnp.maximum(i * hb - 1, 0), bcol + j))
    return pl.pallas_call(
        body, name=name, grid=(t // tb,),
        in_specs=[cur(0), cur(1), cur(2), prev(1), prev(2), pl.BlockSpec(cw.shape, lambda i: (0, 0))],
        out_specs=pl.BlockSpec((tb, w), lambda i: (i, 0)),
        out_shape=jax.ShapeDtypeStruct((t, w), BF16), compiler_params=_params("parallel"),
    )(proj, proj, proj, proj, proj, cw)


def _conv_bwd(proj, cw, dmix, w, bcol, name):
    t = proj.shape[0]
    tb = _pick(t, ROW_TILES)
    hb = tb // SUBLANES
    nb = t // tb
    n_ext = tb + SUBLANES

    def body(b_ref, c_ref, h_ref, cp_ref, hp_ref, bn_ref, d_ref, dn_ref, w_ref, db_ref, dc_ref, dh_ref, dw_ref):
        i = pl.program_id(0)
        c = c_ref[...]
        hh = h_ref[...]
        zp = jnp.where(i > 0, cp_ref[...] * hp_ref[...], 0.0)
        zext = jnp.concatenate([zp, c * hh], axis=0)
        z0 = zext[SUBLANES:]
        z1 = pltpu.roll(zext, 1, 0)[SUBLANES:]
        z2 = pltpu.roll(zext, 2, 0)[SUBLANES:]
        y = w_ref[2:3, :] * z0 + w_ref[1:2, :] * z1 + w_ref[0:1, :] * z2
        d = d_ref[...]
        db_ref[...] = d * y
        dy = d * b_ref[...]
        dyn = jnp.where(i < nb - 1, dn_ref[...] * bn_ref[...], 0.0)
        dext = jnp.concatenate([dy, dyn], axis=0)
        dy1 = pltpu.roll(dext, n_ext - 1, 0)[:tb]
        dy2 = pltpu.roll(dext, n_ext - 2, 0)[:tb]
        dz = w_ref[2:3, :] * dy + w_ref[1:2, :] * dy1 + w_ref[0:1, :] * dy2
        dc_ref[...] = dz * hh
        dh_ref[...] = dz * c

        @pl.when(i == 0)
        def _():
            dw_ref[...] = jnp.zeros_like(dw_ref)

        dw_ref[0:1, :] += jnp.sum(dy * z2, axis=0, keepdims=True)
        dw_ref[1:2, :] += jnp.sum(dy * z1, axis=0, keepdims=True)
        dw_ref[2:3, :] += jnp.sum(dy * z0, axis=0, keepdims=True)

    cur = lambda j: pl.BlockSpec((tb, w), lambda i: (i, bcol + j))
    prev = lambda j: pl.BlockSpec((SUBLANES, w), lambda i: (jnp.maximum(i * hb - 1, 0), bcol + j))
    nxt = lambda col: pl.BlockSpec((SUBLANES, w), lambda i: (jnp.minimum((i + 1) * hb, nb * hb - 1), col))
    out = pl.BlockSpec((tb, w), lambda i: (i, 0))
    return pl.pallas_call(
        body, name=name, grid=(nb,),
        in_specs=[cur(0), cur(1), cur(2), prev(1), prev(2), nxt(bcol),
                  pl.BlockSpec((tb, w), lambda i: (i, 1)), nxt(1), pl.BlockSpec(cw.shape, lambda i: (0, 0))],
        out_specs=[out, out, out, pl.BlockSpec(cw.shape, lambda i: (0, 0))],
        out_shape=[jax.ShapeDtypeStruct((t, w), F32)] * 3 + [jax.ShapeDtypeStruct(cw.shape, F32)],
        compiler_params=_params("arbitrary"),
    )(proj, proj, proj, proj, proj, proj, dmix, dmix, cw)


SQRT_HALF = 0.7071067811865476
INV_SQRT_2PI = 0.3989422804014327


def _gelu(x):
    return 0.5 * x * (1.0 + lax.erf(x * SQRT_HALF))


def _gelu_grad(x):
    return 0.5 * (1.0 + lax.erf(x * SQRT_HALF)) + x * (INV_SQRT_2PI * jnp.exp(-0.5 * x * x))


def _sgu_fwd(uv, ln_g, ln_b, wm, bs_full, name):
    t, d2 = uv.shape
    d = d2 // 2
    ng, pb, _ = wm.shape
    gd = d // ng
    tb = _pick(t, ROW_TILES[1:] or ROW_TILES)
    assert tb % pb == 0

    def body(uv_ref, g_ref, b_ref, w_ref, bs_ref, o_ref):
        u = _gelu(uv_ref[:, :d])
        v = _gelu(uv_ref[:, d:])
        mu = jnp.mean(v, axis=-1, keepdims=True)
        vc = v - mu
        var = jnp.mean(vc * vc, axis=-1, keepdims=True)
        vn = (vc * lax.rsqrt(var + LN_EPS) * g_ref[...] + b_ref[...]).astype(BF16)
        for r in range(tb // pb):
            rows = slice(r * pb, (r + 1) * pb)
            for gi in range(ng):
                cols = slice(gi * gd, (gi + 1) * gd)
                s = jnp.dot(w_ref[gi], vn[rows, cols], preferred_element_type=F32) + bs_ref[:, cols]
                o_ref[rows, cols] = (u[rows, cols] * s).astype(BF16)

    vec = pl.BlockSpec((1, d), lambda i: (0, 0))
    return pl.pallas_call(
        body, name=name, grid=(t // tb,),
        in_specs=[pl.BlockSpec((tb, d2), lambda i: (i, 0)), vec, vec,
                  pl.BlockSpec(wm.shape, lambda i: (0, 0, 0)), pl.BlockSpec((pb, d), lambda i: (0, 0))],
        out_specs=pl.BlockSpec((tb, d), lambda i: (i, 0)),
        out_shape=jax.ShapeDtypeStruct((t, d), BF16), compiler_params=_params("parallel"),
    )(uv, ln_g.reshape(1, d), ln_b.reshape(1, d), wm, bs_full)


def _sgu_bwd(uv, ln_g, ln_b, wm, bs_full, dgated, name):
    t, d2 = uv.shape
    d = d2 // 2
    ng, pb, _ = wm.shape
    gd = d // ng
    tb = _pick(t, ROW_TILES[1:] or ROW_TILES)
    nb = t // tb

    def body(uv_ref, g_ref, b_ref, w_ref, bs_ref, dg_ref, o_ref, dw_ref, dbs_ref, dlg_ref, dlb_ref,
             du_s, dvn_s, dbs_s):
        i = pl.program_id(0)

        @pl.when(i == 0)
        def _():
            dw_ref[...] = jnp.zeros_like(dw_ref)
            dbs_s[...] = jnp.zeros_like(dbs_s)
            dlg_ref[...] = jnp.zeros_like(dlg_ref)
            dlb_ref[...] = jnp.zeros_like(dlb_ref)

        upre = uv_ref[:, :d]
        vpre = uv_ref[:, d:]
        u = _gelu(upre)
        v = _gelu(vpre)
        mu = jnp.mean(v, axis=-1, keepdims=True)
        vc = v - mu
        var = jnp.mean(vc * vc, axis=-1, keepdims=True)
        rstd = lax.rsqrt(var + LN_EPS)
        xhat = vc * rstd
        vn = (xhat * g_ref[...] + b_ref[...]).astype(BF16)
        dgt = dg_ref[...].astype(F32)
        for r in range(tb // pb):
            rows = slice(r * pb, (r + 1) * pb)
            for gi in range(ng):
                cols = slice(gi * gd, (gi + 1) * gd)
                vblk = vn[rows, cols]
                s = jnp.dot(w_ref[gi], vblk, preferred_element_type=F32) + bs_ref[:, cols]
                dblk = dgt[rows, cols]
                du_s[rows, cols] = dblk * s
                ds = dblk * u[rows, cols]
                dsb = ds.astype(BF16)
                dvn_s[rows, cols] = lax.dot_general(w_ref[gi], dsb, (((0,), (0,)), ((), ())),
                                                    preferred_element_type=F32)
                dw_ref[gi] += lax.dot_general(dsb, vblk, (((1,), (1,)), ((), ())), preferred_element_type=F32)
                dbs_s[:, cols] += ds
        dvn = dvn_s[...]
        dlg_ref[...] += jnp.sum(dvn * xhat, axis=0, keepdims=True)
        dlb_ref[...] += jnp.sum(dvn, axis=0, keepdims=True)
        dxh = dvn * g_ref[...]
        m1 = jnp.mean(dxh, axis=-1, keepdims=True)
        m2 = jnp.mean(dxh * xhat, axis=-1, keepdims=True)
        dv = rstd * (dxh - m1 - xhat * m2)
        o_ref[:, :d] = (du_s[...] * _gelu_grad(upre)).astype(BF16)
        o_ref[:, d:] = (dv * _gelu_grad(vpre)).astype(BF16)

        @pl.when(i == nb - 1)
        def _():
            lane = lax.broadcasted_iota(jnp.int32, (pb, LANES), 1)
            acc = jnp.zeros((pb, LANES), F32)
            for gi in range(ng):
                col = jnp.sum(dbs_s[:, gi * gd:(gi + 1) * gd], axis=1, keepdims=True)
                acc = acc + jnp.where(lane == gi, col, 0.0)
            dbs_ref[...] = acc

    vec = pl.BlockSpec((1, d), lambda i: (0, 0))
    duv, dw, dbs, dlg, dlb = pl.pallas_call(
        body, name=name, grid=(nb,),
        in_specs=[pl.BlockSpec((tb, d2), lambda i: (i, 0)), vec, vec,
                  pl.BlockSpec(wm.shape, lambda i: (0, 0, 0)), pl.BlockSpec((pb, d), lambda i: (0, 0)),
                  pl.BlockSpec((tb, d), lambda i: (i, 0))],
        out_specs=[pl.BlockSpec((tb, d2), lambda i: (i, 0)), pl.BlockSpec(wm.shape, lambda i: (0, 0, 0)),
                   pl.BlockSpec((pb, LANES), lambda i: (0, 0)), vec, vec],
        out_shape=[jax.ShapeDtypeStruct((t, d2), BF16), jax.ShapeDtypeStruct(wm.shape, F32),
                   jax.ShapeDtypeStruct((pb, LANES), F32), jax.ShapeDtypeStruct((1, d), F32),
                   jax.ShapeDtypeStruct((1, d), F32)],
        scratch_shapes=[pltpu.VMEM((tb, d), F32), pltpu.VMEM((tb, d), F32), pltpu.VMEM((pb, d), F32)],
        compiler_params=_params("arbitrary"),
    )(uv, ln_g.reshape(1, d), ln_b.reshape(1, d), wm, bs_full, dgated)
    return duv, dw, dbs, dlg[0], dlb[0]


def _adamw(w, g, m, v, name):
    shape = w.shape
    cols = shape[-1]
    rows = w.size // cols
    tr = _pick(rows, (512, 256, 352, 128, 64, 32, 16, 8))

    def body(w_ref, g_ref, m_ref, v_ref, d_ref, mo_ref, vo_ref):
        d_ref[...], mo_ref[...], vo_ref[...] = _adam_update(w_ref[...], g_ref[...], m_ref[...], v_ref[...])

    spec = pl.BlockSpec((tr, cols), lambda i: (i, 0))
    outs = pl.pallas_call(
        body, name=name, grid=(rows // tr,),
        in_specs=[spec] * 4, out_specs=[spec] * 3,
        out_shape=[jax.ShapeDtypeStruct((rows, cols), F32)] * 3,
        compiler_params=_params("parallel"),
    )(*[a.reshape(rows, cols) for a in (w, g, m, v)])
    return [o.reshape(shape) for o in outs]


ANY = pl.BlockSpec(memory_space=pl.ANY)


def _place():
    return lax.axis_index("x"), lax.axis_index("y"), lax.axis_index("c")


def _all_gather(shards, name):
    n = len(shards)

    def body(*refs):
        start, forward, finish = _gather_phases(refs[:n], refs[n:2 * n], *refs[2 * n:])
        start()
        forward()
        finish()

    return pl.pallas_call(
        body, name=name, in_specs=[ANY] * n, out_specs=[ANY] * n,
        out_shape=_gather_shapes(shards), scratch_shapes=_gather_sems(n),
    )(*shards)


def _gather_shapes(shards):
    return [jax.ShapeDtypeStruct((N_DEV,) + s.shape, s.dtype) for s in shards]


def _gather_sems(n):
    return [pltpu.SemaphoreType.DMA((7 * n,)), pltpu.SemaphoreType.DMA((7 * n,)), pltpu.SemaphoreType.DMA((n,))]


def _gather_phases(x_refs, out_refs, send_sems, recv_sems, local_sems):
    n = len(x_refs)
    x, y, c = _place()
    me, sibling = (x, y, c), (x, y, 1 - c)
    chips = [(1 - x, y), (x, 1 - y), (1 - x, 1 - y)]

    def copy(a, k, block, to, own=False):
        px, py, pc = block
        rows = out_refs[a].at[4 * px + 2 * py + pc]
        return pltpu.make_async_remote_copy(
            src_ref=x_refs[a] if own else rows, dst_ref=rows,
            send_sem=send_sems.at[7 * a + k], recv_sem=recv_sems.at[7 * a + k],
            device_id=to, device_id_type=MESH)

    def local(a):
        return pltpu.make_async_copy(x_refs[a], out_refs[a].at[4 * x + 2 * y + c], local_sems.at[a])

    def first(a):
        return [copy(a, 0, me, sibling, own=True)] + [copy(a, 1 + j, me, (*chip, c), own=True)
                                                      for j, chip in enumerate(chips)]

    def start():
        for a in range(n):
            local(a).start()
            for cp in first(a):
                cp.start()

    def forward():
        for j, chip in enumerate(chips):
            for a in range(n):
                copy(a, 1 + j, (*chip, c), me).wait_recv()
                copy(a, 4 + j, (*chip, c), sibling).start()

    def finish():
        for a in range(n):
            copy(a, 0, sibling, me).wait_recv()
            for j, chip in enumerate(chips):
                copy(a, 4 + j, (*chip, 1 - c), me).wait_recv()
        for a in range(n):
            for cp in first(a) + [copy(a, 4 + j, (*chip, c), sibling) for j, chip in enumerate(chips)]:
                cp.wait_send()
            local(a).wait()

    return start, forward, finish


def _rs_sibling_exchange(packed, name):
    rider = _sibling_exchange_rider(packed)
    n = len(packed)

    def body(*refs):
        start, finish = rider.phases(refs[:n], refs[n:2 * n], refs[2 * n:])
        start()
        finish()

    return pl.pallas_call(
        body, name=name, in_specs=rider.in_specs, out_specs=rider.out_specs, out_shape=rider.out_shapes,
        scratch_shapes=rider.scratch,
    )(*packed)


def _sibling_exchange_rider(packed):
    n = len(packed)

    def phases(p_refs, r_refs, scratch):
        send_sems, recv_sems = scratch
        x, y, c = _place()

        def copies():
            return [pltpu.make_async_remote_copy(
                src_ref=p_refs[a].at[2 * j + (1 - c)], dst_ref=r_refs[a].at[j],
                send_sem=send_sems.at[4 * a + j], recv_sem=recv_sems.at[4 * a + j],
                device_id=(x, y, 1 - c), device_id_type=MESH) for a in range(n) for j in range(4)]

        def start():
            for cp in copies():
                cp.start()

        def finish():
            for cp in copies():
                cp.wait()

        return start, finish

    return _Rider(list(packed), [ANY] * n, [jax.ShapeDtypeStruct((4,) + p.shape[1:], p.dtype) for p in packed],
                  [ANY] * n, [pltpu.SemaphoreType.DMA((4 * n,)), pltpu.SemaphoreType.DMA((4 * n,))], phases)


def _rs_chip_sum(packed, from_sibling, c_idx, name):
    _, r, cc = packed.shape
    tr = _pick(r, (512, 256, 352, 128))

    def body(c_ref, a_ref, b_ref, o_ref):
        o_ref[...] = (a_ref[...].astype(F32) + b_ref[...].astype(F32)).astype(o_ref.dtype)

    return pl.pallas_call(
        body, name=name,
        grid_spec=pltpu.PrefetchScalarGridSpec(
            num_scalar_prefetch=1, grid=(4, r // tr),
            in_specs=[pl.BlockSpec((None, tr, cc), lambda j, i, c_ref: (2 * j + c_ref[0], i, 0)),
                      pl.BlockSpec((None, tr, cc), lambda j, i, c_ref: (j, i, 0))],
            out_specs=pl.BlockSpec((None, tr, cc), lambda j, i, c_ref: (j, i, 0))),
        out_shape=jax.ShapeDtypeStruct((4, r, cc), packed.dtype),
        compiler_params=_params("parallel", "parallel"),
    )(c_idx, packed, from_sibling)


def _chip_exchange_shapes(partial):
    return [jax.ShapeDtypeStruct((3,) + p.shape[1:], p.dtype) for p in partial]


def _chip_exchange_sems(n):
    return [pltpu.SemaphoreType.DMA((3 * n,)), pltpu.SemaphoreType.DMA((3 * n,))]


def _chip_exchange_phases(p_refs, r_refs, send_sems, recv_sems):
    x, y, c = _place()
    chips = [(1 - x, y), (x, 1 - y), (1 - x, 1 - y)]

    def copies():
        return [pltpu.make_async_remote_copy(
            src_ref=p_refs[a].at[2 * tx + ty], dst_ref=r_refs[a].at[k],
            send_sem=send_sems.at[3 * a + k], recv_sem=recv_sems.at[3 * a + k],
            device_id=(tx, ty, c), device_id_type=MESH)
            for a in range(len(p_refs)) for k, (tx, ty) in enumerate(chips)]

    def start():
        for cp in copies():
            cp.start()

    def finish():
        for cp in copies():
            cp.wait()

    return start, finish


def _adam_update(w, g, m, v):
    mn = ADAM_B1 * m + (1.0 - ADAM_B1) * g
    vn = ADAM_B2 * v + (1.0 - ADAM_B2) * (g * g)
    m_hat = mn / (1.0 - ADAM_B1 ** ADAM_STEP)
    v_hat = vn / (1.0 - ADAM_B2 ** ADAM_STEP)
    return -ADAM_LR * (m_hat / (jnp.sqrt(v_hat) + ADAM_EPS) + ADAM_WD * w), mn, vn


def _rs_final_adamw(partial, received, chip_idx, w, m, v, name):
    _, r, cc = partial.shape
    tr = _pick(r, (512, 256, 352, 128))

    def body(c_ref, a_ref, r_ref, w_ref, m_ref, v_ref, g_ref, d_ref, mo_ref, vo_ref):
        g = a_ref[...].astype(F32)
        for k in range(3):
            g = g + r_ref[k].astype(F32)
        g_ref[...] = g
        d_ref[...], mo_ref[...], vo_ref[...] = _adam_update(w_ref[...], g, m_ref[...], v_ref[...])

    row = pl.BlockSpec((tr, cc), lambda i, c_ref: (i, 0))
    return pl.pallas_call(
        body, name=name,
        grid_spec=pltpu.PrefetchScalarGridSpec(
            num_scalar_prefetch=1, grid=(r // tr,),
            in_specs=[pl.BlockSpec((None, tr, cc), lambda i, c_ref: (c_ref[0], i, 0)),
                      pl.BlockSpec((3, tr, cc), lambda i, c_ref: (0, i, 0)), row, row, row],
            out_specs=[row] * 4),
        out_shape=[jax.ShapeDtypeStruct((r, cc), F32)] * 4,
        compiler_params=_params("parallel"),
    )(chip_idx, partial, received, w.reshape(r, cc), m.reshape(r, cc), v.reshape(r, cc))


def _all_reduce_small(vals, name):
    rider = _all_reduce_rider(vals)

    def body(v_ref, o_ref, *scratch):
        start, finish = rider.phases([v_ref], [o_ref], scratch)
        start()
        finish()

    return pl.pallas_call(
        body, name=name, in_specs=rider.in_specs, out_specs=rider.out_specs[0], out_shape=rider.out_shapes[0],
        scratch_shapes=rider.scratch, compiler_params=pltpu.CompilerParams(vmem_limit_bytes=VMEM_LIMIT),
    )(vals)


def _all_reduce_rider(vals):
    r, cc = vals.shape

    def phases(ins, outs, scratch):
        (v_ref,), (o_ref,), (buf, send_sems, recv_sems) = ins, outs, scratch
        x, y, c = _place()
        me = 4 * x + 2 * y + c

        def copies():
            cps = []
            for k in range(1, N_DEV):
                kx, ky, kc = (k >> 2) & 1, (k >> 1) & 1, k & 1
                peer = (1 - x if kx else x, 1 - y if ky else y, 1 - c if kc else c)
                cps.append(pltpu.make_async_remote_copy(
                    src_ref=buf.at[0], dst_ref=buf.at[k], send_sem=send_sems.at[k - 1],
                    recv_sem=recv_sems.at[k - 1], device_id=peer, device_id_type=MESH))
            return cps

        def start():
            buf[0] = v_ref[...]
            for cp in copies():
                cp.start()

        def finish():
            for cp in copies():
                cp.wait()
            acc = buf[jnp.bitwise_xor(me, 0)]
            for dev in range(1, N_DEV):
                acc = acc + buf[jnp.bitwise_xor(me, dev)]
            o_ref[...] = acc

        return start, finish

    vm = pl.BlockSpec(memory_space=pltpu.VMEM)
    return _Rider([vals], [vm], [jax.ShapeDtypeStruct((r, cc), F32)], [vm],
                  [pltpu.VMEM((N_DEV, r, cc), F32), pltpu.SemaphoreType.DMA((7,)), pltpu.SemaphoreType.DMA((7,))],
                  phases)


def _chip_exchange_rider(partial):
    n = len(partial)
    return _Rider(list(partial), [ANY] * n, _chip_exchange_shapes(partial), [ANY] * n, _chip_exchange_sems(n),
                  lambda ins, outs, scratch: _chip_exchange_phases(ins, outs, *scratch))


def _lanes(flat):
    pad = (-flat.shape[0]) % (SUBLANES * LANES)
    return jnp.pad(flat, (0, pad)).reshape(-1, LANES)


def kernel(x, even_w_in, even_b_f, even_conv_w, even_w_out, odd_w_in, odd_v_ln_g, odd_v_ln_b, odd_w_s, odd_b_s, odd_w_out, mix_ln_g, mix_ln_b, ffn_w_in, ffn_w_out, ffn_ln_g, ffn_ln_b, loss_target, m_even_w_in, m_even_b_f, m_even_conv_w, m_even_w_out, m_odd_w_in, m_odd_v_ln_g, m_odd_v_ln_b, m_odd_w_s, m_odd_b_s, m_odd_w_out, m_mix_ln_g, m_mix_ln_b, m_ffn_w_in, m_ffn_w_out, m_ffn_ln_g, m_ffn_ln_b, v_even_w_in, v_even_b_f, v_even_conv_w, v_even_w_out, v_odd_w_in, v_odd_v_ln_g, v_odd_v_ln_b, v_odd_w_s, v_odd_b_s, v_odd_w_out, v_mix_ln_g, v_mix_ln_b, v_ffn_w_in, v_ffn_w_out, v_ffn_ln_g, v_ffn_ln_b):
    t, d = x.shape[1], x.shape[2]
    nh = even_b_f.shape[-1]
    w = even_conv_w.shape[-1] * N_DEV
    dh = w // nh
    scale = dh ** -0.5
    e_in = even_w_in.shape[-1] * N_DEV
    f2 = ffn_w_in.shape[-1] * N_DEV
    f = f2 // 2
    ng, pb = odd_w_s.shape[1], odd_w_s.shape[2]
    assert e_in == 6 * w + nh and nh <= SUBLANES and (6 * w) % LANES == 0 and d % N_DEV == 0
    mx, my, mc = _place()
    me = 4 * mx + 2 * my + mc

    big = [even_w_in[0], even_w_out[0], odd_w_in[0], odd_w_out[0],
           ffn_w_in[0], ffn_w_in[1], ffn_w_out[0], ffn_w_out[1]]
    g_in0, = _all_gather([big[0].astype(BF16)], "ag_even_w_in")
    w_in0 = g_in0.transpose(1, 0, 2).reshape(d, e_in)
    w_all0 = jnp.concatenate([w_in0[:, :3 * w], w_in0[:, 3 * w + nh:], w_in0[:, 3 * w:3 * w + nh],
                              jnp.zeros((d, LANES - nh), BF16)], axis=1)

    cs, vs = even_conv_w.shape[-1], odd_v_ln_g.shape[-1]
    small_mine = jnp.concatenate([
        lax.dynamic_update_slice(jnp.zeros((3, w), F32), even_conv_w[0], (0, me * cs)).reshape(-1),
        lax.dynamic_update_slice(jnp.zeros((d,), F32), odd_v_ln_g[0], (me * vs,)),
        lax.dynamic_update_slice(jnp.zeros((d,), F32), odd_v_ln_b[0], (me * vs,))])
    small_all = _all_reduce_small(_lanes(small_mine), "ag_small").reshape(-1)
    conv_w = small_all[:3 * w].reshape(3, w)
    vln_g = small_all[3 * w:3 * w + d]
    vln_b = small_all[3 * w + d:3 * w + 2 * d]

    bf_pad = jnp.pad(even_b_f[0], (0, LANES - nh)).reshape(1, LANES)
    chunk = jnp.arange(pb) // (pb // 2)
    ws_mask = (chunk[None, :] <= chunk[:, None])[None]
    wm = jnp.where(ws_mask, odd_w_s[0], 0.0).astype(BF16)
    bs_full = jnp.repeat(odd_b_s[0].T, d // ng, axis=1)

    x0 = x[0]
    tgt = loss_target[0]
    fcol = 6 * w // LANES
    x0b = x0.astype(BF16)
    p0 = _mm(x0b, w_all0, "nn", F32, "l0_in_proj")
    cgate = _fgate_fwd(p0, bf_pad, fcol, nh, "l0_fgate")
    assert dh + 7 <= LANES
    qa, ka, va = _attn_pack(p0, cgate, w, nh, scale, "l0_attn_pack")
    oa, g_out0, g_in1, g_out1, g_fi0, g_fi1, g_fo0, g_fo1 = _attn_fwd(
        qa, ka, va, dh, "l0_attn", gather=[s.astype(BF16) for s in big[1:]])
    w_out0, w_out1 = g_out0.reshape(2 * w, d), g_out1.reshape(d, d)
    w_fo0, w_fo1 = g_fo0.reshape(f, d), g_fo1.reshape(f, d)
    nb = N_DEV // 2
    w_fi0, w_fi1 = g_fi0.reshape(2, nb, d, -1), g_fi1.reshape(2, nb, d, -1)
    attn, = _attn_unpack(oa, w, nh, 1.0, None, 1.0, "l0_attn_unpack")
    yconv = _conv_fwd(p0, conv_w, w, 3, "l0_conv")
    mix = jnp.concatenate([attn, yconv], axis=1)
    m0, x1, x1b = _mm_ln(mix, w_out0, x0, mix_ln_g[0], mix_ln_b[0], "l0_out_proj_ln")
    h0, gu0 = _ffn_in_swiglu(x1b, w_fi0, "l0_ffn_in")
    f0, x2, x2b = _mm_ln(h0, w_fo0, x1, ffn_ln_g[0], ffn_ln_b[0], "l0_ffn_out_ln")

    uv = _mm_cols_fwd(x2b, g_in1, False, F32, "l1_in_proj")
    gated = _sgu_fwd(uv, vln_g, vln_b, wm, bs_full, "l1_sgu")
    m1, x3, x3b = _mm_ln(gated, w_out1, x2, mix_ln_g[1], mix_ln_b[1], "l1_out_proj_ln")
    h1, gu1 = _ffn_in_swiglu(x3b, w_fi1, "l1_ffn_in")
    dz4, dz4b, g_ffn_g1, g_ffn_b1, loss_part = _mm_ln(h1, w_fo1, x3, ffn_ln_g[1], ffn_ln_b[1],
                                                      "l1_ffn_out_ln_loss", target=tgt)
    gd_fo1 = _mm_blk_dw(h1, dz4b, BF16, "l1_ffn_out_dw").reshape(N_DEV, -1, d)
    dgu1 = _ffn_out_dx_swiglu(dz4b, w_fo1, gu1, "l1_ffn_out_dx").reshape(N_DEV, t, -1)
    gd_fi1 = _mm_cols_dw(x3b, dgu1, N_DEV, True, BF16, "l1_ffn_in_dw")
    dz3, dz3b, g_mix_g1, g_mix_b1 = _mm_cols_dx_ln_bwd(dgu1, g_fi1, True, x2, m1, mix_ln_g[1], dz4,
                                                       "l1_ffn_in_dx_ln_bwd")
    gd_out1 = _mm(gated, dz3b, "tn", BF16, "l1_out_proj_dw").reshape(N_DEV, -1, d)
    dgated = _mm(dz3b, w_out1, "nt", BF16, "l1_out_proj_dx")
    duv, g_wm, g_bs_t, g_vln_g, g_vln_b = _sgu_bwd(uv, vln_g, vln_b, wm, bs_full, dgated, "l1_sgu_bwd")
    gd_in1 = _mm_cols_dw(x2b, duv, N_DEV, False, BF16, "l1_in_proj_dw")

    dz2, dz2b, g_ffn_g0, g_ffn_b0 = _mm_cols_dx_ln_bwd(duv, g_in1, False, x1, f0, ffn_ln_g[0], dz3,
                                                       "l1_in_proj_dx_ln_bwd")
    gd_fo0 = _mm_blk_dw(h0, dz2b, BF16, "l0_ffn_out_dw").reshape(N_DEV, -1, d)
    dgu0 = _ffn_out_dx_swiglu(dz2b, w_fo0, gu0, "l0_ffn_out_dx").reshape(N_DEV, t, -1)
    gd_fi0 = _mm_cols_dw(x1b, dgu0, N_DEV, True, BF16, "l0_ffn_in_dw")
    dz1, dz1b, g_mix_g0, g_mix_b0 = _mm_cols_dx_ln_bwd(dgu0, g_fi0, True, x0, m0, mix_ln_g[0], dz2,
                                                       "l0_ffn_in_dx_ln_bwd")
    gd_out0 = _mm(mix, dz1b, "tn", BF16, "l0_out_proj_dw").reshape(N_DEV, -1, d)
    early_g = [gd_out0, gd_in1, gd_out1, gd_fi0, gd_fi1, gd_fo0, gd_fo1]
    dmix, *early_sib = _mm(dz1b, w_out0, "nt", F32, "l0_out_proj_dx", rider=_sibling_exchange_rider(early_g))
    d_b, d_c, d_h, g_conv = _conv_bwd(p0, conv_w, dmix, w, 3, "l0_conv_bwd")
    doa, qa2 = _attn_pack_bwd(dmix, oa, qa, w, nh, "l0_attn_pack_bwd")
    big_names = ["even_w_in", "even_w_out", "odd_w_in", "odd_w_out", "ffn_w_in0", "ffn_w_in1", "ffn_w_out0", "ffn_w_out1"]
    c_idx = mc.reshape(1).astype(jnp.int32)
    chip_idx = (2 * mx + my).reshape(1).astype(jnp.int32)
    early_partial = [_rs_chip_sum(g, s, c_idx, "rs_chip_sum_" + n)
                     for g, s, n in zip(early_g, early_sib, big_names[1:])]
    dqa, dka, dva, *early_received = _attn_bwd(qa2, ka, va, doa, "l0_attn_bwd", exchange=early_partial)
    dq, dcq = _attn_unpack(dqa, w, nh, scale, dh + 3, 1.0, "l0_attn_unpack_dq")
    dk, dck = _attn_unpack(dka, w, nh, 1.0, dh, -1.0, "l0_attn_unpack_dk")
    dv, = _attn_unpack(dva, w, nh, 1.0, None, 1.0, "l0_attn_unpack_dv")
    dzf, g_bf = _fgate_bwd(p0, bf_pad, dcq, dck, fcol, nh, "l0_fgate_bwd")
    dp0 = jnp.concatenate([dq, dk, dv, d_b.astype(BF16), d_c.astype(BF16), d_h.astype(BF16), dzf.astype(BF16)], axis=1)
    g_ws = jnp.where(ws_mask, g_wm, 0.0)
    g_bs = g_bs_t[:, :ng].T
    small_g = [g_bf[:nh], g_conv, g_vln_g, g_vln_b, g_ws, g_bs,
               jnp.stack([g_mix_g0, g_mix_g1]), jnp.stack([g_mix_b0, g_mix_b1]),
               jnp.stack([g_ffn_g0, g_ffn_g1]), jnp.stack([g_ffn_b0, g_ffn_b1])]
    small_rider = _all_reduce_rider(_lanes(jnp.concatenate([a.reshape(-1) for a in small_g])))
    g_all0, small_sum = _mm(x0b, dp0, "tn", F32, "l0_in_proj_dw", rider=small_rider)
    gd_in0 = jnp.concatenate([g_all0[:, :3 * w], g_all0[:, 6 * w:6 * w + nh], g_all0[:, 3 * w:6 * w]], axis=1)
    gd_in0 = gd_in0.reshape(d, N_DEV, -1).transpose(1, 0, 2).astype(BF16)

    big_m = [m_even_w_in[0], m_even_w_out[0], m_odd_w_in[0], m_odd_w_out[0],
             m_ffn_w_in[0], m_ffn_w_in[1], m_ffn_w_out[0], m_ffn_w_out[1]]
    big_v = [v_even_w_in[0], v_even_w_out[0], v_odd_w_in[0], v_odd_w_out[0],
             v_ffn_w_in[0], v_ffn_w_in[1], v_ffn_w_out[0], v_ffn_w_out[1]]
    late_sib = _rs_sibling_exchange([gd_in0], "rs_sibling_late")
    late_partial = [_rs_chip_sum(gd_in0, late_sib[0], c_idx, "rs_chip_sum_" + big_names[0])]
    partial = late_partial + early_partial
    grad_x, *late_received = _mm(dp0, w_all0, "nt", F32, "l0_in_proj_dx", rider=_chip_exchange_rider(late_partial),
                                 addend=(dz1, ALPHA))
    received = list(late_received) + list(early_received)
    upd = [_rs_final_adamw(p, r, chip_idx, wt, mt, vt, "rs_final_adamw_" + n)
           for p, r, wt, mt, vt, n in zip(partial, received, big, big_m, big_v, big_names)]
    big_out = {}
    for i, n in enumerate(["even_w_in", "even_w_out", "odd_w_in", "odd_w_out"]):
        big_out[n] = [o[None] for o in upd[i]]
    big_out["ffn_w_in"] = [jnp.stack([a, b]) for a, b in zip(upd[4], upd[5])]
    big_out["ffn_w_out"] = [jnp.stack([a, b]) for a, b in zip(upd[6], upd[7])]

    small_sum = small_sum.reshape(-1)
    outs_small = []
    off = 0
    for a in small_g:
        outs_small.append(small_sum[off:off + a.size].reshape(a.shape))
        off += a.size
    gr_bf, gr_conv, gr_vg, gr_vb, gr_ws, gr_bs, gr_mg, gr_mb, gr_fg, gr_fb = outs_small

    loss = lax.psum(loss_part, ("x", "y", "c"))

    grads = {
        "even_b_f": gr_bf[None],
        "even_conv_w": lax.dynamic_slice(gr_conv, (0, me * cs), (3, cs))[None],
        "odd_v_ln_g": lax.dynamic_slice(gr_vg, (me * vs,), (vs,))[None],
        "odd_v_ln_b": lax.dynamic_slice(gr_vb, (me * vs,), (vs,))[None],
        "odd_w_s": gr_ws[None], "odd_b_s": gr_bs[None],
        "mix_ln_g": gr_mg, "mix_ln_b": gr_mb, "ffn_ln_g": gr_fg, "ffn_ln_b": gr_fb,
    }
    weights = dict(even_w_in=even_w_in, even_b_f=even_b_f, even_conv_w=even_conv_w, even_w_out=even_w_out,
                   odd_w_in=odd_w_in, odd_v_ln_g=odd_v_ln_g, odd_v_ln_b=odd_v_ln_b, odd_w_s=odd_w_s,
                   odd_b_s=odd_b_s, odd_w_out=odd_w_out, mix_ln_g=mix_ln_g, mix_ln_b=mix_ln_b,
                   ffn_w_in=ffn_w_in, ffn_w_out=ffn_w_out, ffn_ln_g=ffn_ln_g, ffn_ln_b=ffn_ln_b)
    moms = dict(even_w_in=(m_even_w_in, v_even_w_in), even_b_f=(m_even_b_f, v_even_b_f),
                even_conv_w=(m_even_conv_w, v_even_conv_w), even_w_out=(m_even_w_out, v_even_w_out),
                odd_w_in=(m_odd_w_in, v_odd_w_in), odd_v_ln_g=(m_odd_v_ln_g, v_odd_v_ln_g),
                odd_v_ln_b=(m_odd_v_ln_b, v_odd_v_ln_b), odd_w_s=(m_odd_w_s, v_odd_w_s),
                odd_b_s=(m_odd_b_s, v_odd_b_s), odd_w_out=(m_odd_w_out, v_odd_w_out),
                mix_ln_g=(m_mix_ln_g, v_mix_ln_g), mix_ln_b=(m_mix_ln_b, v_mix_ln_b),
                ffn_w_in=(m_ffn_w_in, v_ffn_w_in), ffn_w_out=(m_ffn_w_out, v_ffn_w_out),
                ffn_ln_g=(m_ffn_ln_g, v_ffn_ln_g), ffn_ln_b=(m_ffn_ln_b, v_ffn_ln_b))
    names = list(weights)
    gout, deltas, new_m, new_v = [], [], [], []
    for n in names:
        if n in big_out:
            gr, dlt, mn, vn = big_out[n]
        else:
            gr = grads[n]
            dlt, mn, vn = _adamw(weights[n], gr, moms[n][0], moms[n][1], "adamw_" + n)
        gout.append(gr.reshape(weights[n].shape))
        deltas.append(dlt.reshape(weights[n].shape))
        new_m.append(mn.reshape(weights[n].shape))
        new_v.append(vn.reshape(weights[n].shape))
    return (loss, grad_x[None], *gout, *deltas, *new_m, *new_v)
```

```python
import functools
from typing import Callable, NamedTuple

import jax
import jax.numpy as jnp
from jax import lax
from jax.experimental import pallas as pl
from jax.experimental.pallas import tpu as pltpu

F32 = jnp.float32
BF16 = jnp.bfloat16
MESH = pl.DeviceIdType.MESH

DEPTH = 2
ALPHA = (2.0 * DEPTH) ** 0.25
LN_EPS = 1e-5
ADAM_LR = 0.001
ADAM_B1 = 0.9
ADAM_B2 = 0.999
ADAM_EPS = 1e-08
ADAM_WD = 0.01
ADAM_STEP = 10

N_DEV = 8
LANES = 128
SUBLANES = 8
VMEM_LIMIT = 48 * 1024 * 1024
NEG_BIG = -1e30
ROW_TILES = (512, 256, 128)


def _pick(n, cands):
    for c in cands:
        if c <= n and n % c == 0:
            return c
    return n


def _params(*sem):
    return pltpu.CompilerParams(dimension_semantics=sem, vmem_limit_bytes=VMEM_LIMIT)


NN = (((1,), (0,)), ((), ()))
NT = (((1,), (1,)), ((), ()))
TN = (((0,), (0,)), ((), ()))
M_TILES = (1024, 512, 1408, 256, 128)
N_TILES = (512, 640, 256, 128)
K_TILES = (4096, 2048, 1024, 512, 640, 1408, 256, 128)
K_WHOLE = 3328


class _Rider(NamedTuple):
    inputs: list
    in_specs: list
    out_shapes: list
    out_specs: list
    scratch: list
    phases: Callable


def _mm_core(name, grid, a, b, a_spec, b_spec, o_spec, o_shape, o_dtype, dims, tile, pieces=None, rider=None,
             addend=None):
    nred = grid[2]
    pieces = pieces or [(lambda r: r[...], lambda r: r[...])]
    ni = len(rider.inputs) if rider else 0
    no = len(rider.out_shapes) if rider else 0
    nacc = 0 if nred == 1 else 1
    add_arrays = [addend[0]] if addend else []

    def body(a_ref, b_ref, *rest):
        if addend:
            add_ref, rest = rest[0], rest[1:]
        finished = (lambda v: addend[1] * add_ref[...] + v) if addend else (lambda v: v)
        o_ref = rest[ni]
        if rider:
            start, finish = rider.phases(rest[:ni], rest[ni + 1:ni + 1 + no], rest[ni + 1 + no + nacc:])
            ids = [pl.program_id(ax) for ax in range(3)]
            first = functools.reduce(jnp.logical_and, [i == 0 for i in ids])
            last = functools.reduce(jnp.logical_and, [i == g - 1 for i, g in zip(ids, grid)])
            pl.when(first)(start)
        part = None
        for fa, fb in pieces:
            prod = lax.dot_general(fa(a_ref).astype(BF16), fb(b_ref).astype(BF16), dims, preferred_element_type=F32)
            part = prod if part is None else part + prod
        if nred == 1:
            o_ref[...] = finished(part).astype(o_ref.dtype)
        else:
            acc_ref = rest[ni + 1 + no]
            kk = pl.program_id(2)

            @pl.when(kk == 0)
            def _():
                acc_ref[...] = jnp.zeros_like(acc_ref)

            acc_ref[...] += part

            @pl.when(kk == nred - 1)
            def _():
                o_ref[...] = finished(acc_ref[...]).astype(o_ref.dtype)
        if rider:
            pl.when(last)(finish)

    out = pl.pallas_call(
        body, name=name, grid=grid,
        in_specs=[a_spec, b_spec] + ([o_spec] if addend else []) + (rider.in_specs if rider else []),
        out_specs=[o_spec] + (rider.out_specs if rider else []),
        out_shape=[jax.ShapeDtypeStruct(o_shape, o_dtype)] + (rider.out_shapes if rider else []),
        scratch_shapes=([] if nred == 1 else [pltpu.VMEM(tile, F32)]) + (rider.scratch if rider else []),
        compiler_params=_params(*(["arbitrary"] * 3 if rider else ["parallel", "parallel", "arbitrary"])),
    )(a, b, *add_arrays, *(rider.inputs if rider else []))
    return out if rider else out[0]


def _mm(a, b, mode, out_dtype, name, rider=None, addend=None):
    if mode == "nn":
        (m, k), (k2, n) = a.shape, b.shape
    elif mode == "nt":
        (m, k), (n, k2) = a.shape, b.shape
    else:
        (k, m), (k2, n) = a.shape, b.shape
    assert k == k2, (a.shape, b.shape, mode)
    tm, tn = _pick(m, M_TILES), _pick(n, N_TILES)
    tk = k if k <= K_WHOLE else _pick(k, K_TILES)
    if mode == "nn":
        a_spec = pl.BlockSpec((tm, tk), lambda i, j, kk: (i, kk))
        b_spec = pl.BlockSpec((tk, tn), lambda i, j, kk: (kk, j))
        dims = NN
    elif mode == "nt":
        a_spec = pl.BlockSpec((tm, tk), lambda i, j, kk: (i, kk))
        b_spec = pl.BlockSpec((tn, tk), lambda i, j, kk: (j, kk))
        dims = NT
    else:
        a_spec = pl.BlockSpec((tk, tm), lambda i, j, kk: (kk, i))
        b_spec = pl.BlockSpec((tk, tn), lambda i, j, kk: (kk, j))
        dims = TN
    return _mm_core(name, (m // tm, n // tn, k // tk), a, b, a_spec, b_spec,
                    pl.BlockSpec((tm, tn), lambda i, j, kk: (i, j)), (m, n), out_dtype, dims, (tm, tn), rider=rider,
                    addend=addend)


def _act_spec(blocked, rows, ns, row_ax, d_ax):
    if blocked:
        return pl.BlockSpec((None, rows, ns), lambda *g: (g[d_ax], g[row_ax], 0))
    return pl.BlockSpec((rows, ns), lambda *g: (g[row_ax], g[d_ax]))


def _mm_cols_fwd(a, g3, blocked, out_dtype, name):
    (t, k), (nd, k2, ns) = a.shape, g3.shape
    assert k == k2
    tm, tk = _pick(t, M_TILES), _pick(k, K_TILES)
    return _mm_core(name, (t // tm, nd, k // tk), a, g3,
                    pl.BlockSpec((tm, tk), lambda i, d, kk: (i, kk)),
                    pl.BlockSpec((None, tk, ns), lambda i, d, kk: (d, kk, 0)),
                    _act_spec(blocked, tm, ns, 0, 1), (nd, t, ns) if blocked else (t, nd * ns), out_dtype, NN, (tm, ns))


def _mm_cols_dx_ln_bwd(dy, g3, blocked, xa, xb, gam, dya, name):
    nd, k, ns = g3.shape
    t, d = xa.shape
    assert k == d
    tm = _pick(t, ROW_TILES)
    grp = nd if not blocked else (2 if nd % 2 == 0 else 1)
    nred = nd // grp

    def body(a_ref, b_ref, xa_ref, xb_ref, g_ref, dya_ref, dz_ref, dzb_ref, dg_ref, db_ref, *acc):
        i, kk = pl.program_id(0), pl.program_id(1)

        @pl.when(jnp.logical_and(i == 0, kk == 0))
        def _():
            dg_ref[...] = jnp.zeros_like(dg_ref)
            db_ref[...] = jnp.zeros_like(db_ref)

        if blocked:
            part = None
            for s in range(grp):
                prod = lax.dot_general(a_ref[s], b_ref[s], NT, preferred_element_type=F32)
                part = prod if part is None else part + prod
        else:
            whole_b = jnp.concatenate([b_ref[s] for s in range(nd)], axis=1)
            part = lax.dot_general(a_ref[...], whole_b, NT, preferred_element_type=F32)

        def ln_bwd(dyb):
            dy_t = ALPHA * dya_ref[...] + dyb
            z = ALPHA * xa_ref[...] + xb_ref[...]
            mu = jnp.mean(z, axis=-1, keepdims=True)
            zc = z - mu
            var = jnp.mean(zc * zc, axis=-1, keepdims=True)
            rstd = lax.rsqrt(var + LN_EPS)
            xhat = zc * rstd
            dxh = dy_t * g_ref[...]
            m1 = jnp.mean(dxh, axis=-1, keepdims=True)
            m2 = jnp.mean(dxh * xhat, axis=-1, keepdims=True)
            dz = rstd * (dxh - m1 - xhat * m2)
            dz_ref[...] = dz
            dzb_ref[...] = dz.astype(BF16)
            dg_ref[...] += jnp.sum(dy_t * xhat, axis=0, keepdims=True)
            db_ref[...] += jnp.sum(dy_t, axis=0, keepdims=True)

        if nred == 1:
            ln_bwd(part)
        else:
            acc_ref, = acc

            @pl.when(kk == 0)
            def _():
                acc_ref[...] = part

            @pl.when(kk > 0)
            def _():
                acc_ref[...] += part

            @pl.when(kk == nred - 1)
            def _():
                ln_bwd(acc_ref[...])

    row = pl.BlockSpec((tm, d), lambda i, kk: (i, 0))
    vec = pl.BlockSpec((1, d), lambda i, kk: (0, 0))
    if blocked:
        a_spec = pl.BlockSpec((grp, tm, ns), lambda i, kk: (kk, i, 0))
        b_spec = pl.BlockSpec((grp, k, ns), lambda i, kk: (kk, 0, 0))
    else:
        a_spec = pl.BlockSpec((tm, nd * ns), lambda i, kk: (i, 0))
        b_spec = pl.BlockSpec((nd, k, ns), lambda i, kk: (0, 0, 0))
    dz, dzb, dg, db = pl.pallas_call(
        body, name=name, grid=(t // tm, nred),
        in_specs=[a_spec, b_spec, row, row, vec, row], out_specs=[row, row, vec, vec],
        out_shape=[jax.ShapeDtypeStruct((t, d), F32), jax.ShapeDtypeStruct((t, d), BF16),
                   jax.ShapeDtypeStruct((1, d), F32), jax.ShapeDtypeStruct((1, d), F32)],
        scratch_shapes=[] if nred == 1 else [pltpu.VMEM((tm, d), F32)],
        compiler_params=_params("arbitrary", "arbitrary"),
    )(dy, g3, xa, xb, gam.reshape(1, d), dya)
    return dz, dzb, dg[0], db[0]


def _mm_cols_dw(a, dy, nd, blocked, out_dtype, name):
    t, k = a.shape
    ns = dy.shape[2] if blocked else dy.shape[1] // nd
    tmk, tk = _pick(k, M_TILES), _pick(t, K_TILES)
    return _mm_core(name, (nd, k // tmk, t // tk), a, dy,
                    pl.BlockSpec((tk, tmk), lambda d, j, kk: (kk, j)),
                    _act_spec(blocked, tk, ns, 2, 0),
                    pl.BlockSpec((None, tmk, ns), lambda d, j, kk: (d, j, 0)), (nd, k, ns), out_dtype, TN, (tmk, ns))


def _mm_blk_dw(h3, dz, out_dtype, name):
    (nb, t, ns), (_, n) = h3.shape, dz.shape
    tn, tk = _pick(n, (1024,) + N_TILES), _pick(t, K_TILES)
    return _mm_core(name, (nb, n // tn, t // tk), h3, dz,
                    pl.BlockSpec((None, tk, ns), lambda d, j, kk: (d, kk, 0)),
                    pl.BlockSpec((tk, tn), lambda d, j, kk: (kk, j)),
                    pl.BlockSpec((ns, tn), lambda d, j, kk: (d, j)), (nb * ns, n), out_dtype, TN, (ns, tn))


def _mm_ln(a, w, xa, g, b, name, target=None):
    blocked = a.ndim == 3
    t, d = xa.shape
    k = w.shape[0]
    tm = _pick(t, ROW_TILES)
    nb = a.shape[0] if blocked else 1
    ns = k // nb
    halves = [slice(0, tm // 2), slice(tm // 2, tm)] if tm % 32 == 0 else [slice(0, tm)]

    def body(a_ref, w_ref, xa_ref, g_ref, b_ref, *rest):
        def product(rows):
            if not blocked:
                return jnp.dot(a_ref[rows, :], w_ref[...], preferred_element_type=F32)
            acc = None
            for s in range(nb):
                prod = jnp.dot(a_ref[s, rows, :], w_ref[s * ns:(s + 1) * ns, :], preferred_element_type=F32)
                acc = prod if acc is None else acc + prod
            return acc

        if target is not None:
            t_ref, dz_ref, dzb_ref, dg_ref, db_ref, l_ref = rest

            @pl.when(pl.program_id(0) == 0)
            def _():
                l_ref[...] = jnp.zeros_like(l_ref)
                dg_ref[...] = jnp.zeros_like(dg_ref)
                db_ref[...] = jnp.zeros_like(db_ref)
        else:
            xb_ref, y_ref, yb_ref = rest
        for rows, xb in zip(halves, [product(rows) for rows in halves]):
            z = ALPHA * xa_ref[rows, :] + xb
            mu = jnp.mean(z, axis=-1, keepdims=True)
            zc = z - mu
            var = jnp.mean(zc * zc, axis=-1, keepdims=True)
            rstd = lax.rsqrt(var + LN_EPS)
            xhat = zc * rstd
            y = xhat * g_ref[...] + b_ref[...]
            if target is not None:
                e = y - t_ref[rows, :]
                l_ref[...] += 0.5 * jnp.sum(jnp.mean(e * e, axis=-1, keepdims=True))
                dy = e * (1.0 / d)
                dxh = dy * g_ref[...]
                m1 = jnp.mean(dxh, axis=-1, keepdims=True)
                m2 = jnp.mean(dxh * xhat, axis=-1, keepdims=True)
                dz = rstd * (dxh - m1 - xhat * m2)
                dz_ref[rows, :] = dz
                dzb_ref[rows, :] = dz.astype(BF16)
                dg_ref[...] += jnp.sum(dy * xhat, axis=0, keepdims=True)
                db_ref[...] += jnp.sum(dy, axis=0, keepdims=True)
            else:
                xb_ref[rows, :] = xb
                y_ref[rows, :] = y
                yb_ref[rows, :] = y.astype(BF16)

    row = pl.BlockSpec((tm, d), lambda i: (i, 0))
    vec = pl.BlockSpec((1, d), lambda i: (0, 0))
    a_spec = pl.BlockSpec((nb, tm, ns), lambda i: (0, i, 0)) if blocked else pl.BlockSpec((tm, k), lambda i: (i, 0))
    ins = [a, w, xa, g.reshape(1, d), b.reshape(1, d)]
    in_specs = [a_spec, pl.BlockSpec((k, d), lambda i: (0, 0)), row, vec, vec]
    if target is not None:
        dz, dzb, dg, db, l = pl.pallas_call(
            body, name=name, grid=(t // tm,), in_specs=in_specs + [row],
            out_specs=[row, row, vec, vec, pl.BlockSpec((1, LANES), lambda i: (0, 0))],
            out_shape=[jax.ShapeDtypeStruct((t, d), F32), jax.ShapeDtypeStruct((t, d), BF16),
                       jax.ShapeDtypeStruct((1, d), F32), jax.ShapeDtypeStruct((1, d), F32),
                       jax.ShapeDtypeStruct((1, LANES), F32)],
            compiler_params=_params("arbitrary"),
        )(*ins, target)
        return dz, dzb, dg[0], db[0], l[0, 0]
    return pl.pallas_call(
        body, name=name, grid=(t // tm,), in_specs=in_specs, out_specs=[row, row, row],
        out_shape=[jax.ShapeDtypeStruct((t, d), F32)] * 2 + [jax.ShapeDtypeStruct((t, d), BF16)],
        compiler_params=_params("parallel"),
    )(*ins)


def _ffn_in_swiglu(xb, g4, name):
    (t, k), (_, nb, _, ns) = xb.shape, g4.shape
    tm = _pick(t, M_TILES)

    def body(x_ref, w_ref, h_ref, gu_ref):
        xv = x_ref[...]
        gate = jnp.dot(xv, w_ref[0], preferred_element_type=F32)
        up = jnp.dot(xv, w_ref[1], preferred_element_type=F32)
        h_ref[...] = (gate * jax.nn.sigmoid(gate) * up).astype(BF16)
        gu_ref[0] = gate.astype(BF16)
        gu_ref[1] = up.astype(BF16)

    return pl.pallas_call(
        body, name=name, grid=(t // tm, nb),
        in_specs=[pl.BlockSpec((tm, k), lambda i, d: (i, 0)),
                  pl.BlockSpec((2, None, k, ns), lambda i, d: (0, d, 0, 0))],
        out_specs=[pl.BlockSpec((None, tm, ns), lambda i, d: (d, i, 0)),
                   pl.BlockSpec((2, None, tm, ns), lambda i, d: (0, d, i, 0))],
        out_shape=[jax.ShapeDtypeStruct((nb, t, ns), BF16), jax.ShapeDtypeStruct((2, nb, t, ns), BF16)],
        compiler_params=_params("parallel", "parallel"),
    )(xb, g4)


def _ffn_out_dx_swiglu(dz, w_out, gu4, name):
    (t, d), (_, nb, _, ns) = dz.shape, gu4.shape
    tm = _pick(t, M_TILES)

    def body(dz_ref, w_ref, gu_ref, o_ref):
        halves = [slice(0, tm // 2), slice(tm // 2, tm)] if tm % 16 == 0 else [slice(0, tm)]
        dhs = [lax.dot_general(dz_ref[rows, :].astype(BF16), w_ref[...], NT, preferred_element_type=F32)
               for rows in halves]
        for rows, dh in zip(halves, dhs):
            gate = gu_ref[0, rows, :].astype(F32)
            up = gu_ref[1, rows, :].astype(F32)
            sg = jax.nn.sigmoid(gate)
            silu = gate * sg
            o_ref[0, rows, :] = (dh * up * (sg + silu * (1.0 - sg))).astype(BF16)
            o_ref[1, rows, :] = (dh * silu).astype(BF16)

    blk = pl.BlockSpec((2, None, tm, ns), lambda i, j: (0, j, i, 0))
    return pl.pallas_call(
        body, name=name, grid=(t // tm, nb),
        in_specs=[pl.BlockSpec((tm, d), lambda i, j: (i, 0)), pl.BlockSpec((ns, d), lambda i, j: (j, 0)), blk],
        out_specs=blk,
        out_shape=jax.ShapeDtypeStruct((2, nb, t, ns), BF16),
        compiler_params=_params("parallel", "parallel"),
    )(dz, w_out, gu4)


def _tri_matmul(tri, x):
    x1 = x.astype(BF16)
    r1 = x - x1.astype(F32)
    x2 = r1.astype(BF16)
    x3 = (r1 - x2.astype(F32)).astype(BF16)
    dot = lambda v: jnp.dot(tri, v, preferred_element_type=F32)
    return dot(x1) + dot(x2) + dot(x3)


def _fgate_fwd(proj, bf_pad, fcol, n_heads, name):
    t = proj.shape[0]
    tb = _pick(t, ROW_TILES)

    def body(p_ref, b_ref, c_ref, carry):
        @pl.when(pl.program_id(0) == 0)
        def _():
            carry[...] = jnp.zeros_like(carry)

        z = p_ref[...] + b_ref[...]
        lf = jnp.minimum(z, 0.0) - jnp.log1p(jnp.exp(-jnp.abs(z)))
        lane = lax.broadcasted_iota(jnp.int32, (tb, LANES), 1)
        lf = jnp.where(lane < n_heads, lf, 0.0)
        r = lax.broadcasted_iota(jnp.int32, (tb, tb), 0)
        s = lax.broadcasted_iota(jnp.int32, (tb, tb), 1)
        tri = (s <= r).astype(BF16)
        c = _tri_matmul(tri, lf) + carry[...]
        c_ref[...] = c
        carry[...] = c[tb - 1:tb, :]

    return pl.pallas_call(
        body, name=name, grid=(t // tb,),
        in_specs=[pl.BlockSpec((tb, LANES), lambda i: (i, fcol)), pl.BlockSpec((1, LANES), lambda i: (0, 0))],
        out_specs=pl.BlockSpec((tb, LANES), lambda i: (i, 0)),
        out_shape=jax.ShapeDtypeStruct((t, LANES), F32),
        scratch_shapes=[pltpu.VMEM((1, LANES), F32)],
        compiler_params=_params("arbitrary"),
    )(proj, bf_pad)


def _fgate_bwd(proj, bf_pad, dcq, dck, fcol, n_heads, name):
    t = proj.shape[0]
    tb = _pick(t, ROW_TILES)
    nb = t // tb

    def body(p_ref, b_ref, dcq_ref, dck_ref, dz_ref, db_ref, carry):
        @pl.when(pl.program_id(0) == 0)
        def _():
            carry[...] = jnp.zeros_like(carry)
            db_ref[...] = jnp.zeros_like(db_ref)

        r = lax.broadcasted_iota(jnp.int32, (tb, tb), 0)
        s = lax.broadcasted_iota(jnp.int32, (tb, tb), 1)
        tri = (s >= r).astype(BF16)
        dlf = _tri_matmul(tri, dcq_ref[...] + dck_ref[...]) + carry[...]
        carry[...] = dlf[0:1, :]
        z = p_ref[...] + b_ref[...]
        lane = lax.broadcasted_iota(jnp.int32, (tb, LANES), 1)
        dz = jnp.where(lane < n_heads, dlf * jax.nn.sigmoid(-z), 0.0)
        dz_ref[...] = dz
        db_ref[...] += jnp.sum(dz, axis=0, keepdims=True)

    dz, db = pl.pallas_call(
        body, name=name, grid=(nb,),
        in_specs=[pl.BlockSpec((tb, LANES), lambda i: (nb - 1 - i, fcol)),
                  pl.BlockSpec((1, LANES), lambda i: (0, 0)),
                  pl.BlockSpec((tb, LANES), lambda i: (nb - 1 - i, 0)),
                  pl.BlockSpec((tb, LANES), lambda i: (nb - 1 - i, 0))],
        out_specs=[pl.BlockSpec((tb, LANES), lambda i: (nb - 1 - i, 0)),
                   pl.BlockSpec((1, LANES), lambda i: (0, 0))],
        out_shape=[jax.ShapeDtypeStruct((t, LANES), F32), jax.ShapeDtypeStruct((1, LANES), F32)],
        scratch_shapes=[pltpu.VMEM((1, LANES), F32)],
        compiler_params=_params("arbitrary"),
    )(proj, bf_pad, dcq, dck)
    return dz, db[0]


def _split3(x):
    hi = x.astype(BF16)
    r = x - hi.astype(F32)
    mid = r.astype(BF16)
    return hi, mid, (r - mid.astype(F32)).astype(BF16)


def _attn_fwd(qa, ka, va, dh, name, gather=()):
    nh, t, da = qa.shape
    tq = _pick(t, ROW_TILES)
    hb = 2 if nh % 2 == 0 else 1
    heads = range(hb)
    n = len(gather)
    steps = (nh // hb, t // tq)

    def body(q_ref, k_ref, v_ref, *rest):
        x_refs, o_ref, g_refs = rest[:n], rest[n], rest[n + 1:2 * n + 1]
        m_s, acc_s, s_a, s_b = rest[2 * n + 1:2 * n + 5]
        qi = pl.program_id(1)
        if n:
            start, forward, finish = _gather_phases(x_refs, g_refs, *rest[2 * n + 5:])
            at = lambda hh, qq: jnp.logical_and(pl.program_id(0) == hh, qi == qq)
            pl.when(at(0, 0))(start)
            pl.when(at(steps[0] // 2, 0))(forward)
        m_s[...] = jnp.full(m_s.shape, NEG_BIG, F32)
        acc_s[...] = jnp.zeros_like(acc_s)

        def scores(s_ref, j):
            off = pl.multiple_of(j * tq, tq)
            for g in heads:
                s_ref[g] = lax.dot_general(q_ref[g], k_ref[g, pl.ds(off, tq), :], NT, preferred_element_type=F32)

        def absorb(s_ref, j, diagonal):
            off = pl.multiple_of(j * tq, tq)
            s = [s_ref[g] for g in heads]
            if diagonal:
                row = lax.broadcasted_iota(jnp.int32, (tq, tq), 0)
                col = lax.broadcasted_iota(jnp.int32, (tq, tq), 1)
                s = [jnp.where(col > row, NEG_BIG, sg) for sg in s]
            m_prev = [m_s[g] for g in heads]
            m_new = [jnp.maximum(m_prev[g], jnp.max(s[g], axis=1, keepdims=True)) for g in heads]
            p = [jnp.exp(s[g] - m_new[g]).astype(BF16) for g in heads]
            pv = [jnp.dot(p[g], v_ref[g, pl.ds(off, tq), :], preferred_element_type=F32) for g in heads]
            for g in heads:
                acc_s[g] = jnp.exp(m_prev[g] - m_new[g]) * acc_s[g] + pv[g]
                m_s[g] = m_new[g]

        def two_blocks(r, carry):
            scores(s_b, 2 * r + 1)
            absorb(s_a, 2 * r, False)
            scores(s_a, 2 * r + 2)
            absorb(s_b, 2 * r + 1, False)
            return carry

        scores(s_a, 0)
        rounds = qi // 2
        lax.fori_loop(0, rounds, two_blocks, 0)

        @pl.when(qi % 2 == 0)
        def _():
            absorb(s_a, qi, True)

        @pl.when(qi % 2 == 1)
        def _():
            scores(s_b, qi)
            absorb(s_a, qi - 1, False)
            absorb(s_b, qi, True)

        lane = lax.broadcasted_iota(jnp.int32, (tq, da), 1)
        for g in heads:
            acc = acc_s[g]
            l = jnp.sum(jnp.where(lane == dh, acc, 0.0), axis=1, keepdims=True)
            o_ref[g] = jnp.where(lane == dh, m_s[g] + jnp.log(l), acc / l)
        if n:
            pl.when(at(steps[0] - 1, steps[1] - 1))(finish)

    full = pl.BlockSpec((hb, t, da), lambda h, qi: (h, 0, 0))
    blk = pl.BlockSpec((hb, tq, da), lambda h, qi: (h, qi, 0))
    return pl.pallas_call(
        body, name=name, grid=steps,
        in_specs=[blk, full, full] + [ANY] * n, out_specs=[blk] + [ANY] * n,
        out_shape=[jax.ShapeDtypeStruct((nh, t, da), F32)] + _gather_shapes(gather),
        scratch_shapes=[pltpu.VMEM((hb, tq, 1), F32), pltpu.VMEM((hb, tq, da), F32),
                        pltpu.VMEM((hb, tq, tq), F32), pltpu.VMEM((hb, tq, tq), F32)] + (_gather_sems(n) if n else []),
        compiler_params=_params("arbitrary", "arbitrary"),
    )(qa, ka, va, *gather)


def _attn_bwd(qa, ka, va, doa, name, exchange=()):
    nh, t, da = qa.shape
    tq = _pick(t, ROW_TILES)
    nq = t // tq
    n = len(exchange)

    def body(q_ref, do_ref, k_ref, v_ref, *rest):
        p_refs, (dq_ref, dk_ref, dv_ref), r_refs = rest[:n], rest[n:n + 3], rest[n + 3:2 * n + 3]
        kj = pl.program_id(1)
        if n:
            start, finish = _chip_exchange_phases(p_refs, r_refs, *rest[2 * n + 3:])
            pl.when(jnp.logical_and(pl.program_id(0) == 0, kj == 0))(start)

        @pl.when(kj == 0)
        def _():
            dq_ref[...] = jnp.zeros_like(dq_ref)

        dk_ref[...] = jnp.zeros_like(dk_ref)
        dv_ref[...] = jnp.zeros_like(dv_ref)
        kb = k_ref[...]
        vb = v_ref[...]

        def step(i, diagonal, blocks=1):
            off = pl.multiple_of(i * tq, tq)
            rows = blocks * tq
            qb = q_ref[pl.ds(off, rows), :]
            dob = do_ref[pl.ds(off, rows), :]
            st = lax.dot_general(kb, qb, NT, preferred_element_type=F32)
            if diagonal:
                row = lax.broadcasted_iota(jnp.int32, (tq, tq), 0)
                col = lax.broadcasted_iota(jnp.int32, (tq, tq), 1)
                st = jnp.where(row > col, NEG_BIG, st)
            pt = jnp.exp(st)
            dst = (pt * lax.dot_general(vb, dob, NT, preferred_element_type=F32)).astype(BF16)
            dv_ref[...] += jnp.dot(pt.astype(BF16), dob, preferred_element_type=F32)
            dk_ref[...] += jnp.dot(dst, qb, preferred_element_type=F32)
            dq_ref[pl.ds(off, rows), :] += lax.dot_general(dst, kb, TN, preferred_element_type=F32)

        step(kj, True)
        count = nq - 1 - kj
        one, two = count % 2, (count // 2) % 2

        @pl.when(one == 1)
        def _():
            step(kj + 1, False)

        @pl.when(two == 1)
        def _():
            step(kj + 1 + one, False, blocks=2)

        def loop(r, carry):
            step(kj + 1 + one + 2 * two + 4 * r, False, blocks=4)
            return carry

        lax.fori_loop(0, count // 4, loop, 0)
        if n:
            pl.when(jnp.logical_and(pl.program_id(0) == nh - 1, kj == nq - 1))(finish)

    full = pl.BlockSpec((None, t, da), lambda h, j: (h, 0, 0))
    blk = pl.BlockSpec((None, tq, da), lambda h, j: (h, j, 0))
    return pl.pallas_call(
        body, name=name, grid=(nh, nq),
        in_specs=[full, full, blk, blk] + [ANY] * n, out_specs=[full, blk, blk] + [ANY] * n,
        out_shape=[jax.ShapeDtypeStruct((nh, t, da), F32)] * 3 + _chip_exchange_shapes(exchange),
        scratch_shapes=_chip_exchange_sems(n) if n else [],
        compiler_params=_params("arbitrary", "arbitrary"),
    )(qa, doa, ka, va, *exchange)


def _head_group(dh, h):
    g = h // (LANES // dh)
    return slice(g * LANES, (g + 1) * LANES)


def _head_select(dh, h, to_heads):
    r = lax.broadcasted_iota(jnp.int32, (LANES, LANES), 0)
    c = lax.broadcasted_iota(jnp.int32, (LANES, LANES), 1)
    nat, col = (r, c) if to_heads else (c, r)
    return jnp.logical_and(nat == col + (h % (LANES // dh)) * dh, col < dh).astype(BF16)


def _column(x, lane, j):
    return jnp.sum(jnp.where(lane == j, x, 0.0), axis=1, keepdims=True)


def _bias_columns(lane, first, value):
    out = jnp.zeros(lane.shape, F32)
    for j, term in enumerate(_split3(value)):
        out = out + jnp.where(lane == first + j, -term.astype(F32), 0.0)
    return out


def _attn_pack(proj, cgate, w, nh, scale, name):
    t = proj.shape[0]
    dh = w // nh
    tb = _pick(t, ROW_TILES)

    def body(q_ref, k_ref, v_ref, c_ref, qa_ref, ka_ref, va_ref):
        lane = lax.broadcasted_iota(jnp.int32, (tb, LANES), 1)
        ones_qv = jnp.where(jnp.logical_and(lane >= dh, lane < dh + 3), 1.0, 0.0)
        ones_k = jnp.where(jnp.logical_and(lane >= dh + 3, lane < dh + 7), 1.0, 0.0)
        qb = (q_ref[...] * scale).astype(BF16)
        kb = k_ref[...].astype(BF16)
        vb = v_ref[...].astype(BF16)
        cblk = c_ref[...]
        for h in range(nh):
            sel, grp = _head_select(dh, h, True), _head_group(dh, h)
            qa_ref[h] = (jnp.dot(qb[:, grp], sel, preferred_element_type=F32) + ones_qv).astype(BF16)
            va_ref[h] = (jnp.dot(vb[:, grp], sel, preferred_element_type=F32) + ones_qv).astype(BF16)
            bias = _bias_columns(lane, dh, _column(cblk, lane, h))
            ka_ref[h] = (jnp.dot(kb[:, grp], sel, preferred_element_type=F32) + bias + ones_k).astype(BF16)

    col = lambda j: pl.BlockSpec((tb, w), lambda i: (i, j))
    out = pl.BlockSpec((nh, tb, LANES), lambda i: (0, i, 0))
    return pl.pallas_call(
        body, name=name, grid=(t // tb,),
        in_specs=[col(0), col(1), col(2), pl.BlockSpec((tb, LANES), lambda i: (i, 0))],
        out_specs=[out, out, out],
        out_shape=[jax.ShapeDtypeStruct((nh, t, LANES), BF16)] * 3,
        compiler_params=_params("parallel"),
    )(proj, proj, proj, cgate)


def _attn_pack_bwd(dmix, oa, qa, w, nh, name):
    t = dmix.shape[0]
    dh = w // nh
    tb = _pick(t, ROW_TILES)

    def body(d_ref, oa_ref, qa_ref, doa_ref, qa2_ref):
        lane = lax.broadcasted_iota(jnp.int32, (tb, LANES), 1)
        db = d_ref[...].astype(BF16)
        for h in range(nh):
            do_h = jnp.dot(db[:, _head_group(dh, h)], _head_select(dh, h, True), preferred_element_type=F32)
            o_h = oa_ref[h]
            delta = jnp.sum(jnp.where(lane < dh, do_h * o_h, 0.0), axis=1, keepdims=True)
            doa_ref[h] = (do_h + _bias_columns(lane, dh, delta)).astype(BF16)
            qa2_ref[h] = (qa_ref[h].astype(F32) + _bias_columns(lane, dh + 4, _column(o_h, lane, dh))).astype(BF16)

    blk = pl.BlockSpec((nh, tb, LANES), lambda i: (0, i, 0))
    return pl.pallas_call(
        body, name=name, grid=(t // tb,),
        in_specs=[pl.BlockSpec((tb, w), lambda i: (i, 0)), blk, blk], out_specs=[blk, blk],
        out_shape=[jax.ShapeDtypeStruct((nh, t, LANES), BF16)] * 2,
        compiler_params=_params("parallel"),
    )(dmix, oa, qa)


def _attn_unpack(xa, w, nh, mult, sum_col, sum_sign, name):
    t = xa.shape[1]
    dh = w // nh
    tb = _pick(t, ROW_TILES)

    def body(x_ref, o_ref, *rest):
        lane = lax.broadcasted_iota(jnp.int32, (tb, LANES), 1)
        per = LANES // dh
        cols = jnp.zeros((tb, LANES), F32)
        for h0 in range(0, nh, per):
            acc = jnp.zeros((tb, LANES), F32)
            for h in range(h0, h0 + per):
                xh = x_ref[h]
                acc = acc + jnp.dot((xh * mult).astype(BF16), _head_select(dh, h, False), preferred_element_type=F32)
                if sum_col is not None:
                    cols = cols + jnp.where(lane == h, sum_sign * _column(xh, lane, sum_col), 0.0)
            o_ref[:, _head_group(dh, h0)] = acc.astype(BF16)
        if sum_col is not None:
            rest[0][...] = cols

    nat = pl.BlockSpec((tb, w), lambda i: (i, 0))
    lanes = pl.BlockSpec((tb, LANES), lambda i: (i, 0))
    return pl.pallas_call(
        body, name=name, grid=(t // tb,),
        in_specs=[pl.BlockSpec((nh, tb, LANES), lambda i: (0, i, 0))],
        out_specs=[nat, lanes] if sum_col is not None else [nat],
        out_shape=[jax.ShapeDtypeStruct((t, w), BF16)] + ([jax.ShapeDtypeStruct((t, LANES), F32)]
                                                            if sum_col is not None else []),
        compiler_params=_params("parallel"),
    )(xa)


def _conv_fwd(proj, cw, w, bcol, name):
    t = proj.shape[0]
    tb = _pick(t, ROW_TILES)
    hb = tb // SUBLANES

    def body(b_ref, c_ref, h_ref, cp_ref, hp_ref, w_ref, y_ref):
        i = pl.program_id(0)
        zp = jnp.where(i > 0, cp_ref[...] * hp_ref[...], 0.0)
        zext = jnp.concatenate([zp, c_ref[...] * h_ref[...]], axis=0)
        z1 = pltpu.roll(zext, 1, 0)[SUBLANES:]
        z2 = pltpu.roll(zext, 2, 0)[SUBLANES:]
        y = w_ref[2:3, :] * zext[SUBLANES:] + w_ref[1:2, :] * z1 + w_ref[0:1, :] * z2
        y_ref[...] = (b_ref[...] * y).astype(BF16)

    cur = lambda j: pl.BlockSpec((tb, w), lambda i: (i, bcol + j))
    prev = lambda j: pl.BlockSpec((SUBLANES, w), lambda i: (jnp.maximum(i * hb - 1, 0), bcol + j))
    return pl.pallas_call(
        body, name=name, grid=(t // tb,),
        in_specs=[cur(0), cur(1), cur(2), prev(1), prev(2), pl.BlockSpec(cw.shape, lambda i: (0, 0))],
        out_specs=pl.BlockSpec((tb, w), lambda i: (i, 0)),
        out_shape=jax.ShapeDtypeStruct((t, w), BF16), compiler_params=_params("parallel"),
    )(proj, proj, proj, proj, proj, cw)


def _conv_bwd(proj, cw, dmix, w, bcol, name):
    t = proj.shape[0]
    tb = _pick(t, ROW_TILES)
    hb = tb // SUBLANES
    nb = t // tb
    n_ext = tb + SUBLANES

    def body(b_ref, c_ref, h_ref, cp_ref, hp_ref, bn_ref, d_ref, dn_ref, w_ref, db_ref, dc_ref, dh_ref, dw_ref):
        i = pl.program_id(0)
        c = c_ref[...]
        hh = h_ref[...]
        zp = jnp.where(i > 0, cp_ref[...] * hp_ref[...], 0.0)
        zext = jnp.concatenate([zp, c * hh], axis=0)
        z0 = zext[SUBLANES:]
        z1 = pltpu.roll(zext, 1, 0)[SUBLANES:]
        z2 = pltpu.roll(zext, 2, 0)[SUBLANES:]
        y = w_ref[2:3, :] * z0 + w_ref[1:2, :] * z1 + w_ref[0:1, :] * z2
        d = d_ref[...]
        db_ref[...] = d * y
        dy = d * b_ref[...]
        dyn = jnp.where(i < nb - 1, dn_ref[...] * bn_ref[...], 0.0)
        dext = jnp.concatenate([dy, dyn], axis=0)
        dy1 = pltpu.roll(dext, n_ext - 1, 0)[:tb]
        dy2 = pltpu.roll(dext, n_ext - 2, 0)[:tb]
        dz = w_ref[2:3, :] * dy + w_ref[1:2, :] * dy1 + w_ref[0:1, :] * dy2
        dc_ref[...] = dz * hh
        dh_ref[...] = dz * c

        @pl.when(i == 0)
        def _():
            dw_ref[...] = jnp.zeros_like(dw_ref)

        dw_ref[0:1, :] += jnp.sum(dy * z2, axis=0, keepdims=True)
        dw_ref[1:2, :] += jnp.sum(dy * z1, axis=0, keepdims=True)
        dw_ref[2:3, :] += jnp.sum(dy * z0, axis=0, keepdims=True)

    cur = lambda j: pl.BlockSpec((tb, w), lambda i: (i, bcol + j))
    prev = lambda j: pl.BlockSpec((SUBLANES, w), lambda i: (jnp.maximum(i * hb - 1, 0), bcol + j))
    nxt = lambda col: pl.BlockSpec((SUBLANES, w), lambda i: (jnp.minimum((i + 1) * hb, nb * hb - 1), col))
    out = pl.BlockSpec((tb, w), lambda i: (i, 0))
    return pl.pallas_call(
        body, name=name, grid=(nb,),
        in_specs=[cur(0), cur(1), cur(2), prev(1), prev(2), nxt(bcol),
                  pl.BlockSpec((tb, w), lambda i: (i, 1)), nxt(1), pl.BlockSpec(cw.shape, lambda i: (0, 0))],
        out_specs=[out, out, out, pl.BlockSpec(cw.shape, lambda i: (0, 0))],
        out_shape=[jax.ShapeDtypeStruct((t, w), F32)] * 3 + [jax.ShapeDtypeStruct(cw.shape, F32)],
        compiler_params=_params("arbitrary"),
    )(proj, proj, proj, proj, proj, proj, dmix, dmix, cw)


SQRT_HALF = 0.7071067811865476
INV_SQRT_2PI = 0.3989422804014327


def _gelu(x):
    return 0.5 * x * (1.0 + lax.erf(x * SQRT_HALF))


def _gelu_grad(x):
    return 0.5 * (1.0 + lax.erf(x * SQRT_HALF)) + x * (INV_SQRT_2PI * jnp.exp(-0.5 * x * x))


def _sgu_fwd(uv, ln_g, ln_b, wm, bs_full, name):
    t, d2 = uv.shape
    d = d2 // 2
    ng, pb, _ = wm.shape
    gd = d // ng
    tb = _pick(t, ROW_TILES[1:] or ROW_TILES)
    assert tb % pb == 0

    def body(uv_ref, g_ref, b_ref, w_ref, bs_ref, o_ref):
        u = _gelu(uv_ref[:, :d])
        v = _gelu(uv_ref[:, d:])
        mu = jnp.mean(v, axis=-1, keepdims=True)
        vc = v - mu
        var = jnp.mean(vc * vc, axis=-1, keepdims=True)
        vn = (vc * lax.rsqrt(var + LN_EPS) * g_ref[...] + b_ref[...]).astype(BF16)
        for r in range(tb // pb):
            rows = slice(r * pb, (r + 1) * pb)
            for gi in range(ng):
                cols = slice(gi * gd, (gi + 1) * gd)
                s = jnp.dot(w_ref[gi], vn[rows, cols], preferred_element_type=F32) + bs_ref[:, cols]
                o_ref[rows, cols] = (u[rows, cols] * s).astype(BF16)

    vec = pl.BlockSpec((1, d), lambda i: (0, 0))
    return pl.pallas_call(
        body, name=name, grid=(t // tb,),
        in_specs=[pl.BlockSpec((tb, d2), lambda i: (i, 0)), vec, vec,
                  pl.BlockSpec(wm.shape, lambda i: (0, 0, 0)), pl.BlockSpec((pb, d), lambda i: (0, 0))],
        out_specs=pl.BlockSpec((tb, d), lambda i: (i, 0)),
        out_shape=jax.ShapeDtypeStruct((t, d), BF16), compiler_params=_params("parallel"),
    )(uv, ln_g.reshape(1, d), ln_b.reshape(1, d), wm, bs_full)


def _sgu_bwd(uv, ln_g, ln_b, wm, bs_full, dgated, name):
    t, d2 = uv.shape
    d = d2 // 2
    ng, pb, _ = wm.shape
    gd = d // ng
    tb = _pick(t, ROW_TILES[1:] or ROW_TILES)
    nb = t // tb

    def body(uv_ref, g_ref, b_ref, w_ref, bs_ref, dg_ref, o_ref, dw_ref, dbs_ref, dlg_ref, dlb_ref,
             du_s, dvn_s, dbs_s):
        i = pl.program_id(0)

        @pl.when(i == 0)
        def _():
            dw_ref[...] = jnp.zeros_like(dw_ref)
            dbs_s[...] = jnp.zeros_like(dbs_s)
            dlg_ref[...] = jnp.zeros_like(dlg_ref)
            dlb_ref[...] = jnp.zeros_like(dlb_ref)

        upre = uv_ref[:, :d]
        vpre = uv_ref[:, d:]
        u = _gelu(upre)
        v = _gelu(vpre)
        mu = jnp.mean(v, axis=-1, keepdims=True)
        vc = v - mu
        var = jnp.mean(vc * vc, axis=-1, keepdims=True)
        rstd = lax.rsqrt(var + LN_EPS)
        xhat = vc * rstd
        vn = (xhat * g_ref[...] + b_ref[...]).astype(BF16)
        dgt = dg_ref[...].astype(F32)
        for r in range(tb // pb):
            rows = slice(r * pb, (r + 1) * pb)
            for gi in range(ng):
                cols = slice(gi * gd, (gi + 1) * gd)
                vblk = vn[rows, cols]
                s = jnp.dot(w_ref[gi], vblk, preferred_element_type=F32) + bs_ref[:, cols]
                dblk = dgt[rows, cols]
                du_s[rows, cols] = dblk * s
                ds = dblk * u[rows, cols]
                dsb = ds.astype(BF16)
                dvn_s[rows, cols] = lax.dot_general(w_ref[gi], dsb, (((0,), (0,)), ((), ())),
                                                    preferred_element_type=F32)
                dw_ref[gi] += lax.dot_general(dsb, vblk, (((1,), (1,)), ((), ())), preferred_element_type=F32)
                dbs_s[:, cols] += ds
        dvn = dvn_s[...]
        dlg_ref[...] += jnp.sum(dvn * xhat, axis=0, keepdims=True)
        dlb_ref[...] += jnp.sum(dvn, axis=0, keepdims=True)
        dxh = dvn * g_ref[...]
        m1 = jnp.mean(dxh, axis=-1, keepdims=True)
        m2 = jnp.mean(dxh * xhat, axis=-1, keepdims=True)
        dv = rstd * (dxh - m1 - xhat * m2)
        o_ref[:, :d] = (du_s[...] * _gelu_grad(upre)).astype(BF16)
        o_ref[:, d:] = (dv * _gelu_grad(vpre)).astype(BF16)

        @pl.when(i == nb - 1)
        def _():
            lane = lax.broadcasted_iota(jnp.int32, (pb, LANES), 1)
            acc = jnp.zeros((pb, LANES), F32)
            for gi in range(ng):
                col = jnp.sum(dbs_s[:, gi * gd:(gi + 1) * gd], axis=1, keepdims=True)
                acc = acc + jnp.where(lane == gi, col, 0.0)
            dbs_ref[...] = acc

    vec = pl.BlockSpec((1, d), lambda i: (0, 0))
    duv, dw, dbs, dlg, dlb = pl.pallas_call(
        body, name=name, grid=(nb,),
        in_specs=[pl.BlockSpec((tb, d2), lambda i: (i, 0)), vec, vec,
                  pl.BlockSpec(wm.shape, lambda i: (0, 0, 0)), pl.BlockSpec((pb, d), lambda i: (0, 0)),
                  pl.BlockSpec((tb, d), lambda i: (i, 0))],
        out_specs=[pl.BlockSpec((tb, d2), lambda i: (i, 0)), pl.BlockSpec(wm.shape, lambda i: (0, 0, 0)),
                   pl.BlockSpec((pb, LANES), lambda i: (0, 0)), vec, vec],
        out_shape=[jax.ShapeDtypeStruct((t, d2), BF16), jax.ShapeDtypeStruct(wm.shape, F32),
                   jax.ShapeDtypeStruct((pb, LANES), F32), jax.ShapeDtypeStruct((1, d), F32),
                   jax.ShapeDtypeStruct((1, d), F32)],
        scratch_shapes=[pltpu.VMEM((tb, d), F32), pltpu.VMEM((tb, d), F32), pltpu.VMEM((pb, d), F32)],
        compiler_params=_params("arbitrary"),
    )(uv, ln_g.reshape(1, d), ln_b.reshape(1, d), wm, bs_full, dgated)
    return duv, dw, dbs, dlg[0], dlb[0]


def _adamw(w, g, m, v, name):
    shape = w.shape
    cols = shape[-1]
    rows = w.size // cols
    tr = _pick(rows, (512, 256, 352, 128, 64, 32, 16, 8))

    def body(w_ref, g_ref, m_ref, v_ref, d_ref, mo_ref, vo_ref):
        d_ref[...], mo_ref[...], vo_ref[...] = _adam_update(w_ref[...], g_ref[...], m_ref[...], v_ref[...])

    spec = pl.BlockSpec((tr, cols), lambda i: (i, 0))
    outs = pl.pallas_call(
        body, name=name, grid=(rows // tr,),
        in_specs=[spec] * 4, out_specs=[spec] * 3,
        out_shape=[jax.ShapeDtypeStruct((rows, cols), F32)] * 3,
        compiler_params=_params("parallel"),
    )(*[a.reshape(rows, cols) for a in (w, g, m, v)])
    return [o.reshape(shape) for o in outs]


ANY = pl.BlockSpec(memory_space=pl.ANY)


def _place():
    return lax.axis_index("x"), lax.axis_index("y"), lax.axis_index("c")


def _all_gather(shards, name):
    n = len(shards)

    def body(*refs):
        start, forward, finish = _gather_phases(refs[:n], refs[n:2 * n], *refs[2 * n:])
        start()
        forward()
        finish()

    return pl.pallas_call(
        body, name=name, in_specs=[ANY] * n, out_specs=[ANY] * n,
        out_shape=_gather_shapes(shards), scratch_shapes=_gather_sems(n),
    )(*shards)


def _gather_shapes(shards):
    return [jax.ShapeDtypeStruct((N_DEV,) + s.shape, s.dtype) for s in shards]


def _gather_sems(n):
    return [pltpu.SemaphoreType.DMA((7 * n,)), pltpu.SemaphoreType.DMA((7 * n,)), pltpu.SemaphoreType.DMA((n,))]


def _gather_phases(x_refs, out_refs, send_sems, recv_sems, local_sems):
    n = len(x_refs)
    x, y, c = _place()
    me, sibling = (x, y, c), (x, y, 1 - c)
    chips = [(1 - x, y), (x, 1 - y), (1 - x, 1 - y)]

    def copy(a, k, block, to, own=False):
        px, py, pc = block
        rows = out_refs[a].at[4 * px + 2 * py + pc]
        return pltpu.make_async_remote_copy(
            src_ref=x_refs[a] if own else rows, dst_ref=rows,
            send_sem=send_sems.at[7 * a + k], recv_sem=recv_sems.at[7 * a + k],
            device_id=to, device_id_type=MESH)

    def local(a):
        return pltpu.make_async_copy(x_refs[a], out_refs[a].at[4 * x + 2 * y + c], local_sems.at[a])

    def first(a):
        return [copy(a, 0, me, sibling, own=True)] + [copy(a, 1 + j, me, (*chip, c), own=True)
                                                      for j, chip in enumerate(chips)]

    def start():
        for a in range(n):
            local(a).start()
            for cp in first(a):
                cp.start()

    def forward():
        for j, chip in enumerate(chips):
            for a in range(n):
                copy(a, 1 + j, (*chip, c), me).wait_recv()
                copy(a, 4 + j, (*chip, c), sibling).start()

    def finish():
        for a in range(n):
            copy(a, 0, sibling, me).wait_recv()
            for j, chip in enumerate(chips):
                copy(a, 4 + j, (*chip, 1 - c), me).wait_recv()
        for a in range(n):
            for cp in first(a) + [copy(a, 4 + j, (*chip, c), sibling) for j, chip in enumerate(chips)]:
                cp.wait_send()
            local(a).wait()

    return start, forward, finish


def _rs_sibling_exchange(packed, name):
    rider = _sibling_exchange_rider(packed)
    n = len(packed)

    def body(*refs):
        start, finish = rider.phases(refs[:n], refs[n:2 * n], refs[2 * n:])
        start()
        finish()

    return pl.pallas_call(
        body, name=name, in_specs=rider.in_specs, out_specs=rider.out_specs, out_shape=rider.out_shapes,
        scratch_shapes=rider.scratch,
    )(*packed)


def _sibling_exchange_rider(packed):
    n = len(packed)

    def phases(p_refs, r_refs, scratch):
        send_sems, recv_sems = scratch
        x, y, c = _place()

        def copies():
            return [pltpu.make_async_remote_copy(
                src_ref=p_refs[a].at[2 * j + (1 - c)], dst_ref=r_refs[a].at[j],
                send_sem=send_sems.at[4 * a + j], recv_sem=recv_sems.at[4 * a + j],
                device_id=(x, y, 1 - c), device_id_type=MESH) for a in range(n) for j in range(4)]

        def start():
            for cp in copies():
                cp.start()

        def finish():
            for cp in copies():
                cp.wait()

        return start, finish

    return _Rider(list(packed), [ANY] * n, [jax.ShapeDtypeStruct((4,) + p.shape[1:], p.dtype) for p in packed],
                  [ANY] * n, [pltpu.SemaphoreType.DMA((4 * n,)), pltpu.SemaphoreType.DMA((4 * n,))], phases)


def _rs_chip_sum(packed, from_sibling, c_idx, name):
    _, r, cc = packed.shape
    tr = _pick(r, (512, 256, 352, 128))

    def body(c_ref, a_ref, b_ref, o_ref):
        o_ref[...] = (a_ref[...].astype(F32) + b_ref[...].astype(F32)).astype(o_ref.dtype)

    return pl.pallas_call(
        body, name=name,
        grid_spec=pltpu.PrefetchScalarGridSpec(
            num_scalar_prefetch=1, grid=(4, r // tr),
            in_specs=[pl.BlockSpec((None, tr, cc), lambda j, i, c_ref: (2 * j + c_ref[0], i, 0)),
                      pl.BlockSpec((None, tr, cc), lambda j, i, c_ref: (j, i, 0))],
            out_specs=pl.BlockSpec((None, tr, cc), lambda j, i, c_ref: (j, i, 0))),
        out_shape=jax.ShapeDtypeStruct((4, r, cc), packed.dtype),
        compiler_params=_params("parallel", "parallel"),
    )(c_idx, packed, from_sibling)


def _chip_exchange_shapes(partial):
    return [jax.ShapeDtypeStruct((3,) + p.shape[1:], p.dtype) for p in partial]


def _chip_exchange_sems(n):
    return [pltpu.SemaphoreType.DMA((3 * n,)), pltpu.SemaphoreType.DMA((3 * n,))]


def _chip_exchange_phases(p_refs, r_refs, send_sems, recv_sems):
    x, y, c = _place()
    chips = [(1 - x, y), (x, 1 - y), (1 - x, 1 - y)]

    def copies():
        return [pltpu.make_async_remote_copy(
            src_ref=p_refs[a].at[2 * tx + ty], dst_ref=r_refs[a].at[k],
            send_sem=send_sems.at[3 * a + k], recv_sem=recv_sems.at[3 * a + k],
            device_id=(tx, ty, c), device_id_type=MESH)
            for a in range(len(p_refs)) for k, (tx, ty) in enumerate(chips)]

    def start():
        for cp in copies():
            cp.start()

    def finish():
        for cp in copies():
            cp.wait()

    return start, finish


def _adam_update(w, g, m, v):
    mn = ADAM_B1 * m + (1.0 - ADAM_B1) * g
    vn = ADAM_B2 * v + (1.0 - ADAM_B2) * (g * g)
    m_hat = mn / (1.0 - ADAM_B1 ** ADAM_STEP)
    v_hat = vn / (1.0 - ADAM_B2 ** ADAM_STEP)
    return -ADAM_LR * (m_hat / (jnp.sqrt(v_hat) + ADAM_EPS) + ADAM_WD * w), mn, vn


def _rs_final_adamw(partial, received, chip_idx, w, m, v, name):
    _, r, cc = partial.shape
    tr = _pick(r, (512, 256, 352, 128))

    def body(c_ref, a_ref, r_ref, w_ref, m_ref, v_ref, g_ref, d_ref, mo_ref, vo_ref):
        g = a_ref[...].astype(F32)
        for k in range(3):
            g = g + r_ref[k].astype(F32)
        g_ref[...] = g
        d_ref[...], mo_ref[...], vo_ref[...] = _adam_update(w_ref[...], g, m_ref[...], v_ref[...])

    row = pl.BlockSpec((tr, cc), lambda i, c_ref: (i, 0))
    return pl.pallas_call(
        body, name=name,
        grid_spec=pltpu.PrefetchScalarGridSpec(
            num_scalar_prefetch=1, grid=(r // tr,),
            in_specs=[pl.BlockSpec((None, tr, cc), lambda i, c_ref: (c_ref[0], i, 0)),
                      pl.BlockSpec((3, tr, cc), lambda i, c_ref: (0, i, 0)), row, row, row],
            out_specs=[row] * 4),
        out_shape=[jax.ShapeDtypeStruct((r, cc), F32)] * 4,
        compiler_params=_params("parallel"),
    )(chip_idx, partial, received, w.reshape(r, cc), m.reshape(r, cc), v.reshape(r, cc))


def _all_reduce_small(vals, name):
    rider = _all_reduce_rider(vals)

    def body(v_ref, o_ref, *scratch):
        start, finish = rider.phases([v_ref], [o_ref], scratch)
        start()
        finish()

    return pl.pallas_call(
        body, name=name, in_specs=rider.in_specs, out_specs=rider.out_specs[0], out_shape=rider.out_shapes[0],
        scratch_shapes=rider.scratch, compiler_params=pltpu.CompilerParams(vmem_limit_bytes=VMEM_LIMIT),
    )(vals)


def _all_reduce_rider(vals):
    r, cc = vals.shape

    def phases(ins, outs, scratch):
        (v_ref,), (o_ref,), (buf, send_sems, recv_sems) = ins, outs, scratch
        x, y, c = _place()
        me = 4 * x + 2 * y + c

        def copies():
            cps = []
            for k in range(1, N_DEV):
                kx, ky, kc = (k >> 2) & 1, (k >> 1) & 1, k & 1
                peer = (1 - x if kx else x, 1 - y if ky else y, 1 - c if kc else c)
                cps.append(pltpu.make_async_remote_copy(
                    src_ref=buf.at[0], dst_ref=buf.at[k], send_sem=send_sems.at[k - 1],
                    recv_sem=recv_sems.at[k - 1], device_id=peer, device_id_type=MESH))
            return cps

        def start():
            buf[0] = v_ref[...]
            for cp in copies():
                cp.start()

        def finish():
            for cp in copies():
                cp.wait()
            acc = buf[jnp.bitwise_xor(me, 0)]
            for dev in range(1, N_DEV):
                acc = acc + buf[jnp.bitwise_xor(me, dev)]
            o_ref[...] = acc

        return start, finish

    vm = pl.BlockSpec(memory_space=pltpu.VMEM)
    return _Rider([vals], [vm], [jax.ShapeDtypeStruct((r, cc), F32)], [vm],
                  [pltpu.VMEM((N_DEV, r, cc), F32), pltpu.SemaphoreType.DMA((7,)), pltpu.SemaphoreType.DMA((7,))],
                  phases)


def _chip_exchange_rider(partial):
    n = len(partial)
    return _Rider(list(partial), [ANY] * n, _chip_exchange_shapes(partial), [ANY] * n, _chip_exchange_sems(n),
                  lambda ins, outs, scratch: _chip_exchange_phases(ins, outs, *scratch))


def _lanes(flat):
    pad = (-flat.shape[0]) % (SUBLANES * LANES)
    return jnp.pad(flat, (0, pad)).reshape(-1, LANES)


def kernel(x, even_w_in, even_b_f, even_conv_w, even_w_out, odd_w_in, odd_v_ln_g, odd_v_ln_b, odd_w_s, odd_b_s, odd_w_out, mix_ln_g, mix_ln_b, ffn_w_in, ffn_w_out, ffn_ln_g, ffn_ln_b, loss_target, m_even_w_in, m_even_b_f, m_even_conv_w, m_even_w_out, m_odd_w_in, m_odd_v_ln_g, m_odd_v_ln_b, m_odd_w_s, m_odd_b_s, m_odd_w_out, m_mix_ln_g, m_mix_ln_b, m_ffn_w_in, m_ffn_w_out, m_ffn_ln_g, m_ffn_ln_b, v_even_w_in, v_even_b_f, v_even_conv_w, v_even_w_out, v_odd_w_in, v_odd_v_ln_g, v_odd_v_ln_b, v_odd_w_s, v_odd_b_s, v_odd_w_out, v_mix_ln_g, v_mix_ln_b, v_ffn_w_in, v_ffn_w_out, v_ffn_ln_g, v_ffn_ln_b):
    t, d = x.shape[1], x.shape[2]
    nh = even_b_f.shape[-1]
    w = even_conv_w.shape[-1] * N_DEV
    dh = w // nh
    scale = dh ** -0.5
    e_in = even_w_in.shape[-1] * N_DEV
    f2 = ffn_w_in.shape[-1] * N_DEV
    f = f2 // 2
    ng, pb = odd_w_s.shape[1], odd_w_s.shape[2]
    assert e_in == 6 * w + nh and nh <= SUBLANES and (6 * w) % LANES == 0 and d % N_DEV == 0
    mx, my, mc = _place()
    me = 4 * mx + 2 * my + mc

    big = [even_w_in[0], even_w_out[0], odd_w_in[0], odd_w_out[0],
           ffn_w_in[0], ffn_w_in[1], ffn_w_out[0], ffn_w_out[1]]
    g_in0, = _all_gather([big[0].astype(BF16)], "ag_even_w_in")
    w_in0 = g_in0.transpose(1, 0, 2).reshape(d, e_in)
    w_all0 = jnp.concatenate([w_in0[:, :3 * w], w_in0[:, 3 * w + nh:], w_in0[:, 3 * w:3 * w + nh],
                              jnp.zeros((d, LANES - nh), BF16)], axis=1)

    cs, vs = even_conv_w.shape[-1], odd_v_ln_g.shape[-1]
    small_mine = jnp.concatenate([
        lax.dynamic_update_slice(jnp.zeros((3, w), F32), even_conv_w[0], (0, me * cs)).reshape(-1),
        lax.dynamic_update_slice(jnp.zeros((d,), F32), odd_v_ln_g[0], (me * vs,)),
        lax.dynamic_update_slice(jnp.zeros((d,), F32), odd_v_ln_b[0], (me * vs,))])
    small_all = _all_reduce_small(_lanes(small_mine), "ag_small").reshape(-1)
    conv_w = small_all[:3 * w].reshape(3, w)
    vln_g = small_all[3 * w:3 * w + d]
    vln_b = small_all[3 * w + d:3 * w + 2 * d]

    bf_pad = jnp.pad(even_b_f[0], (0, LANES - nh)).reshape(1, LANES)
    chunk = jnp.arange(pb) // (pb // 2)
    ws_mask = (chunk[None, :] <= chunk[:, None])[None]
    wm = jnp.where(ws_mask, odd_w_s[0], 0.0).astype(BF16)
    bs_full = jnp.repeat(odd_b_s[0].T, d // ng, axis=1)

    x0 = x[0]
    tgt = loss_target[0]
    fcol = 6 * w // LANES
    x0b = x0.astype(BF16)
    p0 = _mm(x0b, w_all0, "nn", F32, "l0_in_proj")
    cgate = _fgate_fwd(p0, bf_pad, fcol, nh, "l0_fgate")
    assert dh + 7 <= LANES
    qa, ka, va = _attn_pack(p0, cgate, w, nh, scale, "l0_attn_pack")
    oa, g_out0, g_in1, g_out1, g_fi0, g_fi1, g_fo0, g_fo1 = _attn_fwd(
        qa, ka, va, dh, "l0_attn", gather=[s.astype(BF16) for s in big[1:]])
    w_out0, w_out1 = g_out0.reshape(2 * w, d), g_out1.reshape(d, d)
    w_fo0, w_fo1 = g_fo0.reshape(f, d), g_fo1.reshape(f, d)
    nb = N_DEV // 2
    w_fi0, w_fi1 = g_fi0.reshape(2, nb, d, -1), g_fi1.reshape(2, nb, d, -1)
    attn, = _attn_unpack(oa, w, nh, 1.0, None, 1.0, "l0_attn_unpack")
    yconv = _conv_fwd(p0, conv_w, w, 3, "l0_conv")
    mix = jnp.concatenate([attn, yconv], axis=1)
    m0, x1, x1b = _mm_ln(mix, w_out0, x0, mix_ln_g[0], mix_ln_b[0], "l0_out_proj_ln")
    h0, gu0 = _ffn_in_swiglu(x1b, w_fi0, "l0_ffn_in")
    f0, x2, x2b = _mm_ln(h0, w_fo0, x1, ffn_ln_g[0], ffn_ln_b[0], "l0_ffn_out_ln")

    uv = _mm_cols_fwd(x2b, g_in1, False, F32, "l1_in_proj")
    gated = _sgu_fwd(uv, vln_g, vln_b, wm, bs_full, "l1_sgu")
    m1, x3, x3b = _mm_ln(gated, w_out1, x2, mix_ln_g[1], mix_ln_b[1], "l1_out_proj_ln")
    h1, gu1 = _ffn_in_swiglu(x3b, w_fi1, "l1_ffn_in")
    dz4, dz4b, g_ffn_g1, g_ffn_b1, loss_part = _mm_ln(h1, w_fo1, x3, ffn_ln_g[1], ffn_ln_b[1],
                                                      "l1_ffn_out_ln_loss", target=tgt)
    gd_fo1 = _mm_blk_dw(h1, dz4b, BF16, "l1_ffn_out_dw").reshape(N_DEV, -1, d)
    dgu1 = _ffn_out_dx_swiglu(dz4b, w_fo1, gu1, "l1_ffn_out_dx").reshape(N_DEV, t, -1)
    gd_fi1 = _mm_cols_dw(x3b, dgu1, N_DEV, True, BF16, "l1_ffn_in_dw")
    dz3, dz3b, g_mix_g1, g_mix_b1 = _mm_cols_dx_ln_bwd(dgu1, g_fi1, True, x2, m1, mix_ln_g[1], dz4,
                                                       "l1_ffn_in_dx_ln_bwd")
    gd_out1 = _mm(gated, dz3b, "tn", BF16, "l1_out_proj_dw").reshape(N_DEV, -1, d)
    dgated = _mm(dz3b, w_out1, "nt", BF16, "l1_out_proj_dx")
    duv, g_wm, g_bs_t, g_vln_g, g_vln_b = _sgu_bwd(uv, vln_g, vln_b, wm, bs_full, dgated, "l1_sgu_bwd")
    gd_in1 = _mm_cols_dw(x2b, duv, N_DEV, False, BF16, "l1_in_proj_dw")

    dz2, dz2b, g_ffn_g0, g_ffn_b0 = _mm_cols_dx_ln_bwd(duv, g_in1, False, x1, f0, ffn_ln_g[0], dz3,
                                                       "l1_in_proj_dx_ln_bwd")
    gd_fo0 = _mm_blk_dw(h0, dz2b, BF16, "l0_ffn_out_dw").reshape(N_DEV, -1, d)
    dgu0 = _ffn_out_dx_swiglu(dz2b, w_fo0, gu0, "l0_ffn_out_dx").reshape(N_DEV, t, -1)
    gd_fi0 = _mm_cols_dw(x1b, dgu0, N_DEV, True, BF16, "l0_ffn_in_dw")
    dz1, dz1b, g_mix_g0, g_mix_b0 = _mm_cols_dx_ln_bwd(dgu0, g_fi0, True, x0, m0, mix_ln_g[0], dz2,
                                                       "l0_ffn_in_dx_ln_bwd")
    gd_out0 = _mm(mix, dz1b, "tn", BF16, "l0_out_proj_dw").reshape(N_DEV, -1, d)
    early_g = [gd_out0, gd_in1, gd_out1, gd_fi0, gd_fi1, gd_fo0, gd_fo1]
    dmix, *early_sib = _mm(dz1b, w_out0, "nt", F32, "l0_out_proj_dx", rider=_sibling_exchange_rider(early_g))
    d_b, d_c, d_h, g_conv = _conv_bwd(p0, conv_w, dmix, w, 3, "l0_conv_bwd")
    doa, qa2 = _attn_pack_bwd(dmix, oa, qa, w, nh, "l0_attn_pack_bwd")
    big_names = ["even_w_in", "even_w_out", "odd_w_in", "odd_w_out", "ffn_w_in0", "ffn_w_in1", "ffn_w_out0", "ffn_w_out1"]
    c_idx = mc.reshape(1).astype(jnp.int32)
    chip_idx = (2 * mx + my).reshape(1).astype(jnp.int32)
    early_partial = [_rs_chip_sum(g, s, c_idx, "rs_chip_sum_" + n)
                     for g, s, n in zip(early_g, early_sib, big_names[1:])]
    dqa, dka, dva, *early_received = _attn_bwd(qa2, ka, va, doa, "l0_attn_bwd", exchange=early_partial)
    dq, dcq = _attn_unpack(dqa, w, nh, scale, dh + 3, 1.0, "l0_attn_unpack_dq")
    dk, dck = _attn_unpack(dka, w, nh, 1.0, dh, -1.0, "l0_attn_unpack_dk")
    dv, = _attn_unpack(dva, w, nh, 1.0, None, 1.0, "l0_attn_unpack_dv")
    dzf, g_bf = _fgate_bwd(p0, bf_pad, dcq, dck, fcol, nh, "l0_fgate_bwd")
    dp0 = jnp.concatenate([dq, dk, dv, d_b.astype(BF16), d_c.astype(BF16), d_h.astype(BF16), dzf.astype(BF16)], axis=1)
    g_ws = jnp.where(ws_mask, g_wm, 0.0)
    g_bs = g_bs_t[:, :ng].T
    small_g = [g_bf[:nh], g_conv, g_vln_g, g_vln_b, g_ws, g_bs,
               jnp.stack([g_mix_g0, g_mix_g1]), jnp.stack([g_mix_b0, g_mix_b1]),
               jnp.stack([g_ffn_g0, g_ffn_g1]), jnp.stack([g_ffn_b0, g_ffn_b1])]
    small_rider = _all_reduce_rider(_lanes(jnp.concatenate([a.reshape(-1) for a in small_g])))
    g_all0, small_sum = _mm(x0b, dp0, "tn", F32, "l0_in_proj_dw", rider=small_rider)
    gd_in0 = jnp.concatenate([g_all0[:, :3 * w], g_all0[:, 6 * w:6 * w + nh], g_all0[:, 3 * w:6 * w]], axis=1)
    gd_in0 = gd_in0.reshape(d, N_DEV, -1).transpose(1, 0, 2).astype(BF16)

    big_m = [m_even_w_in[0], m_even_w_out[0], m_odd_w_in[0], m_odd_w_out[0],
             m_ffn_w_in[0], m_ffn_w_in[1], m_ffn_w_out[0], m_ffn_w_out[1]]
    big_v = [v_even_w_in[0], v_even_w_out[0], v_odd_w_in[0], v_odd_w_out[0],
             v_ffn_w_in[0], v_ffn_w_in[1], v_ffn_w_out[0], v_ffn_w_out[1]]
    late_sib = _rs_sibling_exchange([gd_in0], "rs_sibling_late")
    late_partial = [_rs_chip_sum(gd_in0, late_sib[0], c_idx, "rs_chip_sum_" + big_names[0])]
    partial = late_partial + early_partial
    grad_x, *late_received = _mm(dp0, w_all0, "nt", F32, "l0_in_proj_dx", rider=_chip_exchange_rider(late_partial),
                                 addend=(dz1, ALPHA))
    received = list(late_received) + list(early_received)
    upd = [_rs_final_adamw(p, r, chip_idx, wt, mt, vt, "rs_final_adamw_" + n)
           for p, r, wt, mt, vt, n in zip(partial, received, big, big_m, big_v, big_names)]
    big_out = {}
    for i, n in enumerate(["even_w_in", "even_w_out", "odd_w_in", "odd_w_out"]):
        big_out[n] = [o[None] for o in upd[i]]
    big_out["ffn_w_in"] = [jnp.stack([a, b]) for a, b in zip(upd[4], upd[5])]
    big_out["ffn_w_out"] = [jnp.stack([a, b]) for a, b in zip(upd[6], upd[7])]

    small_sum = small_sum.reshape(-1)
    outs_small = []
    off = 0
    for a in small_g:
        outs_small.append(small_sum[off:off + a.size].reshape(a.shape))
        off += a.size
    gr_bf, gr_conv, gr_vg, gr_vb, gr_ws, gr_bs, gr_mg, gr_mb, gr_fg, gr_fb = outs_small

    loss = lax.psum(loss_part, ("x", "y", "c"))

    grads = {
        "even_b_f": gr_bf[None],
        "even_conv_w": lax.dynamic_slice(gr_conv, (0, me * cs), (3, cs))[None],
        "odd_v_ln_g": lax.dynamic_slice(gr_vg, (me * vs,), (vs,))[None],
        "odd_v_ln_b": lax.dynamic_slice(gr_vb, (me * vs,), (vs,))[None],
        "odd_w_s": gr_ws[None], "odd_b_s": gr_bs[None],
        "mix_ln_g": gr_mg, "mix_ln_b": gr_mb, "ffn_ln_g": gr_fg, "ffn_ln_b": gr_fb,
    }
    weights = dict(even_w_in=even_w_in, even_b_f=even_b_f, even_conv_w=even_conv_w, even_w_out=even_w_out,
                   odd_w_in=odd_w_in, odd_v_ln_g=odd_v_ln_g, odd_v_ln_b=odd_v_ln_b, odd_w_s=odd_w_s,
                   odd_b_s=odd_b_s, odd_w_out=odd_w_out, mix_ln_g=mix_ln_g, mix_ln_b=mix_ln_b,
                   ffn_w_in=ffn_w_in, ffn_w_out=ffn_w_out, ffn_ln_g=ffn_ln_g, ffn_ln_b=ffn_ln_b)
    moms = dict(even_w_in=(m_even_w_in, v_even_w_in), even_b_f=(m_even_b_f, v_even_b_f),
                even_conv_w=(m_even_conv_w, v_even_conv_w), even_w_out=(m_even_w_out, v_even_w_out),
                odd_w_in=(m_odd_w_in, v_odd_w_in), odd_v_ln_g=(m_odd_v_ln_g, v_odd_v_ln_g),
                odd_v_ln_b=(m_odd_v_ln_b, v_odd_v_ln_b), odd_w_s=(m_odd_w_s, v_odd_w_s),
                odd_b_s=(m_odd_b_s, v_odd_b_s), odd_w_out=(m_odd_w_out, v_odd_w_out),
                mix_ln_g=(m_mix_ln_g, v_mix_ln_g), mix_ln_b=(m_mix_ln_b, v_mix_ln_b),
                ffn_w_in=(m_ffn_w_in, v_ffn_w_in), ffn_w_out=(m_ffn_w_out, v_ffn_w_out),
                ffn_ln_g=(m_ffn_ln_g, v_ffn_ln_g), ffn_ln_b=(m_ffn_ln_b, v_ffn_ln_b))
    names = list(weights)
    gout, deltas, new_m, new_v = [], [], [], []
    for n in names:
        if n in big_out:
            gr, dlt, mn, vn = big_out[n]
        else:
            gr = grads[n]
            dlt, mn, vn = _adamw(weights[n], gr, moms[n][0], moms[n][1], "adamw_" + n)
        gout.append(gr.reshape(weights[n].shape))
        deltas.append(dlt.reshape(weights[n].shape))
        new_m.append(mn.reshape(weights[n].shape))
        new_v.append(vn.reshape(weights[n].shape))
    return (loss, grad_x[None], *gout, *deltas, *new_m, *new_v)
```

```python
import functools
from typing import Callable, NamedTuple

import jax
import jax.numpy as jnp
from jax import lax
from jax.experimental import pallas as pl
from jax.experimental.pallas import tpu as pltpu

F32 = jnp.float32
BF16 = jnp.bfloat16
MESH = pl.DeviceIdType.MESH

DEPTH = 2
ALPHA = (2.0 * DEPTH) ** 0.25
LN_EPS = 1e-5
ADAM_LR = 0.001
ADAM_B1 = 0.9
ADAM_B2 = 0.999
ADAM_EPS = 1e-08
ADAM_WD = 0.01
ADAM_STEP = 10

N_DEV = 8
LANES = 128
SUBLANES = 8
VMEM_LIMIT = 48 * 1024 * 1024
NEG_BIG = -1e30
ROW_TILES = (512, 256, 128)


def _pick(n, cands):
    for c in cands:
        if c <= n and n % c == 0:
            return c
    return n


def _params(*sem):
    return pltpu.CompilerParams(dimension_semantics=sem, vmem_limit_bytes=VMEM_LIMIT)


NN = (((1,), (0,)), ((), ()))
NT = (((1,), (1,)), ((), ()))
TN = (((0,), (0,)), ((), ()))
M_TILES = (1024, 512, 1408, 256, 128)
N_TILES = (512, 640, 256, 128)
K_TILES = (4096, 2048, 1024, 512, 640, 1408, 256, 128)
K_WHOLE = 3328


class _Rider(NamedTuple):
    inputs: list
    in_specs: list
    out_shapes: list
    out_specs: list
    scratch: list
    phases: Callable


def _mm_core(name, grid, a, b, a_spec, b_spec, o_spec, o_shape, o_dtype, dims, tile, pieces=None, rider=None,
             addend=None):
    nred = grid[2]
    pieces = pieces or [(lambda r: r[...], lambda r: r[...])]
    ni = len(rider.inputs) if rider else 0
    no = len(rider.out_shapes) if rider else 0
    nacc = 0 if nred == 1 else 1
    add_arrays = [addend[0]] if addend else []

    def body(a_ref, b_ref, *rest):
        if addend:
            add_ref, rest = rest[0], rest[1:]
        finished = (lambda v: addend[1] * add_ref[...] + v) if addend else (lambda v: v)
        o_ref = rest[ni]
        if rider:
            start, finish = rider.phases(rest[:ni], rest[ni + 1:ni + 1 + no], rest[ni + 1 + no + nacc:])
            ids = [pl.program_id(ax) for ax in range(3)]
            first = functools.reduce(jnp.logical_and, [i == 0 for i in ids])
            last = functools.reduce(jnp.logical_and, [i == g - 1 for i, g in zip(ids, grid)])
            pl.when(first)(start)
        part = None
        for fa, fb in pieces:
            prod = lax.dot_general(fa(a_ref).astype(BF16), fb(b_ref).astype(BF16), dims, preferred_element_type=F32)
            part = prod if part is None else part + prod
        if nred == 1:
            o_ref[...] = finished(part).astype(o_ref.dtype)
        else:
            acc_ref = rest[ni + 1 + no]
            kk = pl.program_id(2)

            @pl.when(kk == 0)
            def _():
                acc_ref[...] = jnp.zeros_like(acc_ref)

            acc_ref[...] += part

            @pl.when(kk == nred - 1)
            def _():
                o_ref[...] = finished(acc_ref[...]).astype(o_ref.dtype)
        if rider:
            pl.when(last)(finish)

    out = pl.pallas_call(
        body, name=name, grid=grid,
        in_specs=[a_spec, b_spec] + ([o_spec] if addend else []) + (rider.in_specs if rider else []),
        out_specs=[o_spec] + (rider.out_specs if rider else []),
        out_shape=[jax.ShapeDtypeStruct(o_shape, o_dtype)] + (rider.out_shapes if rider else []),
        scratch_shapes=([] if nred == 1 else [pltpu.VMEM(tile, F32)]) + (rider.scratch if rider else []),
        compiler_params=_params(*(["arbitrary"] * 3 if rider else ["parallel", "parallel", "arbitrary"])),
    )(a, b, *add_arrays, *(rider.inputs if rider else []))
    return out if rider else out[0]


def _mm(a, b, mode, out_dtype, name, rider=None, addend=None):
    if mode == "nn":
        (m, k), (k2, n) = a.shape, b.shape
    elif mode == "nt":
        (m, k), (n, k2) = a.shape, b.shape
    else:
        (k, m), (k2, n) = a.shape, b.shape
    assert k == k2, (a.shape, b.shape, mode)
    tm, tn = _pick(m, M_TILES), _pick(n, N_TILES)
    tk = k if k <= K_WHOLE else _pick(k, K_TILES)
    if mode == "nn":
        a_spec = pl.BlockSpec((tm, tk), lambda i, j, kk: (i, kk))
        b_spec = pl.BlockSpec((tk, tn), lambda i, j, kk: (kk, j))
        dims = NN
    elif mode == "nt":
        a_spec = pl.BlockSpec((tm, tk), lambda i, j, kk: (i, kk))
        b_spec = pl.BlockSpec((tn, tk), lambda i, j, kk: (j, kk))
        dims = NT
    else:
        a_spec = pl.BlockSpec((tk, tm), lambda i, j, kk: (kk, i))
        b_spec = pl.BlockSpec((tk, tn), lambda i, j, kk: (kk, j))
        dims = TN
    return _mm_core(name, (m // tm, n // tn, k // tk), a, b, a_spec, b_spec,
                    pl.BlockSpec((tm, tn), lambda i, j, kk: (i, j)), (m, n), out_dtype, dims, (tm, tn), rider=rider,
                    addend=addend)


def _act_spec(blocked, rows, ns, row_ax, d_ax):
    if blocked:
        return pl.BlockSpec((None, rows, ns), lambda *g: (g[d_ax], g[row_ax], 0))
    return pl.BlockSpec((rows, ns), lambda *g: (g[row_ax], g[d_ax]))


def _mm_cols_fwd(a, g3, blocked, out_dtype, name):
    (t, k), (nd, k2, ns) = a.shape, g3.shape
    assert k == k2
    tm, tk = _pick(t, M_TILES), _pick(k, K_TILES)
    return _mm_core(name, (t // tm, nd, k // tk), a, g3,
                    pl.BlockSpec((tm, tk), lambda i, d, kk: (i, kk)),
                    pl.BlockSpec((None, tk, ns), lambda i, d, kk: (d, kk, 0)),
                    _act_spec(blocked, tm, ns, 0, 1), (nd, t, ns) if blocked else (t, nd * ns), out_dtype, NN, (tm, ns))


def _mm_cols_dx_ln_bwd(dy, g3, blocked, xa, xb, gam, dya, name):
    nd, k, ns = g3.shape
    t, d = xa.shape
    assert k == d
    tm = _pick(t, ROW_TILES)
    grp = nd if not blocked else (2 if nd % 2 == 0 else 1)
    nred = nd // grp

    def body(a_ref, b_ref, xa_ref, xb_ref, g_ref, dya_ref, dz_ref, dzb_ref, dg_ref, db_ref, *acc):
        i, kk = pl.program_id(0), pl.program_id(1)

        @pl.when(jnp.logical_and(i == 0, kk == 0))
        def _():
            dg_ref[...] = jnp.zeros_like(dg_ref)
            db_ref[...] = jnp.zeros_like(db_ref)

        if blocked:
            part = None
            for s in range(grp):
                prod = lax.dot_general(a_ref[s], b_ref[s], NT, preferred_element_type=F32)
                part = prod if part is None else part + prod
        else:
            whole_b = jnp.concatenate([b_ref[s] for s in range(nd)], axis=1)
            part = lax.dot_general(a_ref[...], whole_b, NT, preferred_element_type=F32)

        def ln_bwd(dyb):
            dy_t = ALPHA * dya_ref[...] + dyb
            z = ALPHA * xa_ref[...] + xb_ref[...]
            mu = jnp.mean(z, axis=-1, keepdims=True)
            zc = z - mu
            var = jnp.mean(zc * zc, axis=-1, keepdims=True)
            rstd = lax.rsqrt(var + LN_EPS)
            xhat = zc * rstd
            dxh = dy_t * g_ref[...]
            m1 = jnp.mean(dxh, axis=-1, keepdims=True)
            m2 = jnp.mean(dxh * xhat, axis=-1, keepdims=True)
            dz = rstd * (dxh - m1 - xhat * m2)
            dz_ref[...] = dz
            dzb_ref[...] = dz.astype(BF16)
            dg_ref[...] += jnp.sum(dy_t * xhat, axis=0, keepdims=True)
            db_ref[...] += jnp.sum(dy_t, axis=0, keepdims=True)

        if nred == 1:
            ln_bwd(part)
        else:
            acc_ref, = acc

            @pl.when(kk == 0)
            def _():
                acc_ref[...] = part

            @pl.when(kk > 0)
            def _():
                acc_ref[...] += part

            @pl.when(kk == nred - 1)
            def _():
                ln_bwd(acc_ref[...])

    row = pl.BlockSpec((tm, d), lambda i, kk: (i, 0))
    vec = pl.BlockSpec((1, d), lambda i, kk: (0, 0))
    if blocked:
        a_spec = pl.BlockSpec((grp, tm, ns), lambda i, kk: (kk, i, 0))
        b_spec = pl.BlockSpec((grp, k, ns), lambda i, kk: (kk, 0, 0))
    else:
        a_spec = pl.BlockSpec((tm, nd * ns), lambda i, kk: (i, 0))
        b_spec = pl.BlockSpec((nd, k, ns), lambda i, kk: (0, 0, 0))
    dz, dzb, dg, db = pl.pallas_call(
        body, name=name, grid=(t // tm, nred),
        in_specs=[a_spec, b_spec, row, row, vec, row], out_specs=[row, row, vec, vec],
        out_shape=[jax.ShapeDtypeStruct((t, d), F32), jax.ShapeDtypeStruct((t, d), BF16),
                   jax.ShapeDtypeStruct((1, d), F32), jax.ShapeDtypeStruct((1, d), F32)],
        scratch_shapes=[] if nred == 1 else [pltpu.VMEM((tm, d), F32)],
        compiler_params=_params("arbitrary", "arbitrary"),
    )(dy, g3, xa, xb, gam.reshape(1, d), dya)
    return dz, dzb, dg[0], db[0]


def _mm_cols_dw(a, dy, nd, blocked, out_dtype, name):
    t, k = a.shape
    ns = dy.shape[2] if blocked else dy.shape[1] // nd
    tmk, tk = _pick(k, M_TILES), _pick(t, K_TILES)
    return _mm_core(name, (nd, k // tmk, t // tk), a, dy,
                    pl.BlockSpec((tk, tmk), lambda d, j, kk: (kk, j)),
                    _act_spec(blocked, tk, ns, 2, 0),
                    pl.BlockSpec((None, tmk, ns), lambda d, j, kk: (d, j, 0)), (nd, k, ns), out_dtype, TN, (tmk, ns))


def _mm_blk_dw(h3, dz, out_dtype, name):
    (nb, t, ns), (_, n) = h3.shape, dz.shape
    tn, tk = _pick(n, (1024,) + N_TILES), _pick(t, K_TILES)
    return _mm_core(name, (nb, n // tn, t // tk), h3, dz,
                    pl.BlockSpec((None, tk, ns), lambda d, j, kk: (d, kk, 0)),
                    pl.BlockSpec((tk, tn), lambda d, j, kk: (kk, j)),
                    pl.BlockSpec((ns, tn), lambda d, j, kk: (d, j)), (nb * ns, n), out_dtype, TN, (ns, tn))


def _mm_ln(a, w, xa, g, b, name, target=None):
    blocked = a.ndim == 3
    t, d = xa.shape
    k = w.shape[0]
    tm = _pick(t, ROW_TILES)
    nb = a.shape[0] if blocked else 1
    ns = k // nb
    halves = [slice(0, tm // 2), slice(tm // 2, tm)] if tm % 32 == 0 else [slice(0, tm)]

    def body(a_ref, w_ref, xa_ref, g_ref, b_ref, *rest):
        def product(rows):
            if not blocked:
                return jnp.dot(a_ref[rows, :], w_ref[...], preferred_element_type=F32)
            acc = None
            for s in range(nb):
                prod = jnp.dot(a_ref[s, rows, :], w_ref[s * ns:(s + 1) * ns, :], preferred_element_type=F32)
                acc = prod if acc is None else acc + prod
            return acc

        if target is not None:
            t_ref, dz_ref, dzb_ref, dg_ref, db_ref, l_ref = rest

            @pl.when(pl.program_id(0) == 0)
            def _():
                l_ref[...] = jnp.zeros_like(l_ref)
                dg_ref[...] = jnp.zeros_like(dg_ref)
                db_ref[...] = jnp.zeros_like(db_ref)
        else:
            xb_ref, y_ref, yb_ref = rest
        for rows, xb in zip(halves, [product(rows) for rows in halves]):
            z = ALPHA * xa_ref[rows, :] + xb
            mu = jnp.mean(z, axis=-1, keepdims=True)
            zc = z - mu
            var = jnp.mean(zc * zc, axis=-1, keepdims=True)
            rstd = lax.rsqrt(var + LN_EPS)
            xhat = zc * rstd
            y = xhat * g_ref[...] + b_ref[...]
            if target is not None:
                e = y - t_ref[rows, :]
                l_ref[...] += 0.5 * jnp.sum(jnp.mean(e * e, axis=-1, keepdims=True))
                dy = e * (1.0 / d)
                dxh = dy * g_ref[...]
                m1 = jnp.mean(dxh, axis=-1, keepdims=True)
                m2 = jnp.mean(dxh * xhat, axis=-1, keepdims=True)
                dz = rstd * (dxh - m1 - xhat * m2)
                dz_ref[rows, :] = dz
                dzb_ref[rows, :] = dz.astype(BF16)
                dg_ref[...] += jnp.sum(dy * xhat, axis=0, keepdims=True)
                db_ref[...] += jnp.sum(dy, axis=0, keepdims=True)
            else:
                xb_ref[rows, :] = xb
                y_ref[rows, :] = y
                yb_ref[rows, :] = y.astype(BF16)

    row = pl.BlockSpec((tm, d), lambda i: (i, 0))
    vec = pl.BlockSpec((1, d), lambda i: (0, 0))
    a_spec = pl.BlockSpec((nb, tm, ns), lambda i: (0, i, 0)) if blocked else pl.BlockSpec((tm, k), lambda i: (i, 0))
    ins = [a, w, xa, g.reshape(1, d), b.reshape(1, d)]
    in_specs = [a_spec, pl.BlockSpec((k, d), lambda i: (0, 0)), row, vec, vec]
    if target is not None:
        dz, dzb, dg, db, l = pl.pallas_call(
            body, name=name, grid=(t // tm,), in_specs=in_specs + [row],
            out_specs=[row, row, vec, vec, pl.BlockSpec((1, LANES), lambda i: (0, 0))],
            out_shape=[jax.ShapeDtypeStruct((t, d), F32), jax.ShapeDtypeStruct((t, d), BF16),
                       jax.ShapeDtypeStruct((1, d), F32), jax.ShapeDtypeStruct((1, d), F32),
                       jax.ShapeDtypeStruct((1, LANES), F32)],
            compiler_params=_params("arbitrary"),
        )(*ins, target)
        return dz, dzb, dg[0], db[0], l[0, 0]
    return pl.pallas_call(
        body, name=name, grid=(t // tm,), in_specs=in_specs, out_specs=[row, row, row],
        out_shape=[jax.ShapeDtypeStruct((t, d), F32)] * 2 + [jax.ShapeDtypeStruct((t, d), BF16)],
        compiler_params=_params("parallel"),
    )(*ins)


def _ffn_in_swiglu(xb, g4, name):
    (t, k), (_, nb, _, ns) = xb.shape, g4.shape
    tm = _pick(t, M_TILES)

    def body(x_ref, w_ref, h_ref, gu_ref):
        xv = x_ref[...]
        gate = jnp.dot(xv, w_ref[0], preferred_element_type=F32)
        up = jnp.dot(xv, w_ref[1], preferred_element_type=F32)
        h_ref[...] = (gate * jax.nn.sigmoid(gate) * up).astype(BF16)
        gu_ref[0] = gate.astype(BF16)
        gu_ref[1] = up.astype(BF16)

    return pl.pallas_call(
        body, name=name, grid=(t // tm, nb),
        in_specs=[pl.BlockSpec((tm, k), lambda i, d: (i, 0)),
                  pl.BlockSpec((2, None, k, ns), lambda i, d: (0, d, 0, 0))],
        out_specs=[pl.BlockSpec((None, tm, ns), lambda i, d: (d, i, 0)),
                   pl.BlockSpec((2, None, tm, ns), lambda i, d: (0, d, i, 0))],
        out_shape=[jax.ShapeDtypeStruct((nb, t, ns), BF16), jax.ShapeDtypeStruct((2, nb, t, ns), BF16)],
        compiler_params=_params("parallel", "parallel"),
    )(xb, g4)


def _ffn_out_dx_swiglu(dz, w_out, gu4, name):
    (t, d), (_, nb, _, ns) = dz.shape, gu4.shape
    tm = _pick(t, M_TILES)

    def body(dz_ref, w_ref, gu_ref, o_ref):
        halves = [slice(0, tm // 2), slice(tm // 2, tm)] if tm % 16 == 0 else [slice(0, tm)]
        dhs = [lax.dot_general(dz_ref[rows, :].astype(BF16), w_ref[...], NT, preferred_element_type=F32)
               for rows in halves]
        for rows, dh in zip(halves, dhs):
            gate = gu_ref[0, rows, :].astype(F32)
            up = gu_ref[1, rows, :].astype(F32)
            sg = jax.nn.sigmoid(gate)
            silu = gate * sg
            o_ref[0, rows, :] = (dh * up * (sg + silu * (1.0 - sg))).astype(BF16)
            o_ref[1, rows, :] = (dh * silu).astype(BF16)

    blk = pl.BlockSpec((2, None, tm, ns), lambda i, j: (0, j, i, 0))
    return pl.pallas_call(
        body, name=name, grid=(t // tm, nb),
        in_specs=[pl.BlockSpec((tm, d), lambda i, j: (i, 0)), pl.BlockSpec((ns, d), lambda i, j: (j, 0)), blk],
        out_specs=blk,
        out_shape=jax.ShapeDtypeStruct((2, nb, t, ns), BF16),
        compiler_params=_params("parallel", "parallel"),
    )(dz, w_out, gu4)


def _tri_matmul(tri, x):
    x1 = x.astype(BF16)
    r1 = x - x1.astype(F32)
    x2 = r1.astype(BF16)
    x3 = (r1 - x2.astype(F32)).astype(BF16)
    dot = lambda v: jnp.dot(tri, v, preferred_element_type=F32)
    return dot(x1) + dot(x2) + dot(x3)


def _fgate_fwd(proj, bf_pad, fcol, n_heads, name):
    t = proj.shape[0]
    tb = _pick(t, ROW_TILES)

    def body(p_ref, b_ref, c_ref, carry):
        @pl.when(pl.program_id(0) == 0)
        def _():
            carry[...] = jnp.zeros_like(carry)

        z = p_ref[...] + b_ref[...]
        lf = jnp.minimum(z, 0.0) - jnp.log1p(jnp.exp(-jnp.abs(z)))
        lane = lax.broadcasted_iota(jnp.int32, (tb, LANES), 1)
        lf = jnp.where(lane < n_heads, lf, 0.0)
        r = lax.broadcasted_iota(jnp.int32, (tb, tb), 0)
        s = lax.broadcasted_iota(jnp.int32, (tb, tb), 1)
        tri = (s <= r).astype(BF16)
        c = _tri_matmul(tri, lf) + carry[...]
        c_ref[...] = c
        carry[...] = c[tb - 1:tb, :]

    return pl.pallas_call(
        body, name=name, grid=(t // tb,),
        in_specs=[pl.BlockSpec((tb, LANES), lambda i: (i, fcol)), pl.BlockSpec((1, LANES), lambda i: (0, 0))],
        out_specs=pl.BlockSpec((tb, LANES), lambda i: (i, 0)),
        out_shape=jax.ShapeDtypeStruct((t, LANES), F32),
        scratch_shapes=[pltpu.VMEM((1, LANES), F32)],
        compiler_params=_params("arbitrary"),
    )(proj, bf_pad)


def _fgate_bwd(proj, bf_pad, dcq, dck, fcol, n_heads, name):
    t = proj.shape[0]
    tb = _pick(t, ROW_TILES)
    nb = t // tb

    def body(p_ref, b_ref, dcq_ref, dck_ref, dz_ref, db_ref, carry):
        @pl.when(pl.program_id(0) == 0)
        def _():
            carry[...] = jnp.zeros_like(carry)
            db_ref[...] = jnp.zeros_like(db_ref)

        r = lax.broadcasted_iota(jnp.int32, (tb, tb), 0)
        s = lax.broadcasted_iota(jnp.int32, (tb, tb), 1)
        tri = (s >= r).astype(BF16)
        dlf = _tri_matmul(tri, dcq_ref[...] + dck_ref[...]) + carry[...]
        carry[...] = dlf[0:1, :]
        z = p_ref[...] + b_ref[...]
        lane = lax.broadcasted_iota(jnp.int32, (tb, LANES), 1)
        dz = jnp.where(lane < n_heads, dlf * jax.nn.sigmoid(-z), 0.0)
        dz_ref[...] = dz
        db_ref[...] += jnp.sum(dz, axis=0, keepdims=True)

    dz, db = pl.pallas_call(
        body, name=name, grid=(nb,),
        in_specs=[pl.BlockSpec((tb, LANES), lambda i: (nb - 1 - i, fcol)),
                  pl.BlockSpec((1, LANES), lambda i: (0, 0)),
                  pl.BlockSpec((tb, LANES), lambda i: (nb - 1 - i, 0)),
                  pl.BlockSpec((tb, LANES), lambda i: (nb - 1 - i, 0))],
        out_specs=[pl.BlockSpec((tb, LANES), lambda i: (nb - 1 - i, 0)),
                   pl.BlockSpec((1, LANES), lambda i: (0, 0))],
        out_shape=[jax.ShapeDtypeStruct((t, LANES), F32), jax.ShapeDtypeStruct((1, LANES), F32)],
        scratch_shapes=[pltpu.VMEM((1, LANES), F32)],
        compiler_params=_params("arbitrary"),
    )(proj, bf_pad, dcq, dck)
    return dz, db[0]


def _split3(x):
    hi = x.astype(BF16)
    r = x - hi.astype(F32)
    mid = r.astype(BF16)
    return hi, mid, (r - mid.astype(F32)).astype(BF16)


def _attn_fwd(qa, ka, va, dh, name, gather=()):
    nh, t, da = qa.shape
    tq = _pick(t, ROW_TILES)
    hb = 2 if nh % 2 == 0 else 1
    heads = range(hb)
    n = len(gather)
    steps = (nh // hb, t // tq)

    def body(q_ref, k_ref, v_ref, *rest):
        x_refs, o_ref, g_refs = rest[:n], rest[n], rest[n + 1:2 * n + 1]
        m_s, acc_s, s_a, s_b = rest[2 * n + 1:2 * n + 5]
        qi = pl.program_id(1)
        if n:
            start, forward, finish = _gather_phases(x_refs, g_refs, *rest[2 * n + 5:])
            at = lambda hh, qq: jnp.logical_and(pl.program_id(0) == hh, qi == qq)
            pl.when(at(0, 0))(start)
            pl.when(at(steps[0] // 2, 0))(forward)
        m_s[...] = jnp.full(m_s.shape, NEG_BIG, F32)
        acc_s[...] = jnp.zeros_like(acc_s)

        def scores(s_ref, j):
            off = pl.multiple_of(j * tq, tq)
            for g in heads:
                s_ref[g] = lax.dot_general(q_ref[g], k_ref[g, pl.ds(off, tq), :], NT, preferred_element_type=F32)

        def absorb(s_ref, j, diagonal):
            off = pl.multiple_of(j * tq, tq)
            s = [s_ref[g] for g in heads]
            if diagonal:
                row = lax.broadcasted_iota(jnp.int32, (tq, tq), 0)
                col = lax.broadcasted_iota(jnp.int32, (tq, tq), 1)
                s = [jnp.where(col > row, NEG_BIG, sg) for sg in s]
            m_prev = [m_s[g] for g in heads]
            m_new = [jnp.maximum(m_prev[g], jnp.max(s[g], axis=1, keepdims=True)) for g in heads]
            p = [jnp.exp(s[g] - m_new[g]).astype(BF16) for g in heads]
            pv = [jnp.dot(p[g], v_ref[g, pl.ds(off, tq), :], preferred_element_type=F32) for g in heads]
            for g in heads:
                acc_s[g] = jnp.exp(m_prev[g] - m_new[g]) * acc_s[g] + pv[g]
                m_s[g] = m_new[g]

        def two_blocks(r, carry):
            scores(s_b, 2 * r + 1)
            absorb(s_a, 2 * r, False)
            scores(s_a, 2 * r + 2)
            absorb(s_b, 2 * r + 1, False)
            return carry

        scores(s_a, 0)
        rounds = qi // 2
        lax.fori_loop(0, rounds, two_blocks, 0)

        @pl.when(qi % 2 == 0)
        def _():
            absorb(s_a, qi, True)

        @pl.when(qi % 2 == 1)
        def _():
            scores(s_b, qi)
            absorb(s_a, qi - 1, False)
            absorb(s_b, qi, True)

        lane = lax.broadcasted_iota(jnp.int32, (tq, da), 1)
        for g in heads:
            acc = acc_s[g]
            l = jnp.sum(jnp.where(lane == dh, acc, 0.0), axis=1, keepdims=True)
            o_ref[g] = jnp.where(lane == dh, m_s[g] + jnp.log(l), acc / l)
        if n:
            pl.when(at(steps[0] - 1, steps[1] - 1))(finish)

    full = pl.BlockSpec((hb, t, da), lambda h, qi: (h, 0, 0))
    blk = pl.BlockSpec((hb, tq, da), lambda h, qi: (h, qi, 0))
    return pl.pallas_call(
        body, name=name, grid=steps,
        in_specs=[blk, full, full] + [ANY] * n, out_specs=[blk] + [ANY] * n,
        out_shape=[jax.ShapeDtypeStruct((nh, t, da), F32)] + _gather_shapes(gather),
        scratch_shapes=[pltpu.VMEM((hb, tq, 1), F32), pltpu.VMEM((hb, tq, da), F32),
                        pltpu.VMEM((hb, tq, tq), F32), pltpu.VMEM((hb, tq, tq), F32)] + (_gather_sems(n) if n else []),
        compiler_params=_params("arbitrary", "arbitrary"),
    )(qa, ka, va, *gather)


def _attn_bwd(qa, ka, va, doa, name, exchange=()):
    nh, t, da = qa.shape
    tq = _pick(t, ROW_TILES)
    nq = t // tq
    n = len(exchange)

    def body(q_ref, do_ref, k_ref, v_ref, *rest):
        p_refs, (dq_ref, dk_ref, dv_ref), r_refs = rest[:n], rest[n:n + 3], rest[n + 3:2 * n + 3]
        kj = pl.program_id(1)
        if n:
            start, finish = _chip_exchange_phases(p_refs, r_refs, *rest[2 * n + 3:])
            pl.when(jnp.logical_and(pl.program_id(0) == 0, kj == 0))(start)

        @pl.when(kj == 0)
        def _():
            dq_ref[...] = jnp.zeros_like(dq_ref)

        dk_ref[...] = jnp.zeros_like(dk_ref)
        dv_ref[...] = jnp.zeros_like(dv_ref)
        kb = k_ref[...]
        vb = v_ref[...]

        def step(i, diagonal, blocks=1):
            off = pl.multiple_of(i * tq, tq)
            rows = blocks * tq
            qb = q_ref[pl.ds(off, rows), :]
            dob = do_ref[pl.ds(off, rows), :]
            st = lax.dot_general(kb, qb, NT, preferred_element_type=F32)
            if diagonal:
                row = lax.broadcasted_iota(jnp.int32, (tq, tq), 0)
                col = lax.broadcasted_iota(jnp.int32, (tq, tq), 1)
                st = jnp.where(row > col, NEG_BIG, st)
            pt = jnp.exp(st)
            dst = (pt * lax.dot_general(vb, dob, NT, preferred_element_type=F32)).astype(BF16)
            dv_ref[...] += jnp.dot(pt.astype(BF16), dob, preferred_element_type=F32)
            dk_ref[...] += jnp.dot(dst, qb, preferred_element_type=F32)
            dq_ref[pl.ds(off, rows), :] += lax.dot_general(dst, kb, TN, preferred_element_type=F32)

        step(kj, True)
        count = nq - 1 - kj
        one, two = count % 2, (count // 2) % 2

        @pl.when(one == 1)
        def _():
            step(kj + 1, False)

        @pl.when(two == 1)
        def _():
            step(kj + 1 + one, False, blocks=2)

        def loop(r, carry):
            step(kj + 1 + one + 2 * two + 4 * r, False, blocks=4)
            return carry

        lax.fori_loop(0, count // 4, loop, 0)
        if n:
            pl.when(jnp.logical_and(pl.program_id(0) == nh - 1, kj == nq - 1))(finish)

    full = pl.BlockSpec((None, t, da), lambda h, j: (h, 0, 0))
    blk = pl.BlockSpec((None, tq, da), lambda h, j: (h, j, 0))
    return pl.pallas_call(
        body, name=name, grid=(nh, nq),
        in_specs=[full, full, blk, blk] + [ANY] * n, out_specs=[full, blk, blk] + [ANY] * n,
        out_shape=[jax.ShapeDtypeStruct((nh, t, da), F32)] * 3 + _chip_exchange_shapes(exchange),
        scratch_shapes=_chip_exchange_sems(n) if n else [],
        compiler_params=_params("arbitrary", "arbitrary"),
    )(qa, doa, ka, va, *exchange)


def _head_group(dh, h):
    g = h // (LANES // dh)
    return slice(g * LANES, (g + 1) * LANES)


def _head_select(dh, h, to_heads):
    r = lax.broadcasted_iota(jnp.int32, (LANES, LANES), 0)
    c = lax.broadcasted_iota(jnp.int32, (LANES, LANES), 1)
    nat, col = (r, c) if to_heads else (c, r)
    return jnp.logical_and(nat == col + (h % (LANES // dh)) * dh, col < dh).astype(BF16)


def _column(x, lane, j):
    return jnp.sum(jnp.where(lane == j, x, 0.0), axis=1, keepdims=True)


def _bias_columns(lane, first, value):
    hi, mid, lo = [-term.astype(F32) for term in _split3(value)]
    return jnp.where(lane == first, hi, jnp.where(lane == first + 1, mid, jnp.where(lane == first + 2, lo, 0.0)))


def _attn_pack(proj, cgate, w, nh, scale, name):
    t = proj.shape[0]
    dh = w // nh
    tb = _pick(t, ROW_TILES)

    def body(q_ref, k_ref, v_ref, c_ref, qa_ref, ka_ref, va_ref):
        lane = lax.broadcasted_iota(jnp.int32, (tb, LANES), 1)
        ones_qv = jnp.where(jnp.logical_and(lane >= dh, lane < dh + 3), 1.0, 0.0)
        ones_k = jnp.where(jnp.logical_and(lane >= dh + 3, lane < dh + 7), 1.0, 0.0)
        qb = (q_ref[...] * scale).astype(BF16)
        kb = k_ref[...].astype(BF16)
        vb = v_ref[...].astype(BF16)
        cblk = c_ref[...]
        for h in range(nh):
            sel, grp = _head_select(dh, h, True), _head_group(dh, h)
            qa_ref[h] = (jnp.dot(qb[:, grp], sel, preferred_element_type=F32) + ones_qv).astype(BF16)
            va_ref[h] = (jnp.dot(vb[:, grp], sel, preferred_element_type=F32) + ones_qv).astype(BF16)
            bias = _bias_columns(lane, dh, _column(cblk, lane, h))
            ka_ref[h] = (jnp.dot(kb[:, grp], sel, preferred_element_type=F32) + bias + ones_k).astype(BF16)

    col = lambda j: pl.BlockSpec((tb, w), lambda i: (i, j))
    out = pl.BlockSpec((nh, tb, LANES), lambda i: (0, i, 0))
    return pl.pallas_call(
        body, name=name, grid=(t // tb,),
        in_specs=[col(0), col(1), col(2), pl.BlockSpec((tb, LANES), lambda i: (i, 0))],
        out_specs=[out, out, out],
        out_shape=[jax.ShapeDtypeStruct((nh, t, LANES), BF16)] * 3,
        compiler_params=_params("parallel"),
    )(proj, proj, proj, cgate)


def _attn_pack_bwd(dmix, oa, qa, w, nh, name):
    t = dmix.shape[0]
    dh = w // nh
    tb = _pick(t, ROW_TILES)

    def body(d_ref, oa_ref, qa_ref, doa_ref, qa2_ref):
        lane = lax.broadcasted_iota(jnp.int32, (tb, LANES), 1)
        db = d_ref[...].astype(BF16)
        for h in range(nh):
            do_h = jnp.dot(db[:, _head_group(dh, h)], _head_select(dh, h, True), preferred_element_type=F32)
            o_h = oa_ref[h]
            delta = jnp.sum(jnp.where(lane < dh, do_h * o_h, 0.0), axis=1, keepdims=True)
            doa_ref[h] = (do_h + _bias_columns(lane, dh, delta)).astype(BF16)
            qa2_ref[h] = (qa_ref[h].astype(F32) + _bias_columns(lane, dh + 4, _column(o_h, lane, dh))).astype(BF16)

    blk = pl.BlockSpec((nh, tb, LANES), lambda i: (0, i, 0))
    return pl.pallas_call(
        body, name=name, grid=(t // tb,),
        in_specs=[pl.BlockSpec((tb, w), lambda i: (i, 0)), blk, blk], out_specs=[blk, blk],
        out_shape=[jax.ShapeDtypeStruct((nh, t, LANES), BF16)] * 2,
        compiler_params=_params("parallel"),
    )(dmix, oa, qa)


def _attn_unpack(xa, w, nh, mult, sum_col, sum_sign, name):
    t = xa.shape[1]
    dh = w // nh
    tb = _pick(t, ROW_TILES)

    def body(x_ref, o_ref, *rest):
        lane = lax.broadcasted_iota(jnp.int32, (tb, LANES), 1)
        per = LANES // dh
        cols = jnp.zeros((tb, LANES), F32)
        for h0 in range(0, nh, per):
            acc = jnp.zeros((tb, LANES), F32)
            for h in range(h0, h0 + per):
                xh = x_ref[h]
                acc = acc + jnp.dot((xh * mult).astype(BF16), _head_select(dh, h, False), preferred_element_type=F32)
                if sum_col is not None:
                    cols = cols + jnp.where(lane == h, sum_sign * _column(xh, lane, sum_col), 0.0)
            o_ref[:, _head_group(dh, h0)] = acc.astype(BF16)
        if sum_col is not None:
            rest[0][...] = cols

    nat = pl.BlockSpec((tb, w), lambda i: (i, 0))
    lanes = pl.BlockSpec((tb, LANES), lambda i: (i, 0))
    return pl.pallas_call(
        body, name=name, grid=(t // tb,),
        in_specs=[pl.BlockSpec((nh, tb, LANES), lambda i: (0, i, 0))],
        out_specs=[nat, lanes] if sum_col is not None else [nat],
        out_shape=[jax.ShapeDtypeStruct((t, w), BF16)] + ([jax.ShapeDtypeStruct((t, LANES), F32)]
                                                            if sum_col is not None else []),
        compiler_params=_params("parallel"),
    )(xa)


def _conv_fwd(proj, cw, w, bcol, name):
    t = proj.shape[0]
    tb = _pick(t, ROW_TILES)
    hb = tb // SUBLANES

    def body(b_ref, c_ref, h_ref, cp_ref, hp_ref, w_ref, y_ref):
        i = pl.program_id(0)
        zp = jnp.where(i > 0, cp_ref[...] * hp_ref[...], 0.0)
        zext = jnp.concatenate([zp, c_ref[...] * h_ref[...]], axis=0)
        z1 = pltpu.roll(zext, 1, 0)[SUBLANES:]
        z2 = pltpu.roll(zext, 2, 0)[SUBLANES:]
        y = w_ref[2:3, :] * zext[SUBLANES:] + w_ref[1:2, :] * z1 + w_ref[0:1, :] * z2
        y_ref[...] = (b_ref[...] * y).astype(BF16)

    cur = lambda j: pl.BlockSpec((tb, w), lambda i: (i, bcol + j))
    prev = lambda j: pl.BlockSpec((SUBLANES, w), lambda i: (jnp.maximum(i * hb - 1, 0), bcol + j))
    return pl.pallas_call(
        body, name=name, grid=(t // tb,),
        in_specs=[cur(0), cur(1), cur(2), prev(1), prev(2), pl.BlockSpec(cw.shape, lambda i: (0, 0))],
        out_specs=pl.BlockSpec((tb, w), lambda i: (i, 0)),
        out_shape=jax.ShapeDtypeStruct((t, w), BF16), compiler_params=_params("parallel"),
    )(proj, proj, proj, proj, proj, cw)


def _conv_bwd(proj, cw, dmix, w, bcol, name):
    t = proj.shape[0]
    tb = _pick(t, ROW_TILES)
    hb = tb // SUBLANES
    nb = t // tb
    n_ext = tb + SUBLANES

    def body(b_ref, c_ref, h_ref, cp_ref, hp_ref, bn_ref, d_ref, dn_ref, w_ref, db_ref, dc_ref, dh_ref, dw_ref):
        i = pl.program_id(0)
        c = c_ref[...]
        hh = h_ref[...]
        zp = jnp.where(i > 0, cp_ref[...] * hp_ref[...], 0.0)
        zext = jnp.concatenate([zp, c * hh], axis=0)
        z0 = zext[SUBLANES:]
        z1 = pltpu.roll(zext, 1, 0)[SUBLANES:]
        z2 = pltpu.roll(zext, 2, 0)[SUBLANES:]
        y = w_ref[2:3, :] * z0 + w_ref[1:2, :] * z1 + w_ref[0:1, :] * z2
        d = d_ref[...]
        db_ref[...] = d * y
        dy = d * b_ref[...]
        dyn = jnp.where(i < nb - 1, dn_ref[...] * bn_ref[...], 0.0)
        dext = jnp.concatenate([dy, dyn], axis=0)
        dy1 = pltpu.roll(dext, n_ext - 1, 0)[:tb]
        dy2 = pltpu.roll(dext, n_ext - 2, 0)[:tb]
        dz = w_ref[2:3, :] * dy + w_ref[1:2, :] * dy1 + w_ref[0:1, :] * dy2
        dc_ref[...] = dz * hh
        dh_ref[...] = dz * c

        @pl.when(i == 0)
        def _():
            dw_ref[...] = jnp.zeros_like(dw_ref)

        dw_ref[0:1, :] += jnp.sum(dy * z2, axis=0, keepdims=True)
        dw_ref[1:2, :] += jnp.sum(dy * z1, axis=0, keepdims=True)
        dw_ref[2:3, :] += jnp.sum(dy * z0, axis=0, keepdims=True)

    cur = lambda j: pl.BlockSpec((tb, w), lambda i: (i, bcol + j))
    prev = lambda j: pl.BlockSpec((SUBLANES, w), lambda i: (jnp.maximum(i * hb - 1, 0), bcol + j))
    nxt = lambda col: pl.BlockSpec((SUBLANES, w), lambda i: (jnp.minimum((i + 1) * hb, nb * hb - 1), col))
    out = pl.BlockSpec((tb, w), lambda i: (i, 0))
    return pl.pallas_call(
        body, name=name, grid=(nb,),
        in_specs=[cur(0), cur(1), cur(2), prev(1), prev(2), nxt(bcol),
                  pl.BlockSpec((tb, w), lambda i: (i, 1)), nxt(1), pl.BlockSpec(cw.shape, lambda i: (0, 0))],
        out_specs=[out, out, out, pl.BlockSpec(cw.shape, lambda i: (0, 0))],
        out_shape=[jax.ShapeDtypeStruct((t, w), F32)] * 3 + [jax.ShapeDtypeStruct(cw.shape, F32)],
        compiler_params=_params("arbitrary"),
    )(proj, proj, proj, proj, proj, proj, dmix, dmix, cw)


SQRT_HALF = 0.7071067811865476
INV_SQRT_2PI = 0.3989422804014327


def _gelu(x):
    return 0.5 * x * (1.0 + lax.erf(x * SQRT_HALF))


def _gelu_grad(x):
    return 0.5 * (1.0 + lax.erf(x * SQRT_HALF)) + x * (INV_SQRT_2PI * jnp.exp(-0.5 * x * x))


def _sgu_fwd(uv, ln_g, ln_b, wm, bs_full, name):
    t, d2 = uv.shape
    d = d2 // 2
    ng, pb, _ = wm.shape
    gd = d // ng
    tb = _pick(t, ROW_TILES[1:] or ROW_TILES)
    assert tb % pb == 0

    def body(uv_ref, g_ref, b_ref, w_ref, bs_ref, o_ref):
        u = _gelu(uv_ref[:, :d])
        v = _gelu(uv_ref[:, d:])
        mu = jnp.mean(v, axis=-1, keepdims=True)
        vc = v - mu
        var = jnp.mean(vc * vc, axis=-1, keepdims=True)
        vn = (vc * lax.rsqrt(var + LN_EPS) * g_ref[...] + b_ref[...]).astype(BF16)
        for r in range(tb // pb):
            rows = slice(r * pb, (r + 1) * pb)
            for gi in range(ng):
                cols = slice(gi * gd, (gi + 1) * gd)
                s = jnp.dot(w_ref[gi], vn[rows, cols], preferred_element_type=F32) + bs_ref[:, cols]
                o_ref[rows, cols] = (u[rows, cols] * s).astype(BF16)

    vec = pl.BlockSpec((1, d), lambda i: (0, 0))
    return pl.pallas_call(
        body, name=name, grid=(t // tb,),
        in_specs=[pl.BlockSpec((tb, d2), lambda i: (i, 0)), vec, vec,
                  pl.BlockSpec(wm.shape, lambda i: (0, 0, 0)), pl.BlockSpec((pb, d), lambda i: (0, 0))],
        out_specs=pl.BlockSpec((tb, d), lambda i: (i, 0)),
        out_shape=jax.ShapeDtypeStruct((t, d), BF16), compiler_params=_params("parallel"),
    )(uv, ln_g.reshape(1, d), ln_b.reshape(1, d), wm, bs_full)


def _sgu_bwd(uv, ln_g, ln_b, wm, bs_full, dgated, name):
    t, d2 = uv.shape
    d = d2 // 2
    ng, pb, _ = wm.shape
    gd = d // ng
    tb = _pick(t, ROW_TILES[1:] or ROW_TILES)
    nb = t // tb

    def body(uv_ref, g_ref, b_ref, w_ref, bs_ref, dg_ref, o_ref, dw_ref, dbs_ref, dlg_ref, dlb_ref,
             du_s, dvn_s, dbs_s):
        i = pl.program_id(0)

        @pl.when(i == 0)
        def _():
            dw_ref[...] = jnp.zeros_like(dw_ref)
            dbs_s[...] = jnp.zeros_like(dbs_s)
            dlg_ref[...] = jnp.zeros_like(dlg_ref)
            dlb_ref[...] = jnp.zeros_like(dlb_ref)

        upre = uv_ref[:, :d]
        vpre = uv_ref[:, d:]
        u = _gelu(upre)
        v = _gelu(vpre)
        mu = jnp.mean(v, axis=-1, keepdims=True)
        vc = v - mu
        var = jnp.mean(vc * vc, axis=-1, keepdims=True)
        rstd = lax.rsqrt(var + LN_EPS)
        xhat = vc * rstd
        vn = (xhat * g_ref[...] + b_ref[...]).astype(BF16)
        dgt = dg_ref[...].astype(F32)
        for r in range(tb // pb):
            rows = slice(r * pb, (r + 1) * pb)
            for gi in range(ng):
                cols = slice(gi * gd, (gi + 1) * gd)
                vblk = vn[rows, cols]
                s = jnp.dot(w_ref[gi], vblk, preferred_element_type=F32) + bs_ref[:, cols]
                dblk = dgt[rows, cols]
                du_s[rows, cols] = dblk * s
                ds = dblk * u[rows, cols]
                dsb = ds.astype(BF16)
                dvn_s[rows, cols] = lax.dot_general(w_ref[gi], dsb, (((0,), (0,)), ((), ())),
                                                    preferred_element_type=F32)
                dw_ref[gi] += lax.dot_general(dsb, vblk, (((1,), (1,)), ((), ())), preferred_element_type=F32)
                dbs_s[:, cols] += ds
        dvn = dvn_s[...]
        dlg_ref[...] += jnp.sum(dvn * xhat, axis=0, keepdims=True)
        dlb_ref[...] += jnp.sum(dvn, axis=0, keepdims=True)
        dxh = dvn * g_ref[...]
        m1 = jnp.mean(dxh, axis=-1, keepdims=True)
        m2 = jnp.mean(dxh * xhat, axis=-1, keepdims=True)
        dv = rstd * (dxh - m1 - xhat * m2)
        o_ref[:, :d] = (du_s[...] * _gelu_grad(upre)).astype(BF16)
        o_ref[:, d:] = (dv * _gelu_grad(vpre)).astype(BF16)

        @pl.when(i == nb - 1)
        def _():
            lane = lax.broadcasted_iota(jnp.int32, (pb, LANES), 1)
            acc = jnp.zeros((pb, LANES), F32)
            for gi in range(ng):
                col = jnp.sum(dbs_s[:, gi * gd:(gi + 1) * gd], axis=1, keepdims=True)
                acc = acc + jnp.where(lane == gi, col, 0.0)
            dbs_ref[...] = acc

    vec = pl.BlockSpec((1, d), lambda i: (0, 0))
    duv, dw, dbs, dlg, dlb = pl.pallas_call(
        body, name=name, grid=(nb,),
        in_specs=[pl.BlockSpec((tb, d2), lambda i: (i, 0)), vec, vec,
                  pl.BlockSpec(wm.shape, lambda i: (0, 0, 0)), pl.BlockSpec((pb, d), lambda i: (0, 0)),
                  pl.BlockSpec((tb, d), lambda i: (i, 0))],
        out_specs=[pl.BlockSpec((tb, d2), lambda i: (i, 0)), pl.BlockSpec(wm.shape, lambda i: (0, 0, 0)),
                   pl.BlockSpec((pb, LANES), lambda i: (0, 0)), vec, vec],
        out_shape=[jax.ShapeDtypeStruct((t, d2), BF16), jax.ShapeDtypeStruct(wm.shape, F32),
                   jax.ShapeDtypeStruct((pb, LANES), F32), jax.ShapeDtypeStruct((1, d), F32),
                   jax.ShapeDtypeStruct((1, d), F32)],
        scratch_shapes=[pltpu.VMEM((tb, d), F32), pltpu.VMEM((tb, d), F32), pltpu.VMEM((pb, d), F32)],
        compiler_params=_params("arbitrary"),
    )(uv, ln_g.reshape(1, d), ln_b.reshape(1, d), wm, bs_full, dgated)
    return duv, dw, dbs, dlg[0], dlb[0]


def _adamw(w, g, m, v, name):
    shape = w.shape
    cols = shape[-1]
    rows = w.size // cols
    tr = _pick(rows, (512, 256, 352, 128, 64, 32, 16, 8))

    def body(w_ref, g_ref, m_ref, v_ref, d_ref, mo_ref, vo_ref):
        d_ref[...], mo_ref[...], vo_ref[...] = _adam_update(w_ref[...], g_ref[...], m_ref[...], v_ref[...])

    spec = pl.BlockSpec((tr, cols), lambda i: (i, 0))
    outs = pl.pallas_call(
        body, name=name, grid=(rows // tr,),
        in_specs=[spec] * 4, out_specs=[spec] * 3,
        out_shape=[jax.ShapeDtypeStruct((rows, cols), F32)] * 3,
        compiler_params=_params("parallel"),
    )(*[a.reshape(rows, cols) for a in (w, g, m, v)])
    return [o.reshape(shape) for o in outs]


ANY = pl.BlockSpec(memory_space=pl.ANY)


def _place():
    return lax.axis_index("x"), lax.axis_index("y"), lax.axis_index("c")


def _all_gather(shards, name):
    n = len(shards)

    def body(*refs):
        start, forward, finish = _gather_phases(refs[:n], refs[n:2 * n], *refs[2 * n:])
        start()
        forward()
        finish()

    return pl.pallas_call(
        body, name=name, in_specs=[ANY] * n, out_specs=[ANY] * n,
        out_shape=_gather_shapes(shards), scratch_shapes=_gather_sems(n),
    )(*shards)


def _gather_shapes(shards):
    return [jax.ShapeDtypeStruct((N_DEV,) + s.shape, s.dtype) for s in shards]


def _gather_sems(n):
    return [pltpu.SemaphoreType.DMA((7 * n,)), pltpu.SemaphoreType.DMA((7 * n,)), pltpu.SemaphoreType.DMA((n,))]


def _gather_phases(x_refs, out_refs, send_sems, recv_sems, local_sems):
    n = len(x_refs)
    x, y, c = _place()
    me, sibling = (x, y, c), (x, y, 1 - c)
    chips = [(1 - x, y), (x, 1 - y), (1 - x, 1 - y)]

    def copy(a, k, block, to, own=False):
        px, py, pc = block
        rows = out_refs[a].at[4 * px + 2 * py + pc]
        return pltpu.make_async_remote_copy(
            src_ref=x_refs[a] if own else rows, dst_ref=rows,
            send_sem=send_sems.at[7 * a + k], recv_sem=recv_sems.at[7 * a + k],
            device_id=to, device_id_type=MESH)

    def local(a):
        return pltpu.make_async_copy(x_refs[a], out_refs[a].at[4 * x + 2 * y + c], local_sems.at[a])

    def first(a):
        return [copy(a, 0, me, sibling, own=True)] + [copy(a, 1 + j, me, (*chip, c), own=True)
                                                      for j, chip in enumerate(chips)]

    def start():
        for a in range(n):
            local(a).start()
            for cp in first(a):
                cp.start()

    def forward():
        for j, chip in enumerate(chips):
            for a in range(n):
                copy(a, 1 + j, (*chip, c), me).wait_recv()
                copy(a, 4 + j, (*chip, c), sibling).start()

    def finish():
        for a in range(n):
            copy(a, 0, sibling, me).wait_recv()
            for j, chip in enumerate(chips):
                copy(a, 4 + j, (*chip, 1 - c), me).wait_recv()
        for a in range(n):
            for cp in first(a) + [copy(a, 4 + j, (*chip, c), sibling) for j, chip in enumerate(chips)]:
                cp.wait_send()
            local(a).wait()

    return start, forward, finish


def _rs_sibling_exchange(packed, name):
    rider = _sibling_exchange_rider(packed)
    n = len(packed)

    def body(*refs):
        start, finish = rider.phases(refs[:n], refs[n:2 * n], refs[2 * n:])
        start()
        finish()

    return pl.pallas_call(
        body, name=name, in_specs=rider.in_specs, out_specs=rider.out_specs, out_shape=rider.out_shapes,
        scratch_shapes=rider.scratch,
    )(*packed)


def _sibling_exchange_rider(packed):
    n = len(packed)

    def phases(p_refs, r_refs, scratch):
        send_sems, recv_sems = scratch
        x, y, c = _place()

        def copies():
            return [pltpu.make_async_remote_copy(
                src_ref=p_refs[a].at[2 * j + (1 - c)], dst_ref=r_refs[a].at[j],
                send_sem=send_sems.at[4 * a + j], recv_sem=recv_sems.at[4 * a + j],
                device_id=(x, y, 1 - c), device_id_type=MESH) for a in range(n) for j in range(4)]

        def start():
            for cp in copies():
                cp.start()

        def finish():
            for cp in copies():
                cp.wait()

        return start, finish

    return _Rider(list(packed), [ANY] * n, [jax.ShapeDtypeStruct((4,) + p.shape[1:], p.dtype) for p in packed],
                  [ANY] * n, [pltpu.SemaphoreType.DMA((4 * n,)), pltpu.SemaphoreType.DMA((4 * n,))], phases)


def _rs_chip_sum(packed, from_sibling, c_idx, name):
    _, r, cc = packed.shape
    tr = _pick(r, (512, 256, 352, 128))

    def body(c_ref, a_ref, b_ref, o_ref):
        o_ref[...] = (a_ref[...].astype(F32) + b_ref[...].astype(F32)).astype(o_ref.dtype)

    return pl.pallas_call(
        body, name=name,
        grid_spec=pltpu.PrefetchScalarGridSpec(
            num_scalar_prefetch=1, grid=(4, r // tr),
            in_specs=[pl.BlockSpec((None, tr, cc), lambda j, i, c_ref: (2 * j + c_ref[0], i, 0)),
                      pl.BlockSpec((None, tr, cc), lambda j, i, c_ref: (j, i, 0))],
            out_specs=pl.BlockSpec((None, tr, cc), lambda j, i, c_ref: (j, i, 0))),
        out_shape=jax.ShapeDtypeStruct((4, r, cc), packed.dtype),
        compiler_params=_params("parallel", "parallel"),
    )(c_idx, packed, from_sibling)


def _chip_exchange_shapes(partial):
    return [jax.ShapeDtypeStruct((3,) + p.shape[1:], p.dtype) for p in partial]


def _chip_exchange_sems(n):
    return [pltpu.SemaphoreType.DMA((3 * n,)), pltpu.SemaphoreType.DMA((3 * n,))]


def _chip_exchange_phases(p_refs, r_refs, send_sems, recv_sems):
    x, y, c = _place()
    chips = [(1 - x, y), (x, 1 - y), (1 - x, 1 - y)]

    def copies():
        return [pltpu.make_async_remote_copy(
            src_ref=p_refs[a].at[2 * tx + ty], dst_ref=r_refs[a].at[k],
            send_sem=send_sems.at[3 * a + k], recv_sem=recv_sems.at[3 * a + k],
            device_id=(tx, ty, c), device_id_type=MESH)
            for a in range(len(p_refs)) for k, (tx, ty) in enumerate(chips)]

    def start():
        for cp in copies():
            cp.start()

    def finish():
        for cp in copies():
            cp.wait()

    return start, finish


def _adam_update(w, g, m, v):
    mn = ADAM_B1 * m + (1.0 - ADAM_B1) * g
    vn = ADAM_B2 * v + (1.0 - ADAM_B2) * (g * g)
    m_hat = mn / (1.0 - ADAM_B1 ** ADAM_STEP)
    v_hat = vn / (1.0 - ADAM_B2 ** ADAM_STEP)
    return -ADAM_LR * (m_hat / (jnp.sqrt(v_hat) + ADAM_EPS) + ADAM_WD * w), mn, vn


def _rs_final_adamw(partial, received, chip_idx, w, m, v, name):
    _, r, cc = partial.shape
    tr = _pick(r, (512, 256, 352, 128))

    def body(c_ref, a_ref, r_ref, w_ref, m_ref, v_ref, g_ref, d_ref, mo_ref, vo_ref):
        g = a_ref[...].astype(F32)
        for k in range(3):
            g = g + r_ref[k].astype(F32)
        g_ref[...] = g
        d_ref[...], mo_ref[...], vo_ref[...] = _adam_update(w_ref[...], g, m_ref[...], v_ref[...])

    row = pl.BlockSpec((tr, cc), lambda i, c_ref: (i, 0))
    return pl.pallas_call(
        body, name=name,
        grid_spec=pltpu.PrefetchScalarGridSpec(
            num_scalar_prefetch=1, grid=(r // tr,),
            in_specs=[pl.BlockSpec((None, tr, cc), lambda i, c_ref: (c_ref[0], i, 0)),
                      pl.BlockSpec((3, tr, cc), lambda i, c_ref: (0, i, 0)), row, row, row],
            out_specs=[row] * 4),
        out_shape=[jax.ShapeDtypeStruct((r, cc), F32)] * 4,
        compiler_params=_params("parallel"),
    )(chip_idx, partial, received, w.reshape(r, cc), m.reshape(r, cc), v.reshape(r, cc))


def _all_reduce_small(vals, name):
    rider = _all_reduce_rider(vals)

    def body(v_ref, o_ref, *scratch):
        start, finish = rider.phases([v_ref], [o_ref], scratch)
        start()
        finish()

    return pl.pallas_call(
        body, name=name, in_specs=rider.in_specs, out_specs=rider.out_specs[0], out_shape=rider.out_shapes[0],
        scratch_shapes=rider.scratch, compiler_params=pltpu.CompilerParams(vmem_limit_bytes=VMEM_LIMIT),
    )(vals)


def _all_reduce_rider(vals):
    r, cc = vals.shape

    def phases(ins, outs, scratch):
        (v_ref,), (o_ref,), (buf, send_sems, recv_sems) = ins, outs, scratch
        x, y, c = _place()
        me = 4 * x + 2 * y + c

        def copies():
            cps = []
            for k in range(1, N_DEV):
                kx, ky, kc = (k >> 2) & 1, (k >> 1) & 1, k & 1
                peer = (1 - x if kx else x, 1 - y if ky else y, 1 - c if kc else c)
                cps.append(pltpu.make_async_remote_copy(
                    src_ref=buf.at[0], dst_ref=buf.at[k], send_sem=send_sems.at[k - 1],
                    recv_sem=recv_sems.at[k - 1], device_id=peer, device_id_type=MESH))
            return cps

        def start():
            buf[0] = v_ref[...]
            for cp in copies():
                cp.start()

        def finish():
            for cp in copies():
                cp.wait()
            acc = buf[jnp.bitwise_xor(me, 0)]
            for dev in range(1, N_DEV):
                acc = acc + buf[jnp.bitwise_xor(me, dev)]
            o_ref[...] = acc

        return start, finish

    vm = pl.BlockSpec(memory_space=pltpu.VMEM)
    return _Rider([vals], [vm], [jax.ShapeDtypeStruct((r, cc), F32)], [vm],
                  [pltpu.VMEM((N_DEV, r, cc), F32), pltpu.SemaphoreType.DMA((7,)), pltpu.SemaphoreType.DMA((7,))],
                  phases)


def _chip_exchange_rider(partial):
    n = len(partial)
    return _Rider(list(partial), [ANY] * n, _chip_exchange_shapes(partial), [ANY] * n, _chip_exchange_sems(n),
                  lambda ins, outs, scratch: _chip_exchange_phases(ins, outs, *scratch))


def _lanes(flat):
    pad = (-flat.shape[0]) % (SUBLANES * LANES)
    return jnp.pad(flat, (0, pad)).reshape(-1, LANES)


def kernel(x, even_w_in, even_b_f, even_conv_w, even_w_out, odd_w_in, odd_v_ln_g, odd_v_ln_b, odd_w_s, odd_b_s, odd_w_out, mix_ln_g, mix_ln_b, ffn_w_in, ffn_w_out, ffn_ln_g, ffn_ln_b, loss_target, m_even_w_in, m_even_b_f, m_even_conv_w, m_even_w_out, m_odd_w_in, m_odd_v_ln_g, m_odd_v_ln_b, m_odd_w_s, m_odd_b_s, m_odd_w_out, m_mix_ln_g, m_mix_ln_b, m_ffn_w_in, m_ffn_w_out, m_ffn_ln_g, m_ffn_ln_b, v_even_w_in, v_even_b_f, v_even_conv_w, v_even_w_out, v_odd_w_in, v_odd_v_ln_g, v_odd_v_ln_b, v_odd_w_s, v_odd_b_s, v_odd_w_out, v_mix_ln_g, v_mix_ln_b, v_ffn_w_in, v_ffn_w_out, v_ffn_ln_g, v_ffn_ln_b):
    t, d = x.shape[1], x.shape[2]
    nh = even_b_f.shape[-1]
    w = even_conv_w.shape[-1] * N_DEV
    dh = w // nh
    scale = dh ** -0.5
    e_in = even_w_in.shape[-1] * N_DEV
    f2 = ffn_w_in.shape[-1] * N_DEV
    f = f2 // 2
    ng, pb = odd_w_s.shape[1], odd_w_s.shape[2]
    assert e_in == 6 * w + nh and nh <= SUBLANES and (6 * w) % LANES == 0 and d % N_DEV == 0
    mx, my, mc = _place()
    me = 4 * mx + 2 * my + mc

    big = [even_w_in[0], even_w_out[0], odd_w_in[0], odd_w_out[0],
           ffn_w_in[0], ffn_w_in[1], ffn_w_out[0], ffn_w_out[1]]
    g_in0, = _all_gather([big[0].astype(BF16)], "ag_even_w_in")
    w_in0 = g_in0.transpose(1, 0, 2).reshape(d, e_in)
    w_all0 = jnp.concatenate([w_in0[:, :3 * w], w_in0[:, 3 * w + nh:], w_in0[:, 3 * w:3 * w + nh],
                              jnp.zeros((d, LANES - nh), BF16)], axis=1)

    cs, vs = even_conv_w.shape[-1], odd_v_ln_g.shape[-1]
    small_mine = jnp.concatenate([
        lax.dynamic_update_slice(jnp.zeros((3, w), F32), even_conv_w[0], (0, me * cs)).reshape(-1),
        lax.dynamic_update_slice(jnp.zeros((d,), F32), odd_v_ln_g[0], (me * vs,)),
        lax.dynamic_update_slice(jnp.zeros((d,), F32), odd_v_ln_b[0], (me * vs,))])
    small_all = _all_reduce_small(_lanes(small_mine), "ag_small").reshape(-1)
    conv_w = small_all[:3 * w].reshape(3, w)
    vln_g = small_all[3 * w:3 * w + d]
    vln_b = small_all[3 * w + d:3 * w + 2 * d]

    bf_pad = jnp.pad(even_b_f[0], (0, LANES - nh)).reshape(1, LANES)
    chunk = jnp.arange(pb) // (pb // 2)
    ws_mask = (chunk[None, :] <= chunk[:, None])[None]
    wm = jnp.where(ws_mask, odd_w_s[0], 0.0).astype(BF16)
    bs_full = jnp.repeat(odd_b_s[0].T, d // ng, axis=1)

    x0 = x[0]
    tgt = loss_target[0]
    fcol = 6 * w // LANES
    x0b = x0.astype(BF16)
    p0 = _mm(x0b, w_all0, "nn", F32, "l0_in_proj")
    cgate = _fgate_fwd(p0, bf_pad, fcol, nh, "l0_fgate")
    assert dh + 7 <= LANES
    qa, ka, va = _attn_pack(p0, cgate, w, nh, scale, "l0_attn_pack")
    oa, g_out0, g_in1, g_out1, g_fi0, g_fi1, g_fo0, g_fo1 = _attn_fwd(
        qa, ka, va, dh, "l0_attn", gather=[s.astype(BF16) for s in big[1:]])
    w_out0, w_out1 = g_out0.reshape(2 * w, d), g_out1.reshape(d, d)
    w_fo0, w_fo1 = g_fo0.reshape(f, d), g_fo1.reshape(f, d)
    nb = N_DEV // 2
    w_fi0, w_fi1 = g_fi0.reshape(2, nb, d, -1), g_fi1.reshape(2, nb, d, -1)
    attn, = _attn_unpack(oa, w, nh, 1.0, None, 1.0, "l0_attn_unpack")
    yconv = _conv_fwd(p0, conv_w, w, 3, "l0_conv")
    mix = jnp.concatenate([attn, yconv], axis=1)
    m0, x1, x1b = _mm_ln(mix, w_out0, x0, mix_ln_g[0], mix_ln_b[0], "l0_out_proj_ln")
    h0, gu0 = _ffn_in_swiglu(x1b, w_fi0, "l0_ffn_in")
    f0, x2, x2b = _mm_ln(h0, w_fo0, x1, ffn_ln_g[0], ffn_ln_b[0], "l0_ffn_out_ln")

    uv = _mm_cols_fwd(x2b, g_in1, False, F32, "l1_in_proj")
    gated = _sgu_fwd(uv, vln_g, vln_b, wm, bs_full, "l1_sgu")
    m1, x3, x3b = _mm_ln(gated, w_out1, x2, mix_ln_g[1], mix_ln_b[1], "l1_out_proj_ln")
    h1, gu1 = _ffn_in_swiglu(x3b, w_fi1, "l1_ffn_in")
    dz4, dz4b, g_ffn_g1, g_ffn_b1, loss_part = _mm_ln(h1, w_fo1, x3, ffn_ln_g[1], ffn_ln_b[1],
                                                      "l1_ffn_out_ln_loss", target=tgt)
    gd_fo1 = _mm_blk_dw(h1, dz4b, BF16, "l1_ffn_out_dw").reshape(N_DEV, -1, d)
    dgu1 = _ffn_out_dx_swiglu(dz4b, w_fo1, gu1, "l1_ffn_out_dx").reshape(N_DEV, t, -1)
    gd_fi1 = _mm_cols_dw(x3b, dgu1, N_DEV, True, BF16, "l1_ffn_in_dw")
    dz3, dz3b, g_mix_g1, g_mix_b1 = _mm_cols_dx_ln_bwd(dgu1, g_fi1, True, x2, m1, mix_ln_g[1], dz4,
                                                       "l1_ffn_in_dx_ln_bwd")
    gd_out1 = _mm(gated, dz3b, "tn", BF16, "l1_out_proj_dw").reshape(N_DEV, -1, d)
    dgated = _mm(dz3b, w_out1, "nt", BF16, "l1_out_proj_dx")
    duv, g_wm, g_bs_t, g_vln_g, g_vln_b = _sgu_bwd(uv, vln_g, vln_b, wm, bs_full, dgated, "l1_sgu_bwd")
    gd_in1 = _mm_cols_dw(x2b, duv, N_DEV, False, BF16, "l1_in_proj_dw")

    dz2, dz2b, g_ffn_g0, g_ffn_b0 = _mm_cols_dx_ln_bwd(duv, g_in1, False, x1, f0, ffn_ln_g[0], dz3,
                                                       "l1_in_proj_dx_ln_bwd")
    gd_fo0 = _mm_blk_dw(h0, dz2b, BF16, "l0_ffn_out_dw").reshape(N_DEV, -1, d)
    dgu0 = _ffn_out_dx_swiglu(dz2b, w_fo0, gu0, "l0_ffn_out_dx").reshape(N_DEV, t, -1)
    gd_fi0 = _mm_cols_dw(x1b, dgu0, N_DEV, True, BF16, "l0_ffn_in_dw")
    dz1, dz1b, g_mix_g0, g_mix_b0 = _mm_cols_dx_ln_bwd(dgu0, g_fi0, True, x0, m0, mix_ln_g[0], dz2,
                                                       "l0_ffn_in_dx_ln_bwd")
    gd_out0 = _mm(mix, dz1b, "tn", BF16, "l0_out_proj_dw").reshape(N_DEV, -1, d)
    early_g = [gd_out0, gd_in1, gd_out1, gd_fi0, gd_fi1, gd_fo0, gd_fo1]
    dmix, *early_sib = _mm(dz1b, w_out0, "nt", F32, "l0_out_proj_dx", rider=_sibling_exchange_rider(early_g))
    d_b, d_c, d_h, g_conv = _conv_bwd(p0, conv_w, dmix, w, 3, "l0_conv_bwd")
    doa, qa2 = _attn_pack_bwd(dmix, oa, qa, w, nh, "l0_attn_pack_bwd")
    big_names = ["even_w_in", "even_w_out", "odd_w_in", "odd_w_out", "ffn_w_in0", "ffn_w_in1", "ffn_w_out0", "ffn_w_out1"]
    c_idx = mc.reshape(1).astype(jnp.int32)
    chip_idx = (2 * mx + my).reshape(1).astype(jnp.int32)
    early_partial = [_rs_chip_sum(g, s, c_idx, "rs_chip_sum_" + n)
                     for g, s, n in zip(early_g, early_sib, big_names[1:])]
    dqa, dka, dva, *early_received = _attn_bwd(qa2, ka, va, doa, "l0_attn_bwd", exchange=early_partial)
    dq, dcq = _attn_unpack(dqa, w, nh, scale, dh + 3, 1.0, "l0_attn_unpack_dq")
    dk, dck = _attn_unpack(dka, w, nh, 1.0, dh, -1.0, "l0_attn_unpack_dk")
    dv, = _attn_unpack(dva, w, nh, 1.0, None, 1.0, "l0_attn_unpack_dv")
    dzf, g_bf = _fgate_bwd(p0, bf_pad, dcq, dck, fcol, nh, "l0_fgate_bwd")
    dp0 = jnp.concatenate([dq, dk, dv, d_b.astype(BF16), d_c.astype(BF16), d_h.astype(BF16), dzf.astype(BF16)], axis=1)
    g_ws = jnp.where(ws_mask, g_wm, 0.0)
    g_bs = g_bs_t[:, :ng].T
    small_g = [g_bf[:nh], g_conv, g_vln_g, g_vln_b, g_ws, g_bs,
               jnp.stack([g_mix_g0, g_mix_g1]), jnp.stack([g_mix_b0, g_mix_b1]),
               jnp.stack([g_ffn_g0, g_ffn_g1]), jnp.stack([g_ffn_b0, g_ffn_b1])]
    small_rider = _all_reduce_rider(_lanes(jnp.concatenate([a.reshape(-1) for a in small_g])))
    g_all0, small_sum = _mm(x0b, dp0, "tn", F32, "l0_in_proj_dw", rider=small_rider)
    gd_in0 = jnp.concatenate([g_all0[:, :3 * w], g_all0[:, 6 * w:6 * w + nh], g_all0[:, 3 * w:6 * w]], axis=1)
    gd_in0 = gd_in0.reshape(d, N_DEV, -1).transpose(1, 0, 2).astype(BF16)

    big_m = [m_even_w_in[0], m_even_w_out[0], m_odd_w_in[0], m_odd_w_out[0],
             m_ffn_w_in[0], m_ffn_w_in[1], m_ffn_w_out[0], m_ffn_w_out[1]]
    big_v = [v_even_w_in[0], v_even_w_out[0], v_odd_w_in[0], v_odd_w_out[0],
             v_ffn_w_in[0], v_ffn_w_in[1], v_ffn_w_out[0], v_ffn_w_out[1]]
    late_sib = _rs_sibling_exchange([gd_in0], "rs_sibling_late")
    late_partial = [_rs_chip_sum(gd_in0, late_sib[0], c_idx, "rs_chip_sum_" + big_names[0])]
    partial = late_partial + early_partial
    grad_x, *late_received = _mm(dp0, w_all0, "nt", F32, "l0_in_proj_dx", rider=_chip_exchange_rider(late_partial),
                                 addend=(dz1, ALPHA))
    received = list(late_received) + list(early_received)
    upd = [_rs_final_adamw(p, r, chip_idx, wt, mt, vt, "rs_final_adamw_" + n)
           for p, r, wt, mt, vt, n in zip(partial, received, big, big_m, big_v, big_names)]
    big_out = {}
    for i, n in enumerate(["even_w_in", "even_w_out", "odd_w_in", "odd_w_out"]):
        big_out[n] = [o[None] for o in upd[i]]
    big_out["ffn_w_in"] = [jnp.stack([a, b]) for a, b in zip(upd[4], upd[5])]
    big_out["ffn_w_out"] = [jnp.stack([a, b]) for a, b in zip(upd[6], upd[7])]

    small_sum = small_sum.reshape(-1)
    outs_small = []
    off = 0
    for a in small_g:
        outs_small.append(small_sum[off:off + a.size].reshape(a.shape))
        off += a.size
    gr_bf, gr_conv, gr_vg, gr_vb, gr_ws, gr_bs, gr_mg, gr_mb, gr_fg, gr_fb = outs_small

    loss = lax.psum(loss_part, ("x", "y", "c"))

    grads = {
        "even_b_f": gr_bf[None],
        "even_conv_w": lax.dynamic_slice(gr_conv, (0, me * cs), (3, cs))[None],
        "odd_v_ln_g": lax.dynamic_slice(gr_vg, (me * vs,), (vs,))[None],
        "odd_v_ln_b": lax.dynamic_slice(gr_vb, (me * vs,), (vs,))[None],
        "odd_w_s": gr_ws[None], "odd_b_s": gr_bs[None],
        "mix_ln_g": gr_mg, "mix_ln_b": gr_mb, "ffn_ln_g": gr_fg, "ffn_ln_b": gr_fb,
    }
    weights = dict(even_w_in=even_w_in, even_b_f=even_b_f, even_conv_w=even_conv_w, even_w_out=even_w_out,
                   odd_w_in=odd_w_in, odd_v_ln_g=odd_v_ln_g, odd_v_ln_b=odd_v_ln_b, odd_w_s=odd_w_s,
                   odd_b_s=odd_b_s, odd_w_out=odd_w_out, mix_ln_g=mix_ln_g, mix_ln_b=mix_ln_b,
                   ffn_w_in=ffn_w_in, ffn_w_out=ffn_w_out, ffn_ln_g=ffn_ln_g, ffn_ln_b=ffn_ln_b)
    moms = dict(even_w_in=(m_even_w_in, v_even_w_in), even_b_f=(m_even_b_f, v_even_b_f),
                even_conv_w=(m_even_conv_w, v_even_conv_w), even_w_out=(m_even_w_out, v_even_w_out),
                odd_w_in=(m_odd_w_in, v_odd_w_in), odd_v_ln_g=(m_odd_v_ln_g, v_odd_v_ln_g),
                odd_v_ln_b=(m_odd_v_ln_b, v_odd_v_ln_b), odd_w_s=(m_odd_w_s, v_odd_w_s),
                odd_b_s=(m_odd_b_s, v_odd_b_s), odd_w_out=(m_odd_w_out, v_odd_w_out),
                mix_ln_g=(m_mix_ln_g, v_mix_ln_g), mix_ln_b=(m_mix_ln_b, v_mix_ln_b),
                ffn_w_in=(m_ffn_w_in, v_ffn_w_in), ffn_w_out=(m_ffn_w_out, v_ffn_w_out),
                ffn_ln_g=(m_ffn_ln_g, v_ffn_ln_g), ffn_ln_b=(m_ffn_ln_b, v_ffn_ln_b))
    names = list(weights)
    gout, deltas, new_m, new_v = [], [], [], []
    for n in names:
        if n in big_out:
            gr, dlt, mn, vn = big_out[n]
        else:
            gr = grads[n]
            dlt, mn, vn = _adamw(weights[n], gr, moms[n][0], moms[n][1], "adamw_" + n)
        gout.append(gr.reshape(weights[n].shape))
        deltas.append(dlt.reshape(weights[n].shape))
        new_m.append(mn.reshape(weights[n].shape))
        new_v.append(vn.reshape(weights[n].shape))
    return (loss, grad_x[None], *gout, *deltas, *new_m, *new_v)
```

```python
import functools
from typing import Callable, NamedTuple

import jax
import jax.numpy as jnp
from jax import lax
from jax.experimental import pallas as pl
from jax.experimental.pallas import tpu as pltpu

F32 = jnp.float32
BF16 = jnp.bfloat16
MESH = pl.DeviceIdType.MESH

DEPTH = 2
ALPHA = (2.0 * DEPTH) ** 0.25
LN_EPS = 1e-5
ADAM_LR = 0.001
ADAM_B1 = 0.9
ADAM_B2 = 0.999
ADAM_EPS = 1e-08
ADAM_WD = 0.01
ADAM_STEP = 10

N_DEV = 8
LANES = 128
SUBLANES = 8
VMEM_LIMIT = 48 * 1024 * 1024
NEG_BIG = -1e30
ROW_TILES = (512, 256, 128)


def _pick(n, cands):
    for c in cands:
        if c <= n and n % c == 0:
            return c
    return n


def _params(*sem):
    return pltpu.CompilerParams(dimension_semantics=sem, vmem_limit_bytes=VMEM_LIMIT)


NN = (((1,), (0,)), ((), ()))
NT = (((1,), (1,)), ((), ()))
TN = (((0,), (0,)), ((), ()))
M_TILES = (1024, 512, 1408, 256, 128)
N_TILES = (512, 640, 256, 128)
K_TILES = (4096, 2048, 1024, 512, 640, 1408, 256, 128)
K_WHOLE = 3328


class _Rider(NamedTuple):
    inputs: list
    in_specs: list
    out_shapes: list
    out_specs: list
    scratch: list
    phases: Callable


def _mm_core(name, grid, a, b, a_spec, b_spec, o_spec, o_shape, o_dtype, dims, tile, pieces=None, rider=None,
             addend=None):
    nred = grid[2]
    pieces = pieces or [(lambda r: r[...], lambda r: r[...])]
    ni = len(rider.inputs) if rider else 0
    no = len(rider.out_shapes) if rider else 0
    nacc = 0 if nred == 1 else 1
    add_arrays = [addend[0]] if addend else []

    def body(a_ref, b_ref, *rest):
        if addend:
            add_ref, rest = rest[0], rest[1:]
        finished = (lambda v: addend[1] * add_ref[...] + v) if addend else (lambda v: v)
        o_ref = rest[ni]
        if rider:
            start, finish = rider.phases(rest[:ni], rest[ni + 1:ni + 1 + no], rest[ni + 1 + no + nacc:])
            ids = [pl.program_id(ax) for ax in range(3)]
            first = functools.reduce(jnp.logical_and, [i == 0 for i in ids])
            last = functools.reduce(jnp.logical_and, [i == g - 1 for i, g in zip(ids, grid)])
            pl.when(first)(start)
        part = None
        for fa, fb in pieces:
            prod = lax.dot_general(fa(a_ref).astype(BF16), fb(b_ref).astype(BF16), dims, preferred_element_type=F32)
            part = prod if part is None else part + prod
        if nred == 1:
            o_ref[...] = finished(part).astype(o_ref.dtype)
        else:
            acc_ref = rest[ni + 1 + no]
            kk = pl.program_id(2)

            @pl.when(kk == 0)
            def _():
                acc_ref[...] = jnp.zeros_like(acc_ref)

            acc_ref[...] += part

            @pl.when(kk == nred - 1)
            def _():
                o_ref[...] = finished(acc_ref[...]).astype(o_ref.dtype)
        if rider:
            pl.when(last)(finish)

    out = pl.pallas_call(
        body, name=name, grid=grid,
        in_specs=[a_spec, b_spec] + ([o_spec] if addend else []) + (rider.in_specs if rider else []),
        out_specs=[o_spec] + (rider.out_specs if rider else []),
        out_shape=[jax.ShapeDtypeStruct(o_shape, o_dtype)] + (rider.out_shapes if rider else []),
        scratch_shapes=([] if nred == 1 else [pltpu.VMEM(tile, F32)]) + (rider.scratch if rider else []),
        compiler_params=_params(*(["arbitrary"] * 3 if rider else ["parallel", "parallel", "arbitrary"])),
    )(a, b, *add_arrays, *(rider.inputs if rider else []))
    return out if rider else out[0]


def _mm(a, b, mode, out_dtype, name, rider=None, addend=None):
    if mode == "nn":
        (m, k), (k2, n) = a.shape, b.shape
    elif mode == "nt":
        (m, k), (n, k2) = a.shape, b.shape
    else:
        (k, m), (k2, n) = a.shape, b.shape
    assert k == k2, (a.shape, b.shape, mode)
    tm, tn = _pick(m, M_TILES), _pick(n, N_TILES)
    tk = k if k <= K_WHOLE else _pick(k, K_TILES)
    if mode == "nn":
        a_spec = pl.BlockSpec((tm, tk), lambda i, j, kk: (i, kk))
        b_spec = pl.BlockSpec((tk, tn), lambda i, j, kk: (kk, j))
        dims = NN
    elif mode == "nt":
        a_spec = pl.BlockSpec((tm, tk), lambda i, j, kk: (i, kk))
        b_spec = pl.BlockSpec((tn, tk), lambda i, j, kk: (j, kk))
        dims = NT
    else:
        a_spec = pl.BlockSpec((tk, tm), lambda i, j, kk: (kk, i))
        b_spec = pl.BlockSpec((tk, tn), lambda i, j, kk: (kk, j))
        dims = TN
    return _mm_core(name, (m // tm, n // tn, k // tk), a, b, a_spec, b_spec,
                    pl.BlockSpec((tm, tn), lambda i, j, kk: (i, j)), (m, n), out_dtype, dims, (tm, tn), rider=rider,
                    addend=addend)


def _act_spec(blocked, rows, ns, row_ax, d_ax):
    if blocked:
        return pl.BlockSpec((None, rows, ns), lambda *g: (g[d_ax], g[row_ax], 0))
    return pl.BlockSpec((rows, ns), lambda *g: (g[row_ax], g[d_ax]))


def _mm_cols_fwd(a, g3, blocked, out_dtype, name):
    (t, k), (nd, k2, ns) = a.shape, g3.shape
    assert k == k2
    tm, tk = _pick(t, M_TILES), _pick(k, K_TILES)
    return _mm_core(name, (t // tm, nd, k // tk), a, g3,
                    pl.BlockSpec((tm, tk), lambda i, d, kk: (i, kk)),
                    pl.BlockSpec((None, tk, ns), lambda i, d, kk: (d, kk, 0)),
                    _act_spec(blocked, tm, ns, 0, 1), (nd, t, ns) if blocked else (t, nd * ns), out_dtype, NN, (tm, ns))


def _mm_cols_dx_ln_bwd(dy, g3, blocked, xa, xb, gam, dya, name):
    nd, k, ns = g3.shape
    t, d = xa.shape
    assert k == d
    tm = _pick(t, ROW_TILES)
    grp = nd if not blocked else (2 if nd % 2 == 0 else 1)
    nred = nd // grp

    def body(a_ref, b_ref, xa_ref, xb_ref, g_ref, dya_ref, dz_ref, dzb_ref, dg_ref, db_ref, *acc):
        i, kk = pl.program_id(0), pl.program_id(1)

        @pl.when(jnp.logical_and(i == 0, kk == 0))
        def _():
            dg_ref[...] = jnp.zeros_like(dg_ref)
            db_ref[...] = jnp.zeros_like(db_ref)

        if blocked:
            part = None
            for s in range(grp):
                prod = lax.dot_general(a_ref[s], b_ref[s], NT, preferred_element_type=F32)
                part = prod if part is None else part + prod
        else:
            whole_b = jnp.concatenate([b_ref[s] for s in range(nd)], axis=1)
            part = lax.dot_general(a_ref[...], whole_b, NT, preferred_element_type=F32)

        def ln_bwd(dyb):
            dy_t = ALPHA * dya_ref[...] + dyb
            z = ALPHA * xa_ref[...] + xb_ref[...]
            mu = jnp.mean(z, axis=-1, keepdims=True)
            zc = z - mu
            var = jnp.mean(zc * zc, axis=-1, keepdims=True)
            rstd = lax.rsqrt(var + LN_EPS)
            xhat = zc * rstd
            dxh = dy_t * g_ref[...]
            m1 = jnp.mean(dxh, axis=-1, keepdims=True)
            m2 = jnp.mean(dxh * xhat, axis=-1, keepdims=True)
            dz = rstd * (dxh - m1 - xhat * m2)
            dz_ref[...] = dz
            dzb_ref[...] = dz.astype(BF16)
            dg_ref[...] += jnp.sum(dy_t * xhat, axis=0, keepdims=True)
            db_ref[...] += jnp.sum(dy_t, axis=0, keepdims=True)

        if nred == 1:
            ln_bwd(part)
        else:
            acc_ref, = acc

            @pl.when(kk == 0)
            def _():
                acc_ref[...] = part

            @pl.when(kk > 0)
            def _():
                acc_ref[...] += part

            @pl.when(kk == nred - 1)
            def _():
                ln_bwd(acc_ref[...])

    row = pl.BlockSpec((tm, d), lambda i, kk: (i, 0))
    vec = pl.BlockSpec((1, d), lambda i, kk: (0, 0))
    if blocked:
        a_spec = pl.BlockSpec((grp, tm, ns), lambda i, kk: (kk, i, 0))
        b_spec = pl.BlockSpec((grp, k, ns), lambda i, kk: (kk, 0, 0))
    else:
        a_spec = pl.BlockSpec((tm, nd * ns), lambda i, kk: (i, 0))
        b_spec = pl.BlockSpec((nd, k, ns), lambda i, kk: (0, 0, 0))
    dz, dzb, dg, db = pl.pallas_call(
        body, name=name, grid=(t // tm, nred),
        in_specs=[a_spec, b_spec, row, row, vec, row], out_specs=[row, row, vec, vec],
        out_shape=[jax.ShapeDtypeStruct((t, d), F32), jax.ShapeDtypeStruct((t, d), BF16),
                   jax.ShapeDtypeStruct((1, d), F32), jax.ShapeDtypeStruct((1, d), F32)],
        scratch_shapes=[] if nred == 1 else [pltpu.VMEM((tm, d), F32)],
        compiler_params=_params("arbitrary", "arbitrary"),
    )(dy, g3, xa, xb, gam.reshape(1, d), dya)
    return dz, dzb, dg[0], db[0]


def _mm_cols_dw(a, dy, nd, blocked, out_dtype, name):
    t, k = a.shape
    ns = dy.shape[2] if blocked else dy.shape[1] // nd
    tmk, tk = _pick(k, M_TILES), _pick(t, K_TILES)
    return _mm_core(name, (nd, k // tmk, t // tk), a, dy,
                    pl.BlockSpec((tk, tmk), lambda d, j, kk: (kk, j)),
                    _act_spec(blocked, tk, ns, 2, 0),
                    pl.BlockSpec((None, tmk, ns), lambda d, j, kk: (d, j, 0)), (nd, k, ns), out_dtype, TN, (tmk, ns))


def _mm_blk_dw(h3, dz, out_dtype, name):
    (nb, t, ns), (_, n) = h3.shape, dz.shape
    tn, tk = _pick(n, (1024,) + N_TILES), _pick(t, K_TILES)
    return _mm_core(name, (nb, n // tn, t // tk), h3, dz,
                    pl.BlockSpec((None, tk, ns), lambda d, j, kk: (d, kk, 0)),
                    pl.BlockSpec((tk, tn), lambda d, j, kk: (kk, j)),
                    pl.BlockSpec((ns, tn), lambda d, j, kk: (d, j)), (nb * ns, n), out_dtype, TN, (ns, tn))


def _mm_ln(a, w, xa, g, b, name, target=None):
    blocked = a.ndim == 3
    t, d = xa.shape
    k = w.shape[0]
    tm = _pick(t, ROW_TILES)
    nb = a.shape[0] if blocked else 1
    ns = k // nb
    halves = [slice(0, tm // 2), slice(tm // 2, tm)] if tm % 32 == 0 else [slice(0, tm)]

    def body(a_ref, w_ref, xa_ref, g_ref, b_ref, *rest):
        def product(rows):
            if not blocked:
                return jnp.dot(a_ref[rows, :], w_ref[...], preferred_element_type=F32)
            acc = None
            for s in range(nb):
                prod = jnp.dot(a_ref[s, rows, :], w_ref[s * ns:(s + 1) * ns, :], preferred_element_type=F32)
                acc = prod if acc is None else acc + prod
            return acc

        if target is not None:
            t_ref, dz_ref, dzb_ref, dg_ref, db_ref, l_ref = rest

            @pl.when(pl.program_id(0) == 0)
            def _():
                l_ref[...] = jnp.zeros_like(l_ref)
                dg_ref[...] = jnp.zeros_like(dg_ref)
                db_ref[...] = jnp.zeros_like(db_ref)
        else:
            xb_ref, y_ref, yb_ref = rest
        for rows, xb in zip(halves, [product(rows) for rows in halves]):
            z = ALPHA * xa_ref[rows, :] + xb
            mu = jnp.mean(z, axis=-1, keepdims=True)
            zc = z - mu
            var = jnp.mean(zc * zc, axis=-1, keepdims=True)
            rstd = lax.rsqrt(var + LN_EPS)
            xhat = zc * rstd
            y = xhat * g_ref[...] + b_ref[...]
            if target is not None:
                e = y - t_ref[rows, :]
                l_ref[...] += 0.5 * jnp.sum(jnp.mean(e * e, axis=-1, keepdims=True))
                dy = e * (1.0 / d)
                dxh = dy * g_ref[...]
                m1 = jnp.mean(dxh, axis=-1, keepdims=True)
                m2 = jnp.mean(dxh * xhat, axis=-1, keepdims=True)
                dz = rstd * (dxh - m1 - xhat * m2)
                dz_ref[rows, :] = dz
                dzb_ref[rows, :] = dz.astype(BF16)
                dg_ref[...] += jnp.sum(dy * xhat, axis=0, keepdims=True)
                db_ref[...] += jnp.sum(dy, axis=0, keepdims=True)
            else:
                xb_ref[rows, :] = xb
                y_ref[rows, :] = y
                yb_ref[rows, :] = y.astype(BF16)

    row = pl.BlockSpec((tm, d), lambda i: (i, 0))
    vec = pl.BlockSpec((1, d), lambda i: (0, 0))
    a_spec = pl.BlockSpec((nb, tm, ns), lambda i: (0, i, 0)) if blocked else pl.BlockSpec((tm, k), lambda i: (i, 0))
    ins = [a, w, xa, g.reshape(1, d), b.reshape(1, d)]
    in_specs = [a_spec, pl.BlockSpec((k, d), lambda i: (0, 0)), row, vec, vec]
    if target is not None:
        dz, dzb, dg, db, l = pl.pallas_call(
            body, name=name, grid=(t // tm,), in_specs=in_specs + [row],
            out_specs=[row, row, vec, vec, pl.BlockSpec((1, LANES), lambda i: (0, 0))],
            out_shape=[jax.ShapeDtypeStruct((t, d), F32), jax.ShapeDtypeStruct((t, d), BF16),
                       jax.ShapeDtypeStruct((1, d), F32), jax.ShapeDtypeStruct((1, d), F32),
                       jax.ShapeDtypeStruct((1, LANES), F32)],
            compiler_params=_params("arbitrary"),
        )(*ins, target)
        return dz, dzb, dg[0], db[0], l[0, 0]
    return pl.pallas_call(
        body, name=name, grid=(t // tm,), in_specs=in_specs, out_specs=[row, row, row],
        out_shape=[jax.ShapeDtypeStruct((t, d), F32)] * 2 + [jax.ShapeDtypeStruct((t, d), BF16)],
        compiler_params=_params("parallel"),
    )(*ins)


def _ffn_in_swiglu(xb, g4, name):
    (t, k), (_, nb, _, ns) = xb.shape, g4.shape
    tm = _pick(t, M_TILES)

    def body(x_ref, w_ref, h_ref, gu_ref):
        xv = x_ref[...]
        gate = jnp.dot(xv, w_ref[0], preferred_element_type=F32)
        up = jnp.dot(xv, w_ref[1], preferred_element_type=F32)
        h_ref[...] = (gate * jax.nn.sigmoid(gate) * up).astype(BF16)
        gu_ref[0] = gate.astype(BF16)
        gu_ref[1] = up.astype(BF16)

    return pl.pallas_call(
        body, name=name, grid=(t // tm, nb),
        in_specs=[pl.BlockSpec((tm, k), lambda i, d: (i, 0)),
                  pl.BlockSpec((2, None, k, ns), lambda i, d: (0, d, 0, 0))],
        out_specs=[pl.BlockSpec((None, tm, ns), lambda i, d: (d, i, 0)),
                   pl.BlockSpec((2, None, tm, ns), lambda i, d: (0, d, i, 0))],
        out_shape=[jax.ShapeDtypeStruct((nb, t, ns), BF16), jax.ShapeDtypeStruct((2, nb, t, ns), BF16)],
        compiler_params=_params("parallel", "parallel"),
    )(xb, g4)


def _ffn_out_dx_swiglu(dz, w_out, gu4, name):
    (t, d), (_, nb, _, ns) = dz.shape, gu4.shape
    tm = _pick(t, M_TILES)

    def body(dz_ref, w_ref, gu_ref, o_ref):
        halves = [slice(0, tm // 2), slice(tm // 2, tm)] if tm % 16 == 0 else [slice(0, tm)]
        dhs = [lax.dot_general(dz_ref[rows, :].astype(BF16), w_ref[...], NT, preferred_element_type=F32)
               for rows in halves]
        for rows, dh in zip(halves, dhs):
            gate = gu_ref[0, rows, :].astype(F32)
            up = gu_ref[1, rows, :].astype(F32)
            sg = jax.nn.sigmoid(gate)
            silu = gate * sg
            o_ref[0, rows, :] = (dh * up * (sg + silu * (1.0 - sg))).astype(BF16)
            o_ref[1, rows, :] = (dh * silu).astype(BF16)

    blk = pl.BlockSpec((2, None, tm, ns), lambda i, j: (0, j, i, 0))
    return pl.pallas_call(
        body, name=name, grid=(t // tm, nb),
        in_specs=[pl.BlockSpec((tm, d), lambda i, j: (i, 0)), pl.BlockSpec((ns, d), lambda i, j: (j, 0)), blk],
        out_specs=blk,
        out_shape=jax.ShapeDtypeStruct((2, nb, t, ns), BF16),
        compiler_params=_params("parallel", "parallel"),
    )(dz, w_out, gu4)


def _tri_matmul(tri, x):
    x1 = x.astype(BF16)
    r1 = x - x1.astype(F32)
    x2 = r1.astype(BF16)
    x3 = (r1 - x2.astype(F32)).astype(BF16)
    dot = lambda v: jnp.dot(tri, v, preferred_element_type=F32)
    return dot(x1) + dot(x2) + dot(x3)


def _fgate_fwd(proj, bf_pad, fcol, n_heads, name):
    t = proj.shape[0]
    tb = _pick(t, ROW_TILES)

    def body(p_ref, b_ref, c_ref, carry):
        @pl.when(pl.program_id(0) == 0)
        def _():
            carry[...] = jnp.zeros_like(carry)

        z = p_ref[...] + b_ref[...]
        lf = jnp.minimum(z, 0.0) - jnp.log1p(jnp.exp(-jnp.abs(z)))
        lane = lax.broadcasted_iota(jnp.int32, (tb, LANES), 1)
        lf = jnp.where(lane < n_heads, lf, 0.0)
        r = lax.broadcasted_iota(jnp.int32, (tb, tb), 0)
        s = lax.broadcasted_iota(jnp.int32, (tb, tb), 1)
        tri = (s <= r).astype(BF16)
        c = _tri_matmul(tri, lf) + carry[...]
        c_ref[...] = c
        carry[...] = c[tb - 1:tb, :]

    return pl.pallas_call(
        body, name=name, grid=(t // tb,),
        in_specs=[pl.BlockSpec((tb, LANES), lambda i: (i, fcol)), pl.BlockSpec((1, LANES), lambda i: (0, 0))],
        out_specs=pl.BlockSpec((tb, LANES), lambda i: (i, 0)),
        out_shape=jax.ShapeDtypeStruct((t, LANES), F32),
        scratch_shapes=[pltpu.VMEM((1, LANES), F32)],
        compiler_params=_params("arbitrary"),
    )(proj, bf_pad)


def _fgate_bwd(proj, bf_pad, dcq, dck, fcol, n_heads, name):
    t = proj.shape[0]
    tb = _pick(t, ROW_TILES)
    nb = t // tb

    def body(p_ref, b_ref, dcq_ref, dck_ref, dz_ref, db_ref, carry):
        @pl.when(pl.program_id(0) == 0)
        def _():
            carry[...] = jnp.zeros_like(carry)
            db_ref[...] = jnp.zeros_like(db_ref)

        r = lax.broadcasted_iota(jnp.int32, (tb, tb), 0)
        s = lax.broadcasted_iota(jnp.int32, (tb, tb), 1)
        tri = (s >= r).astype(BF16)
        dlf = _tri_matmul(tri, dcq_ref[...] + dck_ref[...]) + carry[...]
        carry[...] = dlf[0:1, :]
        z = p_ref[...] + b_ref[...]
        lane = lax.broadcasted_iota(jnp.int32, (tb, LANES), 1)
        dz = jnp.where(lane < n_heads, dlf * jax.nn.sigmoid(-z), 0.0)
        dz_ref[...] = dz
        db_ref[...] += jnp.sum(dz, axis=0, keepdims=True)

    dz, db = pl.pallas_call(
        body, name=name, grid=(nb,),
        in_specs=[pl.BlockSpec((tb, LANES), lambda i: (nb - 1 - i, fcol)),
                  pl.BlockSpec((1, LANES), lambda i: (0, 0)),
                  pl.BlockSpec((tb, LANES), lambda i: (nb - 1 - i, 0)),
                  pl.BlockSpec((tb, LANES), lambda i: (nb - 1 - i, 0))],
        out_specs=[pl.BlockSpec((tb, LANES), lambda i: (nb - 1 - i, 0)),
                   pl.BlockSpec((1, LANES), lambda i: (0, 0))],
        out_shape=[jax.ShapeDtypeStruct((t, LANES), F32), jax.ShapeDtypeStruct((1, LANES), F32)],
        scratch_shapes=[pltpu.VMEM((1, LANES), F32)],
        compiler_params=_params("arbitrary"),
    )(proj, bf_pad, dcq, dck)
    return dz, db[0]


def _split3(x):
    hi = x.astype(BF16)
    r = x - hi.astype(F32)
    mid = r.astype(BF16)
    return hi, mid, (r - mid.astype(F32)).astype(BF16)


def _attn_fwd(qa, ka, va, dh, name, gather=()):
    nh, t, da = qa.shape
    tq = _pick(t, ROW_TILES)
    hb = 2 if nh % 2 == 0 else 1
    heads = range(hb)
    n = len(gather)
    steps = (nh // hb, t // tq)

    def body(q_ref, k_ref, v_ref, *rest):
        x_refs, o_ref, g_refs = rest[:n], rest[n], rest[n + 1:2 * n + 1]
        m_s, acc_s, s_a, s_b = rest[2 * n + 1:2 * n + 5]
        qi = pl.program_id(1)
        if n:
            start, forward, finish = _gather_phases(x_refs, g_refs, *rest[2 * n + 5:])
            at = lambda hh, qq: jnp.logical_and(pl.program_id(0) == hh, qi == qq)
            pl.when(at(0, 0))(start)
            pl.when(at(steps[0] // 2, 0))(forward)
        m_s[...] = jnp.full(m_s.shape, NEG_BIG, F32)
        acc_s[...] = jnp.zeros_like(acc_s)

        def scores(s_ref, j):
            off = pl.multiple_of(j * tq, tq)
            for g in heads:
                s_ref[g] = lax.dot_general(q_ref[g], k_ref[g, pl.ds(off, tq), :], NT, preferred_element_type=F32)

        def absorb(s_ref, j, diagonal):
            off = pl.multiple_of(j * tq, tq)
            s = [s_ref[g] for g in heads]
            if diagonal:
                row = lax.broadcasted_iota(jnp.int32, (tq, tq), 0)
                col = lax.broadcasted_iota(jnp.int32, (tq, tq), 1)
                s = [jnp.where(col > row, NEG_BIG, sg) for sg in s]
            m_prev = [m_s[g] for g in heads]
            m_new = [jnp.maximum(m_prev[g], jnp.max(s[g], axis=1, keepdims=True)) for g in heads]
            p = [jnp.exp(s[g] - m_new[g]).astype(BF16) for g in heads]
            pv = [jnp.dot(p[g], v_ref[g, pl.ds(off, tq), :], preferred_element_type=F32) for g in heads]
            for g in heads:
                acc_s[g] = jnp.exp(m_prev[g] - m_new[g]) * acc_s[g] + pv[g]
                m_s[g] = m_new[g]

        def two_blocks(r, carry):
            scores(s_b, 2 * r + 1)
            absorb(s_a, 2 * r, False)
            scores(s_a, 2 * r + 2)
            absorb(s_b, 2 * r + 1, False)
            return carry

        scores(s_a, 0)
        rounds = qi // 2
        lax.fori_loop(0, rounds, two_blocks, 0)

        @pl.when(qi % 2 == 0)
        def _():
            absorb(s_a, qi, True)

        @pl.when(qi % 2 == 1)
        def _():
            scores(s_b, qi)
            absorb(s_a, qi - 1, False)
            absorb(s_b, qi, True)

        lane = lax.broadcasted_iota(jnp.int32, (tq, da), 1)
        for g in heads:
            acc = acc_s[g]
            l = jnp.sum(jnp.where(lane == dh, acc, 0.0), axis=1, keepdims=True)
            o_ref[g] = jnp.where(lane == dh, m_s[g] + jnp.log(l), acc / l)
        if n:
            pl.when(at(steps[0] - 1, steps[1] - 1))(finish)

    full = pl.BlockSpec((hb, t, da), lambda h, qi: (h, 0, 0))
    blk = pl.BlockSpec((hb, tq, da), lambda h, qi: (h, qi, 0))
    return pl.pallas_call(
        body, name=name, grid=steps,
        in_specs=[blk, full, full] + [ANY] * n, out_specs=[blk] + [ANY] * n,
        out_shape=[jax.ShapeDtypeStruct((nh, t, da), F32)] + _gather_shapes(gather),
        scratch_shapes=[pltpu.VMEM((hb, tq, 1), F32), pltpu.VMEM((hb, tq, da), F32),
                        pltpu.VMEM((hb, tq, tq), F32), pltpu.VMEM((hb, tq, tq), F32)] + (_gather_sems(n) if n else []),
        compiler_params=_params("arbitrary", "arbitrary"),
    )(qa, ka, va, *gather)


def _attn_bwd(qa, ka, va, doa, name, exchange=()):
    nh, t, da = qa.shape
    tq = _pick(t, ROW_TILES)
    nq = t // tq
    n = len(exchange)

    def body(q_ref, do_ref, k_ref, v_ref, *rest):
        p_refs, (dq_ref, dk_ref, dv_ref), r_refs = rest[:n], rest[n:n + 3], rest[n + 3:2 * n + 3]
        kj = pl.program_id(1)
        if n:
            start, finish = _chip_exchange_phases(p_refs, r_refs, *rest[2 * n + 3:])
            pl.when(jnp.logical_and(pl.program_id(0) == 0, kj == 0))(start)

        @pl.when(kj == 0)
        def _():
            dq_ref[...] = jnp.zeros_like(dq_ref)

        dk_ref[...] = jnp.zeros_like(dk_ref)
        dv_ref[...] = jnp.zeros_like(dv_ref)
        kb = k_ref[...]
        vb = v_ref[...]

        def step(i, diagonal, blocks=1):
            off = pl.multiple_of(i * tq, tq)
            rows = blocks * tq
            qb = q_ref[pl.ds(off, rows), :]
            dob = do_ref[pl.ds(off, rows), :]
            st = lax.dot_general(kb, qb, NT, preferred_element_type=F32)
            if diagonal:
                row = lax.broadcasted_iota(jnp.int32, (tq, tq), 0)
                col = lax.broadcasted_iota(jnp.int32, (tq, tq), 1)
                st = jnp.where(row > col, NEG_BIG, st)
            pt = jnp.exp(st)
            dst = (pt * lax.dot_general(vb, dob, NT, preferred_element_type=F32)).astype(BF16)
            dv_ref[...] += jnp.dot(pt.astype(BF16), dob, preferred_element_type=F32)
            dk_ref[...] += jnp.dot(dst, qb, preferred_element_type=F32)
            dq_ref[pl.ds(off, rows), :] += lax.dot_general(dst, kb, TN, preferred_element_type=F32)

        step(kj, True)
        count = nq - 1 - kj
        one, two = count % 2, (count // 2) % 2

        @pl.when(one == 1)
        def _():
            step(kj + 1, False)

        @pl.when(two == 1)
        def _():
            step(kj + 1 + one, False, blocks=2)

        def loop(r, carry):
            step(kj + 1 + one + 2 * two + 4 * r, False, blocks=4)
            return carry

        lax.fori_loop(0, count // 4, loop, 0)
        if n:
            pl.when(jnp.logical_and(pl.program_id(0) == nh - 1, kj == nq - 1))(finish)

    full = pl.BlockSpec((None, t, da), lambda h, j: (h, 0, 0))
    blk = pl.BlockSpec((None, tq, da), lambda h, j: (h, j, 0))
    return pl.pallas_call(
        body, name=name, grid=(nh, nq),
        in_specs=[full, full, blk, blk] + [ANY] * n, out_specs=[full, blk, blk] + [ANY] * n,
        out_shape=[jax.ShapeDtypeStruct((nh, t, da), F32)] * 3 + _chip_exchange_shapes(exchange),
        scratch_shapes=_chip_exchange_sems(n) if n else [],
        compiler_params=_params("arbitrary", "arbitrary"),
    )(qa, doa, ka, va, *exchange)


def _head_group(dh, h):
    g = h // (LANES // dh)
    return slice(g * LANES, (g + 1) * LANES)


def _head_select(dh, h, to_heads):
    r = lax.broadcasted_iota(jnp.int32, (LANES, LANES), 0)
    c = lax.broadcasted_iota(jnp.int32, (LANES, LANES), 1)
    nat, col = (r, c) if to_heads else (c, r)
    return jnp.logical_and(nat == col + (h % (LANES // dh)) * dh, col < dh).astype(BF16)


def _column(x, lane, j):
    return jnp.sum(jnp.where(lane == j, x, 0.0), axis=1, keepdims=True)


def _bias_columns(lane, first, value):
    out = jnp.zeros(lane.shape, F32)
    for j, term in enumerate(_split3(value)):
        out = out + jnp.where(lane == first + j, -term.astype(F32), 0.0)
    return out


def _attn_pack(proj, cgate, w, nh, scale, name):
    t = proj.shape[0]
    dh = w // nh
    tb = _pick(t, ROW_TILES)

    def body(q_ref, k_ref, v_ref, c_ref, qa_ref, ka_ref, va_ref):
        lane = lax.broadcasted_iota(jnp.int32, (tb, LANES), 1)
        ones_qv = jnp.where(jnp.logical_and(lane >= dh, lane < dh + 3), 1.0, 0.0)
        ones_k = jnp.where(jnp.logical_and(lane >= dh + 3, lane < dh + 7), 1.0, 0.0)
        qb = (q_ref[...] * scale).astype(BF16)
        kb = k_ref[...].astype(BF16)
        vb = v_ref[...].astype(BF16)
        cblk = c_ref[...]
        for h in range(nh):
            sel, grp = _head_select(dh, h, True), _head_group(dh, h)
            qa_ref[h] = (jnp.dot(qb[:, grp], sel, preferred_element_type=F32) + ones_qv).astype(BF16)
            va_ref[h] = (jnp.dot(vb[:, grp], sel, preferred_element_type=F32) + ones_qv).astype(BF16)
            bias = _bias_columns(lane, dh, _column(cblk, lane, h))
            ka_ref[h] = (jnp.dot(kb[:, grp], sel, preferred_element_type=F32) + bias + ones_k).astype(BF16)

    col = lambda j: pl.BlockSpec((tb, w), lambda i: (i, j))
    out = pl.BlockSpec((nh, tb, LANES), lambda i: (0, i, 0))
    return pl.pallas_call(
        body, name=name, grid=(t // tb,),
        in_specs=[col(0), col(1), col(2), pl.BlockSpec((tb, LANES), lambda i: (i, 0))],
        out_specs=[out, out, out],
        out_shape=[jax.ShapeDtypeStruct((nh, t, LANES), BF16)] * 3,
        compiler_params=_params("parallel"),
    )(proj, proj, proj, cgate)


def _attn_pack_bwd(dmix, oa, qa, w, nh, name):
    t = dmix.shape[0]
    dh = w // nh
    tb = _pick(t, ROW_TILES)

    def body(d_ref, oa_ref, qa_ref, doa_ref, qa2_ref):
        lane = lax.broadcasted_iota(jnp.int32, (tb, LANES), 1)
        db = d_ref[...].astype(BF16)
        for h in range(nh):
            do_h = jnp.dot(db[:, _head_group(dh, h)], _head_select(dh, h, True), preferred_element_type=F32)
            o_h = oa_ref[h]
            delta = jnp.sum(jnp.where(lane < dh, do_h * o_h, 0.0), axis=1, keepdims=True)
            doa_ref[h] = (do_h + _bias_columns(lane, dh, delta)).astype(BF16)
            qa2_ref[h] = (qa_ref[h].astype(F32) + _bias_columns(lane, dh + 4, _column(o_h, lane, dh))).astype(BF16)

    blk = pl.BlockSpec((nh, tb, LANES), lambda i: (0, i, 0))
    return pl.pallas_call(
        body, name=name, grid=(t // tb,),
        in_specs=[pl.BlockSpec((tb, w), lambda i: (i, 0)), blk, blk], out_specs=[blk, blk],
        out_shape=[jax.ShapeDtypeStruct((nh, t, LANES), BF16)] * 2,
        compiler_params=_params("parallel"),
    )(dmix, oa, qa)


def _attn_unpack(xa, w, nh, mult, sum_col, sum_sign, name):
    t = xa.shape[1]
    dh = w // nh
    tb = _pick(t, ROW_TILES)

    def body(x_ref, o_ref, *rest):
        lane = lax.broadcasted_iota(jnp.int32, (tb, LANES), 1)
        per = LANES // dh
        cols = jnp.zeros((tb, LANES), F32)
        for h0 in range(0, nh, per):
            acc = jnp.zeros((tb, LANES), F32)
            for h in range(h0, h0 + per):
                xh = x_ref[h]
                acc = acc + jnp.dot((xh * mult).astype(BF16), _head_select(dh, h, False), preferred_element_type=F32)
                if sum_col is not None:
                    cols = cols + jnp.where(lane == h, sum_sign * _column(xh, lane, sum_col), 0.0)
            o_ref[:, _head_group(dh, h0)] = acc.astype(BF16)
        if sum_col is not None:
            rest[0][...] = cols

    nat = pl.BlockSpec((tb, w), lambda i: (i, 0))
    lanes = pl.BlockSpec((tb, LANES), lambda i: (i, 0))
    return pl.pallas_call(
        body, name=name, grid=(t // tb,),
        in_specs=[pl.BlockSpec((nh, tb, LANES), lambda i: (0, i, 0))],
        out_specs=[nat, lanes] if sum_col is not None else [nat],
        out_shape=[jax.ShapeDtypeStruct((t, w), BF16)] + ([jax.ShapeDtypeStruct((t, LANES), F32)]
                                                            if sum_col is not None else []),
        compiler_params=_params("parallel"),
    )(xa)


def _conv_fwd(proj, cw, w, bcol, name):
    t = proj.shape[0]
    tb = _pick(t, ROW_TILES)
    hb = tb // SUBLANES

    def body(b_ref, c_ref, h_ref, cp_ref, hp_ref, w_ref, y_ref):
        i = pl.program_id(0)
        zp = jnp.where(i > 0, cp_ref[...] * hp_ref[...], 0.0)
        zext = jnp.concatenate([zp, c_ref[...] * h_ref[...]], axis=0)
        z1 = pltpu.roll(zext, 1, 0)[SUBLANES:]
        z2 = pltpu.roll(zext, 2, 0)[SUBLANES:]
        y = w_ref[2:3, :] * zext[SUBLANES:] + w_ref[1:2, :] * z1 + w_ref[0:1, :] * z2
        y_ref[...] = (b_ref[...] * y).astype(BF16)

    cur = lambda j: pl.BlockSpec((tb, w), lambda i: (i, bcol + j))
    prev = lambda j: pl.BlockSpec((SUBLANES, w), lambda i: (jnp.maximum(i * hb - 1, 0), bcol + j))
    return pl.pallas_call(
        body, name=name, grid=(t // tb,),
        in_specs=[cur(0), cur(1), cur(2), prev(1), prev(2), pl.BlockSpec(cw.shape, lambda i: (0, 0))],
        out_specs=pl.BlockSpec((tb, w), lambda i: (i, 0)),
        out_shape=jax.ShapeDtypeStruct((t, w), BF16), compiler_params=_params("parallel"),
    )(proj, proj, proj, proj, proj, cw)


def _conv_bwd(proj, cw, dmix, w, bcol, name):
    t = proj.shape[0]
    tb = _pick(t, ROW_TILES)
    hb = tb // SUBLANES
    nb = t // tb
    n_ext = tb + SUBLANES

    def body(b_ref, c_ref, h_ref, cp_ref, hp_ref, bn_ref, d_ref, dn_ref, w_ref, db_ref, dc_ref, dh_ref, dw_ref):
        i = pl.program_id(0)
        c = c_ref[...]
        hh = h_ref[...]
        zp = jnp.where(i > 0, cp_ref[...] * hp_ref[...], 0.0)
        zext = jnp.concatenate([zp, c * hh], axis=0)
        z0 = zext[SUBLANES:]
        z1 = pltpu.roll(zext, 1, 0)[SUBLANES:]
        z2 = pltpu.roll(zext, 2, 0)[SUBLANES:]
        y = w_ref[2:3, :] * z0 + w_ref[1:2, :] * z1 + w_ref[0:1, :] * z2
        d = d_ref[...]
        db_ref[...] = d * y
        dy = d * b_ref[...]
        dyn = jnp.where(i < nb - 1, dn_ref[...] * bn_ref[...], 0.0)
        dext = jnp.concatenate([dy, dyn], axis=0)
        dy1 = pltpu.roll(dext, n_ext - 1, 0)[:tb]
        dy2 = pltpu.roll(dext, n_ext - 2, 0)[:tb]
        dz = w_ref[2:3, :] * dy + w_ref[1:2, :] * dy1 + w_ref[0:1, :] * dy2
        dc_ref[...] = dz * hh
        dh_ref[...] = dz * c

        @pl.when(i == 0)
        def _():
            dw_ref[...] = jnp.zeros_like(dw_ref)

        dw_ref[0:1, :] += jnp.sum(dy * z2, axis=0, keepdims=True)
        dw_ref[1:2, :] += jnp.sum(dy * z1, axis=0, keepdims=True)
        dw_ref[2:3, :] += jnp.sum(dy * z0, axis=0, keepdims=True)

    cur = lambda j: pl.BlockSpec((tb, w), lambda i: (i, bcol + j))
    prev = lambda j: pl.BlockSpec((SUBLANES, w), lambda i: (jnp.maximum(i * hb - 1, 0), bcol + j))
    nxt = lambda col: pl.BlockSpec((SUBLANES, w), lambda i: (jnp.minimum((i + 1) * hb, nb * hb - 1), col))
    out = pl.BlockSpec((tb, w), lambda i: (i, 0))
    return pl.pallas_call(
        body, name=name, grid=(nb,),
        in_specs=[cur(0), cur(1), cur(2), prev(1), prev(2), nxt(bcol),
                  pl.BlockSpec((tb, w), lambda i: (i, 1)), nxt(1), pl.BlockSpec(cw.shape, lambda i: (0, 0))],
        out_specs=[out, out, out, pl.BlockSpec(cw.shape, lambda i: (0, 0))],
        out_shape=[jax.ShapeDtypeStruct((t, w), F32)] * 3 + [jax.ShapeDtypeStruct(cw.shape, F32)],
        compiler_params=_params("arbitrary"),
    )(proj, proj, proj, proj, proj, proj, dmix, dmix, cw)


SQRT_HALF = 0.7071067811865476
INV_SQRT_2PI = 0.3989422804014327


def _gelu(x):
    return 0.5 * x * (1.0 + lax.erf(x * SQRT_HALF))


def _gelu_grad(x):
    return 0.5 * (1.0 + lax.erf(x * SQRT_HALF)) + x * (INV_SQRT_2PI * jnp.exp(-0.5 * x * x))


def _sgu_fwd(uv, ln_g, ln_b, wm, bs_full, name):
    t, d2 = uv.shape
    d = d2 // 2
    ng, pb, _ = wm.shape
    gd = d // ng
    tb = _pick(t, ROW_TILES[1:] or ROW_TILES)
    assert tb % pb == 0

    def body(uv_ref, g_ref, b_ref, w_ref, bs_ref, o_ref):
        u = _gelu(uv_ref[:, :d])
        v = _gelu(uv_ref[:, d:])
        mu = jnp.mean(v, axis=-1, keepdims=True)
        vc = v - mu
        var = jnp.mean(vc * vc, axis=-1, keepdims=True)
        vn = (vc * lax.rsqrt(var + LN_EPS) * g_ref[...] + b_ref[...]).astype(BF16)
        for r in range(tb // pb):
            rows = slice(r * pb, (r + 1) * pb)
            for gi in range(ng):
                cols = slice(gi * gd, (gi + 1) * gd)
                s = jnp.dot(w_ref[gi], vn[rows, cols], preferred_element_type=F32) + bs_ref[:, cols]
                o_ref[rows, cols] = (u[rows, cols] * s).astype(BF16)

    vec = pl.BlockSpec((1, d), lambda i: (0, 0))
    return pl.pallas_call(
        body, name=name, grid=(t // tb,),
        in_specs=[pl.BlockSpec((tb, d2), lambda i: (i, 0)), vec, vec,
                  pl.BlockSpec(wm.shape, lambda i: (0, 0, 0)), pl.BlockSpec((pb, d), lambda i: (0, 0))],
        out_specs=pl.BlockSpec((tb, d), lambda i: (i, 0)),
        out_shape=jax.ShapeDtypeStruct((t, d), BF16), compiler_params=_params("parallel"),
    )(uv, ln_g.reshape(1, d), ln_b.reshape(1, d), wm, bs_full)


def _sgu_bwd(uv, ln_g, ln_b, wm, bs_full, dgated, name):
    t, d2 = uv.shape
    d = d2 // 2
    ng, pb, _ = wm.shape
    gd = d // ng
    tb = _pick(t, ROW_TILES[1:] or ROW_TILES)
    nb = t // tb

    def body(uv_ref, g_ref, b_ref, w_ref, bs_ref, dg_ref, o_ref, dw_ref, dbs_ref, dlg_ref, dlb_ref,
             du_s, dvn_s, dbs_s):
        i = pl.program_id(0)

        @pl.when(i == 0)
        def _():
            dw_ref[...] = jnp.zeros_like(dw_ref)
            dbs_s[...] = jnp.zeros_like(dbs_s)
            dlg_ref[...] = jnp.zeros_like(dlg_ref)
            dlb_ref[...] = jnp.zeros_like(dlb_ref)

        upre = uv_ref[:, :d]
        vpre = uv_ref[:, d:]
        u = _gelu(upre)
        v = _gelu(vpre)
        mu = jnp.mean(v, axis=-1, keepdims=True)
        vc = v - mu
        var = jnp.mean(vc * vc, axis=-1, keepdims=True)
        rstd = lax.rsqrt(var + LN_EPS)
        xhat = vc * rstd
        vn = (xhat * g_ref[...] + b_ref[...]).astype(BF16)
        dgt = dg_ref[...].astype(F32)
        for r in range(tb // pb):
            rows = slice(r * pb, (r + 1) * pb)
            for gi in range(ng):
                cols = slice(gi * gd, (gi + 1) * gd)
                vblk = vn[rows, cols]
                s = jnp.dot(w_ref[gi], vblk, preferred_element_type=F32) + bs_ref[:, cols]
                dblk = dgt[rows, cols]
                du_s[rows, cols] = dblk * s
                ds = dblk * u[rows, cols]
                dsb = ds.astype(BF16)
                dvn_s[rows, cols] = lax.dot_general(w_ref[gi], dsb, (((0,), (0,)), ((), ())),
                                                    preferred_element_type=F32)
                dw_ref[gi] += lax.dot_general(dsb, vblk, (((1,), (1,)), ((), ())), preferred_element_type=F32)
                dbs_s[:, cols] += ds
        dvn = dvn_s[...]
        dlg_ref[...] += jnp.sum(dvn * xhat, axis=0, keepdims=True)
        dlb_ref[...] += jnp.sum(dvn, axis=0, keepdims=True)
        dxh = dvn * g_ref[...]
        m1 = jnp.mean(dxh, axis=-1, keepdims=True)
        m2 = jnp.mean(dxh * xhat, axis=-1, keepdims=True)
        dv = rstd * (dxh - m1 - xhat * m2)
        o_ref[:, :d] = (du_s[...] * _gelu_grad(upre)).astype(BF16)
        o_ref[:, d:] = (dv * _gelu_grad(vpre)).astype(BF16)

        @pl.when(i == nb - 1)
        def _():
            lane = lax.broadcasted_iota(jnp.int32, (pb, LANES), 1)
            acc = jnp.zeros((pb, LANES), F32)
            for gi in range(ng):
                col = jnp.sum(dbs_s[:, gi * gd:(gi + 1) * gd], axis=1, keepdims=True)
                acc = acc + jnp.where(lane == gi, col, 0.0)
            dbs_ref[...] = acc

    vec = pl.BlockSpec((1, d), lambda i: (0, 0))
    duv, dw, dbs, dlg, dlb = pl.pallas_call(
        body, name=name, grid=(nb,),
        in_specs=[pl.BlockSpec((tb, d2), lambda i: (i, 0)), vec, vec,
                  pl.BlockSpec(wm.shape, lambda i: (0, 0, 0)), pl.BlockSpec((pb, d), lambda i: (0, 0)),
                  pl.BlockSpec((tb, d), lambda i: (i, 0))],
        out_specs=[pl.BlockSpec((tb, d2), lambda i: (i, 0)), pl.BlockSpec(wm.shape, lambda i: (0, 0, 0)),
                   pl.BlockSpec((pb, LANES), lambda i: (0, 0)), vec, vec],
        out_shape=[jax.ShapeDtypeStruct((t, d2), BF16), jax.ShapeDtypeStruct(wm.shape, F32),
                   jax.ShapeDtypeStruct((pb, LANES), F32), jax.ShapeDtypeStruct((1, d), F32),
                   jax.ShapeDtypeStruct((1, d), F32)],
        scratch_shapes=[pltpu.VMEM((tb, d), F32), pltpu.VMEM((tb, d), F32), pltpu.VMEM((pb, d), F32)],
        compiler_params=_params("arbitrary"),
    )(uv, ln_g.reshape(1, d), ln_b.reshape(1, d), wm, bs_full, dgated)
    return duv, dw, dbs, dlg[0], dlb[0]


def _adamw(w, g, m, v, name):
    shape = w.shape
    cols = shape[-1]
    rows = w.size // cols
    tr = _pick(rows, (512, 256, 352, 128, 64, 32, 16, 8))

    def body(w_ref, g_ref, m_ref, v_ref, d_ref, mo_ref, vo_ref):
        d_ref[...], mo_ref[...], vo_ref[...] = _adam_update(w_ref[...], g_ref[...], m_ref[...], v_ref[...])

    spec = pl.BlockSpec((tr, cols), lambda i: (i, 0))
    outs = pl.pallas_call(
        body, name=name, grid=(rows // tr,),
        in_specs=[spec] * 4, out_specs=[spec] * 3,
        out_shape=[jax.ShapeDtypeStruct((rows, cols), F32)] * 3,
        compiler_params=_params("parallel"),
    )(*[a.reshape(rows, cols) for a in (w, g, m, v)])
    return [o.reshape(shape) for o in outs]


ANY = pl.BlockSpec(memory_space=pl.ANY)


def _place():
    return lax.axis_index("x"), lax.axis_index("y"), lax.axis_index("c")


def _all_gather(shards, name):
    n = len(shards)

    def body(*refs):
        start, forward, finish = _gather_phases(refs[:n], refs[n:2 * n], *refs[2 * n:])
        start()
        forward()
        finish()

    return pl.pallas_call(
        body, name=name, in_specs=[ANY] * n, out_specs=[ANY] * n,
        out_shape=_gather_shapes(shards), scratch_shapes=_gather_sems(n),
    )(*shards)


def _cast_hosting_gather(xv, shards, name):
    t, d = xv.shape
    tb = _pick(t, ROW_TILES)
    n = len(shards)
    steps = t // tb

    def body(x_ref, *rest):
        o_ref = rest[n]
        start, forward, finish = _gather_phases(rest[:n], rest[n + 1:2 * n + 1], *rest[2 * n + 1:])
        i = pl.program_id(0)
        pl.when(i == 0)(start)
        pl.when(i == steps // 2)(forward)
        o_ref[...] = x_ref[...].astype(BF16)
        pl.when(i == steps - 1)(finish)

    row = pl.BlockSpec((tb, d), lambda i: (i, 0))
    return pl.pallas_call(
        body, name=name, grid=(steps,), in_specs=[row] + [ANY] * n, out_specs=[row] + [ANY] * n,
        out_shape=[jax.ShapeDtypeStruct((t, d), BF16)] + _gather_shapes(shards),
        scratch_shapes=_gather_sems(n), compiler_params=_params("arbitrary"),
    )(xv, *shards)


def _gather_shapes(shards):
    return [jax.ShapeDtypeStruct((N_DEV,) + s.shape, s.dtype) for s in shards]


def _gather_sems(n):
    return [pltpu.SemaphoreType.DMA((7 * n,)), pltpu.SemaphoreType.DMA((7 * n,)), pltpu.SemaphoreType.DMA((n,))]


def _gather_phases(x_refs, out_refs, send_sems, recv_sems, local_sems):
    n = len(x_refs)
    x, y, c = _place()
    me, sibling = (x, y, c), (x, y, 1 - c)
    chips = [(1 - x, y), (x, 1 - y), (1 - x, 1 - y)]

    def copy(a, k, block, to, own=False):
        px, py, pc = block
        rows = out_refs[a].at[4 * px + 2 * py + pc]
        return pltpu.make_async_remote_copy(
            src_ref=x_refs[a] if own else rows, dst_ref=rows,
            send_sem=send_sems.at[7 * a + k], recv_sem=recv_sems.at[7 * a + k],
            device_id=to, device_id_type=MESH)

    def local(a):
        return pltpu.make_async_copy(x_refs[a], out_refs[a].at[4 * x + 2 * y + c], local_sems.at[a])

    def first(a):
        return [copy(a, 0, me, sibling, own=True)] + [copy(a, 1 + j, me, (*chip, c), own=True)
                                                      for j, chip in enumerate(chips)]

    def start():
        for a in range(n):
            local(a).start()
            for cp in first(a):
                cp.start()

    def forward():
        for j, chip in enumerate(chips):
            for a in range(n):
                copy(a, 1 + j, (*chip, c), me).wait_recv()
                copy(a, 4 + j, (*chip, c), sibling).start()

    def finish():
        for a in range(n):
            copy(a, 0, sibling, me).wait_recv()
            for j, chip in enumerate(chips):
                copy(a, 4 + j, (*chip, 1 - c), me).wait_recv()
        for a in range(n):
            for cp in first(a) + [copy(a, 4 + j, (*chip, c), sibling) for j, chip in enumerate(chips)]:
                cp.wait_send()
            local(a).wait()

    return start, forward, finish


def _rs_sibling_exchange(packed, name):
    rider = _sibling_exchange_rider(packed)
    n = len(packed)

    def body(*refs):
        start, finish = rider.phases(refs[:n], refs[n:2 * n], refs[2 * n:])
        start()
        finish()

    return pl.pallas_call(
        body, name=name, in_specs=rider.in_specs, out_specs=rider.out_specs, out_shape=rider.out_shapes,
        scratch_shapes=rider.scratch,
    )(*packed)


def _sibling_exchange_rider(packed):
    n = len(packed)

    def phases(p_refs, r_refs, scratch):
        send_sems, recv_sems = scratch
        x, y, c = _place()

        def copies():
            return [pltpu.make_async_remote_copy(
                src_ref=p_refs[a].at[2 * j + (1 - c)], dst_ref=r_refs[a].at[j],
                send_sem=send_sems.at[4 * a + j], recv_sem=recv_sems.at[4 * a + j],
                device_id=(x, y, 1 - c), device_id_type=MESH) for a in range(n) for j in range(4)]

        def start():
            for cp in copies():
                cp.start()

        def finish():
            for cp in copies():
                cp.wait()

        return start, finish

    return _Rider(list(packed), [ANY] * n, [jax.ShapeDtypeStruct((4,) + p.shape[1:], p.dtype) for p in packed],
                  [ANY] * n, [pltpu.SemaphoreType.DMA((4 * n,)), pltpu.SemaphoreType.DMA((4 * n,))], phases)


def _rs_chip_sum(packed, from_sibling, c_idx, name):
    _, r, cc = packed.shape
    tr = _pick(r, (512, 256, 352, 128))

    def body(c_ref, a_ref, b_ref, o_ref):
        o_ref[...] = (a_ref[...].astype(F32) + b_ref[...].astype(F32)).astype(o_ref.dtype)

    return pl.pallas_call(
        body, name=name,
        grid_spec=pltpu.PrefetchScalarGridSpec(
            num_scalar_prefetch=1, grid=(4, r // tr),
            in_specs=[pl.BlockSpec((None, tr, cc), lambda j, i, c_ref: (2 * j + c_ref[0], i, 0)),
                      pl.BlockSpec((None, tr, cc), lambda j, i, c_ref: (j, i, 0))],
            out_specs=pl.BlockSpec((None, tr, cc), lambda j, i, c_ref: (j, i, 0))),
        out_shape=jax.ShapeDtypeStruct((4, r, cc), packed.dtype),
        compiler_params=_params("parallel", "parallel"),
    )(c_idx, packed, from_sibling)


def _chip_exchange_shapes(partial):
    return [jax.ShapeDtypeStruct((3,) + p.shape[1:], p.dtype) for p in partial]


def _chip_exchange_sems(n):
    return [pltpu.SemaphoreType.DMA((3 * n,)), pltpu.SemaphoreType.DMA((3 * n,))]


def _chip_exchange_phases(p_refs, r_refs, send_sems, recv_sems):
    x, y, c = _place()
    chips = [(1 - x, y), (x, 1 - y), (1 - x, 1 - y)]

    def copies():
        return [pltpu.make_async_remote_copy(
            src_ref=p_refs[a].at[2 * tx + ty], dst_ref=r_refs[a].at[k],
            send_sem=send_sems.at[3 * a + k], recv_sem=recv_sems.at[3 * a + k],
            device_id=(tx, ty, c), device_id_type=MESH)
            for a in range(len(p_refs)) for k, (tx, ty) in enumerate(chips)]

    def start():
        for cp in copies():
            cp.start()

    def finish():
        for cp in copies():
            cp.wait()

    return start, finish


def _adam_update(w, g, m, v):
    mn = ADAM_B1 * m + (1.0 - ADAM_B1) * g
    vn = ADAM_B2 * v + (1.0 - ADAM_B2) * (g * g)
    m_hat = mn / (1.0 - ADAM_B1 ** ADAM_STEP)
    v_hat = vn / (1.0 - ADAM_B2 ** ADAM_STEP)
    return -ADAM_LR * (m_hat / (jnp.sqrt(v_hat) + ADAM_EPS) + ADAM_WD * w), mn, vn


def _rs_final_adamw(partial, received, chip_idx, w, m, v, name):
    _, r, cc = partial.shape
    tr = _pick(r, (512, 256, 352, 128))

    def body(c_ref, a_ref, r_ref, w_ref, m_ref, v_ref, g_ref, d_ref, mo_ref, vo_ref):
        g = a_ref[...].astype(F32)
        for k in range(3):
            g = g + r_ref[k].astype(F32)
        g_ref[...] = g
        d_ref[...], mo_ref[...], vo_ref[...] = _adam_update(w_ref[...], g, m_ref[...], v_ref[...])

    row = pl.BlockSpec((tr, cc), lambda i, c_ref: (i, 0))
    return pl.pallas_call(
        body, name=name,
        grid_spec=pltpu.PrefetchScalarGridSpec(
            num_scalar_prefetch=1, grid=(r // tr,),
            in_specs=[pl.BlockSpec((None, tr, cc), lambda i, c_ref: (c_ref[0], i, 0)),
                      pl.BlockSpec((3, tr, cc), lambda i, c_ref: (0, i, 0)), row, row, row],
            out_specs=[row] * 4),
        out_shape=[jax.ShapeDtypeStruct((r, cc), F32)] * 4,
        compiler_params=_params("parallel"),
    )(chip_idx, partial, received, w.reshape(r, cc), m.reshape(r, cc), v.reshape(r, cc))


def _all_reduce_small(vals, name):
    rider = _all_reduce_rider(vals)

    def body(v_ref, o_ref, *scratch):
        start, finish = rider.phases([v_ref], [o_ref], scratch)
        start()
        finish()

    return pl.pallas_call(
        body, name=name, in_specs=rider.in_specs, out_specs=rider.out_specs[0], out_shape=rider.out_shapes[0],
        scratch_shapes=rider.scratch, compiler_params=pltpu.CompilerParams(vmem_limit_bytes=VMEM_LIMIT),
    )(vals)


def _all_reduce_rider(vals):
    r, cc = vals.shape

    def phases(ins, outs, scratch):
        (v_ref,), (o_ref,), (buf, send_sems, recv_sems) = ins, outs, scratch
        x, y, c = _place()
        me = 4 * x + 2 * y + c

        def copies():
            cps = []
            for k in range(1, N_DEV):
                kx, ky, kc = (k >> 2) & 1, (k >> 1) & 1, k & 1
                peer = (1 - x if kx else x, 1 - y if ky else y, 1 - c if kc else c)
                cps.append(pltpu.make_async_remote_copy(
                    src_ref=buf.at[0], dst_ref=buf.at[k], send_sem=send_sems.at[k - 1],
                    recv_sem=recv_sems.at[k - 1], device_id=peer, device_id_type=MESH))
            return cps

        def start():
            buf[0] = v_ref[...]
            for cp in copies():
                cp.start()

        def finish():
            for cp in copies():
                cp.wait()
            acc = buf[jnp.bitwise_xor(me, 0)]
            for dev in range(1, N_DEV):
                acc = acc + buf[jnp.bitwise_xor(me, dev)]
            o_ref[...] = acc

        return start, finish

    vm = pl.BlockSpec(memory_space=pltpu.VMEM)
    return _Rider([vals], [vm], [jax.ShapeDtypeStruct((r, cc), F32)], [vm],
                  [pltpu.VMEM((N_DEV, r, cc), F32), pltpu.SemaphoreType.DMA((7,)), pltpu.SemaphoreType.DMA((7,))],
                  phases)


def _chip_exchange_rider(partial):
    n = len(partial)
    return _Rider(list(partial), [ANY] * n, _chip_exchange_shapes(partial), [ANY] * n, _chip_exchange_sems(n),
                  lambda ins, outs, scratch: _chip_exchange_phases(ins, outs, *scratch))


def _lanes(flat):
    pad = (-flat.shape[0]) % (SUBLANES * LANES)
    return jnp.pad(flat, (0, pad)).reshape(-1, LANES)


def kernel(x, even_w_in, even_b_f, even_conv_w, even_w_out, odd_w_in, odd_v_ln_g, odd_v_ln_b, odd_w_s, odd_b_s, odd_w_out, mix_ln_g, mix_ln_b, ffn_w_in, ffn_w_out, ffn_ln_g, ffn_ln_b, loss_target, m_even_w_in, m_even_b_f, m_even_conv_w, m_even_w_out, m_odd_w_in, m_odd_v_ln_g, m_odd_v_ln_b, m_odd_w_s, m_odd_b_s, m_odd_w_out, m_mix_ln_g, m_mix_ln_b, m_ffn_w_in, m_ffn_w_out, m_ffn_ln_g, m_ffn_ln_b, v_even_w_in, v_even_b_f, v_even_conv_w, v_even_w_out, v_odd_w_in, v_odd_v_ln_g, v_odd_v_ln_b, v_odd_w_s, v_odd_b_s, v_odd_w_out, v_mix_ln_g, v_mix_ln_b, v_ffn_w_in, v_ffn_w_out, v_ffn_ln_g, v_ffn_ln_b):
    t, d = x.shape[1], x.shape[2]
    nh = even_b_f.shape[-1]
    w = even_conv_w.shape[-1] * N_DEV
    dh = w // nh
    scale = dh ** -0.5
    e_in = even_w_in.shape[-1] * N_DEV
    f2 = ffn_w_in.shape[-1] * N_DEV
    f = f2 // 2
    ng, pb = odd_w_s.shape[1], odd_w_s.shape[2]
    assert e_in == 6 * w + nh and nh <= SUBLANES and (6 * w) % LANES == 0 and d % N_DEV == 0
    mx, my, mc = _place()
    me = 4 * mx + 2 * my + mc

    big = [even_w_in[0], even_w_out[0], odd_w_in[0], odd_w_out[0],
           ffn_w_in[0], ffn_w_in[1], ffn_w_out[0], ffn_w_out[1]]
    x0b, g_in0 = _cast_hosting_gather(x[0], [big[0].astype(BF16)], "x_cast_ag_even_w_in")
    w_in0 = g_in0.transpose(1, 0, 2).reshape(d, e_in)
    w_all0 = jnp.concatenate([w_in0[:, :3 * w], w_in0[:, 3 * w + nh:], w_in0[:, 3 * w:3 * w + nh],
                              jnp.zeros((d, LANES - nh), BF16)], axis=1)

    cs, vs = even_conv_w.shape[-1], odd_v_ln_g.shape[-1]
    small_mine = jnp.concatenate([
        lax.dynamic_update_slice(jnp.zeros((3, w), F32), even_conv_w[0], (0, me * cs)).reshape(-1),
        lax.dynamic_update_slice(jnp.zeros((d,), F32), odd_v_ln_g[0], (me * vs,)),
        lax.dynamic_update_slice(jnp.zeros((d,), F32), odd_v_ln_b[0], (me * vs,))])
    small_all = _all_reduce_small(_lanes(small_mine), "ag_small").reshape(-1)
    conv_w = small_all[:3 * w].reshape(3, w)
    vln_g = small_all[3 * w:3 * w + d]
    vln_b = small_all[3 * w + d:3 * w + 2 * d]

    bf_pad = jnp.pad(even_b_f[0], (0, LANES - nh)).reshape(1, LANES)
    chunk = jnp.arange(pb) // (pb // 2)
    ws_mask = (chunk[None, :] <= chunk[:, None])[None]
    wm = jnp.where(ws_mask, odd_w_s[0], 0.0).astype(BF16)
    bs_full = jnp.repeat(odd_b_s[0].T, d // ng, axis=1)

    x0 = x[0]
    tgt = loss_target[0]
    fcol = 6 * w // LANES
    p0 = _mm(x0b, w_all0, "nn", F32, "l0_in_proj")
    cgate = _fgate_fwd(p0, bf_pad, fcol, nh, "l0_fgate")
    assert dh + 7 <= LANES
    qa, ka, va = _attn_pack(p0, cgate, w, nh, scale, "l0_attn_pack")
    oa, g_out0, g_in1, g_out1, g_fi0, g_fi1, g_fo0, g_fo1 = _attn_fwd(
        qa, ka, va, dh, "l0_attn", gather=[s.astype(BF16) for s in big[1:]])
    w_out0, w_out1 = g_out0.reshape(2 * w, d), g_out1.reshape(d, d)
    w_fo0, w_fo1 = g_fo0.reshape(f, d), g_fo1.reshape(f, d)
    nb = N_DEV // 2
    w_fi0, w_fi1 = g_fi0.reshape(2, nb, d, -1), g_fi1.reshape(2, nb, d, -1)
    attn, = _attn_unpack(oa, w, nh, 1.0, None, 1.0, "l0_attn_unpack")
    yconv = _conv_fwd(p0, conv_w, w, 3, "l0_conv")
    mix = jnp.concatenate([attn, yconv], axis=1)
    m0, x1, x1b = _mm_ln(mix, w_out0, x0, mix_ln_g[0], mix_ln_b[0], "l0_out_proj_ln")
    h0, gu0 = _ffn_in_swiglu(x1b, w_fi0, "l0_ffn_in")
    f0, x2, x2b = _mm_ln(h0, w_fo0, x1, ffn_ln_g[0], ffn_ln_b[0], "l0_ffn_out_ln")

    uv = _mm_cols_fwd(x2b, g_in1, False, F32, "l1_in_proj")
    gated = _sgu_fwd(uv, vln_g, vln_b, wm, bs_full, "l1_sgu")
    m1, x3, x3b = _mm_ln(gated, w_out1, x2, mix_ln_g[1], mix_ln_b[1], "l1_out_proj_ln")
    h1, gu1 = _ffn_in_swiglu(x3b, w_fi1, "l1_ffn_in")
    dz4, dz4b, g_ffn_g1, g_ffn_b1, loss_part = _mm_ln(h1, w_fo1, x3, ffn_ln_g[1], ffn_ln_b[1],
                                                      "l1_ffn_out_ln_loss", target=tgt)
    gd_fo1 = _mm_blk_dw(h1, dz4b, BF16, "l1_ffn_out_dw").reshape(N_DEV, -1, d)
    dgu1 = _ffn_out_dx_swiglu(dz4b, w_fo1, gu1, "l1_ffn_out_dx").reshape(N_DEV, t, -1)
    gd_fi1 = _mm_cols_dw(x3b, dgu1, N_DEV, True, BF16, "l1_ffn_in_dw")
    dz3, dz3b, g_mix_g1, g_mix_b1 = _mm_cols_dx_ln_bwd(dgu1, g_fi1, True, x2, m1, mix_ln_g[1], dz4,
                                                       "l1_ffn_in_dx_ln_bwd")
    gd_out1 = _mm(gated, dz3b, "tn", BF16, "l1_out_proj_dw").reshape(N_DEV, -1, d)
    dgated = _mm(dz3b, w_out1, "nt", BF16, "l1_out_proj_dx")
    duv, g_wm, g_bs_t, g_vln_g, g_vln_b = _sgu_bwd(uv, vln_g, vln_b, wm, bs_full, dgated, "l1_sgu_bwd")
    gd_in1 = _mm_cols_dw(x2b, duv, N_DEV, False, BF16, "l1_in_proj_dw")

    dz2, dz2b, g_ffn_g0, g_ffn_b0 = _mm_cols_dx_ln_bwd(duv, g_in1, False, x1, f0, ffn_ln_g[0], dz3,
                                                       "l1_in_proj_dx_ln_bwd")
    gd_fo0 = _mm_blk_dw(h0, dz2b, BF16, "l0_ffn_out_dw").reshape(N_DEV, -1, d)
    dgu0 = _ffn_out_dx_swiglu(dz2b, w_fo0, gu0, "l0_ffn_out_dx").reshape(N_DEV, t, -1)
    gd_fi0 = _mm_cols_dw(x1b, dgu0, N_DEV, True, BF16, "l0_ffn_in_dw")
    dz1, dz1b, g_mix_g0, g_mix_b0 = _mm_cols_dx_ln_bwd(dgu0, g_fi0, True, x0, m0, mix_ln_g[0], dz2,
                                                       "l0_ffn_in_dx_ln_bwd")
    gd_out0 = _mm(mix, dz1b, "tn", BF16, "l0_out_proj_dw").reshape(N_DEV, -1, d)
    early_g = [gd_out0, gd_in1, gd_out1, gd_fi0, gd_fi1, gd_fo0, gd_fo1]
    dmix, *early_sib = _mm(dz1b, w_out0, "nt", F32, "l0_out_proj_dx", rider=_sibling_exchange_rider(early_g))
    d_b, d_c, d_h, g_conv = _conv_bwd(p0, conv_w, dmix, w, 3, "l0_conv_bwd")
    doa, qa2 = _attn_pack_bwd(dmix, oa, qa, w, nh, "l0_attn_pack_bwd")
    big_names = ["even_w_in", "even_w_out", "odd_w_in", "odd_w_out", "ffn_w_in0", "ffn_w_in1", "ffn_w_out0", "ffn_w_out1"]
    c_idx = mc.reshape(1).astype(jnp.int32)
    chip_idx = (2 * mx + my).reshape(1).astype(jnp.int32)
    early_partial = [_rs_chip_sum(g, s, c_idx, "rs_chip_sum_" + n)
                     for g, s, n in zip(early_g, early_sib, big_names[1:])]
    dqa, dka, dva, *early_received = _attn_bwd(qa2, ka, va, doa, "l0_attn_bwd", exchange=early_partial)
    dq, dcq = _attn_unpack(dqa, w, nh, scale, dh + 3, 1.0, "l0_attn_unpack_dq")
    dk, dck = _attn_unpack(dka, w, nh, 1.0, dh, -1.0, "l0_attn_unpack_dk")
    dv, = _attn_unpack(dva, w, nh, 1.0, None, 1.0, "l0_attn_unpack_dv")
    dzf, g_bf = _fgate_bwd(p0, bf_pad, dcq, dck, fcol, nh, "l0_fgate_bwd")
    dp0 = jnp.concatenate([dq, dk, dv, d_b.astype(BF16), d_c.astype(BF16), d_h.astype(BF16), dzf.astype(BF16)], axis=1)
    g_ws = jnp.where(ws_mask, g_wm, 0.0)
    g_bs = g_bs_t[:, :ng].T
    small_g = [g_bf[:nh], g_conv, g_vln_g, g_vln_b, g_ws, g_bs,
               jnp.stack([g_mix_g0, g_mix_g1]), jnp.stack([g_mix_b0, g_mix_b1]),
               jnp.stack([g_ffn_g0, g_ffn_g1]), jnp.stack([g_ffn_b0, g_ffn_b1])]
    small_rider = _all_reduce_rider(_lanes(jnp.concatenate([a.reshape(-1) for a in small_g])))
    g_all0, small_sum = _mm(x0b, dp0, "tn", F32, "l0_in_proj_dw", rider=small_rider)
    gd_in0 = jnp.concatenate([g_all0[:, :3 * w], g_all0[:, 6 * w:6 * w + nh], g_all0[:, 3 * w:6 * w]], axis=1)
    gd_in0 = gd_in0.reshape(d, N_DEV, -1).transpose(1, 0, 2).astype(BF16)

    big_m = [m_even_w_in[0], m_even_w_out[0], m_odd_w_in[0], m_odd_w_out[0],
             m_ffn_w_in[0], m_ffn_w_in[1], m_ffn_w_out[0], m_ffn_w_out[1]]
    big_v = [v_even_w_in[0], v_even_w_out[0], v_odd_w_in[0], v_odd_w_out[0],
             v_ffn_w_in[0], v_ffn_w_in[1], v_ffn_w_out[0], v_ffn_w_out[1]]
    late_sib = _rs_sibling_exchange([gd_in0], "rs_sibling_late")
    late_partial = [_rs_chip_sum(gd_in0, late_sib[0], c_idx, "rs_chip_sum_" + big_names[0])]
    partial = late_partial + early_partial
    grad_x, *late_received = _mm(dp0, w_all0, "nt", F32, "l0_in_proj_dx", rider=_chip_exchange_rider(late_partial),
                                 addend=(dz1, ALPHA))
    received = list(late_received) + list(early_received)
    upd = [_rs_final_adamw(p, r, chip_idx, wt, mt, vt, "rs_final_adamw_" + n)
           for p, r, wt, mt, vt, n in zip(partial, received, big, big_m, big_v, big_names)]
    big_out = {}
    for i, n in enumerate(["even_w_in", "even_w_out", "odd_w_in", "odd_w_out"]):
        big_out[n] = [o[None] for o in upd[i]]
    big_out["ffn_w_in"] = [jnp.stack([a, b]) for a, b in zip(upd[4], upd[5])]
    big_out["ffn_w_out"] = [jnp.stack([a, b]) for a, b in zip(upd[6], upd[7])]

    small_sum = small_sum.reshape(-1)
    outs_small = []
    off = 0
    for a in small_g:
        outs_small.append(small_sum[off:off + a.size].reshape(a.shape))
        off += a.size
    gr_bf, gr_conv, gr_vg, gr_vb, gr_ws, gr_bs, gr_mg, gr_mb, gr_fg, gr_fb = outs_small

    loss = lax.psum(loss_part, ("x", "y", "c"))

    grads = {
        "even_b_f": gr_bf[None],
        "even_conv_w": lax.dynamic_slice(gr_conv, (0, me * cs), (3, cs))[None],
        "odd_v_ln_g": lax.dynamic_slice(gr_vg, (me * vs,), (vs,))[None],
        "odd_v_ln_b": lax.dynamic_slice(gr_vb, (me * vs,), (vs,))[None],
        "odd_w_s": gr_ws[None], "odd_b_s": gr_bs[None],
        "mix_ln_g": gr_mg, "mix_ln_b": gr_mb, "ffn_ln_g": gr_fg, "ffn_ln_b": gr_fb,
    }
    weights = dict(even_w_in=even_w_in, even_b_f=even_b_f, even_conv_w=even_conv_w, even_w_out=even_w_out,
                   odd_w_in=odd_w_in, odd_v_ln_g=odd_v_ln_g, odd_v_ln_b=odd_v_ln_b, odd_w_s=odd_w_s,
                   odd_b_s=odd_b_s, odd_w_out=odd_w_out, mix_ln_g=mix_ln_g, mix_ln_b=mix_ln_b,
                   ffn_w_in=ffn_w_in, ffn_w_out=ffn_w_out, ffn_ln_g=ffn_ln_g, ffn_ln_b=ffn_ln_b)
    moms = dict(even_w_in=(m_even_w_in, v_even_w_in), even_b_f=(m_even_b_f, v_even_b_f),
                even_conv_w=(m_even_conv_w, v_even_conv_w), even_w_out=(m_even_w_out, v_even_w_out),
                odd_w_in=(m_odd_w_in, v_odd_w_in), odd_v_ln_g=(m_odd_v_ln_g, v_odd_v_ln_g),
                odd_v_ln_b=(m_odd_v_ln_b, v_odd_v_ln_b), odd_w_s=(m_odd_w_s, v_odd_w_s),
                odd_b_s=(m_odd_b_s, v_odd_b_s), odd_w_out=(m_odd_w_out, v_odd_w_out),
                mix_ln_g=(m_mix_ln_g, v_mix_ln_g), mix_ln_b=(m_mix_ln_b, v_mix_ln_b),
                ffn_w_in=(m_ffn_w_in, v_ffn_w_in), ffn_w_out=(m_ffn_w_out, v_ffn_w_out),
                ffn_ln_g=(m_ffn_ln_g, v_ffn_ln_g), ffn_ln_b=(m_ffn_ln_b, v_ffn_ln_b))
    names = list(weights)
    gout, deltas, new_m, new_v = [], [], [], []
    for n in names:
        if n in big_out:
            gr, dlt, mn, vn = big_out[n]
        else:
            gr = grads[n]
            dlt, mn, vn = _adamw(weights[n], gr, moms[n][0], moms[n][1], "adamw_" + n)
        gout.append(gr.reshape(weights[n].shape))
        deltas.append(dlt.reshape(weights[n].shape))
        new_m.append(mn.reshape(weights[n].shape))
        new_v.append(vn.reshape(weights[n].shape))
    return (loss, grad_x[None], *gout, *deltas, *new_m, *new_v)
```
